```python
import jax, jax.numpy as jnp
from jax import lax
import numpy as np

D_MODEL = 1024
BATCH = 8
SEQ = 8192
DEPTH = 4

GRID_W = 64
CTX_LEN = 256
HEAD_DIM = 64
N_Q_HEADS = 8
N_KV_HEADS = 2
GROUP = N_Q_HEADS // N_KV_HEADS
ATTN_WIDTH = N_Q_HEADS * HEAD_DIM
KV_WIDTH = N_KV_HEADS * HEAD_DIM
AXIS_DIM = HEAD_DIM // 2
ROPE_THETA = 10000.0
Q_BLOCK = 128
CONV_WIDTH = D_MODEL // 4
CONV_GROUPS = 4
CONV_K = 3
CHUNK = 128
SG_GROUPS = 4
SG_WIDTH = D_MODEL // 4
N_BRANCH = 3
D_FF = -(-8 * D_MODEL // (3 * 256)) * 256
N_MOD = 6
EPS = 1e-6

OFF_Q = 3 * CONV_WIDTH
OFF_K = OFF_Q + ATTN_WIDTH
OFF_V = OFF_K + KV_WIDTH
OFF_U = OFF_V + KV_WIDTH
OFF_SV = OFF_U + SG_WIDTH
OFF_G = OFF_SV + SG_WIDTH
IN_WIDTH = OFF_G + N_BRANCH * D_MODEL

kernel_name = 'hybrid_conv_gqa_gmlp_dit_block'


def rms_norm(x, g):
    xf = x.astype(jnp.float32)
    y = xf * lax.rsqrt(jnp.mean(xf * xf, axis=-1, keepdims=True) + EPS)
    return (y * g.astype(jnp.float32)).astype(x.dtype)


def modulate(h, shift, scale):
    return h * (1 + scale) + shift


def adaln(cond, w_mod, b_mod):
    return jnp.split(jax.nn.silu(cond) @ w_mod + b_mod, N_MOD, axis=-1)


def axial_rope_tables(n):
    rows = n // GRID_W
    row = jnp.repeat(jnp.arange(rows, dtype=jnp.float32), GRID_W)
    col = jnp.tile(jnp.arange(GRID_W, dtype=jnp.float32), rows)
    inv_freq = ROPE_THETA ** (-jnp.arange(0, AXIS_DIM, 2, dtype=jnp.float32) / AXIS_DIM)
    ang = jnp.stack([row[:, None] * inv_freq, col[:, None] * inv_freq], axis=1)
    return jnp.cos(ang), jnp.sin(ang)


def apply_rope(x, cos, sin):
    xr = x.astype(jnp.float32).reshape(*x.shape[:-1], 2, 2, AXIS_DIM // 2)
    x1, x2 = xr[..., 0, :], xr[..., 1, :]
    cs, sn = cos[:, None], sin[:, None]
    out = jnp.stack([x1 * cs - x2 * sn, x2 * cs + x1 * sn], axis=-2)
    return out.reshape(x.shape).astype(x.dtype)


def short_conv(z, w):
    zp = jnp.pad(z, ((0, 0), (1, 1), (0, 0)))
    return zp[:, :-2] * w[0] + zp[:, 1:-1] * w[1] + zp[:, 2:] * w[2]


def spatial_gate(u, v, sg_norm, w_s, b_s):
    v = rms_norm(v, sg_norm)
    B, S, _ = v.shape
    vc = v.reshape(B, S // CHUNK, CHUNK, SG_GROUPS, SG_WIDTH // SG_GROUPS)
    mixed = jnp.einsum('gts,bcsgd->bctgd', w_s, vc) + b_s.T[None, None, :, :, None]
    return u * mixed.reshape(B, S, SG_WIDTH)


def gqa_attend(q, k, v):
    s = jnp.einsum('bqhgd,bkhd->bhgqk', q, k).astype(jnp.float32) * (HEAD_DIM ** -0.5)
    p = jax.nn.softmax(s, axis=-1).astype(v.dtype)
    return jnp.einsum('bhgqk,bkhd->bqhgd', p, v)


def attend_blocks(q, k, v):
    B, S = q.shape[:2]
    nb = S // Q_BLOCK
    qb = jnp.moveaxis(q.reshape(B, nb, Q_BLOCK, N_KV_HEADS, GROUP, HEAD_DIM), 1, 0)
    out = lax.map(lambda qi: gqa_attend(qi, k, v), qb)
    return jnp.moveaxis(out, 0, 1).reshape(B, S, ATTN_WIDTH)


def project(h, w_in, q_gain, k_gain):
    p = h @ w_in
    a_b, a_c, a_x, q, k, v, u, sv, g = jnp.split(
        p, (CONV_WIDTH, 2 * CONV_WIDTH, OFF_Q, OFF_K, OFF_V, OFF_U, OFF_SV, OFF_G), axis=-1)
    q = rms_norm(q.reshape(*q.shape[:-1], N_Q_HEADS, HEAD_DIM), q_gain)
    k = rms_norm(k.reshape(*k.shape[:-1], N_KV_HEADS, HEAD_DIM), k_gain)
    v = v.reshape(*v.shape[:-1], N_KV_HEADS, HEAD_DIM)
    return a_b, a_c, a_x, q, k, v, u, sv, g


def project_kv(h, w_in, k_gain):
    k, v = jnp.split(h @ w_in[:, OFF_K:OFF_U], 2, axis=-1)
    k = rms_norm(k.reshape(*k.shape[:-1], N_KV_HEADS, HEAD_DIM), k_gain)
    return k, v.reshape(*v.shape[:-1], N_KV_HEADS, HEAD_DIM)


def merge_branches(a_b, a_c, a_x, attn, u, sv, g, conv_w, sg_norm, w_s, b_s, w_a, w_b, w_c, w_o):
    y_a = (a_b * short_conv(a_c * a_x, conv_w)) @ w_a
    y_b = attn @ w_b
    y_c = spatial_gate(jax.nn.gelu(u), jax.nn.gelu(sv), sg_norm, w_s, b_s) @ w_c
    g_a, g_b, g_c = jnp.split(jax.nn.sigmoid(g), N_BRANCH, axis=-1)
    return (g_a * y_a + g_b * y_b + g_c * y_c) @ w_o


def swiglu(h, w1, w3, w2):
    return (jax.nn.silu(h @ w1) * (h @ w3)) @ w2


def _fwd_setup_inputs(seed: int = 0) -> dict:
    key = jax.random.key(seed)
    ks = jax.random.split(key, 24)
    f = jnp.float32
    D = D_MODEL

    def nrm(k, shape, scale):
        return jax.random.normal(k, shape, f) * scale

    return {
        'x': nrm(ks[0], (BATCH, SEQ, D), 1.0),
        'c': nrm(ks[1], (BATCH, D), 1.0),
        'ctx': nrm(ks[2], (BATCH, CTX_LEN, D), 1.0),
        'c_ctx': nrm(ks[3], (D,), 1.0),
        'w_mod': nrm(ks[4], (DEPTH, D, N_MOD * D), 0.5 * D ** -0.5),
        'b_mod': nrm(ks[5], (DEPTH, N_MOD * D), 0.02),
        'norm1': 1.0 + nrm(ks[6], (DEPTH, D), 0.02),
        'w_in': nrm(ks[7], (DEPTH, D, IN_WIDTH), D ** -0.5),
        'q_gain': 1.0 + nrm(ks[8], (DEPTH, HEAD_DIM), 0.02),
        'k_gain': 1.0 + nrm(ks[9], (DEPTH, HEAD_DIM), 0.02),
        'conv_w': nrm(ks[10], (DEPTH, CONV_K, CONV_WIDTH), CONV_K ** -0.5),
        'sg_norm': 1.0 + nrm(ks[11], (DEPTH, SG_WIDTH), 0.02),
        'w_s': nrm(ks[12], (DEPTH, SG_GROUPS, CHUNK, CHUNK), CHUNK ** -0.5),
        'b_s': 1.0 + nrm(ks[13], (DEPTH, SG_GROUPS, CHUNK), 0.02),
        'w_a': nrm(ks[14], (DEPTH, CONV_WIDTH, D), CONV_WIDTH ** -0.5),
        'w_b': nrm(ks[15], (DEPTH, ATTN_WIDTH, D), ATTN_WIDTH ** -0.5),
        'w_c': nrm(ks[16], (DEPTH, SG_WIDTH, D), SG_WIDTH ** -0.5),
        'w_o': nrm(ks[17], (DEPTH, D, D), D ** -0.5),
        'norm2': 1.0 + nrm(ks[18], (DEPTH, D), 0.02),
        'w_ff1': nrm(ks[19], (DEPTH, D, D_FF), D ** -0.5),
        'w_ff3': nrm(ks[20], (DEPTH, D, D_FF), D ** -0.5),
        'w_ff2': nrm(ks[21], (DEPTH, D_FF, D), D_FF ** -0.5),
    }


def _fwd_reference(x, c, ctx, c_ctx, w_mod, b_mod, norm1, w_in, q_gain, k_gain, conv_w, sg_norm,
              w_s, b_s, w_a, w_b, w_c, w_o, norm2, w_ff1, w_ff3, w_ff2):
    n = x.shape[1]
    cos, sin = axial_rope_tables(n)
    B, L = ctx.shape[:2]
    for l in range(DEPTH):
        last = l == DEPTH - 1
        sh1, sc1, gt1, sh2, sc2, gt2 = [m[:, None, :] for m in adaln(c, w_mod[l], b_mod[l])]
        csh1, csc1, cgt1, csh2, csc2, cgt2 = adaln(c_ctx, w_mod[l], b_mod[l])

        h_ctx = modulate(rms_norm(ctx, norm1[l]), csh1, csc1)
        if last:
            k_c, v_c = project_kv(h_ctx, w_in[l], k_gain[l])
        else:
            ca_b, ca_c, ca_x, q_c, k_c, v_c, cu, csv, cg = project(h_ctx, w_in[l], q_gain[l], k_gain[l])
            attn_c = gqa_attend(q_c.reshape(B, L, N_KV_HEADS, GROUP, HEAD_DIM), k_c, v_c)
            attn_c = attn_c.reshape(B, L, ATTN_WIDTH)

        h = modulate(rms_norm(x, norm1[l]), sh1, sc1)
        a_b, a_c, a_x, q, k, v, u, sv, g = project(h, w_in[l], q_gain[l], k_gain[l])
        q = apply_rope(q, cos, sin)
        k = apply_rope(k, cos, sin)
        k_all = jnp.concatenate([k, k_c], axis=1)
        v_all = jnp.concatenate([v, v_c], axis=1)
        attn = attend_blocks(q, k_all, v_all)
        x = x + gt1 * merge_branches(a_b, a_c, a_x, attn, u, sv, g, conv_w[l], sg_norm[l],
                                     w_s[l], b_s[l], w_a[l], w_b[l], w_c[l], w_o[l])
        x = x + gt2 * swiglu(modulate(rms_norm(x, norm2[l]), sh2, sc2), w_ff1[l], w_ff3[l], w_ff2[l])

        if not last:
            ctx = ctx + cgt1 * merge_branches(ca_b, ca_c, ca_x, attn_c, cu, csv, cg, conv_w[l],
                                              sg_norm[l], w_s[l], b_s[l], w_a[l], w_b[l],
                                              w_c[l], w_o[l])
            ctx = ctx + cgt2 * swiglu(modulate(rms_norm(ctx, norm2[l]), csh2, csc2),
                                      w_ff1[l], w_ff3[l], w_ff2[l])
    return x


import jax as _jax
import jax.numpy as _jnp

TWIN_FORMAT = 'train_step'
FWD_PARAMS = ['x', 'c', 'ctx', 'c_ctx', 'w_mod', 'b_mod', 'norm1', 'w_in', 'q_gain', 'k_gain', 'conv_w', 'sg_norm', 'w_s', 'b_s', 'w_a', 'w_b', 'w_c', 'w_o', 'norm2', 'w_ff1', 'w_ff3', 'w_ff2']
TWIN_WEIGHTS = ['c_ctx', 'w_mod', 'b_mod', 'norm1', 'w_in', 'q_gain', 'k_gain', 'conv_w', 'sg_norm', 'w_s', 'b_s', 'w_a', 'w_b', 'w_c', 'w_o', 'norm2', 'w_ff1', 'w_ff3', 'w_ff2']
TWIN_DIFF_INPUT = 'x'
TWIN_INPUTS = ['x', 'c', 'ctx', 'c_ctx', 'w_mod', 'b_mod', 'norm1', 'w_in', 'q_gain', 'k_gain', 'conv_w', 'sg_norm', 'w_s', 'b_s', 'w_a', 'w_b', 'w_c', 'w_o', 'norm2', 'w_ff1', 'w_ff3', 'w_ff2', 'loss_target', 'm_c_ctx', 'm_w_mod', 'm_b_mod', 'm_norm1', 'm_w_in', 'm_q_gain', 'm_k_gain', 'm_conv_w', 'm_sg_norm', 'm_w_s', 'm_b_s', 'm_w_a', 'm_w_b', 'm_w_c', 'm_w_o', 'm_norm2', 'm_w_ff1', 'm_w_ff3', 'm_w_ff2', 'v_c_ctx', 'v_w_mod', 'v_b_mod', 'v_norm1', 'v_w_in', 'v_q_gain', 'v_k_gain', 'v_conv_w', 'v_sg_norm', 'v_w_s', 'v_b_s', 'v_w_a', 'v_w_b', 'v_w_c', 'v_w_o', 'v_norm2', 'v_w_ff1', 'v_w_ff3', 'v_w_ff2']
TWIN_OUTPUTS = ['loss', 'grad_x', 'grad_c_ctx', 'grad_w_mod', 'grad_b_mod', 'grad_norm1', 'grad_w_in', 'grad_q_gain', 'grad_k_gain', 'grad_conv_w', 'grad_sg_norm', 'grad_w_s', 'grad_b_s', 'grad_w_a', 'grad_w_b', 'grad_w_c', 'grad_w_o', 'grad_norm2', 'grad_w_ff1', 'grad_w_ff3', 'grad_w_ff2', 'delta_c_ctx', 'delta_w_mod', 'delta_b_mod', 'delta_norm1', 'delta_w_in', 'delta_q_gain', 'delta_k_gain', 'delta_conv_w', 'delta_sg_norm', 'delta_w_s', 'delta_b_s', 'delta_w_a', 'delta_w_b', 'delta_w_c', 'delta_w_o', 'delta_norm2', 'delta_w_ff1', 'delta_w_ff3', 'delta_w_ff2', 'new_m_c_ctx', 'new_m_w_mod', 'new_m_b_mod', 'new_m_norm1', 'new_m_w_in', 'new_m_q_gain', 'new_m_k_gain', 'new_m_conv_w', 'new_m_sg_norm', 'new_m_w_s', 'new_m_b_s', 'new_m_w_a', 'new_m_w_b', 'new_m_w_c', 'new_m_w_o', 'new_m_norm2', 'new_m_w_ff1', 'new_m_w_ff3', 'new_m_w_ff2', 'new_v_c_ctx', 'new_v_w_mod', 'new_v_b_mod', 'new_v_norm1', 'new_v_w_in', 'new_v_q_gain', 'new_v_k_gain', 'new_v_conv_w', 'new_v_sg_norm', 'new_v_w_s', 'new_v_b_s', 'new_v_w_a', 'new_v_w_b', 'new_v_w_c', 'new_v_w_o', 'new_v_norm2', 'new_v_w_ff1', 'new_v_w_ff3', 'new_v_w_ff2']
TWIN_LEAF_KINDS = {'loss': 'loss', 'grad_x': 'grad_x', 'grad_c_ctx': 'grad_w', 'grad_w_mod': 'grad_w', 'grad_b_mod': 'grad_w', 'grad_norm1': 'grad_w', 'grad_w_in': 'grad_w', 'grad_q_gain': 'grad_w', 'grad_k_gain': 'grad_w', 'grad_conv_w': 'grad_w', 'grad_sg_norm': 'grad_w', 'grad_w_s': 'grad_w', 'grad_b_s': 'grad_w', 'grad_w_a': 'grad_w', 'grad_w_b': 'grad_w', 'grad_w_c': 'grad_w', 'grad_w_o': 'grad_w', 'grad_norm2': 'grad_w', 'grad_w_ff1': 'grad_w', 'grad_w_ff3': 'grad_w', 'grad_w_ff2': 'grad_w', 'delta_c_ctx': 'delta_w', 'delta_w_mod': 'delta_w', 'delta_b_mod': 'delta_w', 'delta_norm1': 'delta_w', 'delta_w_in': 'delta_w', 'delta_q_gain': 'delta_w', 'delta_k_gain': 'delta_w', 'delta_conv_w': 'delta_w', 'delta_sg_norm': 'delta_w', 'delta_w_s': 'delta_w', 'delta_b_s': 'delta_w', 'delta_w_a': 'delta_w', 'delta_w_b': 'delta_w', 'delta_w_c': 'delta_w', 'delta_w_o': 'delta_w', 'delta_norm2': 'delta_w', 'delta_w_ff1': 'delta_w', 'delta_w_ff3': 'delta_w', 'delta_w_ff2': 'delta_w', 'new_m_c_ctx': 'new_m', 'new_m_w_mod': 'new_m', 'new_m_b_mod': 'new_m', 'new_m_norm1': 'new_m', 'new_m_w_in': 'new_m', 'new_m_q_gain': 'new_m', 'new_m_k_gain': 'new_m', 'new_m_conv_w': 'new_m', 'new_m_sg_norm': 'new_m', 'new_m_w_s': 'new_m', 'new_m_b_s': 'new_m', 'new_m_w_a': 'new_m', 'new_m_w_b': 'new_m', 'new_m_w_c': 'new_m', 'new_m_w_o': 'new_m', 'new_m_norm2': 'new_m', 'new_m_w_ff1': 'new_m', 'new_m_w_ff3': 'new_m', 'new_m_w_ff2': 'new_m', 'new_v_c_ctx': 'new_v', 'new_v_w_mod': 'new_v', 'new_v_b_mod': 'new_v', 'new_v_norm1': 'new_v', 'new_v_w_in': 'new_v', 'new_v_q_gain': 'new_v', 'new_v_k_gain': 'new_v', 'new_v_conv_w': 'new_v', 'new_v_sg_norm': 'new_v', 'new_v_w_s': 'new_v', 'new_v_b_s': 'new_v', 'new_v_w_a': 'new_v', 'new_v_w_b': 'new_v', 'new_v_w_c': 'new_v', 'new_v_w_o': 'new_v', 'new_v_norm2': 'new_v', 'new_v_w_ff1': 'new_v', 'new_v_w_ff3': 'new_v', 'new_v_w_ff2': 'new_v'}


def _forward(args):
    return _fwd_reference(*[args[k] for k in FWD_PARAMS])


def _output_shape():
    def fwd():
        inp = _fwd_setup_inputs(0)
        return _fwd_reference(*[inp[k] for k in FWD_PARAMS])
    out = _jax.eval_shape(fwd)
    return out.shape, out.dtype

N_MICROBATCH = 1
ADAM_LR = 0.001
ADAM_B1 = 0.9
ADAM_B2 = 0.999
ADAM_EPS = 1e-08
ADAM_WD = 0.01
ADAM_STEP = 10
PER_EXAMPLE_BATCH_AXIS = {'x': 0, 'c': 0, 'ctx': 0, 'loss_target': 0}
SHARED_INPUTS = []
_WEIGHT_DTYPES = {'c_ctx': _jnp.float32, 'w_mod': _jnp.float32, 'b_mod': _jnp.float32, 'norm1': _jnp.float32, 'w_in': _jnp.float32, 'q_gain': _jnp.float32, 'k_gain': _jnp.float32, 'conv_w': _jnp.float32, 'sg_norm': _jnp.float32, 'w_s': _jnp.float32, 'b_s': _jnp.float32, 'w_a': _jnp.float32, 'w_b': _jnp.float32, 'w_c': _jnp.float32, 'w_o': _jnp.float32, 'norm2': _jnp.float32, 'w_ff1': _jnp.float32, 'w_ff3': _jnp.float32, 'w_ff2': _jnp.float32}
MOMENT_SCALE = {'c_ctx': 1.026097e-01, 'w_mod': 2.199999e+00, 'b_mod': 6.043291e+00, 'norm1': 1.075758e+01, 'w_in': 2.661280e-01, 'q_gain': 4.227194e-02, 'k_gain': 4.269860e-02, 'conv_w': 6.617920e+00, 'sg_norm': 3.820894e+00, 'w_s': 8.293361e-01, 'b_s': 1.915628e+00, 'w_a': 2.301012e-01, 'w_b': 2.237002e-01, 'w_c': 3.341325e-01, 'w_o': 3.678211e-01, 'norm2': 6.272497e+00, 'w_ff1': 1.142980e-01, 'w_ff3': 9.483906e-02, 'w_ff2': 1.512046e-01}


def _to_microbatches(a, axis):
    t = _jnp.moveaxis(a, axis, 0)
    t = t.reshape((N_MICROBATCH, t.shape[0] // N_MICROBATCH) + t.shape[1:])
    return _jnp.moveaxis(t, 1, axis + 1)


def setup_inputs(seed: int = 0) -> dict:
    inp = _fwd_setup_inputs(seed)
    key = _jax.random.fold_in(_jax.random.key(seed), 7919)
    shape, _ = _output_shape()
    out = dict(inp)
    out["loss_target"] = _jax.random.normal(_jax.random.fold_in(key, 0), shape, _jnp.float32)
    for i, name in enumerate(TWIN_WEIGHTS):
        w = inp[name].astype(_jnp.float32)
        if MOMENT_SCALE is None:
            s = _jnp.sqrt(_jnp.mean(_jnp.square(w)) + 1e-30)
        else:
            s = MOMENT_SCALE[name]
        km, kv = _jax.random.split(_jax.random.fold_in(key, i + 1))
        out[name] = w
        out["m_" + name] = s * _jax.random.normal(km, w.shape, _jnp.float32)
        out["v_" + name] = (s * s) * _jax.random.uniform(kv, w.shape, _jnp.float32, 0.5, 1.5)
    if N_MICROBATCH > 1:
        for name, axis in PER_EXAMPLE_BATCH_AXIS.items():
            out[name] = _to_microbatches(out[name], axis)
    return {'x': out['x'], 'c': out['c'], 'ctx': out['ctx'], 'c_ctx': out['c_ctx'], 'w_mod': out['w_mod'], 'b_mod': out['b_mod'], 'norm1': out['norm1'], 'w_in': out['w_in'], 'q_gain': out['q_gain'], 'k_gain': out['k_gain'], 'conv_w': out['conv_w'], 'sg_norm': out['sg_norm'], 'w_s': out['w_s'], 'b_s': out['b_s'], 'w_a': out['w_a'], 'w_b': out['w_b'], 'w_c': out['w_c'], 'w_o': out['w_o'], 'norm2': out['norm2'], 'w_ff1': out['w_ff1'], 'w_ff3': out['w_ff3'], 'w_ff2': out['w_ff2'], 'loss_target': out['loss_target'], 'm_c_ctx': out['m_c_ctx'], 'm_w_mod': out['m_w_mod'], 'm_b_mod': out['m_b_mod'], 'm_norm1': out['m_norm1'], 'm_w_in': out['m_w_in'], 'm_q_gain': out['m_q_gain'], 'm_k_gain': out['m_k_gain'], 'm_conv_w': out['m_conv_w'], 'm_sg_norm': out['m_sg_norm'], 'm_w_s': out['m_w_s'], 'm_b_s': out['m_b_s'], 'm_w_a': out['m_w_a'], 'm_w_b': out['m_w_b'], 'm_w_c': out['m_w_c'], 'm_w_o': out['m_w_o'], 'm_norm2': out['m_norm2'], 'm_w_ff1': out['m_w_ff1'], 'm_w_ff3': out['m_w_ff3'], 'm_w_ff2': out['m_w_ff2'], 'v_c_ctx': out['v_c_ctx'], 'v_w_mod': out['v_w_mod'], 'v_b_mod': out['v_b_mod'], 'v_norm1': out['v_norm1'], 'v_w_in': out['v_w_in'], 'v_q_gain': out['v_q_gain'], 'v_k_gain': out['v_k_gain'], 'v_conv_w': out['v_conv_w'], 'v_sg_norm': out['v_sg_norm'], 'v_w_s': out['v_w_s'], 'v_b_s': out['v_b_s'], 'v_w_a': out['v_w_a'], 'v_w_b': out['v_w_b'], 'v_w_c': out['v_w_c'], 'v_w_o': out['v_w_o'], 'v_norm2': out['v_norm2'], 'v_w_ff1': out['v_w_ff1'], 'v_w_ff3': out['v_w_ff3'], 'v_w_ff2': out['v_w_ff2']}


def _loss(weights, diff, rest, loss_target):
    with _jax.named_scope("forward"):
        args = {**rest, TWIN_DIFF_INPUT: diff, **{k: w.astype(_WEIGHT_DTYPES[k]) for k, w in weights.items()}}
        y = _forward(args)
    with _jax.named_scope("loss_head"):
        err = _jnp.square(y.astype(_jnp.float32) - loss_target)
        return 0.5 * _jnp.sum(_jnp.mean(err, axis=-1)) if err.ndim else 0.5 * err


def _adamw(w, g, m, v):
    m = ADAM_B1 * m + (1.0 - ADAM_B1) * g
    v = ADAM_B2 * v + (1.0 - ADAM_B2) * _jnp.square(g)
    m_hat = m / (1.0 - ADAM_B1 ** ADAM_STEP)
    v_hat = v / (1.0 - ADAM_B2 ** ADAM_STEP)
    delta = -ADAM_LR * (m_hat / (_jnp.sqrt(v_hat) + ADAM_EPS) + ADAM_WD * w)
    return delta, m, v


def reference(x, c, ctx, c_ctx, w_mod, b_mod, norm1, w_in, q_gain, k_gain, conv_w, sg_norm, w_s, b_s, w_a, w_b, w_c, w_o, norm2, w_ff1, w_ff3, w_ff2, loss_target, m_c_ctx, m_w_mod, m_b_mod, m_norm1, m_w_in, m_q_gain, m_k_gain, m_conv_w, m_sg_norm, m_w_s, m_b_s, m_w_a, m_w_b, m_w_c, m_w_o, m_norm2, m_w_ff1, m_w_ff3, m_w_ff2, v_c_ctx, v_w_mod, v_b_mod, v_norm1, v_w_in, v_q_gain, v_k_gain, v_conv_w, v_sg_norm, v_w_s, v_b_s, v_w_a, v_w_b, v_w_c, v_w_o, v_norm2, v_w_ff1, v_w_ff3, v_w_ff2):
    given = dict(x=x, c=c, ctx=ctx, c_ctx=c_ctx, w_mod=w_mod, b_mod=b_mod, norm1=norm1, w_in=w_in, q_gain=q_gain, k_gain=k_gain, conv_w=conv_w, sg_norm=sg_norm, w_s=w_s, b_s=b_s, w_a=w_a, w_b=w_b, w_c=w_c, w_o=w_o, norm2=norm2, w_ff1=w_ff1, w_ff3=w_ff3, w_ff2=w_ff2, loss_target=loss_target, m_c_ctx=m_c_ctx, m_w_mod=m_w_mod, m_b_mod=m_b_mod, m_norm1=m_norm1, m_w_in=m_w_in, m_q_gain=m_q_gain, m_k_gain=m_k_gain, m_conv_w=m_conv_w, m_sg_norm=m_sg_norm, m_w_s=m_w_s, m_b_s=m_b_s, m_w_a=m_w_a, m_w_b=m_w_b, m_w_c=m_w_c, m_w_o=m_w_o, m_norm2=m_norm2, m_w_ff1=m_w_ff1, m_w_ff3=m_w_ff3, m_w_ff2=m_w_ff2, v_c_ctx=v_c_ctx, v_w_mod=v_w_mod, v_b_mod=v_b_mod, v_norm1=v_norm1, v_w_in=v_w_in, v_q_gain=v_q_gain, v_k_gain=v_k_gain, v_conv_w=v_conv_w, v_sg_norm=v_sg_norm, v_w_s=v_w_s, v_b_s=v_b_s, v_w_a=v_w_a, v_w_b=v_w_b, v_w_c=v_w_c, v_w_o=v_w_o, v_norm2=v_norm2, v_w_ff1=v_w_ff1, v_w_ff3=v_w_ff3, v_w_ff2=v_w_ff2)
    weights = {n: given[n] for n in TWIN_WEIGHTS}
    shared = {n: given[n] for n in SHARED_INPUTS}
    per_example = {n: given[n] for n in ['x', 'c', 'ctx']}
    grad_fn = _jax.value_and_grad(_loss, argnums=(0, 1))

    def one_microbatch(ex, loss_target):
        ex = dict(ex)
        diff = ex.pop(TWIN_DIFF_INPUT)
        return grad_fn(weights, diff, {**shared, **ex}, loss_target)

    if N_MICROBATCH == 1:
        loss, (grad_w, grad_x) = one_microbatch(per_example, given["loss_target"])
    else:
        def body(carry, xs):
            loss_sum, grad_sum = carry
            l_k, (gw_k, gx_k) = one_microbatch(xs[0], xs[1])
            with _jax.named_scope("update"):
                return (loss_sum + l_k, _jax.tree.map(_jnp.add, grad_sum, gw_k)), gx_k

        init = (_jnp.zeros((), _jnp.float32), _jax.tree.map(_jnp.zeros_like, weights))
        (loss, grad_w), grad_x = _jax.lax.scan(body, init, (per_example, given["loss_target"]))
    with _jax.named_scope("update"):
        delta_w, new_m, new_v = {}, {}, {}
        for n in TWIN_WEIGHTS:
            delta_w[n], new_m[n], new_v[n] = _adamw(weights[n], grad_w[n], given["m_" + n], given["v_" + n])
    return (loss, grad_x, *[grad_w[n] for n in TWIN_WEIGHTS], *[delta_w[n] for n in TWIN_WEIGHTS],
            *[new_m[n] for n in TWIN_WEIGHTS], *[new_v[n] for n in TWIN_WEIGHTS])
```

```python
import functools

import jax
import jax.numpy as jnp
import numpy as np
from jax import lax
from jax.experimental import pallas as pl
from jax.experimental.pallas import tpu as pltpu

F32 = jnp.float32
BF16 = jnp.bfloat16
EPS = 1e-6
D_MODEL = 1024
HEAD_DIM = 64
N_Q_HEADS = 8
N_KV_HEADS = 2
GROUP = N_Q_HEADS // N_KV_HEADS
GRID_W = 64
ROPE_THETA = 10000.0
CHUNK = 128
CONV_W = 256
SG_W = 256
OFF_Q = 3 * CONV_W
QKV_W = 768
OFF_U = OFF_Q + QKV_W
OFF_G = OFF_U + 2 * SG_W
IN_W = OFF_G + 3 * D_MODEL
N_CHIPS = 4
N_DEV = 8
LANES = 128
AUG = 3
ADAM_LR, ADAM_B1, ADAM_B2, ADAM_EPS, ADAM_WD, ADAM_STEP = 0.001, 0.9, 0.999, 1e-8, 0.01, 10
VMEM_LIMIT_V7X = 52 * 1024 * 1024
MESH_ID = pl.DeviceIdType.MESH
NT = (((1,), (1,)), ((), ()))
TN = (((0,), (0,)), ((), ()))
ANY = pl.BlockSpec(memory_space=pl.ANY)


def _cp(*sem):
    return pltpu.CompilerParams(dimension_semantics=sem or None, vmem_limit_bytes=VMEM_LIMIT_V7X)


def _tile(n, target, mult=16):
    best = None
    for t in range(mult, n + 1, mult):
        if n % t == 0 and t <= target:
            best = t
    assert best is not None, (n, target, mult)
    return best


def _full(shape):
    nd = len(shape)
    return pl.BlockSpec(tuple(shape), lambda *_: (0,) * nd)


def _segments(i, tm, n_lat, fn):
    k, off = divmod(n_lat, tm)

    @pl.when(i < k)
    def _():
        fn(0, tm, 0)

    @pl.when(i == k)
    def _():
        if off:
            fn(0, off, 0)
        fn(off, tm, 1)

    @pl.when(i > k)
    def _():
        fn(0, tm, 1)


def _dot(a, b):
    return jnp.dot(a, b, preferred_element_type=F32)


def _dg(a, b, dims):
    return lax.dot_general(a, b, dims, preferred_element_type=F32)


def _split3(x):
    hi = x.astype(BF16)
    r1 = x - hi.astype(F32)
    mid = r1.astype(BF16)
    lo = (r1 - mid.astype(F32)).astype(BF16)
    return hi.astype(F32), mid.astype(F32), lo.astype(F32)


def _lane(shape):
    return lax.broadcasted_iota(jnp.int32, shape, len(shape) - 1)


def _aug(val, stat):
    lane = _lane(val.shape)
    hi, mid, lo = _split3(stat)
    ext = jnp.where(lane == 64, hi, jnp.where(lane == 65, mid, jnp.where(lane == 66, lo, 0.0)))
    return jnp.where(lane < 64, val, ext)


def _seg_mean(x, e):
    outs = []
    for g in range(x.shape[1] // LANES):
        blk = x[:, g * LANES:(g + 1) * LANES]
        hi = blk.astype(BF16)
        lo = (blk - hi.astype(F32)).astype(BF16)
        outs.append(_dot(hi, e) + _dot(lo, e))
    return outs[0] if len(outs) == 1 else jnp.concatenate(outs, axis=1)


def _rope(x, cos, sin_signed, inverse):
    w = x.shape[1]
    reps = w // LANES
    c = cos if reps == 1 else jnp.tile(cos, (1, reps))
    s = sin_signed if reps == 1 else jnp.tile(sin_signed, (1, reps))
    first = (_lane(x.shape) % 32) < 16
    partner = jnp.where(first, pltpu.roll(x, w - 16, 1), pltpu.roll(x, 16, 1))
    return x * c - partner * s if inverse else x * c + partner * s


def _sigmoid(x):
    return 1.0 / (1.0 + jnp.exp(-x))


_GELU_K = 0.7978845608028654
_GELU_C = 0.044715


def _gelu(x):
    return 0.5 * x * (1.0 + jnp.tanh(_GELU_K * (x + _GELU_C * x * x * x)))


def _gelu_grad(x):
    t = jnp.tanh(_GELU_K * (x + _GELU_C * x * x * x))
    return 0.5 * (1.0 + t) + 0.5 * x * (1.0 - t * t) * _GELU_K * (1.0 + 3.0 * _GELU_C * x * x)


def _heads_to_rows(x, n_heads):
    out = []
    for h in range(n_heads):
        grp = x[:, (h // 2) * LANES:(h // 2 + 1) * LANES]
        out.append(grp if h % 2 == 0 else pltpu.roll(grp, 64, 1))
    return out


def _rows_to_heads(blocks):
    outs = []
    lane = _lane(blocks[0].shape)
    for a in range(len(blocks) // 2):
        outs.append(jnp.where(lane < 64, blocks[2 * a], pltpu.roll(blocks[2 * a + 1], 64, 1)))
    return outs[0] if len(outs) == 1 else jnp.concatenate(outs, axis=1)


def _norm_mod(x, g, mod, i_shift, i_scale, n_lat):
    T, D = x.shape
    tm = _tile(T, 528)

    def body(x_ref, g_ref, mod_ref, h_ref):
        def fn(r0, r1, seg):
            xv = x_ref[r0:r1, :]
            r = lax.rsqrt(jnp.mean(xv * xv, axis=-1, keepdims=True) + EPS)
            n = xv * r * g_ref[...]
            h = n * (1.0 + mod_ref[seg, i_scale:i_scale + 1, :]) + mod_ref[seg, i_shift:i_shift + 1, :]
            h_ref[r0:r1, :] = h.astype(BF16)

        _segments(pl.program_id(0), tm, n_lat, fn)

    return pl.pallas_call(
        body, grid=(T // tm,), name="norm_mod",
        in_specs=[pl.BlockSpec((tm, D), lambda i: (i, 0)), _full(g.shape), _full(mod.shape)],
        out_specs=pl.BlockSpec((tm, D), lambda i: (i, 0)),
        out_shape=jax.ShapeDtypeStruct((T, D), BF16), compiler_params=_cp("parallel"))(x, g, mod)


def _norm_mod_bwd(x, dh, dres, g, mod, i_scale, n_lat):
    T, D = x.shape
    tm = _tile(T, 528)

    def body(x_ref, dh_ref, dres_ref, g_ref, mod_ref, dx_ref, dg_ref, dsh_ref, dsc_ref):
        i = pl.program_id(0)

        @pl.when(i == 0)
        def _():
            dg_ref[...] = jnp.zeros_like(dg_ref)
            dsh_ref[...] = jnp.zeros_like(dsh_ref)
            dsc_ref[...] = jnp.zeros_like(dsc_ref)

        def fn(r0, r1, seg):
            xv = x_ref[r0:r1, :]
            dh = dh_ref[r0:r1, :]
            r = lax.rsqrt(jnp.mean(xv * xv, axis=-1, keepdims=True) + EPS)
            xh = xv * r
            gv = g_ref[...]
            dsh_ref[seg] += jnp.sum(dh, axis=0, keepdims=True)
            dsc_ref[seg] += jnp.sum(dh * (xh * gv), axis=0, keepdims=True)
            dn = dh * (1.0 + mod_ref[seg, i_scale:i_scale + 1, :])
            dg_ref[...] += jnp.sum(dn * xh, axis=0, keepdims=True)
            gd = gv * dn
            dx_ref[r0:r1, :] = dres_ref[r0:r1, :] + r * (gd - xh * jnp.mean(xh * gd, axis=-1, keepdims=True))

        _segments(i, tm, n_lat, fn)

    row = pl.BlockSpec((tm, D), lambda i: (i, 0))
    return pl.pallas_call(
        body, grid=(T // tm,), name="norm_mod_bwd",
        in_specs=[row, row, row, _full(g.shape), _full(mod.shape)],
        out_specs=[row, _full((1, D)), _full((2, 1, D)), _full((2, 1, D))],
        out_shape=[jax.ShapeDtypeStruct((T, D), F32), jax.ShapeDtypeStruct((1, D), F32),
                   jax.ShapeDtypeStruct((2, 1, D), F32), jax.ShapeDtypeStruct((2, 1, D), F32)],
        compiler_params=_cp("arbitrary"))(x, dh, dres, g, mod)


def _gate_bwd(dx, f, mod, i_gate, n_lat):
    T, D = dx.shape
    tm = _tile(T, 528)

    def body(dx_ref, f_ref, mod_ref, dy_ref, dg_ref):
        i = pl.program_id(0)

        @pl.when(i == 0)
        def _():
            dg_ref[...] = jnp.zeros_like(dg_ref)

        def fn(r0, r1, seg):
            dxv = dx_ref[r0:r1, :]
            dy_ref[r0:r1, :] = (dxv * mod_ref[seg, i_gate:i_gate + 1, :]).astype(BF16)
            dg_ref[seg] += jnp.sum(dxv * f_ref[r0:r1, :], axis=0, keepdims=True)

        _segments(i, tm, n_lat, fn)

    row = pl.BlockSpec((tm, D), lambda i: (i, 0))
    return pl.pallas_call(
        body, grid=(T // tm,), name="gate_bwd",
        in_specs=[row, row, _full(mod.shape)], out_specs=[row, _full((2, 1, D))],
        out_shape=[jax.ShapeDtypeStruct((T, D), BF16), jax.ShapeDtypeStruct((2, 1, D), F32)],
        compiler_params=_cp("arbitrary"))(dx, f, mod)


def _mm_nn(a, w, out_dtype, name):
    M, K = a.shape
    J, _, n = w.shape
    tm = _tile(M, 1056)

    def body(a_ref, w_ref, o_ref):
        o_ref[...] = _dot(a_ref[...], w_ref[...]).astype(o_ref.dtype)

    return pl.pallas_call(
        body, grid=(M // tm, J), name=name,
        in_specs=[pl.BlockSpec((tm, K), lambda i, j: (i, 0)), pl.BlockSpec((None, K, n), lambda i, j: (j, 0, 0))],
        out_specs=pl.BlockSpec((tm, n), lambda i, j: (i, j)),
        out_shape=jax.ShapeDtypeStruct((M, J * n), out_dtype), compiler_params=_cp("parallel", "arbitrary"))(a, w)


def _mm_res(a3, w, res, mod, i_gate, n_lat, name):
    J, M, k = a3.shape
    N = w.shape[2]
    tm = _tile(M, 528)

    def body(a_ref, w_ref, res_ref, mod_ref, x_ref, f_ref):
        acc = _dot(a_ref[0], w_ref[0])
        for j in range(1, J):
            acc += _dot(a_ref[j], w_ref[j])
        f_ref[...] = acc

        def fn(r0, r1, seg):
            x_ref[r0:r1, :] = res_ref[r0:r1, :] + mod_ref[seg, i_gate:i_gate + 1, :] * f_ref[r0:r1, :]

        _segments(pl.program_id(0), tm, n_lat, fn)

    row = pl.BlockSpec((tm, N), lambda i: (i, 0))
    return pl.pallas_call(
        body, grid=(M // tm,), name=name,
        in_specs=[pl.BlockSpec((J, tm, k), lambda i: (0, i, 0)), _full(w.shape), row, _full(mod.shape)],
        out_specs=[row, row],
        out_shape=[jax.ShapeDtypeStruct((M, N), F32), jax.ShapeDtypeStruct((M, N), F32)],
        compiler_params=_cp("parallel"))(a3, w, res, mod)


def _mm_nt_acc(dys, ws, row_major, name):
    J, K, n = ws[0].shape
    M = dys[0].shape[0] if row_major else dys[0].shape[1]
    tm = _tile(M, 1056)
    P = len(dys)

    def body(*refs):
        o_ref = refs[2 * P]
        j = pl.program_id(1)
        part = _dg(refs[0][...], refs[P][...], NT)
        for p in range(1, P):
            part += _dg(refs[p][...], refs[P + p][...], NT)

        @pl.when(j == 0)
        def _():
            o_ref[...] = part

        @pl.when(j > 0)
        def _():
            o_ref[...] += part

    dy_spec = (pl.BlockSpec((tm, n), lambda i, j: (i, j)) if row_major
               else pl.BlockSpec((None, tm, n), lambda i, j: (j, i, 0)))
    w_spec = pl.BlockSpec((None, K, n), lambda i, j: (j, 0, 0))
    return pl.pallas_call(
        body, grid=(M // tm, J), name=name,
        in_specs=[dy_spec] * P + [w_spec] * P,
        out_specs=pl.BlockSpec((tm, K), lambda i, j: (i, 0)),
        out_shape=jax.ShapeDtypeStruct((M, K), F32), compiler_params=_cp("parallel", "arbitrary"))(*dys, *ws)


def _mm_tn(x, dy, x_spec, dy_spec, J, K, n, T, name):
    tk = _tile(T, 1056)

    def body(x_ref, dy_ref, o_ref):
        t = pl.program_id(1)
        part = _dg(x_ref[...], dy_ref[...], TN)

        @pl.when(t == 0)
        def _():
            o_ref[...] = part

        @pl.when(t > 0)
        def _():
            o_ref[...] += part

    return pl.pallas_call(
        body, grid=(J, T // tk), name=name,
        in_specs=[x_spec(tk), dy_spec(tk)],
        out_specs=pl.BlockSpec((None, K, n), lambda j, t: (j, 0, 0)),
        out_shape=jax.ShapeDtypeStruct((J, K, n), F32), compiler_params=_cp("parallel", "arbitrary"))(x, dy)


def _rows(width):
    return lambda tk: pl.BlockSpec((tk, width), lambda j, t: (t, 0))


def _row_cols(width):
    return lambda tk: pl.BlockSpec((tk, width), lambda j, t: (t, j))


def _shard_rows(width):
    return lambda tk: pl.BlockSpec((None, tk, width), lambda j, t: (j, t, 0))


def _ffn_up(h, w1, w3):
    T, D = h.shape
    J, _, n = w1.shape
    tm = _tile(T, 1056)

    def body(h_ref, w1_ref, w3_ref, a1_ref, a3_ref, act_ref):
        hv = h_ref[...]
        a1 = _dot(hv, w1_ref[...])
        a3 = _dot(hv, w3_ref[...])
        a1_ref[...] = a1
        a3_ref[...] = a3
        act_ref[...] = (a1 * _sigmoid(a1) * a3).astype(BF16)

    w_spec = pl.BlockSpec((None, D, n), lambda i, j: (j, 0, 0))
    o_spec = pl.BlockSpec((None, tm, n), lambda i, j: (j, i, 0))
    return pl.pallas_call(
        body, grid=(T // tm, J), name="ffn_up",
        in_specs=[pl.BlockSpec((tm, D), lambda i, j: (i, 0)), w_spec, w_spec], out_specs=[o_spec] * 3,
        out_shape=[jax.ShapeDtypeStruct((J, T, n), F32)] * 2 + [jax.ShapeDtypeStruct((J, T, n), BF16)],
        compiler_params=_cp("parallel", "arbitrary"))(h, w1, w3)


def _ffn_down_bwd(dy, w2, a1, a3):
    T, D = dy.shape
    J, n, _ = w2.shape
    tm = _tile(T, 1056)

    def body(dy_ref, w2_ref, a1_ref, a3_ref, da1_ref, da3_ref):
        dact = _dg(dy_ref[...], w2_ref[...], NT)
        a1v = a1_ref[...]
        sig = _sigmoid(a1v)
        da3_ref[...] = (dact * a1v * sig).astype(BF16)
        da1_ref[...] = (dact * a3_ref[...] * (sig * (1.0 + a1v * (1.0 - sig)))).astype(BF16)

    a_spec = pl.BlockSpec((None, tm, n), lambda i, j: (j, i, 0))
    return pl.pallas_call(
        body, grid=(T // tm, J), name="ffn_down_bwd",
        in_specs=[pl.BlockSpec((tm, D), lambda i, j: (i, 0)), pl.BlockSpec((None, n, D), lambda i, j: (j, 0, 0)),
                  a_spec, a_spec],
        out_specs=[a_spec, a_spec], out_shape=[jax.ShapeDtypeStruct((J, T, n), BF16)] * 2,
        compiler_params=_cp("parallel", "arbitrary"))(dy, w2, a1, a3)


def _qkv_prep(p, cos, sin, qg, kg, e):
    T = p.shape[0]
    tm = _tile(T, 528)

    def body(p_ref, cos_ref, sin_ref, qg_ref, kg_ref, e_ref, q_ref, k_ref, v_ref):
        ev = e_ref[...]
        cv, sv = cos_ref[...], sin_ref[...]
        xq = p_ref[:, 0:512]
        qn = xq * lax.rsqrt(_seg_mean(xq * xq, ev) + EPS) * qg_ref[...]
        qr = _rope(qn, cv, sv, False) * (HEAD_DIM ** -0.5)
        xk = p_ref[:, 512:640]
        kn = xk * lax.rsqrt(_seg_mean(xk * xk, ev) + EPS) * kg_ref[...]
        kr = _rope(kn, cv, sv, False)
        lane = _lane((tm, LANES))
        ones = jnp.where(lane < 64 + AUG, -1.0, 0.0)
        for h, blk in enumerate(_heads_to_rows(qr, N_Q_HEADS)):
            q_ref[h] = jnp.where(lane < 64, blk, 0.0).astype(BF16)
        for h, blk in enumerate(_heads_to_rows(kr, N_KV_HEADS)):
            k_ref[h] = jnp.where(lane < 64, blk, ones).astype(BF16)
        for h, blk in enumerate(_heads_to_rows(p_ref[:, 640:768], N_KV_HEADS)):
            v_ref[h] = jnp.where(lane < 64, blk, ones).astype(BF16)

    tab = pl.BlockSpec((tm, LANES), lambda i: (i, 0))
    return pl.pallas_call(
        body, grid=(T // tm,), name="qkv_prep",
        in_specs=[pl.BlockSpec((tm, QKV_W), lambda i: (i, 1)), tab, tab, _full(qg.shape), _full(kg.shape),
                  _full(e.shape)],
        out_specs=[pl.BlockSpec((N_Q_HEADS, tm, LANES), lambda i: (0, i, 0)),
                   pl.BlockSpec((N_KV_HEADS, tm, LANES), lambda i: (0, i, 0)),
                   pl.BlockSpec((N_KV_HEADS, tm, LANES), lambda i: (0, i, 0))],
        out_shape=[jax.ShapeDtypeStruct((N_Q_HEADS, T, LANES), BF16),
                   jax.ShapeDtypeStruct((N_KV_HEADS, T, LANES), BF16),
                   jax.ShapeDtypeStruct((N_KV_HEADS, T, LANES), BF16)],
        compiler_params=_cp("parallel"))(p, cos, sin, qg, kg, e)


def _qkv_prep_bwd(dp, dq, dk, dv, p, cos, sin, qg, kg, e, fold):
    T = p.shape[0]
    tm = _tile(T, 528)
    nt = T // tm

    def body(dp_in, dq_ref, dk_ref, dv_ref, p_ref, cos_ref, sin_ref, qg_ref, kg_ref, e_ref, fold_ref,
             dp_ref, dqg_ref, dkg_ref, accq, acck):
        del dp_in
        i = pl.program_id(0)

        @pl.when(i == 0)
        def _():
            accq[...] = jnp.zeros_like(accq)
            acck[...] = jnp.zeros_like(acck)

        ev = e_ref[...]
        cv, sv = cos_ref[...], sin_ref[...]

        def one(x, dr, gain, acc):
            r = lax.rsqrt(_seg_mean(x * x, ev) + EPS)
            xh = x * r
            dn = _rope(dr, cv, sv, True)
            acc[0:1, :] += jnp.sum(dn * xh, axis=0, keepdims=True)
            gd = gain * dn
            return r * (gd - xh * _seg_mean(xh * gd, ev))

        dqr = _rows_to_heads([dq_ref[h] for h in range(N_Q_HEADS)]) * (HEAD_DIM ** -0.5)
        dkr = _rows_to_heads([dk_ref[h] for h in range(N_KV_HEADS)])
        dvv = _rows_to_heads([dv_ref[h] for h in range(N_KV_HEADS)])
        dp_ref[:, 0:512] = one(p_ref[:, 0:512], dqr, qg_ref[...], accq).astype(BF16)
        dp_ref[:, 512:640] = one(p_ref[:, 512:640], dkr, kg_ref[...], acck).astype(BF16)
        dp_ref[:, 640:768] = dvv.astype(BF16)

        @pl.when(i == nt - 1)
        def _():
            fv = fold_ref[...]
            dqg_ref[...] = jnp.dot(accq[...], fv, preferred_element_type=F32, precision=lax.Precision.HIGHEST)
            dkg_ref[...] = jnp.dot(acck[...], fv[0:LANES, :], preferred_element_type=F32,
                                   precision=lax.Precision.HIGHEST)

    tab = pl.BlockSpec((tm, LANES), lambda i: (i, 0))
    sec = pl.BlockSpec((tm, QKV_W), lambda i: (i, 1))
    return pl.pallas_call(
        body, grid=(nt,), name="qkv_prep_bwd",
        in_specs=[ANY, pl.BlockSpec((N_Q_HEADS, tm, LANES), lambda i: (0, i, 0)),
                  pl.BlockSpec((N_KV_HEADS, tm, LANES), lambda i: (0, i, 0)),
                  pl.BlockSpec((N_KV_HEADS, tm, LANES), lambda i: (0, i, 0)),
                  sec, tab, tab, _full(qg.shape), _full(kg.shape), _full(e.shape), _full(fold.shape)],
        out_specs=[sec, _full((8, LANES)), _full((8, LANES))],
        out_shape=[jax.ShapeDtypeStruct(dp.shape, BF16), jax.ShapeDtypeStruct((8, LANES), F32),
                   jax.ShapeDtypeStruct((8, LANES), F32)],
        scratch_shapes=[pltpu.VMEM((8, 512), F32), pltpu.VMEM((8, LANES), F32)],
        input_output_aliases={0: 0}, compiler_params=_cp("arbitrary"))(dp, dq, dk, dv, p, cos, sin, qg, kg, e, fold)


def _flash_fwd(q, k, v, n_lat):
    _, _, T, _ = q.shape
    tq = tk = 256
    nq = T // tq
    M = GROUP * tq

    def body(q_ref, k_ref, v_ref, o_ref, qa_ref):
        i = pl.program_id(0)
        qv = q_ref[...].reshape(M, LANES)
        lo = jnp.where(i >= n_lat // tq, n_lat // tk, 0)

        def step(s, carry):
            m, acc = carry
            r0 = pl.multiple_of(s * tk, tk)
            sc = _dg(qv, k_ref[pl.ds(r0, tk), :], NT)
            m_new = jnp.maximum(m, jnp.max(sc, axis=1, keepdims=True))
            pr = jnp.exp(sc - m_new)
            acc = jnp.exp(m - m_new) * acc + _dot(pr.astype(BF16), v_ref[pl.ds(r0, tk), :])
            return m_new, acc

        m, acc = lax.fori_loop(lo, T // tk, step,
                               (jnp.full((M, 1), -1e30, F32), jnp.zeros((M, LANES), F32)))
        den = -acc[:, 64:65]
        out = acc / den
        o_ref[...] = _rows_to_heads([out[g * tq:(g + 1) * tq] for g in range(GROUP)]).astype(BF16)
        qa_ref[...] = _aug(qv.astype(F32), m + jnp.log(den)).astype(BF16).reshape(GROUP, tq, LANES)

    q_spec = pl.BlockSpec((None, GROUP, tq, LANES), lambda i, h: (h, 0, i, 0))
    kv_spec = pl.BlockSpec((None, T, LANES), lambda i, h: (h, 0, 0))
    return pl.pallas_call(
        body, grid=(nq, N_KV_HEADS), name="flash_fwd",
        in_specs=[q_spec, kv_spec, kv_spec],
        out_specs=[pl.BlockSpec((tq, GROUP * HEAD_DIM), lambda i, h: (i, h)), q_spec],
        out_shape=[jax.ShapeDtypeStruct((T, N_Q_HEADS * HEAD_DIM), BF16), jax.ShapeDtypeStruct(q.shape, BF16)],
        compiler_params=_cp("parallel", "arbitrary"))(q, k, v)


def _flash_bwd(qa, doa, k, v, n_lat):
    _, _, T, _ = qa.shape
    tq = tk = 256
    nkv = T // tk
    M = GROUP * tq

    def body(qa_hbm, doa_hbm, k_ref, v_ref, dq_hbm, dk_ref, dv_ref, q_sc, do_sc, dq_sc, sems):
        h = pl.program_id(0)
        j = pl.program_id(1)

        @pl.when(j == 0)
        def _():
            c1 = pltpu.make_async_copy(qa_hbm.at[h], q_sc, sems.at[0])
            c2 = pltpu.make_async_copy(doa_hbm.at[h], do_sc, sems.at[1])
            c1.start()
            c2.start()
            dq_sc[...] = jnp.zeros_like(dq_sc)
            c1.wait()
            c2.wait()

        kb = k_ref[...]
        vb = v_ref[...]
        n_q = jnp.where(j >= n_lat // tk, T // tq, n_lat // tq)

        def step(i, carry):
            dk, dv = carry
            r0 = pl.multiple_of(i * tq, tq)
            qv = q_sc[:, pl.ds(r0, tq), :].reshape(M, LANES)
            dov = do_sc[:, pl.ds(r0, tq), :].reshape(M, LANES)
            pr = jnp.exp(_dg(qv, kb, NT))
            ds = (pr * _dg(dov, vb, NT)).astype(BF16)
            dv = dv + _dg(pr.astype(BF16), dov, TN)
            dk = dk + _dg(ds, qv, TN)
            dq_sc[:, pl.ds(r0, tq), :] += _dot(ds, kb).reshape(GROUP, tq, LANES)
            return dk, dv

        dk, dv = lax.fori_loop(0, n_q, step, (jnp.zeros((tk, LANES), F32), jnp.zeros((tk, LANES), F32)))
        dk_ref[...] = dk
        dv_ref[...] = dv

        @pl.when(j == nkv - 1)
        def _():
            c3 = pltpu.make_async_copy(dq_sc, dq_hbm.at[h], sems.at[2])
            c3.start()
            c3.wait()

    kv_spec = pl.BlockSpec((None, tk, LANES), lambda h, j: (h, j, 0))
    return pl.pallas_call(
        body, grid=(N_KV_HEADS, nkv), name="flash_bwd",
        in_specs=[ANY, ANY, kv_spec, kv_spec], out_specs=[ANY, kv_spec, kv_spec],
        out_shape=[jax.ShapeDtypeStruct(qa.shape, F32), jax.ShapeDtypeStruct(k.shape, F32),
                   jax.ShapeDtypeStruct(k.shape, F32)],
        scratch_shapes=[pltpu.VMEM((GROUP, T, LANES), BF16), pltpu.VMEM((GROUP, T, LANES), BF16),
                        pltpu.VMEM((GROUP, T, LANES), F32), pltpu.SemaphoreType.DMA((3,))],
        compiler_params=_cp("arbitrary", "arbitrary"))(qa, doa, k, v)


def _conv_masks(i, tm, n_lat, T):
    row = lax.broadcasted_iota(jnp.int32, (tm, 1), 0)
    g = row + i * tm
    return row, (g == 0) | (g == n_lat), (g == n_lat - 1) | (g == T - 1)


def _shift_rows(v, prev_row, next_row, row, first, last):
    tm = v.shape[0]
    down = jnp.where(row == 0, prev_row, pltpu.roll(v, 1, 0))
    up = jnp.where(row == tm - 1, next_row, pltpu.roll(v, tm - 1, 0))
    return jnp.where(first, 0.0, down), jnp.where(last, 0.0, up)


def _halo_specs(tm, T, width, col):
    nb = T // 8
    return (pl.BlockSpec((8, width), lambda i: (jnp.maximum(i * (tm // 8) - 1, 0), col)),
            pl.BlockSpec((8, width), lambda i: (jnp.minimum((i + 1) * (tm // 8), nb - 1), col)))


def _conv_fwd(p, cw, n_lat):
    T = p.shape[0]
    tm = _tile(T, 1056)

    def body(p_ref, pp_ref, pn_ref, cw_ref, o_ref):
        row, first, last = _conv_masks(pl.program_id(0), tm, n_lat, T)
        z = p_ref[:, 256:512] * p_ref[:, 512:768]
        zp = pp_ref[7:8, 256:512] * pp_ref[7:8, 512:768]
        zn = pn_ref[0:1, 256:512] * pn_ref[0:1, 512:768]
        zd, zu = _shift_rows(z, zp, zn, row, first, last)
        conv = cw_ref[0:1, :] * zd + cw_ref[1:2, :] * z + cw_ref[2:3, :] * zu
        o_ref[...] = (p_ref[:, 0:256] * conv).astype(BF16)

    prev, nxt = _halo_specs(tm, T, 768, 0)
    return pl.pallas_call(
        body, grid=(T // tm,), name="conv_fwd",
        in_specs=[pl.BlockSpec((tm, 768), lambda i: (i, 0)), prev, nxt, _full(cw.shape)],
        out_specs=pl.BlockSpec((tm, CONV_W), lambda i: (i, 0)),
        out_shape=jax.ShapeDtypeStruct((T, CONV_W), BF16), compiler_params=_cp("parallel"))(p, p, p, cw)


def _conv_bwd(dp, dy, p, cw, n_lat):
    T = p.shape[0]
    tm = _tile(T, 1056)

    def body(dp_in, dy_ref, dyp_ref, dyn_ref, p_ref, pp_ref, pn_ref, cw_ref, dp_ref, dcw_ref):
        del dp_in
        i = pl.program_id(0)

        @pl.when(i == 0)
        def _():
            dcw_ref[...] = jnp.zeros_like(dcw_ref)

        row, first, last = _conv_masks(i, tm, n_lat, T)
        ab, ac, ax = p_ref[:, 0:256], p_ref[:, 256:512], p_ref[:, 512:768]
        z = ac * ax
        zp = pp_ref[7:8, 256:512] * pp_ref[7:8, 512:768]
        zn = pn_ref[0:1, 256:512] * pn_ref[0:1, 512:768]
        zd, zu = _shift_rows(z, zp, zn, row, first, last)
        w0, w1, w2 = cw_ref[0:1, :], cw_ref[1:2, :], cw_ref[2:3, :]
        dy = dy_ref[...]
        dc = dy * ab
        dcd, dcu = _shift_rows(dc, dyp_ref[7:8, :] * pp_ref[7:8, 0:256], dyn_ref[0:1, :] * pn_ref[0:1, 0:256],
                               row, first, last)
        dz = w0 * dcu + w1 * dc + w2 * dcd
        dp_ref[:, 0:256] = (dy * (w0 * zd + w1 * z + w2 * zu)).astype(BF16)
        dp_ref[:, 256:512] = (dz * ax).astype(BF16)
        dp_ref[:, 512:768] = (dz * ac).astype(BF16)
        dcw_ref[0:1, :] += jnp.sum(dc * zd, axis=0, keepdims=True)
        dcw_ref[1:2, :] += jnp.sum(dc * z, axis=0, keepdims=True)
        dcw_ref[2:3, :] += jnp.sum(dc * zu, axis=0, keepdims=True)

    prev, nxt = _halo_specs(tm, T, 768, 0)
    dprev, dnxt = _halo_specs(tm, T, CONV_W, 0)
    sec = pl.BlockSpec((tm, 768), lambda i: (i, 0))
    return pl.pallas_call(
        body, grid=(T // tm,), name="conv_bwd",
        in_specs=[ANY, pl.BlockSpec((tm, CONV_W), lambda i: (i, 0)), dprev, dnxt, sec, prev, nxt, _full(cw.shape)],
        out_specs=[sec, _full((8, CONV_W))],
        out_shape=[jax.ShapeDtypeStruct(dp.shape, BF16), jax.ShapeDtypeStruct((8, CONV_W), F32)],
        input_output_aliases={0: 0}, compiler_params=_cp("arbitrary"))(dp, dy, dy, dy, p, p, p, cw)


def _gmlp_mix(bd_ref, vs, grp):
    out = jnp.zeros((2 * CHUNK, SG_W), F32)
    for g in range(4):
        out = jnp.where(grp == g, _dot(bd_ref[g], vs), out)
    return out


def _gmlp_fwd(p, sgn, bd, bias):
    T = p.shape[0]
    tm = _tile(T, 768, 2 * CHUNK)

    def body(p_ref, sgn_ref, bd_ref, bias_ref, o_ref):
        x = _gelu(p_ref[:, 256:512])
        vn = (x * lax.rsqrt(jnp.mean(x * x, axis=-1, keepdims=True) + EPS) * sgn_ref[...]).astype(BF16)
        grp = _lane((2 * CHUNK, SG_W)) // 64
        for s in range(tm // (2 * CHUNK)):
            rs = slice(s * 2 * CHUNK, (s + 1) * 2 * CHUNK)
            mixed = _gmlp_mix(bd_ref, vn[rs], grp) + bias_ref[...]
            o_ref[rs, :] = (_gelu(p_ref[rs, 0:256]) * mixed).astype(BF16)

    return pl.pallas_call(
        body, grid=(T // tm,), name="gmlp_fwd",
        in_specs=[pl.BlockSpec((tm, 2 * SG_W), lambda i: (i, 3)), _full(sgn.shape), _full(bd.shape),
                  _full(bias.shape)],
        out_specs=pl.BlockSpec((tm, SG_W), lambda i: (i, 0)),
        out_shape=jax.ShapeDtypeStruct((T, SG_W), BF16), compiler_params=_cp("parallel"))(p, sgn, bd, bias)


def _gmlp_bwd(dp, dy, p, sgn, bd, bdt, bias, gsum):
    T = p.shape[0]
    tm = _tile(T, 768, 2 * CHUNK)
    nt = T // tm
    C2 = 2 * CHUNK

    def body(dp_in, dy_ref, p_ref, sgn_ref, bd_ref, bdt_ref, bias_ref, gsum_ref,
             dp_ref, dsg_ref, dws_ref, dbs_ref, acc_w, acc_b):
        del dp_in
        i = pl.program_id(0)

        @pl.when(i == 0)
        def _():
            dsg_ref[...] = jnp.zeros_like(dsg_ref)
            acc_w[...] = jnp.zeros_like(acc_w)
            acc_b[...] = jnp.zeros_like(acc_b)

        u = p_ref[:, 0:256]
        sv = p_ref[:, 256:512]
        ug = _gelu(u)
        x = _gelu(sv)
        r = lax.rsqrt(jnp.mean(x * x, axis=-1, keepdims=True) + EPS)
        xh = x * r
        sg = sgn_ref[...]
        vn = (xh * sg).astype(BF16)
        grp = _lane((C2, SG_W)) // 64
        dug, dvn = [], []
        for s in range(tm // C2):
            rs = slice(s * C2, (s + 1) * C2)
            vs = vn[rs]
            dys = dy_ref[rs, :]
            dug.append(dys * (_gmlp_mix(bd_ref, vs, grp) + bias_ref[...]))
            dmix = dys * ug[rs]
            acc_b[...] += dmix
            dmb = dmix.astype(BF16)
            dvn.append(_gmlp_mix(bdt_ref, dmb, grp))
            for g in range(4):
                acc_w[g] += _dg(jnp.where(grp == g, dmb, jnp.zeros_like(dmb)), vs, NT)
        dug = jnp.concatenate(dug, axis=0)
        dvn = jnp.concatenate(dvn, axis=0)
        dsg_ref[...] += jnp.sum(dvn * xh, axis=0, keepdims=True)
        gd = sg * dvn
        dx = r * (gd - xh * jnp.mean(xh * gd, axis=-1, keepdims=True))
        dp_ref[:, 0:256] = (dug * _gelu_grad(u)).astype(BF16)
        dp_ref[:, 256:512] = (dx * _gelu_grad(sv)).astype(BF16)

        @pl.when(i == nt - 1)
        def _():
            for g in range(4):
                dws_ref[g] = acc_w[g, 0:CHUNK, 0:CHUNK] + acc_w[g, CHUNK:C2, CHUNK:C2]
            dbs_ref[...] = jnp.dot(acc_b[0:CHUNK, :] + acc_b[CHUNK:C2, :], gsum_ref[...],
                                   preferred_element_type=F32, precision=lax.Precision.HIGHEST)

    sec = pl.BlockSpec((tm, 2 * SG_W), lambda i: (i, 3))
    return pl.pallas_call(
        body, grid=(nt,), name="gmlp_bwd",
        in_specs=[ANY, pl.BlockSpec((tm, SG_W), lambda i: (i, 0)), sec, _full(sgn.shape), _full(bd.shape),
                  _full(bdt.shape), _full(bias.shape), _full(gsum.shape)],
        out_specs=[sec, _full((1, SG_W)), _full((4, CHUNK, CHUNK)), _full((CHUNK, LANES))],
        out_shape=[jax.ShapeDtypeStruct(dp.shape, BF16), jax.ShapeDtypeStruct((1, SG_W), F32),
                   jax.ShapeDtypeStruct((4, CHUNK, CHUNK), F32), jax.ShapeDtypeStruct((CHUNK, LANES), F32)],
        scratch_shapes=[pltpu.VMEM((4, C2, C2), F32), pltpu.VMEM((C2, SG_W), F32)],
        input_output_aliases={0: 0}, compiler_params=_cp("arbitrary"))(dp, dy, p, sgn, bd, bdt, bias, gsum)


def _merge_fwd(ya, at, yc, p, wa, wb, wc):
    T = p.shape[0]
    tm = _tile(T, 528)
    n = wa.shape[2]

    def body(ya_ref, at_ref, yc_ref, ga_ref, gb_ref, gc_ref, wa_ref, wb_ref, wc_ref, o_ref):
        yav, atv, ycv = ya_ref[...], at_ref[...], yc_ref[...]
        for j in range(N_CHIPS):
            cs = slice(j * n, (j + 1) * n)
            m = (_sigmoid(ga_ref[:, cs]) * _dot(yav, wa_ref[j]) + _sigmoid(gb_ref[:, cs]) * _dot(atv, wb_ref[j])
                 + _sigmoid(gc_ref[:, cs]) * _dot(ycv, wc_ref[j]))
            o_ref[:, cs] = m.astype(BF16)

    def rows(w, col=0):
        return pl.BlockSpec((tm, w), lambda i: (i, col))

    return pl.pallas_call(
        body, grid=(T // tm,), name="merge_fwd",
        in_specs=[rows(CONV_W), rows(512), rows(SG_W), rows(D_MODEL, 2), rows(D_MODEL, 3), rows(D_MODEL, 4),
                  _full(wa.shape), _full(wb.shape), _full(wc.shape)],
        out_specs=rows(D_MODEL), out_shape=jax.ShapeDtypeStruct((T, D_MODEL), BF16),
        compiler_params=_cp("parallel"))(ya, at, yc, p, p, p, wa, wb, wc)


def _merge_bwd(dyo, ya, at, yc, p, wa, wb, wc, wo):
    T = p.shape[0]
    tm = _tile(T, 528)
    n = wa.shape[2]

    def body(dyo_ref, ya_ref, at_ref, yc_ref, ga_ref, gb_ref, gc_ref, wa_ref, wb_ref, wc_ref, wo_ref,
             dp_ref, dya_ref, doa_ref, dyc_ref, dwa_ref, dwb_ref, dwc_ref):
        i = pl.program_id(0)

        @pl.when(i == 0)
        def _():
            dwa_ref[...] = jnp.zeros_like(dwa_ref)
            dwb_ref[...] = jnp.zeros_like(dwb_ref)
            dwc_ref[...] = jnp.zeros_like(dwc_ref)

        dp_ref[:, 0:OFF_G] = jnp.zeros((tm, OFF_G), BF16)
        dm = _dg(dyo_ref[...], wo_ref[...], NT)
        yav, atv, ycv = ya_ref[...], at_ref[...], yc_ref[...]
        dya = jnp.zeros((tm, CONV_W), F32)
        dat = jnp.zeros((tm, 512), F32)
        dyc = jnp.zeros((tm, SG_W), F32)
        for j in range(N_CHIPS):
            cs = slice(j * n, (j + 1) * n)
            dmj = dm[:, cs]
            for y_in, w_ref, g_ref, dw_ref, which in (
                    (yav, wa_ref, ga_ref, dwa_ref, 0), (atv, wb_ref, gb_ref, dwb_ref, 1),
                    (ycv, wc_ref, gc_ref, dwc_ref, 2)):
                sg = _sigmoid(g_ref[:, cs])
                y = _dot(y_in, w_ref[j])
                c0 = OFF_G + which * D_MODEL + j * n
                dp_ref[:, c0:c0 + n] = (dmj * y * sg * (1.0 - sg)).astype(BF16)
                dyb = (dmj * sg).astype(BF16)
                dw_ref[j] += _dg(y_in, dyb, TN)
                back = _dg(dyb, w_ref[j], NT)
                if which == 0:
                    dya = dya + back
                elif which == 1:
                    dat = dat + back
                else:
                    dyc = dyc + back
        dya_ref[...] = dya
        dyc_ref[...] = dyc
        prod = dat * atv.astype(F32)
        lane = _lane((tm, LANES))
        dat_rows = _heads_to_rows(dat, N_Q_HEADS)
        for h in range(N_Q_HEADS):
            grp = prod[:, (h // 2) * LANES:(h // 2 + 1) * LANES]
            keep = (lane < 64) if h % 2 == 0 else (lane >= 64)
            delta = jnp.sum(jnp.where(keep, grp, 0.0), axis=1, keepdims=True)
            doa_ref[h] = _aug(dat_rows[h], delta).astype(BF16)

    def rows(w, col=0):
        return pl.BlockSpec((tm, w), lambda i: (i, col))

    return pl.pallas_call(
        body, grid=(T // tm,), name="merge_bwd",
        in_specs=[rows(D_MODEL), rows(CONV_W), rows(512), rows(SG_W), rows(D_MODEL, 2), rows(D_MODEL, 3),
                  rows(D_MODEL, 4), _full(wa.shape), _full(wb.shape), _full(wc.shape), _full(wo.shape)],
        out_specs=[rows(IN_W), rows(CONV_W),
                   pl.BlockSpec((N_Q_HEADS, tm, LANES), lambda i: (0, i, 0)), rows(SG_W),
                   _full(wa.shape), _full(wb.shape), _full(wc.shape)],
        out_shape=[jax.ShapeDtypeStruct((T, IN_W), BF16)] + [
            jax.ShapeDtypeStruct((T, CONV_W), F32), jax.ShapeDtypeStruct((N_Q_HEADS, T, LANES), BF16),
            jax.ShapeDtypeStruct((T, SG_W), F32), jax.ShapeDtypeStruct(wa.shape, F32),
            jax.ShapeDtypeStruct(wb.shape, F32), jax.ShapeDtypeStruct(wc.shape, F32)],
        compiler_params=_cp("arbitrary"))(dyo, ya, at, yc, p, p, p, wa, wb, wc, wo)


def _loss_grad(xf, tgt, n_lat):
    T, D = xf.shape
    tm = _tile(np.gcd(n_lat, T), 512)
    nl = n_lat // tm

    def body(x_ref, t_ref, dy_ref, l_ref):
        i = pl.program_id(0)

        @pl.when(i == 0)
        def _():
            l_ref[...] = jnp.zeros_like(l_ref)

        @pl.when(i < nl)
        def _():
            err = x_ref[...] - t_ref[...]
            dy_ref[...] = err * (1.0 / D)
            sq = jnp.sum(jnp.sum(err * err, axis=1, keepdims=True), axis=0, keepdims=True)
            l_ref[...] += (0.5 / D) * sq

        @pl.when(i >= nl)
        def _():
            dy_ref[...] = jnp.zeros_like(dy_ref)

    return pl.pallas_call(
        body, grid=(T // tm,), name="loss_grad",
        in_specs=[pl.BlockSpec((tm, D), lambda i: (i, 0)), pl.BlockSpec((tm, D), lambda i: (jnp.minimum(i, nl - 1), 0))],
        out_specs=[pl.BlockSpec((tm, D), lambda i: (i, 0)), _full((8, LANES))],
        out_shape=[jax.ShapeDtypeStruct((T, D), F32), jax.ShapeDtypeStruct((8, LANES), F32)],
        compiler_params=_cp("arbitrary"))(xf, tgt)


def _row_tile(R, C):
    if R * C <= (1 << 19) or R % 8:
        return R
    return _tile(R, max(8, (1 << 19) // C), 8)


def _adamw(w, m, v, g1, g2=None):
    shape = w.shape
    C = shape[-1]
    R = int(np.prod(shape[:-1])) if len(shape) > 1 else 1
    tr = _row_tile(R, C)
    ins = [a.reshape(R, C) for a in ((w, m, v, g1) if g2 is None else (w, m, v, g1, g2))]

    def body(*refs):
        w_ref, m_ref, v_ref = refs[0], refs[1], refs[2]
        g_ref, d_ref, m2_ref, v2_ref = refs[-4:]
        g = refs[3][...] if g2 is None else refs[3][...] + refs[4][...]
        m2 = ADAM_B1 * m_ref[...] + (1.0 - ADAM_B1) * g
        v2 = ADAM_B2 * v_ref[...] + (1.0 - ADAM_B2) * (g * g)
        m_hat = m2 / (1.0 - ADAM_B1 ** ADAM_STEP)
        v_hat = v2 / (1.0 - ADAM_B2 ** ADAM_STEP)
        g_ref[...] = g
        d_ref[...] = -ADAM_LR * (m_hat / (jnp.sqrt(v_hat) + ADAM_EPS) + ADAM_WD * w_ref[...])
        m2_ref[...] = m2
        v2_ref[...] = v2

    spec = pl.BlockSpec((tr, C), lambda i: (i, 0))
    outs = pl.pallas_call(
        body, grid=(R // tr,), name="adamw", in_specs=[spec] * len(ins), out_specs=[spec] * 4,
        out_shape=[jax.ShapeDtypeStruct((R, C), F32)] * 4, compiler_params=_cp("parallel"))(*ins)
    return [o.reshape(shape) for o in outs]


def _sum_lead(x, name):
    n, R, C = x.shape
    tr = _row_tile(R, C * n)

    def body(x_ref, o_ref):
        acc = x_ref[0].astype(F32)
        for s in range(1, n):
            acc = acc + x_ref[s].astype(F32)
        o_ref[...] = acc

    return pl.pallas_call(
        body, grid=(R // tr,), name=name, in_specs=[pl.BlockSpec((n, tr, C), lambda i: (0, i, 0))],
        out_specs=pl.BlockSpec((tr, C), lambda i: (i, 0)), out_shape=jax.ShapeDtypeStruct((R, C), F32),
        compiler_params=_cp("parallel"))(x)


def _silu(x):
    return x * _sigmoid(x)


def _mod_fwd(a_raw, w_mod, bsh):
    L, D, n = w_mod.shape

    def body(a_ref, w_ref, b_ref, o_ref):
        o_ref[...] = _dot(_silu(a_ref[...]).astype(BF16), w_ref[...].astype(BF16)) + b_ref[...]

    return pl.pallas_call(
        body, grid=(L,), name="mod_fwd",
        in_specs=[_full(a_raw.shape), pl.BlockSpec((None, D, n), lambda l: (l, 0, 0)),
                  pl.BlockSpec((None, 1, n), lambda l: (l, 0, 0))],
        out_specs=pl.BlockSpec((None, 16, n), lambda l: (l, 0, 0)),
        out_shape=jax.ShapeDtypeStruct((L, 16, n), F32), compiler_params=_cp("parallel"))(a_raw, w_mod, bsh)


def _wmod_grad(a_raw, dms):
    L, _, n = dms.shape
    D = a_raw.shape[1]

    def body(a_ref, dm_ref, o_ref):
        o_ref[...] = _dg(_silu(a_ref[...]).astype(BF16), dm_ref[...].astype(BF16), TN)

    return pl.pallas_call(
        body, grid=(L,), name="wmod_grad",
        in_specs=[_full(a_raw.shape), pl.BlockSpec((None, 16, n), lambda l: (l, 0, 0))],
        out_specs=pl.BlockSpec((None, D, n), lambda l: (l, 0, 0)),
        out_shape=jax.ShapeDtypeStruct((L, D, n), F32), compiler_params=_cp("parallel"))(a_raw, dms)


def _cctx_partial(dmc, w_mod):
    L, D, n = w_mod.shape

    def body(dm_ref, w_ref, o_ref):
        part = _dg(dm_ref[...].astype(BF16), w_ref[...].astype(BF16), NT)

        @pl.when(pl.program_id(0) == 0)
        def _():
            o_ref[...] = part

        @pl.when(pl.program_id(0) > 0)
        def _():
            o_ref[...] += part

    return pl.pallas_call(
        body, grid=(L,), name="cctx_partial",
        in_specs=[pl.BlockSpec((None, 16, n), lambda l: (l, 0, 0)), pl.BlockSpec((None, D, n), lambda l: (l, 0, 0))],
        out_specs=_full((16, D)), out_shape=jax.ShapeDtypeStruct((16, D), F32),
        compiler_params=_cp("arbitrary"))(dmc, w_mod)


def _cctx_final(parts, cc):
    def body(p_ref, c_ref, o_ref):
        s = p_ref[0, 0:8, :]
        for j in range(1, N_CHIPS):
            s = s + p_ref[2 * j, 0:8, :]
        xv = c_ref[...]
        sg = _sigmoid(xv)
        o_ref[...] = s * (sg * (1.0 + xv * (1.0 - sg)))

    return pl.pallas_call(
        body, name="cctx_final", in_specs=[_full(parts.shape), _full(cc.shape)], out_specs=_full((8, LANES)),
        out_shape=jax.ShapeDtypeStruct((8, LANES), F32), compiler_params=_cp())(parts, cc)


def _me():
    return lax.axis_index("x"), lax.axis_index("y"), lax.axis_index("c")


def _flip(v, bit):
    return 1 - v if bit else v


def _remote(src, dst, ssem, rsem, peer):
    return pltpu.make_async_remote_copy(src_ref=src, dst_ref=dst, send_sem=ssem, recv_sem=rsem,
                                        device_id=peer, device_id_type=MESH_ID)


def _ag8(xb, name):
    R = xb.shape[0]

    def body(x_ref, o_ref, ssem, rsem, lsem):
        mx, my, mc = _me()
        me = 4 * mx + 2 * my + mc
        loc = pltpu.make_async_copy(x_ref, o_ref.at[me], lsem.at[0])
        loc.start()
        sends = []
        for k in range(1, N_DEV):
            peer = (_flip(mx, k & 4), _flip(my, k & 2), _flip(mc, k & 1))
            cp = _remote(x_ref, o_ref.at[me], ssem.at[k - 1], rsem.at[k - 1], peer)
            cp.start()
            sends.append((cp, peer))
        for k, (cp, peer) in enumerate(sends):
            pid = 4 * peer[0] + 2 * peer[1] + peer[2]
            _remote(x_ref, o_ref.at[pid], ssem.at[k], rsem.at[k], peer).wait_recv()
        for cp, _ in sends:
            cp.wait_send()
        loc.wait()

    return pl.pallas_call(
        body, name=name, in_specs=[ANY], out_specs=ANY, out_shape=jax.ShapeDtypeStruct((N_DEV, R, LANES), F32),
        scratch_shapes=[pltpu.SemaphoreType.DMA((N_DEV - 1,)), pltpu.SemaphoreType.DMA((N_DEV - 1,)),
                        pltpu.SemaphoreType.DMA((1,))])(xb)


def _plane_peers(mx, my, mc):
    out = []
    for k in range(1, N_CHIPS):
        px, py = _flip(mx, k & 2), _flip(my, k & 1)
        out.append(((px, py, mc), 2 * px + py))
    return out


def _chip_gather(arrs, name):
    n = len(arrs)

    def body(*refs):
        ins, outs = refs[:n], refs[n:2 * n]
        ssem, rsem, lsem = refs[2 * n:]
        mx, my, mc = _me()
        j = 2 * mx + my
        locs = [pltpu.make_async_copy(ins[a], outs[a].at[j], lsem.at[a]) for a in range(n)]
        for cp in locs:
            cp.start()
        peers = _plane_peers(mx, my, mc)
        sends = []
        for k, (peer, _) in enumerate(peers):
            for a in range(n):
                cp = _remote(ins[a], outs[a].at[j], ssem.at[k * n + a], rsem.at[k * n + a], peer)
                cp.start()
                sends.append(cp)
        for k, (peer, pj) in enumerate(peers):
            for a in range(n):
                _remote(ins[a], outs[a].at[pj], ssem.at[k * n + a], rsem.at[k * n + a], peer).wait_recv()
        for cp in sends:
            cp.wait_send()
        for cp in locs:
            cp.wait()

    return pl.pallas_call(
        body, name=name, in_specs=[ANY] * n, out_specs=[ANY] * n,
        out_shape=[jax.ShapeDtypeStruct((N_CHIPS,) + a.shape, a.dtype) for a in arrs],
        scratch_shapes=[pltpu.SemaphoreType.DMA((3 * n,)), pltpu.SemaphoreType.DMA((3 * n,)),
                        pltpu.SemaphoreType.DMA((n,))])(*arrs)


def _chip_scatter(gs, name):
    n = len(gs)

    def body(*refs):
        ins, outs = refs[:n], refs[n:2 * n]
        ssem, rsem, lsem = refs[2 * n:]
        mx, my, mc = _me()
        j = 2 * mx + my
        locs = [pltpu.make_async_copy(ins[a].at[j], outs[a].at[j], lsem.at[a]) for a in range(n)]
        for cp in locs:
            cp.start()
        peers = _plane_peers(mx, my, mc)
        sends = []
        for k, (peer, pj) in enumerate(peers):
            for a in range(n):
                cp = _remote(ins[a].at[pj], outs[a].at[j], ssem.at[k * n + a], rsem.at[k * n + a], peer)
                cp.start()
                sends.append(cp)
        for k, (peer, pj) in enumerate(peers):
            for a in range(n):
                _remote(ins[a].at[pj], outs[a].at[pj], ssem.at[k * n + a], rsem.at[k * n + a], peer).wait_recv()
        for cp in sends:
            cp.wait_send()
        for cp in locs:
            cp.wait()

    return pl.pallas_call(
        body, name=name, in_specs=[ANY] * n, out_specs=[ANY] * n,
        out_shape=[jax.ShapeDtypeStruct(g.shape, g.dtype) for g in gs],
        scratch_shapes=[pltpu.SemaphoreType.DMA((3 * n,)), pltpu.SemaphoreType.DMA((3 * n,)),
                        pltpu.SemaphoreType.DMA((n,))])(*gs)


def _sibling_swap(xs, name):
    n = len(xs)

    def body(*refs):
        ins, outs = refs[:n], refs[n:2 * n]
        ssem, rsem = refs[2 * n:]
        mx, my, mc = _me()
        cps = [_remote(ins[a], outs[a], ssem.at[a], rsem.at[a], (mx, my, 1 - mc)) for a in range(n)]
        for cp in cps:
            cp.start()
        for cp in cps:
            cp.wait()

    return pl.pallas_call(
        body, name=name, in_specs=[ANY] * n, out_specs=[ANY] * n,
        out_shape=[jax.ShapeDtypeStruct(x.shape, x.dtype) for x in xs],
        scratch_shapes=[pltpu.SemaphoreType.DMA((n,)), pltpu.SemaphoreType.DMA((n,))])(*xs)


_WEIGHTS = ("c_ctx", "w_mod", "b_mod", "norm1", "w_in", "q_gain", "k_gain", "conv_w", "sg_norm", "w_s", "b_s",
            "w_a", "w_b", "w_c", "w_o", "norm2", "w_ff1", "w_ff3", "w_ff2")
_BIG = ("w_in", "w_a", "w_b", "w_c", "w_o", "w_ff1", "w_ff3", "w_ff2")


def _constants():
    idx = np.arange(LANES)
    e = (idx[:, None] // 64 == idx[None, :] // 64).astype(np.float32) / 64.0
    c512 = np.arange(512)
    fold = (c512[:, None] % 64 == idx[None, :]).astype(np.float32)
    c256 = np.arange(SG_W)
    gsum = (c256[:, None] // 64 == idx[None, :]).astype(np.float32)
    return jnp.asarray(e, BF16), jnp.asarray(fold, F32), jnp.asarray(gsum, F32)


def _rope_tables(n_lat, n_ctx):
    t = jnp.arange(n_lat)
    inv = ROPE_THETA ** (-jnp.arange(0, HEAD_DIM // 2, 2, dtype=F32) / (HEAD_DIM // 2))
    ar = (t // GRID_W).astype(F32)[:, None] * inv
    ac = (t % GRID_W).astype(F32)[:, None] * inv
    cos = jnp.concatenate([jnp.cos(ar), jnp.cos(ar), jnp.cos(ac), jnp.cos(ac)], axis=1)
    sin = jnp.concatenate([-jnp.sin(ar), jnp.sin(ar), -jnp.sin(ac), jnp.sin(ac)], axis=1)
    cos = jnp.concatenate([cos, jnp.ones((n_ctx, HEAD_DIM), F32)], axis=0)
    sin = jnp.concatenate([sin, jnp.zeros((n_ctx, HEAD_DIM), F32)], axis=0)
    return jnp.concatenate([cos, cos], axis=1), jnp.concatenate([sin, sin], axis=1)


def kernel(x, c, ctx, c_ctx, w_mod, b_mod, norm1, w_in, q_gain, k_gain, conv_w, sg_norm, w_s, b_s, w_a, w_b, w_c, w_o, norm2, w_ff1, w_ff3, w_ff2, loss_target, m_c_ctx, m_w_mod, m_b_mod, m_norm1, m_w_in, m_q_gain, m_k_gain, m_conv_w, m_sg_norm, m_w_s, m_b_s, m_w_a, m_w_b, m_w_c, m_w_o, m_norm2, m_w_ff1, m_w_ff3, m_w_ff2, v_c_ctx, v_w_mod, v_b_mod, v_norm1, v_w_in, v_q_gain, v_k_gain, v_conv_w, v_sg_norm, v_w_s, v_b_s, v_w_a, v_w_b, v_w_c, v_w_o, v_norm2, v_w_ff1, v_w_ff3, v_w_ff2):
    given = dict(locals())
    mx, my, mc = _me()
    chip = 2 * mx + my
    dev = 4 * mx + 2 * my + mc
    L = norm1.shape[0]
    S, Lc = x.shape[1], ctx.shape[1]
    T = S + Lc
    D = D_MODEL
    n_mod, n_in, n_ff = w_mod.shape[2], w_in.shape[2], w_ff1.shape[2]
    n_cw = conv_w.shape[2]
    e_avg, fold, gsum = _constants()
    cos_t, sin_t = _rope_tables(S, Lc)

    cw_rows = (L * 3 * n_cw) // LANES
    pad = (-(8 + cw_rows)) % 8
    buf = jnp.concatenate([c.reshape(8, LANES), conv_w.reshape(cw_rows, LANES), jnp.zeros((pad, LANES), F32)], axis=0)
    g1 = _ag8(buf, "gather_cond")
    conds = g1[:, :8].reshape(N_DEV, D)
    cw_full = jnp.stack([g1[2 * j, 8:8 + cw_rows].reshape(L, 3, n_cw) for j in range(N_CHIPS)], axis=2)
    cw_full = cw_full.reshape(L, 3, N_CHIPS * n_cw)
    cw8 = jnp.pad(cw_full, ((0, 0), (0, 5), (0, 0)))
    a_raw = jnp.concatenate([conds, c_ctx[None], jnp.zeros((7, D), F32)], axis=0)
    bsh = lax.dynamic_slice_in_dim(b_mod, chip * n_mod, n_mod, axis=1)[:, None, :]
    mod_sh = _mod_fwd(a_raw, w_mod, bsh)
    g2 = _ag8(mod_sh.reshape(-1, LANES), "gather_mod")
    mods = jnp.stack([g2[2 * j].reshape(L, 16, n_mod) for j in range(N_CHIPS)], axis=2).reshape(L, 16, N_CHIPS * n_mod)
    lat = lax.dynamic_index_in_dim(mods, dev, axis=1, keepdims=False)
    mod = jnp.stack([lat.reshape(L, 6, D), mods[:, 8].reshape(L, 6, D)], axis=1)
    mod = jnp.pad(mod, ((0, 0), (0, 0), (0, 2), (0, 0)))

    qg = jnp.tile(q_gain, (1, N_Q_HEADS))[:, None, :]
    kg = jnp.tile(k_gain, (1, N_KV_HEADS))[:, None, :]
    sgn = sg_norm[:, None, :]
    ws_b = w_s.astype(BF16)
    zero = jnp.zeros_like(ws_b)
    bd = jnp.concatenate([jnp.concatenate([ws_b, zero], axis=3), jnp.concatenate([zero, ws_b], axis=3)], axis=2)
    bdt = jnp.swapaxes(bd, 2, 3)
    bias = jnp.tile(jnp.repeat(jnp.swapaxes(b_s, 1, 2), SG_W // 4, axis=2), (1, 2, 1))

    def gathered(l):
        arrs = [given[nm][l].astype(BF16) for nm in _BIG]
        win, wa, wb, wc, wo, w1, w3, w2 = _chip_gather(arrs, "gather_weights")
        return win, wa, wb, wc, wo.reshape(1, D, D), w1, w3, w2

    def layer_fwd(X, l, W):
        win, wa, wb, wc, wo, w1, w3, w2 = W
        h = _norm_mod(X, norm1[l][None], mod[l], 0, 1, S)
        p = _mm_nn(h, win, F32, "in_proj")
        ya = _conv_fwd(p, cw8[l], S)
        q, k, v = _qkv_prep(p, cos_t, sin_t, qg[l], kg[l], e_avg)
        at, qa = _flash_fwd(q.reshape(N_KV_HEADS, GROUP, T, LANES), k, v, S)
        yc = _gmlp_fwd(p, sgn[l], bd[l], bias[l])
        mg = _merge_fwd(ya, at, yc, p, wa, wb, wc)
        X1, f1 = _mm_res(mg[None], wo, X, mod[l], 2, S, "out_proj")
        h2 = _norm_mod(X1, norm2[l][None], mod[l], 3, 4, S)
        a1, a3, act = _ffn_up(h2, w1, w3)
        X2, f2 = _mm_res(act, w2, X1, mod[l], 5, S, "ffn_down")
        return X2, dict(X=X, h=h, p=p, ya=ya, k=k, v=v, at=at, qa=qa, yc=yc, mg=mg, X1=X1, f1=f1, h2=h2,
                        a1=a1, a3=a3, act=act, f2=f2)

    def layer_bwd(dX2, l, W, sv):
        win, wa, wb, wc, wo, w1, w3, w2 = W
        dyf, dgt2 = _gate_bwd(dX2, sv["f2"], mod[l], 5, S)
        da1, da3 = _ffn_down_bwd(dyf, w2, sv["a1"], sv["a3"])
        dw2 = _mm_tn(sv["act"], dyf, _shard_rows(n_ff), _rows(D), N_CHIPS, n_ff, D, T, "dw_ff2")
        dh2 = _mm_nt_acc([da1, da3], [w1, w3], False, "ffn_up_bwd")
        dw1 = _mm_tn(sv["h2"], da1, _rows(D), _shard_rows(n_ff), N_CHIPS, D, n_ff, T, "dw_ff1")
        dw3 = _mm_tn(sv["h2"], da3, _rows(D), _shard_rows(n_ff), N_CHIPS, D, n_ff, T, "dw_ff3")
        dX1, dn2, dsh2, dsc2 = _norm_mod_bwd(sv["X1"], dh2, dX2, norm2[l][None], mod[l], 4, S)
        dyo, dgt1 = _gate_bwd(dX1, sv["f1"], mod[l], 2, S)
        dwo = _mm_tn(sv["mg"], dyo, _rows(D), _rows(D), 1, D, D, T, "dw_o")
        dp, dya, doa, dyc, dwa, dwb, dwc = _merge_bwd(dyo, sv["ya"], sv["at"], sv["yc"], sv["p"], wa, wb, wc, wo[0])
        dp, dcw = _conv_bwd(dp, dya, sv["p"], cw8[l], S)
        dp, dsg, dws, dbs = _gmlp_bwd(dp, dyc, sv["p"], sgn[l], bd[l], bdt[l], bias[l], gsum)
        dq, dk, dv = _flash_bwd(sv["qa"], doa.reshape(N_KV_HEADS, GROUP, T, LANES), sv["k"], sv["v"], S)
        dp, dqg, dkg = _qkv_prep_bwd(dp, dq.reshape(N_Q_HEADS, T, LANES), dk, dv, sv["p"], cos_t, sin_t,
                                     qg[l], kg[l], e_avg, fold)
        dh = _mm_nt_acc([dp], [win], True, "in_proj_bwd")
        dwin = _mm_tn(sv["h"], dp, _rows(D), _row_cols(n_in), N_CHIPS, D, n_in, T, "dw_in")
        dX0, dn1, dsh1, dsc1 = _norm_mod_bwd(sv["X"], dh, dX1, norm1[l][None], mod[l], 1, S)
        dmod = jnp.concatenate([dsh1, dsc1, dgt1, dsh2, dsc2, dgt2], axis=1)
        big = [dwin, dwa, dwb, dwc, dwo.reshape(N_CHIPS, D // N_CHIPS, D), dw1, dw3, dw2]
        small = dict(norm1=dn1[0], norm2=dn2[0], q_gain=dqg[0, :HEAD_DIM], k_gain=dkg[0, :HEAD_DIM],
                     conv_w=dcw[:3], sg_norm=dsg[0], w_s=dws, b_s=jnp.swapaxes(dbs[:, :4], 0, 1), dmod=dmod)
        return dX0, big, small

    X = jnp.concatenate([x[0], ctx[0]], axis=0)
    Ws, saved = [], []
    for l in range(L):
        Ws.append(gathered(l))
        X, sv = layer_fwd(X, l, Ws[l])
        saved.append(sv)
    dX, lpart = _loss_grad(X, loss_target[0], S)
    loss = lax.psum(lpart[0, 0], ("x", "y", "c"))

    out = {nm: [None] * L for nm in _BIG}
    smalls = [None] * L
    for l in reversed(range(L)):
        dX, big, smalls[l] = layer_bwd(dX, l, Ws[l], saved[l])
        recv = _chip_scatter([g.astype(BF16) for g in big], "scatter_grads")
        mine = [_sum_lead(r.reshape(N_CHIPS, r.shape[1], r.shape[2]), "sum_chips") for r in recv]
        theirs = _sibling_swap(mine, "swap_planes")
        for nm, g1_, g2_ in zip(_BIG, mine, theirs):
            out[nm][l] = _adamw(given[nm][l], given["m_" + nm][l], given["v_" + nm][l], g1_, g2_)
    grad_x = dX[:S][None]

    def flat(nm):
        return jnp.stack([smalls[l][nm] for l in range(L)]).reshape(-1)

    dmod_all = jnp.stack([smalls[l]["dmod"] for l in range(L)])
    dml = dmod_all[:, 0].reshape(-1)
    dmc = dmod_all[:, 1].reshape(-1)
    names = ("norm1", "q_gain", "k_gain", "conv_w", "sg_norm", "w_s", "b_s", "norm2")
    parts = [dml, dml + dmc, dmc] + [flat(nm) for nm in names]
    sizes = [int(a.shape[0]) for a in parts]
    total = sum(sizes)
    padn = (-total) % (8 * LANES)
    sbuf = jnp.concatenate(parts + [jnp.zeros((padn,), F32)]).reshape(-1, LANES)
    g3 = _ag8(sbuf, "gather_small")
    ssum = _sum_lead(g3, "sum_devices").reshape(-1)
    offs = np.cumsum([0] + sizes)
    seg = {nm: ssum[offs[i + 3]:offs[i + 4]] for i, nm in enumerate(names)}
    gb_mod = ssum[offs[1]:offs[2]].reshape(L, N_CHIPS * n_mod)
    dmc_sum = ssum[offs[2]:offs[3]].reshape(L, N_CHIPS * n_mod)
    dml_all = g3.reshape(N_DEV, -1)[:, :sizes[0]].reshape(N_DEV, L, N_CHIPS * n_mod)
    dml_sh = jnp.swapaxes(lax.dynamic_slice_in_dim(dml_all, chip * n_mod, n_mod, axis=2), 0, 1)
    dmc_sh = lax.dynamic_slice_in_dim(dmc_sum, chip * n_mod, n_mod, axis=1)[:, None, :]
    dms = jnp.concatenate([dml_sh, dmc_sh, jnp.zeros((L, 7, n_mod), F32)], axis=1)
    g_wmod = _wmod_grad(a_raw, dms)
    part = _cctx_partial(jnp.concatenate([dmc_sh, jnp.zeros((L, 15, n_mod), F32)], axis=1), w_mod)
    g4 = _ag8(part.reshape(-1, LANES), "gather_cctx")
    g_cctx = _cctx_final(g4, c_ctx.reshape(8, LANES)).reshape(D)

    g_conv = lax.dynamic_slice_in_dim(seg["conv_w"].reshape(L, 3, N_CHIPS * n_cw), chip * n_cw, n_cw, axis=2)
    small_g = dict(c_ctx=g_cctx, w_mod=g_wmod, b_mod=gb_mod, norm1=seg["norm1"].reshape(norm1.shape),
                   q_gain=seg["q_gain"].reshape(q_gain.shape), k_gain=seg["k_gain"].reshape(k_gain.shape),
                   conv_w=g_conv, sg_norm=seg["sg_norm"].reshape(sg_norm.shape), w_s=seg["w_s"].reshape(w_s.shape),
                   b_s=seg["b_s"].reshape(b_s.shape), norm2=seg["norm2"].reshape(norm2.shape))
    res = {}
    for nm in _WEIGHTS:
        if nm in _BIG:
            res[nm] = [jnp.stack([out[nm][l][k] for l in range(L)]) for k in range(4)]
        else:
            res[nm] = _adamw(given[nm], given["m_" + nm], given["v_" + nm], small_g[nm])
    return (loss, grad_x, *[res[nm][0] for nm in _WEIGHTS], *[res[nm][1] for nm in _WEIGHTS],
            *[res[nm][2] for nm in _WEIGHTS], *[res[nm][3] for nm in _WEIGHTS])
```

```python
import functools

import jax
import jax.numpy as jnp
import numpy as np
from jax import lax
from jax.experimental import pallas as pl
from jax.experimental.pallas import tpu as pltpu

F32 = jnp.float32
BF16 = jnp.bfloat16
EPS = 1e-6
D_MODEL = 1024
HEAD_DIM = 64
N_Q_HEADS = 8
N_KV_HEADS = 2
GROUP = N_Q_HEADS // N_KV_HEADS
GRID_W = 64
ROPE_THETA = 10000.0
CHUNK = 128
CONV_W = 256
SG_W = 256
OFF_Q = 3 * CONV_W
QKV_W = 768
OFF_U = OFF_Q + QKV_W
OFF_G = OFF_U + 2 * SG_W
IN_W = OFF_G + 3 * D_MODEL
N_CHIPS = 4
N_DEV = 8
LANES = 128
UNROLL = 4
AUG = 3
ADAM_LR, ADAM_B1, ADAM_B2, ADAM_EPS, ADAM_WD, ADAM_STEP = 0.001, 0.9, 0.999, 1e-8, 0.01, 10
VMEM_LIMIT_V7X = 52 * 1024 * 1024
MESH_ID = pl.DeviceIdType.MESH
NT = (((1,), (1,)), ((), ()))
TN = (((0,), (0,)), ((), ()))
ANY = pl.BlockSpec(memory_space=pl.ANY)


def _cp(*sem):
    return pltpu.CompilerParams(dimension_semantics=sem or None, vmem_limit_bytes=VMEM_LIMIT_V7X)


def _tile(n, target, mult=16):
    best = None
    for t in range(mult, n + 1, mult):
        if n % t == 0 and t <= target:
            best = t
    assert best is not None, (n, target, mult)
    return best


def _full(shape):
    nd = len(shape)
    return pl.BlockSpec(tuple(shape), lambda *_: (0,) * nd)


def _segments(i, tm, n_lat, fn):
    k, off = divmod(n_lat, tm)

    @pl.when(i < k)
    def _():
        fn(0, tm, 0)

    @pl.when(i == k)
    def _():
        if off:
            fn(0, off, 0)
        fn(off, tm, 1)

    @pl.when(i > k)
    def _():
        fn(0, tm, 1)


def _dot(a, b):
    return jnp.dot(a, b, preferred_element_type=F32)


def _dg(a, b, dims):
    return lax.dot_general(a, b, dims, preferred_element_type=F32)


def _split3(x):
    hi = x.astype(BF16)
    r1 = x - hi.astype(F32)
    mid = r1.astype(BF16)
    lo = (r1 - mid.astype(F32)).astype(BF16)
    return hi.astype(F32), mid.astype(F32), lo.astype(F32)


def _lane(shape):
    return lax.broadcasted_iota(jnp.int32, shape, len(shape) - 1)


def _aug(val, stat):
    lane = _lane(val.shape)
    hi, mid, lo = _split3(stat)
    ext = jnp.where(lane == 64, hi, jnp.where(lane == 65, mid, jnp.where(lane == 66, lo, 0.0)))
    return jnp.where(lane < 64, val, ext)


def _seg_mean(x, e):
    outs = []
    for g in range(x.shape[1] // LANES):
        blk = x[:, g * LANES:(g + 1) * LANES]
        hi = blk.astype(BF16)
        lo = (blk - hi.astype(F32)).astype(BF16)
        outs.append(_dot(hi, e) + _dot(lo, e))
    return outs[0] if len(outs) == 1 else jnp.concatenate(outs, axis=1)


def _rope(x, cos, sin_signed, inverse):
    w = x.shape[1]
    reps = w // LANES
    c = cos if reps == 1 else jnp.tile(cos, (1, reps))
    s = sin_signed if reps == 1 else jnp.tile(sin_signed, (1, reps))
    first = (_lane(x.shape) % 32) < 16
    partner = jnp.where(first, pltpu.roll(x, w - 16, 1), pltpu.roll(x, 16, 1))
    return x * c - partner * s if inverse else x * c + partner * s


def _sigmoid(x):
    return 1.0 / (1.0 + jnp.exp(-x))


_GELU_K = 0.7978845608028654
_GELU_C = 0.044715


def _gelu(x):
    return 0.5 * x * (1.0 + jnp.tanh(_GELU_K * (x + _GELU_C * x * x * x)))


def _gelu_grad(x):
    t = jnp.tanh(_GELU_K * (x + _GELU_C * x * x * x))
    return 0.5 * (1.0 + t) + 0.5 * x * (1.0 - t * t) * _GELU_K * (1.0 + 3.0 * _GELU_C * x * x)


def _loop_unrolled(n, step, init):
    def trip(t, carry):
        for u in range(UNROLL):
            carry = step(t * UNROLL + u, carry)
        return carry

    carry = lax.fori_loop(0, n // UNROLL, trip, init) if n >= UNROLL else init
    for r in range(n - n % UNROLL, n):
        carry = step(r, carry)
    return carry


def _heads_to_rows(x, n_heads):
    out = []
    for h in range(n_heads):
        grp = x[:, (h // 2) * LANES:(h // 2 + 1) * LANES]
        out.append(grp if h % 2 == 0 else pltpu.roll(grp, 64, 1))
    return out


def _rows_to_heads(blocks):
    outs = []
    lane = _lane(blocks[0].shape)
    for a in range(len(blocks) // 2):
        outs.append(jnp.where(lane < 64, blocks[2 * a], pltpu.roll(blocks[2 * a + 1], 64, 1)))
    return outs[0] if len(outs) == 1 else jnp.concatenate(outs, axis=1)


def _norm_mod(x, g, mod, i_shift, i_scale, n_lat):
    T, D = x.shape
    tm = _tile(T, 528)

    def body(x_ref, g_ref, mod_ref, h_ref):
        def fn(r0, r1, seg):
            xv = x_ref[r0:r1, :]
            r = lax.rsqrt(jnp.mean(xv * xv, axis=-1, keepdims=True) + EPS)
            n = xv * r * g_ref[...]
            h = n * (1.0 + mod_ref[seg, i_scale:i_scale + 1, :]) + mod_ref[seg, i_shift:i_shift + 1, :]
            h_ref[r0:r1, :] = h.astype(BF16)

        _segments(pl.program_id(0), tm, n_lat, fn)

    return pl.pallas_call(
        body, grid=(T // tm,), name="norm_mod",
        in_specs=[pl.BlockSpec((tm, D), lambda i: (i, 0)), _full(g.shape), _full(mod.shape)],
        out_specs=pl.BlockSpec((tm, D), lambda i: (i, 0)),
        out_shape=jax.ShapeDtypeStruct((T, D), BF16), compiler_params=_cp("parallel"))(x, g, mod)


def _norm_mod_bwd(x, dh, dres, g, mod, i_scale, n_lat):
    T, D = x.shape
    tm = _tile(T, 528)

    def body(x_ref, dh_ref, dres_ref, g_ref, mod_ref, dx_ref, dg_ref, dsh_ref, dsc_ref):
        i = pl.program_id(0)

        @pl.when(i == 0)
        def _():
            dg_ref[...] = jnp.zeros_like(dg_ref)
            dsh_ref[...] = jnp.zeros_like(dsh_ref)
            dsc_ref[...] = jnp.zeros_like(dsc_ref)

        def fn(r0, r1, seg):
            xv = x_ref[r0:r1, :]
            dh = dh_ref[r0:r1, :]
            r = lax.rsqrt(jnp.mean(xv * xv, axis=-1, keepdims=True) + EPS)
            xh = xv * r
            gv = g_ref[...]
            dsh_ref[seg] += jnp.sum(dh, axis=0, keepdims=True)
            dsc_ref[seg] += jnp.sum(dh * (xh * gv), axis=0, keepdims=True)
            dn = dh * (1.0 + mod_ref[seg, i_scale:i_scale + 1, :])
            dg_ref[...] += jnp.sum(dn * xh, axis=0, keepdims=True)
            gd = gv * dn
            dx_ref[r0:r1, :] = dres_ref[r0:r1, :] + r * (gd - xh * jnp.mean(xh * gd, axis=-1, keepdims=True))

        _segments(i, tm, n_lat, fn)

    row = pl.BlockSpec((tm, D), lambda i: (i, 0))
    return pl.pallas_call(
        body, grid=(T // tm,), name="norm_mod_bwd",
        in_specs=[row, row, row, _full(g.shape), _full(mod.shape)],
        out_specs=[row, _full((1, D)), _full((2, 1, D)), _full((2, 1, D))],
        out_shape=[jax.ShapeDtypeStruct((T, D), F32), jax.ShapeDtypeStruct((1, D), F32),
                   jax.ShapeDtypeStruct((2, 1, D), F32), jax.ShapeDtypeStruct((2, 1, D), F32)],
        compiler_params=_cp("arbitrary"))(x, dh, dres, g, mod)


def _gate_bwd(dx, f, mod, i_gate, n_lat):
    T, D = dx.shape
    tm = _tile(T, 528)

    def body(dx_ref, f_ref, mod_ref, dy_ref, dg_ref):
        i = pl.program_id(0)

        @pl.when(i == 0)
        def _():
            dg_ref[...] = jnp.zeros_like(dg_ref)

        def fn(r0, r1, seg):
            dxv = dx_ref[r0:r1, :]
            dy_ref[r0:r1, :] = (dxv * mod_ref[seg, i_gate:i_gate + 1, :]).astype(BF16)
            dg_ref[seg] += jnp.sum(dxv * f_ref[r0:r1, :], axis=0, keepdims=True)

        _segments(i, tm, n_lat, fn)

    row = pl.BlockSpec((tm, D), lambda i: (i, 0))
    return pl.pallas_call(
        body, grid=(T // tm,), name="gate_bwd",
        in_specs=[row, row, _full(mod.shape)], out_specs=[row, _full((2, 1, D))],
        out_shape=[jax.ShapeDtypeStruct((T, D), BF16), jax.ShapeDtypeStruct((2, 1, D), F32)],
        compiler_params=_cp("arbitrary"))(dx, f, mod)


def _mm_nn(a, w, out_dtype, name):
    M, K = a.shape
    J, _, n = w.shape
    tm = _tile(M, 1056)

    def body(a_ref, w_ref, o_ref):
        o_ref[...] = _dot(a_ref[...], w_ref[...]).astype(o_ref.dtype)

    return pl.pallas_call(
        body, grid=(M // tm, J), name=name,
        in_specs=[pl.BlockSpec((tm, K), lambda i, j: (i, 0)), pl.BlockSpec((None, K, n), lambda i, j: (j, 0, 0))],
        out_specs=pl.BlockSpec((tm, n), lambda i, j: (i, j)),
        out_shape=jax.ShapeDtypeStruct((M, J * n), out_dtype), compiler_params=_cp("parallel", "arbitrary"))(a, w)


def _mm_res(a3, w, res, mod, i_gate, n_lat, name):
    J, M, k = a3.shape
    N = w.shape[2]
    tm = _tile(M, 528)

    def body(a_ref, w_ref, res_ref, mod_ref, x_ref, f_ref):
        acc = _dot(a_ref[0], w_ref[0])
        for j in range(1, J):
            acc += _dot(a_ref[j], w_ref[j])
        f_ref[...] = acc

        def fn(r0, r1, seg):
            x_ref[r0:r1, :] = res_ref[r0:r1, :] + mod_ref[seg, i_gate:i_gate + 1, :] * f_ref[r0:r1, :]

        _segments(pl.program_id(0), tm, n_lat, fn)

    row = pl.BlockSpec((tm, N), lambda i: (i, 0))
    return pl.pallas_call(
        body, grid=(M // tm,), name=name,
        in_specs=[pl.BlockSpec((J, tm, k), lambda i: (0, i, 0)), _full(w.shape), row, _full(mod.shape)],
        out_specs=[row, row],
        out_shape=[jax.ShapeDtypeStruct((M, N), F32), jax.ShapeDtypeStruct((M, N), F32)],
        compiler_params=_cp("parallel"))(a3, w, res, mod)


def _mm_nt_acc(dys, ws, row_major, name):
    J, K, n = ws[0].shape
    M = dys[0].shape[0] if row_major else dys[0].shape[1]
    tm = _tile(M, 1056)
    P = len(dys)

    def body(*refs):
        o_ref = refs[2 * P]
        j = pl.program_id(1)
        part = _dg(refs[0][...], refs[P][...], NT)
        for p in range(1, P):
            part += _dg(refs[p][...], refs[P + p][...], NT)

        @pl.when(j == 0)
        def _():
            o_ref[...] = part

        @pl.when(j > 0)
        def _():
            o_ref[...] += part

    dy_spec = (pl.BlockSpec((tm, n), lambda i, j: (i, j)) if row_major
               else pl.BlockSpec((None, tm, n), lambda i, j: (j, i, 0)))
    w_spec = pl.BlockSpec((None, K, n), lambda i, j: (j, 0, 0))
    return pl.pallas_call(
        body, grid=(M // tm, J), name=name,
        in_specs=[dy_spec] * P + [w_spec] * P,
        out_specs=pl.BlockSpec((tm, K), lambda i, j: (i, 0)),
        out_shape=jax.ShapeDtypeStruct((M, K), F32), compiler_params=_cp("parallel", "arbitrary"))(*dys, *ws)


def _mm_tn(x, dy, x_spec, dy_spec, J, K, n, T, name):
    tk = _tile(T, 1056)

    def body(x_ref, dy_ref, o_ref):
        t = pl.program_id(1)
        part = _dg(x_ref[...], dy_ref[...], TN)

        @pl.when(t == 0)
        def _():
            o_ref[...] = part

        @pl.when(t > 0)
        def _():
            o_ref[...] += part

    return pl.pallas_call(
        body, grid=(J, T // tk), name=name,
        in_specs=[x_spec(tk), dy_spec(tk)],
        out_specs=pl.BlockSpec((None, K, n), lambda j, t: (j, 0, 0)),
        out_shape=jax.ShapeDtypeStruct((J, K, n), F32), compiler_params=_cp("parallel", "arbitrary"))(x, dy)


def _rows(width):
    return lambda tk: pl.BlockSpec((tk, width), lambda j, t: (t, 0))


def _row_cols(width):
    return lambda tk: pl.BlockSpec((tk, width), lambda j, t: (t, j))


def _shard_rows(width):
    return lambda tk: pl.BlockSpec((None, tk, width), lambda j, t: (j, t, 0))


def _ffn_up(h, w1, w3):
    T, D = h.shape
    J, _, n = w1.shape
    tm = _tile(T, 1056)

    def body(h_ref, w1_ref, w3_ref, a1_ref, a3_ref, act_ref):
        hv = h_ref[...]
        a1 = _dot(hv, w1_ref[...])
        a3 = _dot(hv, w3_ref[...])
        a1_ref[...] = a1
        a3_ref[...] = a3
        act_ref[...] = (a1 * _sigmoid(a1) * a3).astype(BF16)

    w_spec = pl.BlockSpec((None, D, n), lambda i, j: (j, 0, 0))
    o_spec = pl.BlockSpec((None, tm, n), lambda i, j: (j, i, 0))
    return pl.pallas_call(
        body, grid=(T // tm, J), name="ffn_up",
        in_specs=[pl.BlockSpec((tm, D), lambda i, j: (i, 0)), w_spec, w_spec], out_specs=[o_spec] * 3,
        out_shape=[jax.ShapeDtypeStruct((J, T, n), F32)] * 2 + [jax.ShapeDtypeStruct((J, T, n), BF16)],
        compiler_params=_cp("parallel", "arbitrary"))(h, w1, w3)


def _ffn_down_bwd(dy, w2, a1, a3):
    T, D = dy.shape
    J, n, _ = w2.shape
    tm = _tile(T, 1056)

    def body(dy_ref, w2_ref, a1_ref, a3_ref, da1_ref, da3_ref):
        dact = _dg(dy_ref[...], w2_ref[...], NT)
        a1v = a1_ref[...]
        sig = _sigmoid(a1v)
        da3_ref[...] = (dact * a1v * sig).astype(BF16)
        da1_ref[...] = (dact * a3_ref[...] * (sig * (1.0 + a1v * (1.0 - sig)))).astype(BF16)

    a_spec = pl.BlockSpec((None, tm, n), lambda i, j: (j, i, 0))
    return pl.pallas_call(
        body, grid=(T // tm, J), name="ffn_down_bwd",
        in_specs=[pl.BlockSpec((tm, D), lambda i, j: (i, 0)), pl.BlockSpec((None, n, D), lambda i, j: (j, 0, 0)),
                  a_spec, a_spec],
        out_specs=[a_spec, a_spec], out_shape=[jax.ShapeDtypeStruct((J, T, n), BF16)] * 2,
        compiler_params=_cp("parallel", "arbitrary"))(dy, w2, a1, a3)


def _qkv_prep(p, cos, sin, qg, kg, e):
    T = p.shape[0]
    tm = _tile(T, 528)

    def body(p_ref, cos_ref, sin_ref, qg_ref, kg_ref, e_ref, q_ref, k_ref, v_ref):
        ev = e_ref[...]
        cv, sv = cos_ref[...], sin_ref[...]
        xq = p_ref[:, 0:512]
        qn = xq * lax.rsqrt(_seg_mean(xq * xq, ev) + EPS) * qg_ref[...]
        qr = _rope(qn, cv, sv, False) * (HEAD_DIM ** -0.5)
        xk = p_ref[:, 512:640]
        kn = xk * lax.rsqrt(_seg_mean(xk * xk, ev) + EPS) * kg_ref[...]
        kr = _rope(kn, cv, sv, False)
        lane = _lane((tm, LANES))
        ones = jnp.where(lane < 64 + AUG, -1.0, 0.0)
        for h, blk in enumerate(_heads_to_rows(qr, N_Q_HEADS)):
            q_ref[h] = jnp.where(lane < 64, blk, 0.0).astype(BF16)
        for h, blk in enumerate(_heads_to_rows(kr, N_KV_HEADS)):
            k_ref[h] = jnp.where(lane < 64, blk, ones).astype(BF16)
        for h, blk in enumerate(_heads_to_rows(p_ref[:, 640:768], N_KV_HEADS)):
            v_ref[h] = jnp.where(lane < 64, blk, ones).astype(BF16)

    tab = pl.BlockSpec((tm, LANES), lambda i: (i, 0))
    return pl.pallas_call(
        body, grid=(T // tm,), name="qkv_prep",
        in_specs=[pl.BlockSpec((tm, QKV_W), lambda i: (i, 1)), tab, tab, _full(qg.shape), _full(kg.shape),
                  _full(e.shape)],
        out_specs=[pl.BlockSpec((N_Q_HEADS, tm, LANES), lambda i: (0, i, 0)),
                   pl.BlockSpec((N_KV_HEADS, tm, LANES), lambda i: (0, i, 0)),
                   pl.BlockSpec((N_KV_HEADS, tm, LANES), lambda i: (0, i, 0))],
        out_shape=[jax.ShapeDtypeStruct((N_Q_HEADS, T, LANES), BF16),
                   jax.ShapeDtypeStruct((N_KV_HEADS, T, LANES), BF16),
                   jax.ShapeDtypeStruct((N_KV_HEADS, T, LANES), BF16)],
        compiler_params=_cp("parallel"))(p, cos, sin, qg, kg, e)


def _qkv_prep_bwd(dp, dq, dk, dv, p, cos, sin, qg, kg, e, fold):
    T = p.shape[0]
    tm = _tile(T, 528)
    nt = T // tm

    def body(dp_in, dq_ref, dk_ref, dv_ref, p_ref, cos_ref, sin_ref, qg_ref, kg_ref, e_ref, fold_ref,
             dp_ref, dqg_ref, dkg_ref, accq, acck):
        del dp_in
        i = pl.program_id(0)

        @pl.when(i == 0)
        def _():
            accq[...] = jnp.zeros_like(accq)
            acck[...] = jnp.zeros_like(acck)

        ev = e_ref[...]
        cv, sv = cos_ref[...], sin_ref[...]

        def one(x, dr, gain, acc):
            r = lax.rsqrt(_seg_mean(x * x, ev) + EPS)
            xh = x * r
            dn = _rope(dr, cv, sv, True)
            acc[0:1, :] += jnp.sum(dn * xh, axis=0, keepdims=True)
            gd = gain * dn
            return r * (gd - xh * _seg_mean(xh * gd, ev))

        dqr = _rows_to_heads([dq_ref[h] for h in range(N_Q_HEADS)]) * (HEAD_DIM ** -0.5)
        dkr = _rows_to_heads([dk_ref[h] for h in range(N_KV_HEADS)])
        dvv = _rows_to_heads([dv_ref[h] for h in range(N_KV_HEADS)])
        dp_ref[:, 0:512] = one(p_ref[:, 0:512], dqr, qg_ref[...], accq).astype(BF16)
        dp_ref[:, 512:640] = one(p_ref[:, 512:640], dkr, kg_ref[...], acck).astype(BF16)
        dp_ref[:, 640:768] = dvv.astype(BF16)

        @pl.when(i == nt - 1)
        def _():
            fv = fold_ref[...]
            dqg_ref[...] = jnp.dot(accq[...], fv, preferred_element_type=F32, precision=lax.Precision.HIGHEST)
            dkg_ref[...] = jnp.dot(acck[...], fv[0:LANES, :], preferred_element_type=F32,
                                   precision=lax.Precision.HIGHEST)

    tab = pl.BlockSpec((tm, LANES), lambda i: (i, 0))
    sec = pl.BlockSpec((tm, QKV_W), lambda i: (i, 1))
    return pl.pallas_call(
        body, grid=(nt,), name="qkv_prep_bwd",
        in_specs=[ANY, pl.BlockSpec((N_Q_HEADS, tm, LANES), lambda i: (0, i, 0)),
                  pl.BlockSpec((N_KV_HEADS, tm, LANES), lambda i: (0, i, 0)),
                  pl.BlockSpec((N_KV_HEADS, tm, LANES), lambda i: (0, i, 0)),
                  sec, tab, tab, _full(qg.shape), _full(kg.shape), _full(e.shape), _full(fold.shape)],
        out_specs=[sec, _full((8, LANES)), _full((8, LANES))],
        out_shape=[jax.ShapeDtypeStruct(dp.shape, BF16), jax.ShapeDtypeStruct((8, LANES), F32),
                   jax.ShapeDtypeStruct((8, LANES), F32)],
        scratch_shapes=[pltpu.VMEM((8, 512), F32), pltpu.VMEM((8, LANES), F32)],
        input_output_aliases={0: 0}, compiler_params=_cp("arbitrary"))(dp, dq, dk, dv, p, cos, sin, qg, kg, e, fold)


def _flash_fwd(q, k, v, n_lat):
    _, _, T, _ = q.shape
    tq = tk = 256
    nq = T // tq
    M = GROUP * tq

    nk_lat = n_lat // tk

    def body(q_ref, k_ref, v_ref, o_ref, qa_ref):
        i = pl.program_id(0)
        qv = q_ref[...].reshape(M, LANES)

        def step(s, carry):
            m, acc = carry
            r0 = s * tk if isinstance(s, int) else pl.multiple_of(s * tk, tk)
            sc = _dg(qv, k_ref[pl.ds(r0, tk), :], NT)
            m_new = jnp.maximum(m, jnp.max(sc, axis=1, keepdims=True))
            pr = jnp.exp(sc - m_new)
            acc = jnp.exp(m - m_new) * acc + _dot(pr.astype(BF16), v_ref[pl.ds(r0, tk), :])
            return m_new, acc

        def finish(m, acc):
            den = -acc[:, 64:65]
            out = acc / den
            o_ref[...] = _rows_to_heads([out[g * tq:(g + 1) * tq] for g in range(GROUP)]).astype(BF16)
            qa_ref[...] = _aug(qv.astype(F32), m + jnp.log(den)).astype(BF16).reshape(GROUP, tq, LANES)

        init = (jnp.full((M, 1), -1e30, F32), jnp.zeros((M, LANES), F32))

        @pl.when(i < n_lat // tq)
        def _():
            carry = _loop_unrolled(nk_lat, step, init)
            for s in range(nk_lat, T // tk):
                carry = step(s, carry)
            finish(*carry)

        @pl.when(i >= n_lat // tq)
        def _():
            carry = init
            for s in range(nk_lat, T // tk):
                carry = step(s, carry)
            finish(*carry)

    q_spec = pl.BlockSpec((None, GROUP, tq, LANES), lambda i, h: (h, 0, i, 0))
    kv_spec = pl.BlockSpec((None, T, LANES), lambda i, h: (h, 0, 0))
    return pl.pallas_call(
        body, grid=(nq, N_KV_HEADS), name="flash_fwd",
        in_specs=[q_spec, kv_spec, kv_spec],
        out_specs=[pl.BlockSpec((tq, GROUP * HEAD_DIM), lambda i, h: (i, h)), q_spec],
        out_shape=[jax.ShapeDtypeStruct((T, N_Q_HEADS * HEAD_DIM), BF16), jax.ShapeDtypeStruct(q.shape, BF16)],
        compiler_params=_cp("parallel", "arbitrary"))(q, k, v)


def _flash_bwd(qa, doa, k, v, n_lat):
    _, _, T, _ = qa.shape
    tq = tk = 256
    nkv = T // tk
    M = GROUP * tq

    def body(qa_hbm, doa_hbm, k_ref, v_ref, dq_hbm, dk_ref, dv_ref, q_sc, do_sc, dq_sc, sems):
        h = pl.program_id(0)
        j = pl.program_id(1)

        @pl.when(j == 0)
        def _():
            c1 = pltpu.make_async_copy(qa_hbm.at[h], q_sc, sems.at[0])
            c2 = pltpu.make_async_copy(doa_hbm.at[h], do_sc, sems.at[1])
            c1.start()
            c2.start()
            dq_sc[...] = jnp.zeros_like(dq_sc)
            c1.wait()
            c2.wait()

        kb = k_ref[...]
        vb = v_ref[...]

        def step(i, carry):
            dk, dv = carry
            r0 = i * tq if isinstance(i, int) else pl.multiple_of(i * tq, tq)
            qv = q_sc[:, pl.ds(r0, tq), :].reshape(M, LANES)
            dov = do_sc[:, pl.ds(r0, tq), :].reshape(M, LANES)
            pr = jnp.exp(_dg(qv, kb, NT))
            ds = (pr * _dg(dov, vb, NT)).astype(BF16)
            dv = dv + _dg(pr.astype(BF16), dov, TN)
            dk = dk + _dg(ds, qv, TN)
            dq_sc[:, pl.ds(r0, tq), :] += _dot(ds, kb).reshape(GROUP, tq, LANES)
            return dk, dv

        z = jnp.zeros((tk, LANES), F32)
        carry = _loop_unrolled(n_lat // tq, step, (z, z))
        dk_ref[...] = carry[0]
        dv_ref[...] = carry[1]

        @pl.when(j >= n_lat // tk)
        def _():
            c = (dk_ref[...], dv_ref[...])
            for i in range(n_lat // tq, T // tq):
                c = step(i, c)
            dk_ref[...] = c[0]
            dv_ref[...] = c[1]

        @pl.when(j == nkv - 1)
        def _():
            c3 = pltpu.make_async_copy(dq_sc, dq_hbm.at[h], sems.at[2])
            c3.start()
            c3.wait()

    kv_spec = pl.BlockSpec((None, tk, LANES), lambda h, j: (h, j, 0))
    return pl.pallas_call(
        body, grid=(N_KV_HEADS, nkv), name="flash_bwd",
        in_specs=[ANY, ANY, kv_spec, kv_spec], out_specs=[ANY, kv_spec, kv_spec],
        out_shape=[jax.ShapeDtypeStruct(qa.shape, F32), jax.ShapeDtypeStruct(k.shape, F32),
                   jax.ShapeDtypeStruct(k.shape, F32)],
        scratch_shapes=[pltpu.VMEM((GROUP, T, LANES), BF16), pltpu.VMEM((GROUP, T, LANES), BF16),
                        pltpu.VMEM((GROUP, T, LANES), F32), pltpu.SemaphoreType.DMA((3,))],
        compiler_params=_cp("arbitrary", "arbitrary"))(qa, doa, k, v)


def _conv_masks(i, tm, n_lat, T):
    row = lax.broadcasted_iota(jnp.int32, (tm, 1), 0)
    g = row + i * tm
    return row, (g == 0) | (g == n_lat), (g == n_lat - 1) | (g == T - 1)


def _shift_rows(v, prev_row, next_row, row, first, last):
    tm = v.shape[0]
    down = jnp.where(row == 0, prev_row, pltpu.roll(v, 1, 0))
    up = jnp.where(row == tm - 1, next_row, pltpu.roll(v, tm - 1, 0))
    return jnp.where(first, 0.0, down), jnp.where(last, 0.0, up)


def _halo_specs(tm, T, width, col):
    nb = T // 8
    return (pl.BlockSpec((8, width), lambda i: (jnp.maximum(i * (tm // 8) - 1, 0), col)),
            pl.BlockSpec((8, width), lambda i: (jnp.minimum((i + 1) * (tm // 8), nb - 1), col)))


def _conv_fwd(p, cw, n_lat):
    T = p.shape[0]
    tm = _tile(T, 1056)

    def body(p_ref, pp_ref, pn_ref, cw_ref, o_ref):
        row, first, last = _conv_masks(pl.program_id(0), tm, n_lat, T)
        z = p_ref[:, 256:512] * p_ref[:, 512:768]
        zp = pp_ref[7:8, 256:512] * pp_ref[7:8, 512:768]
        zn = pn_ref[0:1, 256:512] * pn_ref[0:1, 512:768]
        zd, zu = _shift_rows(z, zp, zn, row, first, last)
        conv = cw_ref[0:1, :] * zd + cw_ref[1:2, :] * z + cw_ref[2:3, :] * zu
        o_ref[...] = (p_ref[:, 0:256] * conv).astype(BF16)

    prev, nxt = _halo_specs(tm, T, 768, 0)
    return pl.pallas_call(
        body, grid=(T // tm,), name="conv_fwd",
        in_specs=[pl.BlockSpec((tm, 768), lambda i: (i, 0)), prev, nxt, _full(cw.shape)],
        out_specs=pl.BlockSpec((tm, CONV_W), lambda i: (i, 0)),
        out_shape=jax.ShapeDtypeStruct((T, CONV_W), BF16), compiler_params=_cp("parallel"))(p, p, p, cw)


def _conv_bwd(dp, dy, p, cw, n_lat):
    T = p.shape[0]
    tm = _tile(T, 1056)

    def body(dp_in, dy_ref, dyp_ref, dyn_ref, p_ref, pp_ref, pn_ref, cw_ref, dp_ref, dcw_ref):
        del dp_in
        i = pl.program_id(0)

        @pl.when(i == 0)
        def _():
            dcw_ref[...] = jnp.zeros_like(dcw_ref)

        row, first, last = _conv_masks(i, tm, n_lat, T)
        ab, ac, ax = p_ref[:, 0:256], p_ref[:, 256:512], p_ref[:, 512:768]
        z = ac * ax
        zp = pp_ref[7:8, 256:512] * pp_ref[7:8, 512:768]
        zn = pn_ref[0:1, 256:512] * pn_ref[0:1, 512:768]
        zd, zu = _shift_rows(z, zp, zn, row, first, last)
        w0, w1, w2 = cw_ref[0:1, :], cw_ref[1:2, :], cw_ref[2:3, :]
        dy = dy_ref[...]
        dc = dy * ab
        dcd, dcu = _shift_rows(dc, dyp_ref[7:8, :] * pp_ref[7:8, 0:256], dyn_ref[0:1, :] * pn_ref[0:1, 0:256],
                               row, first, last)
        dz = w0 * dcu + w1 * dc + w2 * dcd
        dp_ref[:, 0:256] = (dy * (w0 * zd + w1 * z + w2 * zu)).astype(BF16)
        dp_ref[:, 256:512] = (dz * ax).astype(BF16)
        dp_ref[:, 512:768] = (dz * ac).astype(BF16)
        dcw_ref[0:1, :] += jnp.sum(dc * zd, axis=0, keepdims=True)
        dcw_ref[1:2, :] += jnp.sum(dc * z, axis=0, keepdims=True)
        dcw_ref[2:3, :] += jnp.sum(dc * zu, axis=0, keepdims=True)

    prev, nxt = _halo_specs(tm, T, 768, 0)
    dprev, dnxt = _halo_specs(tm, T, CONV_W, 0)
    sec = pl.BlockSpec((tm, 768), lambda i: (i, 0))
    return pl.pallas_call(
        body, grid=(T // tm,), name="conv_bwd",
        in_specs=[ANY, pl.BlockSpec((tm, CONV_W), lambda i: (i, 0)), dprev, dnxt, sec, prev, nxt, _full(cw.shape)],
        out_specs=[sec, _full((8, CONV_W))],
        out_shape=[jax.ShapeDtypeStruct(dp.shape, BF16), jax.ShapeDtypeStruct((8, CONV_W), F32)],
        input_output_aliases={0: 0}, compiler_params=_cp("arbitrary"))(dp, dy, dy, dy, p, p, p, cw)


def _gmlp_mix(bd_ref, vs, grp):
    out = jnp.zeros((2 * CHUNK, SG_W), F32)
    for g in range(4):
        out = jnp.where(grp == g, _dot(bd_ref[g], vs), out)
    return out


def _gmlp_fwd(p, sgn, bd, bias):
    T = p.shape[0]
    tm = _tile(T, 768, 2 * CHUNK)

    def body(p_ref, sgn_ref, bd_ref, bias_ref, o_ref):
        x = _gelu(p_ref[:, 256:512])
        vn = (x * lax.rsqrt(jnp.mean(x * x, axis=-1, keepdims=True) + EPS) * sgn_ref[...]).astype(BF16)
        grp = _lane((2 * CHUNK, SG_W)) // 64
        for s in range(tm // (2 * CHUNK)):
            rs = slice(s * 2 * CHUNK, (s + 1) * 2 * CHUNK)
            mixed = _gmlp_mix(bd_ref, vn[rs], grp) + bias_ref[...]
            o_ref[rs, :] = (_gelu(p_ref[rs, 0:256]) * mixed).astype(BF16)

    return pl.pallas_call(
        body, grid=(T // tm,), name="gmlp_fwd",
        in_specs=[pl.BlockSpec((tm, 2 * SG_W), lambda i: (i, 3)), _full(sgn.shape), _full(bd.shape),
                  _full(bias.shape)],
        out_specs=pl.BlockSpec((tm, SG_W), lambda i: (i, 0)),
        out_shape=jax.ShapeDtypeStruct((T, SG_W), BF16), compiler_params=_cp("parallel"))(p, sgn, bd, bias)


def _gmlp_bwd(dp, dy, p, sgn, bd, bdt, bias, gsum):
    T = p.shape[0]
    tm = _tile(T, 768, 2 * CHUNK)
    nt = T // tm
    C2 = 2 * CHUNK

    def body(dp_in, dy_ref, p_ref, sgn_ref, bd_ref, bdt_ref, bias_ref, gsum_ref,
             dp_ref, dsg_ref, dws_ref, dbs_ref, acc_w, acc_b):
        del dp_in
        i = pl.program_id(0)

        @pl.when(i == 0)
        def _():
            dsg_ref[...] = jnp.zeros_like(dsg_ref)
            acc_w[...] = jnp.zeros_like(acc_w)
            acc_b[...] = jnp.zeros_like(acc_b)

        u = p_ref[:, 0:256]
        sv = p_ref[:, 256:512]
        ug = _gelu(u)
        x = _gelu(sv)
        r = lax.rsqrt(jnp.mean(x * x, axis=-1, keepdims=True) + EPS)
        xh = x * r
        sg = sgn_ref[...]
        vn = (xh * sg).astype(BF16)
        grp = _lane((C2, SG_W)) // 64
        dug, dvn = [], []
        for s in range(tm // C2):
            rs = slice(s * C2, (s + 1) * C2)
            vs = vn[rs]
            dys = dy_ref[rs, :]
            dug.append(dys * (_gmlp_mix(bd_ref, vs, grp) + bias_ref[...]))
            dmix = dys * ug[rs]
            acc_b[...] += dmix
            dmb = dmix.astype(BF16)
            dvn.append(_gmlp_mix(bdt_ref, dmb, grp))
            for g in range(4):
                acc_w[g] += _dg(jnp.where(grp == g, dmb, jnp.zeros_like(dmb)), vs, NT)
        dug = jnp.concatenate(dug, axis=0)
        dvn = jnp.concatenate(dvn, axis=0)
        dsg_ref[...] += jnp.sum(dvn * xh, axis=0, keepdims=True)
        gd = sg * dvn
        dx = r * (gd - xh * jnp.mean(xh * gd, axis=-1, keepdims=True))
        dp_ref[:, 0:256] = (dug * _gelu_grad(u)).astype(BF16)
        dp_ref[:, 256:512] = (dx * _gelu_grad(sv)).astype(BF16)

        @pl.when(i == nt - 1)
        def _():
            for g in range(4):
                dws_ref[g] = acc_w[g, 0:CHUNK, 0:CHUNK] + acc_w[g, CHUNK:C2, CHUNK:C2]
            dbs_ref[...] = jnp.dot(acc_b[0:CHUNK, :] + acc_b[CHUNK:C2, :], gsum_ref[...],
                                   preferred_element_type=F32, precision=lax.Precision.HIGHEST)

    sec = pl.BlockSpec((tm, 2 * SG_W), lambda i: (i, 3))
    return pl.pallas_call(
        body, grid=(nt,), name="gmlp_bwd",
        in_specs=[ANY, pl.BlockSpec((tm, SG_W), lambda i: (i, 0)), sec, _full(sgn.shape), _full(bd.shape),
                  _full(bdt.shape), _full(bias.shape), _full(gsum.shape)],
        out_specs=[sec, _full((1, SG_W)), _full((4, CHUNK, CHUNK)), _full((CHUNK, LANES))],
        out_shape=[jax.ShapeDtypeStruct(dp.shape, BF16), jax.ShapeDtypeStruct((1, SG_W), F32),
                   jax.ShapeDtypeStruct((4, CHUNK, CHUNK), F32), jax.ShapeDtypeStruct((CHUNK, LANES), F32)],
        scratch_shapes=[pltpu.VMEM((4, C2, C2), F32), pltpu.VMEM((C2, SG_W), F32)],
        input_output_aliases={0: 0}, compiler_params=_cp("arbitrary"))(dp, dy, p, sgn, bd, bdt, bias, gsum)


def _merge_fwd(ya, at, yc, p, wa, wb, wc):
    T = p.shape[0]
    tm = _tile(T, 528)
    n = wa.shape[2]

    def body(ya_ref, at_ref, yc_ref, ga_ref, gb_ref, gc_ref, wa_ref, wb_ref, wc_ref, o_ref):
        yav, atv, ycv = ya_ref[...], at_ref[...], yc_ref[...]
        for j in range(N_CHIPS):
            cs = slice(j * n, (j + 1) * n)
            m = (_sigmoid(ga_ref[:, cs]) * _dot(yav, wa_ref[j]) + _sigmoid(gb_ref[:, cs]) * _dot(atv, wb_ref[j])
                 + _sigmoid(gc_ref[:, cs]) * _dot(ycv, wc_ref[j]))
            o_ref[:, cs] = m.astype(BF16)

    def rows(w, col=0):
        return pl.BlockSpec((tm, w), lambda i: (i, col))

    return pl.pallas_call(
        body, grid=(T // tm,), name="merge_fwd",
        in_specs=[rows(CONV_W), rows(512), rows(SG_W), rows(D_MODEL, 2), rows(D_MODEL, 3), rows(D_MODEL, 4),
                  _full(wa.shape), _full(wb.shape), _full(wc.shape)],
        out_specs=rows(D_MODEL), out_shape=jax.ShapeDtypeStruct((T, D_MODEL), BF16),
        compiler_params=_cp("parallel"))(ya, at, yc, p, p, p, wa, wb, wc)


def _merge_bwd(dyo, ya, at, yc, p, wa, wb, wc, wo):
    T = p.shape[0]
    tm = _tile(T, 528)
    n = wa.shape[2]

    def body(dyo_ref, ya_ref, at_ref, yc_ref, ga_ref, gb_ref, gc_ref, wa_ref, wb_ref, wc_ref, wo_ref,
             dp_ref, dya_ref, doa_ref, dyc_ref, dwa_ref, dwb_ref, dwc_ref):
        i = pl.program_id(0)

        @pl.when(i == 0)
        def _():
            dwa_ref[...] = jnp.zeros_like(dwa_ref)
            dwb_ref[...] = jnp.zeros_like(dwb_ref)
            dwc_ref[...] = jnp.zeros_like(dwc_ref)

        dp_ref[:, 0:OFF_G] = jnp.zeros((tm, OFF_G), BF16)
        dm = _dg(dyo_ref[...], wo_ref[...], NT)
        yav, atv, ycv = ya_ref[...], at_ref[...], yc_ref[...]
        dya = jnp.zeros((tm, CONV_W), F32)
        dat = jnp.zeros((tm, 512), F32)
        dyc = jnp.zeros((tm, SG_W), F32)
        for j in range(N_CHIPS):
            cs = slice(j * n, (j + 1) * n)
            dmj = dm[:, cs]
            for y_in, w_ref, g_ref, dw_ref, which in (
                    (yav, wa_ref, ga_ref, dwa_ref, 0), (atv, wb_ref, gb_ref, dwb_ref, 1),
                    (ycv, wc_ref, gc_ref, dwc_ref, 2)):
                sg = _sigmoid(g_ref[:, cs])
                y = _dot(y_in, w_ref[j])
                c0 = OFF_G + which * D_MODEL + j * n
                dp_ref[:, c0:c0 + n] = (dmj * y * sg * (1.0 - sg)).astype(BF16)
                dyb = (dmj * sg).astype(BF16)
                dw_ref[j] += _dg(y_in, dyb, TN)
                back = _dg(dyb, w_ref[j], NT)
                if which == 0:
                    dya = dya + back
                elif which == 1:
                    dat = dat + back
                else:
                    dyc = dyc + back
        dya_ref[...] = dya
        dyc_ref[...] = dyc
        prod = dat * atv.astype(F32)
        lane = _lane((tm, LANES))
        dat_rows = _heads_to_rows(dat, N_Q_HEADS)
        for h in range(N_Q_HEADS):
            grp = prod[:, (h // 2) * LANES:(h // 2 + 1) * LANES]
            keep = (lane < 64) if h % 2 == 0 else (lane >= 64)
            delta = jnp.sum(jnp.where(keep, grp, 0.0), axis=1, keepdims=True)
            doa_ref[h] = _aug(dat_rows[h], delta).astype(BF16)

    def rows(w, col=0):
        return pl.BlockSpec((tm, w), lambda i: (i, col))

    return pl.pallas_call(
        body, grid=(T // tm,), name="merge_bwd",
        in_specs=[rows(D_MODEL), rows(CONV_W), rows(512), rows(SG_W), rows(D_MODEL, 2), rows(D_MODEL, 3),
                  rows(D_MODEL, 4), _full(wa.shape), _full(wb.shape), _full(wc.shape), _full(wo.shape)],
        out_specs=[rows(IN_W), rows(CONV_W),
                   pl.BlockSpec((N_Q_HEADS, tm, LANES), lambda i: (0, i, 0)), rows(SG_W),
                   _full(wa.shape), _full(wb.shape), _full(wc.shape)],
        out_shape=[jax.ShapeDtypeStruct((T, IN_W), BF16)] + [
            jax.ShapeDtypeStruct((T, CONV_W), F32), jax.ShapeDtypeStruct((N_Q_HEADS, T, LANES), BF16),
            jax.ShapeDtypeStruct((T, SG_W), F32), jax.ShapeDtypeStruct(wa.shape, F32),
            jax.ShapeDtypeStruct(wb.shape, F32), jax.ShapeDtypeStruct(wc.shape, F32)],
        compiler_params=_cp("arbitrary"))(dyo, ya, at, yc, p, p, p, wa, wb, wc, wo)


def _loss_grad(xf, tgt, n_lat):
    T, D = xf.shape
    tm = _tile(np.gcd(n_lat, T), 512)
    nl = n_lat // tm

    def body(x_ref, t_ref, dy_ref, l_ref):
        i = pl.program_id(0)

        @pl.when(i == 0)
        def _():
            l_ref[...] = jnp.zeros_like(l_ref)

        @pl.when(i < nl)
        def _():
            err = x_ref[...] - t_ref[...]
            dy_ref[...] = err * (1.0 / D)
            sq = jnp.sum(jnp.sum(err * err, axis=1, keepdims=True), axis=0, keepdims=True)
            l_ref[...] += (0.5 / D) * sq

        @pl.when(i >= nl)
        def _():
            dy_ref[...] = jnp.zeros_like(dy_ref)

    return pl.pallas_call(
        body, grid=(T // tm,), name="loss_grad",
        in_specs=[pl.BlockSpec((tm, D), lambda i: (i, 0)), pl.BlockSpec((tm, D), lambda i: (jnp.minimum(i, nl - 1), 0))],
        out_specs=[pl.BlockSpec((tm, D), lambda i: (i, 0)), _full((8, LANES))],
        out_shape=[jax.ShapeDtypeStruct((T, D), F32), jax.ShapeDtypeStruct((8, LANES), F32)],
        compiler_params=_cp("arbitrary"))(xf, tgt)


def _row_tile(R, C):
    if R * C <= (1 << 19) or R % 8:
        return R
    return _tile(R, max(8, (1 << 19) // C), 8)


def _adamw(w, m, v, g1, g2=None):
    shape = w.shape
    C = shape[-1]
    R = int(np.prod(shape[:-1])) if len(shape) > 1 else 1
    tr = _row_tile(R, C)
    ins = [a.reshape(R, C) for a in ((w, m, v, g1) if g2 is None else (w, m, v, g1, g2))]

    def body(*refs):
        w_ref, m_ref, v_ref = refs[0], refs[1], refs[2]
        g_ref, d_ref, m2_ref, v2_ref = refs[-4:]
        g = refs[3][...] if g2 is None else refs[3][...] + refs[4][...]
        m2 = ADAM_B1 * m_ref[...] + (1.0 - ADAM_B1) * g
        v2 = ADAM_B2 * v_ref[...] + (1.0 - ADAM_B2) * (g * g)
        m_hat = m2 / (1.0 - ADAM_B1 ** ADAM_STEP)
        v_hat = v2 / (1.0 - ADAM_B2 ** ADAM_STEP)
        g_ref[...] = g
        d_ref[...] = -ADAM_LR * (m_hat / (jnp.sqrt(v_hat) + ADAM_EPS) + ADAM_WD * w_ref[...])
        m2_ref[...] = m2
        v2_ref[...] = v2

    spec = pl.BlockSpec((tr, C), lambda i: (i, 0))
    outs = pl.pallas_call(
        body, grid=(R // tr,), name="adamw", in_specs=[spec] * len(ins), out_specs=[spec] * 4,
        out_shape=[jax.ShapeDtypeStruct((R, C), F32)] * 4, compiler_params=_cp("parallel"))(*ins)
    return [o.reshape(shape) for o in outs]


def _sum_lead(x, name):
    n, R, C = x.shape
    tr = _row_tile(R, C * n)

    def body(x_ref, o_ref):
        acc = x_ref[0].astype(F32)
        for s in range(1, n):
            acc = acc + x_ref[s].astype(F32)
        o_ref[...] = acc

    return pl.pallas_call(
        body, grid=(R // tr,), name=name, in_specs=[pl.BlockSpec((n, tr, C), lambda i: (0, i, 0))],
        out_specs=pl.BlockSpec((tr, C), lambda i: (i, 0)), out_shape=jax.ShapeDtypeStruct((R, C), F32),
        compiler_params=_cp("parallel"))(x)


def _silu(x):
    return x * _sigmoid(x)


def _mod_fwd(a_raw, w_mod, bsh):
    L, D, n = w_mod.shape

    def body(a_ref, w_ref, b_ref, o_ref):
        o_ref[...] = _dot(_silu(a_ref[...]).astype(BF16), w_ref[...].astype(BF16)) + b_ref[...]

    return pl.pallas_call(
        body, grid=(L,), name="mod_fwd",
        in_specs=[_full(a_raw.shape), pl.BlockSpec((None, D, n), lambda l: (l, 0, 0)),
                  pl.BlockSpec((None, 1, n), lambda l: (l, 0, 0))],
        out_specs=pl.BlockSpec((None, 16, n), lambda l: (l, 0, 0)),
        out_shape=jax.ShapeDtypeStruct((L, 16, n), F32), compiler_params=_cp("parallel"))(a_raw, w_mod, bsh)


def _wmod_grad(a_raw, dms):
    L, _, n = dms.shape
    D = a_raw.shape[1]

    def body(a_ref, dm_ref, o_ref):
        o_ref[...] = _dg(_silu(a_ref[...]).astype(BF16), dm_ref[...].astype(BF16), TN)

    return pl.pallas_call(
        body, grid=(L,), name="wmod_grad",
        in_specs=[_full(a_raw.shape), pl.BlockSpec((None, 16, n), lambda l: (l, 0, 0))],
        out_specs=pl.BlockSpec((None, D, n), lambda l: (l, 0, 0)),
        out_shape=jax.ShapeDtypeStruct((L, D, n), F32), compiler_params=_cp("parallel"))(a_raw, dms)


def _cctx_partial(dmc, w_mod):
    L, D, n = w_mod.shape

    def body(dm_ref, w_ref, o_ref):
        part = _dg(dm_ref[...].astype(BF16), w_ref[...].astype(BF16), NT)

        @pl.when(pl.program_id(0) == 0)
        def _():
            o_ref[...] = part

        @pl.when(pl.program_id(0) > 0)
        def _():
            o_ref[...] += part

    return pl.pallas_call(
        body, grid=(L,), name="cctx_partial",
        in_specs=[pl.BlockSpec((None, 16, n), lambda l: (l, 0, 0)), pl.BlockSpec((None, D, n), lambda l: (l, 0, 0))],
        out_specs=_full((16, D)), out_shape=jax.ShapeDtypeStruct((16, D), F32),
        compiler_params=_cp("arbitrary"))(dmc, w_mod)


def _cctx_final(parts, cc):
    def body(p_ref, c_ref, o_ref):
        s = p_ref[0, 0:8, :]
        for j in range(1, N_CHIPS):
            s = s + p_ref[2 * j, 0:8, :]
        xv = c_ref[...]
        sg = _sigmoid(xv)
        o_ref[...] = s * (sg * (1.0 + xv * (1.0 - sg)))

    return pl.pallas_call(
        body, name="cctx_final", in_specs=[_full(parts.shape), _full(cc.shape)], out_specs=_full((8, LANES)),
        out_shape=jax.ShapeDtypeStruct((8, LANES), F32), compiler_params=_cp())(parts, cc)


def _me():
    return lax.axis_index("x"), lax.axis_index("y"), lax.axis_index("c")


def _flip(v, bit):
    return 1 - v if bit else v


def _remote(src, dst, ssem, rsem, peer):
    return pltpu.make_async_remote_copy(src_ref=src, dst_ref=dst, send_sem=ssem, recv_sem=rsem,
                                        device_id=peer, device_id_type=MESH_ID)


def _ag8(xb, name):
    R = xb.shape[0]

    def body(x_ref, o_ref, ssem, rsem, lsem):
        mx, my, mc = _me()
        me = 4 * mx + 2 * my + mc
        loc = pltpu.make_async_copy(x_ref, o_ref.at[me], lsem.at[0])
        loc.start()
        sends = []
        for k in range(1, N_DEV):
            peer = (_flip(mx, k & 4), _flip(my, k & 2), _flip(mc, k & 1))
            cp = _remote(x_ref, o_ref.at[me], ssem.at[k - 1], rsem.at[k - 1], peer)
            cp.start()
            sends.append((cp, peer))
        for k, (cp, peer) in enumerate(sends):
            pid = 4 * peer[0] + 2 * peer[1] + peer[2]
            _remote(x_ref, o_ref.at[pid], ssem.at[k], rsem.at[k], peer).wait_recv()
        for cp, _ in sends:
            cp.wait_send()
        loc.wait()

    return pl.pallas_call(
        body, name=name, in_specs=[ANY], out_specs=ANY, out_shape=jax.ShapeDtypeStruct((N_DEV, R, LANES), F32),
        scratch_shapes=[pltpu.SemaphoreType.DMA((N_DEV - 1,)), pltpu.SemaphoreType.DMA((N_DEV - 1,)),
                        pltpu.SemaphoreType.DMA((1,))])(xb)


def _plane_peers(mx, my, mc):
    out = []
    for k in range(1, N_CHIPS):
        px, py = _flip(mx, k & 2), _flip(my, k & 1)
        out.append(((px, py, mc), 2 * px + py))
    return out


def _chip_gather(arrs, name):
    n = len(arrs)
    halves = [a.shape[0] // 2 for a in arrs]

    def body(*refs):
        ins, outs = refs[:n], refs[n:2 * n]
        ssem, rsem, fsem, gsem, lsem = refs[2 * n:]
        mx, my, mc = _me()
        j = 2 * mx + my
        sib = (mx, my, 1 - mc)

        def half(ref, a, c):
            return ref.at[pl.ds(c * halves[a], halves[a]), :]

        locs = [pltpu.make_async_copy(ins[a], outs[a].at[j], lsem.at[a]) for a in range(n)]
        for cp in locs:
            cp.start()
        peers = _plane_peers(mx, my, mc)
        sends = []
        for k, (peer, _) in enumerate(peers):
            for a in range(n):
                cp = _remote(half(ins[a], a, mc), half(outs[a].at[j], a, mc), ssem.at[k * n + a], rsem.at[k * n + a], peer)
                cp.start()
                sends.append(cp)
        for k, (peer, pj) in enumerate(peers):
            for a in range(n):
                got = half(outs[a].at[pj], a, mc)
                _remote(got, got, ssem.at[k * n + a], rsem.at[k * n + a], peer).wait_recv()
                fw = _remote(got, got, fsem.at[k * n + a], gsem.at[k * n + a], sib)
                fw.start()
                sends.append(fw)
        for k, (_, pj) in enumerate(peers):
            for a in range(n):
                theirs = half(outs[a].at[pj], a, 1 - mc)
                _remote(theirs, theirs, fsem.at[k * n + a], gsem.at[k * n + a], sib).wait_recv()
        for cp in sends:
            cp.wait_send()
        for cp in locs:
            cp.wait()

    sems = pltpu.SemaphoreType.DMA((3 * n,))
    return pl.pallas_call(
        body, name=name, in_specs=[ANY] * n, out_specs=[ANY] * n,
        out_shape=[jax.ShapeDtypeStruct((N_CHIPS,) + a.shape, a.dtype) for a in arrs],
        scratch_shapes=[sems, sems, sems, sems, pltpu.SemaphoreType.DMA((n,))])(*arrs)


def _chip_scatter(gs, name):
    n = len(gs)

    def body(*refs):
        ins, outs = refs[:n], refs[n:2 * n]
        ssem, rsem, lsem = refs[2 * n:]
        mx, my, mc = _me()
        j = 2 * mx + my
        locs = [pltpu.make_async_copy(ins[a].at[j], outs[a].at[j], lsem.at[a]) for a in range(n)]
        for cp in locs:
            cp.start()
        peers = _plane_peers(mx, my, mc)
        sends = []
        for k, (peer, pj) in enumerate(peers):
            for a in range(n):
                cp = _remote(ins[a].at[pj], outs[a].at[j], ssem.at[k * n + a], rsem.at[k * n + a], peer)
                cp.start()
                sends.append(cp)
        for k, (peer, pj) in enumerate(peers):
            for a in range(n):
                _remote(ins[a].at[pj], outs[a].at[pj], ssem.at[k * n + a], rsem.at[k * n + a], peer).wait_recv()
        for cp in sends:
            cp.wait_send()
        for cp in locs:
            cp.wait()

    return pl.pallas_call(
        body, name=name, in_specs=[ANY] * n, out_specs=[ANY] * n,
        out_shape=[jax.ShapeDtypeStruct(g.shape, g.dtype) for g in gs],
        scratch_shapes=[pltpu.SemaphoreType.DMA((3 * n,)), pltpu.SemaphoreType.DMA((3 * n,)),
                        pltpu.SemaphoreType.DMA((n,))])(*gs)


def _sibling_halves(gs, name):
    n = len(gs)

    def body(*refs):
        ins, outs = refs[:n], refs[n:2 * n]
        ssem, rsem = refs[2 * n:]
        mx, my, mc = _me()
        cps = []
        for a in range(n):
            h = gs[a].shape[1] // 2
            cps.append(_remote(ins[a].at[:, pl.ds((1 - mc) * h, h), :], outs[a], ssem.at[a], rsem.at[a],
                               (mx, my, 1 - mc)))
        for cp in cps:
            cp.start()
        for cp in cps:
            cp.wait()

    return pl.pallas_call(
        body, name=name, in_specs=[ANY] * n, out_specs=[ANY] * n,
        out_shape=[jax.ShapeDtypeStruct((g.shape[0], g.shape[1] // 2, g.shape[2]), g.dtype) for g in gs],
        scratch_shapes=[pltpu.SemaphoreType.DMA((n,)), pltpu.SemaphoreType.DMA((n,))])(*gs)


def _sibling_fill(hs, name):
    n = len(hs)

    def body(*refs):
        ins, outs = refs[:n], refs[n:2 * n]
        ssem, rsem, lsem = refs[2 * n:]
        mx, my, mc = _me()
        cps, locs = [], []
        for a in range(n):
            h = hs[a].shape[0]
            mine = outs[a].at[pl.ds(mc * h, h), :]
            locs.append(pltpu.make_async_copy(ins[a], mine, lsem.at[a]))
            cps.append(_remote(ins[a], mine, ssem.at[a], rsem.at[a], (mx, my, 1 - mc)))
        for cp in locs + cps:
            cp.start()
        for a, cp in enumerate(cps):
            h = hs[a].shape[0]
            theirs = outs[a].at[pl.ds((1 - mc) * h, h), :]
            _remote(ins[a], theirs, ssem.at[a], rsem.at[a], (mx, my, 1 - mc)).wait_recv()
            cp.wait_send()
        for cp in locs:
            cp.wait()

    return pl.pallas_call(
        body, name=name, in_specs=[ANY] * n, out_specs=[ANY] * n,
        out_shape=[jax.ShapeDtypeStruct((2 * x.shape[0], x.shape[1]), x.dtype) for x in hs],
        scratch_shapes=[pltpu.SemaphoreType.DMA((n,)), pltpu.SemaphoreType.DMA((n,)),
                        pltpu.SemaphoreType.DMA((n,))])(*hs)


def _add_cast(g, sb):
    J, h, b = g.shape
    th = _row_tile(h, b * J)

    def body(g_ref, s_ref, o_ref):
        o_ref[...] = (g_ref[...] + s_ref[...].astype(F32)).astype(BF16)

    spec = pl.BlockSpec((J, th, b), lambda i: (0, i, 0))
    return pl.pallas_call(
        body, grid=(h // th,), name="add_planes", in_specs=[spec, spec], out_specs=spec,
        out_shape=jax.ShapeDtypeStruct((J, h, b), BF16), compiler_params=_cp("parallel"))(g, sb)


_WEIGHTS = ("c_ctx", "w_mod", "b_mod", "norm1", "w_in", "q_gain", "k_gain", "conv_w", "sg_norm", "w_s", "b_s",
            "w_a", "w_b", "w_c", "w_o", "norm2", "w_ff1", "w_ff3", "w_ff2")
_BIG = ("w_in", "w_a", "w_b", "w_c", "w_o", "w_ff1", "w_ff3", "w_ff2")


def _constants():
    idx = np.arange(LANES)
    e = (idx[:, None] // 64 == idx[None, :] // 64).astype(np.float32) / 64.0
    c512 = np.arange(512)
    fold = (c512[:, None] % 64 == idx[None, :]).astype(np.float32)
    c256 = np.arange(SG_W)
    gsum = (c256[:, None] // 64 == idx[None, :]).astype(np.float32)
    return jnp.asarray(e, BF16), jnp.asarray(fold, F32), jnp.asarray(gsum, F32)


def _rope_tables(n_lat, n_ctx):
    t = jnp.arange(n_lat)
    inv = ROPE_THETA ** (-jnp.arange(0, HEAD_DIM // 2, 2, dtype=F32) / (HEAD_DIM // 2))
    ar = (t // GRID_W).astype(F32)[:, None] * inv
    ac = (t % GRID_W).astype(F32)[:, None] * inv
    cos = jnp.concatenate([jnp.cos(ar), jnp.cos(ar), jnp.cos(ac), jnp.cos(ac)], axis=1)
    sin = jnp.concatenate([-jnp.sin(ar), jnp.sin(ar), -jnp.sin(ac), jnp.sin(ac)], axis=1)
    cos = jnp.concatenate([cos, jnp.ones((n_ctx, HEAD_DIM), F32)], axis=0)
    sin = jnp.concatenate([sin, jnp.zeros((n_ctx, HEAD_DIM), F32)], axis=0)
    return jnp.concatenate([cos, cos], axis=1), jnp.concatenate([sin, sin], axis=1)


def kernel(x, c, ctx, c_ctx, w_mod, b_mod, norm1, w_in, q_gain, k_gain, conv_w, sg_norm, w_s, b_s, w_a, w_b, w_c, w_o, norm2, w_ff1, w_ff3, w_ff2, loss_target, m_c_ctx, m_w_mod, m_b_mod, m_norm1, m_w_in, m_q_gain, m_k_gain, m_conv_w, m_sg_norm, m_w_s, m_b_s, m_w_a, m_w_b, m_w_c, m_w_o, m_norm2, m_w_ff1, m_w_ff3, m_w_ff2, v_c_ctx, v_w_mod, v_b_mod, v_norm1, v_w_in, v_q_gain, v_k_gain, v_conv_w, v_sg_norm, v_w_s, v_b_s, v_w_a, v_w_b, v_w_c, v_w_o, v_norm2, v_w_ff1, v_w_ff3, v_w_ff2):
    given = dict(locals())
    mx, my, mc = _me()
    chip = 2 * mx + my
    dev = 4 * mx + 2 * my + mc
    L = norm1.shape[0]
    S, Lc = x.shape[1], ctx.shape[1]
    T = S + Lc
    D = D_MODEL
    n_mod, n_in, n_ff = w_mod.shape[2], w_in.shape[2], w_ff1.shape[2]
    n_cw = conv_w.shape[2]
    e_avg, fold, gsum = _constants()
    cos_t, sin_t = _rope_tables(S, Lc)

    cw_rows = (L * 3 * n_cw) // LANES
    pad = (-(8 + cw_rows)) % 8
    buf = jnp.concatenate([c.reshape(8, LANES), conv_w.reshape(cw_rows, LANES), jnp.zeros((pad, LANES), F32)], axis=0)
    g1 = _ag8(buf, "gather_cond")
    conds = g1[:, :8].reshape(N_DEV, D)
    cw_full = jnp.stack([g1[2 * j, 8:8 + cw_rows].reshape(L, 3, n_cw) for j in range(N_CHIPS)], axis=2)
    cw_full = cw_full.reshape(L, 3, N_CHIPS * n_cw)
    cw8 = jnp.pad(cw_full, ((0, 0), (0, 5), (0, 0)))
    a_raw = jnp.concatenate([conds, c_ctx[None], jnp.zeros((7, D), F32)], axis=0)
    bsh = lax.dynamic_slice_in_dim(b_mod, chip * n_mod, n_mod, axis=1)[:, None, :]
    mod_sh = _mod_fwd(a_raw, w_mod, bsh)
    g2 = _ag8(mod_sh.reshape(-1, LANES), "gather_mod")
    mods = jnp.stack([g2[2 * j].reshape(L, 16, n_mod) for j in range(N_CHIPS)], axis=2).reshape(L, 16, N_CHIPS * n_mod)
    lat = lax.dynamic_index_in_dim(mods, dev, axis=1, keepdims=False)
    mod = jnp.stack([lat.reshape(L, 6, D), mods[:, 8].reshape(L, 6, D)], axis=1)
    mod = jnp.pad(mod, ((0, 0), (0, 0), (0, 2), (0, 0)))

    qg = jnp.tile(q_gain, (1, N_Q_HEADS))[:, None, :]
    kg = jnp.tile(k_gain, (1, N_KV_HEADS))[:, None, :]
    sgn = sg_norm[:, None, :]
    ws_b = w_s.astype(BF16)
    zero = jnp.zeros_like(ws_b)
    bd = jnp.concatenate([jnp.concatenate([ws_b, zero], axis=3), jnp.concatenate([zero, ws_b], axis=3)], axis=2)
    bdt = jnp.swapaxes(bd, 2, 3)
    bias = jnp.tile(jnp.repeat(jnp.swapaxes(b_s, 1, 2), SG_W // 4, axis=2), (1, 2, 1))

    def gathered(l):
        arrs = [given[nm][l].astype(BF16) for nm in _BIG]
        win, wa, wb, wc, wo, w1, w3, w2 = _chip_gather(arrs, "gather_weights")
        return win, wa, wb, wc, wo.reshape(1, D, D), w1, w3, w2

    def layer_fwd(X, l, W):
        win, wa, wb, wc, wo, w1, w3, w2 = W
        h = _norm_mod(X, norm1[l][None], mod[l], 0, 1, S)
        p = _mm_nn(h, win, F32, "in_proj")
        ya = _conv_fwd(p, cw8[l], S)
        q, k, v = _qkv_prep(p, cos_t, sin_t, qg[l], kg[l], e_avg)
        at, qa = _flash_fwd(q.reshape(N_KV_HEADS, GROUP, T, LANES), k, v, S)
        yc = _gmlp_fwd(p, sgn[l], bd[l], bias[l])
        mg = _merge_fwd(ya, at, yc, p, wa, wb, wc)
        X1, f1 = _mm_res(mg[None], wo, X, mod[l], 2, S, "out_proj")
        h2 = _norm_mod(X1, norm2[l][None], mod[l], 3, 4, S)
        a1, a3, act = _ffn_up(h2, w1, w3)
        X2, f2 = _mm_res(act, w2, X1, mod[l], 5, S, "ffn_down")
        return X2, dict(X=X, h=h, p=p, ya=ya, k=k, v=v, at=at, qa=qa, yc=yc, mg=mg, X1=X1, f1=f1, h2=h2,
                        a1=a1, a3=a3, act=act, f2=f2)

    def layer_bwd(dX2, l, W, sv):
        win, wa, wb, wc, wo, w1, w3, w2 = W
        dyf, dgt2 = _gate_bwd(dX2, sv["f2"], mod[l], 5, S)
        da1, da3 = _ffn_down_bwd(dyf, w2, sv["a1"], sv["a3"])
        dw2 = _mm_tn(sv["act"], dyf, _shard_rows(n_ff), _rows(D), N_CHIPS, n_ff, D, T, "dw_ff2")
        dh2 = _mm_nt_acc([da1, da3], [w1, w3], False, "ffn_up_bwd")
        dw1 = _mm_tn(sv["h2"], da1, _rows(D), _shard_rows(n_ff), N_CHIPS, D, n_ff, T, "dw_ff1")
        dw3 = _mm_tn(sv["h2"], da3, _rows(D), _shard_rows(n_ff), N_CHIPS, D, n_ff, T, "dw_ff3")
        dX1, dn2, dsh2, dsc2 = _norm_mod_bwd(sv["X1"], dh2, dX2, norm2[l][None], mod[l], 4, S)
        dyo, dgt1 = _gate_bwd(dX1, sv["f1"], mod[l], 2, S)
        dwo = _mm_tn(sv["mg"], dyo, _rows(D), _rows(D), 1, D, D, T, "dw_o")
        dp, dya, doa, dyc, dwa, dwb, dwc = _merge_bwd(dyo, sv["ya"], sv["at"], sv["yc"], sv["p"], wa, wb, wc, wo[0])
        dp, dcw = _conv_bwd(dp, dya, sv["p"], cw8[l], S)
        dp, dsg, dws, dbs = _gmlp_bwd(dp, dyc, sv["p"], sgn[l], bd[l], bdt[l], bias[l], gsum)
        dq, dk, dv = _flash_bwd(sv["qa"], doa.reshape(N_KV_HEADS, GROUP, T, LANES), sv["k"], sv["v"], S)
        dp, dqg, dkg = _qkv_prep_bwd(dp, dq.reshape(N_Q_HEADS, T, LANES), dk, dv, sv["p"], cos_t, sin_t,
                                     qg[l], kg[l], e_avg, fold)
        dh = _mm_nt_acc([dp], [win], True, "in_proj_bwd")
        dwin = _mm_tn(sv["h"], dp, _rows(D), _row_cols(n_in), N_CHIPS, D, n_in, T, "dw_in")
        dX0, dn1, dsh1, dsc1 = _norm_mod_bwd(sv["X"], dh, dX1, norm1[l][None], mod[l], 1, S)
        dmod = jnp.concatenate([dsh1, dsc1, dgt1, dsh2, dsc2, dgt2], axis=1)
        big = [dwin, dwa, dwb, dwc, dwo.reshape(N_CHIPS, D // N_CHIPS, D), dw1, dw3, dw2]
        small = dict(norm1=dn1[0], norm2=dn2[0], q_gain=dqg[0, :HEAD_DIM], k_gain=dkg[0, :HEAD_DIM],
                     conv_w=dcw[:3], sg_norm=dsg[0], w_s=dws, b_s=jnp.swapaxes(dbs[:, :4], 0, 1), dmod=dmod)
        return dX0, big, small

    X = jnp.concatenate([x[0], ctx[0]], axis=0)
    Ws, saved = [], []
    for l in range(L):
        Ws.append(gathered(l))
        X, sv = layer_fwd(X, l, Ws[l])
        saved.append(sv)
    dX, lpart = _loss_grad(X, loss_target[0], S)
    loss = lax.psum(lpart[0, 0], ("x", "y", "c"))

    out = {nm: [None] * L for nm in _BIG}
    smalls = [None] * L
    for l in reversed(range(L)):
        dX, big, smalls[l] = layer_bwd(dX, l, Ws[l], saved[l])
        sib = _sibling_halves([g.astype(BF16) for g in big], "swap_halves")
        own = [lax.dynamic_slice_in_dim(g, mc * (g.shape[1] // 2), g.shape[1] // 2, axis=1) for g in big]
        recv = _chip_scatter([_add_cast(g, s) for g, s in zip(own, sib)], "scatter_grads")
        full = _sibling_fill([_sum_lead(r, "sum_chips") for r in recv], "fill_halves")
        for nm, g_ in zip(_BIG, full):
            out[nm][l] = _adamw(given[nm][l], given["m_" + nm][l], given["v_" + nm][l], g_)
    grad_x = dX[:S][None]

    def flat(nm):
        return jnp.stack([smalls[l][nm] for l in range(L)]).reshape(-1)

    dmod_all = jnp.stack([smalls[l]["dmod"] for l in range(L)])
    dml = dmod_all[:, 0].reshape(-1)
    dmc = dmod_all[:, 1].reshape(-1)
    names = ("norm1", "q_gain", "k_gain", "conv_w", "sg_norm", "w_s", "b_s", "norm2")
    parts = [dml, dml + dmc, dmc] + [flat(nm) for nm in names]
    sizes = [int(a.shape[0]) for a in parts]
    total = sum(sizes)
    padn = (-total) % (8 * LANES)
    sbuf = jnp.concatenate(parts + [jnp.zeros((padn,), F32)]).reshape(-1, LANES)
    g3 = _ag8(sbuf, "gather_small")
    ssum = _sum_lead(g3, "sum_devices").reshape(-1)
    offs = np.cumsum([0] + sizes)
    seg = {nm: ssum[offs[i + 3]:offs[i + 4]] for i, nm in enumerate(names)}
    gb_mod = ssum[offs[1]:offs[2]].reshape(L, N_CHIPS * n_mod)
    dmc_sum = ssum[offs[2]:offs[3]].reshape(L, N_CHIPS * n_mod)
    dml_all = g3.reshape(N_DEV, -1)[:, :sizes[0]].reshape(N_DEV, L, N_CHIPS * n_mod)
    dml_sh = jnp.swapaxes(lax.dynamic_slice_in_dim(dml_all, chip * n_mod, n_mod, axis=2), 0, 1)
    dmc_sh = lax.dynamic_slice_in_dim(dmc_sum, chip * n_mod, n_mod, axis=1)[:, None, :]
    dms = jnp.concatenate([dml_sh, dmc_sh, jnp.zeros((L, 7, n_mod), F32)], axis=1)
    g_wmod = _wmod_grad(a_raw, dms)
    part = _cctx_partial(jnp.concatenate([dmc_sh, jnp.zeros((L, 15, n_mod), F32)], axis=1), w_mod)
    g4 = _ag8(part.reshape(-1, LANES), "gather_cctx")
    g_cctx = _cctx_final(g4, c_ctx.reshape(8, LANES)).reshape(D)

    g_conv = lax.dynamic_slice_in_dim(seg["conv_w"].reshape(L, 3, N_CHIPS * n_cw), chip * n_cw, n_cw, axis=2)
    small_g = dict(c_ctx=g_cctx, w_mod=g_wmod, b_mod=gb_mod, norm1=seg["norm1"].reshape(norm1.shape),
                   q_gain=seg["q_gain"].reshape(q_gain.shape), k_gain=seg["k_gain"].reshape(k_gain.shape),
                   conv_w=g_conv, sg_norm=seg["sg_norm"].reshape(sg_norm.shape), w_s=seg["w_s"].reshape(w_s.shape),
                   b_s=seg["b_s"].reshape(b_s.shape), norm2=seg["norm2"].reshape(norm2.shape))
    res = {}
    for nm in _WEIGHTS:
        if nm in _BIG:
            res[nm] = [jnp.stack([out[nm][l][k] for l in range(L)]) for k in range(4)]
        else:
            res[nm] = _adamw(given[nm], given["m_" + nm], given["v_" + nm], small_g[nm])
    return (loss, grad_x, *[res[nm][0] for nm in _WEIGHTS], *[res[nm][1] for nm in _WEIGHTS],
            *[res[nm][2] for nm in _WEIGHTS], *[res[nm][3] for nm in _WEIGHTS])
```

```python
import functools

import jax
import jax.numpy as jnp
import numpy as np
from jax import lax
from jax.experimental import pallas as pl
from jax.experimental.pallas import tpu as pltpu

F32 = jnp.float32
BF16 = jnp.bfloat16
EPS = 1e-6
D_MODEL = 1024
HEAD_DIM = 64
N_Q_HEADS = 8
N_KV_HEADS = 2
GROUP = N_Q_HEADS // N_KV_HEADS
GRID_W = 64
ROPE_THETA = 10000.0
CHUNK = 128
CONV_W = 256
SG_W = 256
OFF_Q = 3 * CONV_W
QKV_W = 768
OFF_U = OFF_Q + QKV_W
OFF_G = OFF_U + 2 * SG_W
IN_W = OFF_G + 3 * D_MODEL
N_CHIPS = 4
N_DEV = 8
LANES = 128
UNROLL_FWD = 8
UNROLL_BWD = 4
AUG = 3
ADAM_LR, ADAM_B1, ADAM_B2, ADAM_EPS, ADAM_WD, ADAM_STEP = 0.001, 0.9, 0.999, 1e-8, 0.01, 10
VMEM_LIMIT_V7X = 52 * 1024 * 1024
MESH_ID = pl.DeviceIdType.MESH
NT = (((1,), (1,)), ((), ()))
TN = (((0,), (0,)), ((), ()))
ANY = pl.BlockSpec(memory_space=pl.ANY)


def _cp(*sem):
    return pltpu.CompilerParams(dimension_semantics=sem or None, vmem_limit_bytes=VMEM_LIMIT_V7X)


def _tile(n, target, mult=16):
    best = None
    for t in range(mult, n + 1, mult):
        if n % t == 0 and t <= target:
            best = t
    assert best is not None, (n, target, mult)
    return best


def _full(shape):
    nd = len(shape)
    return pl.BlockSpec(tuple(shape), lambda *_: (0,) * nd)


def _segments(i, tm, n_lat, fn):
    k, off = divmod(n_lat, tm)

    @pl.when(i < k)
    def _():
        fn(0, tm, 0)

    @pl.when(i == k)
    def _():
        if off:
            fn(0, off, 0)
        fn(off, tm, 1)

    @pl.when(i > k)
    def _():
        fn(0, tm, 1)


def _dot(a, b):
    return jnp.dot(a, b, preferred_element_type=F32)


def _dg(a, b, dims):
    return lax.dot_general(a, b, dims, preferred_element_type=F32)


def _split3(x):
    hi = x.astype(BF16)
    r1 = x - hi.astype(F32)
    mid = r1.astype(BF16)
    lo = (r1 - mid.astype(F32)).astype(BF16)
    return hi.astype(F32), mid.astype(F32), lo.astype(F32)


def _lane(shape):
    return lax.broadcasted_iota(jnp.int32, shape, len(shape) - 1)


def _aug(val, stat):
    lane = _lane(val.shape)
    hi, mid, lo = _split3(stat)
    ext = jnp.where(lane == 64, hi, jnp.where(lane == 65, mid, jnp.where(lane == 66, lo, 0.0)))
    return jnp.where(lane < 64, val, ext)


def _seg_mean(x, e):
    outs = []
    for g in range(x.shape[1] // LANES):
        blk = x[:, g * LANES:(g + 1) * LANES]
        hi = blk.astype(BF16)
        lo = (blk - hi.astype(F32)).astype(BF16)
        outs.append(_dot(hi, e) + _dot(lo, e))
    return outs[0] if len(outs) == 1 else jnp.concatenate(outs, axis=1)


def _rope(x, cos, sin_signed, inverse):
    w = x.shape[1]
    reps = w // LANES
    c = cos if reps == 1 else jnp.tile(cos, (1, reps))
    s = sin_signed if reps == 1 else jnp.tile(sin_signed, (1, reps))
    first = (_lane(x.shape) % 32) < 16
    partner = jnp.where(first, pltpu.roll(x, w - 16, 1), pltpu.roll(x, 16, 1))
    return x * c - partner * s if inverse else x * c + partner * s


def _sigmoid(x):
    return 1.0 / (1.0 + jnp.exp(-x))


_GELU_K = 0.7978845608028654
_GELU_C = 0.044715


def _gelu(x):
    return 0.5 * x * (1.0 + jnp.tanh(_GELU_K * (x + _GELU_C * x * x * x)))


def _gelu_grad(x):
    t = jnp.tanh(_GELU_K * (x + _GELU_C * x * x * x))
    return 0.5 * (1.0 + t) + 0.5 * x * (1.0 - t * t) * _GELU_K * (1.0 + 3.0 * _GELU_C * x * x)


def _loop_unrolled(n, step, init, unroll):
    def trip(t, carry):
        for u in range(unroll):
            carry = step(t * unroll + u, carry)
        return carry

    carry = lax.fori_loop(0, n // unroll, trip, init) if n >= unroll else init
    for r in range(n - n % unroll, n):
        carry = step(r, carry)
    return carry


def _heads_to_rows(x, n_heads):
    out = []
    for h in range(n_heads):
        grp = x[:, (h // 2) * LANES:(h // 2 + 1) * LANES]
        out.append(grp if h % 2 == 0 else pltpu.roll(grp, 64, 1))
    return out


def _rows_to_heads(blocks):
    outs = []
    lane = _lane(blocks[0].shape)
    for a in range(len(blocks) // 2):
        outs.append(jnp.where(lane < 64, blocks[2 * a], pltpu.roll(blocks[2 * a + 1], 64, 1)))
    return outs[0] if len(outs) == 1 else jnp.concatenate(outs, axis=1)


def _norm_mod(x, g, mod, i_shift, i_scale, n_lat):
    T, D = x.shape
    tm = _tile(T, 528)

    def body(x_ref, g_ref, mod_ref, h_ref):
        def fn(r0, r1, seg):
            xv = x_ref[r0:r1, :]
            r = lax.rsqrt(jnp.mean(xv * xv, axis=-1, keepdims=True) + EPS)
            n = xv * r * g_ref[...]
            h = n * (1.0 + mod_ref[seg, i_scale:i_scale + 1, :]) + mod_ref[seg, i_shift:i_shift + 1, :]
            h_ref[r0:r1, :] = h.astype(BF16)

        _segments(pl.program_id(0), tm, n_lat, fn)

    return pl.pallas_call(
        body, grid=(T // tm,), name="norm_mod",
        in_specs=[pl.BlockSpec((tm, D), lambda i: (i, 0)), _full(g.shape), _full(mod.shape)],
        out_specs=pl.BlockSpec((tm, D), lambda i: (i, 0)),
        out_shape=jax.ShapeDtypeStruct((T, D), BF16), compiler_params=_cp("parallel"))(x, g, mod)


def _norm_mod_bwd(x, dh, dres, g, mod, i_scale, n_lat):
    T, D = x.shape
    tm = _tile(T, 528)

    def body(x_ref, dh_ref, dres_ref, g_ref, mod_ref, dx_ref, dg_ref, dsh_ref, dsc_ref):
        i = pl.program_id(0)

        @pl.when(i == 0)
        def _():
            dg_ref[...] = jnp.zeros_like(dg_ref)
            dsh_ref[...] = jnp.zeros_like(dsh_ref)
            dsc_ref[...] = jnp.zeros_like(dsc_ref)

        def fn(r0, r1, seg):
            xv = x_ref[r0:r1, :]
            dh = dh_ref[r0:r1, :]
            r = lax.rsqrt(jnp.mean(xv * xv, axis=-1, keepdims=True) + EPS)
            xh = xv * r
            gv = g_ref[...]
            dsh_ref[seg] += jnp.sum(dh, axis=0, keepdims=True)
            dsc_ref[seg] += jnp.sum(dh * (xh * gv), axis=0, keepdims=True)
            dn = dh * (1.0 + mod_ref[seg, i_scale:i_scale + 1, :])
            dg_ref[...] += jnp.sum(dn * xh, axis=0, keepdims=True)
            gd = gv * dn
            dx_ref[r0:r1, :] = dres_ref[r0:r1, :] + r * (gd - xh * jnp.mean(xh * gd, axis=-1, keepdims=True))

        _segments(i, tm, n_lat, fn)

    row = pl.BlockSpec((tm, D), lambda i: (i, 0))
    return pl.pallas_call(
        body, grid=(T // tm,), name="norm_mod_bwd",
        in_specs=[row, row, row, _full(g.shape), _full(mod.shape)],
        out_specs=[row, _full((1, D)), _full((2, 1, D)), _full((2, 1, D))],
        out_shape=[jax.ShapeDtypeStruct((T, D), F32), jax.ShapeDtypeStruct((1, D), F32),
                   jax.ShapeDtypeStruct((2, 1, D), F32), jax.ShapeDtypeStruct((2, 1, D), F32)],
        compiler_params=_cp("arbitrary"))(x, dh, dres, g, mod)


def _gate_bwd(dx, f, mod, i_gate, n_lat):
    T, D = dx.shape
    tm = _tile(T, 528)

    def body(dx_ref, f_ref, mod_ref, dy_ref, dg_ref):
        i = pl.program_id(0)

        @pl.when(i == 0)
        def _():
            dg_ref[...] = jnp.zeros_like(dg_ref)

        def fn(r0, r1, seg):
            dxv = dx_ref[r0:r1, :]
            dy_ref[r0:r1, :] = (dxv * mod_ref[seg, i_gate:i_gate + 1, :]).astype(BF16)
            dg_ref[seg] += jnp.sum(dxv * f_ref[r0:r1, :], axis=0, keepdims=True)

        _segments(i, tm, n_lat, fn)

    row = pl.BlockSpec((tm, D), lambda i: (i, 0))
    return pl.pallas_call(
        body, grid=(T // tm,), name="gate_bwd",
        in_specs=[row, row, _full(mod.shape)], out_specs=[row, _full((2, 1, D))],
        out_shape=[jax.ShapeDtypeStruct((T, D), BF16), jax.ShapeDtypeStruct((2, 1, D), F32)],
        compiler_params=_cp("arbitrary"))(dx, f, mod)


def _mm_nn(a, w, out_dtype, name):
    M, K = a.shape
    J, _, n = w.shape
    tm = _tile(M, 1056)

    def body(a_ref, w_ref, o_ref):
        o_ref[...] = _dot(a_ref[...], w_ref[...]).astype(o_ref.dtype)

    return pl.pallas_call(
        body, grid=(M // tm, J), name=name,
        in_specs=[pl.BlockSpec((tm, K), lambda i, j: (i, 0)), pl.BlockSpec((None, K, n), lambda i, j: (j, 0, 0))],
        out_specs=pl.BlockSpec((tm, n), lambda i, j: (i, j)),
        out_shape=jax.ShapeDtypeStruct((M, J * n), out_dtype), compiler_params=_cp("parallel", "arbitrary"))(a, w)


def _mm_res(a3, w, res, mod, i_gate, n_lat, name):
    J, M, k = a3.shape
    N = w.shape[2]
    tm = _tile(M, 528)

    def body(a_ref, w_ref, res_ref, mod_ref, x_ref, f_ref):
        acc = _dot(a_ref[0], w_ref[0])
        for j in range(1, J):
            acc += _dot(a_ref[j], w_ref[j])
        f_ref[...] = acc

        def fn(r0, r1, seg):
            x_ref[r0:r1, :] = res_ref[r0:r1, :] + mod_ref[seg, i_gate:i_gate + 1, :] * f_ref[r0:r1, :]

        _segments(pl.program_id(0), tm, n_lat, fn)

    row = pl.BlockSpec((tm, N), lambda i: (i, 0))
    return pl.pallas_call(
        body, grid=(M // tm,), name=name,
        in_specs=[pl.BlockSpec((J, tm, k), lambda i: (0, i, 0)), _full(w.shape), row, _full(mod.shape)],
        out_specs=[row, row],
        out_shape=[jax.ShapeDtypeStruct((M, N), F32), jax.ShapeDtypeStruct((M, N), F32)],
        compiler_params=_cp("parallel"))(a3, w, res, mod)


def _mm_nt_acc(dys, ws, row_major, name):
    J, K, n = ws[0].shape
    M = dys[0].shape[0] if row_major else dys[0].shape[1]
    tm = _tile(M, 1056)
    P = len(dys)

    def body(*refs):
        o_ref = refs[2 * P]
        j = pl.program_id(1)
        part = _dg(refs[0][...], refs[P][...], NT)
        for p in range(1, P):
            part += _dg(refs[p][...], refs[P + p][...], NT)

        @pl.when(j == 0)
        def _():
            o_ref[...] = part

        @pl.when(j > 0)
        def _():
            o_ref[...] += part

    dy_spec = (pl.BlockSpec((tm, n), lambda i, j: (i, j)) if row_major
               else pl.BlockSpec((None, tm, n), lambda i, j: (j, i, 0)))
    w_spec = pl.BlockSpec((None, K, n), lambda i, j: (j, 0, 0))
    return pl.pallas_call(
        body, grid=(M // tm, J), name=name,
        in_specs=[dy_spec] * P + [w_spec] * P,
        out_specs=pl.BlockSpec((tm, K), lambda i, j: (i, 0)),
        out_shape=jax.ShapeDtypeStruct((M, K), F32), compiler_params=_cp("parallel", "arbitrary"))(*dys, *ws)


def _mm_tn(x, dy, x_spec, dy_spec, J, K, n, T, name):
    tk = _tile(T, 1056)

    def body(x_ref, dy_ref, o_ref):
        t = pl.program_id(1)
        part = _dg(x_ref[...], dy_ref[...], TN)

        @pl.when(t == 0)
        def _():
            o_ref[...] = part

        @pl.when(t > 0)
        def _():
            o_ref[...] += part

    return pl.pallas_call(
        body, grid=(J, T // tk), name=name,
        in_specs=[x_spec(tk), dy_spec(tk)],
        out_specs=pl.BlockSpec((None, K, n), lambda j, t: (j, 0, 0)),
        out_shape=jax.ShapeDtypeStruct((J, K, n), F32), compiler_params=_cp("parallel", "arbitrary"))(x, dy)


def _rows(width):
    return lambda tk: pl.BlockSpec((tk, width), lambda j, t: (t, 0))


def _row_cols(width):
    return lambda tk: pl.BlockSpec((tk, width), lambda j, t: (t, j))


def _shard_rows(width):
    return lambda tk: pl.BlockSpec((None, tk, width), lambda j, t: (j, t, 0))


def _ffn_up(h, w1, w3):
    T, D = h.shape
    J, _, n = w1.shape
    tm = _tile(T, 1056)

    def body(h_ref, w1_ref, w3_ref, a1_ref, a3_ref, act_ref):
        hv = h_ref[...]
        a1 = _dot(hv, w1_ref[...])
        a3 = _dot(hv, w3_ref[...])
        a1_ref[...] = a1
        a3_ref[...] = a3
        act_ref[...] = (a1 * _sigmoid(a1) * a3).astype(BF16)

    w_spec = pl.BlockSpec((None, D, n), lambda i, j: (j, 0, 0))
    o_spec = pl.BlockSpec((None, tm, n), lambda i, j: (j, i, 0))
    return pl.pallas_call(
        body, grid=(T // tm, J), name="ffn_up",
        in_specs=[pl.BlockSpec((tm, D), lambda i, j: (i, 0)), w_spec, w_spec], out_specs=[o_spec] * 3,
        out_shape=[jax.ShapeDtypeStruct((J, T, n), F32)] * 2 + [jax.ShapeDtypeStruct((J, T, n), BF16)],
        compiler_params=_cp("parallel", "arbitrary"))(h, w1, w3)


def _ffn_down_bwd(dy, w2, a1, a3):
    T, D = dy.shape
    J, n, _ = w2.shape
    tm = _tile(T, 1056)

    def body(dy_ref, w2_ref, a1_ref, a3_ref, da1_ref, da3_ref):
        dact = _dg(dy_ref[...], w2_ref[...], NT)
        a1v = a1_ref[...]
        sig = _sigmoid(a1v)
        da3_ref[...] = (dact * a1v * sig).astype(BF16)
        da1_ref[...] = (dact * a3_ref[...] * (sig * (1.0 + a1v * (1.0 - sig)))).astype(BF16)

    a_spec = pl.BlockSpec((None, tm, n), lambda i, j: (j, i, 0))
    return pl.pallas_call(
        body, grid=(T // tm, J), name="ffn_down_bwd",
        in_specs=[pl.BlockSpec((tm, D), lambda i, j: (i, 0)), pl.BlockSpec((None, n, D), lambda i, j: (j, 0, 0)),
                  a_spec, a_spec],
        out_specs=[a_spec, a_spec], out_shape=[jax.ShapeDtypeStruct((J, T, n), BF16)] * 2,
        compiler_params=_cp("parallel", "arbitrary"))(dy, w2, a1, a3)


def _qkv_prep(p, cos, sin, qg, kg, e):
    T = p.shape[0]
    tm = _tile(T, 528)

    def body(p_ref, cos_ref, sin_ref, qg_ref, kg_ref, e_ref, q_ref, k_ref, v_ref):
        ev = e_ref[...]
        cv, sv = cos_ref[...], sin_ref[...]
        xq = p_ref[:, 0:512]
        qn = xq * lax.rsqrt(_seg_mean(xq * xq, ev) + EPS) * qg_ref[...]
        qr = _rope(qn, cv, sv, False) * (HEAD_DIM ** -0.5)
        xk = p_ref[:, 512:640]
        kn = xk * lax.rsqrt(_seg_mean(xk * xk, ev) + EPS) * kg_ref[...]
        kr = _rope(kn, cv, sv, False)
        lane = _lane((tm, LANES))
        ones = jnp.where(lane < 64 + AUG, -1.0, 0.0)
        for h, blk in enumerate(_heads_to_rows(qr, N_Q_HEADS)):
            q_ref[h] = jnp.where(lane < 64, blk, 0.0).astype(BF16)
        for h, blk in enumerate(_heads_to_rows(kr, N_KV_HEADS)):
            k_ref[h] = jnp.where(lane < 64, blk, ones).astype(BF16)
        for h, blk in enumerate(_heads_to_rows(p_ref[:, 640:768], N_KV_HEADS)):
            v_ref[h] = jnp.where(lane < 64, blk, ones).astype(BF16)

    tab = pl.BlockSpec((tm, LANES), lambda i: (i, 0))
    return pl.pallas_call(
        body, grid=(T // tm,), name="qkv_prep",
        in_specs=[pl.BlockSpec((tm, QKV_W), lambda i: (i, 1)), tab, tab, _full(qg.shape), _full(kg.shape),
                  _full(e.shape)],
        out_specs=[pl.BlockSpec((N_Q_HEADS, tm, LANES), lambda i: (0, i, 0)),
                   pl.BlockSpec((N_KV_HEADS, tm, LANES), lambda i: (0, i, 0)),
                   pl.BlockSpec((N_KV_HEADS, tm, LANES), lambda i: (0, i, 0))],
        out_shape=[jax.ShapeDtypeStruct((N_Q_HEADS, T, LANES), BF16),
                   jax.ShapeDtypeStruct((N_KV_HEADS, T, LANES), BF16),
                   jax.ShapeDtypeStruct((N_KV_HEADS, T, LANES), BF16)],
        compiler_params=_cp("parallel"))(p, cos, sin, qg, kg, e)


def _qkv_prep_bwd(dp, dq, dk, dv, p, cos, sin, qg, kg, e, fold):
    T = p.shape[0]
    tm = _tile(T, 528)
    nt = T // tm

    def body(dp_in, dq_ref, dk_ref, dv_ref, p_ref, cos_ref, sin_ref, qg_ref, kg_ref, e_ref, fold_ref,
             dp_ref, dqg_ref, dkg_ref, accq, acck):
        del dp_in
        i = pl.program_id(0)

        @pl.when(i == 0)
        def _():
            accq[...] = jnp.zeros_like(accq)
            acck[...] = jnp.zeros_like(acck)

        ev = e_ref[...]
        cv, sv = cos_ref[...], sin_ref[...]

        def one(x, dr, gain, acc):
            r = lax.rsqrt(_seg_mean(x * x, ev) + EPS)
            xh = x * r
            dn = _rope(dr, cv, sv, True)
            acc[0:1, :] += jnp.sum(dn * xh, axis=0, keepdims=True)
            gd = gain * dn
            return r * (gd - xh * _seg_mean(xh * gd, ev))

        dqr = _rows_to_heads([dq_ref[h] for h in range(N_Q_HEADS)]) * (HEAD_DIM ** -0.5)
        dkr = _rows_to_heads([dk_ref[h] for h in range(N_KV_HEADS)])
        dvv = _rows_to_heads([dv_ref[h] for h in range(N_KV_HEADS)])
        dp_ref[:, 0:512] = one(p_ref[:, 0:512], dqr, qg_ref[...], accq).astype(BF16)
        dp_ref[:, 512:640] = one(p_ref[:, 512:640], dkr, kg_ref[...], acck).astype(BF16)
        dp_ref[:, 640:768] = dvv.astype(BF16)

        @pl.when(i == nt - 1)
        def _():
            fv = fold_ref[...]
            dqg_ref[...] = jnp.dot(accq[...], fv, preferred_element_type=F32, precision=lax.Precision.HIGHEST)
            dkg_ref[...] = jnp.dot(acck[...], fv[0:LANES, :], preferred_element_type=F32,
                                   precision=lax.Precision.HIGHEST)

    tab = pl.BlockSpec((tm, LANES), lambda i: (i, 0))
    sec = pl.BlockSpec((tm, QKV_W), lambda i: (i, 1))
    return pl.pallas_call(
        body, grid=(nt,), name="qkv_prep_bwd",
        in_specs=[ANY, pl.BlockSpec((N_Q_HEADS, tm, LANES), lambda i: (0, i, 0)),
                  pl.BlockSpec((N_KV_HEADS, tm, LANES), lambda i: (0, i, 0)),
                  pl.BlockSpec((N_KV_HEADS, tm, LANES), lambda i: (0, i, 0)),
                  sec, tab, tab, _full(qg.shape), _full(kg.shape), _full(e.shape), _full(fold.shape)],
        out_specs=[sec, _full((8, LANES)), _full((8, LANES))],
        out_shape=[jax.ShapeDtypeStruct(dp.shape, BF16), jax.ShapeDtypeStruct((8, LANES), F32),
                   jax.ShapeDtypeStruct((8, LANES), F32)],
        scratch_shapes=[pltpu.VMEM((8, 512), F32), pltpu.VMEM((8, LANES), F32)],
        input_output_aliases={0: 0}, compiler_params=_cp("arbitrary"))(dp, dq, dk, dv, p, cos, sin, qg, kg, e, fold)


def _flash_fwd(q, k, v, n_lat):
    _, _, T, _ = q.shape
    tq = tk = 256
    nq = T // tq
    M = GROUP * tq

    nk_lat = n_lat // tk

    def body(q_ref, k_ref, v_ref, o_ref, qa_ref):
        i = pl.program_id(0)
        qv = q_ref[...].reshape(M, LANES)

        def step(s, carry):
            m, acc = carry
            r0 = s * tk if isinstance(s, int) else pl.multiple_of(s * tk, tk)
            sc = _dg(qv, k_ref[pl.ds(r0, tk), :], NT)
            m_new = jnp.maximum(m, jnp.max(sc, axis=1, keepdims=True))
            pr = jnp.exp(sc - m_new)
            acc = jnp.exp(m - m_new) * acc + _dot(pr.astype(BF16), v_ref[pl.ds(r0, tk), :])
            return m_new, acc

        def finish(m, acc):
            den = -acc[:, 64:65]
            out = acc / den
            o_ref[...] = _rows_to_heads([out[g * tq:(g + 1) * tq] for g in range(GROUP)]).astype(BF16)
            qa_ref[...] = _aug(qv.astype(F32), m + jnp.log(den)).astype(BF16).reshape(GROUP, tq, LANES)

        init = (jnp.full((M, 1), -1e30, F32), jnp.zeros((M, LANES), F32))

        @pl.when(i < n_lat // tq)
        def _():
            carry = _loop_unrolled(nk_lat, step, init, UNROLL_FWD)
            for s in range(nk_lat, T // tk):
                carry = step(s, carry)
            finish(*carry)

        @pl.when(i >= n_lat // tq)
        def _():
            carry = init
            for s in range(nk_lat, T // tk):
                carry = step(s, carry)
            finish(*carry)

    q_spec = pl.BlockSpec((None, GROUP, tq, LANES), lambda i, h: (h, 0, i, 0))
    kv_spec = pl.BlockSpec((None, T, LANES), lambda i, h: (h, 0, 0))
    return pl.pallas_call(
        body, grid=(nq, N_KV_HEADS), name="flash_fwd",
        in_specs=[q_spec, kv_spec, kv_spec],
        out_specs=[pl.BlockSpec((tq, GROUP * HEAD_DIM), lambda i, h: (i, h)), q_spec],
        out_shape=[jax.ShapeDtypeStruct((T, N_Q_HEADS * HEAD_DIM), BF16), jax.ShapeDtypeStruct(q.shape, BF16)],
        compiler_params=_cp("parallel", "arbitrary"))(q, k, v)


def _flash_bwd(qa, doa, k, v, n_lat):
    _, _, T, _ = qa.shape
    tq = tk = 256
    nkv = T // tk
    M = GROUP * tq

    def body(qa_hbm, doa_hbm, k_ref, v_ref, dq_hbm, dk_ref, dv_ref, q_sc, do_sc, dq_sc, sems):
        h = pl.program_id(0)
        j = pl.program_id(1)

        @pl.when(j == 0)
        def _():
            c1 = pltpu.make_async_copy(qa_hbm.at[h], q_sc, sems.at[0])
            c2 = pltpu.make_async_copy(doa_hbm.at[h], do_sc, sems.at[1])
            c1.start()
            c2.start()
            dq_sc[...] = jnp.zeros_like(dq_sc)
            c1.wait()
            c2.wait()

        kb = k_ref[...]
        vb = v_ref[...]

        def step(i, carry):
            dk, dv = carry
            r0 = i * tq if isinstance(i, int) else pl.multiple_of(i * tq, tq)
            qv = q_sc[:, pl.ds(r0, tq), :].reshape(M, LANES)
            dov = do_sc[:, pl.ds(r0, tq), :].reshape(M, LANES)
            pr = jnp.exp(_dg(qv, kb, NT))
            ds = (pr * _dg(dov, vb, NT)).astype(BF16)
            dv = dv + _dg(pr.astype(BF16), dov, TN)
            dk = dk + _dg(ds, qv, TN)
            dq_sc[:, pl.ds(r0, tq), :] += _dot(ds, kb).reshape(GROUP, tq, LANES)
            return dk, dv

        z = jnp.zeros((tk, LANES), F32)
        carry = _loop_unrolled(n_lat // tq, step, (z, z), UNROLL_BWD)
        dk_ref[...] = carry[0]
        dv_ref[...] = carry[1]

        @pl.when(j >= n_lat // tk)
        def _():
            c = (dk_ref[...], dv_ref[...])
            for i in range(n_lat // tq, T // tq):
                c = step(i, c)
            dk_ref[...] = c[0]
            dv_ref[...] = c[1]

        @pl.when(j == nkv - 1)
        def _():
            c3 = pltpu.make_async_copy(dq_sc, dq_hbm.at[h], sems.at[2])
            c3.start()
            c3.wait()

    kv_spec = pl.BlockSpec((None, tk, LANES), lambda h, j: (h, j, 0))
    return pl.pallas_call(
        body, grid=(N_KV_HEADS, nkv), name="flash_bwd",
        in_specs=[ANY, ANY, kv_spec, kv_spec], out_specs=[ANY, kv_spec, kv_spec],
        out_shape=[jax.ShapeDtypeStruct(qa.shape, F32), jax.ShapeDtypeStruct(k.shape, F32),
                   jax.ShapeDtypeStruct(k.shape, F32)],
        scratch_shapes=[pltpu.VMEM((GROUP, T, LANES), BF16), pltpu.VMEM((GROUP, T, LANES), BF16),
                        pltpu.VMEM((GROUP, T, LANES), F32), pltpu.SemaphoreType.DMA((3,))],
        compiler_params=_cp("arbitrary", "arbitrary"))(qa, doa, k, v)


def _conv_masks(i, tm, n_lat, T):
    row = lax.broadcasted_iota(jnp.int32, (tm, 1), 0)
    g = row + i * tm
    return row, (g == 0) | (g == n_lat), (g == n_lat - 1) | (g == T - 1)


def _shift_rows(v, prev_row, next_row, row, first, last):
    tm = v.shape[0]
    down = jnp.where(row == 0, prev_row, pltpu.roll(v, 1, 0))
    up = jnp.where(row == tm - 1, next_row, pltpu.roll(v, tm - 1, 0))
    return jnp.where(first, 0.0, down), jnp.where(last, 0.0, up)


def _halo_specs(tm, T, width, col):
    nb = T // 8
    return (pl.BlockSpec((8, width), lambda i: (jnp.maximum(i * (tm // 8) - 1, 0), col)),
            pl.BlockSpec((8, width), lambda i: (jnp.minimum((i + 1) * (tm // 8), nb - 1), col)))


def _conv_fwd(p, cw, n_lat):
    T = p.shape[0]
    tm = _tile(T, 1056)

    def body(p_ref, pp_ref, pn_ref, cw_ref, o_ref):
        row, first, last = _conv_masks(pl.program_id(0), tm, n_lat, T)
        z = p_ref[:, 256:512] * p_ref[:, 512:768]
        zp = pp_ref[7:8, 256:512] * pp_ref[7:8, 512:768]
        zn = pn_ref[0:1, 256:512] * pn_ref[0:1, 512:768]
        zd, zu = _shift_rows(z, zp, zn, row, first, last)
        conv = cw_ref[0:1, :] * zd + cw_ref[1:2, :] * z + cw_ref[2:3, :] * zu
        o_ref[...] = (p_ref[:, 0:256] * conv).astype(BF16)

    prev, nxt = _halo_specs(tm, T, 768, 0)
    return pl.pallas_call(
        body, grid=(T // tm,), name="conv_fwd",
        in_specs=[pl.BlockSpec((tm, 768), lambda i: (i, 0)), prev, nxt, _full(cw.shape)],
        out_specs=pl.BlockSpec((tm, CONV_W), lambda i: (i, 0)),
        out_shape=jax.ShapeDtypeStruct((T, CONV_W), BF16), compiler_params=_cp("parallel"))(p, p, p, cw)


def _conv_bwd(dp, dy, p, cw, n_lat):
    T = p.shape[0]
    tm = _tile(T, 1056)

    def body(dp_in, dy_ref, dyp_ref, dyn_ref, p_ref, pp_ref, pn_ref, cw_ref, dp_ref, dcw_ref):
        del dp_in
        i = pl.program_id(0)

        @pl.when(i == 0)
        def _():
            dcw_ref[...] = jnp.zeros_like(dcw_ref)

        row, first, last = _conv_masks(i, tm, n_lat, T)
        ab, ac, ax = p_ref[:, 0:256], p_ref[:, 256:512], p_ref[:, 512:768]
        z = ac * ax
        zp = pp_ref[7:8, 256:512] * pp_ref[7:8, 512:768]
        zn = pn_ref[0:1, 256:512] * pn_ref[0:1, 512:768]
        zd, zu = _shift_rows(z, zp, zn, row, first, last)
        w0, w1, w2 = cw_ref[0:1, :], cw_ref[1:2, :], cw_ref[2:3, :]
        dy = dy_ref[...]
        dc = dy * ab
        dcd, dcu = _shift_rows(dc, dyp_ref[7:8, :] * pp_ref[7:8, 0:256], dyn_ref[0:1, :] * pn_ref[0:1, 0:256],
                               row, first, last)
        dz = w0 * dcu + w1 * dc + w2 * dcd
        dp_ref[:, 0:256] = (dy * (w0 * zd + w1 * z + w2 * zu)).astype(BF16)
        dp_ref[:, 256:512] = (dz * ax).astype(BF16)
        dp_ref[:, 512:768] = (dz * ac).astype(BF16)
        dcw_ref[0:1, :] += jnp.sum(dc * zd, axis=0, keepdims=True)
        dcw_ref[1:2, :] += jnp.sum(dc * z, axis=0, keepdims=True)
        dcw_ref[2:3, :] += jnp.sum(dc * zu, axis=0, keepdims=True)

    prev, nxt = _halo_specs(tm, T, 768, 0)
    dprev, dnxt = _halo_specs(tm, T, CONV_W, 0)
    sec = pl.BlockSpec((tm, 768), lambda i: (i, 0))
    return pl.pallas_call(
        body, grid=(T // tm,), name="conv_bwd",
        in_specs=[ANY, pl.BlockSpec((tm, CONV_W), lambda i: (i, 0)), dprev, dnxt, sec, prev, nxt, _full(cw.shape)],
        out_specs=[sec, _full((8, CONV_W))],
        out_shape=[jax.ShapeDtypeStruct(dp.shape, BF16), jax.ShapeDtypeStruct((8, CONV_W), F32)],
        input_output_aliases={0: 0}, compiler_params=_cp("arbitrary"))(dp, dy, dy, dy, p, p, p, cw)


def _gmlp_mix(bd_ref, vs, grp):
    out = jnp.zeros((2 * CHUNK, SG_W), F32)
    for g in range(4):
        out = jnp.where(grp == g, _dot(bd_ref[g], vs), out)
    return out


def _gmlp_fwd(p, sgn, bd, bias):
    T = p.shape[0]
    tm = _tile(T, 768, 2 * CHUNK)

    def body(p_ref, sgn_ref, bd_ref, bias_ref, o_ref):
        x = _gelu(p_ref[:, 256:512])
        vn = (x * lax.rsqrt(jnp.mean(x * x, axis=-1, keepdims=True) + EPS) * sgn_ref[...]).astype(BF16)
        grp = _lane((2 * CHUNK, SG_W)) // 64
        for s in range(tm // (2 * CHUNK)):
            rs = slice(s * 2 * CHUNK, (s + 1) * 2 * CHUNK)
            mixed = _gmlp_mix(bd_ref, vn[rs], grp) + bias_ref[...]
            o_ref[rs, :] = (_gelu(p_ref[rs, 0:256]) * mixed).astype(BF16)

    return pl.pallas_call(
        body, grid=(T // tm,), name="gmlp_fwd",
        in_specs=[pl.BlockSpec((tm, 2 * SG_W), lambda i: (i, 3)), _full(sgn.shape), _full(bd.shape),
                  _full(bias.shape)],
        out_specs=pl.BlockSpec((tm, SG_W), lambda i: (i, 0)),
        out_shape=jax.ShapeDtypeStruct((T, SG_W), BF16), compiler_params=_cp("parallel"))(p, sgn, bd, bias)


def _gmlp_bwd(dp, dy, p, sgn, bd, bdt, bias, gsum):
    T = p.shape[0]
    tm = _tile(T, 768, 2 * CHUNK)
    nt = T // tm
    C2 = 2 * CHUNK

    def body(dp_in, dy_ref, p_ref, sgn_ref, bd_ref, bdt_ref, bias_ref, gsum_ref,
             dp_ref, dsg_ref, dws_ref, dbs_ref, acc_w, acc_b):
        del dp_in
        i = pl.program_id(0)

        @pl.when(i == 0)
        def _():
            dsg_ref[...] = jnp.zeros_like(dsg_ref)
            acc_w[...] = jnp.zeros_like(acc_w)
            acc_b[...] = jnp.zeros_like(acc_b)

        u = p_ref[:, 0:256]
        sv = p_ref[:, 256:512]
        ug = _gelu(u)
        x = _gelu(sv)
        r = lax.rsqrt(jnp.mean(x * x, axis=-1, keepdims=True) + EPS)
        xh = x * r
        sg = sgn_ref[...]
        vn = (xh * sg).astype(BF16)
        grp = _lane((C2, SG_W)) // 64
        dug, dvn = [], []
        for s in range(tm // C2):
            rs = slice(s * C2, (s + 1) * C2)
            vs = vn[rs]
            dys = dy_ref[rs, :]
            dug.append(dys * (_gmlp_mix(bd_ref, vs, grp) + bias_ref[...]))
            dmix = dys * ug[rs]
            acc_b[...] += dmix
            dmb = dmix.astype(BF16)
            dvn.append(_gmlp_mix(bdt_ref, dmb, grp))
            for g in range(4):
                acc_w[g] += _dg(jnp.where(grp == g, dmb, jnp.zeros_like(dmb)), vs, NT)
        dug = jnp.concatenate(dug, axis=0)
        dvn = jnp.concatenate(dvn, axis=0)
        dsg_ref[...] += jnp.sum(dvn * xh, axis=0, keepdims=True)
        gd = sg * dvn
        dx = r * (gd - xh * jnp.mean(xh * gd, axis=-1, keepdims=True))
        dp_ref[:, 0:256] = (dug * _gelu_grad(u)).astype(BF16)
        dp_ref[:, 256:512] = (dx * _gelu_grad(sv)).astype(BF16)

        @pl.when(i == nt - 1)
        def _():
            for g in range(4):
                dws_ref[g] = acc_w[g, 0:CHUNK, 0:CHUNK] + acc_w[g, CHUNK:C2, CHUNK:C2]
            dbs_ref[...] = jnp.dot(acc_b[0:CHUNK, :] + acc_b[CHUNK:C2, :], gsum_ref[...],
                                   preferred_element_type=F32, precision=lax.Precision.HIGHEST)

    sec = pl.BlockSpec((tm, 2 * SG_W), lambda i: (i, 3))
    return pl.pallas_call(
        body, grid=(nt,), name="gmlp_bwd",
        in_specs=[ANY, pl.BlockSpec((tm, SG_W), lambda i: (i, 0)), sec, _full(sgn.shape), _full(bd.shape),
                  _full(bdt.shape), _full(bias.shape), _full(gsum.shape)],
        out_specs=[sec, _full((1, SG_W)), _full((4, CHUNK, CHUNK)), _full((CHUNK, LANES))],
        out_shape=[jax.ShapeDtypeStruct(dp.shape, BF16), jax.ShapeDtypeStruct((1, SG_W), F32),
                   jax.ShapeDtypeStruct((4, CHUNK, CHUNK), F32), jax.ShapeDtypeStruct((CHUNK, LANES), F32)],
        scratch_shapes=[pltpu.VMEM((4, C2, C2), F32), pltpu.VMEM((C2, SG_W), F32)],
        input_output_aliases={0: 0}, compiler_params=_cp("arbitrary"))(dp, dy, p, sgn, bd, bdt, bias, gsum)


def _merge_fwd(ya, at, yc, p, wa, wb, wc):
    T = p.shape[0]
    tm = _tile(T, 528)
    n = wa.shape[2]

    def body(ya_ref, at_ref, yc_ref, ga_ref, gb_ref, gc_ref, wa_ref, wb_ref, wc_ref, o_ref):
        yav, atv, ycv = ya_ref[...], at_ref[...], yc_ref[...]
        for j in range(N_CHIPS):
            cs = slice(j * n, (j + 1) * n)
            m = (_sigmoid(ga_ref[:, cs]) * _dot(yav, wa_ref[j]) + _sigmoid(gb_ref[:, cs]) * _dot(atv, wb_ref[j])
                 + _sigmoid(gc_ref[:, cs]) * _dot(ycv, wc_ref[j]))
            o_ref[:, cs] = m.astype(BF16)

    def rows(w, col=0):
        return pl.BlockSpec((tm, w), lambda i: (i, col))

    return pl.pallas_call(
        body, grid=(T // tm,), name="merge_fwd",
        in_specs=[rows(CONV_W), rows(512), rows(SG_W), rows(D_MODEL, 2), rows(D_MODEL, 3), rows(D_MODEL, 4),
                  _full(wa.shape), _full(wb.shape), _full(wc.shape)],
        out_specs=rows(D_MODEL), out_shape=jax.ShapeDtypeStruct((T, D_MODEL), BF16),
        compiler_params=_cp("parallel"))(ya, at, yc, p, p, p, wa, wb, wc)


def _merge_bwd(dyo, ya, at, yc, p, wa, wb, wc, wo):
    T = p.shape[0]
    tm = _tile(T, 528)
    n = wa.shape[2]

    def body(dyo_ref, ya_ref, at_ref, yc_ref, ga_ref, gb_ref, gc_ref, wa_ref, wb_ref, wc_ref, wo_ref,
             dp_ref, dya_ref, doa_ref, dyc_ref, dwa_ref, dwb_ref, dwc_ref):
        i = pl.program_id(0)

        @pl.when(i == 0)
        def _():
            dwa_ref[...] = jnp.zeros_like(dwa_ref)
            dwb_ref[...] = jnp.zeros_like(dwb_ref)
            dwc_ref[...] = jnp.zeros_like(dwc_ref)

        dp_ref[:, 0:OFF_G] = jnp.zeros((tm, OFF_G), BF16)
        dm = _dg(dyo_ref[...], wo_ref[...], NT)
        yav, atv, ycv = ya_ref[...], at_ref[...], yc_ref[...]
        dya = jnp.zeros((tm, CONV_W), F32)
        dat = jnp.zeros((tm, 512), F32)
        dyc = jnp.zeros((tm, SG_W), F32)
        for j in range(N_CHIPS):
            cs = slice(j * n, (j + 1) * n)
            dmj = dm[:, cs]
            for y_in, w_ref, g_ref, dw_ref, which in (
                    (yav, wa_ref, ga_ref, dwa_ref, 0), (atv, wb_ref, gb_ref, dwb_ref, 1),
                    (ycv, wc_ref, gc_ref, dwc_ref, 2)):
                sg = _sigmoid(g_ref[:, cs])
                y = _dot(y_in, w_ref[j])
                c0 = OFF_G + which * D_MODEL + j * n
                dp_ref[:, c0:c0 + n] = (dmj * y * sg * (1.0 - sg)).astype(BF16)
                dyb = (dmj * sg).astype(BF16)
                dw_ref[j] += _dg(y_in, dyb, TN)
                back = _dg(dyb, w_ref[j], NT)
                if which == 0:
                    dya = dya + back
                elif which == 1:
                    dat = dat + back
                else:
                    dyc = dyc + back
        dya_ref[...] = dya
        dyc_ref[...] = dyc
        prod = dat * atv.astype(F32)
        lane = _lane((tm, LANES))
        dat_rows = _heads_to_rows(dat, N_Q_HEADS)
        for h in range(N_Q_HEADS):
            grp = prod[:, (h // 2) * LANES:(h // 2 + 1) * LANES]
            keep = (lane < 64) if h % 2 == 0 else (lane >= 64)
            delta = jnp.sum(jnp.where(keep, grp, 0.0), axis=1, keepdims=True)
            doa_ref[h] = _aug(dat_rows[h], delta).astype(BF16)

    def rows(w, col=0):
        return pl.BlockSpec((tm, w), lambda i: (i, col))

    return pl.pallas_call(
        body, grid=(T // tm,), name="merge_bwd",
        in_specs=[rows(D_MODEL), rows(CONV_W), rows(512), rows(SG_W), rows(D_MODEL, 2), rows(D_MODEL, 3),
                  rows(D_MODEL, 4), _full(wa.shape), _full(wb.shape), _full(wc.shape), _full(wo.shape)],
        out_specs=[rows(IN_W), rows(CONV_W),
                   pl.BlockSpec((N_Q_HEADS, tm, LANES), lambda i: (0, i, 0)), rows(SG_W),
                   _full(wa.shape), _full(wb.shape), _full(wc.shape)],
        out_shape=[jax.ShapeDtypeStruct((T, IN_W), BF16)] + [
            jax.ShapeDtypeStruct((T, CONV_W), F32), jax.ShapeDtypeStruct((N_Q_HEADS, T, LANES), BF16),
            jax.ShapeDtypeStruct((T, SG_W), F32), jax.ShapeDtypeStruct(wa.shape, F32),
            jax.ShapeDtypeStruct(wb.shape, F32), jax.ShapeDtypeStruct(wc.shape, F32)],
        compiler_params=_cp("arbitrary"))(dyo, ya, at, yc, p, p, p, wa, wb, wc, wo)


def _loss_grad(xf, tgt, n_lat):
    T, D = xf.shape
    tm = _tile(np.gcd(n_lat, T), 512)
    nl = n_lat // tm

    def body(x_ref, t_ref, dy_ref, l_ref):
        i = pl.program_id(0)

        @pl.when(i == 0)
        def _():
            l_ref[...] = jnp.zeros_like(l_ref)

        @pl.when(i < nl)
        def _():
            err = x_ref[...] - t_ref[...]
            dy_ref[...] = err * (1.0 / D)
            sq = jnp.sum(jnp.sum(err * err, axis=1, keepdims=True), axis=0, keepdims=True)
            l_ref[...] += (0.5 / D) * sq

        @pl.when(i >= nl)
        def _():
            dy_ref[...] = jnp.zeros_like(dy_ref)

    return pl.pallas_call(
        body, grid=(T // tm,), name="loss_grad",
        in_specs=[pl.BlockSpec((tm, D), lambda i: (i, 0)), pl.BlockSpec((tm, D), lambda i: (jnp.minimum(i, nl - 1), 0))],
        out_specs=[pl.BlockSpec((tm, D), lambda i: (i, 0)), _full((8, LANES))],
        out_shape=[jax.ShapeDtypeStruct((T, D), F32), jax.ShapeDtypeStruct((8, LANES), F32)],
        compiler_params=_cp("arbitrary"))(xf, tgt)


def _row_tile(R, C):
    if R * C <= (1 << 19) or R % 8:
        return R
    return _tile(R, max(8, (1 << 19) // C), 8)


def _adamw(w, m, v, g1, g2=None):
    shape = w.shape
    C = shape[-1]
    R = int(np.prod(shape[:-1])) if len(shape) > 1 else 1
    tr = _row_tile(R, C)
    ins = [a.reshape(R, C) for a in ((w, m, v, g1) if g2 is None else (w, m, v, g1, g2))]

    def body(*refs):
        w_ref, m_ref, v_ref = refs[0], refs[1], refs[2]
        g_ref, d_ref, m2_ref, v2_ref = refs[-4:]
        g = refs[3][...] if g2 is None else refs[3][...] + refs[4][...]
        m2 = ADAM_B1 * m_ref[...] + (1.0 - ADAM_B1) * g
        v2 = ADAM_B2 * v_ref[...] + (1.0 - ADAM_B2) * (g * g)
        m_hat = m2 / (1.0 - ADAM_B1 ** ADAM_STEP)
        v_hat = v2 / (1.0 - ADAM_B2 ** ADAM_STEP)
        g_ref[...] = g
        d_ref[...] = -ADAM_LR * (m_hat / (jnp.sqrt(v_hat) + ADAM_EPS) + ADAM_WD * w_ref[...])
        m2_ref[...] = m2
        v2_ref[...] = v2

    spec = pl.BlockSpec((tr, C), lambda i: (i, 0))
    outs = pl.pallas_call(
        body, grid=(R // tr,), name="adamw", in_specs=[spec] * len(ins), out_specs=[spec] * 4,
        out_shape=[jax.ShapeDtypeStruct((R, C), F32)] * 4, compiler_params=_cp("parallel"))(*ins)
    return [o.reshape(shape) for o in outs]


def _sum_lead(x, name):
    n, R, C = x.shape
    tr = _row_tile(R, C * n)

    def body(x_ref, o_ref):
        acc = x_ref[0].astype(F32)
        for s in range(1, n):
            acc = acc + x_ref[s].astype(F32)
        o_ref[...] = acc

    return pl.pallas_call(
        body, grid=(R // tr,), name=name, in_specs=[pl.BlockSpec((n, tr, C), lambda i: (0, i, 0))],
        out_specs=pl.BlockSpec((tr, C), lambda i: (i, 0)), out_shape=jax.ShapeDtypeStruct((R, C), F32),
        compiler_params=_cp("parallel"))(x)


def _silu(x):
    return x * _sigmoid(x)


def _mod_fwd(a_raw, w_mod, bsh):
    L, D, n = w_mod.shape

    def body(a_ref, w_ref, b_ref, o_ref):
        o_ref[...] = _dot(_silu(a_ref[...]).astype(BF16), w_ref[...].astype(BF16)) + b_ref[...]

    return pl.pallas_call(
        body, grid=(L,), name="mod_fwd",
        in_specs=[_full(a_raw.shape), pl.BlockSpec((None, D, n), lambda l: (l, 0, 0)),
                  pl.BlockSpec((None, 1, n), lambda l: (l, 0, 0))],
        out_specs=pl.BlockSpec((None, 16, n), lambda l: (l, 0, 0)),
        out_shape=jax.ShapeDtypeStruct((L, 16, n), F32), compiler_params=_cp("parallel"))(a_raw, w_mod, bsh)


def _wmod_grad(a_raw, dms):
    L, _, n = dms.shape
    D = a_raw.shape[1]

    def body(a_ref, dm_ref, o_ref):
        o_ref[...] = _dg(_silu(a_ref[...]).astype(BF16), dm_ref[...].astype(BF16), TN)

    return pl.pallas_call(
        body, grid=(L,), name="wmod_grad",
        in_specs=[_full(a_raw.shape), pl.BlockSpec((None, 16, n), lambda l: (l, 0, 0))],
        out_specs=pl.BlockSpec((None, D, n), lambda l: (l, 0, 0)),
        out_shape=jax.ShapeDtypeStruct((L, D, n), F32), compiler_params=_cp("parallel"))(a_raw, dms)


def _cctx_partial(dmc, w_mod):
    L, D, n = w_mod.shape

    def body(dm_ref, w_ref, o_ref):
        part = _dg(dm_ref[...].astype(BF16), w_ref[...].astype(BF16), NT)

        @pl.when(pl.program_id(0) == 0)
        def _():
            o_ref[...] = part

        @pl.when(pl.program_id(0) > 0)
        def _():
            o_ref[...] += part

    return pl.pallas_call(
        body, grid=(L,), name="cctx_partial",
        in_specs=[pl.BlockSpec((None, 16, n), lambda l: (l, 0, 0)), pl.BlockSpec((None, D, n), lambda l: (l, 0, 0))],
        out_specs=_full((16, D)), out_shape=jax.ShapeDtypeStruct((16, D), F32),
        compiler_params=_cp("arbitrary"))(dmc, w_mod)


def _cctx_final(parts, cc):
    def body(p_ref, c_ref, o_ref):
        s = p_ref[0, 0:8, :]
        for j in range(1, N_CHIPS):
            s = s + p_ref[2 * j, 0:8, :]
        xv = c_ref[...]
        sg = _sigmoid(xv)
        o_ref[...] = s * (sg * (1.0 + xv * (1.0 - sg)))

    return pl.pallas_call(
        body, name="cctx_final", in_specs=[_full(parts.shape), _full(cc.shape)], out_specs=_full((8, LANES)),
        out_shape=jax.ShapeDtypeStruct((8, LANES), F32), compiler_params=_cp())(parts, cc)


def _me():
    return lax.axis_index("x"), lax.axis_index("y"), lax.axis_index("c")


def _flip(v, bit):
    return 1 - v if bit else v


def _remote(src, dst, ssem, rsem, peer):
    return pltpu.make_async_remote_copy(src_ref=src, dst_ref=dst, send_sem=ssem, recv_sem=rsem,
                                        device_id=peer, device_id_type=MESH_ID)


def _ag8(xb, name):
    R = xb.shape[0]

    def body(x_ref, o_ref, ssem, rsem, lsem):
        mx, my, mc = _me()
        me = 4 * mx + 2 * my + mc
        loc = pltpu.make_async_copy(x_ref, o_ref.at[me], lsem.at[0])
        loc.start()
        sends = []
        for k in range(1, N_DEV):
            peer = (_flip(mx, k & 4), _flip(my, k & 2), _flip(mc, k & 1))
            cp = _remote(x_ref, o_ref.at[me], ssem.at[k - 1], rsem.at[k - 1], peer)
            cp.start()
            sends.append((cp, peer))
        for k, (cp, peer) in enumerate(sends):
            pid = 4 * peer[0] + 2 * peer[1] + peer[2]
            _remote(x_ref, o_ref.at[pid], ssem.at[k], rsem.at[k], peer).wait_recv()
        for cp, _ in sends:
            cp.wait_send()
        loc.wait()

    return pl.pallas_call(
        body, name=name, in_specs=[ANY], out_specs=ANY, out_shape=jax.ShapeDtypeStruct((N_DEV, R, LANES), F32),
        scratch_shapes=[pltpu.SemaphoreType.DMA((N_DEV - 1,)), pltpu.SemaphoreType.DMA((N_DEV - 1,)),
                        pltpu.SemaphoreType.DMA((1,))])(xb)


def _plane_peers(mx, my, mc):
    out = []
    for k in range(1, N_CHIPS):
        px, py = _flip(mx, k & 2), _flip(my, k & 1)
        out.append(((px, py, mc), 2 * px + py))
    return out


def _chip_gather(bufs, name):
    n = len(bufs)
    halves = [b.shape[1] // 2 for b in bufs]

    def body(*refs):
        outs = refs[n:2 * n]
        ssem, rsem, fsem, gsem = refs[2 * n:]
        mx, my, mc = _me()
        j = 2 * mx + my
        sib = (mx, my, 1 - mc)

        def half(a, blk, c):
            return outs[a].at[blk, pl.ds(c * halves[a], halves[a]), :]

        peers = _plane_peers(mx, my, mc)
        sends = []
        for k, (peer, _) in enumerate(peers):
            for a in range(n):
                mine = half(a, j, mc)
                cp = _remote(mine, mine, ssem.at[k * n + a], rsem.at[k * n + a], peer)
                cp.start()
                sends.append(cp)
        for k, (peer, pj) in enumerate(peers):
            for a in range(n):
                got = half(a, pj, mc)
                _remote(got, got, ssem.at[k * n + a], rsem.at[k * n + a], peer).wait_recv()
                fw = _remote(got, got, fsem.at[k * n + a], gsem.at[k * n + a], sib)
                fw.start()
                sends.append(fw)
        for k, (_, pj) in enumerate(peers):
            for a in range(n):
                theirs = half(a, pj, 1 - mc)
                _remote(theirs, theirs, fsem.at[k * n + a], gsem.at[k * n + a], sib).wait_recv()
        for cp in sends:
            cp.wait_send()

    sems = pltpu.SemaphoreType.DMA((3 * n,))
    return pl.pallas_call(
        body, name=name, in_specs=[ANY] * n, out_specs=[ANY] * n,
        out_shape=[jax.ShapeDtypeStruct(b.shape, b.dtype) for b in bufs],
        input_output_aliases={a: a for a in range(n)},
        scratch_shapes=[sems, sems, sems, sems])(*bufs)


def _chip_scatter(gs, name):
    n = len(gs)

    def body(*refs):
        ins, outs = refs[:n], refs[n:2 * n]
        ssem, rsem = refs[2 * n:]
        mx, my, mc = _me()
        j = 2 * mx + my
        peers = _plane_peers(mx, my, mc)
        sends = []
        for k, (peer, pj) in enumerate(peers):
            for a in range(n):
                cp = _remote(ins[a].at[pj], outs[a].at[j], ssem.at[k * n + a], rsem.at[k * n + a], peer)
                cp.start()
                sends.append(cp)
        for k, (peer, pj) in enumerate(peers):
            for a in range(n):
                _remote(ins[a].at[pj], outs[a].at[pj], ssem.at[k * n + a], rsem.at[k * n + a], peer).wait_recv()
        for cp in sends:
            cp.wait_send()

    return pl.pallas_call(
        body, name=name, in_specs=[ANY] * n, out_specs=[ANY] * n,
        out_shape=[jax.ShapeDtypeStruct(g.shape, g.dtype) for g in gs],
        scratch_shapes=[pltpu.SemaphoreType.DMA((3 * n,)), pltpu.SemaphoreType.DMA((3 * n,))])(*gs)


def _sibling_halves(gs, name):
    n = len(gs)

    def body(*refs):
        ins, outs = refs[:n], refs[n:2 * n]
        ssem, rsem = refs[2 * n:]
        mx, my, mc = _me()
        cps = []
        for a in range(n):
            h = gs[a].shape[1] // 2
            cps.append(_remote(ins[a].at[:, pl.ds((1 - mc) * h, h), :], outs[a], ssem.at[a], rsem.at[a],
                               (mx, my, 1 - mc)))
        for cp in cps:
            cp.start()
        for cp in cps:
            cp.wait()

    return pl.pallas_call(
        body, name=name, in_specs=[ANY] * n, out_specs=[ANY] * n,
        out_shape=[jax.ShapeDtypeStruct((g.shape[0], g.shape[1] // 2, g.shape[2]), g.dtype) for g in gs],
        scratch_shapes=[pltpu.SemaphoreType.DMA((n,)), pltpu.SemaphoreType.DMA((n,))])(*gs)


def _sibling_fill(hs, name):
    n = len(hs)

    def body(*refs):
        ins, outs = refs[:n], refs[n:2 * n]
        ssem, rsem = refs[2 * n:]
        mx, my, mc = _me()
        cps = []
        for a in range(n):
            h = hs[a].shape[0]
            cps.append(_remote(ins[a], outs[a].at[pl.ds(mc * h, h), :], ssem.at[a], rsem.at[a], (mx, my, 1 - mc)))
        for cp in cps:
            cp.start()
        for a, cp in enumerate(cps):
            h = hs[a].shape[0]
            theirs = outs[a].at[pl.ds((1 - mc) * h, h), :]
            _remote(ins[a], theirs, ssem.at[a], rsem.at[a], (mx, my, 1 - mc)).wait_recv()
            cp.wait_send()

    return pl.pallas_call(
        body, name=name, in_specs=[ANY] * n, out_specs=[ANY] * n,
        out_shape=[jax.ShapeDtypeStruct((2 * x.shape[0], x.shape[1]), x.dtype) for x in hs],
        scratch_shapes=[pltpu.SemaphoreType.DMA((n,)), pltpu.SemaphoreType.DMA((n,))])(*hs)


def _add_cast(g, sb):
    J, h, b = g.shape
    th = _row_tile(h, b * J)

    def body(g_ref, s_ref, o_ref):
        o_ref[...] = (g_ref[...] + s_ref[...].astype(F32)).astype(BF16)

    spec = pl.BlockSpec((J, th, b), lambda i: (0, i, 0))
    return pl.pallas_call(
        body, grid=(h // th,), name="add_planes", in_specs=[spec, spec], out_specs=spec,
        out_shape=jax.ShapeDtypeStruct((J, h, b), BF16), compiler_params=_cp("parallel"))(g, sb)


_WEIGHTS = ("c_ctx", "w_mod", "b_mod", "norm1", "w_in", "q_gain", "k_gain", "conv_w", "sg_norm", "w_s", "b_s",
            "w_a", "w_b", "w_c", "w_o", "norm2", "w_ff1", "w_ff3", "w_ff2")
_BIG = ("w_in", "w_a", "w_b", "w_c", "w_o", "w_ff1", "w_ff3", "w_ff2")


def _constants():
    idx = np.arange(LANES)
    e = (idx[:, None] // 64 == idx[None, :] // 64).astype(np.float32) / 64.0
    c512 = np.arange(512)
    fold = (c512[:, None] % 64 == idx[None, :]).astype(np.float32)
    c256 = np.arange(SG_W)
    gsum = (c256[:, None] // 64 == idx[None, :]).astype(np.float32)
    return jnp.asarray(e, BF16), jnp.asarray(fold, F32), jnp.asarray(gsum, F32)


def _rope_tables(n_lat, n_ctx):
    t = jnp.arange(n_lat)
    inv = ROPE_THETA ** (-jnp.arange(0, HEAD_DIM // 2, 2, dtype=F32) / (HEAD_DIM // 2))
    ar = (t // GRID_W).astype(F32)[:, None] * inv
    ac = (t % GRID_W).astype(F32)[:, None] * inv
    cos = jnp.concatenate([jnp.cos(ar), jnp.cos(ar), jnp.cos(ac), jnp.cos(ac)], axis=1)
    sin = jnp.concatenate([-jnp.sin(ar), jnp.sin(ar), -jnp.sin(ac), jnp.sin(ac)], axis=1)
    cos = jnp.concatenate([cos, jnp.ones((n_ctx, HEAD_DIM), F32)], axis=0)
    sin = jnp.concatenate([sin, jnp.zeros((n_ctx, HEAD_DIM), F32)], axis=0)
    return jnp.concatenate([cos, cos], axis=1), jnp.concatenate([sin, sin], axis=1)


def kernel(x, c, ctx, c_ctx, w_mod, b_mod, norm1, w_in, q_gain, k_gain, conv_w, sg_norm, w_s, b_s, w_a, w_b, w_c, w_o, norm2, w_ff1, w_ff3, w_ff2, loss_target, m_c_ctx, m_w_mod, m_b_mod, m_norm1, m_w_in, m_q_gain, m_k_gain, m_conv_w, m_sg_norm, m_w_s, m_b_s, m_w_a, m_w_b, m_w_c, m_w_o, m_norm2, m_w_ff1, m_w_ff3, m_w_ff2, v_c_ctx, v_w_mod, v_b_mod, v_norm1, v_w_in, v_q_gain, v_k_gain, v_conv_w, v_sg_norm, v_w_s, v_b_s, v_w_a, v_w_b, v_w_c, v_w_o, v_norm2, v_w_ff1, v_w_ff3, v_w_ff2):
    given = dict(locals())
    mx, my, mc = _me()
    chip = 2 * mx + my
    dev = 4 * mx + 2 * my + mc
    L = norm1.shape[0]
    S, Lc = x.shape[1], ctx.shape[1]
    T = S + Lc
    D = D_MODEL
    n_mod, n_in, n_ff = w_mod.shape[2], w_in.shape[2], w_ff1.shape[2]
    n_cw = conv_w.shape[2]
    e_avg, fold, gsum = _constants()
    cos_t, sin_t = _rope_tables(S, Lc)

    cw_rows = (L * 3 * n_cw) // LANES
    pad = (-(8 + cw_rows)) % 8
    buf = jnp.concatenate([c.reshape(8, LANES), conv_w.reshape(cw_rows, LANES), jnp.zeros((pad, LANES), F32)], axis=0)
    g1 = _ag8(buf, "gather_cond")
    conds = g1[:, :8].reshape(N_DEV, D)
    cw_full = jnp.stack([g1[2 * j, 8:8 + cw_rows].reshape(L, 3, n_cw) for j in range(N_CHIPS)], axis=2)
    cw_full = cw_full.reshape(L, 3, N_CHIPS * n_cw)
    cw8 = jnp.pad(cw_full, ((0, 0), (0, 5), (0, 0)))
    a_raw = jnp.concatenate([conds, c_ctx[None], jnp.zeros((7, D), F32)], axis=0)
    bsh = lax.dynamic_slice_in_dim(b_mod, chip * n_mod, n_mod, axis=1)[:, None, :]
    mod_sh = _mod_fwd(a_raw, w_mod, bsh)
    g2 = _ag8(mod_sh.reshape(-1, LANES), "gather_mod")
    mods = jnp.stack([g2[2 * j].reshape(L, 16, n_mod) for j in range(N_CHIPS)], axis=2).reshape(L, 16, N_CHIPS * n_mod)
    lat = lax.dynamic_index_in_dim(mods, dev, axis=1, keepdims=False)
    mod = jnp.stack([lat.reshape(L, 6, D), mods[:, 8].reshape(L, 6, D)], axis=1)
    mod = jnp.pad(mod, ((0, 0), (0, 0), (0, 2), (0, 0)))

    qg = jnp.tile(q_gain, (1, N_Q_HEADS))[:, None, :]
    kg = jnp.tile(k_gain, (1, N_KV_HEADS))[:, None, :]
    sgn = sg_norm[:, None, :]
    ws_b = w_s.astype(BF16)
    zero = jnp.zeros_like(ws_b)
    bd = jnp.concatenate([jnp.concatenate([ws_b, zero], axis=3), jnp.concatenate([zero, ws_b], axis=3)], axis=2)
    bdt = jnp.swapaxes(bd, 2, 3)
    bias = jnp.tile(jnp.repeat(jnp.swapaxes(b_s, 1, 2), SG_W // 4, axis=2), (1, 2, 1))

    def gathered(l):
        bufs = [lax.dynamic_update_slice(jnp.zeros((N_CHIPS,) + given[nm].shape[1:], BF16),
                                         given[nm][l].astype(BF16)[None], (chip, 0, 0)) for nm in _BIG]
        win, wa, wb, wc, wo, w1, w3, w2 = _chip_gather(bufs, "gather_weights")
        return win, wa, wb, wc, wo.reshape(1, D, D), w1, w3, w2

    def layer_fwd(X, l, W):
        win, wa, wb, wc, wo, w1, w3, w2 = W
        h = _norm_mod(X, norm1[l][None], mod[l], 0, 1, S)
        p = _mm_nn(h, win, F32, "in_proj")
        ya = _conv_fwd(p, cw8[l], S)
        q, k, v = _qkv_prep(p, cos_t, sin_t, qg[l], kg[l], e_avg)
        at, qa = _flash_fwd(q.reshape(N_KV_HEADS, GROUP, T, LANES), k, v, S)
        yc = _gmlp_fwd(p, sgn[l], bd[l], bias[l])
        mg = _merge_fwd(ya, at, yc, p, wa, wb, wc)
        X1, f1 = _mm_res(mg[None], wo, X, mod[l], 2, S, "out_proj")
        h2 = _norm_mod(X1, norm2[l][None], mod[l], 3, 4, S)
        a1, a3, act = _ffn_up(h2, w1, w3)
        X2, f2 = _mm_res(act, w2, X1, mod[l], 5, S, "ffn_down")
        return X2, dict(X=X, h=h, p=p, ya=ya, k=k, v=v, at=at, qa=qa, yc=yc, mg=mg, X1=X1, f1=f1, h2=h2,
                        a1=a1, a3=a3, act=act, f2=f2)

    def layer_bwd(dX2, l, W, sv):
        win, wa, wb, wc, wo, w1, w3, w2 = W
        dyf, dgt2 = _gate_bwd(dX2, sv["f2"], mod[l], 5, S)
        da1, da3 = _ffn_down_bwd(dyf, w2, sv["a1"], sv["a3"])
        dw2 = _mm_tn(sv["act"], dyf, _shard_rows(n_ff), _rows(D), N_CHIPS, n_ff, D, T, "dw_ff2")
        dh2 = _mm_nt_acc([da1, da3], [w1, w3], False, "ffn_up_bwd")
        dw1 = _mm_tn(sv["h2"], da1, _rows(D), _shard_rows(n_ff), N_CHIPS, D, n_ff, T, "dw_ff1")
        dw3 = _mm_tn(sv["h2"], da3, _rows(D), _shard_rows(n_ff), N_CHIPS, D, n_ff, T, "dw_ff3")
        dX1, dn2, dsh2, dsc2 = _norm_mod_bwd(sv["X1"], dh2, dX2, norm2[l][None], mod[l], 4, S)
        dyo, dgt1 = _gate_bwd(dX1, sv["f1"], mod[l], 2, S)
        dwo = _mm_tn(sv["mg"], dyo, _rows(D), _rows(D), 1, D, D, T, "dw_o")
        dp, dya, doa, dyc, dwa, dwb, dwc = _merge_bwd(dyo, sv["ya"], sv["at"], sv["yc"], sv["p"], wa, wb, wc, wo[0])
        dp, dcw = _conv_bwd(dp, dya, sv["p"], cw8[l], S)
        dp, dsg, dws, dbs = _gmlp_bwd(dp, dyc, sv["p"], sgn[l], bd[l], bdt[l], bias[l], gsum)
        dq, dk, dv = _flash_bwd(sv["qa"], doa.reshape(N_KV_HEADS, GROUP, T, LANES), sv["k"], sv["v"], S)
        dp, dqg, dkg = _qkv_prep_bwd(dp, dq.reshape(N_Q_HEADS, T, LANES), dk, dv, sv["p"], cos_t, sin_t,
                                     qg[l], kg[l], e_avg, fold)
        dh = _mm_nt_acc([dp], [win], True, "in_proj_bwd")
        dwin = _mm_tn(sv["h"], dp, _rows(D), _row_cols(n_in), N_CHIPS, D, n_in, T, "dw_in")
        dX0, dn1, dsh1, dsc1 = _norm_mod_bwd(sv["X"], dh, dX1, norm1[l][None], mod[l], 1, S)
        dmod = jnp.concatenate([dsh1, dsc1, dgt1, dsh2, dsc2, dgt2], axis=1)
        big = [dwin, dwa, dwb, dwc, dwo.reshape(N_CHIPS, D // N_CHIPS, D), dw1, dw3, dw2]
        small = dict(norm1=dn1[0], norm2=dn2[0], q_gain=dqg[0, :HEAD_DIM], k_gain=dkg[0, :HEAD_DIM],
                     conv_w=dcw[:3], sg_norm=dsg[0], w_s=dws, b_s=jnp.swapaxes(dbs[:, :4], 0, 1), dmod=dmod)
        return dX0, big, small

    X = jnp.concatenate([x[0], ctx[0]], axis=0)
    Ws, saved = [], []
    for l in range(L):
        Ws.append(gathered(l))
        X, sv = layer_fwd(X, l, Ws[l])
        saved.append(sv)
    dX, lpart = _loss_grad(X, loss_target[0], S)
    loss = lax.psum(lpart[0, 0], ("x", "y", "c"))

    out = {nm: [None] * L for nm in _BIG}
    smalls = [None] * L
    for l in reversed(range(L)):
        dX, big, smalls[l] = layer_bwd(dX, l, Ws[l], saved[l])
        sib = _sibling_halves([g.astype(BF16) for g in big], "swap_halves")
        own = [lax.dynamic_slice_in_dim(g, mc * (g.shape[1] // 2), g.shape[1] // 2, axis=1) for g in big]
        sent = [_add_cast(g, s) for g, s in zip(own, sib)]
        recv = _chip_scatter(sent, "scatter_grads")
        recv = [lax.dynamic_update_slice(r, lax.dynamic_slice_in_dim(g, chip, 1, axis=0), (chip, 0, 0))
                for r, g in zip(recv, sent)]
        halves = [_sum_lead(r, "sum_chips") for r in recv]
        full = _sibling_fill(halves, "fill_halves")
        full = [lax.dynamic_update_slice(f, hv, (mc * hv.shape[0], 0)) for f, hv in zip(full, halves)]
        for nm, g_ in zip(_BIG, full):
            out[nm][l] = _adamw(given[nm][l], given["m_" + nm][l], given["v_" + nm][l], g_)
    grad_x = dX[:S][None]

    def flat(nm):
        return jnp.stack([smalls[l][nm] for l in range(L)]).reshape(-1)

    dmod_all = jnp.stack([smalls[l]["dmod"] for l in range(L)])
    dml = dmod_all[:, 0].reshape(-1)
    dmc = dmod_all[:, 1].reshape(-1)
    names = ("norm1", "q_gain", "k_gain", "conv_w", "sg_norm", "w_s", "b_s", "norm2")
    parts = [dml, dml + dmc, dmc] + [flat(nm) for nm in names]
    sizes = [int(a.shape[0]) for a in parts]
    total = sum(sizes)
    padn = (-total) % (8 * LANES)
    sbuf = jnp.concatenate(parts + [jnp.zeros((padn,), F32)]).reshape(-1, LANES)
    g3 = _ag8(sbuf, "gather_small")
    ssum = _sum_lead(g3, "sum_devices").reshape(-1)
    offs = np.cumsum([0] + sizes)
    seg = {nm: ssum[offs[i + 3]:offs[i + 4]] for i, nm in enumerate(names)}
    gb_mod = ssum[offs[1]:offs[2]].reshape(L, N_CHIPS * n_mod)
    dmc_sum = ssum[offs[2]:offs[3]].reshape(L, N_CHIPS * n_mod)
    dml_all = g3.reshape(N_DEV, -1)[:, :sizes[0]].reshape(N_DEV, L, N_CHIPS * n_mod)
    dml_sh = jnp.swapaxes(lax.dynamic_slice_in_dim(dml_all, chip * n_mod, n_mod, axis=2), 0, 1)
    dmc_sh = lax.dynamic_slice_in_dim(dmc_sum, chip * n_mod, n_mod, axis=1)[:, None, :]
    dms = jnp.concatenate([dml_sh, dmc_sh, jnp.zeros((L, 7, n_mod), F32)], axis=1)
    g_wmod = _wmod_grad(a_raw, dms)
    part = _cctx_partial(jnp.concatenate([dmc_sh, jnp.zeros((L, 15, n_mod), F32)], axis=1), w_mod)
    g4 = _ag8(part.reshape(-1, LANES), "gather_cctx")
    g_cctx = _cctx_final(g4, c_ctx.reshape(8, LANES)).reshape(D)

    g_conv = lax.dynamic_slice_in_dim(seg["conv_w"].reshape(L, 3, N_CHIPS * n_cw), chip * n_cw, n_cw, axis=2)
    small_g = dict(c_ctx=g_cctx, w_mod=g_wmod, b_mod=gb_mod, norm1=seg["norm1"].reshape(norm1.shape),
                   q_gain=seg["q_gain"].reshape(q_gain.shape), k_gain=seg["k_gain"].reshape(k_gain.shape),
                   conv_w=g_conv, sg_norm=seg["sg_norm"].reshape(sg_norm.shape), w_s=seg["w_s"].reshape(w_s.shape),
                   b_s=seg["b_s"].reshape(b_s.shape), norm2=seg["norm2"].reshape(norm2.shape))
    res = {}
    for nm in _WEIGHTS:
        if nm in _BIG:
            res[nm] = [jnp.stack([out[nm][l][k] for l in range(L)]) for k in range(4)]
        else:
            res[nm] = _adamw(given[nm], given["m_" + nm], given["v_" + nm], small_g[nm])
    return (loss, grad_x, *[res[nm][0] for nm in _WEIGHTS], *[res[nm][1] for nm in _WEIGHTS],
            *[res[nm][2] for nm in _WEIGHTS], *[res[nm][3] for nm in _WEIGHTS])
```

```python
import functools

import jax
import jax.numpy as jnp
import numpy as np
from jax import lax
from jax.experimental import pallas as pl
from jax.experimental.pallas import tpu as pltpu

F32 = jnp.float32
BF16 = jnp.bfloat16
EPS = 1e-6
D_MODEL = 1024
HEAD_DIM = 64
N_Q_HEADS = 8
N_KV_HEADS = 2
GROUP = N_Q_HEADS // N_KV_HEADS
GRID_W = 64
ROPE_THETA = 10000.0
CHUNK = 128
CONV_W = 256
SG_W = 256
OFF_Q = 3 * CONV_W
QKV_W = 768
OFF_U = OFF_Q + QKV_W
OFF_G = OFF_U + 2 * SG_W
IN_W = OFF_G + 3 * D_MODEL
N_CHIPS = 4
N_DEV = 8
LANES = 128
UNROLL_FWD = 8
UNROLL_BWD = 4
AUG = 3
ADAM_LR, ADAM_B1, ADAM_B2, ADAM_EPS, ADAM_WD, ADAM_STEP = 0.001, 0.9, 0.999, 1e-8, 0.01, 10
VMEM_LIMIT_V7X = 52 * 1024 * 1024
MESH_ID = pl.DeviceIdType.MESH
NT = (((1,), (1,)), ((), ()))
TN = (((0,), (0,)), ((), ()))
ANY = pl.BlockSpec(memory_space=pl.ANY)


def _cp(*sem):
    return pltpu.CompilerParams(dimension_semantics=sem or None, vmem_limit_bytes=VMEM_LIMIT_V7X)


def _tile(n, target, mult=16):
    best = None
    for t in range(mult, n + 1, mult):
        if n % t == 0 and t <= target:
            best = t
    assert best is not None, (n, target, mult)
    return best


def _full(shape):
    nd = len(shape)
    return pl.BlockSpec(tuple(shape), lambda *_: (0,) * nd)


def _segments(i, tm, n_lat, fn):
    k, off = divmod(n_lat, tm)

    @pl.when(i < k)
    def _():
        fn(0, tm, 0)

    @pl.when(i == k)
    def _():
        if off:
            fn(0, off, 0)
        fn(off, tm, 1)

    @pl.when(i > k)
    def _():
        fn(0, tm, 1)


def _dot(a, b):
    return jnp.dot(a, b, preferred_element_type=F32)


def _dg(a, b, dims):
    return lax.dot_general(a, b, dims, preferred_element_type=F32)


def _split3(x):
    hi = x.astype(BF16)
    r1 = x - hi.astype(F32)
    mid = r1.astype(BF16)
    lo = (r1 - mid.astype(F32)).astype(BF16)
    return hi.astype(F32), mid.astype(F32), lo.astype(F32)


def _lane(shape):
    return lax.broadcasted_iota(jnp.int32, shape, len(shape) - 1)


def _aug(val, stat):
    lane = _lane(val.shape)
    hi, mid, lo = _split3(stat)
    ext = jnp.where(lane == 64, hi, jnp.where(lane == 65, mid, jnp.where(lane == 66, lo, 0.0)))
    return jnp.where(lane < 64, val, ext)


def _seg_mean(x, e):
    outs = []
    for g in range(x.shape[1] // LANES):
        blk = x[:, g * LANES:(g + 1) * LANES]
        hi = blk.astype(BF16)
        lo = (blk - hi.astype(F32)).astype(BF16)
        outs.append(_dot(hi, e) + _dot(lo, e))
    return outs[0] if len(outs) == 1 else jnp.concatenate(outs, axis=1)


def _rope(x, cos, sin_signed, inverse):
    w = x.shape[1]
    reps = w // LANES
    c = cos if reps == 1 else jnp.tile(cos, (1, reps))
    s = sin_signed if reps == 1 else jnp.tile(sin_signed, (1, reps))
    first = (_lane(x.shape) % 32) < 16
    partner = jnp.where(first, pltpu.roll(x, w - 16, 1), pltpu.roll(x, 16, 1))
    return x * c - partner * s if inverse else x * c + partner * s


def _sigmoid(x):
    return 1.0 / (1.0 + jnp.exp(-x))


_GELU_K = 0.7978845608028654
_GELU_C = 0.044715


def _gelu(x):
    return 0.5 * x * (1.0 + jnp.tanh(_GELU_K * (x + _GELU_C * x * x * x)))


def _gelu_grad(x):
    t = jnp.tanh(_GELU_K * (x + _GELU_C * x * x * x))
    return 0.5 * (1.0 + t) + 0.5 * x * (1.0 - t * t) * _GELU_K * (1.0 + 3.0 * _GELU_C * x * x)


def _loop_unrolled(n, step, init, unroll):
    def trip(t, carry):
        for u in range(unroll):
            carry = step(t * unroll + u, carry)
        return carry

    carry = lax.fori_loop(0, n // unroll, trip, init) if n >= unroll else init
    for r in range(n - n % unroll, n):
        carry = step(r, carry)
    return carry


def _heads_to_rows(x, n_heads):
    out = []
    for h in range(n_heads):
        grp = x[:, (h // 2) * LANES:(h // 2 + 1) * LANES]
        out.append(grp if h % 2 == 0 else pltpu.roll(grp, 64, 1))
    return out


def _rows_to_heads(blocks):
    outs = []
    lane = _lane(blocks[0].shape)
    for a in range(len(blocks) // 2):
        outs.append(jnp.where(lane < 64, blocks[2 * a], pltpu.roll(blocks[2 * a + 1], 64, 1)))
    return outs[0] if len(outs) == 1 else jnp.concatenate(outs, axis=1)


def _norm_mod(x, g, mod, i_shift, i_scale, n_lat):
    T, D = x.shape
    tm = _tile(T, 528)

    def body(x_ref, g_ref, mod_ref, h_ref):
        def fn(r0, r1, seg):
            xv = x_ref[r0:r1, :]
            r = lax.rsqrt(jnp.mean(xv * xv, axis=-1, keepdims=True) + EPS)
            n = xv * r * g_ref[...]
            h = n * (1.0 + mod_ref[seg, i_scale:i_scale + 1, :]) + mod_ref[seg, i_shift:i_shift + 1, :]
            h_ref[r0:r1, :] = h.astype(BF16)

        _segments(pl.program_id(0), tm, n_lat, fn)

    return pl.pallas_call(
        body, grid=(T // tm,), name="norm_mod",
        in_specs=[pl.BlockSpec((tm, D), lambda i: (i, 0)), _full(g.shape), _full(mod.shape)],
        out_specs=pl.BlockSpec((tm, D), lambda i: (i, 0)),
        out_shape=jax.ShapeDtypeStruct((T, D), BF16), compiler_params=_cp("parallel"))(x, g, mod)


def _norm_mod_bwd(x, dh, dres, g, mod, i_scale, n_lat):
    T, D = x.shape
    tm = _tile(T, 528)

    def body(x_ref, dh_ref, dres_ref, g_ref, mod_ref, dx_ref, dg_ref, dsh_ref, dsc_ref):
        i = pl.program_id(0)

        @pl.when(i == 0)
        def _():
            dg_ref[...] = jnp.zeros_like(dg_ref)
            dsh_ref[...] = jnp.zeros_like(dsh_ref)
            dsc_ref[...] = jnp.zeros_like(dsc_ref)

        def fn(r0, r1, seg):
            xv = x_ref[r0:r1, :]
            dh = dh_ref[r0:r1, :]
            r = lax.rsqrt(jnp.mean(xv * xv, axis=-1, keepdims=True) + EPS)
            xh = xv * r
            gv = g_ref[...]
            dsh_ref[seg] += jnp.sum(dh, axis=0, keepdims=True)
            dsc_ref[seg] += jnp.sum(dh * (xh * gv), axis=0, keepdims=True)
            dn = dh * (1.0 + mod_ref[seg, i_scale:i_scale + 1, :])
            dg_ref[...] += jnp.sum(dn * xh, axis=0, keepdims=True)
            gd = gv * dn
            dx_ref[r0:r1, :] = dres_ref[r0:r1, :] + r * (gd - xh * jnp.mean(xh * gd, axis=-1, keepdims=True))

        _segments(i, tm, n_lat, fn)

    row = pl.BlockSpec((tm, D), lambda i: (i, 0))
    return pl.pallas_call(
        body, grid=(T // tm,), name="norm_mod_bwd",
        in_specs=[row, row, row, _full(g.shape), _full(mod.shape)],
        out_specs=[row, _full((1, D)), _full((2, 1, D)), _full((2, 1, D))],
        out_shape=[jax.ShapeDtypeStruct((T, D), F32), jax.ShapeDtypeStruct((1, D), F32),
                   jax.ShapeDtypeStruct((2, 1, D), F32), jax.ShapeDtypeStruct((2, 1, D), F32)],
        compiler_params=_cp("arbitrary"))(x, dh, dres, g, mod)


def _gate_bwd(dx, f, mod, i_gate, n_lat):
    T, D = dx.shape
    tm = _tile(T, 528)

    def body(dx_ref, f_ref, mod_ref, dy_ref, dg_ref):
        i = pl.program_id(0)

        @pl.when(i == 0)
        def _():
            dg_ref[...] = jnp.zeros_like(dg_ref)

        def fn(r0, r1, seg):
            dxv = dx_ref[r0:r1, :]
            dy_ref[r0:r1, :] = (dxv * mod_ref[seg, i_gate:i_gate + 1, :]).astype(BF16)
            dg_ref[seg] += jnp.sum(dxv * f_ref[r0:r1, :], axis=0, keepdims=True)

        _segments(i, tm, n_lat, fn)

    row = pl.BlockSpec((tm, D), lambda i: (i, 0))
    return pl.pallas_call(
        body, grid=(T // tm,), name="gate_bwd",
        in_specs=[row, row, _full(mod.shape)], out_specs=[row, _full((2, 1, D))],
        out_shape=[jax.ShapeDtypeStruct((T, D), BF16), jax.ShapeDtypeStruct((2, 1, D), F32)],
        compiler_params=_cp("arbitrary"))(dx, f, mod)


def _mm_nn(a, w, out_dtype, name):
    M, K = a.shape
    J, _, n = w.shape
    tm = _tile(M, 1056)

    def body(a_ref, w_ref, o_ref):
        o_ref[...] = _dot(a_ref[...], w_ref[...]).astype(o_ref.dtype)

    return pl.pallas_call(
        body, grid=(M // tm, J), name=name,
        in_specs=[pl.BlockSpec((tm, K), lambda i, j: (i, 0)), pl.BlockSpec((None, K, n), lambda i, j: (j, 0, 0))],
        out_specs=pl.BlockSpec((tm, n), lambda i, j: (i, j)),
        out_shape=jax.ShapeDtypeStruct((M, J * n), out_dtype), compiler_params=_cp("parallel", "arbitrary"))(a, w)


def _mm_res(a3, w, res, mod, i_gate, n_lat, name):
    J, M, k = a3.shape
    N = w.shape[2]
    tm = _tile(M, 528)

    def body(a_ref, w_ref, res_ref, mod_ref, x_ref, f_ref):
        acc = _dot(a_ref[0], w_ref[0])
        for j in range(1, J):
            acc += _dot(a_ref[j], w_ref[j])
        f_ref[...] = acc

        def fn(r0, r1, seg):
            x_ref[r0:r1, :] = res_ref[r0:r1, :] + mod_ref[seg, i_gate:i_gate + 1, :] * f_ref[r0:r1, :]

        _segments(pl.program_id(0), tm, n_lat, fn)

    row = pl.BlockSpec((tm, N), lambda i: (i, 0))
    return pl.pallas_call(
        body, grid=(M // tm,), name=name,
        in_specs=[pl.BlockSpec((J, tm, k), lambda i: (0, i, 0)), _full(w.shape), row, _full(mod.shape)],
        out_specs=[row, row],
        out_shape=[jax.ShapeDtypeStruct((M, N), F32), jax.ShapeDtypeStruct((M, N), F32)],
        compiler_params=_cp("parallel"))(a3, w, res, mod)


def _mm_nt_acc(dys, ws, row_major, name):
    J, K, n = ws[0].shape
    M = dys[0].shape[0] if row_major else dys[0].shape[1]
    tm = _tile(M, 1056)
    P = len(dys)

    def body(*refs):
        o_ref = refs[2 * P]
        j = pl.program_id(1)
        part = _dg(refs[0][...], refs[P][...], NT)
        for p in range(1, P):
            part += _dg(refs[p][...], refs[P + p][...], NT)

        @pl.when(j == 0)
        def _():
            o_ref[...] = part

        @pl.when(j > 0)
        def _():
            o_ref[...] += part

    dy_spec = (pl.BlockSpec((tm, n), lambda i, j: (i, j)) if row_major
               else pl.BlockSpec((None, tm, n), lambda i, j: (j, i, 0)))
    w_spec = pl.BlockSpec((None, K, n), lambda i, j: (j, 0, 0))
    return pl.pallas_call(
        body, grid=(M // tm, J), name=name,
        in_specs=[dy_spec] * P + [w_spec] * P,
        out_specs=pl.BlockSpec((tm, K), lambda i, j: (i, 0)),
        out_shape=jax.ShapeDtypeStruct((M, K), F32), compiler_params=_cp("parallel", "arbitrary"))(*dys, *ws)


def _mm_tn(x, dy, x_spec, dy_spec, J, K, n, T, name):
    tk = _tile(T, 1056)

    def body(x_ref, dy_ref, o_ref):
        t = pl.program_id(1)
        part = _dg(x_ref[...], dy_ref[...], TN)

        @pl.when(t == 0)
        def _():
            o_ref[...] = part

        @pl.when(t > 0)
        def _():
            o_ref[...] += part

    return pl.pallas_call(
        body, grid=(J, T // tk), name=name,
        in_specs=[x_spec(tk), dy_spec(tk)],
        out_specs=pl.BlockSpec((None, K, n), lambda j, t: (j, 0, 0)),
        out_shape=jax.ShapeDtypeStruct((J, K, n), F32), compiler_params=_cp("parallel", "arbitrary"))(x, dy)


def _rows(width):
    return lambda tk: pl.BlockSpec((tk, width), lambda j, t: (t, 0))


def _row_cols(width):
    return lambda tk: pl.BlockSpec((tk, width), lambda j, t: (t, j))


def _shard_rows(width):
    return lambda tk: pl.BlockSpec((None, tk, width), lambda j, t: (j, t, 0))


def _ffn_up(h, w1, w3):
    T, D = h.shape
    J, _, n = w1.shape
    tm = _tile(T, 1056)

    def body(h_ref, w1_ref, w3_ref, a1_ref, a3_ref, act_ref):
        hv = h_ref[...]
        a1 = _dot(hv, w1_ref[...])
        a3 = _dot(hv, w3_ref[...])
        a1_ref[...] = a1
        a3_ref[...] = a3
        act_ref[...] = (a1 * _sigmoid(a1) * a3).astype(BF16)

    w_spec = pl.BlockSpec((None, D, n), lambda i, j: (j, 0, 0))
    o_spec = pl.BlockSpec((None, tm, n), lambda i, j: (j, i, 0))
    return pl.pallas_call(
        body, grid=(T // tm, J), name="ffn_up",
        in_specs=[pl.BlockSpec((tm, D), lambda i, j: (i, 0)), w_spec, w_spec], out_specs=[o_spec] * 3,
        out_shape=[jax.ShapeDtypeStruct((J, T, n), F32)] * 2 + [jax.ShapeDtypeStruct((J, T, n), BF16)],
        compiler_params=_cp("parallel", "arbitrary"))(h, w1, w3)


def _ffn_down_bwd(dy, w2, a1, a3):
    T, D = dy.shape
    J, n, _ = w2.shape
    tm = _tile(T, 1056)

    def body(dy_ref, w2_ref, a1_ref, a3_ref, da1_ref, da3_ref):
        dact = _dg(dy_ref[...], w2_ref[...], NT)
        a1v = a1_ref[...]
        sig = _sigmoid(a1v)
        da3_ref[...] = (dact * a1v * sig).astype(BF16)
        da1_ref[...] = (dact * a3_ref[...] * (sig * (1.0 + a1v * (1.0 - sig)))).astype(BF16)

    a_spec = pl.BlockSpec((None, tm, n), lambda i, j: (j, i, 0))
    return pl.pallas_call(
        body, grid=(T // tm, J), name="ffn_down_bwd",
        in_specs=[pl.BlockSpec((tm, D), lambda i, j: (i, 0)), pl.BlockSpec((None, n, D), lambda i, j: (j, 0, 0)),
                  a_spec, a_spec],
        out_specs=[a_spec, a_spec], out_shape=[jax.ShapeDtypeStruct((J, T, n), BF16)] * 2,
        compiler_params=_cp("parallel", "arbitrary"))(dy, w2, a1, a3)


def _qkv_prep(p, cos, sin, qg, kg, e):
    T = p.shape[0]
    tm = _tile(T, 528)

    def body(p_ref, cos_ref, sin_ref, qg_ref, kg_ref, e_ref, q_ref, k_ref, v_ref):
        ev = e_ref[...]
        cv, sv = cos_ref[...], sin_ref[...]
        xq = p_ref[:, 0:512]
        qn = xq * lax.rsqrt(_seg_mean(xq * xq, ev) + EPS) * qg_ref[...]
        qr = _rope(qn, cv, sv, False) * (HEAD_DIM ** -0.5)
        xk = p_ref[:, 512:640]
        kn = xk * lax.rsqrt(_seg_mean(xk * xk, ev) + EPS) * kg_ref[...]
        kr = _rope(kn, cv, sv, False)
        lane = _lane((tm, LANES))
        ones = jnp.where(lane < 64 + AUG, -1.0, 0.0)
        for h, blk in enumerate(_heads_to_rows(qr, N_Q_HEADS)):
            q_ref[h] = jnp.where(lane < 64, blk, 0.0).astype(BF16)
        for h, blk in enumerate(_heads_to_rows(kr, N_KV_HEADS)):
            k_ref[h] = jnp.where(lane < 64, blk, ones).astype(BF16)
        for h, blk in enumerate(_heads_to_rows(p_ref[:, 640:768], N_KV_HEADS)):
            v_ref[h] = jnp.where(lane < 64, blk, ones).astype(BF16)

    tab = pl.BlockSpec((tm, LANES), lambda i: (i, 0))
    return pl.pallas_call(
        body, grid=(T // tm,), name="qkv_prep",
        in_specs=[pl.BlockSpec((tm, QKV_W), lambda i: (i, 1)), tab, tab, _full(qg.shape), _full(kg.shape),
                  _full(e.shape)],
        out_specs=[pl.BlockSpec((N_Q_HEADS, tm, LANES), lambda i: (0, i, 0)),
                   pl.BlockSpec((N_KV_HEADS, tm, LANES), lambda i: (0, i, 0)),
                   pl.BlockSpec((N_KV_HEADS, tm, LANES), lambda i: (0, i, 0))],
        out_shape=[jax.ShapeDtypeStruct((N_Q_HEADS, T, LANES), BF16),
                   jax.ShapeDtypeStruct((N_KV_HEADS, T, LANES), BF16),
                   jax.ShapeDtypeStruct((N_KV_HEADS, T, LANES), BF16)],
        compiler_params=_cp("parallel"))(p, cos, sin, qg, kg, e)


def _qkv_prep_bwd(dp, dq, dk, dv, p, cos, sin, qg, kg, e, fold):
    T = p.shape[0]
    tm = _tile(T, 528)
    nt = T // tm

    def body(dp_in, dq_ref, dk_ref, dv_ref, p_ref, cos_ref, sin_ref, qg_ref, kg_ref, e_ref, fold_ref,
             dp_ref, dqg_ref, dkg_ref, accq, acck):
        del dp_in
        i = pl.program_id(0)

        @pl.when(i == 0)
        def _():
            accq[...] = jnp.zeros_like(accq)
            acck[...] = jnp.zeros_like(acck)

        ev = e_ref[...]
        cv, sv = cos_ref[...], sin_ref[...]

        def one(x, dr, gain, acc):
            r = lax.rsqrt(_seg_mean(x * x, ev) + EPS)
            xh = x * r
            dn = _rope(dr, cv, sv, True)
            acc[0:1, :] += jnp.sum(dn * xh, axis=0, keepdims=True)
            gd = gain * dn
            return r * (gd - xh * _seg_mean(xh * gd, ev))

        dqr = _rows_to_heads([dq_ref[h] for h in range(N_Q_HEADS)]) * (HEAD_DIM ** -0.5)
        dkr = _rows_to_heads([dk_ref[h] for h in range(N_KV_HEADS)])
        dvv = _rows_to_heads([dv_ref[h] for h in range(N_KV_HEADS)])
        dp_ref[:, 0:512] = one(p_ref[:, 0:512], dqr, qg_ref[...], accq).astype(BF16)
        dp_ref[:, 512:640] = one(p_ref[:, 512:640], dkr, kg_ref[...], acck).astype(BF16)
        dp_ref[:, 640:768] = dvv.astype(BF16)

        @pl.when(i == nt - 1)
        def _():
            fv = fold_ref[...]
            dqg_ref[...] = jnp.dot(accq[...], fv, preferred_element_type=F32, precision=lax.Precision.HIGHEST)
            dkg_ref[...] = jnp.dot(acck[...], fv[0:LANES, :], preferred_element_type=F32,
                                   precision=lax.Precision.HIGHEST)

    tab = pl.BlockSpec((tm, LANES), lambda i: (i, 0))
    sec = pl.BlockSpec((tm, QKV_W), lambda i: (i, 1))
    return pl.pallas_call(
        body, grid=(nt,), name="qkv_prep_bwd",
        in_specs=[ANY, pl.BlockSpec((N_Q_HEADS, tm, LANES), lambda i: (0, i, 0)),
                  pl.BlockSpec((N_KV_HEADS, tm, LANES), lambda i: (0, i, 0)),
                  pl.BlockSpec((N_KV_HEADS, tm, LANES), lambda i: (0, i, 0)),
                  sec, tab, tab, _full(qg.shape), _full(kg.shape), _full(e.shape), _full(fold.shape)],
        out_specs=[sec, _full((8, LANES)), _full((8, LANES))],
        out_shape=[jax.ShapeDtypeStruct(dp.shape, BF16), jax.ShapeDtypeStruct((8, LANES), F32),
                   jax.ShapeDtypeStruct((8, LANES), F32)],
        scratch_shapes=[pltpu.VMEM((8, 512), F32), pltpu.VMEM((8, LANES), F32)],
        input_output_aliases={0: 0}, compiler_params=_cp("arbitrary"))(dp, dq, dk, dv, p, cos, sin, qg, kg, e, fold)


def _flash_fwd(q, k, v, n_lat, gather=()):
    _, _, T, _ = q.shape
    tq = tk = 256
    nq = T // tq
    M = GROUP * tq

    nk_lat = n_lat // tk

    n_g = len(gather)

    def body(q_ref, k_ref, v_ref, *rest):
        o_ref, qa_ref = rest[n_g], rest[n_g + 1]
        i = pl.program_id(0)
        if n_g:
            bufs = rest[n_g + 2:2 * n_g + 2]
            start, wait = _plane_exchange(bufs, bufs, rest[-2], rest[-1], False)
            pl.when((i == 0) & (pl.program_id(1) == 0))(start)
        qv = q_ref[...].reshape(M, LANES)

        def step(s, carry):
            m, acc = carry
            r0 = s * tk if isinstance(s, int) else pl.multiple_of(s * tk, tk)
            sc = _dg(qv, k_ref[pl.ds(r0, tk), :], NT)
            m_new = jnp.maximum(m, jnp.max(sc, axis=1, keepdims=True))
            pr = jnp.exp(sc - m_new)
            acc = jnp.exp(m - m_new) * acc + _dot(pr.astype(BF16), v_ref[pl.ds(r0, tk), :])
            return m_new, acc

        def finish(m, acc):
            den = -acc[:, 64:65]
            out = acc / den
            o_ref[...] = _rows_to_heads([out[g * tq:(g + 1) * tq] for g in range(GROUP)]).astype(BF16)
            qa_ref[...] = _aug(qv.astype(F32), m + jnp.log(den)).astype(BF16).reshape(GROUP, tq, LANES)

        init = (jnp.full((M, 1), -1e30, F32), jnp.zeros((M, LANES), F32))

        @pl.when(i < n_lat // tq)
        def _():
            carry = _loop_unrolled(nk_lat, step, init, UNROLL_FWD)
            for s in range(nk_lat, T // tk):
                carry = step(s, carry)
            finish(*carry)

        @pl.when(i >= n_lat // tq)
        def _():
            carry = init
            for s in range(nk_lat, T // tk):
                carry = step(s, carry)
            finish(*carry)

        if n_g:
            pl.when((i == nq - 1) & (pl.program_id(1) == N_KV_HEADS - 1))(wait)

    q_spec = pl.BlockSpec((None, GROUP, tq, LANES), lambda i, h: (h, 0, i, 0))
    kv_spec = pl.BlockSpec((None, T, LANES), lambda i, h: (h, 0, 0))
    sems = [pltpu.SemaphoreType.DMA((3 * n_g,))] * 2 if n_g else []
    return pl.pallas_call(
        body, grid=(nq, N_KV_HEADS), name="flash_fwd_gather" if n_g else "flash_fwd",
        in_specs=[q_spec, kv_spec, kv_spec] + [ANY] * n_g,
        out_specs=[pl.BlockSpec((tq, GROUP * HEAD_DIM), lambda i, h: (i, h)), q_spec] + [ANY] * n_g,
        out_shape=[jax.ShapeDtypeStruct((T, N_Q_HEADS * HEAD_DIM), BF16), jax.ShapeDtypeStruct(q.shape, BF16)]
        + [jax.ShapeDtypeStruct(b.shape, b.dtype) for b in gather],
        input_output_aliases={3 + a: 2 + a for a in range(n_g)}, scratch_shapes=sems,
        compiler_params=_cp("arbitrary", "arbitrary"))(q, k, v, *gather)


def _flash_bwd(qa, doa, k, v, n_lat, scatter=()):
    _, _, T, _ = qa.shape
    tq = tk = 256
    nkv = T // tk
    M = GROUP * tq

    n_s = len(scatter)

    def body(qa_hbm, doa_hbm, k_ref, v_ref, *rest):
        dq_hbm, dk_ref, dv_ref = rest[n_s:n_s + 3]
        q_sc, do_sc, dq_sc, sems = rest[2 * n_s + 3:2 * n_s + 7]
        h = pl.program_id(0)
        j = pl.program_id(1)
        if n_s:
            start, wait = _plane_exchange(rest[:n_s], rest[n_s + 3:2 * n_s + 3], rest[-2], rest[-1], True)
            pl.when((h == 0) & (j == 0))(start)

        @pl.when(j == 0)
        def _():
            c1 = pltpu.make_async_copy(qa_hbm.at[h], q_sc, sems.at[0])
            c2 = pltpu.make_async_copy(doa_hbm.at[h], do_sc, sems.at[1])
            c1.start()
            c2.start()
            dq_sc[...] = jnp.zeros_like(dq_sc)
            c1.wait()
            c2.wait()

        kb = k_ref[...]
        vb = v_ref[...]

        def step(i, carry):
            dk, dv = carry
            r0 = i * tq if isinstance(i, int) else pl.multiple_of(i * tq, tq)
            qv = q_sc[:, pl.ds(r0, tq), :].reshape(M, LANES)
            dov = do_sc[:, pl.ds(r0, tq), :].reshape(M, LANES)
            pr = jnp.exp(_dg(qv, kb, NT))
            ds = (pr * _dg(dov, vb, NT)).astype(BF16)
            dv = dv + _dg(pr.astype(BF16), dov, TN)
            dk = dk + _dg(ds, qv, TN)
            dq_sc[:, pl.ds(r0, tq), :] += _dot(ds, kb).reshape(GROUP, tq, LANES)
            return dk, dv

        z = jnp.zeros((tk, LANES), F32)
        carry = _loop_unrolled(n_lat // tq, step, (z, z), UNROLL_BWD)
        dk_ref[...] = carry[0]
        dv_ref[...] = carry[1]

        @pl.when(j >= n_lat // tk)
        def _():
            c = (dk_ref[...], dv_ref[...])
            for i in range(n_lat // tq, T // tq):
                c = step(i, c)
            dk_ref[...] = c[0]
            dv_ref[...] = c[1]

        @pl.when(j == nkv - 1)
        def _():
            c3 = pltpu.make_async_copy(dq_sc, dq_hbm.at[h], sems.at[2])
            c3.start()
            c3.wait()

        if n_s:
            pl.when((h == N_KV_HEADS - 1) & (j == nkv - 1))(wait)

    kv_spec = pl.BlockSpec((None, tk, LANES), lambda h, j: (h, j, 0))
    return pl.pallas_call(
        body, grid=(N_KV_HEADS, nkv), name="flash_bwd_scatter" if n_s else "flash_bwd",
        in_specs=[ANY, ANY, kv_spec, kv_spec] + [ANY] * n_s, out_specs=[ANY, kv_spec, kv_spec] + [ANY] * n_s,
        out_shape=[jax.ShapeDtypeStruct(qa.shape, F32), jax.ShapeDtypeStruct(k.shape, F32),
                   jax.ShapeDtypeStruct(k.shape, F32)] + [jax.ShapeDtypeStruct(g.shape, g.dtype) for g in scatter],
        scratch_shapes=[pltpu.VMEM((GROUP, T, LANES), BF16), pltpu.VMEM((GROUP, T, LANES), BF16),
                        pltpu.VMEM((GROUP, T, LANES), F32), pltpu.SemaphoreType.DMA((3,))]
        + ([pltpu.SemaphoreType.DMA((3 * n_s,))] * 2 if n_s else []),
        compiler_params=_cp("arbitrary", "arbitrary"))(qa, doa, k, v, *scatter)


def _conv_masks(i, tm, n_lat, T):
    row = lax.broadcasted_iota(jnp.int32, (tm, 1), 0)
    g = row + i * tm
    return row, (g == 0) | (g == n_lat), (g == n_lat - 1) | (g == T - 1)


def _shift_rows(v, prev_row, next_row, row, first, last):
    tm = v.shape[0]
    down = jnp.where(row == 0, prev_row, pltpu.roll(v, 1, 0))
    up = jnp.where(row == tm - 1, next_row, pltpu.roll(v, tm - 1, 0))
    return jnp.where(first, 0.0, down), jnp.where(last, 0.0, up)


def _halo_specs(tm, T, width, col):
    nb = T // 8
    return (pl.BlockSpec((8, width), lambda i: (jnp.maximum(i * (tm // 8) - 1, 0), col)),
            pl.BlockSpec((8, width), lambda i: (jnp.minimum((i + 1) * (tm // 8), nb - 1), col)))


def _conv_fwd(p, cw, n_lat):
    T = p.shape[0]
    tm = _tile(T, 1056)

    def body(p_ref, pp_ref, pn_ref, cw_ref, o_ref):
        row, first, last = _conv_masks(pl.program_id(0), tm, n_lat, T)
        z = p_ref[:, 256:512] * p_ref[:, 512:768]
        zp = pp_ref[7:8, 256:512] * pp_ref[7:8, 512:768]
        zn = pn_ref[0:1, 256:512] * pn_ref[0:1, 512:768]
        zd, zu = _shift_rows(z, zp, zn, row, first, last)
        conv = cw_ref[0:1, :] * zd + cw_ref[1:2, :] * z + cw_ref[2:3, :] * zu
        o_ref[...] = (p_ref[:, 0:256] * conv).astype(BF16)

    prev, nxt = _halo_specs(tm, T, 768, 0)
    return pl.pallas_call(
        body, grid=(T // tm,), name="conv_fwd",
        in_specs=[pl.BlockSpec((tm, 768), lambda i: (i, 0)), prev, nxt, _full(cw.shape)],
        out_specs=pl.BlockSpec((tm, CONV_W), lambda i: (i, 0)),
        out_shape=jax.ShapeDtypeStruct((T, CONV_W), BF16), compiler_params=_cp("parallel"))(p, p, p, cw)


def _conv_bwd(dp, dy, p, cw, n_lat):
    T = p.shape[0]
    tm = _tile(T, 1056)

    def body(dp_in, dy_ref, dyp_ref, dyn_ref, p_ref, pp_ref, pn_ref, cw_ref, dp_ref, dcw_ref):
        del dp_in
        i = pl.program_id(0)

        @pl.when(i == 0)
        def _():
            dcw_ref[...] = jnp.zeros_like(dcw_ref)

        row, first, last = _conv_masks(i, tm, n_lat, T)
        ab, ac, ax = p_ref[:, 0:256], p_ref[:, 256:512], p_ref[:, 512:768]
        z = ac * ax
        zp = pp_ref[7:8, 256:512] * pp_ref[7:8, 512:768]
        zn = pn_ref[0:1, 256:512] * pn_ref[0:1, 512:768]
        zd, zu = _shift_rows(z, zp, zn, row, first, last)
        w0, w1, w2 = cw_ref[0:1, :], cw_ref[1:2, :], cw_ref[2:3, :]
        dy = dy_ref[...]
        dc = dy * ab
        dcd, dcu = _shift_rows(dc, dyp_ref[7:8, :] * pp_ref[7:8, 0:256], dyn_ref[0:1, :] * pn_ref[0:1, 0:256],
                               row, first, last)
        dz = w0 * dcu + w1 * dc + w2 * dcd
        dp_ref[:, 0:256] = (dy * (w0 * zd + w1 * z + w2 * zu)).astype(BF16)
        dp_ref[:, 256:512] = (dz * ax).astype(BF16)
        dp_ref[:, 512:768] = (dz * ac).astype(BF16)
        dcw_ref[0:1, :] += jnp.sum(dc * zd, axis=0, keepdims=True)
        dcw_ref[1:2, :] += jnp.sum(dc * z, axis=0, keepdims=True)
        dcw_ref[2:3, :] += jnp.sum(dc * zu, axis=0, keepdims=True)

    prev, nxt = _halo_specs(tm, T, 768, 0)
    dprev, dnxt = _halo_specs(tm, T, CONV_W, 0)
    sec = pl.BlockSpec((tm, 768), lambda i: (i, 0))
    return pl.pallas_call(
        body, grid=(T // tm,), name="conv_bwd",
        in_specs=[ANY, pl.BlockSpec((tm, CONV_W), lambda i: (i, 0)), dprev, dnxt, sec, prev, nxt, _full(cw.shape)],
        out_specs=[sec, _full((8, CONV_W))],
        out_shape=[jax.ShapeDtypeStruct(dp.shape, BF16), jax.ShapeDtypeStruct((8, CONV_W), F32)],
        input_output_aliases={0: 0}, compiler_params=_cp("arbitrary"))(dp, dy, dy, dy, p, p, p, cw)


def _gmlp_mix(bd_ref, vs, grp):
    out = jnp.zeros((2 * CHUNK, SG_W), F32)
    for g in range(4):
        out = jnp.where(grp == g, _dot(bd_ref[g], vs), out)
    return out


def _gmlp_fwd(p, sgn, bd, bias):
    T = p.shape[0]
    tm = _tile(T, 768, 2 * CHUNK)

    def body(p_ref, sgn_ref, bd_ref, bias_ref, o_ref):
        x = _gelu(p_ref[:, 256:512])
        vn = (x * lax.rsqrt(jnp.mean(x * x, axis=-1, keepdims=True) + EPS) * sgn_ref[...]).astype(BF16)
        grp = _lane((2 * CHUNK, SG_W)) // 64
        for s in range(tm // (2 * CHUNK)):
            rs = slice(s * 2 * CHUNK, (s + 1) * 2 * CHUNK)
            mixed = _gmlp_mix(bd_ref, vn[rs], grp) + bias_ref[...]
            o_ref[rs, :] = (_gelu(p_ref[rs, 0:256]) * mixed).astype(BF16)

    return pl.pallas_call(
        body, grid=(T // tm,), name="gmlp_fwd",
        in_specs=[pl.BlockSpec((tm, 2 * SG_W), lambda i: (i, 3)), _full(sgn.shape), _full(bd.shape),
                  _full(bias.shape)],
        out_specs=pl.BlockSpec((tm, SG_W), lambda i: (i, 0)),
        out_shape=jax.ShapeDtypeStruct((T, SG_W), BF16), compiler_params=_cp("parallel"))(p, sgn, bd, bias)


def _gmlp_bwd(dp, dy, p, sgn, bd, bdt, bias, gsum):
    T = p.shape[0]
    tm = _tile(T, 768, 2 * CHUNK)
    nt = T // tm
    C2 = 2 * CHUNK

    def body(dp_in, dy_ref, p_ref, sgn_ref, bd_ref, bdt_ref, bias_ref, gsum_ref,
             dp_ref, dsg_ref, dws_ref, dbs_ref, acc_w, acc_b):
        del dp_in
        i = pl.program_id(0)

        @pl.when(i == 0)
        def _():
            dsg_ref[...] = jnp.zeros_like(dsg_ref)
            acc_w[...] = jnp.zeros_like(acc_w)
            acc_b[...] = jnp.zeros_like(acc_b)

        u = p_ref[:, 0:256]
        sv = p_ref[:, 256:512]
        ug = _gelu(u)
        x = _gelu(sv)
        r = lax.rsqrt(jnp.mean(x * x, axis=-1, keepdims=True) + EPS)
        xh = x * r
        sg = sgn_ref[...]
        vn = (xh * sg).astype(BF16)
        grp = _lane((C2, SG_W)) // 64
        dug, dvn = [], []
        for s in range(tm // C2):
            rs = slice(s * C2, (s + 1) * C2)
            vs = vn[rs]
            dys = dy_ref[rs, :]
            dug.append(dys * (_gmlp_mix(bd_ref, vs, grp) + bias_ref[...]))
            dmix = dys * ug[rs]
            acc_b[...] += dmix
            dmb = dmix.astype(BF16)
            dvn.append(_gmlp_mix(bdt_ref, dmb, grp))
            for g in range(4):
                acc_w[g] += _dg(jnp.where(grp == g, dmb, jnp.zeros_like(dmb)), vs, NT)
        dug = jnp.concatenate(dug, axis=0)
        dvn = jnp.concatenate(dvn, axis=0)
        dsg_ref[...] += jnp.sum(dvn * xh, axis=0, keepdims=True)
        gd = sg * dvn
        dx = r * (gd - xh * jnp.mean(xh * gd, axis=-1, keepdims=True))
        dp_ref[:, 0:256] = (dug * _gelu_grad(u)).astype(BF16)
        dp_ref[:, 256:512] = (dx * _gelu_grad(sv)).astype(BF16)

        @pl.when(i == nt - 1)
        def _():
            for g in range(4):
                dws_ref[g] = acc_w[g, 0:CHUNK, 0:CHUNK] + acc_w[g, CHUNK:C2, CHUNK:C2]
            dbs_ref[...] = jnp.dot(acc_b[0:CHUNK, :] + acc_b[CHUNK:C2, :], gsum_ref[...],
                                   preferred_element_type=F32, precision=lax.Precision.HIGHEST)

    sec = pl.BlockSpec((tm, 2 * SG_W), lambda i: (i, 3))
    return pl.pallas_call(
        body, grid=(nt,), name="gmlp_bwd",
        in_specs=[ANY, pl.BlockSpec((tm, SG_W), lambda i: (i, 0)), sec, _full(sgn.shape), _full(bd.shape),
                  _full(bdt.shape), _full(bias.shape), _full(gsum.shape)],
        out_specs=[sec, _full((1, SG_W)), _full((4, CHUNK, CHUNK)), _full((CHUNK, LANES))],
        out_shape=[jax.ShapeDtypeStruct(dp.shape, BF16), jax.ShapeDtypeStruct((1, SG_W), F32),
                   jax.ShapeDtypeStruct((4, CHUNK, CHUNK), F32), jax.ShapeDtypeStruct((CHUNK, LANES), F32)],
        scratch_shapes=[pltpu.VMEM((4, C2, C2), F32), pltpu.VMEM((C2, SG_W), F32)],
        input_output_aliases={0: 0}, compiler_params=_cp("arbitrary"))(dp, dy, p, sgn, bd, bdt, bias, gsum)


def _merge_fwd(ya, at, yc, p, wa, wb, wc):
    T = p.shape[0]
    tm = _tile(T, 528)
    n = wa.shape[2]

    def body(ya_ref, at_ref, yc_ref, ga_ref, gb_ref, gc_ref, wa_ref, wb_ref, wc_ref, o_ref):
        yav, atv, ycv = ya_ref[...], at_ref[...], yc_ref[...]
        for j in range(N_CHIPS):
            cs = slice(j * n, (j + 1) * n)
            m = (_sigmoid(ga_ref[:, cs]) * _dot(yav, wa_ref[j]) + _sigmoid(gb_ref[:, cs]) * _dot(atv, wb_ref[j])
                 + _sigmoid(gc_ref[:, cs]) * _dot(ycv, wc_ref[j]))
            o_ref[:, cs] = m.astype(BF16)

    def rows(w, col=0):
        return pl.BlockSpec((tm, w), lambda i: (i, col))

    return pl.pallas_call(
        body, grid=(T // tm,), name="merge_fwd",
        in_specs=[rows(CONV_W), rows(512), rows(SG_W), rows(D_MODEL, 2), rows(D_MODEL, 3), rows(D_MODEL, 4),
                  _full(wa.shape), _full(wb.shape), _full(wc.shape)],
        out_specs=rows(D_MODEL), out_shape=jax.ShapeDtypeStruct((T, D_MODEL), BF16),
        compiler_params=_cp("parallel"))(ya, at, yc, p, p, p, wa, wb, wc)


def _merge_bwd(dyo, ya, at, yc, p, wa, wb, wc, wo):
    T = p.shape[0]
    tm = _tile(T, 528)
    n = wa.shape[2]

    def body(dyo_ref, ya_ref, at_ref, yc_ref, ga_ref, gb_ref, gc_ref, wa_ref, wb_ref, wc_ref, wo_ref,
             dp_ref, dya_ref, doa_ref, dyc_ref, dwa_ref, dwb_ref, dwc_ref):
        i = pl.program_id(0)

        @pl.when(i == 0)
        def _():
            dwa_ref[...] = jnp.zeros_like(dwa_ref)
            dwb_ref[...] = jnp.zeros_like(dwb_ref)
            dwc_ref[...] = jnp.zeros_like(dwc_ref)

        dp_ref[:, 0:OFF_G] = jnp.zeros((tm, OFF_G), BF16)
        dm = _dg(dyo_ref[...], wo_ref[...], NT)
        yav, atv, ycv = ya_ref[...], at_ref[...], yc_ref[...]
        dya = jnp.zeros((tm, CONV_W), F32)
        dat = jnp.zeros((tm, 512), F32)
        dyc = jnp.zeros((tm, SG_W), F32)
        for j in range(N_CHIPS):
            cs = slice(j * n, (j + 1) * n)
            dmj = dm[:, cs]
            for y_in, w_ref, g_ref, dw_ref, which in (
                    (yav, wa_ref, ga_ref, dwa_ref, 0), (atv, wb_ref, gb_ref, dwb_ref, 1),
                    (ycv, wc_ref, gc_ref, dwc_ref, 2)):
                sg = _sigmoid(g_ref[:, cs])
                y = _dot(y_in, w_ref[j])
                c0 = OFF_G + which * D_MODEL + j * n
                dp_ref[:, c0:c0 + n] = (dmj * y * sg * (1.0 - sg)).astype(BF16)
                dyb = (dmj * sg).astype(BF16)
                dw_ref[j] += _dg(y_in, dyb, TN)
                back = _dg(dyb, w_ref[j], NT)
                if which == 0:
                    dya = dya + back
                elif which == 1:
                    dat = dat + back
                else:
                    dyc = dyc + back
        dya_ref[...] = dya
        dyc_ref[...] = dyc
        prod = dat * atv.astype(F32)
        lane = _lane((tm, LANES))
        dat_rows = _heads_to_rows(dat, N_Q_HEADS)
        for h in range(N_Q_HEADS):
            grp = prod[:, (h // 2) * LANES:(h // 2 + 1) * LANES]
            keep = (lane < 64) if h % 2 == 0 else (lane >= 64)
            delta = jnp.sum(jnp.where(keep, grp, 0.0), axis=1, keepdims=True)
            doa_ref[h] = _aug(dat_rows[h], delta).astype(BF16)

    def rows(w, col=0):
        return pl.BlockSpec((tm, w), lambda i: (i, col))

    return pl.pallas_call(
        body, grid=(T // tm,), name="merge_bwd",
        in_specs=[rows(D_MODEL), rows(CONV_W), rows(512), rows(SG_W), rows(D_MODEL, 2), rows(D_MODEL, 3),
                  rows(D_MODEL, 4), _full(wa.shape), _full(wb.shape), _full(wc.shape), _full(wo.shape)],
        out_specs=[rows(IN_W), rows(CONV_W),
                   pl.BlockSpec((N_Q_HEADS, tm, LANES), lambda i: (0, i, 0)), rows(SG_W),
                   _full(wa.shape), _full(wb.shape), _full(wc.shape)],
        out_shape=[jax.ShapeDtypeStruct((T, IN_W), BF16)] + [
            jax.ShapeDtypeStruct((T, CONV_W), F32), jax.ShapeDtypeStruct((N_Q_HEADS, T, LANES), BF16),
            jax.ShapeDtypeStruct((T, SG_W), F32), jax.ShapeDtypeStruct(wa.shape, F32),
            jax.ShapeDtypeStruct(wb.shape, F32), jax.ShapeDtypeStruct(wc.shape, F32)],
        compiler_params=_cp("arbitrary"))(dyo, ya, at, yc, p, p, p, wa, wb, wc, wo)


def _loss_grad(xf, tgt, n_lat):
    T, D = xf.shape
    tm = _tile(np.gcd(n_lat, T), 512)
    nl = n_lat // tm

    def body(x_ref, t_ref, dy_ref, l_ref):
        i = pl.program_id(0)

        @pl.when(i == 0)
        def _():
            l_ref[...] = jnp.zeros_like(l_ref)

        @pl.when(i < nl)
        def _():
            err = x_ref[...] - t_ref[...]
            dy_ref[...] = err * (1.0 / D)
            sq = jnp.sum(jnp.sum(err * err, axis=1, keepdims=True), axis=0, keepdims=True)
            l_ref[...] += (0.5 / D) * sq

        @pl.when(i >= nl)
        def _():
            dy_ref[...] = jnp.zeros_like(dy_ref)

    return pl.pallas_call(
        body, grid=(T // tm,), name="loss_grad",
        in_specs=[pl.BlockSpec((tm, D), lambda i: (i, 0)), pl.BlockSpec((tm, D), lambda i: (jnp.minimum(i, nl - 1), 0))],
        out_specs=[pl.BlockSpec((tm, D), lambda i: (i, 0)), _full((8, LANES))],
        out_shape=[jax.ShapeDtypeStruct((T, D), F32), jax.ShapeDtypeStruct((8, LANES), F32)],
        compiler_params=_cp("arbitrary"))(xf, tgt)


def _row_tile(R, C):
    if R * C <= (1 << 19) or R % 8:
        return R
    return _tile(R, max(8, (1 << 19) // C), 8)


def _adamw(w, m, v, g1, g2=None):
    shape = w.shape
    C = shape[-1]
    R = int(np.prod(shape[:-1])) if len(shape) > 1 else 1
    tr = _row_tile(R, C)
    ins = [a.reshape(R, C) for a in ((w, m, v, g1) if g2 is None else (w, m, v, g1, g2))]

    def body(*refs):
        w_ref, m_ref, v_ref = refs[0], refs[1], refs[2]
        g_ref, d_ref, m2_ref, v2_ref = refs[-4:]
        g = refs[3][...] if g2 is None else refs[3][...] + refs[4][...]
        m2 = ADAM_B1 * m_ref[...] + (1.0 - ADAM_B1) * g
        v2 = ADAM_B2 * v_ref[...] + (1.0 - ADAM_B2) * (g * g)
        m_hat = m2 / (1.0 - ADAM_B1 ** ADAM_STEP)
        v_hat = v2 / (1.0 - ADAM_B2 ** ADAM_STEP)
        g_ref[...] = g
        d_ref[...] = -ADAM_LR * (m_hat / (jnp.sqrt(v_hat) + ADAM_EPS) + ADAM_WD * w_ref[...])
        m2_ref[...] = m2
        v2_ref[...] = v2

    spec = pl.BlockSpec((tr, C), lambda i: (i, 0))
    outs = pl.pallas_call(
        body, grid=(R // tr,), name="adamw", in_specs=[spec] * len(ins), out_specs=[spec] * 4,
        out_shape=[jax.ShapeDtypeStruct((R, C), F32)] * 4, compiler_params=_cp("parallel"))(*ins)
    return [o.reshape(shape) for o in outs]


def _sum_lead(x, name):
    n, R, C = x.shape
    tr = _row_tile(R, C * n)

    def body(x_ref, o_ref):
        acc = x_ref[0].astype(F32)
        for s in range(1, n):
            acc = acc + x_ref[s].astype(F32)
        o_ref[...] = acc

    return pl.pallas_call(
        body, grid=(R // tr,), name=name, in_specs=[pl.BlockSpec((n, tr, C), lambda i: (0, i, 0))],
        out_specs=pl.BlockSpec((tr, C), lambda i: (i, 0)), out_shape=jax.ShapeDtypeStruct((R, C), F32),
        compiler_params=_cp("parallel"))(x)


def _silu(x):
    return x * _sigmoid(x)


def _mod_fwd(a_raw, w_mod, bsh):
    L, D, n = w_mod.shape

    def body(a_ref, w_ref, b_ref, o_ref):
        o_ref[...] = _dot(_silu(a_ref[...]).astype(BF16), w_ref[...].astype(BF16)) + b_ref[...]

    return pl.pallas_call(
        body, grid=(L,), name="mod_fwd",
        in_specs=[_full(a_raw.shape), pl.BlockSpec((None, D, n), lambda l: (l, 0, 0)),
                  pl.BlockSpec((None, 1, n), lambda l: (l, 0, 0))],
        out_specs=pl.BlockSpec((None, 16, n), lambda l: (l, 0, 0)),
        out_shape=jax.ShapeDtypeStruct((L, 16, n), F32), compiler_params=_cp("parallel"))(a_raw, w_mod, bsh)


def _wmod_grad(a_raw, dms):
    L, _, n = dms.shape
    D = a_raw.shape[1]

    def body(a_ref, dm_ref, o_ref):
        o_ref[...] = _dg(_silu(a_ref[...]).astype(BF16), dm_ref[...].astype(BF16), TN)

    return pl.pallas_call(
        body, grid=(L,), name="wmod_grad",
        in_specs=[_full(a_raw.shape), pl.BlockSpec((None, 16, n), lambda l: (l, 0, 0))],
        out_specs=pl.BlockSpec((None, D, n), lambda l: (l, 0, 0)),
        out_shape=jax.ShapeDtypeStruct((L, D, n), F32), compiler_params=_cp("parallel"))(a_raw, dms)


def _cctx_partial(dmc, w_mod):
    L, D, n = w_mod.shape

    def body(dm_ref, w_ref, o_ref):
        part = _dg(dm_ref[...].astype(BF16), w_ref[...].astype(BF16), NT)

        @pl.when(pl.program_id(0) == 0)
        def _():
            o_ref[...] = part

        @pl.when(pl.program_id(0) > 0)
        def _():
            o_ref[...] += part

    return pl.pallas_call(
        body, grid=(L,), name="cctx_partial",
        in_specs=[pl.BlockSpec((None, 16, n), lambda l: (l, 0, 0)), pl.BlockSpec((None, D, n), lambda l: (l, 0, 0))],
        out_specs=_full((16, D)), out_shape=jax.ShapeDtypeStruct((16, D), F32),
        compiler_params=_cp("arbitrary"))(dmc, w_mod)


def _cctx_final(parts, cc):
    def body(p_ref, c_ref, o_ref):
        s = p_ref[0, 0:8, :]
        for j in range(1, N_CHIPS):
            s = s + p_ref[2 * j, 0:8, :]
        xv = c_ref[...]
        sg = _sigmoid(xv)
        o_ref[...] = s * (sg * (1.0 + xv * (1.0 - sg)))

    return pl.pallas_call(
        body, name="cctx_final", in_specs=[_full(parts.shape), _full(cc.shape)], out_specs=_full((8, LANES)),
        out_shape=jax.ShapeDtypeStruct((8, LANES), F32), compiler_params=_cp())(parts, cc)


def _me():
    return lax.axis_index("x"), lax.axis_index("y"), lax.axis_index("c")


def _flip(v, bit):
    return 1 - v if bit else v


def _remote(src, dst, ssem, rsem, peer):
    return pltpu.make_async_remote_copy(src_ref=src, dst_ref=dst, send_sem=ssem, recv_sem=rsem,
                                        device_id=peer, device_id_type=MESH_ID)


def _ag8(xb, name):
    R = xb.shape[0]

    def body(x_ref, o_ref, ssem, rsem, lsem):
        mx, my, mc = _me()
        me = 4 * mx + 2 * my + mc
        loc = pltpu.make_async_copy(x_ref, o_ref.at[me], lsem.at[0])
        loc.start()
        sends = []
        for k in range(1, N_DEV):
            peer = (_flip(mx, k & 4), _flip(my, k & 2), _flip(mc, k & 1))
            cp = _remote(x_ref, o_ref.at[me], ssem.at[k - 1], rsem.at[k - 1], peer)
            cp.start()
            sends.append((cp, peer))
        for k, (cp, peer) in enumerate(sends):
            pid = 4 * peer[0] + 2 * peer[1] + peer[2]
            _remote(x_ref, o_ref.at[pid], ssem.at[k], rsem.at[k], peer).wait_recv()
        for cp, _ in sends:
            cp.wait_send()
        loc.wait()

    return pl.pallas_call(
        body, name=name, in_specs=[ANY], out_specs=ANY, out_shape=jax.ShapeDtypeStruct((N_DEV, R, LANES), F32),
        scratch_shapes=[pltpu.SemaphoreType.DMA((N_DEV - 1,)), pltpu.SemaphoreType.DMA((N_DEV - 1,)),
                        pltpu.SemaphoreType.DMA((1,))])(xb)


def _plane_peers(mx, my, mc):
    out = []
    for k in range(1, N_CHIPS):
        px, py = _flip(mx, k & 2), _flip(my, k & 1)
        out.append(((px, py, mc), 2 * px + py))
    return out


def _plane_exchange(ins, outs, ssem, rsem, scatter):
    n = len(ins)

    def desc(k, a, arriving):
        mx, my, mc = _me()
        j = 2 * mx + my
        peer, pj = _plane_peers(mx, my, mc)[k]
        src = ins[a].at[pj if scatter else j]
        dst = outs[a].at[pj if arriving else j]
        return _remote(src, dst, ssem.at[k * n + a], rsem.at[k * n + a], peer)

    def start():
        for k in range(N_CHIPS - 1):
            for a in range(n):
                desc(k, a, False).start()

    def wait():
        for k in range(N_CHIPS - 1):
            for a in range(n):
                desc(k, a, True).wait_recv()
        for k in range(N_CHIPS - 1):
            for a in range(n):
                desc(k, a, False).wait_send()

    return start, wait


def _chip_gather(bufs, name):
    n = len(bufs)
    halves = [b.shape[1] // 2 for b in bufs]

    def body(*refs):
        outs = refs[n:2 * n]
        ssem, rsem, fsem, gsem = refs[2 * n:]
        mx, my, mc = _me()
        j = 2 * mx + my
        sib = (mx, my, 1 - mc)

        def half(a, blk, c):
            return outs[a].at[blk, pl.ds(c * halves[a], halves[a]), :]

        peers = _plane_peers(mx, my, mc)
        sends = []
        for k, (peer, _) in enumerate(peers):
            for a in range(n):
                mine = half(a, j, mc)
                cp = _remote(mine, mine, ssem.at[k * n + a], rsem.at[k * n + a], peer)
                cp.start()
                sends.append(cp)
        for k, (peer, pj) in enumerate(peers):
            for a in range(n):
                got = half(a, pj, mc)
                _remote(got, got, ssem.at[k * n + a], rsem.at[k * n + a], peer).wait_recv()
                fw = _remote(got, got, fsem.at[k * n + a], gsem.at[k * n + a], sib)
                fw.start()
                sends.append(fw)
        for k, (_, pj) in enumerate(peers):
            for a in range(n):
                theirs = half(a, pj, 1 - mc)
                _remote(theirs, theirs, fsem.at[k * n + a], gsem.at[k * n + a], sib).wait_recv()
        for cp in sends:
            cp.wait_send()

    sems = pltpu.SemaphoreType.DMA((3 * n,))
    return pl.pallas_call(
        body, name=name, in_specs=[ANY] * n, out_specs=[ANY] * n,
        out_shape=[jax.ShapeDtypeStruct(b.shape, b.dtype) for b in bufs],
        input_output_aliases={a: a for a in range(n)},
        scratch_shapes=[sems, sems, sems, sems])(*bufs)


def _chip_scatter(gs, name):
    n = len(gs)

    def body(*refs):
        ins, outs = refs[:n], refs[n:2 * n]
        ssem, rsem = refs[2 * n:]
        mx, my, mc = _me()
        j = 2 * mx + my
        peers = _plane_peers(mx, my, mc)
        sends = []
        for k, (peer, pj) in enumerate(peers):
            for a in range(n):
                cp = _remote(ins[a].at[pj], outs[a].at[j], ssem.at[k * n + a], rsem.at[k * n + a], peer)
                cp.start()
                sends.append(cp)
        for k, (peer, pj) in enumerate(peers):
            for a in range(n):
                _remote(ins[a].at[pj], outs[a].at[pj], ssem.at[k * n + a], rsem.at[k * n + a], peer).wait_recv()
        for cp in sends:
            cp.wait_send()

    return pl.pallas_call(
        body, name=name, in_specs=[ANY] * n, out_specs=[ANY] * n,
        out_shape=[jax.ShapeDtypeStruct(g.shape, g.dtype) for g in gs],
        scratch_shapes=[pltpu.SemaphoreType.DMA((3 * n,)), pltpu.SemaphoreType.DMA((3 * n,))])(*gs)


def _sibling_swap(xs, name):
    n = len(xs)

    def body(*refs):
        ins, outs = refs[:n], refs[n:2 * n]
        ssem, rsem = refs[2 * n:]
        mx, my, mc = _me()
        cps = [_remote(ins[a], outs[a], ssem.at[a], rsem.at[a], (mx, my, 1 - mc)) for a in range(n)]
        for cp in cps:
            cp.start()
        for cp in cps:
            cp.wait()

    return pl.pallas_call(
        body, name=name, in_specs=[ANY] * n, out_specs=[ANY] * n,
        out_shape=[jax.ShapeDtypeStruct(x.shape, x.dtype) for x in xs],
        scratch_shapes=[pltpu.SemaphoreType.DMA((n,)), pltpu.SemaphoreType.DMA((n,))])(*xs)


def _sibling_halves(gs, name):
    n = len(gs)

    def body(*refs):
        ins, outs = refs[:n], refs[n:2 * n]
        ssem, rsem = refs[2 * n:]
        mx, my, mc = _me()
        cps = []
        for a in range(n):
            h = gs[a].shape[1] // 2
            cps.append(_remote(ins[a].at[:, pl.ds((1 - mc) * h, h), :], outs[a], ssem.at[a], rsem.at[a],
                               (mx, my, 1 - mc)))
        for cp in cps:
            cp.start()
        for cp in cps:
            cp.wait()

    return pl.pallas_call(
        body, name=name, in_specs=[ANY] * n, out_specs=[ANY] * n,
        out_shape=[jax.ShapeDtypeStruct((g.shape[0], g.shape[1] // 2, g.shape[2]), g.dtype) for g in gs],
        scratch_shapes=[pltpu.SemaphoreType.DMA((n,)), pltpu.SemaphoreType.DMA((n,))])(*gs)


def _sibling_fill(hs, name):
    n = len(hs)

    def body(*refs):
        ins, outs = refs[:n], refs[n:2 * n]
        ssem, rsem = refs[2 * n:]
        mx, my, mc = _me()
        cps = []
        for a in range(n):
            h = hs[a].shape[0]
            cps.append(_remote(ins[a], outs[a].at[pl.ds(mc * h, h), :], ssem.at[a], rsem.at[a], (mx, my, 1 - mc)))
        for cp in cps:
            cp.start()
        for a, cp in enumerate(cps):
            h = hs[a].shape[0]
            theirs = outs[a].at[pl.ds((1 - mc) * h, h), :]
            _remote(ins[a], theirs, ssem.at[a], rsem.at[a], (mx, my, 1 - mc)).wait_recv()
            cp.wait_send()

    return pl.pallas_call(
        body, name=name, in_specs=[ANY] * n, out_specs=[ANY] * n,
        out_shape=[jax.ShapeDtypeStruct((2 * x.shape[0], x.shape[1]), x.dtype) for x in hs],
        scratch_shapes=[pltpu.SemaphoreType.DMA((n,)), pltpu.SemaphoreType.DMA((n,))])(*hs)


def _add_cast(g, sb):
    J, h, b = g.shape
    th = _row_tile(h, b * J)

    def body(g_ref, s_ref, o_ref):
        o_ref[...] = (g_ref[...] + s_ref[...].astype(F32)).astype(BF16)

    spec = pl.BlockSpec((J, th, b), lambda i: (0, i, 0))
    return pl.pallas_call(
        body, grid=(h // th,), name="add_planes", in_specs=[spec, spec], out_specs=spec,
        out_shape=jax.ShapeDtypeStruct((J, h, b), BF16), compiler_params=_cp("parallel"))(g, sb)


_WEIGHTS = ("c_ctx", "w_mod", "b_mod", "norm1", "w_in", "q_gain", "k_gain", "conv_w", "sg_norm", "w_s", "b_s",
            "w_a", "w_b", "w_c", "w_o", "norm2", "w_ff1", "w_ff3", "w_ff2")
_BIG = ("w_in", "w_a", "w_b", "w_c", "w_o", "w_ff1", "w_ff3", "w_ff2")


def _constants():
    idx = np.arange(LANES)
    e = (idx[:, None] // 64 == idx[None, :] // 64).astype(np.float32) / 64.0
    c512 = np.arange(512)
    fold = (c512[:, None] % 64 == idx[None, :]).astype(np.float32)
    c256 = np.arange(SG_W)
    gsum = (c256[:, None] // 64 == idx[None, :]).astype(np.float32)
    return jnp.asarray(e, BF16), jnp.asarray(fold, F32), jnp.asarray(gsum, F32)


def _rope_tables(n_lat, n_ctx):
    t = jnp.arange(n_lat)
    inv = ROPE_THETA ** (-jnp.arange(0, HEAD_DIM // 2, 2, dtype=F32) / (HEAD_DIM // 2))
    ar = (t // GRID_W).astype(F32)[:, None] * inv
    ac = (t % GRID_W).astype(F32)[:, None] * inv
    cos = jnp.concatenate([jnp.cos(ar), jnp.cos(ar), jnp.cos(ac), jnp.cos(ac)], axis=1)
    sin = jnp.concatenate([-jnp.sin(ar), jnp.sin(ar), -jnp.sin(ac), jnp.sin(ac)], axis=1)
    cos = jnp.concatenate([cos, jnp.ones((n_ctx, HEAD_DIM), F32)], axis=0)
    sin = jnp.concatenate([sin, jnp.zeros((n_ctx, HEAD_DIM), F32)], axis=0)
    return jnp.concatenate([cos, cos], axis=1), jnp.concatenate([sin, sin], axis=1)


def kernel(x, c, ctx, c_ctx, w_mod, b_mod, norm1, w_in, q_gain, k_gain, conv_w, sg_norm, w_s, b_s, w_a, w_b, w_c, w_o, norm2, w_ff1, w_ff3, w_ff2, loss_target, m_c_ctx, m_w_mod, m_b_mod, m_norm1, m_w_in, m_q_gain, m_k_gain, m_conv_w, m_sg_norm, m_w_s, m_b_s, m_w_a, m_w_b, m_w_c, m_w_o, m_norm2, m_w_ff1, m_w_ff3, m_w_ff2, v_c_ctx, v_w_mod, v_b_mod, v_norm1, v_w_in, v_q_gain, v_k_gain, v_conv_w, v_sg_norm, v_w_s, v_b_s, v_w_a, v_w_b, v_w_c, v_w_o, v_norm2, v_w_ff1, v_w_ff3, v_w_ff2):
    given = dict(locals())
    mx, my, mc = _me()
    chip = 2 * mx + my
    dev = 4 * mx + 2 * my + mc
    L = norm1.shape[0]
    S, Lc = x.shape[1], ctx.shape[1]
    T = S + Lc
    D = D_MODEL
    n_mod, n_in, n_ff = w_mod.shape[2], w_in.shape[2], w_ff1.shape[2]
    n_cw = conv_w.shape[2]
    e_avg, fold, gsum = _constants()
    cos_t, sin_t = _rope_tables(S, Lc)

    cw_rows = (L * 3 * n_cw) // LANES
    pad = (-(8 + cw_rows)) % 8
    buf = jnp.concatenate([c.reshape(8, LANES), conv_w.reshape(cw_rows, LANES), jnp.zeros((pad, LANES), F32)], axis=0)
    g1 = _ag8(buf, "gather_cond")
    conds = g1[:, :8].reshape(N_DEV, D)
    cw_full = jnp.stack([g1[2 * j, 8:8 + cw_rows].reshape(L, 3, n_cw) for j in range(N_CHIPS)], axis=2)
    cw_full = cw_full.reshape(L, 3, N_CHIPS * n_cw)
    cw8 = jnp.pad(cw_full, ((0, 0), (0, 5), (0, 0)))
    a_raw = jnp.concatenate([conds, c_ctx[None], jnp.zeros((7, D), F32)], axis=0)
    bsh = lax.dynamic_slice_in_dim(b_mod, chip * n_mod, n_mod, axis=1)[:, None, :]
    mod_sh = _mod_fwd(a_raw, w_mod, bsh)
    g2 = _ag8(mod_sh.reshape(-1, LANES), "gather_mod")
    mods = jnp.stack([g2[2 * j].reshape(L, 16, n_mod) for j in range(N_CHIPS)], axis=2).reshape(L, 16, N_CHIPS * n_mod)
    lat = lax.dynamic_index_in_dim(mods, dev, axis=1, keepdims=False)
    mod = jnp.stack([lat.reshape(L, 6, D), mods[:, 8].reshape(L, 6, D)], axis=1)
    mod = jnp.pad(mod, ((0, 0), (0, 0), (0, 2), (0, 0)))

    qg = jnp.tile(q_gain, (1, N_Q_HEADS))[:, None, :]
    kg = jnp.tile(k_gain, (1, N_KV_HEADS))[:, None, :]
    sgn = sg_norm[:, None, :]
    ws_b = w_s.astype(BF16)
    zero = jnp.zeros_like(ws_b)
    bd = jnp.concatenate([jnp.concatenate([ws_b, zero], axis=3), jnp.concatenate([zero, ws_b], axis=3)], axis=2)
    bdt = jnp.swapaxes(bd, 2, 3)
    bias = jnp.tile(jnp.repeat(jnp.swapaxes(b_s, 1, 2), SG_W // 4, axis=2), (1, 2, 1))

    def shard_bufs(l):
        return [lax.dynamic_update_slice(jnp.zeros((N_CHIPS,) + given[nm].shape[1:], BF16),
                                         given[nm][l].astype(BF16)[None], (chip, 0, 0)) for nm in _BIG]

    def unpack(bufs):
        win, wa, wb, wc, wo, w1, w3, w2 = bufs
        return win, wa, wb, wc, wo.reshape(1, D, D), w1, w3, w2

    def layer_fwd(X, l, W, nxt):
        win, wa, wb, wc, wo, w1, w3, w2 = W
        h = _norm_mod(X, norm1[l][None], mod[l], 0, 1, S)
        p = _mm_nn(h, win, F32, "in_proj")
        ya = _conv_fwd(p, cw8[l], S)
        q, k, v = _qkv_prep(p, cos_t, sin_t, qg[l], kg[l], e_avg)
        at, qa, *got = _flash_fwd(q.reshape(N_KV_HEADS, GROUP, T, LANES), k, v, S, nxt)
        yc = _gmlp_fwd(p, sgn[l], bd[l], bias[l])
        mg = _merge_fwd(ya, at, yc, p, wa, wb, wc)
        X1, f1 = _mm_res(mg[None], wo, X, mod[l], 2, S, "out_proj")
        h2 = _norm_mod(X1, norm2[l][None], mod[l], 3, 4, S)
        a1, a3, act = _ffn_up(h2, w1, w3)
        X2, f2 = _mm_res(act, w2, X1, mod[l], 5, S, "ffn_down")
        return X2, got, dict(X=X, h=h, p=p, ya=ya, k=k, v=v, at=at, qa=qa, yc=yc, mg=mg, X1=X1, f1=f1, h2=h2,
                             a1=a1, a3=a3, act=act, f2=f2)

    def layer_bwd(dX2, l, W, sv, pending):
        win, wa, wb, wc, wo, w1, w3, w2 = W
        dyf, dgt2 = _gate_bwd(dX2, sv["f2"], mod[l], 5, S)
        da1, da3 = _ffn_down_bwd(dyf, w2, sv["a1"], sv["a3"])
        dw2 = _mm_tn(sv["act"], dyf, _shard_rows(n_ff), _rows(D), N_CHIPS, n_ff, D, T, "dw_ff2")
        dh2 = _mm_nt_acc([da1, da3], [w1, w3], False, "ffn_up_bwd")
        dw1 = _mm_tn(sv["h2"], da1, _rows(D), _shard_rows(n_ff), N_CHIPS, D, n_ff, T, "dw_ff1")
        dw3 = _mm_tn(sv["h2"], da3, _rows(D), _shard_rows(n_ff), N_CHIPS, D, n_ff, T, "dw_ff3")
        dX1, dn2, dsh2, dsc2 = _norm_mod_bwd(sv["X1"], dh2, dX2, norm2[l][None], mod[l], 4, S)
        dyo, dgt1 = _gate_bwd(dX1, sv["f1"], mod[l], 2, S)
        dwo = _mm_tn(sv["mg"], dyo, _rows(D), _rows(D), 1, D, D, T, "dw_o")
        dp, dya, doa, dyc, dwa, dwb, dwc = _merge_bwd(dyo, sv["ya"], sv["at"], sv["yc"], sv["p"], wa, wb, wc, wo[0])
        dp, dcw = _conv_bwd(dp, dya, sv["p"], cw8[l], S)
        dp, dsg, dws, dbs = _gmlp_bwd(dp, dyc, sv["p"], sgn[l], bd[l], bdt[l], bias[l], gsum)
        dq, dk, dv, *recv = _flash_bwd(sv["qa"], doa.reshape(N_KV_HEADS, GROUP, T, LANES), sv["k"], sv["v"], S,
                                       pending)
        dp, dqg, dkg = _qkv_prep_bwd(dp, dq.reshape(N_Q_HEADS, T, LANES), dk, dv, sv["p"], cos_t, sin_t,
                                     qg[l], kg[l], e_avg, fold)
        dh = _mm_nt_acc([dp], [win], True, "in_proj_bwd")
        dwin = _mm_tn(sv["h"], dp, _rows(D), _row_cols(n_in), N_CHIPS, D, n_in, T, "dw_in")
        dX0, dn1, dsh1, dsc1 = _norm_mod_bwd(sv["X"], dh, dX1, norm1[l][None], mod[l], 1, S)
        dmod = jnp.concatenate([dsh1, dsc1, dgt1, dsh2, dsc2, dgt2], axis=1)
        big = [dwin, dwa, dwb, dwc, dwo.reshape(N_CHIPS, D // N_CHIPS, D), dw1, dw3, dw2]
        small = dict(norm1=dn1[0], norm2=dn2[0], q_gain=dqg[0, :HEAD_DIM], k_gain=dkg[0, :HEAD_DIM],
                     conv_w=dcw[:3], sg_norm=dsg[0], w_s=dws, b_s=jnp.swapaxes(dbs[:, :4], 0, 1), dmod=dmod)
        return dX0, big, small, recv

    X = jnp.concatenate([x[0], ctx[0]], axis=0)
    Ws, saved = [unpack(_chip_gather(shard_bufs(0), "gather_weights"))], []
    for l in range(L):
        X, got, sv = layer_fwd(X, l, Ws[l], shard_bufs(l + 1) if l + 1 < L else ())
        if got:
            Ws.append(unpack(got))
        saved.append(sv)
    dX, lpart = _loss_grad(X, loss_target[0], S)
    loss = lax.psum(lpart[0, 0], ("x", "y", "c"))

    out = {nm: [None] * L for nm in _BIG}
    smalls = [None] * L

    def own_block(r, g):
        return lax.dynamic_update_slice(r, lax.dynamic_slice_in_dim(g, chip, 1, axis=0), (chip, 0, 0))

    def update(l, grads):
        for nm, g in zip(_BIG, grads):
            out[nm][l] = _adamw(given[nm][l], given["m_" + nm][l], given["v_" + nm][l], *g)

    pending = ()
    for l in reversed(range(L)):
        dX, big, smalls[l], recv = layer_bwd(dX, l, Ws[l], saved[l], pending)
        if recv:
            mine = [_sum_lead(own_block(r, g), "sum_chips") for r, g in zip(recv, pending)]
            update(l + 1, zip(mine, _sibling_swap(mine, "swap_planes")))
        pending = [g.astype(BF16) for g in big]
    sib = _sibling_halves(pending, "swap_halves")
    own = [lax.dynamic_slice_in_dim(g, mc * (g.shape[1] // 2), g.shape[1] // 2, axis=1) for g in big]
    sent = [_add_cast(g, s_) for g, s_ in zip(own, sib)]
    recv = [own_block(r, g) for r, g in zip(_chip_scatter(sent, "scatter_grads"), sent)]
    halves = [_sum_lead(r, "sum_chips") for r in recv]
    full = _sibling_fill(halves, "fill_halves")
    update(0, [(lax.dynamic_update_slice(f, hv, (mc * hv.shape[0], 0)),) for f, hv in zip(full, halves)])
    grad_x = dX[:S][None]

    def flat(nm):
        return jnp.stack([smalls[l][nm] for l in range(L)]).reshape(-1)

    dmod_all = jnp.stack([smalls[l]["dmod"] for l in range(L)])
    dml = dmod_all[:, 0].reshape(-1)
    dmc = dmod_all[:, 1].reshape(-1)
    names = ("norm1", "q_gain", "k_gain", "conv_w", "sg_norm", "w_s", "b_s", "norm2")
    parts = [dml, dml + dmc, dmc] + [flat(nm) for nm in names]
    sizes = [int(a.shape[0]) for a in parts]
    total = sum(sizes)
    padn = (-total) % (8 * LANES)
    sbuf = jnp.concatenate(parts + [jnp.zeros((padn,), F32)]).reshape(-1, LANES)
    g3 = _ag8(sbuf, "gather_small")
    ssum = _sum_lead(g3, "sum_devices").reshape(-1)
    offs = np.cumsum([0] + sizes)
    seg = {nm: ssum[offs[i + 3]:offs[i + 4]] for i, nm in enumerate(names)}
    gb_mod = ssum[offs[1]:offs[2]].reshape(L, N_CHIPS * n_mod)
    dmc_sum = ssum[offs[2]:offs[3]].reshape(L, N_CHIPS * n_mod)
    dml_all = g3.reshape(N_DEV, -1)[:, :sizes[0]].reshape(N_DEV, L, N_CHIPS * n_mod)
    dml_sh = jnp.swapaxes(lax.dynamic_slice_in_dim(dml_all, chip * n_mod, n_mod, axis=2), 0, 1)
    dmc_sh = lax.dynamic_slice_in_dim(dmc_sum, chip * n_mod, n_mod, axis=1)[:, None, :]
    dms = jnp.concatenate([dml_sh, dmc_sh, jnp.zeros((L, 7, n_mod), F32)], axis=1)
    g_wmod = _wmod_grad(a_raw, dms)
    part = _cctx_partial(jnp.concatenate([dmc_sh, jnp.zeros((L, 15, n_mod), F32)], axis=1), w_mod)
    g4 = _ag8(part.reshape(-1, LANES), "gather_cctx")
    g_cctx = _cctx_final(g4, c_ctx.reshape(8, LANES)).reshape(D)

    g_conv = lax.dynamic_slice_in_dim(seg["conv_w"].reshape(L, 3, N_CHIPS * n_cw), chip * n_cw, n_cw, axis=2)
    small_g = dict(c_ctx=g_cctx, w_mod=g_wmod, b_mod=gb_mod, norm1=seg["norm1"].reshape(norm1.shape),
                   q_gain=seg["q_gain"].reshape(q_gain.shape), k_gain=seg["k_gain"].reshape(k_gain.shape),
                   conv_w=g_conv, sg_norm=seg["sg_norm"].reshape(sg_norm.shape), w_s=seg["w_s"].reshape(w_s.shape),
                   b_s=seg["b_s"].reshape(b_s.shape), norm2=seg["norm2"].reshape(norm2.shape))
    res = {}
    for nm in _WEIGHTS:
        if nm in _BIG:
            res[nm] = [jnp.stack([out[nm][l][k] for l in range(L)]) for k in range(4)]
        else:
            res[nm] = _adamw(given[nm], given["m_" + nm], given["v_" + nm], small_g[nm])
    return (loss, grad_x, *[res[nm][0] for nm in _WEIGHTS], *[res[nm][1] for nm in _WEIGHTS],
            *[res[nm][2] for nm in _WEIGHTS], *[res[nm][3] for nm in _WEIGHTS])
```

```python
import functools

import jax
import jax.numpy as jnp
import numpy as np
from jax import lax
from jax.experimental import pallas as pl
from jax.experimental.pallas import tpu as pltpu

F32 = jnp.float32
BF16 = jnp.bfloat16
EPS = 1e-6
D_MODEL = 1024
HEAD_DIM = 64
N_Q_HEADS = 8
N_KV_HEADS = 2
GROUP = N_Q_HEADS // N_KV_HEADS
GRID_W = 64
ROPE_THETA = 10000.0
CHUNK = 128
CONV_W = 256
SG_W = 256
OFF_Q = 3 * CONV_W
QKV_W = 768
OFF_U = OFF_Q + QKV_W
OFF_G = OFF_U + 2 * SG_W
IN_W = OFF_G + 3 * D_MODEL
N_CHIPS = 4
N_DEV = 8
LANES = 128
UNROLL_FWD = 8
UNROLL_BWD = 4
AUG = 3
ADAM_LR, ADAM_B1, ADAM_B2, ADAM_EPS, ADAM_WD, ADAM_STEP = 0.001, 0.9, 0.999, 1e-8, 0.01, 10
VMEM_LIMIT_V7X = 52 * 1024 * 1024
MXU_DEPTH_V7X = 256
MESH_ID = pl.DeviceIdType.MESH
NT = (((1,), (1,)), ((), ()))
TN = (((0,), (0,)), ((), ()))
ANY = pl.BlockSpec(memory_space=pl.ANY)


def _cp(*sem):
    return pltpu.CompilerParams(dimension_semantics=sem or None, vmem_limit_bytes=VMEM_LIMIT_V7X)


def _tile(n, target, mult=16):
    best = None
    for t in range(mult, n + 1, mult):
        if n % t == 0 and t <= target:
            best = t
    assert best is not None, (n, target, mult)
    return best


def _full(shape):
    nd = len(shape)
    return pl.BlockSpec(tuple(shape), lambda *_: (0,) * nd)


def _segments(i, tm, n_lat, fn):
    k, off = divmod(n_lat, tm)

    @pl.when(i < k)
    def _():
        fn(0, tm, 0)

    @pl.when(i == k)
    def _():
        if off:
            fn(0, off, 0)
        fn(off, tm, 1)

    @pl.when(i > k)
    def _():
        fn(0, tm, 1)


def _dot(a, b):
    return jnp.dot(a, b, preferred_element_type=F32)


def _dg(a, b, dims):
    return lax.dot_general(a, b, dims, preferred_element_type=F32)


def _split3(x):
    hi = x.astype(BF16)
    r1 = x - hi.astype(F32)
    mid = r1.astype(BF16)
    lo = (r1 - mid.astype(F32)).astype(BF16)
    return hi.astype(F32), mid.astype(F32), lo.astype(F32)


def _lane(shape):
    return lax.broadcasted_iota(jnp.int32, shape, len(shape) - 1)


def _aug(val, stat):
    lane = _lane(val.shape)
    hi, mid, lo = _split3(stat)
    ext = jnp.where(lane == 64, hi, jnp.where(lane == 65, mid, jnp.where(lane == 66, lo, 0.0)))
    return jnp.where(lane < 64, val, ext)


def _seg_mean(x, e):
    outs = []
    for g in range(x.shape[1] // LANES):
        blk = x[:, g * LANES:(g + 1) * LANES]
        hi = blk.astype(BF16)
        lo = (blk - hi.astype(F32)).astype(BF16)
        outs.append(_dot(hi, e) + _dot(lo, e))
    return outs[0] if len(outs) == 1 else jnp.concatenate(outs, axis=1)


def _rope(x, cos, sin_signed, inverse):
    w = x.shape[1]
    reps = w // LANES
    c = cos if reps == 1 else jnp.tile(cos, (1, reps))
    s = sin_signed if reps == 1 else jnp.tile(sin_signed, (1, reps))
    first = (_lane(x.shape) % 32) < 16
    partner = jnp.where(first, pltpu.roll(x, w - 16, 1), pltpu.roll(x, 16, 1))
    return x * c - partner * s if inverse else x * c + partner * s


def _sigmoid(x):
    return 1.0 / (1.0 + jnp.exp(-x))


_GELU_K = 0.7978845608028654
_GELU_C = 0.044715


def _gelu(x):
    return 0.5 * x * (1.0 + jnp.tanh(_GELU_K * (x + _GELU_C * x * x * x)))


def _gelu_grad(x):
    t = jnp.tanh(_GELU_K * (x + _GELU_C * x * x * x))
    return 0.5 * (1.0 + t) + 0.5 * x * (1.0 - t * t) * _GELU_K * (1.0 + 3.0 * _GELU_C * x * x)


def _loop_unrolled(n, step, init, unroll):
    def trip(t, carry):
        for u in range(unroll):
            carry = step(t * unroll + u, carry)
        return carry

    carry = lax.fori_loop(0, n // unroll, trip, init) if n >= unroll else init
    for r in range(n - n % unroll, n):
        carry = step(r, carry)
    return carry


def _heads_to_rows(x, n_heads):
    out = []
    for h in range(n_heads):
        grp = x[:, (h // 2) * LANES:(h // 2 + 1) * LANES]
        out.append(grp if h % 2 == 0 else pltpu.roll(grp, 64, 1))
    return out


def _rows_to_heads(blocks):
    outs = []
    lane = _lane(blocks[0].shape)
    for a in range(len(blocks) // 2):
        outs.append(jnp.where(lane < 64, blocks[2 * a], pltpu.roll(blocks[2 * a + 1], 64, 1)))
    return outs[0] if len(outs) == 1 else jnp.concatenate(outs, axis=1)


def _norm_mod(x, g, mod, i_shift, i_scale, n_lat):
    T, D = x.shape
    tm = _tile(T, 528)

    def body(x_ref, g_ref, mod_ref, h_ref):
        def fn(r0, r1, seg):
            xv = x_ref[r0:r1, :]
            r = lax.rsqrt(jnp.mean(xv * xv, axis=-1, keepdims=True) + EPS)
            n = xv * r * g_ref[...]
            h = n * (1.0 + mod_ref[seg, i_scale:i_scale + 1, :]) + mod_ref[seg, i_shift:i_shift + 1, :]
            h_ref[r0:r1, :] = h.astype(BF16)

        _segments(pl.program_id(0), tm, n_lat, fn)

    return pl.pallas_call(
        body, grid=(T // tm,), name="norm_mod",
        in_specs=[pl.BlockSpec((tm, D), lambda i: (i, 0)), _full(g.shape), _full(mod.shape)],
        out_specs=pl.BlockSpec((tm, D), lambda i: (i, 0)),
        out_shape=jax.ShapeDtypeStruct((T, D), BF16), compiler_params=_cp("parallel"))(x, g, mod)


def _norm_mod_bwd(x, dh, dres, g, mod, i_scale, n_lat):
    T, D = x.shape
    tm = _tile(T, 528)

    def body(x_ref, dh_ref, dres_ref, g_ref, mod_ref, dx_ref, dg_ref, dsh_ref, dsc_ref):
        i = pl.program_id(0)

        @pl.when(i == 0)
        def _():
            dg_ref[...] = jnp.zeros_like(dg_ref)
            dsh_ref[...] = jnp.zeros_like(dsh_ref)
            dsc_ref[...] = jnp.zeros_like(dsc_ref)

        def fn(r0, r1, seg):
            xv = x_ref[r0:r1, :]
            dh = dh_ref[r0:r1, :]
            r = lax.rsqrt(jnp.mean(xv * xv, axis=-1, keepdims=True) + EPS)
            xh = xv * r
            gv = g_ref[...]
            dsh_ref[seg] += jnp.sum(dh, axis=0, keepdims=True)
            dsc_ref[seg] += jnp.sum(dh * (xh * gv), axis=0, keepdims=True)
            dn = dh * (1.0 + mod_ref[seg, i_scale:i_scale + 1, :])
            dg_ref[...] += jnp.sum(dn * xh, axis=0, keepdims=True)
            gd = gv * dn
            dx_ref[r0:r1, :] = dres_ref[r0:r1, :] + r * (gd - xh * jnp.mean(xh * gd, axis=-1, keepdims=True))

        _segments(i, tm, n_lat, fn)

    row = pl.BlockSpec((tm, D), lambda i: (i, 0))
    return pl.pallas_call(
        body, grid=(T // tm,), name="norm_mod_bwd",
        in_specs=[row, row, row, _full(g.shape), _full(mod.shape)],
        out_specs=[row, _full((1, D)), _full((2, 1, D)), _full((2, 1, D))],
        out_shape=[jax.ShapeDtypeStruct((T, D), F32), jax.ShapeDtypeStruct((1, D), F32),
                   jax.ShapeDtypeStruct((2, 1, D), F32), jax.ShapeDtypeStruct((2, 1, D), F32)],
        compiler_params=_cp("arbitrary"))(x, dh, dres, g, mod)


def _gate_bwd(dx, f, mod, i_gate, n_lat):
    T, D = dx.shape
    tm = _tile(T, 528)

    def body(dx_ref, f_ref, mod_ref, dy_ref, dg_ref):
        i = pl.program_id(0)

        @pl.when(i == 0)
        def _():
            dg_ref[...] = jnp.zeros_like(dg_ref)

        def fn(r0, r1, seg):
            dxv = dx_ref[r0:r1, :]
            dy_ref[r0:r1, :] = (dxv * mod_ref[seg, i_gate:i_gate + 1, :]).astype(BF16)
            dg_ref[seg] += jnp.sum(dxv * f_ref[r0:r1, :].astype(F32), axis=0, keepdims=True)

        _segments(i, tm, n_lat, fn)

    row = pl.BlockSpec((tm, D), lambda i: (i, 0))
    return pl.pallas_call(
        body, grid=(T // tm,), name="gate_bwd",
        in_specs=[row, row, _full(mod.shape)], out_specs=[row, _full((2, 1, D))],
        out_shape=[jax.ShapeDtypeStruct((T, D), BF16), jax.ShapeDtypeStruct((2, 1, D), F32)],
        compiler_params=_cp("arbitrary"))(dx, f, mod)


def _mm_nn(a, w, out_dtype, name):
    M, K = a.shape
    J, _, n = w.shape
    tm = _tile(M, 1056)

    def body(a_ref, w_ref, o_ref):
        o_ref[...] = _dot(a_ref[...], w_ref[...]).astype(o_ref.dtype)

    return pl.pallas_call(
        body, grid=(M // tm, J), name=name,
        in_specs=[pl.BlockSpec((tm, K), lambda i, j: (i, 0)), pl.BlockSpec((None, K, n), lambda i, j: (j, 0, 0))],
        out_specs=pl.BlockSpec((tm, n), lambda i, j: (i, j)),
        out_shape=jax.ShapeDtypeStruct((M, J * n), out_dtype), compiler_params=_cp("parallel", "arbitrary"))(a, w)


def _mm_res(a3, w, res, mod, i_gate, n_lat, name):
    J, M, k = a3.shape
    N = w.shape[2]
    tm = _tile(M, 528)

    def body(a_ref, w_ref, res_ref, mod_ref, x_ref, f_ref):
        acc = _dot(a_ref[0], w_ref[0])
        for j in range(1, J):
            acc += _dot(a_ref[j], w_ref[j])
        f_ref[...] = acc.astype(BF16)

        def fn(r0, r1, seg):
            x_ref[r0:r1, :] = res_ref[r0:r1, :] + mod_ref[seg, i_gate:i_gate + 1, :] * acc[r0:r1, :]

        _segments(pl.program_id(0), tm, n_lat, fn)

    row = pl.BlockSpec((tm, N), lambda i: (i, 0))
    return pl.pallas_call(
        body, grid=(M // tm,), name=name,
        in_specs=[pl.BlockSpec((J, tm, k), lambda i: (0, i, 0)), _full(w.shape), row, _full(mod.shape)],
        out_specs=[row, row],
        out_shape=[jax.ShapeDtypeStruct((M, N), F32), jax.ShapeDtypeStruct((M, N), BF16)],
        compiler_params=_cp("parallel"))(a3, w, res, mod)


def _mm_nt_acc(dys, ws, row_major, name):
    J, K, n = ws[0].shape
    M = dys[0].shape[0] if row_major else dys[0].shape[1]
    tm = _tile(M, 1056)
    P = len(dys)

    def body(*refs):
        o_ref = refs[2 * P]
        j = pl.program_id(1)
        part = _dg(refs[0][...], refs[P][...], NT)
        for p in range(1, P):
            part += _dg(refs[p][...], refs[P + p][...], NT)

        @pl.when(j == 0)
        def _():
            o_ref[...] = part

        @pl.when(j > 0)
        def _():
            o_ref[...] += part

    dy_spec = (pl.BlockSpec((tm, n), lambda i, j: (i, j)) if row_major
               else pl.BlockSpec((None, tm, n), lambda i, j: (j, i, 0)))
    w_spec = pl.BlockSpec((None, K, n), lambda i, j: (j, 0, 0))
    return pl.pallas_call(
        body, grid=(M // tm, J), name=name,
        in_specs=[dy_spec] * P + [w_spec] * P,
        out_specs=pl.BlockSpec((tm, K), lambda i, j: (i, 0)),
        out_shape=jax.ShapeDtypeStruct((M, K), F32), compiler_params=_cp("parallel", "arbitrary"))(*dys, *ws)


def _mm_tn(x, dy, x_spec, dy_spec, J, K, n, T, name):
    tk = _tile(T, 1056, MXU_DEPTH_V7X)
    nt = T // tk

    def body(x_ref, dy_ref, o_ref, acc):
        t = pl.program_id(1)
        part = _dg(x_ref[...], dy_ref[...], TN)

        @pl.when(t == 0)
        def _():
            acc[...] = part

        @pl.when(t > 0)
        def _():
            acc[...] += part

        @pl.when(t == nt - 1)
        def _():
            o_ref[...] = acc[...].astype(BF16)

    return pl.pallas_call(
        body, grid=(J, nt), name=name,
        in_specs=[x_spec(tk), dy_spec(tk)],
        out_specs=pl.BlockSpec((None, K, n), lambda j, t: (j, 0, 0)),
        out_shape=jax.ShapeDtypeStruct((J, K, n), BF16), scratch_shapes=[pltpu.VMEM((K, n), F32)],
        compiler_params=_cp("parallel", "arbitrary"))(x, dy)


def _rows(width):
    return lambda tk: pl.BlockSpec((tk, width), lambda j, t: (t, 0))


def _row_cols(width):
    return lambda tk: pl.BlockSpec((tk, width), lambda j, t: (t, j))


def _shard_rows(width):
    return lambda tk: pl.BlockSpec((None, tk, width), lambda j, t: (j, t, 0))


def _ffn_up(h, w1, w3):
    T, D = h.shape
    J, _, n = w1.shape
    tm = _tile(T, 1056)

    def body(h_ref, w1_ref, w3_ref, a1_ref, a3_ref, act_ref):
        hv = h_ref[...]
        a1 = _dot(hv, w1_ref[...])
        a3 = _dot(hv, w3_ref[...])
        a1_ref[...] = a1.astype(BF16)
        a3_ref[...] = a3.astype(BF16)
        act_ref[...] = (a1 * _sigmoid(a1) * a3).astype(BF16)

    w_spec = pl.BlockSpec((None, D, n), lambda i, j: (j, 0, 0))
    o_spec = pl.BlockSpec((None, tm, n), lambda i, j: (j, i, 0))
    return pl.pallas_call(
        body, grid=(T // tm, J), name="ffn_up",
        in_specs=[pl.BlockSpec((tm, D), lambda i, j: (i, 0)), w_spec, w_spec], out_specs=[o_spec] * 3,
        out_shape=[jax.ShapeDtypeStruct((J, T, n), BF16)] * 3,
        compiler_params=_cp("parallel", "arbitrary"))(h, w1, w3)


def _ffn_down_bwd(dy, w2, a1, a3):
    T, D = dy.shape
    J, n, _ = w2.shape
    tm = _tile(T, 1056)

    def body(dy_ref, w2_ref, a1_ref, a3_ref, da1_ref, da3_ref):
        dact = _dg(dy_ref[...], w2_ref[...], NT)
        a1v = a1_ref[...].astype(F32)
        sig = _sigmoid(a1v)
        da3_ref[...] = (dact * a1v * sig).astype(BF16)
        da1_ref[...] = (dact * a3_ref[...].astype(F32) * (sig * (1.0 + a1v * (1.0 - sig)))).astype(BF16)

    a_spec = pl.BlockSpec((None, tm, n), lambda i, j: (j, i, 0))
    return pl.pallas_call(
        body, grid=(T // tm, J), name="ffn_down_bwd",
        in_specs=[pl.BlockSpec((tm, D), lambda i, j: (i, 0)), pl.BlockSpec((None, n, D), lambda i, j: (j, 0, 0)),
                  a_spec, a_spec],
        out_specs=[a_spec, a_spec], out_shape=[jax.ShapeDtypeStruct((J, T, n), BF16)] * 2,
        compiler_params=_cp("parallel", "arbitrary"))(dy, w2, a1, a3)


def _qkv_prep(p, cos, sin, qg, kg, e):
    T = p.shape[0]
    tm = _tile(T, 528)

    def body(p_ref, cos_ref, sin_ref, qg_ref, kg_ref, e_ref, q_ref, k_ref, v_ref):
        ev = e_ref[...]
        cv, sv = cos_ref[...], sin_ref[...]
        xq = p_ref[:, 0:512]
        qn = xq * lax.rsqrt(_seg_mean(xq * xq, ev) + EPS) * qg_ref[...]
        qr = _rope(qn, cv, sv, False) * (HEAD_DIM ** -0.5)
        xk = p_ref[:, 512:640]
        kn = xk * lax.rsqrt(_seg_mean(xk * xk, ev) + EPS) * kg_ref[...]
        kr = _rope(kn, cv, sv, False)
        lane = _lane((tm, LANES))
        ones = jnp.where(lane < 64 + AUG, -1.0, 0.0)
        for h, blk in enumerate(_heads_to_rows(qr, N_Q_HEADS)):
            q_ref[h] = jnp.where(lane < 64, blk, 0.0).astype(BF16)
        for h, blk in enumerate(_heads_to_rows(kr, N_KV_HEADS)):
            k_ref[h] = jnp.where(lane < 64, blk, ones).astype(BF16)
        for h, blk in enumerate(_heads_to_rows(p_ref[:, 640:768], N_KV_HEADS)):
            v_ref[h] = jnp.where(lane < 64, blk, ones).astype(BF16)

    tab = pl.BlockSpec((tm, LANES), lambda i: (i, 0))
    return pl.pallas_call(
        body, grid=(T // tm,), name="qkv_prep",
        in_specs=[pl.BlockSpec((tm, QKV_W), lambda i: (i, 1)), tab, tab, _full(qg.shape), _full(kg.shape),
                  _full(e.shape)],
        out_specs=[pl.BlockSpec((N_Q_HEADS, tm, LANES), lambda i: (0, i, 0)),
                   pl.BlockSpec((N_KV_HEADS, tm, LANES), lambda i: (0, i, 0)),
                   pl.BlockSpec((N_KV_HEADS, tm, LANES), lambda i: (0, i, 0))],
        out_shape=[jax.ShapeDtypeStruct((N_Q_HEADS, T, LANES), BF16),
                   jax.ShapeDtypeStruct((N_KV_HEADS, T, LANES), BF16),
                   jax.ShapeDtypeStruct((N_KV_HEADS, T, LANES), BF16)],
        compiler_params=_cp("parallel"))(p, cos, sin, qg, kg, e)


def _qkv_prep_bwd(dp, dq, dk, dv, p, cos, sin, qg, kg, e, fold):
    T = p.shape[0]
    tq = dq.shape[3] // GROUP
    tm = _tile(T, 768, tq)
    nt = T // tm

    def body(dp_in, dq_ref, dk_ref, dv_ref, p_ref, cos_ref, sin_ref, qg_ref, kg_ref, e_ref, fold_ref,
             dp_ref, dqg_ref, dkg_ref, accq, acck):
        del dp_in
        i = pl.program_id(0)

        @pl.when(i == 0)
        def _():
            accq[...] = jnp.zeros_like(accq)
            acck[...] = jnp.zeros_like(acck)

        ev = e_ref[...]
        cv, sv = cos_ref[...], sin_ref[...]

        def one(x, dr, gain, acc):
            r = lax.rsqrt(_seg_mean(x * x, ev) + EPS)
            xh = x * r
            dn = _rope(dr, cv, sv, True)
            acc[0:1, :] += jnp.sum(dn * xh, axis=0, keepdims=True)
            gd = gain * dn
            return r * (gd - xh * _seg_mean(xh * gd, ev))

        slabs = [[dq_ref[h, b].T for b in range(tm // tq)] for h in range(N_KV_HEADS)]
        heads = [jnp.concatenate([sl[g * tq:(g + 1) * tq] for sl in slabs[h]], axis=0)
                 for h in range(N_KV_HEADS) for g in range(GROUP)]
        dqr = _rows_to_heads(heads) * (HEAD_DIM ** -0.5)
        dkr = _rows_to_heads([dk_ref[h] for h in range(N_KV_HEADS)])
        dvv = _rows_to_heads([dv_ref[h] for h in range(N_KV_HEADS)])
        dp_ref[:, 0:512] = one(p_ref[:, 0:512], dqr, qg_ref[...], accq).astype(BF16)
        dp_ref[:, 512:640] = one(p_ref[:, 512:640], dkr, kg_ref[...], acck).astype(BF16)
        dp_ref[:, 640:768] = dvv.astype(BF16)

        @pl.when(i == nt - 1)
        def _():
            fv = fold_ref[...]
            dqg_ref[...] = jnp.dot(accq[...], fv, preferred_element_type=F32, precision=lax.Precision.HIGHEST)
            dkg_ref[...] = jnp.dot(acck[...], fv[0:LANES, :], preferred_element_type=F32,
                                   precision=lax.Precision.HIGHEST)

    tab = pl.BlockSpec((tm, LANES), lambda i: (i, 0))
    sec = pl.BlockSpec((tm, QKV_W), lambda i: (i, 1))
    return pl.pallas_call(
        body, grid=(nt,), name="qkv_prep_bwd",
        in_specs=[ANY, pl.BlockSpec((N_KV_HEADS, tm // tq, LANES, GROUP * tq), lambda i: (0, i, 0, 0)),
                  pl.BlockSpec((N_KV_HEADS, tm, LANES), lambda i: (0, i, 0)),
                  pl.BlockSpec((N_KV_HEADS, tm, LANES), lambda i: (0, i, 0)),
                  sec, tab, tab, _full(qg.shape), _full(kg.shape), _full(e.shape), _full(fold.shape)],
        out_specs=[sec, _full((8, LANES)), _full((8, LANES))],
        out_shape=[jax.ShapeDtypeStruct(dp.shape, BF16), jax.ShapeDtypeStruct((8, LANES), F32),
                   jax.ShapeDtypeStruct((8, LANES), F32)],
        scratch_shapes=[pltpu.VMEM((8, 512), F32), pltpu.VMEM((8, LANES), F32)],
        input_output_aliases={0: 0}, compiler_params=_cp("arbitrary"))(dp, dq, dk, dv, p, cos, sin, qg, kg, e, fold)


def _flash_fwd(q, k, v, n_lat, gather=()):
    _, _, T, _ = q.shape
    tq = tk = 256
    nq = T // tq
    M = GROUP * tq

    nk_lat = n_lat // tk

    n_g = len(gather)

    def body(q_ref, k_ref, v_ref, *rest):
        o_ref, qa_ref = rest[n_g], rest[n_g + 1]
        i = pl.program_id(0)
        if n_g:
            bufs = rest[n_g + 2:2 * n_g + 2]
            start, wait = _plane_exchange(bufs, bufs, rest[-2], rest[-1], False)
            pl.when((i == 0) & (pl.program_id(1) == 0))(start)
        qv = q_ref[...].reshape(M, LANES)

        def step(s, carry):
            m, acc = carry
            r0 = s * tk if isinstance(s, int) else pl.multiple_of(s * tk, tk)
            sc = _dg(qv, k_ref[pl.ds(r0, tk), :], NT)
            m_new = jnp.maximum(m, jnp.max(sc, axis=1, keepdims=True))
            pr = jnp.exp(sc - m_new)
            acc = jnp.exp(m - m_new) * acc + _dot(pr.astype(BF16), v_ref[pl.ds(r0, tk), :])
            return m_new, acc

        def finish(m, acc):
            den = -acc[:, 64:65]
            out = acc / den
            o_ref[...] = _rows_to_heads([out[g * tq:(g + 1) * tq] for g in range(GROUP)]).astype(BF16)
            qa_ref[...] = _aug(qv.astype(F32), m + jnp.log(den)).astype(BF16).reshape(GROUP, tq, LANES)

        init = (jnp.full((M, 1), -1e30, F32), jnp.zeros((M, LANES), F32))

        @pl.when(i < n_lat // tq)
        def _():
            carry = _loop_unrolled(nk_lat, step, init, UNROLL_FWD)
            for s in range(nk_lat, T // tk):
                carry = step(s, carry)
            finish(*carry)

        @pl.when(i >= n_lat // tq)
        def _():
            carry = init
            for s in range(nk_lat, T // tk):
                carry = step(s, carry)
            finish(*carry)

        if n_g:
            pl.when((i == nq - 1) & (pl.program_id(1) == N_KV_HEADS - 1))(wait)

    q_spec = pl.BlockSpec((None, GROUP, tq, LANES), lambda i, h: (h, 0, i, 0))
    kv_spec = pl.BlockSpec((None, T, LANES), lambda i, h: (h, 0, 0))
    sems = [pltpu.SemaphoreType.DMA((3 * n_g,))] * 2 if n_g else []
    return pl.pallas_call(
        body, grid=(nq, N_KV_HEADS), name="flash_fwd_gather" if n_g else "flash_fwd",
        in_specs=[q_spec, kv_spec, kv_spec] + [ANY] * n_g,
        out_specs=[pl.BlockSpec((tq, GROUP * HEAD_DIM), lambda i, h: (i, h)), q_spec] + [ANY] * n_g,
        out_shape=[jax.ShapeDtypeStruct((T, N_Q_HEADS * HEAD_DIM), BF16), jax.ShapeDtypeStruct(q.shape, BF16)]
        + [jax.ShapeDtypeStruct(b.shape, b.dtype) for b in gather],
        input_output_aliases={3 + a: 2 + a for a in range(n_g)}, scratch_shapes=sems,
        compiler_params=_cp("arbitrary", "arbitrary"))(q, k, v, *gather)


def _flash_bwd(qa, doa, k, v, n_lat, scatter=()):
    _, _, T, _ = qa.shape
    tq = tk = 256
    nkv = T // tk
    M = GROUP * tq

    n_s = len(scatter)

    def body(qa_hbm, doa_hbm, k_ref, v_ref, *rest):
        dq_hbm, dk_ref, dv_ref = rest[n_s:n_s + 3]
        q_sc, do_sc, dq_sc, sems = rest[2 * n_s + 3:2 * n_s + 7]
        h = pl.program_id(0)
        j = pl.program_id(1)
        if n_s:
            start, wait = _plane_exchange(rest[:n_s], rest[n_s + 3:2 * n_s + 3], rest[-2], rest[-1], True)
            pl.when((h == 0) & (j == 0))(start)

        @pl.when(j == 0)
        def _():
            c1 = pltpu.make_async_copy(qa_hbm.at[h], q_sc, sems.at[0])
            c2 = pltpu.make_async_copy(doa_hbm.at[h], do_sc, sems.at[1])
            c1.start()
            c2.start()
            dq_sc[...] = jnp.zeros_like(dq_sc)
            c1.wait()
            c2.wait()

        kb = k_ref[...]
        vb = v_ref[...]
        kbt = kb.astype(F32).T.astype(BF16)

        def step(i, carry):
            dk, dv = carry
            r0 = i * tq if isinstance(i, int) else pl.multiple_of(i * tq, tq)
            qv = q_sc[:, pl.ds(r0, tq), :].reshape(M, LANES)
            dov = do_sc[:, pl.ds(r0, tq), :].reshape(M, LANES)
            pr = jnp.exp(_dg(kb, qv, NT))
            ds = (pr * _dg(vb, dov, NT)).astype(BF16)
            dv = dv + _dot(pr.astype(BF16), dov)
            dk = dk + _dot(ds, qv)
            dq_sc[i] += _dot(kbt, ds)
            return dk, dv

        z = jnp.zeros((tk, LANES), F32)
        carry = _loop_unrolled(n_lat // tq, step, (z, z), UNROLL_BWD)
        dk_ref[...] = carry[0]
        dv_ref[...] = carry[1]

        @pl.when(j >= n_lat // tk)
        def _():
            c = (dk_ref[...], dv_ref[...])
            for i in range(n_lat // tq, T // tq):
                c = step(i, c)
            dk_ref[...] = c[0]
            dv_ref[...] = c[1]

        @pl.when(j == nkv - 1)
        def _():
            c3 = pltpu.make_async_copy(dq_sc, dq_hbm.at[h], sems.at[2])
            c3.start()
            c3.wait()

        if n_s:
            pl.when((h == N_KV_HEADS - 1) & (j == nkv - 1))(wait)

    kv_spec = pl.BlockSpec((None, tk, LANES), lambda h, j: (h, j, 0))
    return pl.pallas_call(
        body, grid=(N_KV_HEADS, nkv), name="flash_bwd_scatter" if n_s else "flash_bwd",
        in_specs=[ANY, ANY, kv_spec, kv_spec] + [ANY] * n_s, out_specs=[ANY, kv_spec, kv_spec] + [ANY] * n_s,
        out_shape=[jax.ShapeDtypeStruct((N_KV_HEADS, T // tq, LANES, M), F32), jax.ShapeDtypeStruct(k.shape, F32),
                   jax.ShapeDtypeStruct(k.shape, F32)] + [jax.ShapeDtypeStruct(g.shape, g.dtype) for g in scatter],
        scratch_shapes=[pltpu.VMEM((GROUP, T, LANES), BF16), pltpu.VMEM((GROUP, T, LANES), BF16),
                        pltpu.VMEM((T // tq, LANES, M), F32), pltpu.SemaphoreType.DMA((3,))]
        + ([pltpu.SemaphoreType.DMA((3 * n_s,))] * 2 if n_s else []),
        compiler_params=_cp("arbitrary", "arbitrary"))(qa, doa, k, v, *scatter)


def _conv_masks(i, tm, n_lat, T):
    row = lax.broadcasted_iota(jnp.int32, (tm, 1), 0)
    g = row + i * tm
    return row, (g == 0) | (g == n_lat), (g == n_lat - 1) | (g == T - 1)


def _shift_rows(v, prev_row, next_row, row, first, last):
    tm = v.shape[0]
    down = jnp.where(row == 0, prev_row, pltpu.roll(v, 1, 0))
    up = jnp.where(row == tm - 1, next_row, pltpu.roll(v, tm - 1, 0))
    return jnp.where(first, 0.0, down), jnp.where(last, 0.0, up)


def _halo_specs(tm, T, width, col):
    nb = T // 8
    return (pl.BlockSpec((8, width), lambda i: (jnp.maximum(i * (tm // 8) - 1, 0), col)),
            pl.BlockSpec((8, width), lambda i: (jnp.minimum((i + 1) * (tm // 8), nb - 1), col)))


def _conv_fwd(p, cw, n_lat):
    T = p.shape[0]
    tm = _tile(T, 1056)

    def body(p_ref, pp_ref, pn_ref, cw_ref, o_ref):
        row, first, last = _conv_masks(pl.program_id(0), tm, n_lat, T)
        z = p_ref[:, 256:512] * p_ref[:, 512:768]
        zp = pp_ref[7:8, 256:512] * pp_ref[7:8, 512:768]
        zn = pn_ref[0:1, 256:512] * pn_ref[0:1, 512:768]
        zd, zu = _shift_rows(z, zp, zn, row, first, last)
        conv = cw_ref[0:1, :] * zd + cw_ref[1:2, :] * z + cw_ref[2:3, :] * zu
        o_ref[...] = (p_ref[:, 0:256] * conv).astype(BF16)

    prev, nxt = _halo_specs(tm, T, 768, 0)
    return pl.pallas_call(
        body, grid=(T // tm,), name="conv_fwd",
        in_specs=[pl.BlockSpec((tm, 768), lambda i: (i, 0)), prev, nxt, _full(cw.shape)],
        out_specs=pl.BlockSpec((tm, CONV_W), lambda i: (i, 0)),
        out_shape=jax.ShapeDtypeStruct((T, CONV_W), BF16), compiler_params=_cp("parallel"))(p, p, p, cw)


def _conv_bwd(dp, dy, p, cw, n_lat):
    T = p.shape[0]
    tm = _tile(T, 1056)

    def body(dp_in, dy_ref, dyp_ref, dyn_ref, p_ref, pp_ref, pn_ref, cw_ref, dp_ref, dcw_ref):
        del dp_in
        i = pl.program_id(0)

        @pl.when(i == 0)
        def _():
            dcw_ref[...] = jnp.zeros_like(dcw_ref)

        row, first, last = _conv_masks(i, tm, n_lat, T)
        ab, ac, ax = p_ref[:, 0:256], p_ref[:, 256:512], p_ref[:, 512:768]
        z = ac * ax
        zp = pp_ref[7:8, 256:512] * pp_ref[7:8, 512:768]
        zn = pn_ref[0:1, 256:512] * pn_ref[0:1, 512:768]
        zd, zu = _shift_rows(z, zp, zn, row, first, last)
        w0, w1, w2 = cw_ref[0:1, :], cw_ref[1:2, :], cw_ref[2:3, :]
        dy = dy_ref[...]
        dc = dy * ab
        dcd, dcu = _shift_rows(dc, dyp_ref[7:8, :] * pp_ref[7:8, 0:256], dyn_ref[0:1, :] * pn_ref[0:1, 0:256],
                               row, first, last)
        dz = w0 * dcu + w1 * dc + w2 * dcd
        dp_ref[:, 0:256] = (dy * (w0 * zd + w1 * z + w2 * zu)).astype(BF16)
        dp_ref[:, 256:512] = (dz * ax).astype(BF16)
        dp_ref[:, 512:768] = (dz * ac).astype(BF16)
        dcw_ref[0:1, :] += jnp.sum(dc * zd, axis=0, keepdims=True)
        dcw_ref[1:2, :] += jnp.sum(dc * z, axis=0, keepdims=True)
        dcw_ref[2:3, :] += jnp.sum(dc * zu, axis=0, keepdims=True)

    prev, nxt = _halo_specs(tm, T, 768, 0)
    dprev, dnxt = _halo_specs(tm, T, CONV_W, 0)
    sec = pl.BlockSpec((tm, 768), lambda i: (i, 0))
    return pl.pallas_call(
        body, grid=(T // tm,), name="conv_bwd",
        in_specs=[ANY, pl.BlockSpec((tm, CONV_W), lambda i: (i, 0)), dprev, dnxt, sec, prev, nxt, _full(cw.shape)],
        out_specs=[sec, _full((8, CONV_W))],
        out_shape=[jax.ShapeDtypeStruct(dp.shape, BF16), jax.ShapeDtypeStruct((8, CONV_W), F32)],
        input_output_aliases={0: 0}, compiler_params=_cp("arbitrary"))(dp, dy, dy, dy, p, p, p, cw)


def _gmlp_mix(bd_ref, vs, grp):
    out = jnp.zeros((2 * CHUNK, SG_W), F32)
    for g in range(4):
        out = jnp.where(grp == g, _dot(bd_ref[g], vs), out)
    return out


def _gmlp_fwd(p, sgn, bd, bias):
    T = p.shape[0]
    tm = _tile(T, 768, 2 * CHUNK)

    def body(p_ref, sgn_ref, bd_ref, bias_ref, o_ref):
        x = _gelu(p_ref[:, 256:512])
        vn = (x * lax.rsqrt(jnp.mean(x * x, axis=-1, keepdims=True) + EPS) * sgn_ref[...]).astype(BF16)
        grp = _lane((2 * CHUNK, SG_W)) // 64
        for s in range(tm // (2 * CHUNK)):
            rs = slice(s * 2 * CHUNK, (s + 1) * 2 * CHUNK)
            mixed = _gmlp_mix(bd_ref, vn[rs], grp) + bias_ref[...]
            o_ref[rs, :] = (_gelu(p_ref[rs, 0:256]) * mixed).astype(BF16)

    return pl.pallas_call(
        body, grid=(T // tm,), name="gmlp_fwd",
        in_specs=[pl.BlockSpec((tm, 2 * SG_W), lambda i: (i, 3)), _full(sgn.shape), _full(bd.shape),
                  _full(bias.shape)],
        out_specs=pl.BlockSpec((tm, SG_W), lambda i: (i, 0)),
        out_shape=jax.ShapeDtypeStruct((T, SG_W), BF16), compiler_params=_cp("parallel"))(p, sgn, bd, bias)


def _gmlp_bwd(dp, dy, p, sgn, bd, bdt, bias, gsum):
    T = p.shape[0]
    tm = _tile(T, 768, 2 * CHUNK)
    nt = T // tm
    C2 = 2 * CHUNK

    def body(dp_in, dy_ref, p_ref, sgn_ref, bd_ref, bdt_ref, bias_ref, gsum_ref,
             dp_ref, dsg_ref, dws_ref, dbs_ref, acc_w, acc_b):
        del dp_in
        i = pl.program_id(0)

        @pl.when(i == 0)
        def _():
            dsg_ref[...] = jnp.zeros_like(dsg_ref)
            acc_w[...] = jnp.zeros_like(acc_w)
            acc_b[...] = jnp.zeros_like(acc_b)

        u = p_ref[:, 0:256]
        sv = p_ref[:, 256:512]
        ug = _gelu(u)
        x = _gelu(sv)
        r = lax.rsqrt(jnp.mean(x * x, axis=-1, keepdims=True) + EPS)
        xh = x * r
        sg = sgn_ref[...]
        vn = (xh * sg).astype(BF16)
        grp = _lane((C2, SG_W)) // 64
        dug, dvn = [], []
        for s in range(tm // C2):
            rs = slice(s * C2, (s + 1) * C2)
            vs = vn[rs]
            dys = dy_ref[rs, :]
            dug.append(dys * (_gmlp_mix(bd_ref, vs, grp) + bias_ref[...]))
            dmix = dys * ug[rs]
            acc_b[...] += dmix
            dmb = dmix.astype(BF16)
            dvn.append(_gmlp_mix(bdt_ref, dmb, grp))
            for g in range(4):
                acc_w[g] += _dg(jnp.where(grp == g, dmb, jnp.zeros_like(dmb)), vs, NT)
        dug = jnp.concatenate(dug, axis=0)
        dvn = jnp.concatenate(dvn, axis=0)
        dsg_ref[...] += jnp.sum(dvn * xh, axis=0, keepdims=True)
        gd = sg * dvn
        dx = r * (gd - xh * jnp.mean(xh * gd, axis=-1, keepdims=True))
        dp_ref[:, 0:256] = (dug * _gelu_grad(u)).astype(BF16)
        dp_ref[:, 256:512] = (dx * _gelu_grad(sv)).astype(BF16)

        @pl.when(i == nt - 1)
        def _():
            for g in range(4):
                dws_ref[g] = acc_w[g, 0:CHUNK, 0:CHUNK] + acc_w[g, CHUNK:C2, CHUNK:C2]
            dbs_ref[...] = jnp.dot(acc_b[0:CHUNK, :] + acc_b[CHUNK:C2, :], gsum_ref[...],
                                   preferred_element_type=F32, precision=lax.Precision.HIGHEST)

    sec = pl.BlockSpec((tm, 2 * SG_W), lambda i: (i, 3))
    return pl.pallas_call(
        body, grid=(nt,), name="gmlp_bwd",
        in_specs=[ANY, pl.BlockSpec((tm, SG_W), lambda i: (i, 0)), sec, _full(sgn.shape), _full(bd.shape),
                  _full(bdt.shape), _full(bias.shape), _full(gsum.shape)],
        out_specs=[sec, _full((1, SG_W)), _full((4, CHUNK, CHUNK)), _full((CHUNK, LANES))],
        out_shape=[jax.ShapeDtypeStruct(dp.shape, BF16), jax.ShapeDtypeStruct((1, SG_W), F32),
                   jax.ShapeDtypeStruct((4, CHUNK, CHUNK), F32), jax.ShapeDtypeStruct((CHUNK, LANES), F32)],
        scratch_shapes=[pltpu.VMEM((4, C2, C2), F32), pltpu.VMEM((C2, SG_W), F32)],
        input_output_aliases={0: 0}, compiler_params=_cp("arbitrary"))(dp, dy, p, sgn, bd, bdt, bias, gsum)


def _merge_fwd(ya, at, yc, p, wa, wb, wc):
    T = p.shape[0]
    tm = _tile(T, 528)
    n = wa.shape[2]

    def body(ya_ref, at_ref, yc_ref, ga_ref, gb_ref, gc_ref, wa_ref, wb_ref, wc_ref, o_ref):
        yav, atv, ycv = ya_ref[...], at_ref[...], yc_ref[...]
        for j in range(N_CHIPS):
            cs = slice(j * n, (j + 1) * n)
            m = (_sigmoid(ga_ref[:, cs]) * _dot(yav, wa_ref[j]) + _sigmoid(gb_ref[:, cs]) * _dot(atv, wb_ref[j])
                 + _sigmoid(gc_ref[:, cs]) * _dot(ycv, wc_ref[j]))
            o_ref[:, cs] = m.astype(BF16)

    def rows(w, col=0):
        return pl.BlockSpec((tm, w), lambda i: (i, col))

    return pl.pallas_call(
        body, grid=(T // tm,), name="merge_fwd",
        in_specs=[rows(CONV_W), rows(512), rows(SG_W), rows(D_MODEL, 2), rows(D_MODEL, 3), rows(D_MODEL, 4),
                  _full(wa.shape), _full(wb.shape), _full(wc.shape)],
        out_specs=rows(D_MODEL), out_shape=jax.ShapeDtypeStruct((T, D_MODEL), BF16),
        compiler_params=_cp("parallel"))(ya, at, yc, p, p, p, wa, wb, wc)


def _merge_bwd(dyo, ya, at, yc, p, wa, wb, wc, wo):
    T = p.shape[0]
    tm = _tile(T, 528)
    n = wa.shape[2]

    def body(dyo_ref, ya_ref, at_ref, yc_ref, ga_ref, gb_ref, gc_ref, wa_ref, wb_ref, wc_ref, wo_ref,
             dp_ref, dya_ref, doa_ref, dyc_ref, dwa_ref, dwb_ref, dwc_ref):
        i = pl.program_id(0)

        @pl.when(i == 0)
        def _():
            dwa_ref[...] = jnp.zeros_like(dwa_ref)
            dwb_ref[...] = jnp.zeros_like(dwb_ref)
            dwc_ref[...] = jnp.zeros_like(dwc_ref)

        dp_ref[:, 0:OFF_G] = jnp.zeros((tm, OFF_G), BF16)
        dm = _dg(dyo_ref[...], wo_ref[...], NT)
        yav, atv, ycv = ya_ref[...], at_ref[...], yc_ref[...]
        dya = jnp.zeros((tm, CONV_W), F32)
        dat = jnp.zeros((tm, 512), F32)
        dyc = jnp.zeros((tm, SG_W), F32)
        for j in range(N_CHIPS):
            cs = slice(j * n, (j + 1) * n)
            dmj = dm[:, cs]
            for y_in, w_ref, g_ref, dw_ref, which in (
                    (yav, wa_ref, ga_ref, dwa_ref, 0), (atv, wb_ref, gb_ref, dwb_ref, 1),
                    (ycv, wc_ref, gc_ref, dwc_ref, 2)):
                sg = _sigmoid(g_ref[:, cs])
                y = _dot(y_in, w_ref[j])
                c0 = OFF_G + which * D_MODEL + j * n
                dp_ref[:, c0:c0 + n] = (dmj * y * sg * (1.0 - sg)).astype(BF16)
                dyb = (dmj * sg).astype(BF16)
                dw_ref[j] += _dg(y_in, dyb, TN)
                back = _dg(dyb, w_ref[j], NT)
                if which == 0:
                    dya = dya + back
                elif which == 1:
                    dat = dat + back
                else:
                    dyc = dyc + back
        dya_ref[...] = dya
        dyc_ref[...] = dyc
        prod = dat * atv.astype(F32)
        lane = _lane((tm, LANES))
        dat_rows = _heads_to_rows(dat, N_Q_HEADS)
        for h in range(N_Q_HEADS):
            grp = prod[:, (h // 2) * LANES:(h // 2 + 1) * LANES]
            keep = (lane < 64) if h % 2 == 0 else (lane >= 64)
            delta = jnp.sum(jnp.where(keep, grp, 0.0), axis=1, keepdims=True)
            doa_ref[h] = _aug(dat_rows[h], delta).astype(BF16)

    def rows(w, col=0):
        return pl.BlockSpec((tm, w), lambda i: (i, col))

    return pl.pallas_call(
        body, grid=(T // tm,), name="merge_bwd",
        in_specs=[rows(D_MODEL), rows(CONV_W), rows(512), rows(SG_W), rows(D_MODEL, 2), rows(D_MODEL, 3),
                  rows(D_MODEL, 4), _full(wa.shape), _full(wb.shape), _full(wc.shape), _full(wo.shape)],
        out_specs=[rows(IN_W), rows(CONV_W),
                   pl.BlockSpec((N_Q_HEADS, tm, LANES), lambda i: (0, i, 0)), rows(SG_W),
                   _full(wa.shape), _full(wb.shape), _full(wc.shape)],
        out_shape=[jax.ShapeDtypeStruct((T, IN_W), BF16)] + [
            jax.ShapeDtypeStruct((T, CONV_W), F32), jax.ShapeDtypeStruct((N_Q_HEADS, T, LANES), BF16),
            jax.ShapeDtypeStruct((T, SG_W), F32), jax.ShapeDtypeStruct(wa.shape, F32),
            jax.ShapeDtypeStruct(wb.shape, F32), jax.ShapeDtypeStruct(wc.shape, F32)],
        compiler_params=_cp("arbitrary"))(dyo, ya, at, yc, p, p, p, wa, wb, wc, wo)


def _loss_grad(xf, tgt, n_lat):
    T, D = xf.shape
    tm = _tile(np.gcd(n_lat, T), 512)
    nl = n_lat // tm

    def body(x_ref, t_ref, dy_ref, l_ref):
        i = pl.program_id(0)

        @pl.when(i == 0)
        def _():
            l_ref[...] = jnp.zeros_like(l_ref)

        @pl.when(i < nl)
        def _():
            err = x_ref[...] - t_ref[...]
            dy_ref[...] = err * (1.0 / D)
            sq = jnp.sum(jnp.sum(err * err, axis=1, keepdims=True), axis=0, keepdims=True)
            l_ref[...] += (0.5 / D) * sq

        @pl.when(i >= nl)
        def _():
            dy_ref[...] = jnp.zeros_like(dy_ref)

    return pl.pallas_call(
        body, grid=(T // tm,), name="loss_grad",
        in_specs=[pl.BlockSpec((tm, D), lambda i: (i, 0)), pl.BlockSpec((tm, D), lambda i: (jnp.minimum(i, nl - 1), 0))],
        out_specs=[pl.BlockSpec((tm, D), lambda i: (i, 0)), _full((8, LANES))],
        out_shape=[jax.ShapeDtypeStruct((T, D), F32), jax.ShapeDtypeStruct((8, LANES), F32)],
        compiler_params=_cp("arbitrary"))(xf, tgt)


def _row_tile(R, C):
    if R * C <= (1 << 19) or R % 8:
        return R
    return _tile(R, max(8, (1 << 19) // C), 8)


def _adamw(w, m, v, g1, g2=None):
    shape = w.shape
    C = shape[-1]
    R = int(np.prod(shape[:-1])) if len(shape) > 1 else 1
    tr = _row_tile(R, C)
    ins = [a.reshape(R, C) for a in ((w, m, v, g1) if g2 is None else (w, m, v, g1, g2))]

    def body(*refs):
        w_ref, m_ref, v_ref = refs[0], refs[1], refs[2]
        g_ref, d_ref, m2_ref, v2_ref = refs[-4:]
        g = refs[3][...] if g2 is None else refs[3][...] + refs[4][...]
        m2 = ADAM_B1 * m_ref[...] + (1.0 - ADAM_B1) * g
        v2 = ADAM_B2 * v_ref[...] + (1.0 - ADAM_B2) * (g * g)
        m_hat = m2 / (1.0 - ADAM_B1 ** ADAM_STEP)
        v_hat = v2 / (1.0 - ADAM_B2 ** ADAM_STEP)
        g_ref[...] = g
        d_ref[...] = -ADAM_LR * (m_hat / (jnp.sqrt(v_hat) + ADAM_EPS) + ADAM_WD * w_ref[...])
        m2_ref[...] = m2
        v2_ref[...] = v2

    spec = pl.BlockSpec((tr, C), lambda i: (i, 0))
    outs = pl.pallas_call(
        body, grid=(R // tr,), name="adamw", in_specs=[spec] * len(ins), out_specs=[spec] * 4,
        out_shape=[jax.ShapeDtypeStruct((R, C), F32)] * 4, compiler_params=_cp("parallel"))(*ins)
    return [o.reshape(shape) for o in outs]


def _sum_lead(x, name):
    n, R, C = x.shape
    tr = _row_tile(R, C * n)

    def body(x_ref, o_ref):
        acc = x_ref[0].astype(F32)
        for s in range(1, n):
            acc = acc + x_ref[s].astype(F32)
        o_ref[...] = acc

    return pl.pallas_call(
        body, grid=(R // tr,), name=name, in_specs=[pl.BlockSpec((n, tr, C), lambda i: (0, i, 0))],
        out_specs=pl.BlockSpec((tr, C), lambda i: (i, 0)), out_shape=jax.ShapeDtypeStruct((R, C), F32),
        compiler_params=_cp("parallel"))(x)


def _silu(x):
    return x * _sigmoid(x)


def _mod_fwd(a_raw, w_mod, bsh):
    L, D, n = w_mod.shape

    def body(a_ref, w_ref, b_ref, o_ref):
        o_ref[...] = _dot(_silu(a_ref[...]).astype(BF16), w_ref[...].astype(BF16)) + b_ref[...]

    return pl.pallas_call(
        body, grid=(L,), name="mod_fwd",
        in_specs=[_full(a_raw.shape), pl.BlockSpec((None, D, n), lambda l: (l, 0, 0)),
                  pl.BlockSpec((None, 1, n), lambda l: (l, 0, 0))],
        out_specs=pl.BlockSpec((None, 16, n), lambda l: (l, 0, 0)),
        out_shape=jax.ShapeDtypeStruct((L, 16, n), F32), compiler_params=_cp("parallel"))(a_raw, w_mod, bsh)


def _wmod_grad(a_raw, dms):
    L, _, n = dms.shape
    D = a_raw.shape[1]

    def body(a_ref, dm_ref, o_ref):
        o_ref[...] = _dg(_silu(a_ref[...]).astype(BF16), dm_ref[...].astype(BF16), TN)

    return pl.pallas_call(
        body, grid=(L,), name="wmod_grad",
        in_specs=[_full(a_raw.shape), pl.BlockSpec((None, 16, n), lambda l: (l, 0, 0))],
        out_specs=pl.BlockSpec((None, D, n), lambda l: (l, 0, 0)),
        out_shape=jax.ShapeDtypeStruct((L, D, n), F32), compiler_params=_cp("parallel"))(a_raw, dms)


def _cctx_partial(dmc, w_mod):
    L, D, n = w_mod.shape

    def body(dm_ref, w_ref, o_ref):
        part = _dg(dm_ref[...].astype(BF16), w_ref[...].astype(BF16), NT)

        @pl.when(pl.program_id(0) == 0)
        def _():
            o_ref[...] = part

        @pl.when(pl.program_id(0) > 0)
        def _():
            o_ref[...] += part

    return pl.pallas_call(
        body, grid=(L,), name="cctx_partial",
        in_specs=[pl.BlockSpec((None, 16, n), lambda l: (l, 0, 0)), pl.BlockSpec((None, D, n), lambda l: (l, 0, 0))],
        out_specs=_full((16, D)), out_shape=jax.ShapeDtypeStruct((16, D), F32),
        compiler_params=_cp("arbitrary"))(dmc, w_mod)


def _cctx_final(parts, cc):
    def body(p_ref, c_ref, o_ref):
        s = p_ref[0, 0:8, :]
        for j in range(1, N_CHIPS):
            s = s + p_ref[2 * j, 0:8, :]
        xv = c_ref[...]
        sg = _sigmoid(xv)
        o_ref[...] = s * (sg * (1.0 + xv * (1.0 - sg)))

    return pl.pallas_call(
        body, name="cctx_final", in_specs=[_full(parts.shape), _full(cc.shape)], out_specs=_full((8, LANES)),
        out_shape=jax.ShapeDtypeStruct((8, LANES), F32), compiler_params=_cp())(parts, cc)


def _me():
    return lax.axis_index("x"), lax.axis_index("y"), lax.axis_index("c")


def _flip(v, bit):
    return 1 - v if bit else v


def _remote(src, dst, ssem, rsem, peer):
    return pltpu.make_async_remote_copy(src_ref=src, dst_ref=dst, send_sem=ssem, recv_sem=rsem,
                                        device_id=peer, device_id_type=MESH_ID)


def _ag8(xb, name):
    R = xb.shape[0]

    def body(x_ref, o_ref, ssem, rsem, lsem):
        mx, my, mc = _me()
        me = 4 * mx + 2 * my + mc
        loc = pltpu.make_async_copy(x_ref, o_ref.at[me], lsem.at[0])
        loc.start()
        sends = []
        for k in range(1, N_DEV):
            peer = (_flip(mx, k & 4), _flip(my, k & 2), _flip(mc, k & 1))
            cp = _remote(x_ref, o_ref.at[me], ssem.at[k - 1], rsem.at[k - 1], peer)
            cp.start()
            sends.append((cp, peer))
        for k, (cp, peer) in enumerate(sends):
            pid = 4 * peer[0] + 2 * peer[1] + peer[2]
            _remote(x_ref, o_ref.at[pid], ssem.at[k], rsem.at[k], peer).wait_recv()
        for cp, _ in sends:
            cp.wait_send()
        loc.wait()

    return pl.pallas_call(
        body, name=name, in_specs=[ANY], out_specs=ANY, out_shape=jax.ShapeDtypeStruct((N_DEV, R, LANES), F32),
        scratch_shapes=[pltpu.SemaphoreType.DMA((N_DEV - 1,)), pltpu.SemaphoreType.DMA((N_DEV - 1,)),
                        pltpu.SemaphoreType.DMA((1,))])(xb)


def _plane_peers(mx, my, mc):
    out = []
    for k in range(1, N_CHIPS):
        px, py = _flip(mx, k & 2), _flip(my, k & 1)
        out.append(((px, py, mc), 2 * px + py))
    return out


def _plane_exchange(ins, outs, ssem, rsem, scatter):
    n = len(ins)

    def desc(k, a, arriving):
        mx, my, mc = _me()
        j = 2 * mx + my
        peer, pj = _plane_peers(mx, my, mc)[k]
        src = ins[a].at[pj if scatter else j]
        dst = outs[a].at[pj if arriving else j]
        return _remote(src, dst, ssem.at[k * n + a], rsem.at[k * n + a], peer)

    def start():
        for k in range(N_CHIPS - 1):
            for a in range(n):
                desc(k, a, False).start()

    def wait():
        for k in range(N_CHIPS - 1):
            for a in range(n):
                desc(k, a, True).wait_recv()
        for k in range(N_CHIPS - 1):
            for a in range(n):
                desc(k, a, False).wait_send()

    return start, wait


def _chip_gather(bufs, name):
    n = len(bufs)
    halves = [b.shape[1] // 2 for b in bufs]

    def body(*refs):
        outs = refs[n:2 * n]
        ssem, rsem, fsem, gsem = refs[2 * n:]
        mx, my, mc = _me()
        j = 2 * mx + my
        sib = (mx, my, 1 - mc)

        def half(a, blk, c):
            return outs[a].at[blk, pl.ds(c * halves[a], halves[a]), :]

        peers = _plane_peers(mx, my, mc)
        sends = []
        for k, (peer, _) in enumerate(peers):
            for a in range(n):
                mine = half(a, j, mc)
                cp = _remote(mine, mine, ssem.at[k * n + a], rsem.at[k * n + a], peer)
                cp.start()
                sends.append(cp)
        for k, (peer, pj) in enumerate(peers):
            for a in range(n):
                got = half(a, pj, mc)
                _remote(got, got, ssem.at[k * n + a], rsem.at[k * n + a], peer).wait_recv()
                fw = _remote(got, got, fsem.at[k * n + a], gsem.at[k * n + a], sib)
                fw.start()
                sends.append(fw)
        for k, (_, pj) in enumerate(peers):
            for a in range(n):
                theirs = half(a, pj, 1 - mc)
                _remote(theirs, theirs, fsem.at[k * n + a], gsem.at[k * n + a], sib).wait_recv()
        for cp in sends:
            cp.wait_send()

    sems = pltpu.SemaphoreType.DMA((3 * n,))
    return pl.pallas_call(
        body, name=name, in_specs=[ANY] * n, out_specs=[ANY] * n,
        out_shape=[jax.ShapeDtypeStruct(b.shape, b.dtype) for b in bufs],
        input_output_aliases={a: a for a in range(n)},
        scratch_shapes=[sems, sems, sems, sems])(*bufs)


def _chip_scatter(gs, name):
    n = len(gs)

    def body(*refs):
        ins, outs = refs[:n], refs[n:2 * n]
        ssem, rsem = refs[2 * n:]
        mx, my, mc = _me()
        j = 2 * mx + my
        peers = _plane_peers(mx, my, mc)
        sends = []
        for k, (peer, pj) in enumerate(peers):
            for a in range(n):
                cp = _remote(ins[a].at[pj], outs[a].at[j], ssem.at[k * n + a], rsem.at[k * n + a], peer)
                cp.start()
                sends.append(cp)
        for k, (peer, pj) in enumerate(peers):
            for a in range(n):
                _remote(ins[a].at[pj], outs[a].at[pj], ssem.at[k * n + a], rsem.at[k * n + a], peer).wait_recv()
        for cp in sends:
            cp.wait_send()

    return pl.pallas_call(
        body, name=name, in_specs=[ANY] * n, out_specs=[ANY] * n,
        out_shape=[jax.ShapeDtypeStruct(g.shape, g.dtype) for g in gs],
        scratch_shapes=[pltpu.SemaphoreType.DMA((3 * n,)), pltpu.SemaphoreType.DMA((3 * n,))])(*gs)


def _sibling_swap(xs, name):
    n = len(xs)

    def body(*refs):
        ins, outs = refs[:n], refs[n:2 * n]
        ssem, rsem = refs[2 * n:]
        mx, my, mc = _me()
        cps = [_remote(ins[a], outs[a], ssem.at[a], rsem.at[a], (mx, my, 1 - mc)) for a in range(n)]
        for cp in cps:
            cp.start()
        for cp in cps:
            cp.wait()

    return pl.pallas_call(
        body, name=name, in_specs=[ANY] * n, out_specs=[ANY] * n,
        out_shape=[jax.ShapeDtypeStruct(x.shape, x.dtype) for x in xs],
        scratch_shapes=[pltpu.SemaphoreType.DMA((n,)), pltpu.SemaphoreType.DMA((n,))])(*xs)


def _sibling_halves(gs, name):
    n = len(gs)

    def body(*refs):
        ins, outs = refs[:n], refs[n:2 * n]
        ssem, rsem = refs[2 * n:]
        mx, my, mc = _me()
        cps = []
        for a in range(n):
            h = gs[a].shape[1] // 2
            cps.append(_remote(ins[a].at[:, pl.ds((1 - mc) * h, h), :], outs[a], ssem.at[a], rsem.at[a],
                               (mx, my, 1 - mc)))
        for cp in cps:
            cp.start()
        for cp in cps:
            cp.wait()

    return pl.pallas_call(
        body, name=name, in_specs=[ANY] * n, out_specs=[ANY] * n,
        out_shape=[jax.ShapeDtypeStruct((g.shape[0], g.shape[1] // 2, g.shape[2]), g.dtype) for g in gs],
        scratch_shapes=[pltpu.SemaphoreType.DMA((n,)), pltpu.SemaphoreType.DMA((n,))])(*gs)


def _sibling_fill(hs, name):
    n = len(hs)

    def body(*refs):
        ins, outs = refs[:n], refs[n:2 * n]
        ssem, rsem = refs[2 * n:]
        mx, my, mc = _me()
        cps = []
        for a in range(n):
            h = hs[a].shape[0]
            cps.append(_remote(ins[a], outs[a].at[pl.ds(mc * h, h), :], ssem.at[a], rsem.at[a], (mx, my, 1 - mc)))
        for cp in cps:
            cp.start()
        for a, cp in enumerate(cps):
            h = hs[a].shape[0]
            theirs = outs[a].at[pl.ds((1 - mc) * h, h), :]
            _remote(ins[a], theirs, ssem.at[a], rsem.at[a], (mx, my, 1 - mc)).wait_recv()
            cp.wait_send()

    return pl.pallas_call(
        body, name=name, in_specs=[ANY] * n, out_specs=[ANY] * n,
        out_shape=[jax.ShapeDtypeStruct((2 * x.shape[0], x.shape[1]), x.dtype) for x in hs],
        scratch_shapes=[pltpu.SemaphoreType.DMA((n,)), pltpu.SemaphoreType.DMA((n,))])(*hs)


def _add_cast(g, sb):
    J, h, b = g.shape
    th = _row_tile(h, b * J)

    def body(g_ref, s_ref, o_ref):
        o_ref[...] = (g_ref[...].astype(F32) + s_ref[...].astype(F32)).astype(BF16)

    spec = pl.BlockSpec((J, th, b), lambda i: (0, i, 0))
    return pl.pallas_call(
        body, grid=(h // th,), name="add_planes", in_specs=[spec, spec], out_specs=spec,
        out_shape=jax.ShapeDtypeStruct((J, h, b), BF16), compiler_params=_cp("parallel"))(g, sb)


_WEIGHTS = ("c_ctx", "w_mod", "b_mod", "norm1", "w_in", "q_gain", "k_gain", "conv_w", "sg_norm", "w_s", "b_s",
            "w_a", "w_b", "w_c", "w_o", "norm2", "w_ff1", "w_ff3", "w_ff2")
_BIG = ("w_in", "w_a", "w_b", "w_c", "w_o", "w_ff1", "w_ff3", "w_ff2")


def _constants():
    idx = np.arange(LANES)
    e = (idx[:, None] // 64 == idx[None, :] // 64).astype(np.float32) / 64.0
    c512 = np.arange(512)
    fold = (c512[:, None] % 64 == idx[None, :]).astype(np.float32)
    c256 = np.arange(SG_W)
    gsum = (c256[:, None] // 64 == idx[None, :]).astype(np.float32)
    return jnp.asarray(e, BF16), jnp.asarray(fold, F32), jnp.asarray(gsum, F32)


def _rope_tables(n_lat, n_ctx):
    t = jnp.arange(n_lat)
    inv = ROPE_THETA ** (-jnp.arange(0, HEAD_DIM // 2, 2, dtype=F32) / (HEAD_DIM // 2))
    ar = (t // GRID_W).astype(F32)[:, None] * inv
    ac = (t % GRID_W).astype(F32)[:, None] * inv
    cos = jnp.concatenate([jnp.cos(ar), jnp.cos(ar), jnp.cos(ac), jnp.cos(ac)], axis=1)
    sin = jnp.concatenate([-jnp.sin(ar), jnp.sin(ar), -jnp.sin(ac), jnp.sin(ac)], axis=1)
    cos = jnp.concatenate([cos, jnp.ones((n_ctx, HEAD_DIM), F32)], axis=0)
    sin = jnp.concatenate([sin, jnp.zeros((n_ctx, HEAD_DIM), F32)], axis=0)
    return jnp.concatenate([cos, cos], axis=1), jnp.concatenate([sin, sin], axis=1)


def kernel(x, c, ctx, c_ctx, w_mod, b_mod, norm1, w_in, q_gain, k_gain, conv_w, sg_norm, w_s, b_s, w_a, w_b, w_c, w_o, norm2, w_ff1, w_ff3, w_ff2, loss_target, m_c_ctx, m_w_mod, m_b_mod, m_norm1, m_w_in, m_q_gain, m_k_gain, m_conv_w, m_sg_norm, m_w_s, m_b_s, m_w_a, m_w_b, m_w_c, m_w_o, m_norm2, m_w_ff1, m_w_ff3, m_w_ff2, v_c_ctx, v_w_mod, v_b_mod, v_norm1, v_w_in, v_q_gain, v_k_gain, v_conv_w, v_sg_norm, v_w_s, v_b_s, v_w_a, v_w_b, v_w_c, v_w_o, v_norm2, v_w_ff1, v_w_ff3, v_w_ff2):
    given = dict(locals())
    mx, my, mc = _me()
    chip = 2 * mx + my
    dev = 4 * mx + 2 * my + mc
    L = norm1.shape[0]
    S, Lc = x.shape[1], ctx.shape[1]
    T = S + Lc
    D = D_MODEL
    n_mod, n_in, n_ff = w_mod.shape[2], w_in.shape[2], w_ff1.shape[2]
    n_cw = conv_w.shape[2]
    e_avg, fold, gsum = _constants()
    cos_t, sin_t = _rope_tables(S, Lc)

    cw_rows = (L * 3 * n_cw) // LANES
    pad = (-(8 + cw_rows)) % 8
    buf = jnp.concatenate([c.reshape(8, LANES), conv_w.reshape(cw_rows, LANES), jnp.zeros((pad, LANES), F32)], axis=0)
    g1 = _ag8(buf, "gather_cond")
    conds = g1[:, :8].reshape(N_DEV, D)
    cw_full = jnp.stack([g1[2 * j, 8:8 + cw_rows].reshape(L, 3, n_cw) for j in range(N_CHIPS)], axis=2)
    cw_full = cw_full.reshape(L, 3, N_CHIPS * n_cw)
    cw8 = jnp.pad(cw_full, ((0, 0), (0, 5), (0, 0)))
    a_raw = jnp.concatenate([conds, c_ctx[None], jnp.zeros((7, D), F32)], axis=0)
    bsh = lax.dynamic_slice_in_dim(b_mod, chip * n_mod, n_mod, axis=1)[:, None, :]
    mod_sh = _mod_fwd(a_raw, w_mod, bsh)
    g2 = _ag8(mod_sh.reshape(-1, LANES), "gather_mod")
    mods = jnp.stack([g2[2 * j].reshape(L, 16, n_mod) for j in range(N_CHIPS)], axis=2).reshape(L, 16, N_CHIPS * n_mod)
    lat = lax.dynamic_index_in_dim(mods, dev, axis=1, keepdims=False)
    mod = jnp.stack([lat.reshape(L, 6, D), mods[:, 8].reshape(L, 6, D)], axis=1)
    mod = jnp.pad(mod, ((0, 0), (0, 0), (0, 2), (0, 0)))

    qg = jnp.tile(q_gain, (1, N_Q_HEADS))[:, None, :]
    kg = jnp.tile(k_gain, (1, N_KV_HEADS))[:, None, :]
    sgn = sg_norm[:, None, :]
    ws_b = w_s.astype(BF16)
    zero = jnp.zeros_like(ws_b)
    bd = jnp.concatenate([jnp.concatenate([ws_b, zero], axis=3), jnp.concatenate([zero, ws_b], axis=3)], axis=2)
    bdt = jnp.swapaxes(bd, 2, 3)
    bias = jnp.tile(jnp.repeat(jnp.swapaxes(b_s, 1, 2), SG_W // 4, axis=2), (1, 2, 1))

    def shard_bufs(l):
        return [lax.dynamic_update_slice(lax.empty((N_CHIPS,) + given[nm].shape[1:], BF16),
                                         given[nm][l].astype(BF16)[None], (chip, 0, 0)) for nm in _BIG]

    def unpack(bufs):
        win, wa, wb, wc, wo, w1, w3, w2 = bufs
        return win, wa, wb, wc, wo.reshape(1, D, D), w1, w3, w2

    def layer_fwd(X, l, W, nxt):
        win, wa, wb, wc, wo, w1, w3, w2 = W
        h = _norm_mod(X, norm1[l][None], mod[l], 0, 1, S)
        p = _mm_nn(h, win, F32, "in_proj")
        ya = _conv_fwd(p, cw8[l], S)
        q, k, v = _qkv_prep(p, cos_t, sin_t, qg[l], kg[l], e_avg)
        at, qa, *got = _flash_fwd(q.reshape(N_KV_HEADS, GROUP, T, LANES), k, v, S, nxt)
        yc = _gmlp_fwd(p, sgn[l], bd[l], bias[l])
        mg = _merge_fwd(ya, at, yc, p, wa, wb, wc)
        X1, f1 = _mm_res(mg[None], wo, X, mod[l], 2, S, "out_proj")
        h2 = _norm_mod(X1, norm2[l][None], mod[l], 3, 4, S)
        a1, a3, act = _ffn_up(h2, w1, w3)
        X2, f2 = _mm_res(act, w2, X1, mod[l], 5, S, "ffn_down")
        return X2, got, dict(X=X, h=h, p=p, ya=ya, k=k, v=v, at=at, qa=qa, yc=yc, mg=mg, X1=X1, f1=f1, h2=h2,
                             a1=a1, a3=a3, act=act, f2=f2)

    def layer_bwd(dX2, l, W, sv, pending):
        win, wa, wb, wc, wo, w1, w3, w2 = W
        dyf, dgt2 = _gate_bwd(dX2, sv["f2"], mod[l], 5, S)
        da1, da3 = _ffn_down_bwd(dyf, w2, sv["a1"], sv["a3"])
        dw2 = _mm_tn(sv["act"], dyf, _shard_rows(n_ff), _rows(D), N_CHIPS, n_ff, D, T, "dw_ff2")
        dh2 = _mm_nt_acc([da1, da3], [w1, w3], False, "ffn_up_bwd")
        dw1 = _mm_tn(sv["h2"], da1, _rows(D), _shard_rows(n_ff), N_CHIPS, D, n_ff, T, "dw_ff1")
        dw3 = _mm_tn(sv["h2"], da3, _rows(D), _shard_rows(n_ff), N_CHIPS, D, n_ff, T, "dw_ff3")
        dX1, dn2, dsh2, dsc2 = _norm_mod_bwd(sv["X1"], dh2, dX2, norm2[l][None], mod[l], 4, S)
        dyo, dgt1 = _gate_bwd(dX1, sv["f1"], mod[l], 2, S)
        dwo = _mm_tn(sv["mg"], dyo, _rows(D), _rows(D), 1, D, D, T, "dw_o")
        dp, dya, doa, dyc, dwa, dwb, dwc = _merge_bwd(dyo, sv["ya"], sv["at"], sv["yc"], sv["p"], wa, wb, wc, wo[0])
        dp, dcw = _conv_bwd(dp, dya, sv["p"], cw8[l], S)
        dp, dsg, dws, dbs = _gmlp_bwd(dp, dyc, sv["p"], sgn[l], bd[l], bdt[l], bias[l], gsum)
        dq, dk, dv, *recv = _flash_bwd(sv["qa"], doa.reshape(N_KV_HEADS, GROUP, T, LANES), sv["k"], sv["v"], S,
                                       pending)
        dp, dqg, dkg = _qkv_prep_bwd(dp, dq, dk, dv, sv["p"], cos_t, sin_t,
                                     qg[l], kg[l], e_avg, fold)
        dh = _mm_nt_acc([dp], [win], True, "in_proj_bwd")
        dwin = _mm_tn(sv["h"], dp, _rows(D), _row_cols(n_in), N_CHIPS, D, n_in, T, "dw_in")
        dX0, dn1, dsh1, dsc1 = _norm_mod_bwd(sv["X"], dh, dX1, norm1[l][None], mod[l], 1, S)
        dmod = jnp.concatenate([dsh1, dsc1, dgt1, dsh2, dsc2, dgt2], axis=1)
        big = [dwin, dwa.astype(BF16), dwb.astype(BF16), dwc.astype(BF16),
               dwo.reshape(N_CHIPS, D // N_CHIPS, D), dw1, dw3, dw2]
        small = dict(norm1=dn1[0], norm2=dn2[0], q_gain=dqg[0, :HEAD_DIM], k_gain=dkg[0, :HEAD_DIM],
                     conv_w=dcw[:3], sg_norm=dsg[0], w_s=dws, b_s=jnp.swapaxes(dbs[:, :4], 0, 1), dmod=dmod)
        return dX0, big, small, recv

    X = jnp.concatenate([x[0], ctx[0]], axis=0)
    Ws, saved = [unpack(_chip_gather(shard_bufs(0), "gather_weights"))], []
    for l in range(L):
        X, got, sv = layer_fwd(X, l, Ws[l], shard_bufs(l + 1) if l + 1 < L else ())
        if got:
            Ws.append(unpack(got))
        saved.append(sv)
    dX, lpart = _loss_grad(X, loss_target[0], S)
    loss = lax.psum(lpart[0, 0], ("x", "y", "c"))

    out = {nm: [None] * L for nm in _BIG}
    smalls = [None] * L

    def own_block(r, g):
        return lax.dynamic_update_slice(r, lax.dynamic_slice_in_dim(g, chip, 1, axis=0), (chip, 0, 0))

    def update(l, grads):
        for nm, g in zip(_BIG, grads):
            out[nm][l] = _adamw(given[nm][l], given["m_" + nm][l], given["v_" + nm][l], *g)

    pending = ()
    for l in reversed(range(L)):
        dX, big, smalls[l], recv = layer_bwd(dX, l, Ws[l], saved[l], pending)
        if recv:
            mine = [_sum_lead(own_block(r, g), "sum_chips") for r, g in zip(recv, pending)]
            update(l + 1, zip(mine, _sibling_swap(mine, "swap_planes")))
        pending = big
    sib = _sibling_halves(pending, "swap_halves")
    own = [lax.dynamic_slice_in_dim(g, mc * (g.shape[1] // 2), g.shape[1] // 2, axis=1) for g in big]
    sent = [_add_cast(g, s_) for g, s_ in zip(own, sib)]
    recv = [own_block(r, g) for r, g in zip(_chip_scatter(sent, "scatter_grads"), sent)]
    halves = [_sum_lead(r, "sum_chips") for r in recv]
    full = _sibling_fill(halves, "fill_halves")
    update(0, [(lax.dynamic_update_slice(f, hv, (mc * hv.shape[0], 0)),) for f, hv in zip(full, halves)])
    grad_x = dX[:S][None]

    def flat(nm):
        return jnp.stack([smalls[l][nm] for l in range(L)]).reshape(-1)

    dmod_all = jnp.stack([smalls[l]["dmod"] for l in range(L)])
    dml = dmod_all[:, 0].reshape(-1)
    dmc = dmod_all[:, 1].reshape(-1)
    names = ("norm1", "q_gain", "k_gain", "conv_w", "sg_norm", "w_s", "b_s", "norm2")
    parts = [dml, dml + dmc, dmc] + [flat(nm) for nm in names]
    sizes = [int(a.shape[0]) for a in parts]
    total = sum(sizes)
    padn = (-total) % (8 * LANES)
    sbuf = jnp.concatenate(parts + [jnp.zeros((padn,), F32)]).reshape(-1, LANES)
    g3 = _ag8(sbuf, "gather_small")
    ssum = _sum_lead(g3, "sum_devices").reshape(-1)
    offs = np.cumsum([0] + sizes)
    seg = {nm: ssum[offs[i + 3]:offs[i + 4]] for i, nm in enumerate(names)}
    gb_mod = ssum[offs[1]:offs[2]].reshape(L, N_CHIPS * n_mod)
    dmc_sum = ssum[offs[2]:offs[3]].reshape(L, N_CHIPS * n_mod)
    dml_all = g3.reshape(N_DEV, -1)[:, :sizes[0]].reshape(N_DEV, L, N_CHIPS * n_mod)
    dml_sh = jnp.swapaxes(lax.dynamic_slice_in_dim(dml_all, chip * n_mod, n_mod, axis=2), 0, 1)
    dmc_sh = lax.dynamic_slice_in_dim(dmc_sum, chip * n_mod, n_mod, axis=1)[:, None, :]
    dms = jnp.concatenate([dml_sh, dmc_sh, jnp.zeros((L, 7, n_mod), F32)], axis=1)
    g_wmod = _wmod_grad(a_raw, dms)
    part = _cctx_partial(jnp.concatenate([dmc_sh, jnp.zeros((L, 15, n_mod), F32)], axis=1), w_mod)
    g4 = _ag8(part.reshape(-1, LANES), "gather_cctx")
    g_cctx = _cctx_final(g4, c_ctx.reshape(8, LANES)).reshape(D)

    g_conv = lax.dynamic_slice_in_dim(seg["conv_w"].reshape(L, 3, N_CHIPS * n_cw), chip * n_cw, n_cw, axis=2)
    small_g = dict(c_ctx=g_cctx, w_mod=g_wmod, b_mod=gb_mod, norm1=seg["norm1"].reshape(norm1.shape),
                   q_gain=seg["q_gain"].reshape(q_gain.shape), k_gain=seg["k_gain"].reshape(k_gain.shape),
                   conv_w=g_conv, sg_norm=seg["sg_norm"].reshape(sg_norm.shape), w_s=seg["w_s"].reshape(w_s.shape),
                   b_s=seg["b_s"].reshape(b_s.shape), norm2=seg["norm2"].reshape(norm2.shape))
    res = {}
    for nm in _WEIGHTS:
        if nm in _BIG:
            res[nm] = [jnp.stack([out[nm][l][k] for l in range(L)]) for k in range(4)]
        else:
            res[nm] = _adamw(given[nm], given["m_" + nm], given["v_" + nm], small_g[nm])
    return (loss, grad_x, *[res[nm][0] for nm in _WEIGHTS], *[res[nm][1] for nm in _WEIGHTS],
            *[res[nm][2] for nm in _WEIGHTS], *[res[nm][3] for nm in _WEIGHTS])
```

```python
import functools

import jax
import jax.numpy as jnp
import numpy as np
from jax import lax
from jax.experimental import pallas as pl
from jax.experimental.pallas import tpu as pltpu

F32 = jnp.float32
BF16 = jnp.bfloat16
EPS = 1e-6
D_MODEL = 1024
HEAD_DIM = 64
N_Q_HEADS = 8
N_KV_HEADS = 2
GROUP = N_Q_HEADS // N_KV_HEADS
GRID_W = 64
ROPE_THETA = 10000.0
CHUNK = 128
CONV_W = 256
SG_W = 256
OFF_Q = 3 * CONV_W
QKV_W = 768
OFF_U = OFF_Q + QKV_W
OFF_G = OFF_U + 2 * SG_W
IN_W = OFF_G + 3 * D_MODEL
N_CHIPS = 4
N_DEV = 8
LANES = 128
UNROLL_FWD = 8
UNROLL_BWD = 4
AUG = 3
ADAM_LR, ADAM_B1, ADAM_B2, ADAM_EPS, ADAM_WD, ADAM_STEP = 0.001, 0.9, 0.999, 1e-8, 0.01, 10
VMEM_LIMIT_V7X = 52 * 1024 * 1024
MXU_DEPTH_V7X = 256
MESH_ID = pl.DeviceIdType.MESH
NT = (((1,), (1,)), ((), ()))
TN = (((0,), (0,)), ((), ()))
ANY = pl.BlockSpec(memory_space=pl.ANY)


def _cp(*sem):
    return pltpu.CompilerParams(dimension_semantics=sem or None, vmem_limit_bytes=VMEM_LIMIT_V7X)


def _tile(n, target, mult=16):
    best = None
    for t in range(mult, n + 1, mult):
        if n % t == 0 and t <= target:
            best = t
    assert best is not None, (n, target, mult)
    return best


def _full(shape):
    nd = len(shape)
    return pl.BlockSpec(tuple(shape), lambda *_: (0,) * nd)


def _segments(i, tm, n_lat, fn):
    k, off = divmod(n_lat, tm)

    @pl.when(i < k)
    def _():
        fn(0, tm, 0)

    @pl.when(i == k)
    def _():
        if off:
            fn(0, off, 0)
        fn(off, tm, 1)

    @pl.when(i > k)
    def _():
        fn(0, tm, 1)


def _dot(a, b):
    return jnp.dot(a, b, preferred_element_type=F32)


def _dg(a, b, dims):
    return lax.dot_general(a, b, dims, preferred_element_type=F32)


def _split3(x):
    hi = x.astype(BF16)
    r1 = x - hi.astype(F32)
    mid = r1.astype(BF16)
    lo = (r1 - mid.astype(F32)).astype(BF16)
    return hi.astype(F32), mid.astype(F32), lo.astype(F32)


def _lane(shape):
    return lax.broadcasted_iota(jnp.int32, shape, len(shape) - 1)


def _aug(val, stat):
    lane = _lane(val.shape)
    hi, mid, lo = _split3(stat)
    ext = jnp.where(lane == 64, hi, jnp.where(lane == 65, mid, jnp.where(lane == 66, lo, 0.0)))
    return jnp.where(lane < 64, val, ext)


def _seg_mean(x, e):
    outs = []
    for g in range(x.shape[1] // LANES):
        blk = x[:, g * LANES:(g + 1) * LANES]
        hi = blk.astype(BF16)
        lo = (blk - hi.astype(F32)).astype(BF16)
        outs.append(_dot(hi, e) + _dot(lo, e))
    return outs[0] if len(outs) == 1 else jnp.concatenate(outs, axis=1)


def _rope(x, cos, sin_signed, inverse):
    w = x.shape[1]
    reps = w // LANES
    c = cos if reps == 1 else jnp.tile(cos, (1, reps))
    s = sin_signed if reps == 1 else jnp.tile(sin_signed, (1, reps))
    first = (_lane(x.shape) % 32) < 16
    partner = jnp.where(first, pltpu.roll(x, w - 16, 1), pltpu.roll(x, 16, 1))
    return x * c - partner * s if inverse else x * c + partner * s


def _sigmoid(x):
    return 1.0 / (1.0 + jnp.exp(-x))


_GELU_K = 0.7978845608028654
_GELU_C = 0.044715


def _gelu(x):
    return 0.5 * x * (1.0 + jnp.tanh(_GELU_K * (x + _GELU_C * x * x * x)))


def _gelu_grad(x):
    t = jnp.tanh(_GELU_K * (x + _GELU_C * x * x * x))
    return 0.5 * (1.0 + t) + 0.5 * x * (1.0 - t * t) * _GELU_K * (1.0 + 3.0 * _GELU_C * x * x)


def _loop_unrolled(n, step, init, unroll):
    def trip(t, carry):
        for u in range(unroll):
            carry = step(t * unroll + u, carry)
        return carry

    carry = lax.fori_loop(0, n // unroll, trip, init) if n >= unroll else init
    for r in range(n - n % unroll, n):
        carry = step(r, carry)
    return carry


def _heads_to_rows(x, n_heads):
    out = []
    for h in range(n_heads):
        grp = x[:, (h // 2) * LANES:(h // 2 + 1) * LANES]
        out.append(grp if h % 2 == 0 else pltpu.roll(grp, 64, 1))
    return out


def _rows_to_heads(blocks):
    outs = []
    lane = _lane(blocks[0].shape)
    for a in range(len(blocks) // 2):
        outs.append(jnp.where(lane < 64, blocks[2 * a], pltpu.roll(blocks[2 * a + 1], 64, 1)))
    return outs[0] if len(outs) == 1 else jnp.concatenate(outs, axis=1)


def _norm_mod(x, g, mod, i_shift, i_scale, n_lat):
    T, D = x.shape
    tm = _tile(T, 768, LANES)

    def body(x_ref, g_ref, mod_ref, h_ref, ht_ref):
        def fn(r0, r1, seg):
            xv = x_ref[r0:r1, :]
            r = lax.rsqrt(jnp.mean(xv * xv, axis=-1, keepdims=True) + EPS)
            n = xv * r * g_ref[...]
            h = n * (1.0 + mod_ref[seg, i_scale:i_scale + 1, :]) + mod_ref[seg, i_shift:i_shift + 1, :]
            h_ref[r0:r1, :] = h.astype(BF16)

        _segments(pl.program_id(0), tm, n_lat, fn)
        ht_ref[...] = h_ref[...].astype(F32).T.astype(BF16)

    return pl.pallas_call(
        body, grid=(T // tm,), name="norm_mod",
        in_specs=[pl.BlockSpec((tm, D), lambda i: (i, 0)), _full(g.shape), _full(mod.shape)],
        out_specs=[pl.BlockSpec((tm, D), lambda i: (i, 0)), pl.BlockSpec((D, tm), lambda i: (0, i))],
        out_shape=[jax.ShapeDtypeStruct((T, D), BF16), jax.ShapeDtypeStruct((D, T), BF16)],
        compiler_params=_cp("parallel"))(x, g, mod)


def _norm_mod_bwd(x, dh, dres, g, mod, i_scale, n_lat):
    T, D = x.shape
    tm = _tile(T, 528)

    def body(x_ref, dh_ref, dres_ref, g_ref, mod_ref, dx_ref, dg_ref, dsh_ref, dsc_ref):
        i = pl.program_id(0)

        @pl.when(i == 0)
        def _():
            dg_ref[...] = jnp.zeros_like(dg_ref)
            dsh_ref[...] = jnp.zeros_like(dsh_ref)
            dsc_ref[...] = jnp.zeros_like(dsc_ref)

        def fn(r0, r1, seg):
            xv = x_ref[r0:r1, :]
            dh = dh_ref[r0:r1, :]
            r = lax.rsqrt(jnp.mean(xv * xv, axis=-1, keepdims=True) + EPS)
            xh = xv * r
            gv = g_ref[...]
            dsh_ref[seg] += jnp.sum(dh, axis=0, keepdims=True)
            dsc_ref[seg] += jnp.sum(dh * (xh * gv), axis=0, keepdims=True)
            dn = dh * (1.0 + mod_ref[seg, i_scale:i_scale + 1, :])
            dg_ref[...] += jnp.sum(dn * xh, axis=0, keepdims=True)
            gd = gv * dn
            dx_ref[r0:r1, :] = dres_ref[r0:r1, :] + r * (gd - xh * jnp.mean(xh * gd, axis=-1, keepdims=True))

        _segments(i, tm, n_lat, fn)

    row = pl.BlockSpec((tm, D), lambda i: (i, 0))
    return pl.pallas_call(
        body, grid=(T // tm,), name="norm_mod_bwd",
        in_specs=[row, row, row, _full(g.shape), _full(mod.shape)],
        out_specs=[row, _full((1, D)), _full((2, 1, D)), _full((2, 1, D))],
        out_shape=[jax.ShapeDtypeStruct((T, D), F32), jax.ShapeDtypeStruct((1, D), F32),
                   jax.ShapeDtypeStruct((2, 1, D), F32), jax.ShapeDtypeStruct((2, 1, D), F32)],
        compiler_params=_cp("arbitrary"))(x, dh, dres, g, mod)


def _gate_bwd(dx, f, mod, i_gate, n_lat):
    T, D = dx.shape
    tm = _tile(T, 528)

    def body(dx_ref, f_ref, mod_ref, dy_ref, dg_ref):
        i = pl.program_id(0)

        @pl.when(i == 0)
        def _():
            dg_ref[...] = jnp.zeros_like(dg_ref)

        def fn(r0, r1, seg):
            dxv = dx_ref[r0:r1, :]
            dy_ref[r0:r1, :] = (dxv * mod_ref[seg, i_gate:i_gate + 1, :]).astype(BF16)
            dg_ref[seg] += jnp.sum(dxv * f_ref[r0:r1, :].astype(F32), axis=0, keepdims=True)

        _segments(i, tm, n_lat, fn)

    row = pl.BlockSpec((tm, D), lambda i: (i, 0))
    return pl.pallas_call(
        body, grid=(T // tm,), name="gate_bwd",
        in_specs=[row, row, _full(mod.shape)], out_specs=[row, _full((2, 1, D))],
        out_shape=[jax.ShapeDtypeStruct((T, D), BF16), jax.ShapeDtypeStruct((2, 1, D), F32)],
        compiler_params=_cp("arbitrary"))(dx, f, mod)


def _mm_nn(a, w, out_dtype, name):
    M, K = a.shape
    J, _, n = w.shape
    tm = _tile(M, 1056)

    def body(a_ref, w_ref, o_ref):
        o_ref[...] = _dot(a_ref[...], w_ref[...]).astype(o_ref.dtype)

    return pl.pallas_call(
        body, grid=(M // tm, J), name=name,
        in_specs=[pl.BlockSpec((tm, K), lambda i, j: (i, 0)), pl.BlockSpec((None, K, n), lambda i, j: (j, 0, 0))],
        out_specs=pl.BlockSpec((tm, n), lambda i, j: (i, j)),
        out_shape=jax.ShapeDtypeStruct((M, J * n), out_dtype), compiler_params=_cp("parallel", "arbitrary"))(a, w)


def _mm_res(a3, w, res, mod, i_gate, n_lat, name):
    J, M, k = a3.shape
    N = w.shape[2]
    tm = _tile(M, 528)

    def body(a_ref, w_ref, res_ref, mod_ref, x_ref, f_ref):
        acc = _dot(a_ref[0], w_ref[0])
        for j in range(1, J):
            acc += _dot(a_ref[j], w_ref[j])
        f_ref[...] = acc.astype(BF16)

        def fn(r0, r1, seg):
            x_ref[r0:r1, :] = res_ref[r0:r1, :] + mod_ref[seg, i_gate:i_gate + 1, :] * acc[r0:r1, :]

        _segments(pl.program_id(0), tm, n_lat, fn)

    row = pl.BlockSpec((tm, N), lambda i: (i, 0))
    return pl.pallas_call(
        body, grid=(M // tm,), name=name,
        in_specs=[pl.BlockSpec((J, tm, k), lambda i: (0, i, 0)), _full(w.shape), row, _full(mod.shape)],
        out_specs=[row, row],
        out_shape=[jax.ShapeDtypeStruct((M, N), F32), jax.ShapeDtypeStruct((M, N), BF16)],
        compiler_params=_cp("parallel"))(a3, w, res, mod)


def _mm_nt_acc(dys, ws, row_major, name):
    J, K, n = ws[0].shape
    M = dys[0].shape[0] if row_major else dys[0].shape[1]
    tm = _tile(M, 1056)
    P = len(dys)

    def body(*refs):
        o_ref = refs[2 * P]
        j = pl.program_id(1)
        part = _dg(refs[0][...], refs[P][...], NT)
        for p in range(1, P):
            part += _dg(refs[p][...], refs[P + p][...], NT)

        @pl.when(j == 0)
        def _():
            o_ref[...] = part

        @pl.when(j > 0)
        def _():
            o_ref[...] += part

    dy_spec = (pl.BlockSpec((tm, n), lambda i, j: (i, j)) if row_major
               else pl.BlockSpec((None, tm, n), lambda i, j: (j, i, 0)))
    w_spec = pl.BlockSpec((None, K, n), lambda i, j: (j, 0, 0))
    return pl.pallas_call(
        body, grid=(M // tm, J), name=name,
        in_specs=[dy_spec] * P + [w_spec] * P,
        out_specs=pl.BlockSpec((tm, K), lambda i, j: (i, 0)),
        out_shape=jax.ShapeDtypeStruct((M, K), F32), compiler_params=_cp("parallel", "arbitrary"))(*dys, *ws)


def _mm_tn(x, dy, x_spec, dy_spec, J, K, n, T, name):
    tk = _tile(T, 1056, MXU_DEPTH_V7X)
    nt = T // tk

    def body(x_ref, dy_ref, o_ref, acc):
        t = pl.program_id(1)
        part = _dg(x_ref[...], dy_ref[...], TN)

        @pl.when(t == 0)
        def _():
            acc[...] = part

        @pl.when(t > 0)
        def _():
            acc[...] += part

        @pl.when(t == nt - 1)
        def _():
            o_ref[...] = acc[...].astype(BF16)

    return pl.pallas_call(
        body, grid=(J, nt), name=name,
        in_specs=[x_spec(tk), dy_spec(tk)],
        out_specs=pl.BlockSpec((None, K, n), lambda j, t: (j, 0, 0)),
        out_shape=jax.ShapeDtypeStruct((J, K, n), BF16), scratch_shapes=[pltpu.VMEM((K, n), F32)],
        compiler_params=_cp("parallel", "arbitrary"))(x, dy)


def _mm_dw(xt, dy, dy_spec, J, n, name):
    K, T = xt.shape
    tk = _tile(T, 1056, MXU_DEPTH_V7X)
    nt = T // tk

    def body(xt_ref, dy_ref, o_ref, acc):
        t = pl.program_id(1)
        part = _dot(xt_ref[...], dy_ref[...])

        @pl.when(t == 0)
        def _():
            acc[...] = part

        @pl.when(t > 0)
        def _():
            acc[...] += part

        @pl.when(t == nt - 1)
        def _():
            o_ref[...] = acc[...].astype(BF16)

    return pl.pallas_call(
        body, grid=(J, nt), name=name,
        in_specs=[pl.BlockSpec((K, tk), lambda j, t: (0, t)), dy_spec(tk)],
        out_specs=pl.BlockSpec((None, K, n), lambda j, t: (j, 0, 0)),
        out_shape=jax.ShapeDtypeStruct((J, K, n), BF16), scratch_shapes=[pltpu.VMEM((K, n), F32)],
        compiler_params=_cp("parallel", "arbitrary"))(xt, dy)


def _rows(width):
    return lambda tk: pl.BlockSpec((tk, width), lambda j, t: (t, 0))


def _row_cols(width):
    return lambda tk: pl.BlockSpec((tk, width), lambda j, t: (t, j))


def _shard_rows(width):
    return lambda tk: pl.BlockSpec((None, tk, width), lambda j, t: (j, t, 0))


def _ffn_up(h, w1, w3):
    T, D = h.shape
    J, _, n = w1.shape
    tm = _tile(T, 1056)

    def body(h_ref, w1_ref, w3_ref, a1_ref, a3_ref, act_ref):
        hv = h_ref[...]
        a1 = _dot(hv, w1_ref[...])
        a3 = _dot(hv, w3_ref[...])
        a1_ref[...] = a1.astype(BF16)
        a3_ref[...] = a3.astype(BF16)
        act_ref[...] = (a1 * _sigmoid(a1) * a3).astype(BF16)

    w_spec = pl.BlockSpec((None, D, n), lambda i, j: (j, 0, 0))
    o_spec = pl.BlockSpec((None, tm, n), lambda i, j: (j, i, 0))
    return pl.pallas_call(
        body, grid=(T // tm, J), name="ffn_up",
        in_specs=[pl.BlockSpec((tm, D), lambda i, j: (i, 0)), w_spec, w_spec], out_specs=[o_spec] * 3,
        out_shape=[jax.ShapeDtypeStruct((J, T, n), BF16)] * 3,
        compiler_params=_cp("parallel", "arbitrary"))(h, w1, w3)


def _ffn_down_bwd(dy, w2, a1, a3):
    T, D = dy.shape
    J, n, _ = w2.shape
    tm = _tile(T, 1056)

    def body(dy_ref, w2_ref, a1_ref, a3_ref, da1_ref, da3_ref):
        dact = _dg(dy_ref[...], w2_ref[...], NT)
        a1v = a1_ref[...].astype(F32)
        sig = _sigmoid(a1v)
        da3_ref[...] = (dact * a1v * sig).astype(BF16)
        da1_ref[...] = (dact * a3_ref[...].astype(F32) * (sig * (1.0 + a1v * (1.0 - sig)))).astype(BF16)

    a_spec = pl.BlockSpec((None, tm, n), lambda i, j: (j, i, 0))
    return pl.pallas_call(
        body, grid=(T // tm, J), name="ffn_down_bwd",
        in_specs=[pl.BlockSpec((tm, D), lambda i, j: (i, 0)), pl.BlockSpec((None, n, D), lambda i, j: (j, 0, 0)),
                  a_spec, a_spec],
        out_specs=[a_spec, a_spec], out_shape=[jax.ShapeDtypeStruct((J, T, n), BF16)] * 2,
        compiler_params=_cp("parallel", "arbitrary"))(dy, w2, a1, a3)


def _qkv_prep(p, cos, sin, qg, kg, e):
    T = p.shape[0]
    tm = _tile(T, 528)

    def body(p_ref, cos_ref, sin_ref, qg_ref, kg_ref, e_ref, q_ref, k_ref, v_ref):
        ev = e_ref[...]
        cv, sv = cos_ref[...], sin_ref[...]
        xq = p_ref[:, 0:512]
        qn = xq * lax.rsqrt(_seg_mean(xq * xq, ev) + EPS) * qg_ref[...]
        qr = _rope(qn, cv, sv, False) * (HEAD_DIM ** -0.5)
        xk = p_ref[:, 512:640]
        kn = xk * lax.rsqrt(_seg_mean(xk * xk, ev) + EPS) * kg_ref[...]
        kr = _rope(kn, cv, sv, False)
        lane = _lane((tm, LANES))
        ones = jnp.where(lane < 64 + AUG, -1.0, 0.0)
        for h, blk in enumerate(_heads_to_rows(qr, N_Q_HEADS)):
            q_ref[h] = jnp.where(lane < 64, blk, 0.0).astype(BF16)
        for h, blk in enumerate(_heads_to_rows(kr, N_KV_HEADS)):
            k_ref[h] = jnp.where(lane < 64, blk, ones).astype(BF16)
        for h, blk in enumerate(_heads_to_rows(p_ref[:, 640:768], N_KV_HEADS)):
            v_ref[h] = jnp.where(lane < 64, blk, ones).astype(BF16)

    tab = pl.BlockSpec((tm, LANES), lambda i: (i, 0))
    return pl.pallas_call(
        body, grid=(T // tm,), name="qkv_prep",
        in_specs=[pl.BlockSpec((tm, QKV_W), lambda i: (i, 1)), tab, tab, _full(qg.shape), _full(kg.shape),
                  _full(e.shape)],
        out_specs=[pl.BlockSpec((N_Q_HEADS, tm, LANES), lambda i: (0, i, 0)),
                   pl.BlockSpec((N_KV_HEADS, tm, LANES), lambda i: (0, i, 0)),
                   pl.BlockSpec((N_KV_HEADS, tm, LANES), lambda i: (0, i, 0))],
        out_shape=[jax.ShapeDtypeStruct((N_Q_HEADS, T, LANES), BF16),
                   jax.ShapeDtypeStruct((N_KV_HEADS, T, LANES), BF16),
                   jax.ShapeDtypeStruct((N_KV_HEADS, T, LANES), BF16)],
        compiler_params=_cp("parallel"))(p, cos, sin, qg, kg, e)


def _qkv_prep_bwd(dp, dq, dk, dv, p, cos, sin, qg, kg, e, fold):
    T = p.shape[0]
    tq = dq.shape[3] // GROUP
    tm = _tile(T, 768, tq)
    nt = T // tm

    def body(dp_in, dq_ref, dk_ref, dv_ref, p_ref, cos_ref, sin_ref, qg_ref, kg_ref, e_ref, fold_ref,
             dp_ref, dqg_ref, dkg_ref, accq, acck):
        del dp_in
        i = pl.program_id(0)

        @pl.when(i == 0)
        def _():
            accq[...] = jnp.zeros_like(accq)
            acck[...] = jnp.zeros_like(acck)

        ev = e_ref[...]
        cv, sv = cos_ref[...], sin_ref[...]

        def one(x, dr, gain, acc):
            r = lax.rsqrt(_seg_mean(x * x, ev) + EPS)
            xh = x * r
            dn = _rope(dr, cv, sv, True)
            acc[0:1, :] += jnp.sum(dn * xh, axis=0, keepdims=True)
            gd = gain * dn
            return r * (gd - xh * _seg_mean(xh * gd, ev))

        slabs = [[dq_ref[h, b].T for b in range(tm // tq)] for h in range(N_KV_HEADS)]
        heads = [jnp.concatenate([sl[g * tq:(g + 1) * tq] for sl in slabs[h]], axis=0)
                 for h in range(N_KV_HEADS) for g in range(GROUP)]
        dqr = _rows_to_heads(heads) * (HEAD_DIM ** -0.5)
        dkr = _rows_to_heads([dk_ref[h] for h in range(N_KV_HEADS)])
        dvv = _rows_to_heads([dv_ref[h] for h in range(N_KV_HEADS)])
        dp_ref[:, 0:512] = one(p_ref[:, 0:512], dqr, qg_ref[...], accq).astype(BF16)
        dp_ref[:, 512:640] = one(p_ref[:, 512:640], dkr, kg_ref[...], acck).astype(BF16)
        dp_ref[:, 640:768] = dvv.astype(BF16)

        @pl.when(i == nt - 1)
        def _():
            fv = fold_ref[...]
            dqg_ref[...] = jnp.dot(accq[...], fv, preferred_element_type=F32, precision=lax.Precision.HIGHEST)
            dkg_ref[...] = jnp.dot(acck[...], fv[0:LANES, :], preferred_element_type=F32,
                                   precision=lax.Precision.HIGHEST)

    tab = pl.BlockSpec((tm, LANES), lambda i: (i, 0))
    sec = pl.BlockSpec((tm, QKV_W), lambda i: (i, 1))
    return pl.pallas_call(
        body, grid=(nt,), name="qkv_prep_bwd",
        in_specs=[ANY, pl.BlockSpec((N_KV_HEADS, tm // tq, LANES, GROUP * tq), lambda i: (0, i, 0, 0)),
                  pl.BlockSpec((N_KV_HEADS, tm, LANES), lambda i: (0, i, 0)),
                  pl.BlockSpec((N_KV_HEADS, tm, LANES), lambda i: (0, i, 0)),
                  sec, tab, tab, _full(qg.shape), _full(kg.shape), _full(e.shape), _full(fold.shape)],
        out_specs=[sec, _full((8, LANES)), _full((8, LANES))],
        out_shape=[jax.ShapeDtypeStruct(dp.shape, BF16), jax.ShapeDtypeStruct((8, LANES), F32),
                   jax.ShapeDtypeStruct((8, LANES), F32)],
        scratch_shapes=[pltpu.VMEM((8, 512), F32), pltpu.VMEM((8, LANES), F32)],
        input_output_aliases={0: 0}, compiler_params=_cp("arbitrary"))(dp, dq, dk, dv, p, cos, sin, qg, kg, e, fold)


def _flash_fwd(q, k, v, n_lat, gather=()):
    _, _, T, _ = q.shape
    tq = tk = 256
    nq = T // tq
    M = GROUP * tq

    nk_lat = n_lat // tk

    n_g = len(gather)

    def body(q_ref, k_ref, v_ref, *rest):
        o_ref, qa_ref = rest[n_g], rest[n_g + 1]
        i = pl.program_id(0)
        if n_g:
            bufs = rest[n_g + 2:2 * n_g + 2]
            start, wait = _plane_exchange(bufs, bufs, rest[-2], rest[-1], False)
            pl.when((i == 0) & (pl.program_id(1) == 0))(start)
        qv = q_ref[...].reshape(M, LANES)

        def step(s, carry):
            m, acc = carry
            r0 = s * tk if isinstance(s, int) else pl.multiple_of(s * tk, tk)
            sc = _dg(qv, k_ref[pl.ds(r0, tk), :], NT)
            m_new = jnp.maximum(m, jnp.max(sc, axis=1, keepdims=True))
            pr = jnp.exp(sc - m_new)
            acc = jnp.exp(m - m_new) * acc + _dot(pr.astype(BF16), v_ref[pl.ds(r0, tk), :])
            return m_new, acc

        def finish(m, acc):
            den = -acc[:, 64:65]
            out = acc / den
            o_ref[...] = _rows_to_heads([out[g * tq:(g + 1) * tq] for g in range(GROUP)]).astype(BF16)
            qa_ref[...] = _aug(qv.astype(F32), m + jnp.log(den)).astype(BF16).reshape(GROUP, tq, LANES)

        init = (jnp.full((M, 1), -1e30, F32), jnp.zeros((M, LANES), F32))

        @pl.when(i < n_lat // tq)
        def _():
            carry = _loop_unrolled(nk_lat, step, init, UNROLL_FWD)
            for s in range(nk_lat, T // tk):
                carry = step(s, carry)
            finish(*carry)

        @pl.when(i >= n_lat // tq)
        def _():
            carry = init
            for s in range(nk_lat, T // tk):
                carry = step(s, carry)
            finish(*carry)

        if n_g:
            pl.when((i == nq - 1) & (pl.program_id(1) == N_KV_HEADS - 1))(wait)

    q_spec = pl.BlockSpec((None, GROUP, tq, LANES), lambda i, h: (h, 0, i, 0))
    kv_spec = pl.BlockSpec((None, T, LANES), lambda i, h: (h, 0, 0))
    sems = [pltpu.SemaphoreType.DMA((3 * n_g,))] * 2 if n_g else []
    return pl.pallas_call(
        body, grid=(nq, N_KV_HEADS), name="flash_fwd_gather" if n_g else "flash_fwd",
        in_specs=[q_spec, kv_spec, kv_spec] + [ANY] * n_g,
        out_specs=[pl.BlockSpec((tq, GROUP * HEAD_DIM), lambda i, h: (i, h)), q_spec] + [ANY] * n_g,
        out_shape=[jax.ShapeDtypeStruct((T, N_Q_HEADS * HEAD_DIM), BF16), jax.ShapeDtypeStruct(q.shape, BF16)]
        + [jax.ShapeDtypeStruct(b.shape, b.dtype) for b in gather],
        input_output_aliases={3 + a: 2 + a for a in range(n_g)}, scratch_shapes=sems,
        compiler_params=_cp("arbitrary", "arbitrary"))(q, k, v, *gather)


def _flash_bwd(qa, doa, k, v, n_lat, scatter=()):
    _, _, T, _ = qa.shape
    tq = tk = 256
    nkv = T // tk
    M = GROUP * tq

    n_s = len(scatter)

    def body(qa_hbm, doa_hbm, k_ref, v_ref, *rest):
        dq_hbm, dk_ref, dv_ref = rest[n_s:n_s + 3]
        q_sc, do_sc, dq_sc, sems = rest[2 * n_s + 3:2 * n_s + 7]
        h = pl.program_id(0)
        j = pl.program_id(1)
        if n_s:
            start, wait = _plane_exchange(rest[:n_s], rest[n_s + 3:2 * n_s + 3], rest[-2], rest[-1], True)
            pl.when((h == 0) & (j == 0))(start)

        @pl.when(j == 0)
        def _():
            c1 = pltpu.make_async_copy(qa_hbm.at[h], q_sc, sems.at[0])
            c2 = pltpu.make_async_copy(doa_hbm.at[h], do_sc, sems.at[1])
            c1.start()
            c2.start()
            dq_sc[...] = jnp.zeros_like(dq_sc)
            c1.wait()
            c2.wait()

        kb = k_ref[...]
        vb = v_ref[...]
        kbt = kb.astype(F32).T.astype(BF16)

        def step(i, carry):
            dk, dv = carry
            r0 = i * tq if isinstance(i, int) else pl.multiple_of(i * tq, tq)
            qv = q_sc[:, pl.ds(r0, tq), :].reshape(M, LANES)
            dov = do_sc[:, pl.ds(r0, tq), :].reshape(M, LANES)
            pr = jnp.exp(_dg(kb, qv, NT))
            ds = (pr * _dg(vb, dov, NT)).astype(BF16)
            dv = dv + _dot(pr.astype(BF16), dov)
            dk = dk + _dot(ds, qv)
            dq_sc[i] += _dot(kbt, ds)
            return dk, dv

        z = jnp.zeros((tk, LANES), F32)
        carry = _loop_unrolled(n_lat // tq, step, (z, z), UNROLL_BWD)
        dk_ref[...] = carry[0]
        dv_ref[...] = carry[1]

        @pl.when(j >= n_lat // tk)
        def _():
            c = (dk_ref[...], dv_ref[...])
            for i in range(n_lat // tq, T // tq):
                c = step(i, c)
            dk_ref[...] = c[0]
            dv_ref[...] = c[1]

        @pl.when(j == nkv - 1)
        def _():
            c3 = pltpu.make_async_copy(dq_sc, dq_hbm.at[h], sems.at[2])
            c3.start()
            c3.wait()

        if n_s:
            pl.when((h == N_KV_HEADS - 1) & (j == nkv - 1))(wait)

    kv_spec = pl.BlockSpec((None, tk, LANES), lambda h, j: (h, j, 0))
    return pl.pallas_call(
        body, grid=(N_KV_HEADS, nkv), name="flash_bwd_scatter" if n_s else "flash_bwd",
        in_specs=[ANY, ANY, kv_spec, kv_spec] + [ANY] * n_s, out_specs=[ANY, kv_spec, kv_spec] + [ANY] * n_s,
        out_shape=[jax.ShapeDtypeStruct((N_KV_HEADS, T // tq, LANES, M), F32), jax.ShapeDtypeStruct(k.shape, F32),
                   jax.ShapeDtypeStruct(k.shape, F32)] + [jax.ShapeDtypeStruct(g.shape, g.dtype) for g in scatter],
        scratch_shapes=[pltpu.VMEM((GROUP, T, LANES), BF16), pltpu.VMEM((GROUP, T, LANES), BF16),
                        pltpu.VMEM((T // tq, LANES, M), F32), pltpu.SemaphoreType.DMA((3,))]
        + ([pltpu.SemaphoreType.DMA((3 * n_s,))] * 2 if n_s else []),
        compiler_params=_cp("arbitrary", "arbitrary"))(qa, doa, k, v, *scatter)


def _conv_masks(i, tm, n_lat, T):
    row = lax.broadcasted_iota(jnp.int32, (tm, 1), 0)
    g = row + i * tm
    return row, (g == 0) | (g == n_lat), (g == n_lat - 1) | (g == T - 1)


def _shift_rows(v, prev_row, next_row, row, first, last):
    tm = v.shape[0]
    down = jnp.where(row == 0, prev_row, pltpu.roll(v, 1, 0))
    up = jnp.where(row == tm - 1, next_row, pltpu.roll(v, tm - 1, 0))
    return jnp.where(first, 0.0, down), jnp.where(last, 0.0, up)


def _halo_specs(tm, T, width, col):
    nb = T // 8
    return (pl.BlockSpec((8, width), lambda i: (jnp.maximum(i * (tm // 8) - 1, 0), col)),
            pl.BlockSpec((8, width), lambda i: (jnp.minimum((i + 1) * (tm // 8), nb - 1), col)))


def _conv_fwd(p, cw, n_lat):
    T = p.shape[0]
    tm = _tile(T, 1056)

    def body(p_ref, pp_ref, pn_ref, cw_ref, o_ref):
        row, first, last = _conv_masks(pl.program_id(0), tm, n_lat, T)
        z = p_ref[:, 256:512] * p_ref[:, 512:768]
        zp = pp_ref[7:8, 256:512] * pp_ref[7:8, 512:768]
        zn = pn_ref[0:1, 256:512] * pn_ref[0:1, 512:768]
        zd, zu = _shift_rows(z, zp, zn, row, first, last)
        conv = cw_ref[0:1, :] * zd + cw_ref[1:2, :] * z + cw_ref[2:3, :] * zu
        o_ref[...] = (p_ref[:, 0:256] * conv).astype(BF16)

    prev, nxt = _halo_specs(tm, T, 768, 0)
    return pl.pallas_call(
        body, grid=(T // tm,), name="conv_fwd",
        in_specs=[pl.BlockSpec((tm, 768), lambda i: (i, 0)), prev, nxt, _full(cw.shape)],
        out_specs=pl.BlockSpec((tm, CONV_W), lambda i: (i, 0)),
        out_shape=jax.ShapeDtypeStruct((T, CONV_W), BF16), compiler_params=_cp("parallel"))(p, p, p, cw)


def _conv_bwd(dp, dy, p, cw, n_lat):
    T = p.shape[0]
    tm = _tile(T, 1056)

    def body(dp_in, dy_ref, dyp_ref, dyn_ref, p_ref, pp_ref, pn_ref, cw_ref, dp_ref, dcw_ref):
        del dp_in
        i = pl.program_id(0)

        @pl.when(i == 0)
        def _():
            dcw_ref[...] = jnp.zeros_like(dcw_ref)

        row, first, last = _conv_masks(i, tm, n_lat, T)
        ab, ac, ax = p_ref[:, 0:256], p_ref[:, 256:512], p_ref[:, 512:768]
        z = ac * ax
        zp = pp_ref[7:8, 256:512] * pp_ref[7:8, 512:768]
        zn = pn_ref[0:1, 256:512] * pn_ref[0:1, 512:768]
        zd, zu = _shift_rows(z, zp, zn, row, first, last)
        w0, w1, w2 = cw_ref[0:1, :], cw_ref[1:2, :], cw_ref[2:3, :]
        dy = dy_ref[...]
        dc = dy * ab
        dcd, dcu = _shift_rows(dc, dyp_ref[7:8, :] * pp_ref[7:8, 0:256], dyn_ref[0:1, :] * pn_ref[0:1, 0:256],
                               row, first, last)
        dz = w0 * dcu + w1 * dc + w2 * dcd
        dp_ref[:, 0:256] = (dy * (w0 * zd + w1 * z + w2 * zu)).astype(BF16)
        dp_ref[:, 256:512] = (dz * ax).astype(BF16)
        dp_ref[:, 512:768] = (dz * ac).astype(BF16)
        dcw_ref[0:1, :] += jnp.sum(dc * zd, axis=0, keepdims=True)
        dcw_ref[1:2, :] += jnp.sum(dc * z, axis=0, keepdims=True)
        dcw_ref[2:3, :] += jnp.sum(dc * zu, axis=0, keepdims=True)

    prev, nxt = _halo_specs(tm, T, 768, 0)
    dprev, dnxt = _halo_specs(tm, T, CONV_W, 0)
    sec = pl.BlockSpec((tm, 768), lambda i: (i, 0))
    return pl.pallas_call(
        body, grid=(T // tm,), name="conv_bwd",
        in_specs=[ANY, pl.BlockSpec((tm, CONV_W), lambda i: (i, 0)), dprev, dnxt, sec, prev, nxt, _full(cw.shape)],
        out_specs=[sec, _full((8, CONV_W))],
        out_shape=[jax.ShapeDtypeStruct(dp.shape, BF16), jax.ShapeDtypeStruct((8, CONV_W), F32)],
        input_output_aliases={0: 0}, compiler_params=_cp("arbitrary"))(dp, dy, dy, dy, p, p, p, cw)


def _gmlp_mix(bd_ref, vs, grp):
    out = jnp.zeros((2 * CHUNK, SG_W), F32)
    for g in range(4):
        out = jnp.where(grp == g, _dot(bd_ref[g], vs), out)
    return out


def _gmlp_fwd(p, sgn, bd, bias):
    T = p.shape[0]
    tm = _tile(T, 768, 2 * CHUNK)

    def body(p_ref, sgn_ref, bd_ref, bias_ref, o_ref):
        x = _gelu(p_ref[:, 256:512])
        vn = (x * lax.rsqrt(jnp.mean(x * x, axis=-1, keepdims=True) + EPS) * sgn_ref[...]).astype(BF16)
        grp = _lane((2 * CHUNK, SG_W)) // 64
        for s in range(tm // (2 * CHUNK)):
            rs = slice(s * 2 * CHUNK, (s + 1) * 2 * CHUNK)
            mixed = _gmlp_mix(bd_ref, vn[rs], grp) + bias_ref[...]
            o_ref[rs, :] = (_gelu(p_ref[rs, 0:256]) * mixed).astype(BF16)

    return pl.pallas_call(
        body, grid=(T // tm,), name="gmlp_fwd",
        in_specs=[pl.BlockSpec((tm, 2 * SG_W), lambda i: (i, 3)), _full(sgn.shape), _full(bd.shape),
                  _full(bias.shape)],
        out_specs=pl.BlockSpec((tm, SG_W), lambda i: (i, 0)),
        out_shape=jax.ShapeDtypeStruct((T, SG_W), BF16), compiler_params=_cp("parallel"))(p, sgn, bd, bias)


def _gmlp_bwd(dp, dy, p, sgn, bd, bdt, bias, gsum):
    T = p.shape[0]
    tm = _tile(T, 768, 2 * CHUNK)
    nt = T // tm
    C2 = 2 * CHUNK

    def body(dp_in, dy_ref, p_ref, sgn_ref, bd_ref, bdt_ref, bias_ref, gsum_ref,
             dp_ref, dsg_ref, dws_ref, dbs_ref, acc_w, acc_b):
        del dp_in
        i = pl.program_id(0)

        @pl.when(i == 0)
        def _():
            dsg_ref[...] = jnp.zeros_like(dsg_ref)
            acc_w[...] = jnp.zeros_like(acc_w)
            acc_b[...] = jnp.zeros_like(acc_b)

        u = p_ref[:, 0:256]
        sv = p_ref[:, 256:512]
        ug = _gelu(u)
        x = _gelu(sv)
        r = lax.rsqrt(jnp.mean(x * x, axis=-1, keepdims=True) + EPS)
        xh = x * r
        sg = sgn_ref[...]
        vn = (xh * sg).astype(BF16)
        grp = _lane((C2, SG_W)) // 64
        dug, dvn = [], []
        for s in range(tm // C2):
            rs = slice(s * C2, (s + 1) * C2)
            vs = vn[rs]
            dys = dy_ref[rs, :]
            dug.append(dys * (_gmlp_mix(bd_ref, vs, grp) + bias_ref[...]))
            dmix = dys * ug[rs]
            acc_b[...] += dmix
            dmb = dmix.astype(BF16)
            dvn.append(_gmlp_mix(bdt_ref, dmb, grp))
            for g in range(4):
                acc_w[g] += _dg(jnp.where(grp == g, dmb, jnp.zeros_like(dmb)), vs, NT)
        dug = jnp.concatenate(dug, axis=0)
        dvn = jnp.concatenate(dvn, axis=0)
        dsg_ref[...] += jnp.sum(dvn * xh, axis=0, keepdims=True)
        gd = sg * dvn
        dx = r * (gd - xh * jnp.mean(xh * gd, axis=-1, keepdims=True))
        dp_ref[:, 0:256] = (dug * _gelu_grad(u)).astype(BF16)
        dp_ref[:, 256:512] = (dx * _gelu_grad(sv)).astype(BF16)

        @pl.when(i == nt - 1)
        def _():
            for g in range(4):
                dws_ref[g] = acc_w[g, 0:CHUNK, 0:CHUNK] + acc_w[g, CHUNK:C2, CHUNK:C2]
            dbs_ref[...] = jnp.dot(acc_b[0:CHUNK, :] + acc_b[CHUNK:C2, :], gsum_ref[...],
                                   preferred_element_type=F32, precision=lax.Precision.HIGHEST)

    sec = pl.BlockSpec((tm, 2 * SG_W), lambda i: (i, 3))
    return pl.pallas_call(
        body, grid=(nt,), name="gmlp_bwd",
        in_specs=[ANY, pl.BlockSpec((tm, SG_W), lambda i: (i, 0)), sec, _full(sgn.shape), _full(bd.shape),
                  _full(bdt.shape), _full(bias.shape), _full(gsum.shape)],
        out_specs=[sec, _full((1, SG_W)), _full((4, CHUNK, CHUNK)), _full((CHUNK, LANES))],
        out_shape=[jax.ShapeDtypeStruct(dp.shape, BF16), jax.ShapeDtypeStruct((1, SG_W), F32),
                   jax.ShapeDtypeStruct((4, CHUNK, CHUNK), F32), jax.ShapeDtypeStruct((CHUNK, LANES), F32)],
        scratch_shapes=[pltpu.VMEM((4, C2, C2), F32), pltpu.VMEM((C2, SG_W), F32)],
        input_output_aliases={0: 0}, compiler_params=_cp("arbitrary"))(dp, dy, p, sgn, bd, bdt, bias, gsum)


def _merge_fwd(ya, at, yc, p, wa, wb, wc):
    T = p.shape[0]
    tm = _tile(T, 528)
    n = wa.shape[2]

    def body(ya_ref, at_ref, yc_ref, ga_ref, gb_ref, gc_ref, wa_ref, wb_ref, wc_ref, o_ref):
        yav, atv, ycv = ya_ref[...], at_ref[...], yc_ref[...]
        for j in range(N_CHIPS):
            cs = slice(j * n, (j + 1) * n)
            m = (_sigmoid(ga_ref[:, cs]) * _dot(yav, wa_ref[j]) + _sigmoid(gb_ref[:, cs]) * _dot(atv, wb_ref[j])
                 + _sigmoid(gc_ref[:, cs]) * _dot(ycv, wc_ref[j]))
            o_ref[:, cs] = m.astype(BF16)

    def rows(w, col=0):
        return pl.BlockSpec((tm, w), lambda i: (i, col))

    return pl.pallas_call(
        body, grid=(T // tm,), name="merge_fwd",
        in_specs=[rows(CONV_W), rows(512), rows(SG_W), rows(D_MODEL, 2), rows(D_MODEL, 3), rows(D_MODEL, 4),
                  _full(wa.shape), _full(wb.shape), _full(wc.shape)],
        out_specs=rows(D_MODEL), out_shape=jax.ShapeDtypeStruct((T, D_MODEL), BF16),
        compiler_params=_cp("parallel"))(ya, at, yc, p, p, p, wa, wb, wc)


def _merge_bwd(dyo, ya, at, yc, p, wa, wb, wc, wo):
    T = p.shape[0]
    tm = _tile(T, 528)
    n = wa.shape[2]

    def body(dyo_ref, ya_ref, at_ref, yc_ref, ga_ref, gb_ref, gc_ref, wa_ref, wb_ref, wc_ref, wo_ref,
             dp_ref, dya_ref, doa_ref, dyc_ref, dwa_ref, dwb_ref, dwc_ref):
        i = pl.program_id(0)

        @pl.when(i == 0)
        def _():
            dwa_ref[...] = jnp.zeros_like(dwa_ref)
            dwb_ref[...] = jnp.zeros_like(dwb_ref)
            dwc_ref[...] = jnp.zeros_like(dwc_ref)

        dp_ref[:, 0:OFF_G] = jnp.zeros((tm, OFF_G), BF16)
        dm = _dg(dyo_ref[...], wo_ref[...], NT)
        yav, atv, ycv = ya_ref[...], at_ref[...], yc_ref[...]
        dya = jnp.zeros((tm, CONV_W), F32)
        dat = jnp.zeros((tm, 512), F32)
        dyc = jnp.zeros((tm, SG_W), F32)
        for j in range(N_CHIPS):
            cs = slice(j * n, (j + 1) * n)
            dmj = dm[:, cs]
            for y_in, w_ref, g_ref, dw_ref, which in (
                    (yav, wa_ref, ga_ref, dwa_ref, 0), (atv, wb_ref, gb_ref, dwb_ref, 1),
                    (ycv, wc_ref, gc_ref, dwc_ref, 2)):
                sg = _sigmoid(g_ref[:, cs])
                y = _dot(y_in, w_ref[j])
                c0 = OFF_G + which * D_MODEL + j * n
                dp_ref[:, c0:c0 + n] = (dmj * y * sg * (1.0 - sg)).astype(BF16)
                dyb = (dmj * sg).astype(BF16)
                dw_ref[j] += _dg(y_in, dyb, TN)
                back = _dg(dyb, w_ref[j], NT)
                if which == 0:
                    dya = dya + back
                elif which == 1:
                    dat = dat + back
                else:
                    dyc = dyc + back
        dya_ref[...] = dya
        dyc_ref[...] = dyc
        prod = dat * atv.astype(F32)
        lane = _lane((tm, LANES))
        dat_rows = _heads_to_rows(dat, N_Q_HEADS)
        for h in range(N_Q_HEADS):
            grp = prod[:, (h // 2) * LANES:(h // 2 + 1) * LANES]
            keep = (lane < 64) if h % 2 == 0 else (lane >= 64)
            delta = jnp.sum(jnp.where(keep, grp, 0.0), axis=1, keepdims=True)
            doa_ref[h] = _aug(dat_rows[h], delta).astype(BF16)

    def rows(w, col=0):
        return pl.BlockSpec((tm, w), lambda i: (i, col))

    return pl.pallas_call(
        body, grid=(T // tm,), name="merge_bwd",
        in_specs=[rows(D_MODEL), rows(CONV_W), rows(512), rows(SG_W), rows(D_MODEL, 2), rows(D_MODEL, 3),
                  rows(D_MODEL, 4), _full(wa.shape), _full(wb.shape), _full(wc.shape), _full(wo.shape)],
        out_specs=[rows(IN_W), rows(CONV_W),
                   pl.BlockSpec((N_Q_HEADS, tm, LANES), lambda i: (0, i, 0)), rows(SG_W),
                   _full(wa.shape), _full(wb.shape), _full(wc.shape)],
        out_shape=[jax.ShapeDtypeStruct((T, IN_W), BF16)] + [
            jax.ShapeDtypeStruct((T, CONV_W), F32), jax.ShapeDtypeStruct((N_Q_HEADS, T, LANES), BF16),
            jax.ShapeDtypeStruct((T, SG_W), F32), jax.ShapeDtypeStruct(wa.shape, F32),
            jax.ShapeDtypeStruct(wb.shape, F32), jax.ShapeDtypeStruct(wc.shape, F32)],
        compiler_params=_cp("arbitrary"))(dyo, ya, at, yc, p, p, p, wa, wb, wc, wo)


def _loss_grad(xf, tgt, n_lat):
    T, D = xf.shape
    tm = _tile(np.gcd(n_lat, T), 512)
    nl = n_lat // tm

    def body(x_ref, t_ref, dy_ref, l_ref):
        i = pl.program_id(0)

        @pl.when(i == 0)
        def _():
            l_ref[...] = jnp.zeros_like(l_ref)

        @pl.when(i < nl)
        def _():
            err = x_ref[...] - t_ref[...]
            dy_ref[...] = err * (1.0 / D)
            sq = jnp.sum(jnp.sum(err * err, axis=1, keepdims=True), axis=0, keepdims=True)
            l_ref[...] += (0.5 / D) * sq

        @pl.when(i >= nl)
        def _():
            dy_ref[...] = jnp.zeros_like(dy_ref)

    return pl.pallas_call(
        body, grid=(T // tm,), name="loss_grad",
        in_specs=[pl.BlockSpec((tm, D), lambda i: (i, 0)), pl.BlockSpec((tm, D), lambda i: (jnp.minimum(i, nl - 1), 0))],
        out_specs=[pl.BlockSpec((tm, D), lambda i: (i, 0)), _full((8, LANES))],
        out_shape=[jax.ShapeDtypeStruct((T, D), F32), jax.ShapeDtypeStruct((8, LANES), F32)],
        compiler_params=_cp("arbitrary"))(xf, tgt)


def _row_tile(R, C):
    if R * C <= (1 << 19) or R % 8:
        return R
    return _tile(R, max(8, (1 << 19) // C), 8)


def _adamw(w, m, v, g1, g2=None):
    shape = w.shape
    C = shape[-1]
    R = int(np.prod(shape[:-1])) if len(shape) > 1 else 1
    tr = _row_tile(R, C)
    ins = [a.reshape(R, C) for a in ((w, m, v, g1) if g2 is None else (w, m, v, g1, g2))]

    def body(*refs):
        w_ref, m_ref, v_ref = refs[0], refs[1], refs[2]
        g_ref, d_ref, m2_ref, v2_ref = refs[-4:]
        g = refs[3][...] if g2 is None else refs[3][...] + refs[4][...]
        m2 = ADAM_B1 * m_ref[...] + (1.0 - ADAM_B1) * g
        v2 = ADAM_B2 * v_ref[...] + (1.0 - ADAM_B2) * (g * g)
        m_hat = m2 / (1.0 - ADAM_B1 ** ADAM_STEP)
        v_hat = v2 / (1.0 - ADAM_B2 ** ADAM_STEP)
        g_ref[...] = g
        d_ref[...] = -ADAM_LR * (m_hat / (jnp.sqrt(v_hat) + ADAM_EPS) + ADAM_WD * w_ref[...])
        m2_ref[...] = m2
        v2_ref[...] = v2

    spec = pl.BlockSpec((tr, C), lambda i: (i, 0))
    outs = pl.pallas_call(
        body, grid=(R // tr,), name="adamw", in_specs=[spec] * len(ins), out_specs=[spec] * 4,
        out_shape=[jax.ShapeDtypeStruct((R, C), F32)] * 4, compiler_params=_cp("parallel"))(*ins)
    return [o.reshape(shape) for o in outs]


def _adamw_layer(w, m, v, gs, l, prev):
    L, a, b = w.shape
    tr = _row_tile(a, b)
    n_in = 3 + len(gs)

    def body(*refs):
        g_ref, d_ref, m2_ref, v2_ref = refs[-4:]
        g = refs[3][...]
        for r in refs[4:3 + len(gs)]:
            g = g + r[...]
        m2 = ADAM_B1 * refs[1][...] + (1.0 - ADAM_B1) * g
        v2 = ADAM_B2 * refs[2][...] + (1.0 - ADAM_B2) * (g * g)
        m_hat = m2 / (1.0 - ADAM_B1 ** ADAM_STEP)
        v_hat = v2 / (1.0 - ADAM_B2 ** ADAM_STEP)
        g_ref[...] = g
        d_ref[...] = -ADAM_LR * (m_hat / (jnp.sqrt(v_hat) + ADAM_EPS) + ADAM_WD * refs[0][...])
        m2_ref[...] = m2
        v2_ref[...] = v2

    layer = pl.BlockSpec((None, tr, b), lambda i: (l, i, 0))
    outs = pl.pallas_call(
        body, grid=(a // tr,), name="adamw_layer",
        in_specs=[layer] * 3 + [pl.BlockSpec((tr, b), lambda i: (i, 0))] * len(gs) + [ANY] * len(prev),
        out_specs=[layer] * 4, out_shape=[jax.ShapeDtypeStruct((L, a, b), F32)] * 4,
        input_output_aliases={n_in + k: k for k in range(len(prev))},
        compiler_params=_cp("parallel"))(w, m, v, *gs, *prev)
    return list(outs)


def _sum_lead(x, name):
    n, R, C = x.shape
    tr = _row_tile(R, C * n)

    def body(x_ref, o_ref):
        acc = x_ref[0].astype(F32)
        for s in range(1, n):
            acc = acc + x_ref[s].astype(F32)
        o_ref[...] = acc

    return pl.pallas_call(
        body, grid=(R // tr,), name=name, in_specs=[pl.BlockSpec((n, tr, C), lambda i: (0, i, 0))],
        out_specs=pl.BlockSpec((tr, C), lambda i: (i, 0)), out_shape=jax.ShapeDtypeStruct((R, C), F32),
        compiler_params=_cp("parallel"))(x)


def _silu(x):
    return x * _sigmoid(x)


def _mod_fwd(a_raw, w_mod, bsh):
    L, D, n = w_mod.shape

    def body(a_ref, w_ref, b_ref, o_ref):
        o_ref[...] = _dot(_silu(a_ref[...]).astype(BF16), w_ref[...].astype(BF16)) + b_ref[...]

    return pl.pallas_call(
        body, grid=(L,), name="mod_fwd",
        in_specs=[_full(a_raw.shape), pl.BlockSpec((None, D, n), lambda l: (l, 0, 0)),
                  pl.BlockSpec((None, 1, n), lambda l: (l, 0, 0))],
        out_specs=pl.BlockSpec((None, 16, n), lambda l: (l, 0, 0)),
        out_shape=jax.ShapeDtypeStruct((L, 16, n), F32), compiler_params=_cp("parallel"))(a_raw, w_mod, bsh)


def _wmod_grad(a_raw, dms):
    L, _, n = dms.shape
    D = a_raw.shape[1]

    def body(a_ref, dm_ref, o_ref):
        o_ref[...] = _dg(_silu(a_ref[...]).astype(BF16), dm_ref[...].astype(BF16), TN)

    return pl.pallas_call(
        body, grid=(L,), name="wmod_grad",
        in_specs=[_full(a_raw.shape), pl.BlockSpec((None, 16, n), lambda l: (l, 0, 0))],
        out_specs=pl.BlockSpec((None, D, n), lambda l: (l, 0, 0)),
        out_shape=jax.ShapeDtypeStruct((L, D, n), F32), compiler_params=_cp("parallel"))(a_raw, dms)


def _cctx_partial(dmc, w_mod):
    L, D, n = w_mod.shape

    def body(dm_ref, w_ref, o_ref):
        part = _dg(dm_ref[...].astype(BF16), w_ref[...].astype(BF16), NT)

        @pl.when(pl.program_id(0) == 0)
        def _():
            o_ref[...] = part

        @pl.when(pl.program_id(0) > 0)
        def _():
            o_ref[...] += part

    return pl.pallas_call(
        body, grid=(L,), name="cctx_partial",
        in_specs=[pl.BlockSpec((None, 16, n), lambda l: (l, 0, 0)), pl.BlockSpec((None, D, n), lambda l: (l, 0, 0))],
        out_specs=_full((16, D)), out_shape=jax.ShapeDtypeStruct((16, D), F32),
        compiler_params=_cp("arbitrary"))(dmc, w_mod)


def _cctx_final(parts, cc):
    def body(p_ref, c_ref, o_ref):
        s = p_ref[0, 0:8, :]
        for j in range(1, N_CHIPS):
            s = s + p_ref[2 * j, 0:8, :]
        xv = c_ref[...]
        sg = _sigmoid(xv)
        o_ref[...] = s * (sg * (1.0 + xv * (1.0 - sg)))

    return pl.pallas_call(
        body, name="cctx_final", in_specs=[_full(parts.shape), _full(cc.shape)], out_specs=_full((8, LANES)),
        out_shape=jax.ShapeDtypeStruct((8, LANES), F32), compiler_params=_cp())(parts, cc)


def _me():
    return lax.axis_index("x"), lax.axis_index("y"), lax.axis_index("c")


def _flip(v, bit):
    return 1 - v if bit else v


def _remote(src, dst, ssem, rsem, peer):
    return pltpu.make_async_remote_copy(src_ref=src, dst_ref=dst, send_sem=ssem, recv_sem=rsem,
                                        device_id=peer, device_id_type=MESH_ID)


def _ag8(xb, name):
    R = xb.shape[0]

    def body(x_ref, o_ref, ssem, rsem, lsem):
        mx, my, mc = _me()
        me = 4 * mx + 2 * my + mc
        loc = pltpu.make_async_copy(x_ref, o_ref.at[me], lsem.at[0])
        loc.start()
        sends = []
        for k in range(1, N_DEV):
            peer = (_flip(mx, k & 4), _flip(my, k & 2), _flip(mc, k & 1))
            cp = _remote(x_ref, o_ref.at[me], ssem.at[k - 1], rsem.at[k - 1], peer)
            cp.start()
            sends.append((cp, peer))
        for k, (cp, peer) in enumerate(sends):
            pid = 4 * peer[0] + 2 * peer[1] + peer[2]
            _remote(x_ref, o_ref.at[pid], ssem.at[k], rsem.at[k], peer).wait_recv()
        for cp, _ in sends:
            cp.wait_send()
        loc.wait()

    return pl.pallas_call(
        body, name=name, in_specs=[ANY], out_specs=ANY, out_shape=jax.ShapeDtypeStruct((N_DEV, R, LANES), F32),
        scratch_shapes=[pltpu.SemaphoreType.DMA((N_DEV - 1,)), pltpu.SemaphoreType.DMA((N_DEV - 1,)),
                        pltpu.SemaphoreType.DMA((1,))])(xb)


def _plane_peers(mx, my, mc):
    out = []
    for k in range(1, N_CHIPS):
        px, py = _flip(mx, k & 2), _flip(my, k & 1)
        out.append(((px, py, mc), 2 * px + py))
    return out


def _plane_exchange(ins, outs, ssem, rsem, scatter):
    n = len(ins)

    def desc(k, a, arriving):
        mx, my, mc = _me()
        j = 2 * mx + my
        peer, pj = _plane_peers(mx, my, mc)[k]
        src = ins[a].at[pj if scatter else j]
        dst = outs[a].at[pj if arriving else j]
        return _remote(src, dst, ssem.at[k * n + a], rsem.at[k * n + a], peer)

    def start():
        for k in range(N_CHIPS - 1):
            for a in range(n):
                desc(k, a, False).start()

    def wait():
        for k in range(N_CHIPS - 1):
            for a in range(n):
                desc(k, a, True).wait_recv()
        for k in range(N_CHIPS - 1):
            for a in range(n):
                desc(k, a, False).wait_send()

    return start, wait


def _chip_gather(bufs, name):
    n = len(bufs)
    halves = [b.shape[1] // 2 for b in bufs]

    def body(*refs):
        outs = refs[n:2 * n]
        ssem, rsem, fsem, gsem = refs[2 * n:]
        mx, my, mc = _me()
        j = 2 * mx + my
        sib = (mx, my, 1 - mc)

        def half(a, blk, c):
            return outs[a].at[blk, pl.ds(c * halves[a], halves[a]), :]

        peers = _plane_peers(mx, my, mc)
        sends = []
        for k, (peer, _) in enumerate(peers):
            for a in range(n):
                mine = half(a, j, mc)
                cp = _remote(mine, mine, ssem.at[k * n + a], rsem.at[k * n + a], peer)
                cp.start()
                sends.append(cp)
        for k, (peer, pj) in enumerate(peers):
            for a in range(n):
                got = half(a, pj, mc)
                _remote(got, got, ssem.at[k * n + a], rsem.at[k * n + a], peer).wait_recv()
                fw = _remote(got, got, fsem.at[k * n + a], gsem.at[k * n + a], sib)
                fw.start()
                sends.append(fw)
        for k, (_, pj) in enumerate(peers):
            for a in range(n):
                theirs = half(a, pj, 1 - mc)
                _remote(theirs, theirs, fsem.at[k * n + a], gsem.at[k * n + a], sib).wait_recv()
        for cp in sends:
            cp.wait_send()

    sems = pltpu.SemaphoreType.DMA((3 * n,))
    return pl.pallas_call(
        body, name=name, in_specs=[ANY] * n, out_specs=[ANY] * n,
        out_shape=[jax.ShapeDtypeStruct(b.shape, b.dtype) for b in bufs],
        input_output_aliases={a: a for a in range(n)},
        scratch_shapes=[sems, sems, sems, sems])(*bufs)


def _chip_scatter(gs, name):
    n = len(gs)

    def body(*refs):
        ins, outs = refs[:n], refs[n:2 * n]
        ssem, rsem = refs[2 * n:]
        mx, my, mc = _me()
        j = 2 * mx + my
        peers = _plane_peers(mx, my, mc)
        sends = []
        for k, (peer, pj) in enumerate(peers):
            for a in range(n):
                cp = _remote(ins[a].at[pj], outs[a].at[j], ssem.at[k * n + a], rsem.at[k * n + a], peer)
                cp.start()
                sends.append(cp)
        for k, (peer, pj) in enumerate(peers):
            for a in range(n):
                _remote(ins[a].at[pj], outs[a].at[pj], ssem.at[k * n + a], rsem.at[k * n + a], peer).wait_recv()
        for cp in sends:
            cp.wait_send()

    return pl.pallas_call(
        body, name=name, in_specs=[ANY] * n, out_specs=[ANY] * n,
        out_shape=[jax.ShapeDtypeStruct(g.shape, g.dtype) for g in gs],
        scratch_shapes=[pltpu.SemaphoreType.DMA((3 * n,)), pltpu.SemaphoreType.DMA((3 * n,))])(*gs)


def _sibling_swap(xs, name):
    n = len(xs)

    def body(*refs):
        ins, outs = refs[:n], refs[n:2 * n]
        ssem, rsem = refs[2 * n:]
        mx, my, mc = _me()
        cps = [_remote(ins[a], outs[a], ssem.at[a], rsem.at[a], (mx, my, 1 - mc)) for a in range(n)]
        for cp in cps:
            cp.start()
        for cp in cps:
            cp.wait()

    return pl.pallas_call(
        body, name=name, in_specs=[ANY] * n, out_specs=[ANY] * n,
        out_shape=[jax.ShapeDtypeStruct(x.shape, x.dtype) for x in xs],
        scratch_shapes=[pltpu.SemaphoreType.DMA((n,)), pltpu.SemaphoreType.DMA((n,))])(*xs)


def _sibling_halves(gs, name):
    n = len(gs)

    def body(*refs):
        ins, outs = refs[:n], refs[n:2 * n]
        ssem, rsem = refs[2 * n:]
        mx, my, mc = _me()
        cps = []
        for a in range(n):
            h = gs[a].shape[1] // 2
            cps.append(_remote(ins[a].at[:, pl.ds((1 - mc) * h, h), :], outs[a], ssem.at[a], rsem.at[a],
                               (mx, my, 1 - mc)))
        for cp in cps:
            cp.start()
        for cp in cps:
            cp.wait()

    return pl.pallas_call(
        body, name=name, in_specs=[ANY] * n, out_specs=[ANY] * n,
        out_shape=[jax.ShapeDtypeStruct((g.shape[0], g.shape[1] // 2, g.shape[2]), g.dtype) for g in gs],
        scratch_shapes=[pltpu.SemaphoreType.DMA((n,)), pltpu.SemaphoreType.DMA((n,))])(*gs)


def _sibling_fill(hs, name):
    n = len(hs)

    def body(*refs):
        ins, outs = refs[:n], refs[n:2 * n]
        ssem, rsem = refs[2 * n:]
        mx, my, mc = _me()
        cps = []
        for a in range(n):
            h = hs[a].shape[0]
            cps.append(_remote(ins[a], outs[a].at[pl.ds(mc * h, h), :], ssem.at[a], rsem.at[a], (mx, my, 1 - mc)))
        for cp in cps:
            cp.start()
        for a, cp in enumerate(cps):
            h = hs[a].shape[0]
            theirs = outs[a].at[pl.ds((1 - mc) * h, h), :]
            _remote(ins[a], theirs, ssem.at[a], rsem.at[a], (mx, my, 1 - mc)).wait_recv()
            cp.wait_send()

    return pl.pallas_call(
        body, name=name, in_specs=[ANY] * n, out_specs=[ANY] * n,
        out_shape=[jax.ShapeDtypeStruct((2 * x.shape[0], x.shape[1]), x.dtype) for x in hs],
        scratch_shapes=[pltpu.SemaphoreType.DMA((n,)), pltpu.SemaphoreType.DMA((n,))])(*hs)


def _add_cast(g, sb):
    J, h, b = g.shape
    th = _row_tile(h, b * J)

    def body(g_ref, s_ref, o_ref):
        o_ref[...] = (g_ref[...].astype(F32) + s_ref[...].astype(F32)).astype(BF16)

    spec = pl.BlockSpec((J, th, b), lambda i: (0, i, 0))
    return pl.pallas_call(
        body, grid=(h // th,), name="add_planes", in_specs=[spec, spec], out_specs=spec,
        out_shape=jax.ShapeDtypeStruct((J, h, b), BF16), compiler_params=_cp("parallel"))(g, sb)


_WEIGHTS = ("c_ctx", "w_mod", "b_mod", "norm1", "w_in", "q_gain", "k_gain", "conv_w", "sg_norm", "w_s", "b_s",
            "w_a", "w_b", "w_c", "w_o", "norm2", "w_ff1", "w_ff3", "w_ff2")
_BIG = ("w_in", "w_a", "w_b", "w_c", "w_o", "w_ff1", "w_ff3", "w_ff2")


def _constants():
    idx = np.arange(LANES)
    e = (idx[:, None] // 64 == idx[None, :] // 64).astype(np.float32) / 64.0
    c512 = np.arange(512)
    fold = (c512[:, None] % 64 == idx[None, :]).astype(np.float32)
    c256 = np.arange(SG_W)
    gsum = (c256[:, None] // 64 == idx[None, :]).astype(np.float32)
    return jnp.asarray(e, BF16), jnp.asarray(fold, F32), jnp.asarray(gsum, F32)


def _rope_tables(n_lat, n_ctx):
    t = jnp.arange(n_lat)
    inv = ROPE_THETA ** (-jnp.arange(0, HEAD_DIM // 2, 2, dtype=F32) / (HEAD_DIM // 2))
    ar = (t // GRID_W).astype(F32)[:, None] * inv
    ac = (t % GRID_W).astype(F32)[:, None] * inv
    cos = jnp.concatenate([jnp.cos(ar), jnp.cos(ar), jnp.cos(ac), jnp.cos(ac)], axis=1)
    sin = jnp.concatenate([-jnp.sin(ar), jnp.sin(ar), -jnp.sin(ac), jnp.sin(ac)], axis=1)
    cos = jnp.concatenate([cos, jnp.ones((n_ctx, HEAD_DIM), F32)], axis=0)
    sin = jnp.concatenate([sin, jnp.zeros((n_ctx, HEAD_DIM), F32)], axis=0)
    return jnp.concatenate([cos, cos], axis=1), jnp.concatenate([sin, sin], axis=1)


def kernel(x, c, ctx, c_ctx, w_mod, b_mod, norm1, w_in, q_gain, k_gain, conv_w, sg_norm, w_s, b_s, w_a, w_b, w_c, w_o, norm2, w_ff1, w_ff3, w_ff2, loss_target, m_c_ctx, m_w_mod, m_b_mod, m_norm1, m_w_in, m_q_gain, m_k_gain, m_conv_w, m_sg_norm, m_w_s, m_b_s, m_w_a, m_w_b, m_w_c, m_w_o, m_norm2, m_w_ff1, m_w_ff3, m_w_ff2, v_c_ctx, v_w_mod, v_b_mod, v_norm1, v_w_in, v_q_gain, v_k_gain, v_conv_w, v_sg_norm, v_w_s, v_b_s, v_w_a, v_w_b, v_w_c, v_w_o, v_norm2, v_w_ff1, v_w_ff3, v_w_ff2):
    given = dict(locals())
    mx, my, mc = _me()
    chip = 2 * mx + my
    dev = 4 * mx + 2 * my + mc
    L = norm1.shape[0]
    S, Lc = x.shape[1], ctx.shape[1]
    T = S + Lc
    D = D_MODEL
    n_mod, n_in, n_ff = w_mod.shape[2], w_in.shape[2], w_ff1.shape[2]
    n_cw = conv_w.shape[2]
    e_avg, fold, gsum = _constants()
    cos_t, sin_t = _rope_tables(S, Lc)

    cw_rows = (L * 3 * n_cw) // LANES
    pad = (-(8 + cw_rows)) % 8
    buf = jnp.concatenate([c.reshape(8, LANES), conv_w.reshape(cw_rows, LANES), jnp.zeros((pad, LANES), F32)], axis=0)
    g1 = _ag8(buf, "gather_cond")
    conds = g1[:, :8].reshape(N_DEV, D)
    cw_full = jnp.stack([g1[2 * j, 8:8 + cw_rows].reshape(L, 3, n_cw) for j in range(N_CHIPS)], axis=2)
    cw_full = cw_full.reshape(L, 3, N_CHIPS * n_cw)
    cw8 = jnp.pad(cw_full, ((0, 0), (0, 5), (0, 0)))
    a_raw = jnp.concatenate([conds, c_ctx[None], jnp.zeros((7, D), F32)], axis=0)
    bsh = lax.dynamic_slice_in_dim(b_mod, chip * n_mod, n_mod, axis=1)[:, None, :]
    mod_sh = _mod_fwd(a_raw, w_mod, bsh)
    g2 = _ag8(mod_sh.reshape(-1, LANES), "gather_mod")
    mods = jnp.stack([g2[2 * j].reshape(L, 16, n_mod) for j in range(N_CHIPS)], axis=2).reshape(L, 16, N_CHIPS * n_mod)
    lat = lax.dynamic_index_in_dim(mods, dev, axis=1, keepdims=False)
    mod = jnp.stack([lat.reshape(L, 6, D), mods[:, 8].reshape(L, 6, D)], axis=1)
    mod = jnp.pad(mod, ((0, 0), (0, 0), (0, 2), (0, 0)))

    qg = jnp.tile(q_gain, (1, N_Q_HEADS))[:, None, :]
    kg = jnp.tile(k_gain, (1, N_KV_HEADS))[:, None, :]
    sgn = sg_norm[:, None, :]
    ws_b = w_s.astype(BF16)
    zero = jnp.zeros_like(ws_b)
    bd = jnp.concatenate([jnp.concatenate([ws_b, zero], axis=3), jnp.concatenate([zero, ws_b], axis=3)], axis=2)
    bdt = jnp.swapaxes(bd, 2, 3)
    bias = jnp.tile(jnp.repeat(jnp.swapaxes(b_s, 1, 2), SG_W // 4, axis=2), (1, 2, 1))

    def shard_bufs(l):
        return [lax.dynamic_update_slice(lax.empty((N_CHIPS,) + given[nm].shape[1:], BF16),
                                         given[nm][l].astype(BF16)[None], (chip, 0, 0)) for nm in _BIG]

    def unpack(bufs):
        win, wa, wb, wc, wo, w1, w3, w2 = bufs
        return win, wa, wb, wc, wo.reshape(1, D, D), w1, w3, w2

    def layer_fwd(X, l, W, nxt):
        win, wa, wb, wc, wo, w1, w3, w2 = W
        h, ht = _norm_mod(X, norm1[l][None], mod[l], 0, 1, S)
        p = _mm_nn(h, win, F32, "in_proj")
        ya = _conv_fwd(p, cw8[l], S)
        q, k, v = _qkv_prep(p, cos_t, sin_t, qg[l], kg[l], e_avg)
        at, qa, *got = _flash_fwd(q.reshape(N_KV_HEADS, GROUP, T, LANES), k, v, S, nxt)
        yc = _gmlp_fwd(p, sgn[l], bd[l], bias[l])
        mg = _merge_fwd(ya, at, yc, p, wa, wb, wc)
        X1, f1 = _mm_res(mg[None], wo, X, mod[l], 2, S, "out_proj")
        h2, h2t = _norm_mod(X1, norm2[l][None], mod[l], 3, 4, S)
        a1, a3, act = _ffn_up(h2, w1, w3)
        X2, f2 = _mm_res(act, w2, X1, mod[l], 5, S, "ffn_down")
        return X2, got, dict(X=X, ht=ht, h2t=h2t, p=p, ya=ya, k=k, v=v, at=at, qa=qa, yc=yc, mg=mg, X1=X1, f1=f1,
                             a1=a1, a3=a3, act=act, f2=f2)

    def layer_bwd(dX2, l, W, sv, pending):
        win, wa, wb, wc, wo, w1, w3, w2 = W
        dyf, dgt2 = _gate_bwd(dX2, sv["f2"], mod[l], 5, S)
        da1, da3 = _ffn_down_bwd(dyf, w2, sv["a1"], sv["a3"])
        dw2 = _mm_tn(sv["act"], dyf, _shard_rows(n_ff), _rows(D), N_CHIPS, n_ff, D, T, "dw_ff2")
        dh2 = _mm_nt_acc([da1, da3], [w1, w3], False, "ffn_up_bwd")
        dw1 = _mm_dw(sv["h2t"], da1, _shard_rows(n_ff), N_CHIPS, n_ff, "dw_ff1")
        dw3 = _mm_dw(sv["h2t"], da3, _shard_rows(n_ff), N_CHIPS, n_ff, "dw_ff3")
        dX1, dn2, dsh2, dsc2 = _norm_mod_bwd(sv["X1"], dh2, dX2, norm2[l][None], mod[l], 4, S)
        dyo, dgt1 = _gate_bwd(dX1, sv["f1"], mod[l], 2, S)
        dwo = _mm_tn(sv["mg"], dyo, _rows(D), _rows(D), 1, D, D, T, "dw_o")
        dp, dya, doa, dyc, dwa, dwb, dwc = _merge_bwd(dyo, sv["ya"], sv["at"], sv["yc"], sv["p"], wa, wb, wc, wo[0])
        dp, dcw = _conv_bwd(dp, dya, sv["p"], cw8[l], S)
        dp, dsg, dws, dbs = _gmlp_bwd(dp, dyc, sv["p"], sgn[l], bd[l], bdt[l], bias[l], gsum)
        dq, dk, dv, *recv = _flash_bwd(sv["qa"], doa.reshape(N_KV_HEADS, GROUP, T, LANES), sv["k"], sv["v"], S,
                                       pending)
        dp, dqg, dkg = _qkv_prep_bwd(dp, dq, dk, dv, sv["p"], cos_t, sin_t,
                                     qg[l], kg[l], e_avg, fold)
        dh = _mm_nt_acc([dp], [win], True, "in_proj_bwd")
        dwin = _mm_dw(sv["ht"], dp, _row_cols(n_in), N_CHIPS, n_in, "dw_in")
        dX0, dn1, dsh1, dsc1 = _norm_mod_bwd(sv["X"], dh, dX1, norm1[l][None], mod[l], 1, S)
        dmod = jnp.concatenate([dsh1, dsc1, dgt1, dsh2, dsc2, dgt2], axis=1)
        big = [dwin, dwa.astype(BF16), dwb.astype(BF16), dwc.astype(BF16),
               dwo.reshape(N_CHIPS, D // N_CHIPS, D), dw1, dw3, dw2]
        small = dict(norm1=dn1[0], norm2=dn2[0], q_gain=dqg[0, :HEAD_DIM], k_gain=dkg[0, :HEAD_DIM],
                     conv_w=dcw[:3], sg_norm=dsg[0], w_s=dws, b_s=jnp.swapaxes(dbs[:, :4], 0, 1), dmod=dmod)
        return dX0, big, small, recv

    X = jnp.concatenate([x[0], ctx[0]], axis=0)
    Ws, saved = [unpack(_chip_gather(shard_bufs(0), "gather_weights"))], []
    for l in range(L):
        X, got, sv = layer_fwd(X, l, Ws[l], shard_bufs(l + 1) if l + 1 < L else ())
        if got:
            Ws.append(unpack(got))
        saved.append(sv)
    dX, lpart = _loss_grad(X, loss_target[0], S)
    loss = lax.psum(lpart[0, 0], ("x", "y", "c"))

    out = {nm: () for nm in _BIG}
    smalls = [None] * L

    def own_block(r, g):
        return lax.dynamic_update_slice(r, lax.dynamic_slice_in_dim(g, chip, 1, axis=0), (chip, 0, 0))

    def update(l, grads):
        for nm, g in zip(_BIG, grads):
            out[nm] = _adamw_layer(given[nm], given["m_" + nm], given["v_" + nm], g, l, out[nm])

    pending = ()
    for l in reversed(range(L)):
        dX, big, smalls[l], recv = layer_bwd(dX, l, Ws[l], saved[l], pending)
        if recv:
            mine = [_sum_lead(own_block(r, g), "sum_chips") for r, g in zip(recv, pending)]
            update(l + 1, zip(mine, _sibling_swap(mine, "swap_planes")))
        pending = big
    sib = _sibling_halves(pending, "swap_halves")
    own = [lax.dynamic_slice_in_dim(g, mc * (g.shape[1] // 2), g.shape[1] // 2, axis=1) for g in big]
    sent = [_add_cast(g, s_) for g, s_ in zip(own, sib)]
    recv = [own_block(r, g) for r, g in zip(_chip_scatter(sent, "scatter_grads"), sent)]
    halves = [_sum_lead(r, "sum_chips") for r in recv]
    full = _sibling_fill(halves, "fill_halves")
    update(0, [(lax.dynamic_update_slice(f, hv, (mc * hv.shape[0], 0)),) for f, hv in zip(full, halves)])
    grad_x = dX[:S][None]

    def flat(nm):
        return jnp.stack([smalls[l][nm] for l in range(L)]).reshape(-1)

    dmod_all = jnp.stack([smalls[l]["dmod"] for l in range(L)])
    dml = dmod_all[:, 0].reshape(-1)
    dmc = dmod_all[:, 1].reshape(-1)
    names = ("norm1", "q_gain", "k_gain", "conv_w", "sg_norm", "w_s", "b_s", "norm2")
    parts = [dml, dml + dmc, dmc] + [flat(nm) for nm in names]
    sizes = [int(a.shape[0]) for a in parts]
    total = sum(sizes)
    padn = (-total) % (8 * LANES)
    sbuf = jnp.concatenate(parts + [jnp.zeros((padn,), F32)]).reshape(-1, LANES)
    g3 = _ag8(sbuf, "gather_small")
    ssum = _sum_lead(g3, "sum_devices").reshape(-1)
    offs = np.cumsum([0] + sizes)
    seg = {nm: ssum[offs[i + 3]:offs[i + 4]] for i, nm in enumerate(names)}
    gb_mod = ssum[offs[1]:offs[2]].reshape(L, N_CHIPS * n_mod)
    dmc_sum = ssum[offs[2]:offs[3]].reshape(L, N_CHIPS * n_mod)
    dml_all = g3.reshape(N_DEV, -1)[:, :sizes[0]].reshape(N_DEV, L, N_CHIPS * n_mod)
    dml_sh = jnp.swapaxes(lax.dynamic_slice_in_dim(dml_all, chip * n_mod, n_mod, axis=2), 0, 1)
    dmc_sh = lax.dynamic_slice_in_dim(dmc_sum, chip * n_mod, n_mod, axis=1)[:, None, :]
    dms = jnp.concatenate([dml_sh, dmc_sh, jnp.zeros((L, 7, n_mod), F32)], axis=1)
    g_wmod = _wmod_grad(a_raw, dms)
    part = _cctx_partial(jnp.concatenate([dmc_sh, jnp.zeros((L, 15, n_mod), F32)], axis=1), w_mod)
    g4 = _ag8(part.reshape(-1, LANES), "gather_cctx")
    g_cctx = _cctx_final(g4, c_ctx.reshape(8, LANES)).reshape(D)

    g_conv = lax.dynamic_slice_in_dim(seg["conv_w"].reshape(L, 3, N_CHIPS * n_cw), chip * n_cw, n_cw, axis=2)
    small_g = dict(c_ctx=g_cctx, w_mod=g_wmod, b_mod=gb_mod, norm1=seg["norm1"].reshape(norm1.shape),
                   q_gain=seg["q_gain"].reshape(q_gain.shape), k_gain=seg["k_gain"].reshape(k_gain.shape),
                   conv_w=g_conv, sg_norm=seg["sg_norm"].reshape(sg_norm.shape), w_s=seg["w_s"].reshape(w_s.shape),
                   b_s=seg["b_s"].reshape(b_s.shape), norm2=seg["norm2"].reshape(norm2.shape))
    res = {}
    for nm in _WEIGHTS:
        if nm in _BIG:
            res[nm] = out[nm]
        else:
            res[nm] = _adamw(given[nm], given["m_" + nm], given["v_" + nm], small_g[nm])
    return (loss, grad_x, *[res[nm][0] for nm in _WEIGHTS], *[res[nm][1] for nm in _WEIGHTS],
            *[res[nm][2] for nm in _WEIGHTS], *[res[nm][3] for nm in _WEIGHTS])
```

```python
import functools

import jax
import jax.numpy as jnp
import numpy as np
from jax import lax
from jax.experimental import pallas as pl
from jax.experimental.pallas import tpu as pltpu

F32 = jnp.float32
BF16 = jnp.bfloat16
EPS = 1e-6
D_MODEL = 1024
HEAD_DIM = 64
N_Q_HEADS = 8
N_KV_HEADS = 2
GROUP = N_Q_HEADS // N_KV_HEADS
GRID_W = 64
ROPE_THETA = 10000.0
CHUNK = 128
CONV_W = 256
SG_W = 256
OFF_Q = 3 * CONV_W
QKV_W = 768
OFF_U = OFF_Q + QKV_W
OFF_G = OFF_U + 2 * SG_W
IN_W = OFF_G + 3 * D_MODEL
N_CHIPS = 4
N_DEV = 8
LANES = 128
FWD_KEYS = 256
UNROLL_FWD = 8
UNROLL_BWD = 4
AUG = 3
ADAM_LR, ADAM_B1, ADAM_B2, ADAM_EPS, ADAM_WD, ADAM_STEP = 0.001, 0.9, 0.999, 1e-8, 0.01, 10
VMEM_LIMIT_V7X = 52 * 1024 * 1024
MXU_DEPTH_V7X = 256
MESH_ID = pl.DeviceIdType.MESH
NT = (((1,), (1,)), ((), ()))
TN = (((0,), (0,)), ((), ()))
ANY = pl.BlockSpec(memory_space=pl.ANY)


def _cp(*sem):
    return pltpu.CompilerParams(dimension_semantics=sem or None, vmem_limit_bytes=VMEM_LIMIT_V7X)


def _tile(n, target, mult=16):
    best = None
    for t in range(mult, n + 1, mult):
        if n % t == 0 and t <= target:
            best = t
    assert best is not None, (n, target, mult)
    return best


def _full(shape):
    nd = len(shape)
    return pl.BlockSpec(tuple(shape), lambda *_: (0,) * nd)


def _segments(i, tm, n_lat, fn):
    k, off = divmod(n_lat, tm)

    @pl.when(i < k)
    def _():
        fn(0, tm, 0)

    @pl.when(i == k)
    def _():
        if off:
            fn(0, off, 0)
        fn(off, tm, 1)

    @pl.when(i > k)
    def _():
        fn(0, tm, 1)


def _dot(a, b):
    return jnp.dot(a, b, preferred_element_type=F32)


def _dg(a, b, dims):
    return lax.dot_general(a, b, dims, preferred_element_type=F32)


def _split3(x):
    hi = x.astype(BF16)
    r1 = x - hi.astype(F32)
    mid = r1.astype(BF16)
    lo = (r1 - mid.astype(F32)).astype(BF16)
    return hi.astype(F32), mid.astype(F32), lo.astype(F32)


def _lane(shape):
    return lax.broadcasted_iota(jnp.int32, shape, len(shape) - 1)


def _aug(val, stat):
    lane = _lane(val.shape)
    hi, mid, lo = _split3(stat)
    ext = jnp.where(lane == 64, hi, jnp.where(lane == 65, mid, jnp.where(lane == 66, lo, 0.0)))
    return jnp.where(lane < 64, val, ext)


def _seg_mean(x, e):
    outs = []
    for g in range(x.shape[1] // LANES):
        blk = x[:, g * LANES:(g + 1) * LANES]
        hi = blk.astype(BF16)
        lo = (blk - hi.astype(F32)).astype(BF16)
        outs.append(_dot(hi, e) + _dot(lo, e))
    return outs[0] if len(outs) == 1 else jnp.concatenate(outs, axis=1)


def _rope(x, cos, sin_signed, inverse):
    w = x.shape[1]
    reps = w // LANES
    c = cos if reps == 1 else jnp.tile(cos, (1, reps))
    s = sin_signed if reps == 1 else jnp.tile(sin_signed, (1, reps))
    first = (_lane(x.shape) % 32) < 16
    partner = jnp.where(first, pltpu.roll(x, w - 16, 1), pltpu.roll(x, 16, 1))
    return x * c - partner * s if inverse else x * c + partner * s


def _sigmoid(x):
    return 1.0 / (1.0 + jnp.exp(-x))


_GELU_K = 0.7978845608028654
_GELU_C = 0.044715


def _gelu(x):
    return 0.5 * x * (1.0 + jnp.tanh(_GELU_K * (x + _GELU_C * x * x * x)))


def _gelu_grad(x):
    t = jnp.tanh(_GELU_K * (x + _GELU_C * x * x * x))
    return 0.5 * (1.0 + t) + 0.5 * x * (1.0 - t * t) * _GELU_K * (1.0 + 3.0 * _GELU_C * x * x)


def _loop_unrolled(n, step, init, unroll):
    def trip(t, carry):
        for u in range(unroll):
            carry = step(t * unroll + u, carry)
        return carry

    carry = lax.fori_loop(0, n // unroll, trip, init) if n >= unroll else init
    for r in range(n - n % unroll, n):
        carry = step(r, carry)
    return carry


def _heads_to_rows(x, n_heads):
    out = []
    for h in range(n_heads):
        grp = x[:, (h // 2) * LANES:(h // 2 + 1) * LANES]
        out.append(grp if h % 2 == 0 else pltpu.roll(grp, 64, 1))
    return out


def _rows_to_heads(blocks):
    outs = []
    lane = _lane(blocks[0].shape)
    for a in range(len(blocks) // 2):
        outs.append(jnp.where(lane < 64, blocks[2 * a], pltpu.roll(blocks[2 * a + 1], 64, 1)))
    return outs[0] if len(outs) == 1 else jnp.concatenate(outs, axis=1)


def _norm_mod(x, g, mod, i_shift, i_scale, n_lat):
    T, D = x.shape
    tm = _tile(T, 768, LANES)

    def body(x_ref, g_ref, mod_ref, h_ref, ht_ref):
        def fn(r0, r1, seg):
            xv = x_ref[r0:r1, :]
            r = lax.rsqrt(jnp.mean(xv * xv, axis=-1, keepdims=True) + EPS)
            n = xv * r * g_ref[...]
            h = n * (1.0 + mod_ref[seg, i_scale:i_scale + 1, :]) + mod_ref[seg, i_shift:i_shift + 1, :]
            h_ref[r0:r1, :] = h.astype(BF16)

        _segments(pl.program_id(0), tm, n_lat, fn)
        ht_ref[...] = h_ref[...].astype(F32).T.astype(BF16)

    return pl.pallas_call(
        body, grid=(T // tm,), name="norm_mod",
        in_specs=[pl.BlockSpec((tm, D), lambda i: (i, 0)), _full(g.shape), _full(mod.shape)],
        out_specs=[pl.BlockSpec((tm, D), lambda i: (i, 0)), pl.BlockSpec((D, tm), lambda i: (0, i))],
        out_shape=[jax.ShapeDtypeStruct((T, D), BF16), jax.ShapeDtypeStruct((D, T), BF16)],
        compiler_params=_cp("parallel"))(x, g, mod)


def _norm_mod_bwd(x, dh, dres, g, mod, i_scale, n_lat):
    T, D = x.shape
    tm = _tile(T, 528)

    def body(x_ref, dh_ref, dres_ref, g_ref, mod_ref, dx_ref, dg_ref, dsh_ref, dsc_ref):
        i = pl.program_id(0)

        @pl.when(i == 0)
        def _():
            dg_ref[...] = jnp.zeros_like(dg_ref)
            dsh_ref[...] = jnp.zeros_like(dsh_ref)
            dsc_ref[...] = jnp.zeros_like(dsc_ref)

        def fn(r0, r1, seg):
            xv = x_ref[r0:r1, :]
            dh = dh_ref[r0:r1, :]
            r = lax.rsqrt(jnp.mean(xv * xv, axis=-1, keepdims=True) + EPS)
            xh = xv * r
            gv = g_ref[...]
            dsh_ref[seg] += jnp.sum(dh, axis=0, keepdims=True)
            dsc_ref[seg] += jnp.sum(dh * (xh * gv), axis=0, keepdims=True)
            dn = dh * (1.0 + mod_ref[seg, i_scale:i_scale + 1, :])
            dg_ref[...] += jnp.sum(dn * xh, axis=0, keepdims=True)
            gd = gv * dn
            dx_ref[r0:r1, :] = dres_ref[r0:r1, :] + r * (gd - xh * jnp.mean(xh * gd, axis=-1, keepdims=True))

        _segments(i, tm, n_lat, fn)

    row = pl.BlockSpec((tm, D), lambda i: (i, 0))
    return pl.pallas_call(
        body, grid=(T // tm,), name="norm_mod_bwd",
        in_specs=[row, row, row, _full(g.shape), _full(mod.shape)],
        out_specs=[row, _full((1, D)), _full((2, 1, D)), _full((2, 1, D))],
        out_shape=[jax.ShapeDtypeStruct((T, D), F32), jax.ShapeDtypeStruct((1, D), F32),
                   jax.ShapeDtypeStruct((2, 1, D), F32), jax.ShapeDtypeStruct((2, 1, D), F32)],
        compiler_params=_cp("arbitrary"))(x, dh, dres, g, mod)


def _gate_bwd(dx, f, mod, i_gate, n_lat):
    T, D = dx.shape
    tm = _tile(T, 528)

    def body(dx_ref, f_ref, mod_ref, dy_ref, dg_ref):
        i = pl.program_id(0)

        @pl.when(i == 0)
        def _():
            dg_ref[...] = jnp.zeros_like(dg_ref)

        def fn(r0, r1, seg):
            dxv = dx_ref[r0:r1, :]
            dy_ref[r0:r1, :] = (dxv * mod_ref[seg, i_gate:i_gate + 1, :]).astype(BF16)
            dg_ref[seg] += jnp.sum(dxv * f_ref[r0:r1, :].astype(F32), axis=0, keepdims=True)

        _segments(i, tm, n_lat, fn)

    row = pl.BlockSpec((tm, D), lambda i: (i, 0))
    return pl.pallas_call(
        body, grid=(T // tm,), name="gate_bwd",
        in_specs=[row, row, _full(mod.shape)], out_specs=[row, _full((2, 1, D))],
        out_shape=[jax.ShapeDtypeStruct((T, D), BF16), jax.ShapeDtypeStruct((2, 1, D), F32)],
        compiler_params=_cp("arbitrary"))(dx, f, mod)


def _mm_nn(a, w, out_dtype, name):
    M, K = a.shape
    J, _, n = w.shape
    tm = _tile(M, 1056)

    def body(a_ref, w_ref, o_ref):
        o_ref[...] = _dot(a_ref[...], w_ref[...]).astype(o_ref.dtype)

    return pl.pallas_call(
        body, grid=(M // tm, J), name=name,
        in_specs=[pl.BlockSpec((tm, K), lambda i, j: (i, 0)), pl.BlockSpec((None, K, n), lambda i, j: (j, 0, 0))],
        out_specs=pl.BlockSpec((tm, n), lambda i, j: (i, j)),
        out_shape=jax.ShapeDtypeStruct((M, J * n), out_dtype), compiler_params=_cp("parallel", "arbitrary"))(a, w)


def _mm_res(a3, w, res, mod, i_gate, n_lat, name):
    J, M, k = a3.shape
    N = w.shape[2]
    tm = _tile(M, 528)

    def body(a_ref, w_ref, res_ref, mod_ref, x_ref, f_ref):
        acc = _dot(a_ref[0], w_ref[0])
        for j in range(1, J):
            acc += _dot(a_ref[j], w_ref[j])
        f_ref[...] = acc.astype(BF16)

        def fn(r0, r1, seg):
            x_ref[r0:r1, :] = res_ref[r0:r1, :] + mod_ref[seg, i_gate:i_gate + 1, :] * acc[r0:r1, :]

        _segments(pl.program_id(0), tm, n_lat, fn)

    row = pl.BlockSpec((tm, N), lambda i: (i, 0))
    return pl.pallas_call(
        body, grid=(M // tm,), name=name,
        in_specs=[pl.BlockSpec((J, tm, k), lambda i: (0, i, 0)), _full(w.shape), row, _full(mod.shape)],
        out_specs=[row, row],
        out_shape=[jax.ShapeDtypeStruct((M, N), F32), jax.ShapeDtypeStruct((M, N), BF16)],
        compiler_params=_cp("parallel"))(a3, w, res, mod)


def _mm_nt_acc(dys, ws, row_major, name):
    J, K, n = ws[0].shape
    M = dys[0].shape[0] if row_major else dys[0].shape[1]
    tm = _tile(M, 1056)
    P = len(dys)

    def body(*refs):
        o_ref = refs[2 * P]
        j = pl.program_id(1)
        part = _dg(refs[0][...], refs[P][...], NT)
        for p in range(1, P):
            part += _dg(refs[p][...], refs[P + p][...], NT)

        @pl.when(j == 0)
        def _():
            o_ref[...] = part

        @pl.when(j > 0)
        def _():
            o_ref[...] += part

    dy_spec = (pl.BlockSpec((tm, n), lambda i, j: (i, j)) if row_major
               else pl.BlockSpec((None, tm, n), lambda i, j: (j, i, 0)))
    w_spec = pl.BlockSpec((None, K, n), lambda i, j: (j, 0, 0))
    return pl.pallas_call(
        body, grid=(M // tm, J), name=name,
        in_specs=[dy_spec] * P + [w_spec] * P,
        out_specs=pl.BlockSpec((tm, K), lambda i, j: (i, 0)),
        out_shape=jax.ShapeDtypeStruct((M, K), F32), compiler_params=_cp("parallel", "arbitrary"))(*dys, *ws)


def _mm_tn(x, dy, x_spec, dy_spec, J, K, n, T, name):
    tk = _tile(T, 1056, MXU_DEPTH_V7X)
    nt = T // tk

    def body(x_ref, dy_ref, o_ref, acc):
        t = pl.program_id(1)
        part = _dg(x_ref[...], dy_ref[...], TN)

        @pl.when(t == 0)
        def _():
            acc[...] = part

        @pl.when(t > 0)
        def _():
            acc[...] += part

        @pl.when(t == nt - 1)
        def _():
            o_ref[...] = acc[...].astype(BF16)

    return pl.pallas_call(
        body, grid=(J, nt), name=name,
        in_specs=[x_spec(tk), dy_spec(tk)],
        out_specs=pl.BlockSpec((None, K, n), lambda j, t: (j, 0, 0)),
        out_shape=jax.ShapeDtypeStruct((J, K, n), BF16), scratch_shapes=[pltpu.VMEM((K, n), F32)],
        compiler_params=_cp("parallel", "arbitrary"))(x, dy)


def _mm_dw(xt, dy, dy_spec, J, n, name):
    K, T = xt.shape
    tk = _tile(T, 1056, MXU_DEPTH_V7X)
    nt = T // tk

    def body(xt_ref, dy_ref, o_ref, acc):
        t = pl.program_id(1)
        part = _dot(xt_ref[...], dy_ref[...])

        @pl.when(t == 0)
        def _():
            acc[...] = part

        @pl.when(t > 0)
        def _():
            acc[...] += part

        @pl.when(t == nt - 1)
        def _():
            o_ref[...] = acc[...].astype(BF16)

    return pl.pallas_call(
        body, grid=(J, nt), name=name,
        in_specs=[pl.BlockSpec((K, tk), lambda j, t: (0, t)), dy_spec(tk)],
        out_specs=pl.BlockSpec((None, K, n), lambda j, t: (j, 0, 0)),
        out_shape=jax.ShapeDtypeStruct((J, K, n), BF16), scratch_shapes=[pltpu.VMEM((K, n), F32)],
        compiler_params=_cp("parallel", "arbitrary"))(xt, dy)


def _rows(width):
    return lambda tk: pl.BlockSpec((tk, width), lambda j, t: (t, 0))


def _row_cols(width):
    return lambda tk: pl.BlockSpec((tk, width), lambda j, t: (t, j))


def _shard_rows(width):
    return lambda tk: pl.BlockSpec((None, tk, width), lambda j, t: (j, t, 0))


def _ffn_up(h, w1, w3):
    T, D = h.shape
    J, _, n = w1.shape
    tm = _tile(T, 1056)

    def body(h_ref, w1_ref, w3_ref, a1_ref, a3_ref, act_ref):
        hv = h_ref[...]
        a1 = _dot(hv, w1_ref[...])
        a3 = _dot(hv, w3_ref[...])
        a1_ref[...] = a1.astype(BF16)
        a3_ref[...] = a3.astype(BF16)
        act_ref[...] = (a1 * _sigmoid(a1) * a3).astype(BF16)

    w_spec = pl.BlockSpec((None, D, n), lambda i, j: (j, 0, 0))
    o_spec = pl.BlockSpec((None, tm, n), lambda i, j: (j, i, 0))
    return pl.pallas_call(
        body, grid=(T // tm, J), name="ffn_up",
        in_specs=[pl.BlockSpec((tm, D), lambda i, j: (i, 0)), w_spec, w_spec], out_specs=[o_spec] * 3,
        out_shape=[jax.ShapeDtypeStruct((J, T, n), BF16)] * 3,
        compiler_params=_cp("parallel", "arbitrary"))(h, w1, w3)


def _ffn_down_bwd(dy, w2, a1, a3):
    T, D = dy.shape
    J, n, _ = w2.shape
    tm = _tile(T, 1056)

    def body(dy_ref, w2_ref, a1_ref, a3_ref, da1_ref, da3_ref):
        dact = _dg(dy_ref[...], w2_ref[...], NT)
        a1v = a1_ref[...].astype(F32)
        sig = _sigmoid(a1v)
        da3_ref[...] = (dact * a1v * sig).astype(BF16)
        da1_ref[...] = (dact * a3_ref[...].astype(F32) * (sig * (1.0 + a1v * (1.0 - sig)))).astype(BF16)

    a_spec = pl.BlockSpec((None, tm, n), lambda i, j: (j, i, 0))
    return pl.pallas_call(
        body, grid=(T // tm, J), name="ffn_down_bwd",
        in_specs=[pl.BlockSpec((tm, D), lambda i, j: (i, 0)), pl.BlockSpec((None, n, D), lambda i, j: (j, 0, 0)),
                  a_spec, a_spec],
        out_specs=[a_spec, a_spec], out_shape=[jax.ShapeDtypeStruct((J, T, n), BF16)] * 2,
        compiler_params=_cp("parallel", "arbitrary"))(dy, w2, a1, a3)


def _qkv_prep(p, cos, sin, qg, kg, e):
    T = p.shape[0]
    tm = _tile(T, 528)

    def body(p_ref, cos_ref, sin_ref, qg_ref, kg_ref, e_ref, q_ref, k_ref, v_ref):
        ev = e_ref[...]
        cv, sv = cos_ref[...], sin_ref[...]
        xq = p_ref[:, 0:512]
        qn = xq * lax.rsqrt(_seg_mean(xq * xq, ev) + EPS) * qg_ref[...]
        qr = _rope(qn, cv, sv, False) * (HEAD_DIM ** -0.5)
        xk = p_ref[:, 512:640]
        kn = xk * lax.rsqrt(_seg_mean(xk * xk, ev) + EPS) * kg_ref[...]
        kr = _rope(kn, cv, sv, False)
        lane = _lane((tm, LANES))
        ones = jnp.where(lane < 64 + AUG, -1.0, 0.0)
        for h, blk in enumerate(_heads_to_rows(qr, N_Q_HEADS)):
            q_ref[h] = jnp.where(lane < 64, blk, 0.0).astype(BF16)
        for h, blk in enumerate(_heads_to_rows(kr, N_KV_HEADS)):
            k_ref[h] = jnp.where(lane < 64, blk, ones).astype(BF16)
        for h, blk in enumerate(_heads_to_rows(p_ref[:, 640:768], N_KV_HEADS)):
            v_ref[h] = jnp.where(lane < 64, blk, ones).astype(BF16)

    tab = pl.BlockSpec((tm, LANES), lambda i: (i, 0))
    return pl.pallas_call(
        body, grid=(T // tm,), name="qkv_prep",
        in_specs=[pl.BlockSpec((tm, QKV_W), lambda i: (i, 1)), tab, tab, _full(qg.shape), _full(kg.shape),
                  _full(e.shape)],
        out_specs=[pl.BlockSpec((N_Q_HEADS, tm, LANES), lambda i: (0, i, 0)),
                   pl.BlockSpec((N_KV_HEADS, tm, LANES), lambda i: (0, i, 0)),
                   pl.BlockSpec((N_KV_HEADS, tm, LANES), lambda i: (0, i, 0))],
        out_shape=[jax.ShapeDtypeStruct((N_Q_HEADS, T, LANES), BF16),
                   jax.ShapeDtypeStruct((N_KV_HEADS, T, LANES), BF16),
                   jax.ShapeDtypeStruct((N_KV_HEADS, T, LANES), BF16)],
        compiler_params=_cp("parallel"))(p, cos, sin, qg, kg, e)


def _qkv_prep_bwd(dp, dq, dk, dv, p, cos, sin, qg, kg, e, fold):
    T = p.shape[0]
    tq = dq.shape[3] // GROUP
    tm = _tile(T, 768, tq)
    nt = T // tm

    def body(dp_in, dq_ref, dk_ref, dv_ref, p_ref, cos_ref, sin_ref, qg_ref, kg_ref, e_ref, fold_ref,
             dp_ref, dqg_ref, dkg_ref, accq, acck):
        del dp_in
        i = pl.program_id(0)

        @pl.when(i == 0)
        def _():
            accq[...] = jnp.zeros_like(accq)
            acck[...] = jnp.zeros_like(acck)

        ev = e_ref[...]
        cv, sv = cos_ref[...], sin_ref[...]

        def one(x, dr, gain, acc):
            r = lax.rsqrt(_seg_mean(x * x, ev) + EPS)
            xh = x * r
            dn = _rope(dr, cv, sv, True)
            acc[0:1, :] += jnp.sum(dn * xh, axis=0, keepdims=True)
            gd = gain * dn
            return r * (gd - xh * _seg_mean(xh * gd, ev))

        slabs = [[dq_ref[h, b].T for b in range(tm // tq)] for h in range(N_KV_HEADS)]
        heads = [jnp.concatenate([sl[g * tq:(g + 1) * tq] for sl in slabs[h]], axis=0)
                 for h in range(N_KV_HEADS) for g in range(GROUP)]
        dqr = _rows_to_heads(heads) * (HEAD_DIM ** -0.5)
        dkr = _rows_to_heads([dk_ref[h] for h in range(N_KV_HEADS)])
        dvv = _rows_to_heads([dv_ref[h] for h in range(N_KV_HEADS)])
        dp_ref[:, 0:512] = one(p_ref[:, 0:512], dqr, qg_ref[...], accq).astype(BF16)
        dp_ref[:, 512:640] = one(p_ref[:, 512:640], dkr, kg_ref[...], acck).astype(BF16)
        dp_ref[:, 640:768] = dvv.astype(BF16)

        @pl.when(i == nt - 1)
        def _():
            fv = fold_ref[...]
            dqg_ref[...] = jnp.dot(accq[...], fv, preferred_element_type=F32, precision=lax.Precision.HIGHEST)
            dkg_ref[...] = jnp.dot(acck[...], fv[0:LANES, :], preferred_element_type=F32,
                                   precision=lax.Precision.HIGHEST)

    tab = pl.BlockSpec((tm, LANES), lambda i: (i, 0))
    sec = pl.BlockSpec((tm, QKV_W), lambda i: (i, 1))
    return pl.pallas_call(
        body, grid=(nt,), name="qkv_prep_bwd",
        in_specs=[ANY, pl.BlockSpec((N_KV_HEADS, tm // tq, LANES, GROUP * tq), lambda i: (0, i, 0, 0)),
                  pl.BlockSpec((N_KV_HEADS, tm, LANES), lambda i: (0, i, 0)),
                  pl.BlockSpec((N_KV_HEADS, tm, LANES), lambda i: (0, i, 0)),
                  sec, tab, tab, _full(qg.shape), _full(kg.shape), _full(e.shape), _full(fold.shape)],
        out_specs=[sec, _full((8, LANES)), _full((8, LANES))],
        out_shape=[jax.ShapeDtypeStruct(dp.shape, BF16), jax.ShapeDtypeStruct((8, LANES), F32),
                   jax.ShapeDtypeStruct((8, LANES), F32)],
        scratch_shapes=[pltpu.VMEM((8, 512), F32), pltpu.VMEM((8, LANES), F32)],
        input_output_aliases={0: 0}, compiler_params=_cp("arbitrary"))(dp, dq, dk, dv, p, cos, sin, qg, kg, e, fold)


def _flash_fwd(q, k, v, n_lat, gather=()):
    _, _, T, _ = q.shape
    tq = tk = 256
    nq = T // tq
    M = GROUP * tq

    wide_k = FWD_KEYS if n_lat % FWD_KEYS == 0 else tk
    n_g = len(gather)

    def body(q_ref, k_ref, v_ref, *rest):
        o_ref, qa_ref = rest[n_g], rest[n_g + 1]
        i = pl.program_id(0)
        if n_g:
            bufs = rest[n_g + 2:2 * n_g + 2]
            start, wait = _plane_exchange(bufs, bufs, rest[-2], rest[-1], False)
            pl.when((i == 0) & (pl.program_id(1) == 0))(start)
        qv = q_ref[...].reshape(M, LANES)

        def step(r0, width, carry):
            m, acc = carry
            sc = _dg(qv, k_ref[pl.ds(r0, width), :], NT)
            m_new = jnp.maximum(m, jnp.max(sc, axis=1, keepdims=True))
            pr = jnp.exp(sc - m_new)
            acc = jnp.exp(m - m_new) * acc + _dot(pr.astype(BF16), v_ref[pl.ds(r0, width), :])
            return m_new, acc

        def wide(s, carry):
            return step(s * wide_k if isinstance(s, int) else pl.multiple_of(s * wide_k, wide_k), wide_k, carry)

        def finish(m, acc):
            den = -acc[:, 64:65]
            out = acc / den
            o_ref[...] = _rows_to_heads([out[g * tq:(g + 1) * tq] for g in range(GROUP)]).astype(BF16)
            qa_ref[...] = _aug(qv.astype(F32), m + jnp.log(den)).astype(BF16).reshape(GROUP, tq, LANES)

        init = (jnp.full((M, 1), -1e30, F32), jnp.zeros((M, LANES), F32))

        @pl.when(i < n_lat // tq)
        def _():
            carry = _loop_unrolled(n_lat // wide_k, wide, init, UNROLL_FWD)
            for r0 in range(n_lat, T, tk):
                carry = step(r0, tk, carry)
            finish(*carry)

        @pl.when(i >= n_lat // tq)
        def _():
            carry = init
            for r0 in range(n_lat, T, tk):
                carry = step(r0, tk, carry)
            finish(*carry)

        if n_g:
            pl.when((i == nq - 1) & (pl.program_id(1) == N_KV_HEADS - 1))(wait)

    q_spec = pl.BlockSpec((None, GROUP, tq, LANES), lambda i, h: (h, 0, i, 0))
    kv_spec = pl.BlockSpec((None, T, LANES), lambda i, h: (h, 0, 0))
    sems = [pltpu.SemaphoreType.DMA((3 * n_g,))] * 2 if n_g else []
    return pl.pallas_call(
        body, grid=(nq, N_KV_HEADS), name="flash_fwd_gather" if n_g else "flash_fwd",
        in_specs=[q_spec, kv_spec, kv_spec] + [ANY] * n_g,
        out_specs=[pl.BlockSpec((tq, GROUP * HEAD_DIM), lambda i, h: (i, h)), q_spec] + [ANY] * n_g,
        out_shape=[jax.ShapeDtypeStruct((T, N_Q_HEADS * HEAD_DIM), BF16), jax.ShapeDtypeStruct(q.shape, BF16)]
        + [jax.ShapeDtypeStruct(b.shape, b.dtype) for b in gather],
        input_output_aliases={3 + a: 2 + a for a in range(n_g)}, scratch_shapes=sems,
        compiler_params=_cp("arbitrary", "arbitrary"))(q, k, v, *gather)


def _flash_bwd(qa, doa, k, v, n_lat, scatter=()):
    _, _, T, _ = qa.shape
    tq = tk = 256
    nkv = T // tk
    M = GROUP * tq

    n_s = len(scatter)

    def body(qa_hbm, doa_hbm, k_ref, v_ref, *rest):
        dq_hbm, dk_ref, dv_ref = rest[n_s:n_s + 3]
        q_sc, do_sc, dq_sc, sems = rest[2 * n_s + 3:2 * n_s + 7]
        h = pl.program_id(0)
        j = pl.program_id(1)
        if n_s:
            start, wait = _plane_exchange(rest[:n_s], rest[n_s + 3:2 * n_s + 3], rest[-2], rest[-1], True)
            pl.when((h == 0) & (j == 0))(start)

        @pl.when(j == 0)
        def _():
            c1 = pltpu.make_async_copy(qa_hbm.at[h], q_sc, sems.at[0])
            c2 = pltpu.make_async_copy(doa_hbm.at[h], do_sc, sems.at[1])
            c1.start()
            c2.start()
            dq_sc[...] = jnp.zeros_like(dq_sc)
            c1.wait()
            c2.wait()

        kb = k_ref[...]
        vb = v_ref[...]
        kbt = kb.astype(F32).T.astype(BF16)

        def step(i, carry):
            dk, dv = carry
            r0 = i * tq if isinstance(i, int) else pl.multiple_of(i * tq, tq)
            qv = q_sc[:, pl.ds(r0, tq), :].reshape(M, LANES)
            dov = do_sc[:, pl.ds(r0, tq), :].reshape(M, LANES)
            pr = jnp.exp(_dg(kb, qv, NT))
            ds = (pr * _dg(vb, dov, NT)).astype(BF16)
            dv = dv + _dot(pr.astype(BF16), dov)
            dk = dk + _dot(ds, qv)
            dq_sc[i] += _dot(kbt, ds)
            return dk, dv

        z = jnp.zeros((tk, LANES), F32)
        carry = _loop_unrolled(n_lat // tq, step, (z, z), UNROLL_BWD)
        dk_ref[...] = carry[0]
        dv_ref[...] = carry[1]

        @pl.when(j >= n_lat // tk)
        def _():
            c = (dk_ref[...], dv_ref[...])
            for i in range(n_lat // tq, T // tq):
                c = step(i, c)
            dk_ref[...] = c[0]
            dv_ref[...] = c[1]

        @pl.when(j == nkv - 1)
        def _():
            c3 = pltpu.make_async_copy(dq_sc, dq_hbm.at[h], sems.at[2])
            c3.start()
            c3.wait()

        if n_s:
            pl.when((h == N_KV_HEADS - 1) & (j == nkv - 1))(wait)

    kv_spec = pl.BlockSpec((None, tk, LANES), lambda h, j: (h, j, 0))
    return pl.pallas_call(
        body, grid=(N_KV_HEADS, nkv), name="flash_bwd_scatter" if n_s else "flash_bwd",
        in_specs=[ANY, ANY, kv_spec, kv_spec] + [ANY] * n_s, out_specs=[ANY, kv_spec, kv_spec] + [ANY] * n_s,
        out_shape=[jax.ShapeDtypeStruct((N_KV_HEADS, T // tq, LANES, M), F32), jax.ShapeDtypeStruct(k.shape, F32),
                   jax.ShapeDtypeStruct(k.shape, F32)] + [jax.ShapeDtypeStruct(g.shape, g.dtype) for g in scatter],
        scratch_shapes=[pltpu.VMEM((GROUP, T, LANES), BF16), pltpu.VMEM((GROUP, T, LANES), BF16),
                        pltpu.VMEM((T // tq, LANES, M), F32), pltpu.SemaphoreType.DMA((3,))]
        + ([pltpu.SemaphoreType.DMA((3 * n_s,))] * 2 if n_s else []),
        compiler_params=_cp("arbitrary", "arbitrary"))(qa, doa, k, v, *scatter)


def _conv_masks(i, tm, n_lat, T):
    row = lax.broadcasted_iota(jnp.int32, (tm, 1), 0)
    g = row + i * tm
    return row, (g == 0) | (g == n_lat), (g == n_lat - 1) | (g == T - 1)


def _shift_rows(v, prev_row, next_row, row, first, last):
    tm = v.shape[0]
    down = jnp.where(row == 0, prev_row, pltpu.roll(v, 1, 0))
    up = jnp.where(row == tm - 1, next_row, pltpu.roll(v, tm - 1, 0))
    return jnp.where(first, 0.0, down), jnp.where(last, 0.0, up)


def _halo_specs(tm, T, width, col):
    nb = T // 8
    return (pl.BlockSpec((8, width), lambda i: (jnp.maximum(i * (tm // 8) - 1, 0), col)),
            pl.BlockSpec((8, width), lambda i: (jnp.minimum((i + 1) * (tm // 8), nb - 1), col)))


def _conv_fwd(p, cw, n_lat):
    T = p.shape[0]
    tm = _tile(T, 1056)

    def body(p_ref, pp_ref, pn_ref, cw_ref, o_ref):
        row, first, last = _conv_masks(pl.program_id(0), tm, n_lat, T)
        z = p_ref[:, 256:512] * p_ref[:, 512:768]
        zp = pp_ref[7:8, 256:512] * pp_ref[7:8, 512:768]
        zn = pn_ref[0:1, 256:512] * pn_ref[0:1, 512:768]
        zd, zu = _shift_rows(z, zp, zn, row, first, last)
        conv = cw_ref[0:1, :] * zd + cw_ref[1:2, :] * z + cw_ref[2:3, :] * zu
        o_ref[...] = (p_ref[:, 0:256] * conv).astype(BF16)

    prev, nxt = _halo_specs(tm, T, 768, 0)
    return pl.pallas_call(
        body, grid=(T // tm,), name="conv_fwd",
        in_specs=[pl.BlockSpec((tm, 768), lambda i: (i, 0)), prev, nxt, _full(cw.shape)],
        out_specs=pl.BlockSpec((tm, CONV_W), lambda i: (i, 0)),
        out_shape=jax.ShapeDtypeStruct((T, CONV_W), BF16), compiler_params=_cp("parallel"))(p, p, p, cw)


def _conv_bwd(dp, dy, p, cw, n_lat):
    T = p.shape[0]
    tm = _tile(T, 1056)

    def body(dp_in, dy_ref, dyp_ref, dyn_ref, p_ref, pp_ref, pn_ref, cw_ref, dp_ref, dcw_ref):
        del dp_in
        i = pl.program_id(0)

        @pl.when(i == 0)
        def _():
            dcw_ref[...] = jnp.zeros_like(dcw_ref)

        row, first, last = _conv_masks(i, tm, n_lat, T)
        ab, ac, ax = p_ref[:, 0:256], p_ref[:, 256:512], p_ref[:, 512:768]
        z = ac * ax
        zp = pp_ref[7:8, 256:512] * pp_ref[7:8, 512:768]
        zn = pn_ref[0:1, 256:512] * pn_ref[0:1, 512:768]
        zd, zu = _shift_rows(z, zp, zn, row, first, last)
        w0, w1, w2 = cw_ref[0:1, :], cw_ref[1:2, :], cw_ref[2:3, :]
        dy = dy_ref[...]
        dc = dy * ab
        dcd, dcu = _shift_rows(dc, dyp_ref[7:8, :] * pp_ref[7:8, 0:256], dyn_ref[0:1, :] * pn_ref[0:1, 0:256],
                               row, first, last)
        dz = w0 * dcu + w1 * dc + w2 * dcd
        dp_ref[:, 0:256] = (dy * (w0 * zd + w1 * z + w2 * zu)).astype(BF16)
        dp_ref[:, 256:512] = (dz * ax).astype(BF16)
        dp_ref[:, 512:768] = (dz * ac).astype(BF16)
        dcw_ref[0:1, :] += jnp.sum(dc * zd, axis=0, keepdims=True)
        dcw_ref[1:2, :] += jnp.sum(dc * z, axis=0, keepdims=True)
        dcw_ref[2:3, :] += jnp.sum(dc * zu, axis=0, keepdims=True)

    prev, nxt = _halo_specs(tm, T, 768, 0)
    dprev, dnxt = _halo_specs(tm, T, CONV_W, 0)
    sec = pl.BlockSpec((tm, 768), lambda i: (i, 0))
    return pl.pallas_call(
        body, grid=(T // tm,), name="conv_bwd",
        in_specs=[ANY, pl.BlockSpec((tm, CONV_W), lambda i: (i, 0)), dprev, dnxt, sec, prev, nxt, _full(cw.shape)],
        out_specs=[sec, _full((8, CONV_W))],
        out_shape=[jax.ShapeDtypeStruct(dp.shape, BF16), jax.ShapeDtypeStruct((8, CONV_W), F32)],
        input_output_aliases={0: 0}, compiler_params=_cp("arbitrary"))(dp, dy, dy, dy, p, p, p, cw)


def _gmlp_mix(bd_ref, vs, grp):
    out = jnp.zeros((2 * CHUNK, SG_W), F32)
    for g in range(4):
        out = jnp.where(grp == g, _dot(bd_ref[g], vs), out)
    return out


def _gmlp_fwd(p, sgn, bd, bias):
    T = p.shape[0]
    tm = _tile(T, 768, 2 * CHUNK)

    def body(p_ref, sgn_ref, bd_ref, bias_ref, o_ref):
        x = _gelu(p_ref[:, 256:512])
        vn = (x * lax.rsqrt(jnp.mean(x * x, axis=-1, keepdims=True) + EPS) * sgn_ref[...]).astype(BF16)
        grp = _lane((2 * CHUNK, SG_W)) // 64
        for s in range(tm // (2 * CHUNK)):
            rs = slice(s * 2 * CHUNK, (s + 1) * 2 * CHUNK)
            mixed = _gmlp_mix(bd_ref, vn[rs], grp) + bias_ref[...]
            o_ref[rs, :] = (_gelu(p_ref[rs, 0:256]) * mixed).astype(BF16)

    return pl.pallas_call(
        body, grid=(T // tm,), name="gmlp_fwd",
        in_specs=[pl.BlockSpec((tm, 2 * SG_W), lambda i: (i, 3)), _full(sgn.shape), _full(bd.shape),
                  _full(bias.shape)],
        out_specs=pl.BlockSpec((tm, SG_W), lambda i: (i, 0)),
        out_shape=jax.ShapeDtypeStruct((T, SG_W), BF16), compiler_params=_cp("parallel"))(p, sgn, bd, bias)


def _gmlp_bwd(dp, dy, p, sgn, bd, bdt, bias, gsum):
    T = p.shape[0]
    tm = _tile(T, 768, 2 * CHUNK)
    nt = T // tm
    C2 = 2 * CHUNK

    def body(dp_in, dy_ref, p_ref, sgn_ref, bd_ref, bdt_ref, bias_ref, gsum_ref,
             dp_ref, dsg_ref, dws_ref, dbs_ref, acc_w, acc_b):
        del dp_in
        i = pl.program_id(0)

        @pl.when(i == 0)
        def _():
            dsg_ref[...] = jnp.zeros_like(dsg_ref)
            acc_w[...] = jnp.zeros_like(acc_w)
            acc_b[...] = jnp.zeros_like(acc_b)

        u = p_ref[:, 0:256]
        sv = p_ref[:, 256:512]
        ug = _gelu(u)
        x = _gelu(sv)
        r = lax.rsqrt(jnp.mean(x * x, axis=-1, keepdims=True) + EPS)
        xh = x * r
        sg = sgn_ref[...]
        vn = (xh * sg).astype(BF16)
        grp = _lane((C2, SG_W)) // 64
        dug, dvn = [], []
        for s in range(tm // C2):
            rs = slice(s * C2, (s + 1) * C2)
            vs = vn[rs]
            dys = dy_ref[rs, :]
            dug.append(dys * (_gmlp_mix(bd_ref, vs, grp) + bias_ref[...]))
            dmix = dys * ug[rs]
            acc_b[...] += dmix
            dmb = dmix.astype(BF16)
            dvn.append(_gmlp_mix(bdt_ref, dmb, grp))
            for g in range(4):
                acc_w[g] += _dg(jnp.where(grp == g, dmb, jnp.zeros_like(dmb)), vs, NT)
        dug = jnp.concatenate(dug, axis=0)
        dvn = jnp.concatenate(dvn, axis=0)
        dsg_ref[...] += jnp.sum(dvn * xh, axis=0, keepdims=True)
        gd = sg * dvn
        dx = r * (gd - xh * jnp.mean(xh * gd, axis=-1, keepdims=True))
        dp_ref[:, 0:256] = (dug * _gelu_grad(u)).astype(BF16)
        dp_ref[:, 256:512] = (dx * _gelu_grad(sv)).astype(BF16)

        @pl.when(i == nt - 1)
        def _():
            for g in range(4):
                dws_ref[g] = acc_w[g, 0:CHUNK, 0:CHUNK] + acc_w[g, CHUNK:C2, CHUNK:C2]
            dbs_ref[...] = jnp.dot(acc_b[0:CHUNK, :] + acc_b[CHUNK:C2, :], gsum_ref[...],
                                   preferred_element_type=F32, precision=lax.Precision.HIGHEST)

    sec = pl.BlockSpec((tm, 2 * SG_W), lambda i: (i, 3))
    return pl.pallas_call(
        body, grid=(nt,), name="gmlp_bwd",
        in_specs=[ANY, pl.BlockSpec((tm, SG_W), lambda i: (i, 0)), sec, _full(sgn.shape), _full(bd.shape),
                  _full(bdt.shape), _full(bias.shape), _full(gsum.shape)],
        out_specs=[sec, _full((1, SG_W)), _full((4, CHUNK, CHUNK)), _full((CHUNK, LANES))],
        out_shape=[jax.ShapeDtypeStruct(dp.shape, BF16), jax.ShapeDtypeStruct((1, SG_W), F32),
                   jax.ShapeDtypeStruct((4, CHUNK, CHUNK), F32), jax.ShapeDtypeStruct((CHUNK, LANES), F32)],
        scratch_shapes=[pltpu.VMEM((4, C2, C2), F32), pltpu.VMEM((C2, SG_W), F32)],
        input_output_aliases={0: 0}, compiler_params=_cp("arbitrary"))(dp, dy, p, sgn, bd, bdt, bias, gsum)


def _merge_fwd(ya, at, yc, p, wa, wb, wc):
    T = p.shape[0]
    tm = _tile(T, 528)
    n = wa.shape[2]

    def body(ya_ref, at_ref, yc_ref, ga_ref, gb_ref, gc_ref, wa_ref, wb_ref, wc_ref, o_ref):
        yav, atv, ycv = ya_ref[...], at_ref[...], yc_ref[...]
        for j in range(N_CHIPS):
            cs = slice(j * n, (j + 1) * n)
            m = (_sigmoid(ga_ref[:, cs]) * _dot(yav, wa_ref[j]) + _sigmoid(gb_ref[:, cs]) * _dot(atv, wb_ref[j])
                 + _sigmoid(gc_ref[:, cs]) * _dot(ycv, wc_ref[j]))
            o_ref[:, cs] = m.astype(BF16)

    def rows(w, col=0):
        return pl.BlockSpec((tm, w), lambda i: (i, col))

    return pl.pallas_call(
        body, grid=(T // tm,), name="merge_fwd",
        in_specs=[rows(CONV_W), rows(512), rows(SG_W), rows(D_MODEL, 2), rows(D_MODEL, 3), rows(D_MODEL, 4),
                  _full(wa.shape), _full(wb.shape), _full(wc.shape)],
        out_specs=rows(D_MODEL), out_shape=jax.ShapeDtypeStruct((T, D_MODEL), BF16),
        compiler_params=_cp("parallel"))(ya, at, yc, p, p, p, wa, wb, wc)


def _merge_bwd(dyo, ya, at, yc, p, wa, wb, wc, wo):
    T = p.shape[0]
    tm = _tile(T, 528)
    n = wa.shape[2]

    def body(dyo_ref, ya_ref, at_ref, yc_ref, ga_ref, gb_ref, gc_ref, wa_ref, wb_ref, wc_ref, wo_ref,
             dp_ref, dya_ref, doa_ref, dyc_ref, dwa_ref, dwb_ref, dwc_ref):
        i = pl.program_id(0)

        @pl.when(i == 0)
        def _():
            dwa_ref[...] = jnp.zeros_like(dwa_ref)
            dwb_ref[...] = jnp.zeros_like(dwb_ref)
            dwc_ref[...] = jnp.zeros_like(dwc_ref)

        dp_ref[:, 0:OFF_G] = jnp.zeros((tm, OFF_G), BF16)
        dm = _dg(dyo_ref[...], wo_ref[...], NT)
        yav, atv, ycv = ya_ref[...], at_ref[...], yc_ref[...]
        dya = jnp.zeros((tm, CONV_W), F32)
        dat = jnp.zeros((tm, 512), F32)
        dyc = jnp.zeros((tm, SG_W), F32)
        for j in range(N_CHIPS):
            cs = slice(j * n, (j + 1) * n)
            dmj = dm[:, cs]
            for y_in, w_ref, g_ref, dw_ref, which in (
                    (yav, wa_ref, ga_ref, dwa_ref, 0), (atv, wb_ref, gb_ref, dwb_ref, 1),
                    (ycv, wc_ref, gc_ref, dwc_ref, 2)):
                sg = _sigmoid(g_ref[:, cs])
                y = _dot(y_in, w_ref[j])
                c0 = OFF_G + which * D_MODEL + j * n
                dp_ref[:, c0:c0 + n] = (dmj * y * sg * (1.0 - sg)).astype(BF16)
                dyb = (dmj * sg).astype(BF16)
                dw_ref[j] += _dg(y_in, dyb, TN)
                back = _dg(dyb, w_ref[j], NT)
                if which == 0:
                    dya = dya + back
                elif which == 1:
                    dat = dat + back
                else:
                    dyc = dyc + back
        dya_ref[...] = dya
        dyc_ref[...] = dyc
        prod = dat * atv.astype(F32)
        lane = _lane((tm, LANES))
        dat_rows = _heads_to_rows(dat, N_Q_HEADS)
        for h in range(N_Q_HEADS):
            grp = prod[:, (h // 2) * LANES:(h // 2 + 1) * LANES]
            keep = (lane < 64) if h % 2 == 0 else (lane >= 64)
            delta = jnp.sum(jnp.where(keep, grp, 0.0), axis=1, keepdims=True)
            doa_ref[h] = _aug(dat_rows[h], delta).astype(BF16)

    def rows(w, col=0):
        return pl.BlockSpec((tm, w), lambda i: (i, col))

    return pl.pallas_call(
        body, grid=(T // tm,), name="merge_bwd",
        in_specs=[rows(D_MODEL), rows(CONV_W), rows(512), rows(SG_W), rows(D_MODEL, 2), rows(D_MODEL, 3),
                  rows(D_MODEL, 4), _full(wa.shape), _full(wb.shape), _full(wc.shape), _full(wo.shape)],
        out_specs=[rows(IN_W), rows(CONV_W),
                   pl.BlockSpec((N_Q_HEADS, tm, LANES), lambda i: (0, i, 0)), rows(SG_W),
                   _full(wa.shape), _full(wb.shape), _full(wc.shape)],
        out_shape=[jax.ShapeDtypeStruct((T, IN_W), BF16)] + [
            jax.ShapeDtypeStruct((T, CONV_W), F32), jax.ShapeDtypeStruct((N_Q_HEADS, T, LANES), BF16),
            jax.ShapeDtypeStruct((T, SG_W), F32), jax.ShapeDtypeStruct(wa.shape, F32),
            jax.ShapeDtypeStruct(wb.shape, F32), jax.ShapeDtypeStruct(wc.shape, F32)],
        compiler_params=_cp("arbitrary"))(dyo, ya, at, yc, p, p, p, wa, wb, wc, wo)


def _loss_grad(xf, tgt, n_lat):
    T, D = xf.shape
    tm = _tile(np.gcd(n_lat, T), 512)
    nl = n_lat // tm

    def body(x_ref, t_ref, dy_ref, l_ref):
        i = pl.program_id(0)

        @pl.when(i == 0)
        def _():
            l_ref[...] = jnp.zeros_like(l_ref)

        @pl.when(i < nl)
        def _():
            err = x_ref[...] - t_ref[...]
            dy_ref[...] = err * (1.0 / D)
            sq = jnp.sum(jnp.sum(err * err, axis=1, keepdims=True), axis=0, keepdims=True)
            l_ref[...] += (0.5 / D) * sq

        @pl.when(i >= nl)
        def _():
            dy_ref[...] = jnp.zeros_like(dy_ref)

    return pl.pallas_call(
        body, grid=(T // tm,), name="loss_grad",
        in_specs=[pl.BlockSpec((tm, D), lambda i: (i, 0)), pl.BlockSpec((tm, D), lambda i: (jnp.minimum(i, nl - 1), 0))],
        out_specs=[pl.BlockSpec((tm, D), lambda i: (i, 0)), _full((8, LANES))],
        out_shape=[jax.ShapeDtypeStruct((T, D), F32), jax.ShapeDtypeStruct((8, LANES), F32)],
        compiler_params=_cp("arbitrary"))(xf, tgt)


def _row_tile(R, C):
    if R * C <= (1 << 19) or R % 8:
        return R
    return _tile(R, max(8, (1 << 19) // C), 8)


def _adamw(w, m, v, g1, g2=None):
    shape = w.shape
    C = shape[-1]
    R = int(np.prod(shape[:-1])) if len(shape) > 1 else 1
    tr = _row_tile(R, C)
    ins = [a.reshape(R, C) for a in ((w, m, v, g1) if g2 is None else (w, m, v, g1, g2))]

    def body(*refs):
        w_ref, m_ref, v_ref = refs[0], refs[1], refs[2]
        g_ref, d_ref, m2_ref, v2_ref = refs[-4:]
        g = refs[3][...] if g2 is None else refs[3][...] + refs[4][...]
        m2 = ADAM_B1 * m_ref[...] + (1.0 - ADAM_B1) * g
        v2 = ADAM_B2 * v_ref[...] + (1.0 - ADAM_B2) * (g * g)
        m_hat = m2 / (1.0 - ADAM_B1 ** ADAM_STEP)
        v_hat = v2 / (1.0 - ADAM_B2 ** ADAM_STEP)
        g_ref[...] = g
        d_ref[...] = -ADAM_LR * (m_hat / (jnp.sqrt(v_hat) + ADAM_EPS) + ADAM_WD * w_ref[...])
        m2_ref[...] = m2
        v2_ref[...] = v2

    spec = pl.BlockSpec((tr, C), lambda i: (i, 0))
    outs = pl.pallas_call(
        body, grid=(R // tr,), name="adamw", in_specs=[spec] * len(ins), out_specs=[spec] * 4,
        out_shape=[jax.ShapeDtypeStruct((R, C), F32)] * 4, compiler_params=_cp("parallel"))(*ins)
    return [o.reshape(shape) for o in outs]


def _adamw_layer(w, m, v, gs, l, prev):
    L, a, b = w.shape
    tr = _row_tile(a, b)
    n_in = 3 + len(gs)

    def body(*refs):
        g_ref, d_ref, m2_ref, v2_ref = refs[-4:]
        g = refs[3][...]
        for r in refs[4:3 + len(gs)]:
            g = g + r[...]
        m2 = ADAM_B1 * refs[1][...] + (1.0 - ADAM_B1) * g
        v2 = ADAM_B2 * refs[2][...] + (1.0 - ADAM_B2) * (g * g)
        m_hat = m2 / (1.0 - ADAM_B1 ** ADAM_STEP)
        v_hat = v2 / (1.0 - ADAM_B2 ** ADAM_STEP)
        g_ref[...] = g
        d_ref[...] = -ADAM_LR * (m_hat / (jnp.sqrt(v_hat) + ADAM_EPS) + ADAM_WD * refs[0][...])
        m2_ref[...] = m2
        v2_ref[...] = v2

    layer = pl.BlockSpec((None, tr, b), lambda i: (l, i, 0))
    outs = pl.pallas_call(
        body, grid=(a // tr,), name="adamw_layer",
        in_specs=[layer] * 3 + [pl.BlockSpec((tr, b), lambda i: (i, 0))] * len(gs) + [ANY] * len(prev),
        out_specs=[layer] * 4, out_shape=[jax.ShapeDtypeStruct((L, a, b), F32)] * 4,
        input_output_aliases={n_in + k: k for k in range(len(prev))},
        compiler_params=_cp("parallel"))(w, m, v, *gs, *prev)
    return list(outs)


def _sum_lead(x, name):
    n, R, C = x.shape
    tr = _row_tile(R, C * n)

    def body(x_ref, o_ref):
        acc = x_ref[0].astype(F32)
        for s in range(1, n):
            acc = acc + x_ref[s].astype(F32)
        o_ref[...] = acc

    return pl.pallas_call(
        body, grid=(R // tr,), name=name, in_specs=[pl.BlockSpec((n, tr, C), lambda i: (0, i, 0))],
        out_specs=pl.BlockSpec((tr, C), lambda i: (i, 0)), out_shape=jax.ShapeDtypeStruct((R, C), F32),
        compiler_params=_cp("parallel"))(x)


def _silu(x):
    return x * _sigmoid(x)


def _mod_fwd(a_raw, w_mod, bsh):
    L, D, n = w_mod.shape

    def body(a_ref, w_ref, b_ref, o_ref):
        o_ref[...] = _dot(_silu(a_ref[...]).astype(BF16), w_ref[...].astype(BF16)) + b_ref[...]

    return pl.pallas_call(
        body, grid=(L,), name="mod_fwd",
        in_specs=[_full(a_raw.shape), pl.BlockSpec((None, D, n), lambda l: (l, 0, 0)),
                  pl.BlockSpec((None, 1, n), lambda l: (l, 0, 0))],
        out_specs=pl.BlockSpec((None, 16, n), lambda l: (l, 0, 0)),
        out_shape=jax.ShapeDtypeStruct((L, 16, n), F32), compiler_params=_cp("parallel"))(a_raw, w_mod, bsh)


def _wmod_grad(a_raw, dms):
    L, _, n = dms.shape
    D = a_raw.shape[1]

    def body(a_ref, dm_ref, o_ref):
        o_ref[...] = _dg(_silu(a_ref[...]).astype(BF16), dm_ref[...].astype(BF16), TN)

    return pl.pallas_call(
        body, grid=(L,), name="wmod_grad",
        in_specs=[_full(a_raw.shape), pl.BlockSpec((None, 16, n), lambda l: (l, 0, 0))],
        out_specs=pl.BlockSpec((None, D, n), lambda l: (l, 0, 0)),
        out_shape=jax.ShapeDtypeStruct((L, D, n), F32), compiler_params=_cp("parallel"))(a_raw, dms)


def _cctx_partial(dmc, w_mod):
    L, D, n = w_mod.shape

    def body(dm_ref, w_ref, o_ref):
        part = _dg(dm_ref[...].astype(BF16), w_ref[...].astype(BF16), NT)

        @pl.when(pl.program_id(0) == 0)
        def _():
            o_ref[...] = part

        @pl.when(pl.program_id(0) > 0)
        def _():
            o_ref[...] += part

    return pl.pallas_call(
        body, grid=(L,), name="cctx_partial",
        in_specs=[pl.BlockSpec((None, 16, n), lambda l: (l, 0, 0)), pl.BlockSpec((None, D, n), lambda l: (l, 0, 0))],
        out_specs=_full((16, D)), out_shape=jax.ShapeDtypeStruct((16, D), F32),
        compiler_params=_cp("arbitrary"))(dmc, w_mod)


def _cctx_final(parts, cc):
    def body(p_ref, c_ref, o_ref):
        s = p_ref[0, 0:8, :]
        for j in range(1, N_CHIPS):
            s = s + p_ref[2 * j, 0:8, :]
        xv = c_ref[...]
        sg = _sigmoid(xv)
        o_ref[...] = s * (sg * (1.0 + xv * (1.0 - sg)))

    return pl.pallas_call(
        body, name="cctx_final", in_specs=[_full(parts.shape), _full(cc.shape)], out_specs=_full((8, LANES)),
        out_shape=jax.ShapeDtypeStruct((8, LANES), F32), compiler_params=_cp())(parts, cc)


def _me():
    return lax.axis_index("x"), lax.axis_index("y"), lax.axis_index("c")


def _flip(v, bit):
    return 1 - v if bit else v


def _remote(src, dst, ssem, rsem, peer):
    return pltpu.make_async_remote_copy(src_ref=src, dst_ref=dst, send_sem=ssem, recv_sem=rsem,
                                        device_id=peer, device_id_type=MESH_ID)


def _ag8(xb, name):
    R = xb.shape[0]

    def pallas(x):
        def body(x_ref, o_ref, ssem, rsem):
            mx, my, mc = _me()
            me = 4 * mx + 2 * my + mc
            sib = (mx, my, 1 - mc)
            peers = _plane_peers(mx, my, mc)
            sends = [_remote(x_ref, o_ref.at[me], ssem.at[0], rsem.at[0], sib)]
            sends += [_remote(x_ref, o_ref.at[me], ssem.at[1 + k], rsem.at[1 + k], peer)
                      for k, (peer, _) in enumerate(peers)]
            for cp in sends:
                cp.start()
            for k, (peer, pj) in enumerate(peers):
                got = o_ref.at[2 * pj + mc]
                _remote(got, got, ssem.at[1 + k], rsem.at[1 + k], peer).wait_recv()
                fw = _remote(got, got, ssem.at[4 + k], rsem.at[4 + k], sib)
                fw.start()
                sends.append(fw)
            _remote(x_ref, o_ref.at[4 * mx + 2 * my + 1 - mc], ssem.at[0], rsem.at[0], sib).wait_recv()
            for k, (_, pj) in enumerate(peers):
                theirs = o_ref.at[2 * pj + 1 - mc]
                _remote(theirs, theirs, ssem.at[4 + k], rsem.at[4 + k], sib).wait_recv()
            for cp in sends:
                cp.wait_send()

        return pl.pallas_call(
            body, name=name, in_specs=[ANY], out_specs=ANY, out_shape=jax.ShapeDtypeStruct((N_DEV, R, LANES), F32),
            scratch_shapes=[pltpu.SemaphoreType.DMA((N_DEV - 1,)), pltpu.SemaphoreType.DMA((N_DEV - 1,))])(x)

    mx, my, mc = _me()
    return lax.dynamic_update_slice(pallas(xb), xb[None], (4 * mx + 2 * my + mc, 0, 0))


def _plane_peers(mx, my, mc):
    out = []
    for k in range(1, N_CHIPS):
        px, py = _flip(mx, k & 2), _flip(my, k & 1)
        out.append(((px, py, mc), 2 * px + py))
    return out


def _plane_exchange(ins, outs, ssem, rsem, scatter):
    n = len(ins)

    def desc(k, a, arriving):
        mx, my, mc = _me()
        j = 2 * mx + my
        peer, pj = _plane_peers(mx, my, mc)[k]
        src = ins[a].at[pj if scatter else j]
        dst = outs[a].at[pj if arriving else j]
        return _remote(src, dst, ssem.at[k * n + a], rsem.at[k * n + a], peer)

    def start():
        for k in range(N_CHIPS - 1):
            for a in range(n):
                desc(k, a, False).start()

    def wait():
        for k in range(N_CHIPS - 1):
            for a in range(n):
                desc(k, a, True).wait_recv()
        for k in range(N_CHIPS - 1):
            for a in range(n):
                desc(k, a, False).wait_send()

    return start, wait


def _chip_gather(bufs, name):
    n = len(bufs)
    halves = [b.shape[1] // 2 for b in bufs]

    def body(*refs):
        outs = refs[n:2 * n]
        ssem, rsem, fsem, gsem = refs[2 * n:]
        mx, my, mc = _me()
        j = 2 * mx + my
        sib = (mx, my, 1 - mc)

        def half(a, blk, c):
            return outs[a].at[blk, pl.ds(c * halves[a], halves[a]), :]

        peers = _plane_peers(mx, my, mc)
        sends = []
        for k, (peer, _) in enumerate(peers):
            for a in range(n):
                mine = half(a, j, mc)
                cp = _remote(mine, mine, ssem.at[k * n + a], rsem.at[k * n + a], peer)
                cp.start()
                sends.append(cp)
        for k, (peer, pj) in enumerate(peers):
            for a in range(n):
                got = half(a, pj, mc)
                _remote(got, got, ssem.at[k * n + a], rsem.at[k * n + a], peer).wait_recv()
                fw = _remote(got, got, fsem.at[k * n + a], gsem.at[k * n + a], sib)
                fw.start()
                sends.append(fw)
        for k, (_, pj) in enumerate(peers):
            for a in range(n):
                theirs = half(a, pj, 1 - mc)
                _remote(theirs, theirs, fsem.at[k * n + a], gsem.at[k * n + a], sib).wait_recv()
        for cp in sends:
            cp.wait_send()

    sems = pltpu.SemaphoreType.DMA((3 * n,))
    return pl.pallas_call(
        body, name=name, in_specs=[ANY] * n, out_specs=[ANY] * n,
        out_shape=[jax.ShapeDtypeStruct(b.shape, b.dtype) for b in bufs],
        input_output_aliases={a: a for a in range(n)},
        scratch_shapes=[sems, sems, sems, sems])(*bufs)


def _chip_scatter(gs, name):
    n = len(gs)

    def body(*refs):
        ins, outs = refs[:n], refs[n:2 * n]
        ssem, rsem = refs[2 * n:]
        mx, my, mc = _me()
        j = 2 * mx + my
        peers = _plane_peers(mx, my, mc)
        sends = []
        for k, (peer, pj) in enumerate(peers):
            for a in range(n):
                cp = _remote(ins[a].at[pj], outs[a].at[j], ssem.at[k * n + a], rsem.at[k * n + a], peer)
                cp.start()
                sends.append(cp)
        for k, (peer, pj) in enumerate(peers):
            for a in range(n):
                _remote(ins[a].at[pj], outs[a].at[pj], ssem.at[k * n + a], rsem.at[k * n + a], peer).wait_recv()
        for cp in sends:
            cp.wait_send()

    return pl.pallas_call(
        body, name=name, in_specs=[ANY] * n, out_specs=[ANY] * n,
        out_shape=[jax.ShapeDtypeStruct(g.shape, g.dtype) for g in gs],
        scratch_shapes=[pltpu.SemaphoreType.DMA((3 * n,)), pltpu.SemaphoreType.DMA((3 * n,))])(*gs)


def _sibling_swap(xs, name):
    n = len(xs)

    def body(*refs):
        ins, outs = refs[:n], refs[n:2 * n]
        ssem, rsem = refs[2 * n:]
        mx, my, mc = _me()
        cps = [_remote(ins[a], outs[a], ssem.at[a], rsem.at[a], (mx, my, 1 - mc)) for a in range(n)]
        for cp in cps:
            cp.start()
        for cp in cps:
            cp.wait()

    return pl.pallas_call(
        body, name=name, in_specs=[ANY] * n, out_specs=[ANY] * n,
        out_shape=[jax.ShapeDtypeStruct(x.shape, x.dtype) for x in xs],
        scratch_shapes=[pltpu.SemaphoreType.DMA((n,)), pltpu.SemaphoreType.DMA((n,))])(*xs)


def _sibling_halves(gs, name):
    n = len(gs)

    def body(*refs):
        ins, outs = refs[:n], refs[n:2 * n]
        ssem, rsem = refs[2 * n:]
        mx, my, mc = _me()
        cps = []
        for a in range(n):
            h = gs[a].shape[1] // 2
            cps.append(_remote(ins[a].at[:, pl.ds((1 - mc) * h, h), :], outs[a], ssem.at[a], rsem.at[a],
                               (mx, my, 1 - mc)))
        for cp in cps:
            cp.start()
        for cp in cps:
            cp.wait()

    return pl.pallas_call(
        body, name=name, in_specs=[ANY] * n, out_specs=[ANY] * n,
        out_shape=[jax.ShapeDtypeStruct((g.shape[0], g.shape[1] // 2, g.shape[2]), g.dtype) for g in gs],
        scratch_shapes=[pltpu.SemaphoreType.DMA((n,)), pltpu.SemaphoreType.DMA((n,))])(*gs)


def _sibling_fill(hs, name):
    n = len(hs)

    def body(*refs):
        ins, outs = refs[:n], refs[n:2 * n]
        ssem, rsem = refs[2 * n:]
        mx, my, mc = _me()
        cps = []
        for a in range(n):
            h = hs[a].shape[0]
            cps.append(_remote(ins[a], outs[a].at[pl.ds(mc * h, h), :], ssem.at[a], rsem.at[a], (mx, my, 1 - mc)))
        for cp in cps:
            cp.start()
        for a, cp in enumerate(cps):
            h = hs[a].shape[0]
            theirs = outs[a].at[pl.ds((1 - mc) * h, h), :]
            _remote(ins[a], theirs, ssem.at[a], rsem.at[a], (mx, my, 1 - mc)).wait_recv()
            cp.wait_send()

    return pl.pallas_call(
        body, name=name, in_specs=[ANY] * n, out_specs=[ANY] * n,
        out_shape=[jax.ShapeDtypeStruct((2 * x.shape[0], x.shape[1]), x.dtype) for x in hs],
        scratch_shapes=[pltpu.SemaphoreType.DMA((n,)), pltpu.SemaphoreType.DMA((n,))])(*hs)


def _add_cast(g, sb):
    J, h, b = g.shape
    th = _row_tile(h, b * J)

    def body(g_ref, s_ref, o_ref):
        o_ref[...] = (g_ref[...].astype(F32) + s_ref[...].astype(F32)).astype(BF16)

    spec = pl.BlockSpec((J, th, b), lambda i: (0, i, 0))
    return pl.pallas_call(
        body, grid=(h // th,), name="add_planes", in_specs=[spec, spec], out_specs=spec,
        out_shape=jax.ShapeDtypeStruct((J, h, b), BF16), compiler_params=_cp("parallel"))(g, sb)


_WEIGHTS = ("c_ctx", "w_mod", "b_mod", "norm1", "w_in", "q_gain", "k_gain", "conv_w", "sg_norm", "w_s", "b_s",
            "w_a", "w_b", "w_c", "w_o", "norm2", "w_ff1", "w_ff3", "w_ff2")
_BIG = ("w_in", "w_a", "w_b", "w_c", "w_o", "w_ff1", "w_ff3", "w_ff2")


def _constants():
    idx = np.arange(LANES)
    e = (idx[:, None] // 64 == idx[None, :] // 64).astype(np.float32) / 64.0
    c512 = np.arange(512)
    fold = (c512[:, None] % 64 == idx[None, :]).astype(np.float32)
    c256 = np.arange(SG_W)
    gsum = (c256[:, None] // 64 == idx[None, :]).astype(np.float32)
    return jnp.asarray(e, BF16), jnp.asarray(fold, F32), jnp.asarray(gsum, F32)


def _rope_tables(n_lat, n_ctx):
    t = jnp.arange(n_lat)
    inv = ROPE_THETA ** (-jnp.arange(0, HEAD_DIM // 2, 2, dtype=F32) / (HEAD_DIM // 2))
    ar = (t // GRID_W).astype(F32)[:, None] * inv
    ac = (t % GRID_W).astype(F32)[:, None] * inv
    cos = jnp.concatenate([jnp.cos(ar), jnp.cos(ar), jnp.cos(ac), jnp.cos(ac)], axis=1)
    sin = jnp.concatenate([-jnp.sin(ar), jnp.sin(ar), -jnp.sin(ac), jnp.sin(ac)], axis=1)
    cos = jnp.concatenate([cos, jnp.ones((n_ctx, HEAD_DIM), F32)], axis=0)
    sin = jnp.concatenate([sin, jnp.zeros((n_ctx, HEAD_DIM), F32)], axis=0)
    return jnp.concatenate([cos, cos], axis=1), jnp.concatenate([sin, sin], axis=1)


def kernel(x, c, ctx, c_ctx, w_mod, b_mod, norm1, w_in, q_gain, k_gain, conv_w, sg_norm, w_s, b_s, w_a, w_b, w_c, w_o, norm2, w_ff1, w_ff3, w_ff2, loss_target, m_c_ctx, m_w_mod, m_b_mod, m_norm1, m_w_in, m_q_gain, m_k_gain, m_conv_w, m_sg_norm, m_w_s, m_b_s, m_w_a, m_w_b, m_w_c, m_w_o, m_norm2, m_w_ff1, m_w_ff3, m_w_ff2, v_c_ctx, v_w_mod, v_b_mod, v_norm1, v_w_in, v_q_gain, v_k_gain, v_conv_w, v_sg_norm, v_w_s, v_b_s, v_w_a, v_w_b, v_w_c, v_w_o, v_norm2, v_w_ff1, v_w_ff3, v_w_ff2):
    given = dict(locals())
    mx, my, mc = _me()
    chip = 2 * mx + my
    dev = 4 * mx + 2 * my + mc
    L = norm1.shape[0]
    S, Lc = x.shape[1], ctx.shape[1]
    T = S + Lc
    D = D_MODEL
    n_mod, n_in, n_ff = w_mod.shape[2], w_in.shape[2], w_ff1.shape[2]
    n_cw = conv_w.shape[2]
    e_avg, fold, gsum = _constants()
    cos_t, sin_t = _rope_tables(S, Lc)

    cw_rows = (L * 3 * n_cw) // LANES
    pad = (-(8 + cw_rows)) % 8
    buf = jnp.concatenate([c.reshape(8, LANES), conv_w.reshape(cw_rows, LANES), jnp.zeros((pad, LANES), F32)], axis=0)
    g1 = _ag8(buf, "gather_cond")
    conds = g1[:, :8].reshape(N_DEV, D)
    cw_full = jnp.stack([g1[2 * j, 8:8 + cw_rows].reshape(L, 3, n_cw) for j in range(N_CHIPS)], axis=2)
    cw_full = cw_full.reshape(L, 3, N_CHIPS * n_cw)
    cw8 = jnp.pad(cw_full, ((0, 0), (0, 5), (0, 0)))
    a_raw = jnp.concatenate([conds, c_ctx[None], jnp.zeros((7, D), F32)], axis=0)
    bsh = lax.dynamic_slice_in_dim(b_mod, chip * n_mod, n_mod, axis=1)[:, None, :]
    mod_sh = _mod_fwd(a_raw, w_mod, bsh)
    g2 = _ag8(mod_sh.reshape(-1, LANES), "gather_mod")
    mods = jnp.stack([g2[2 * j].reshape(L, 16, n_mod) for j in range(N_CHIPS)], axis=2).reshape(L, 16, N_CHIPS * n_mod)
    lat = lax.dynamic_index_in_dim(mods, dev, axis=1, keepdims=False)
    mod = jnp.stack([lat.reshape(L, 6, D), mods[:, 8].reshape(L, 6, D)], axis=1)
    mod = jnp.pad(mod, ((0, 0), (0, 0), (0, 2), (0, 0)))

    qg = jnp.tile(q_gain, (1, N_Q_HEADS))[:, None, :]
    kg = jnp.tile(k_gain, (1, N_KV_HEADS))[:, None, :]
    sgn = sg_norm[:, None, :]
    ws_b = w_s.astype(BF16)
    zero = jnp.zeros_like(ws_b)
    bd = jnp.concatenate([jnp.concatenate([ws_b, zero], axis=3), jnp.concatenate([zero, ws_b], axis=3)], axis=2)
    bdt = jnp.swapaxes(bd, 2, 3)
    bias = jnp.tile(jnp.repeat(jnp.swapaxes(b_s, 1, 2), SG_W // 4, axis=2), (1, 2, 1))

    def shard_bufs(l, names=_BIG):
        return [lax.dynamic_update_slice(lax.empty((N_CHIPS,) + given[nm].shape[1:], BF16),
                                         given[nm][l].astype(BF16)[None], (chip, 0, 0)) for nm in names]

    def unpack(bufs):
        win, wa, wb, wc, wo, w1, w3, w2 = bufs
        return win, wa, wb, wc, wo.reshape(1, D, D), w1, w3, w2

    def layer_fwd(X, l, W, nxt):
        win = W[0]
        h, ht = _norm_mod(X, norm1[l][None], mod[l], 0, 1, S)
        p = _mm_nn(h, win, F32, "in_proj")
        ya = _conv_fwd(p, cw8[l], S)
        q, k, v = _qkv_prep(p, cos_t, sin_t, qg[l], kg[l], e_avg)
        at, qa, *got = _flash_fwd(q.reshape(N_KV_HEADS, GROUP, T, LANES), k, v, S, nxt)
        if len(W) == 1:
            W, got = unpack([win] + got[:len(_BIG) - 1]), got[len(_BIG) - 1:]
        win, wa, wb, wc, wo, w1, w3, w2 = W
        yc = _gmlp_fwd(p, sgn[l], bd[l], bias[l])
        mg = _merge_fwd(ya, at, yc, p, wa, wb, wc)
        X1, f1 = _mm_res(mg[None], wo, X, mod[l], 2, S, "out_proj")
        h2, h2t = _norm_mod(X1, norm2[l][None], mod[l], 3, 4, S)
        a1, a3, act = _ffn_up(h2, w1, w3)
        X2, f2 = _mm_res(act, w2, X1, mod[l], 5, S, "ffn_down")
        return X2, W, got, dict(X=X, ht=ht, h2t=h2t, p=p, ya=ya, k=k, v=v, at=at, qa=qa, yc=yc, mg=mg, X1=X1, f1=f1,
                             a1=a1, a3=a3, act=act, f2=f2)

    def layer_bwd(dX2, l, W, sv, pending):
        win, wa, wb, wc, wo, w1, w3, w2 = W
        dyf, dgt2 = _gate_bwd(dX2, sv["f2"], mod[l], 5, S)
        da1, da3 = _ffn_down_bwd(dyf, w2, sv["a1"], sv["a3"])
        dw2 = _mm_tn(sv["act"], dyf, _shard_rows(n_ff), _rows(D), N_CHIPS, n_ff, D, T, "dw_ff2")
        dh2 = _mm_nt_acc([da1, da3], [w1, w3], False, "ffn_up_bwd")
        dw1 = _mm_dw(sv["h2t"], da1, _shard_rows(n_ff), N_CHIPS, n_ff, "dw_ff1")
        dw3 = _mm_dw(sv["h2t"], da3, _shard_rows(n_ff), N_CHIPS, n_ff, "dw_ff3")
        dX1, dn2, dsh2, dsc2 = _norm_mod_bwd(sv["X1"], dh2, dX2, norm2[l][None], mod[l], 4, S)
        dyo, dgt1 = _gate_bwd(dX1, sv["f1"], mod[l], 2, S)
        dwo = _mm_tn(sv["mg"], dyo, _rows(D), _rows(D), 1, D, D, T, "dw_o")
        dp, dya, doa, dyc, dwa, dwb, dwc = _merge_bwd(dyo, sv["ya"], sv["at"], sv["yc"], sv["p"], wa, wb, wc, wo[0])
        dp, dcw = _conv_bwd(dp, dya, sv["p"], cw8[l], S)
        dp, dsg, dws, dbs = _gmlp_bwd(dp, dyc, sv["p"], sgn[l], bd[l], bdt[l], bias[l], gsum)
        early = [dwa.astype(BF16), dwb.astype(BF16), dwc.astype(BF16), dwo.reshape(N_CHIPS, D // N_CHIPS, D),
                 dw1, dw3, dw2]
        dq, dk, dv, *recv = _flash_bwd(sv["qa"], doa.reshape(N_KV_HEADS, GROUP, T, LANES), sv["k"], sv["v"], S,
                                       list(pending) + (early if l == 0 else []))
        dp, dqg, dkg = _qkv_prep_bwd(dp, dq, dk, dv, sv["p"], cos_t, sin_t,
                                     qg[l], kg[l], e_avg, fold)
        dh = _mm_nt_acc([dp], [win], True, "in_proj_bwd")
        dwin = _mm_dw(sv["ht"], dp, _row_cols(n_in), N_CHIPS, n_in, "dw_in")
        dX0, dn1, dsh1, dsc1 = _norm_mod_bwd(sv["X"], dh, dX1, norm1[l][None], mod[l], 1, S)
        dmod = jnp.concatenate([dsh1, dsc1, dgt1, dsh2, dsc2, dgt2], axis=1)
        big = [dwin] + early
        small = dict(norm1=dn1[0], norm2=dn2[0], q_gain=dqg[0, :HEAD_DIM], k_gain=dkg[0, :HEAD_DIM],
                     conv_w=dcw[:3], sg_norm=dsg[0], w_s=dws, b_s=jnp.swapaxes(dbs[:, :4], 0, 1), dmod=dmod)
        return dX0, big, small, recv

    X = jnp.concatenate([x[0], ctx[0]], axis=0)
    Ws, saved = [_chip_gather(shard_bufs(0, _BIG[:1]), "gather_weights")], []
    for l in range(L):
        nxt = (shard_bufs(0, _BIG[1:]) if l == 0 else []) + (shard_bufs(l + 1) if l + 1 < L else [])
        X, Ws[l], got, sv = layer_fwd(X, l, Ws[l], nxt)
        if got:
            Ws.append(unpack(got))
        saved.append(sv)
    dX, lpart = _loss_grad(X, loss_target[0], S)
    loss = lax.psum(lpart[0, 0], ("x", "y", "c"))

    out = {nm: () for nm in _BIG}
    smalls = [None] * L

    def own_block(r, g):
        return lax.dynamic_update_slice(r, lax.dynamic_slice_in_dim(g, chip, 1, axis=0), (chip, 0, 0))

    def update(l, names, grads):
        for nm, g in zip(names, grads):
            out[nm] = _adamw_layer(given[nm], given["m_" + nm], given["v_" + nm], g, l, out[nm])

    def plane_update(l, names, recv, sent):
        mine = [_sum_lead(own_block(r, g), "sum_chips") for r, g in zip(recv, sent)]
        update(l, names, zip(mine, _sibling_swap(mine, "swap_planes")))

    pending = []
    for l in reversed(range(L)):
        dX, big, smalls[l], recv = layer_bwd(dX, l, Ws[l], saved[l], pending)
        if recv:
            plane_update(l + 1, _BIG, recv[:len(pending)], pending)
            if l == 0:
                plane_update(0, _BIG[1:], recv[len(pending):], big[1:])
        pending = big
    last = big[:1]
    sib = _sibling_halves(last, "swap_halves")
    own = [lax.dynamic_slice_in_dim(g, mc * (g.shape[1] // 2), g.shape[1] // 2, axis=1) for g in last]
    sent = [_add_cast(g, s_) for g, s_ in zip(own, sib)]
    recv = [own_block(r, g) for r, g in zip(_chip_scatter(sent, "scatter_grads"), sent)]
    halves = [_sum_lead(r, "sum_chips") for r in recv]
    full = _sibling_fill(halves, "fill_halves")
    update(0, _BIG[:1], [(lax.dynamic_update_slice(f, hv, (mc * hv.shape[0], 0)),) for f, hv in zip(full, halves)])
    grad_x = dX[:S][None]

    def flat(nm):
        return jnp.stack([smalls[l][nm] for l in range(L)]).reshape(-1)

    dmod_all = jnp.stack([smalls[l]["dmod"] for l in range(L)])
    dml = dmod_all[:, 0].reshape(-1)
    dmc = dmod_all[:, 1].reshape(-1)
    names = ("norm1", "q_gain", "k_gain", "conv_w", "sg_norm", "w_s", "b_s", "norm2")
    parts = [dml, dml + dmc, dmc] + [flat(nm) for nm in names]
    sizes = [int(a.shape[0]) for a in parts]
    total = sum(sizes)
    padn = (-total) % (8 * LANES)
    sbuf = jnp.concatenate(parts + [jnp.zeros((padn,), F32)]).reshape(-1, LANES)
    g3 = _ag8(sbuf, "gather_small")
    ssum = _sum_lead(g3, "sum_devices").reshape(-1)
    offs = np.cumsum([0] + sizes)
    seg = {nm: ssum[offs[i + 3]:offs[i + 4]] for i, nm in enumerate(names)}
    gb_mod = ssum[offs[1]:offs[2]].reshape(L, N_CHIPS * n_mod)
    dmc_sum = ssum[offs[2]:offs[3]].reshape(L, N_CHIPS * n_mod)
    dml_all = g3.reshape(N_DEV, -1)[:, :sizes[0]].reshape(N_DEV, L, N_CHIPS * n_mod)
    dml_sh = jnp.swapaxes(lax.dynamic_slice_in_dim(dml_all, chip * n_mod, n_mod, axis=2), 0, 1)
    dmc_sh = lax.dynamic_slice_in_dim(dmc_sum, chip * n_mod, n_mod, axis=1)[:, None, :]
    dms = jnp.concatenate([dml_sh, dmc_sh, jnp.zeros((L, 7, n_mod), F32)], axis=1)
    g_wmod = _wmod_grad(a_raw, dms)
    part = _cctx_partial(jnp.concatenate([dmc_sh, jnp.zeros((L, 15, n_mod), F32)], axis=1), w_mod)
    g4 = _ag8(part.reshape(-1, LANES), "gather_cctx")
    g_cctx = _cctx_final(g4, c_ctx.reshape(8, LANES)).reshape(D)

    g_conv = lax.dynamic_slice_in_dim(seg["conv_w"].reshape(L, 3, N_CHIPS * n_cw), chip * n_cw, n_cw, axis=2)
    small_g = dict(c_ctx=g_cctx, w_mod=g_wmod, b_mod=gb_mod, norm1=seg["norm1"].reshape(norm1.shape),
                   q_gain=seg["q_gain"].reshape(q_gain.shape), k_gain=seg["k_gain"].reshape(k_gain.shape),
                   conv_w=g_conv, sg_norm=seg["sg_norm"].reshape(sg_norm.shape), w_s=seg["w_s"].reshape(w_s.shape),
                   b_s=seg["b_s"].reshape(b_s.shape), norm2=seg["norm2"].reshape(norm2.shape))
    res = {}
    for nm in _WEIGHTS:
        if nm in _BIG:
            res[nm] = out[nm]
        else:
            res[nm] = _adamw(given[nm], given["m_" + nm], given["v_" + nm], small_g[nm])
    return (loss, grad_x, *[res[nm][0] for nm in _WEIGHTS], *[res[nm][1] for nm in _WEIGHTS],
            *[res[nm][2] for nm in _WEIGHTS], *[res[nm][3] for nm in _WEIGHTS])
```

```python
import functools

import jax
import jax.numpy as jnp
import numpy as np
from jax import lax
from jax.experimental import pallas as pl
from jax.experimental.pallas import tpu as pltpu

F32 = jnp.float32
BF16 = jnp.bfloat16
EPS = 1e-6
LOG2E = 1.4426950408889634
D_MODEL = 1024
HEAD_DIM = 64
N_Q_HEADS = 8
N_KV_HEADS = 2
GROUP = N_Q_HEADS // N_KV_HEADS
GRID_W = 64
ROPE_THETA = 10000.0
CHUNK = 128
CONV_W = 256
SG_W = 256
OFF_Q = 3 * CONV_W
QKV_W = 768
OFF_U = OFF_Q + QKV_W
OFF_G = OFF_U + 2 * SG_W
IN_W = OFF_G + 3 * D_MODEL
N_CHIPS = 4
N_DEV = 8
LANES = 128
FWD_KEYS = 256
UNROLL_FWD = 8
UNROLL_BWD = 4
AUG = 3
ADAM_LR, ADAM_B1, ADAM_B2, ADAM_EPS, ADAM_WD, ADAM_STEP = 0.001, 0.9, 0.999, 1e-8, 0.01, 10
VMEM_LIMIT_V7X = 52 * 1024 * 1024
MXU_DEPTH_V7X = 256
MESH_ID = pl.DeviceIdType.MESH
NT = (((1,), (1,)), ((), ()))
TN = (((0,), (0,)), ((), ()))
ANY = pl.BlockSpec(memory_space=pl.ANY)


def _cp(*sem):
    return pltpu.CompilerParams(dimension_semantics=sem or None, vmem_limit_bytes=VMEM_LIMIT_V7X)


def _tile(n, target, mult=16):
    best = None
    for t in range(mult, n + 1, mult):
        if n % t == 0 and t <= target:
            best = t
    assert best is not None, (n, target, mult)
    return best


def _full(shape):
    nd = len(shape)
    return pl.BlockSpec(tuple(shape), lambda *_: (0,) * nd)


def _segments(i, tm, n_lat, fn):
    k, off = divmod(n_lat, tm)

    @pl.when(i < k)
    def _():
        fn(0, tm, 0)

    @pl.when(i == k)
    def _():
        if off:
            fn(0, off, 0)
        fn(off, tm, 1)

    @pl.when(i > k)
    def _():
        fn(0, tm, 1)


def _dot(a, b):
    return jnp.dot(a, b, preferred_element_type=F32)


def _dg(a, b, dims):
    return lax.dot_general(a, b, dims, preferred_element_type=F32)


def _split3(x):
    hi = x.astype(BF16)
    r1 = x - hi.astype(F32)
    mid = r1.astype(BF16)
    lo = (r1 - mid.astype(F32)).astype(BF16)
    return hi.astype(F32), mid.astype(F32), lo.astype(F32)


def _lane(shape):
    return lax.broadcasted_iota(jnp.int32, shape, len(shape) - 1)


def _aug(val, stat):
    lane = _lane(val.shape)
    hi, mid, lo = _split3(stat)
    ext = jnp.where(lane == 64, hi, jnp.where(lane == 65, mid, jnp.where(lane == 66, lo, 0.0)))
    return jnp.where(lane < 64, val, ext)


def _seg_mean(x, e):
    outs = []
    for g in range(x.shape[1] // LANES):
        blk = x[:, g * LANES:(g + 1) * LANES]
        hi = blk.astype(BF16)
        lo = (blk - hi.astype(F32)).astype(BF16)
        outs.append(_dot(hi, e) + _dot(lo, e))
    return outs[0] if len(outs) == 1 else jnp.concatenate(outs, axis=1)


def _rope(x, cos, sin_signed, inverse):
    w = x.shape[1]
    reps = w // LANES
    c = cos if reps == 1 else jnp.tile(cos, (1, reps))
    s = sin_signed if reps == 1 else jnp.tile(sin_signed, (1, reps))
    first = (_lane(x.shape) % 32) < 16
    partner = jnp.where(first, pltpu.roll(x, w - 16, 1), pltpu.roll(x, 16, 1))
    return x * c - partner * s if inverse else x * c + partner * s


def _sigmoid(x):
    return 1.0 / (1.0 + jnp.exp(-x))


_GELU_K = 0.7978845608028654
_GELU_C = 0.044715


def _gelu(x):
    return 0.5 * x * (1.0 + jnp.tanh(_GELU_K * (x + _GELU_C * x * x * x)))


def _gelu_grad(x):
    t = jnp.tanh(_GELU_K * (x + _GELU_C * x * x * x))
    return 0.5 * (1.0 + t) + 0.5 * x * (1.0 - t * t) * _GELU_K * (1.0 + 3.0 * _GELU_C * x * x)


def _loop_unrolled(n, step, init, unroll):
    def trip(t, carry):
        for u in range(unroll):
            carry = step(t * unroll + u, carry)
        return carry

    carry = lax.fori_loop(0, n // unroll, trip, init) if n >= unroll else init
    for r in range(n - n % unroll, n):
        carry = step(r, carry)
    return carry


def _heads_to_rows(x, n_heads):
    out = []
    for h in range(n_heads):
        grp = x[:, (h // 2) * LANES:(h // 2 + 1) * LANES]
        out.append(grp if h % 2 == 0 else pltpu.roll(grp, 64, 1))
    return out


def _rows_to_heads(blocks):
    outs = []
    lane = _lane(blocks[0].shape)
    for a in range(len(blocks) // 2):
        outs.append(jnp.where(lane < 64, blocks[2 * a], pltpu.roll(blocks[2 * a + 1], 64, 1)))
    return outs[0] if len(outs) == 1 else jnp.concatenate(outs, axis=1)


def _norm_mod(x, g, mod, i_shift, i_scale, n_lat):
    T, D = x.shape
    tm = _tile(T, 768, LANES)

    def body(x_ref, g_ref, mod_ref, h_ref, ht_ref):
        def fn(r0, r1, seg):
            xv = x_ref[r0:r1, :]
            r = lax.rsqrt(jnp.mean(xv * xv, axis=-1, keepdims=True) + EPS)
            n = xv * r * g_ref[...]
            h = n * (1.0 + mod_ref[seg, i_scale:i_scale + 1, :]) + mod_ref[seg, i_shift:i_shift + 1, :]
            h_ref[r0:r1, :] = h.astype(BF16)

        _segments(pl.program_id(0), tm, n_lat, fn)
        ht_ref[...] = h_ref[...].astype(F32).T.astype(BF16)

    return pl.pallas_call(
        body, grid=(T // tm,), name="norm_mod",
        in_specs=[pl.BlockSpec((tm, D), lambda i: (i, 0)), _full(g.shape), _full(mod.shape)],
        out_specs=[pl.BlockSpec((tm, D), lambda i: (i, 0)), pl.BlockSpec((D, tm), lambda i: (0, i))],
        out_shape=[jax.ShapeDtypeStruct((T, D), BF16), jax.ShapeDtypeStruct((D, T), BF16)],
        compiler_params=_cp("parallel"))(x, g, mod)


def _norm_mod_bwd(x, dh, dres, g, mod, i_scale, n_lat, gate=None):
    T, D = x.shape
    tm = _tile(T, 528)

    def body(x_ref, dh_ref, dres_ref, g_ref, mod_ref, *rest):
        if gate is None:
            dx_ref, dg_ref, dsh_ref, dsc_ref = rest
        else:
            f_ref, gmod_ref, dx_ref, dg_ref, dsh_ref, dsc_ref, dy_ref, dgt_ref = rest
        i = pl.program_id(0)

        @pl.when(i == 0)
        def _():
            dg_ref[...] = jnp.zeros_like(dg_ref)
            dsh_ref[...] = jnp.zeros_like(dsh_ref)
            dsc_ref[...] = jnp.zeros_like(dsc_ref)
            if gate is not None:
                dgt_ref[...] = jnp.zeros_like(dgt_ref)

        def fn(r0, r1, seg):
            xv = x_ref[r0:r1, :]
            dh = dh_ref[r0:r1, :]
            r = lax.rsqrt(jnp.mean(xv * xv, axis=-1, keepdims=True) + EPS)
            xh = xv * r
            gv = g_ref[...]
            dsh_ref[seg] += jnp.sum(dh, axis=0, keepdims=True)
            dsc_ref[seg] += jnp.sum(dh * (xh * gv), axis=0, keepdims=True)
            dn = dh * (1.0 + mod_ref[seg, i_scale:i_scale + 1, :])
            dg_ref[...] += jnp.sum(dn * xh, axis=0, keepdims=True)
            gd = gv * dn
            dxv = dres_ref[r0:r1, :] + r * (gd - xh * jnp.mean(xh * gd, axis=-1, keepdims=True))
            dx_ref[r0:r1, :] = dxv
            if gate is not None:
                dy_ref[r0:r1, :] = (dxv * gmod_ref[seg, gate[2]:gate[2] + 1, :]).astype(BF16)
                dgt_ref[seg] += jnp.sum(dxv * f_ref[r0:r1, :].astype(F32), axis=0, keepdims=True)

        _segments(i, tm, n_lat, fn)

    row = pl.BlockSpec((tm, D), lambda i: (i, 0))
    extra_in = [] if gate is None else [gate[0], gate[1]]
    return pl.pallas_call(
        body, grid=(T // tm,), name="norm_mod_bwd" if gate is None else "norm_gate_bwd",
        in_specs=[row, row, row, _full(g.shape), _full(mod.shape)] + ([] if gate is None else [row, _full(gate[1].shape)]),
        out_specs=[row, _full((1, D)), _full((2, 1, D)), _full((2, 1, D))] + ([] if gate is None else [row, _full((2, 1, D))]),
        out_shape=[jax.ShapeDtypeStruct((T, D), F32), jax.ShapeDtypeStruct((1, D), F32),
                   jax.ShapeDtypeStruct((2, 1, D), F32), jax.ShapeDtypeStruct((2, 1, D), F32)]
        + ([] if gate is None else [jax.ShapeDtypeStruct((T, D), BF16), jax.ShapeDtypeStruct((2, 1, D), F32)]),
        compiler_params=_cp("arbitrary"))(x, dh, dres, g, mod, *extra_in)


def _gate_bwd(dx, f, mod, i_gate, n_lat):
    T, D = dx.shape
    tm = _tile(T, 528)

    def body(dx_ref, f_ref, mod_ref, dy_ref, dg_ref):
        i = pl.program_id(0)

        @pl.when(i == 0)
        def _():
            dg_ref[...] = jnp.zeros_like(dg_ref)

        def fn(r0, r1, seg):
            dxv = dx_ref[r0:r1, :]
            dy_ref[r0:r1, :] = (dxv * mod_ref[seg, i_gate:i_gate + 1, :]).astype(BF16)
            dg_ref[seg] += jnp.sum(dxv * f_ref[r0:r1, :].astype(F32), axis=0, keepdims=True)

        _segments(i, tm, n_lat, fn)

    row = pl.BlockSpec((tm, D), lambda i: (i, 0))
    return pl.pallas_call(
        body, grid=(T // tm,), name="gate_bwd",
        in_specs=[row, row, _full(mod.shape)], out_specs=[row, _full((2, 1, D))],
        out_shape=[jax.ShapeDtypeStruct((T, D), BF16), jax.ShapeDtypeStruct((2, 1, D), F32)],
        compiler_params=_cp("arbitrary"))(dx, f, mod)


def _mm_nn(a, w, out_dtype, name):
    M, K = a.shape
    J, _, n = w.shape
    tm = _tile(M, 1056)

    def body(a_ref, w_ref, o_ref):
        o_ref[...] = _dot(a_ref[...], w_ref[...]).astype(o_ref.dtype)

    return pl.pallas_call(
        body, grid=(M // tm, J), name=name,
        in_specs=[pl.BlockSpec((tm, K), lambda i, j: (i, 0)), pl.BlockSpec((None, K, n), lambda i, j: (j, 0, 0))],
        out_specs=pl.BlockSpec((tm, n), lambda i, j: (i, j)),
        out_shape=jax.ShapeDtypeStruct((M, J * n), out_dtype), compiler_params=_cp("parallel", "arbitrary"))(a, w)


def _mm_res(a3, w, res, mod, i_gate, n_lat, name):
    J, M, k = a3.shape
    N = w.shape[2]
    tm = _tile(M, 528)

    def body(a_ref, w_ref, res_ref, mod_ref, x_ref, f_ref):
        acc = _dot(a_ref[0], w_ref[0])
        for j in range(1, J):
            acc += _dot(a_ref[j], w_ref[j])
        f_ref[...] = acc.astype(BF16)

        def fn(r0, r1, seg):
            x_ref[r0:r1, :] = res_ref[r0:r1, :] + mod_ref[seg, i_gate:i_gate + 1, :] * acc[r0:r1, :]

        _segments(pl.program_id(0), tm, n_lat, fn)

    row = pl.BlockSpec((tm, N), lambda i: (i, 0))
    return pl.pallas_call(
        body, grid=(M // tm,), name=name,
        in_specs=[pl.BlockSpec((J, tm, k), lambda i: (0, i, 0)), _full(w.shape), row, _full(mod.shape)],
        out_specs=[row, row],
        out_shape=[jax.ShapeDtypeStruct((M, N), F32), jax.ShapeDtypeStruct((M, N), BF16)],
        compiler_params=_cp("parallel"))(a3, w, res, mod)


def _mm_nt_acc(dys, ws, row_major, name):
    J, K, n = ws[0].shape
    M = dys[0].shape[0] if row_major else dys[0].shape[1]
    tm = _tile(M, 1056)
    P = len(dys)

    def body(*refs):
        o_ref = refs[2 * P]
        j = pl.program_id(1)
        part = _dg(refs[0][...], refs[P][...], NT)
        for p in range(1, P):
            part += _dg(refs[p][...], refs[P + p][...], NT)

        @pl.when(j == 0)
        def _():
            o_ref[...] = part

        @pl.when(j > 0)
        def _():
            o_ref[...] += part

    dy_spec = (pl.BlockSpec((tm, n), lambda i, j: (i, j)) if row_major
               else pl.BlockSpec((None, tm, n), lambda i, j: (j, i, 0)))
    w_spec = pl.BlockSpec((None, K, n), lambda i, j: (j, 0, 0))
    return pl.pallas_call(
        body, grid=(M // tm, J), name=name,
        in_specs=[dy_spec] * P + [w_spec] * P,
        out_specs=pl.BlockSpec((tm, K), lambda i, j: (i, 0)),
        out_shape=jax.ShapeDtypeStruct((M, K), F32), compiler_params=_cp("parallel", "arbitrary"))(*dys, *ws)


def _mm_tn(x, dy, x_spec, dy_spec, J, K, n, T, name):
    tk = _tile(T, 1056, MXU_DEPTH_V7X)
    nt = T // tk

    def body(x_ref, dy_ref, o_ref, acc):
        t = pl.program_id(1)
        part = _dg(x_ref[...], dy_ref[...], TN)

        @pl.when(t == 0)
        def _():
            acc[...] = part

        @pl.when(t > 0)
        def _():
            acc[...] += part

        @pl.when(t == nt - 1)
        def _():
            o_ref[...] = acc[...].astype(BF16)

    return pl.pallas_call(
        body, grid=(J, nt), name=name,
        in_specs=[x_spec(tk), dy_spec(tk)],
        out_specs=pl.BlockSpec((None, K, n), lambda j, t: (j, 0, 0)),
        out_shape=jax.ShapeDtypeStruct((J, K, n), BF16), scratch_shapes=[pltpu.VMEM((K, n), F32)],
        compiler_params=_cp("parallel", "arbitrary"))(x, dy)


def _mm_dw(xt, dy, dy_spec, J, n, name):
    K, T = xt.shape
    tk = _tile(T, 1056, MXU_DEPTH_V7X)
    nt = T // tk

    def body(xt_ref, dy_ref, o_ref, acc):
        t = pl.program_id(1)
        part = _dot(xt_ref[...], dy_ref[...])

        @pl.when(t == 0)
        def _():
            acc[...] = part

        @pl.when(t > 0)
        def _():
            acc[...] += part

        @pl.when(t == nt - 1)
        def _():
            o_ref[...] = acc[...].astype(BF16)

    return pl.pallas_call(
        body, grid=(J, nt), name=name,
        in_specs=[pl.BlockSpec((K, tk), lambda j, t: (0, t)), dy_spec(tk)],
        out_specs=pl.BlockSpec((None, K, n), lambda j, t: (j, 0, 0)),
        out_shape=jax.ShapeDtypeStruct((J, K, n), BF16), scratch_shapes=[pltpu.VMEM((K, n), F32)],
        compiler_params=_cp("parallel", "arbitrary"))(xt, dy)


def _rows(width):
    return lambda tk: pl.BlockSpec((tk, width), lambda j, t: (t, 0))


def _row_cols(width):
    return lambda tk: pl.BlockSpec((tk, width), lambda j, t: (t, j))


def _shard_rows(width):
    return lambda tk: pl.BlockSpec((None, tk, width), lambda j, t: (j, t, 0))


def _ffn_up(h, w1, w3):
    T, D = h.shape
    J, _, n = w1.shape
    tm = _tile(T, 1056)

    def body(h_ref, w1_ref, w3_ref, a1_ref, a3_ref, act_ref):
        hv = h_ref[...]
        a1 = _dot(hv, w1_ref[...])
        a3 = _dot(hv, w3_ref[...])
        a1_ref[...] = a1.astype(BF16)
        a3_ref[...] = a3.astype(BF16)
        act_ref[...] = (a1 * _sigmoid(a1) * a3).astype(BF16)

    w_spec = pl.BlockSpec((None, D, n), lambda i, j: (j, 0, 0))
    o_spec = pl.BlockSpec((None, tm, n), lambda i, j: (j, i, 0))
    return pl.pallas_call(
        body, grid=(T // tm, J), name="ffn_up",
        in_specs=[pl.BlockSpec((tm, D), lambda i, j: (i, 0)), w_spec, w_spec], out_specs=[o_spec] * 3,
        out_shape=[jax.ShapeDtypeStruct((J, T, n), BF16)] * 3,
        compiler_params=_cp("parallel", "arbitrary"))(h, w1, w3)


def _ffn_down_bwd(dy, w2, a1, a3):
    T, D = dy.shape
    J, n, _ = w2.shape
    tm = _tile(T, 1056)

    def body(dy_ref, w2_ref, a1_ref, a3_ref, da1_ref, da3_ref):
        dact = _dg(dy_ref[...], w2_ref[...], NT)
        a1v = a1_ref[...].astype(F32)
        sig = _sigmoid(a1v)
        da3_ref[...] = (dact * a1v * sig).astype(BF16)
        da1_ref[...] = (dact * a3_ref[...].astype(F32) * (sig * (1.0 + a1v * (1.0 - sig)))).astype(BF16)

    a_spec = pl.BlockSpec((None, tm, n), lambda i, j: (j, i, 0))
    return pl.pallas_call(
        body, grid=(T // tm, J), name="ffn_down_bwd",
        in_specs=[pl.BlockSpec((tm, D), lambda i, j: (i, 0)), pl.BlockSpec((None, n, D), lambda i, j: (j, 0, 0)),
                  a_spec, a_spec],
        out_specs=[a_spec, a_spec], out_shape=[jax.ShapeDtypeStruct((J, T, n), BF16)] * 2,
        compiler_params=_cp("parallel", "arbitrary"))(dy, w2, a1, a3)


def _qkv_prep(p, cos, sin, qg, kg, e):
    T = p.shape[0]
    tm = _tile(T, 528)

    def body(p_ref, cos_ref, sin_ref, qg_ref, kg_ref, e_ref, q_ref, k_ref, v_ref):
        ev = e_ref[...]
        cv, sv = cos_ref[...], sin_ref[...]
        xq = p_ref[:, 0:512]
        qn = xq * lax.rsqrt(_seg_mean(xq * xq, ev) + EPS) * qg_ref[...]
        qr = _rope(qn, cv, sv, False) * (HEAD_DIM ** -0.5 * LOG2E)
        xk = p_ref[:, 512:640]
        kn = xk * lax.rsqrt(_seg_mean(xk * xk, ev) + EPS) * kg_ref[...]
        kr = _rope(kn, cv, sv, False)
        lane = _lane((tm, LANES))
        ones = jnp.where(lane < 64 + AUG, -1.0, 0.0)
        for h, blk in enumerate(_heads_to_rows(qr, N_Q_HEADS)):
            q_ref[h] = jnp.where(lane < 64, blk, 0.0).astype(BF16)
        for h, blk in enumerate(_heads_to_rows(kr, N_KV_HEADS)):
            k_ref[h] = jnp.where(lane < 64, blk, ones).astype(BF16)
        for h, blk in enumerate(_heads_to_rows(p_ref[:, 640:768], N_KV_HEADS)):
            v_ref[h] = jnp.where(lane < 64, blk, ones).astype(BF16)

    tab = pl.BlockSpec((tm, LANES), lambda i: (i, 0))
    return pl.pallas_call(
        body, grid=(T // tm,), name="qkv_prep",
        in_specs=[pl.BlockSpec((tm, QKV_W), lambda i: (i, 1)), tab, tab, _full(qg.shape), _full(kg.shape),
                  _full(e.shape)],
        out_specs=[pl.BlockSpec((N_Q_HEADS, tm, LANES), lambda i: (0, i, 0)),
                   pl.BlockSpec((N_KV_HEADS, tm, LANES), lambda i: (0, i, 0)),
                   pl.BlockSpec((N_KV_HEADS, tm, LANES), lambda i: (0, i, 0))],
        out_shape=[jax.ShapeDtypeStruct((N_Q_HEADS, T, LANES), BF16),
                   jax.ShapeDtypeStruct((N_KV_HEADS, T, LANES), BF16),
                   jax.ShapeDtypeStruct((N_KV_HEADS, T, LANES), BF16)],
        compiler_params=_cp("parallel"))(p, cos, sin, qg, kg, e)


def _qkv_prep_bwd(dp, dq, dk, dv, p, cos, sin, qg, kg, e, fold):
    T = p.shape[0]
    tq = dq.shape[3] // GROUP
    tm = _tile(T, 768, tq)
    nt = T // tm

    def body(dp_in, dq_ref, dk_ref, dv_ref, p_ref, cos_ref, sin_ref, qg_ref, kg_ref, e_ref, fold_ref,
             dp_ref, dqg_ref, dkg_ref, accq, acck):
        del dp_in
        i = pl.program_id(0)

        @pl.when(i == 0)
        def _():
            accq[...] = jnp.zeros_like(accq)
            acck[...] = jnp.zeros_like(acck)

        ev = e_ref[...]
        cv, sv = cos_ref[...], sin_ref[...]

        def one(x, dr, gain, acc):
            r = lax.rsqrt(_seg_mean(x * x, ev) + EPS)
            xh = x * r
            dn = _rope(dr, cv, sv, True)
            acc[0:1, :] += jnp.sum(dn * xh, axis=0, keepdims=True)
            gd = gain * dn
            return r * (gd - xh * _seg_mean(xh * gd, ev))

        slabs = [[dq_ref[h, b].T for b in range(tm // tq)] for h in range(N_KV_HEADS)]
        heads = [jnp.concatenate([sl[g * tq:(g + 1) * tq] for sl in slabs[h]], axis=0)
                 for h in range(N_KV_HEADS) for g in range(GROUP)]
        dqr = _rows_to_heads(heads) * (HEAD_DIM ** -0.5)
        dkr = _rows_to_heads([dk_ref[h] for h in range(N_KV_HEADS)]) * (1.0 / LOG2E)
        dvv = _rows_to_heads([dv_ref[h] for h in range(N_KV_HEADS)])
        dp_ref[:, 0:512] = one(p_ref[:, 0:512], dqr, qg_ref[...], accq).astype(BF16)
        dp_ref[:, 512:640] = one(p_ref[:, 512:640], dkr, kg_ref[...], acck).astype(BF16)
        dp_ref[:, 640:768] = dvv.astype(BF16)

        @pl.when(i == nt - 1)
        def _():
            fv = fold_ref[...]
            dqg_ref[...] = jnp.dot(accq[...], fv, preferred_element_type=F32, precision=lax.Precision.HIGHEST)
            dkg_ref[...] = jnp.dot(acck[...], fv[0:LANES, :], preferred_element_type=F32,
                                   precision=lax.Precision.HIGHEST)

    tab = pl.BlockSpec((tm, LANES), lambda i: (i, 0))
    sec = pl.BlockSpec((tm, QKV_W), lambda i: (i, 1))
    return pl.pallas_call(
        body, grid=(nt,), name="qkv_prep_bwd",
        in_specs=[ANY, pl.BlockSpec((N_KV_HEADS, tm // tq, LANES, GROUP * tq), lambda i: (0, i, 0, 0)),
                  pl.BlockSpec((N_KV_HEADS, tm, LANES), lambda i: (0, i, 0)),
                  pl.BlockSpec((N_KV_HEADS, tm, LANES), lambda i: (0, i, 0)),
                  sec, tab, tab, _full(qg.shape), _full(kg.shape), _full(e.shape), _full(fold.shape)],
        out_specs=[sec, _full((8, LANES)), _full((8, LANES))],
        out_shape=[jax.ShapeDtypeStruct(dp.shape, BF16), jax.ShapeDtypeStruct((8, LANES), F32),
                   jax.ShapeDtypeStruct((8, LANES), F32)],
        scratch_shapes=[pltpu.VMEM((8, 512), F32), pltpu.VMEM((8, LANES), F32)],
        input_output_aliases={0: 0}, compiler_params=_cp("arbitrary"))(dp, dq, dk, dv, p, cos, sin, qg, kg, e, fold)


def _flash_fwd(q, k, v, n_lat, gather=()):
    _, _, T, _ = q.shape
    tq = tk = 256
    nq = T // tq
    M = GROUP * tq

    wide_k = FWD_KEYS if n_lat % FWD_KEYS == 0 else tk
    n_g = len(gather)

    def body(q_ref, k_ref, v_ref, *rest):
        o_ref, qa_ref = rest[n_g], rest[n_g + 1]
        i = pl.program_id(0)
        if n_g:
            bufs = rest[n_g + 2:2 * n_g + 2]
            start, wait = _plane_exchange(bufs, bufs, rest[-2], rest[-1], False)
            pl.when((i == 0) & (pl.program_id(1) == 0))(start)
        qv = q_ref[...].reshape(M, LANES)

        def step(r0, width, carry):
            m, acc = carry
            sc = _dg(qv, k_ref[pl.ds(r0, width), :], NT)
            m_new = jnp.maximum(m, jnp.max(sc, axis=1, keepdims=True))
            pr = jnp.exp2(sc - m_new)
            acc = jnp.exp2(m - m_new) * acc + _dot(pr.astype(BF16), v_ref[pl.ds(r0, width), :])
            return m_new, acc

        def wide(s, carry):
            return step(s * wide_k if isinstance(s, int) else pl.multiple_of(s * wide_k, wide_k), wide_k, carry)

        def finish(m, acc):
            den = -acc[:, 64:65]
            out = acc / den
            o_ref[...] = _rows_to_heads([out[g * tq:(g + 1) * tq] for g in range(GROUP)]).astype(BF16)
            qa_ref[...] = _aug(qv.astype(F32), m + jnp.log2(den)).astype(BF16).reshape(GROUP, tq, LANES)

        init = (jnp.full((M, 1), -1e30, F32), jnp.zeros((M, LANES), F32))

        @pl.when(i < n_lat // tq)
        def _():
            carry = _loop_unrolled(n_lat // wide_k, wide, init, UNROLL_FWD)
            for r0 in range(n_lat, T, tk):
                carry = step(r0, tk, carry)
            finish(*carry)

        @pl.when(i >= n_lat // tq)
        def _():
            carry = init
            for r0 in range(n_lat, T, tk):
                carry = step(r0, tk, carry)
            finish(*carry)

        if n_g:
            pl.when((i == nq - 1) & (pl.program_id(1) == N_KV_HEADS - 1))(wait)

    q_spec = pl.BlockSpec((None, GROUP, tq, LANES), lambda i, h: (h, 0, i, 0))
    kv_spec = pl.BlockSpec((None, T, LANES), lambda i, h: (h, 0, 0))
    sems = [pltpu.SemaphoreType.DMA((3 * n_g,))] * 2 if n_g else []
    return pl.pallas_call(
        body, grid=(nq, N_KV_HEADS), name="flash_fwd_gather" if n_g else "flash_fwd",
        in_specs=[q_spec, kv_spec, kv_spec] + [ANY] * n_g,
        out_specs=[pl.BlockSpec((tq, GROUP * HEAD_DIM), lambda i, h: (i, h)), q_spec] + [ANY] * n_g,
        out_shape=[jax.ShapeDtypeStruct((T, N_Q_HEADS * HEAD_DIM), BF16), jax.ShapeDtypeStruct(q.shape, BF16)]
        + [jax.ShapeDtypeStruct(b.shape, b.dtype) for b in gather],
        input_output_aliases={3 + a: 2 + a for a in range(n_g)}, scratch_shapes=sems,
        compiler_params=_cp("arbitrary", "arbitrary"))(q, k, v, *gather)


def _flash_bwd(qa, doa, k, v, n_lat, scatter=()):
    _, _, T, _ = qa.shape
    tq = tk = 256
    nkv = T // tk
    M = GROUP * tq

    n_s = len(scatter)

    def body(qa_hbm, doa_hbm, k_ref, v_ref, *rest):
        dq_hbm, dk_ref, dv_ref = rest[n_s:n_s + 3]
        q_sc, do_sc, dq_sc, sems = rest[2 * n_s + 3:2 * n_s + 7]
        h = pl.program_id(0)
        j = pl.program_id(1)
        if n_s:
            start, wait = _plane_exchange(rest[:n_s], rest[n_s + 3:2 * n_s + 3], rest[-2], rest[-1], True)
            pl.when((h == 0) & (j == 0))(start)

        @pl.when(j == 0)
        def _():
            c1 = pltpu.make_async_copy(qa_hbm.at[h], q_sc, sems.at[0])
            c2 = pltpu.make_async_copy(doa_hbm.at[h], do_sc, sems.at[1])
            c1.start()
            c2.start()
            dq_sc[...] = jnp.zeros_like(dq_sc)
            c1.wait()
            c2.wait()

        kb = k_ref[...]
        vb = v_ref[...]
        kbt = kb.astype(F32).T.astype(BF16)

        def step(i, carry):
            dk, dv = carry
            r0 = i * tq if isinstance(i, int) else pl.multiple_of(i * tq, tq)
            qv = q_sc[:, pl.ds(r0, tq), :].reshape(M, LANES)
            dov = do_sc[:, pl.ds(r0, tq), :].reshape(M, LANES)
            pr = jnp.exp2(_dg(kb, qv, NT))
            ds = (pr * _dg(vb, dov, NT)).astype(BF16)
            dv = dv + _dot(pr.astype(BF16), dov)
            dk = dk + _dot(ds, qv)
            dq_sc[i] += _dot(kbt, ds)
            return dk, dv

        z = jnp.zeros((tk, LANES), F32)
        carry = _loop_unrolled(n_lat // tq, step, (z, z), UNROLL_BWD)
        dk_ref[...] = carry[0]
        dv_ref[...] = carry[1]

        @pl.when(j >= n_lat // tk)
        def _():
            c = (dk_ref[...], dv_ref[...])
            for i in range(n_lat // tq, T // tq):
                c = step(i, c)
            dk_ref[...] = c[0]
            dv_ref[...] = c[1]

        @pl.when(j == nkv - 1)
        def _():
            c3 = pltpu.make_async_copy(dq_sc, dq_hbm.at[h], sems.at[2])
            c3.start()
            c3.wait()

        if n_s:
            pl.when((h == N_KV_HEADS - 1) & (j == nkv - 1))(wait)

    kv_spec = pl.BlockSpec((None, tk, LANES), lambda h, j: (h, j, 0))
    return pl.pallas_call(
        body, grid=(N_KV_HEADS, nkv), name="flash_bwd_scatter" if n_s else "flash_bwd",
        in_specs=[ANY, ANY, kv_spec, kv_spec] + [ANY] * n_s, out_specs=[ANY, kv_spec, kv_spec] + [ANY] * n_s,
        out_shape=[jax.ShapeDtypeStruct((N_KV_HEADS, T // tq, LANES, M), F32), jax.ShapeDtypeStruct(k.shape, F32),
                   jax.ShapeDtypeStruct(k.shape, F32)] + [jax.ShapeDtypeStruct(g.shape, g.dtype) for g in scatter],
        scratch_shapes=[pltpu.VMEM((GROUP, T, LANES), BF16), pltpu.VMEM((GROUP, T, LANES), BF16),
                        pltpu.VMEM((T // tq, LANES, M), F32), pltpu.SemaphoreType.DMA((3,))]
        + ([pltpu.SemaphoreType.DMA((3 * n_s,))] * 2 if n_s else []),
        compiler_params=_cp("arbitrary", "arbitrary"))(qa, doa, k, v, *scatter)


def _conv_masks(i, tm, n_lat, T):
    row = lax.broadcasted_iota(jnp.int32, (tm, 1), 0)
    g = row + i * tm
    return row, (g == 0) | (g == n_lat), (g == n_lat - 1) | (g == T - 1)


def _shift_rows(v, prev_row, next_row, row, first, last):
    tm = v.shape[0]
    down = jnp.where(row == 0, prev_row, pltpu.roll(v, 1, 0))
    up = jnp.where(row == tm - 1, next_row, pltpu.roll(v, tm - 1, 0))
    return jnp.where(first, 0.0, down), jnp.where(last, 0.0, up)


def _halo_specs(tm, T, width, col):
    nb = T // 8
    return (pl.BlockSpec((8, width), lambda i: (jnp.maximum(i * (tm // 8) - 1, 0), col)),
            pl.BlockSpec((8, width), lambda i: (jnp.minimum((i + 1) * (tm // 8), nb - 1), col)))


def _conv_fwd(p, cw, n_lat):
    T = p.shape[0]
    tm = _tile(T, 1056)

    def body(p_ref, pp_ref, pn_ref, cw_ref, o_ref):
        row, first, last = _conv_masks(pl.program_id(0), tm, n_lat, T)
        z = p_ref[:, 256:512] * p_ref[:, 512:768]
        zp = pp_ref[7:8, 256:512] * pp_ref[7:8, 512:768]
        zn = pn_ref[0:1, 256:512] * pn_ref[0:1, 512:768]
        zd, zu = _shift_rows(z, zp, zn, row, first, last)
        conv = cw_ref[0:1, :] * zd + cw_ref[1:2, :] * z + cw_ref[2:3, :] * zu
        o_ref[...] = (p_ref[:, 0:256] * conv).astype(BF16)

    prev, nxt = _halo_specs(tm, T, 768, 0)
    return pl.pallas_call(
        body, grid=(T // tm,), name="conv_fwd",
        in_specs=[pl.BlockSpec((tm, 768), lambda i: (i, 0)), prev, nxt, _full(cw.shape)],
        out_specs=pl.BlockSpec((tm, CONV_W), lambda i: (i, 0)),
        out_shape=jax.ShapeDtypeStruct((T, CONV_W), BF16), compiler_params=_cp("parallel"))(p, p, p, cw)


def _conv_bwd(dp, dy, p, cw, n_lat):
    T = p.shape[0]
    tm = _tile(T, 1056)

    def body(dp_in, dy_ref, dyp_ref, dyn_ref, p_ref, pp_ref, pn_ref, cw_ref, dp_ref, dcw_ref):
        del dp_in
        i = pl.program_id(0)

        @pl.when(i == 0)
        def _():
            dcw_ref[...] = jnp.zeros_like(dcw_ref)

        row, first, last = _conv_masks(i, tm, n_lat, T)
        ab, ac, ax = p_ref[:, 0:256], p_ref[:, 256:512], p_ref[:, 512:768]
        z = ac * ax
        zp = pp_ref[7:8, 256:512] * pp_ref[7:8, 512:768]
        zn = pn_ref[0:1, 256:512] * pn_ref[0:1, 512:768]
        zd, zu = _shift_rows(z, zp, zn, row, first, last)
        w0, w1, w2 = cw_ref[0:1, :], cw_ref[1:2, :], cw_ref[2:3, :]
        dy = dy_ref[...]
        dc = dy * ab
        dcd, dcu = _shift_rows(dc, dyp_ref[7:8, :] * pp_ref[7:8, 0:256], dyn_ref[0:1, :] * pn_ref[0:1, 0:256],
                               row, first, last)
        dz = w0 * dcu + w1 * dc + w2 * dcd
        dp_ref[:, 0:256] = (dy * (w0 * zd + w1 * z + w2 * zu)).astype(BF16)
        dp_ref[:, 256:512] = (dz * ax).astype(BF16)
        dp_ref[:, 512:768] = (dz * ac).astype(BF16)
        dcw_ref[0:1, :] += jnp.sum(dc * zd, axis=0, keepdims=True)
        dcw_ref[1:2, :] += jnp.sum(dc * z, axis=0, keepdims=True)
        dcw_ref[2:3, :] += jnp.sum(dc * zu, axis=0, keepdims=True)

    prev, nxt = _halo_specs(tm, T, 768, 0)
    dprev, dnxt = _halo_specs(tm, T, CONV_W, 0)
    sec = pl.BlockSpec((tm, 768), lambda i: (i, 0))
    return pl.pallas_call(
        body, grid=(T // tm,), name="conv_bwd",
        in_specs=[ANY, pl.BlockSpec((tm, CONV_W), lambda i: (i, 0)), dprev, dnxt, sec, prev, nxt, _full(cw.shape)],
        out_specs=[sec, _full((8, CONV_W))],
        out_shape=[jax.ShapeDtypeStruct(dp.shape, BF16), jax.ShapeDtypeStruct((8, CONV_W), F32)],
        input_output_aliases={0: 0}, compiler_params=_cp("arbitrary"))(dp, dy, dy, dy, p, p, p, cw)


def _gmlp_mix(bd_ref, vs, grp):
    out = jnp.zeros((2 * CHUNK, SG_W), F32)
    for g in range(4):
        out = jnp.where(grp == g, _dot(bd_ref[g], vs), out)
    return out


def _gmlp_fwd(p, sgn, bd, bias):
    T = p.shape[0]
    tm = _tile(T, 768, 2 * CHUNK)

    def body(p_ref, sgn_ref, bd_ref, bias_ref, o_ref):
        x = _gelu(p_ref[:, 256:512])
        vn = (x * lax.rsqrt(jnp.mean(x * x, axis=-1, keepdims=True) + EPS) * sgn_ref[...]).astype(BF16)
        grp = _lane((2 * CHUNK, SG_W)) // 64
        for s in range(tm // (2 * CHUNK)):
            rs = slice(s * 2 * CHUNK, (s + 1) * 2 * CHUNK)
            mixed = _gmlp_mix(bd_ref, vn[rs], grp) + bias_ref[...]
            o_ref[rs, :] = (_gelu(p_ref[rs, 0:256]) * mixed).astype(BF16)

    return pl.pallas_call(
        body, grid=(T // tm,), name="gmlp_fwd",
        in_specs=[pl.BlockSpec((tm, 2 * SG_W), lambda i: (i, 3)), _full(sgn.shape), _full(bd.shape),
                  _full(bias.shape)],
        out_specs=pl.BlockSpec((tm, SG_W), lambda i: (i, 0)),
        out_shape=jax.ShapeDtypeStruct((T, SG_W), BF16), compiler_params=_cp("parallel"))(p, sgn, bd, bias)


def _gmlp_bwd(dp, dy, p, sgn, bd, bdt, bias, gsum):
    T = p.shape[0]
    tm = _tile(T, 768, 2 * CHUNK)
    nt = T // tm
    C2 = 2 * CHUNK

    def body(dp_in, dy_ref, p_ref, sgn_ref, bd_ref, bdt_ref, bias_ref, gsum_ref,
             dp_ref, dsg_ref, dws_ref, dbs_ref, acc_w, acc_b):
        del dp_in
        i = pl.program_id(0)

        @pl.when(i == 0)
        def _():
            dsg_ref[...] = jnp.zeros_like(dsg_ref)
            acc_w[...] = jnp.zeros_like(acc_w)
            acc_b[...] = jnp.zeros_like(acc_b)

        u = p_ref[:, 0:256]
        sv = p_ref[:, 256:512]
        ug = _gelu(u)
        x = _gelu(sv)
        r = lax.rsqrt(jnp.mean(x * x, axis=-1, keepdims=True) + EPS)
        xh = x * r
        sg = sgn_ref[...]
        vn = (xh * sg).astype(BF16)
        grp = _lane((C2, SG_W)) // 64
        dug, dvn = [], []
        for s in range(tm // C2):
            rs = slice(s * C2, (s + 1) * C2)
            vs = vn[rs]
            dys = dy_ref[rs, :]
            dug.append(dys * (_gmlp_mix(bd_ref, vs, grp) + bias_ref[...]))
            dmix = dys * ug[rs]
            acc_b[...] += dmix
            dmb = dmix.astype(BF16)
            dvn.append(_gmlp_mix(bdt_ref, dmb, grp))
            for g in range(4):
                acc_w[g] += _dg(jnp.where(grp == g, dmb, jnp.zeros_like(dmb)), vs, NT)
        dug = jnp.concatenate(dug, axis=0)
        dvn = jnp.concatenate(dvn, axis=0)
        dsg_ref[...] += jnp.sum(dvn * xh, axis=0, keepdims=True)
        gd = sg * dvn
        dx = r * (gd - xh * jnp.mean(xh * gd, axis=-1, keepdims=True))
        dp_ref[:, 0:256] = (dug * _gelu_grad(u)).astype(BF16)
        dp_ref[:, 256:512] = (dx * _gelu_grad(sv)).astype(BF16)

        @pl.when(i == nt - 1)
        def _():
            for g in range(4):
                dws_ref[g] = acc_w[g, 0:CHUNK, 0:CHUNK] + acc_w[g, CHUNK:C2, CHUNK:C2]
            dbs_ref[...] = jnp.dot(acc_b[0:CHUNK, :] + acc_b[CHUNK:C2, :], gsum_ref[...],
                                   preferred_element_type=F32, precision=lax.Precision.HIGHEST)

    sec = pl.BlockSpec((tm, 2 * SG_W), lambda i: (i, 3))
    return pl.pallas_call(
        body, grid=(nt,), name="gmlp_bwd",
        in_specs=[ANY, pl.BlockSpec((tm, SG_W), lambda i: (i, 0)), sec, _full(sgn.shape), _full(bd.shape),
                  _full(bdt.shape), _full(bias.shape), _full(gsum.shape)],
        out_specs=[sec, _full((1, SG_W)), _full((4, CHUNK, CHUNK)), _full((CHUNK, LANES))],
        out_shape=[jax.ShapeDtypeStruct(dp.shape, BF16), jax.ShapeDtypeStruct((1, SG_W), F32),
                   jax.ShapeDtypeStruct((4, CHUNK, CHUNK), F32), jax.ShapeDtypeStruct((CHUNK, LANES), F32)],
        scratch_shapes=[pltpu.VMEM((4, C2, C2), F32), pltpu.VMEM((C2, SG_W), F32)],
        input_output_aliases={0: 0}, compiler_params=_cp("arbitrary"))(dp, dy, p, sgn, bd, bdt, bias, gsum)


def _merge_fwd(ya, at, yc, p, wa, wb, wc):
    T = p.shape[0]
    tm = _tile(T, 528)
    n = wa.shape[2]

    def body(ya_ref, at_ref, yc_ref, ga_ref, gb_ref, gc_ref, wa_ref, wb_ref, wc_ref, o_ref):
        yav, atv, ycv = ya_ref[...], at_ref[...], yc_ref[...]
        for j in range(N_CHIPS):
            cs = slice(j * n, (j + 1) * n)
            m = (_sigmoid(ga_ref[:, cs]) * _dot(yav, wa_ref[j]) + _sigmoid(gb_ref[:, cs]) * _dot(atv, wb_ref[j])
                 + _sigmoid(gc_ref[:, cs]) * _dot(ycv, wc_ref[j]))
            o_ref[:, cs] = m.astype(BF16)

    def rows(w, col=0):
        return pl.BlockSpec((tm, w), lambda i: (i, col))

    return pl.pallas_call(
        body, grid=(T // tm,), name="merge_fwd",
        in_specs=[rows(CONV_W), rows(512), rows(SG_W), rows(D_MODEL, 2), rows(D_MODEL, 3), rows(D_MODEL, 4),
                  _full(wa.shape), _full(wb.shape), _full(wc.shape)],
        out_specs=rows(D_MODEL), out_shape=jax.ShapeDtypeStruct((T, D_MODEL), BF16),
        compiler_params=_cp("parallel"))(ya, at, yc, p, p, p, wa, wb, wc)


def _merge_bwd(dyo, ya, at, yc, p, wa, wb, wc, wo):
    T = p.shape[0]
    tm = _tile(T, 528)
    n = wa.shape[2]

    def body(dyo_ref, ya_ref, at_ref, yc_ref, ga_ref, gb_ref, gc_ref, wa_ref, wb_ref, wc_ref, wo_ref,
             dp_ref, dya_ref, doa_ref, dyc_ref, dwa_ref, dwb_ref, dwc_ref):
        i = pl.program_id(0)

        @pl.when(i == 0)
        def _():
            dwa_ref[...] = jnp.zeros_like(dwa_ref)
            dwb_ref[...] = jnp.zeros_like(dwb_ref)
            dwc_ref[...] = jnp.zeros_like(dwc_ref)

        dp_ref[:, 0:OFF_G] = jnp.zeros((tm, OFF_G), BF16)
        dm = _dg(dyo_ref[...], wo_ref[...], NT)
        yav, atv, ycv = ya_ref[...], at_ref[...], yc_ref[...]
        dya = jnp.zeros((tm, CONV_W), F32)
        dat = jnp.zeros((tm, 512), F32)
        dyc = jnp.zeros((tm, SG_W), F32)
        for j in range(N_CHIPS):
            cs = slice(j * n, (j + 1) * n)
            dmj = dm[:, cs]
            for y_in, w_ref, g_ref, dw_ref, which in (
                    (yav, wa_ref, ga_ref, dwa_ref, 0), (atv, wb_ref, gb_ref, dwb_ref, 1),
                    (ycv, wc_ref, gc_ref, dwc_ref, 2)):
                sg = _sigmoid(g_ref[:, cs])
                y = _dot(y_in, w_ref[j])
                c0 = OFF_G + which * D_MODEL + j * n
                dp_ref[:, c0:c0 + n] = (dmj * y * sg * (1.0 - sg)).astype(BF16)
                dyb = (dmj * sg).astype(BF16)
                dw_ref[j] += _dg(y_in, dyb, TN)
                back = _dg(dyb, w_ref[j], NT)
                if which == 0:
                    dya = dya + back
                elif which == 1:
                    dat = dat + back
                else:
                    dyc = dyc + back
        dya_ref[...] = dya
        dyc_ref[...] = dyc
        prod = dat * atv.astype(F32)
        lane = _lane((tm, LANES))
        dat_rows = _heads_to_rows(dat, N_Q_HEADS)
        for h in range(N_Q_HEADS):
            grp = prod[:, (h // 2) * LANES:(h // 2 + 1) * LANES]
            keep = (lane < 64) if h % 2 == 0 else (lane >= 64)
            delta = jnp.sum(jnp.where(keep, grp, 0.0), axis=1, keepdims=True)
            doa_ref[h] = _aug(dat_rows[h], delta).astype(BF16)

    def rows(w, col=0):
        return pl.BlockSpec((tm, w), lambda i: (i, col))

    return pl.pallas_call(
        body, grid=(T // tm,), name="merge_bwd",
        in_specs=[rows(D_MODEL), rows(CONV_W), rows(512), rows(SG_W), rows(D_MODEL, 2), rows(D_MODEL, 3),
                  rows(D_MODEL, 4), _full(wa.shape), _full(wb.shape), _full(wc.shape), _full(wo.shape)],
        out_specs=[rows(IN_W), rows(CONV_W),
                   pl.BlockSpec((N_Q_HEADS, tm, LANES), lambda i: (0, i, 0)), rows(SG_W),
                   _full(wa.shape), _full(wb.shape), _full(wc.shape)],
        out_shape=[jax.ShapeDtypeStruct((T, IN_W), BF16)] + [
            jax.ShapeDtypeStruct((T, CONV_W), F32), jax.ShapeDtypeStruct((N_Q_HEADS, T, LANES), BF16),
            jax.ShapeDtypeStruct((T, SG_W), F32), jax.ShapeDtypeStruct(wa.shape, F32),
            jax.ShapeDtypeStruct(wb.shape, F32), jax.ShapeDtypeStruct(wc.shape, F32)],
        compiler_params=_cp("arbitrary"))(dyo, ya, at, yc, p, p, p, wa, wb, wc, wo)


def _loss_grad(xf, tgt, n_lat):
    T, D = xf.shape
    tm = _tile(np.gcd(n_lat, T), 512)
    nl = n_lat // tm

    def body(x_ref, t_ref, dy_ref, l_ref):
        i = pl.program_id(0)

        @pl.when(i == 0)
        def _():
            l_ref[...] = jnp.zeros_like(l_ref)

        @pl.when(i < nl)
        def _():
            err = x_ref[...] - t_ref[...]
            dy_ref[...] = err * (1.0 / D)
            sq = jnp.sum(jnp.sum(err * err, axis=1, keepdims=True), axis=0, keepdims=True)
            l_ref[...] += (0.5 / D) * sq

        @pl.when(i >= nl)
        def _():
            dy_ref[...] = jnp.zeros_like(dy_ref)

    return pl.pallas_call(
        body, grid=(T // tm,), name="loss_grad",
        in_specs=[pl.BlockSpec((tm, D), lambda i: (i, 0)), pl.BlockSpec((tm, D), lambda i: (jnp.minimum(i, nl - 1), 0))],
        out_specs=[pl.BlockSpec((tm, D), lambda i: (i, 0)), _full((8, LANES))],
        out_shape=[jax.ShapeDtypeStruct((T, D), F32), jax.ShapeDtypeStruct((8, LANES), F32)],
        compiler_params=_cp("arbitrary"))(xf, tgt)


def _row_tile(R, C):
    if R * C <= (1 << 19) or R % 8:
        return R
    return _tile(R, max(8, (1 << 19) // C), 8)


def _adamw(w, m, v, g1, g2=None):
    shape = w.shape
    C = shape[-1]
    R = int(np.prod(shape[:-1])) if len(shape) > 1 else 1
    tr = _row_tile(R, C)
    ins = [a.reshape(R, C) for a in ((w, m, v, g1) if g2 is None else (w, m, v, g1, g2))]

    def body(*refs):
        w_ref, m_ref, v_ref = refs[0], refs[1], refs[2]
        g_ref, d_ref, m2_ref, v2_ref = refs[-4:]
        g = refs[3][...] if g2 is None else refs[3][...] + refs[4][...]
        m2 = ADAM_B1 * m_ref[...] + (1.0 - ADAM_B1) * g
        v2 = ADAM_B2 * v_ref[...] + (1.0 - ADAM_B2) * (g * g)
        m_hat = m2 / (1.0 - ADAM_B1 ** ADAM_STEP)
        v_hat = v2 / (1.0 - ADAM_B2 ** ADAM_STEP)
        g_ref[...] = g
        d_ref[...] = -ADAM_LR * (m_hat / (jnp.sqrt(v_hat) + ADAM_EPS) + ADAM_WD * w_ref[...])
        m2_ref[...] = m2
        v2_ref[...] = v2

    spec = pl.BlockSpec((tr, C), lambda i: (i, 0))
    outs = pl.pallas_call(
        body, grid=(R // tr,), name="adamw", in_specs=[spec] * len(ins), out_specs=[spec] * 4,
        out_shape=[jax.ShapeDtypeStruct((R, C), F32)] * 4, compiler_params=_cp("parallel"))(*ins)
    return [o.reshape(shape) for o in outs]


def _adamw_layer(w, m, v, gs, l, prev):
    L, a, b = w.shape
    tr = _row_tile(a, b)
    n_in = 3 + len(gs)

    def body(*refs):
        g_ref, d_ref, m2_ref, v2_ref = refs[-4:]
        g = refs[3][...]
        for r in refs[4:3 + len(gs)]:
            g = g + r[...]
        m2 = ADAM_B1 * refs[1][...] + (1.0 - ADAM_B1) * g
        v2 = ADAM_B2 * refs[2][...] + (1.0 - ADAM_B2) * (g * g)
        m_hat = m2 / (1.0 - ADAM_B1 ** ADAM_STEP)
        v_hat = v2 / (1.0 - ADAM_B2 ** ADAM_STEP)
        g_ref[...] = g
        d_ref[...] = -ADAM_LR * (m_hat / (jnp.sqrt(v_hat) + ADAM_EPS) + ADAM_WD * refs[0][...])
        m2_ref[...] = m2
        v2_ref[...] = v2

    layer = pl.BlockSpec((None, tr, b), lambda i: (l, i, 0))
    outs = pl.pallas_call(
        body, grid=(a // tr,), name="adamw_layer",
        in_specs=[layer] * 3 + [pl.BlockSpec((tr, b), lambda i: (i, 0))] * len(gs) + [ANY] * len(prev),
        out_specs=[layer] * 4, out_shape=[jax.ShapeDtypeStruct((L, a, b), F32)] * 4,
        input_output_aliases={n_in + k: k for k in range(len(prev))},
        compiler_params=_cp("parallel"))(w, m, v, *gs, *prev)
    return list(outs)


def _sum_lead(x, name):
    n, R, C = x.shape
    tr = _row_tile(R, C * n)

    def body(x_ref, o_ref):
        acc = x_ref[0].astype(F32)
        for s in range(1, n):
            acc = acc + x_ref[s].astype(F32)
        o_ref[...] = acc

    return pl.pallas_call(
        body, grid=(R // tr,), name=name, in_specs=[pl.BlockSpec((n, tr, C), lambda i: (0, i, 0))],
        out_specs=pl.BlockSpec((tr, C), lambda i: (i, 0)), out_shape=jax.ShapeDtypeStruct((R, C), F32),
        compiler_params=_cp("parallel"))(x)


def _silu(x):
    return x * _sigmoid(x)


def _mod_fwd(a_raw, w_mod, bsh):
    L, D, n = w_mod.shape

    def body(a_ref, w_ref, b_ref, o_ref):
        o_ref[...] = _dot(_silu(a_ref[...]).astype(BF16), w_ref[...].astype(BF16)) + b_ref[...]

    return pl.pallas_call(
        body, grid=(L,), name="mod_fwd",
        in_specs=[_full(a_raw.shape), pl.BlockSpec((None, D, n), lambda l: (l, 0, 0)),
                  pl.BlockSpec((None, 1, n), lambda l: (l, 0, 0))],
        out_specs=pl.BlockSpec((None, 16, n), lambda l: (l, 0, 0)),
        out_shape=jax.ShapeDtypeStruct((L, 16, n), F32), compiler_params=_cp("parallel"))(a_raw, w_mod, bsh)


def _wmod_grad(a_raw, dms):
    L, _, n = dms.shape
    D = a_raw.shape[1]

    def body(a_ref, dm_ref, o_ref):
        o_ref[...] = _dg(_silu(a_ref[...]).astype(BF16), dm_ref[...].astype(BF16), TN)

    return pl.pallas_call(
        body, grid=(L,), name="wmod_grad",
        in_specs=[_full(a_raw.shape), pl.BlockSpec((None, 16, n), lambda l: (l, 0, 0))],
        out_specs=pl.BlockSpec((None, D, n), lambda l: (l, 0, 0)),
        out_shape=jax.ShapeDtypeStruct((L, D, n), F32), compiler_params=_cp("parallel"))(a_raw, dms)


def _cctx_partial(dmc, w_mod):
    L, D, n = w_mod.shape

    def body(dm_ref, w_ref, o_ref):
        part = _dg(dm_ref[...].astype(BF16), w_ref[...].astype(BF16), NT)

        @pl.when(pl.program_id(0) == 0)
        def _():
            o_ref[...] = part

        @pl.when(pl.program_id(0) > 0)
        def _():
            o_ref[...] += part

    return pl.pallas_call(
        body, grid=(L,), name="cctx_partial",
        in_specs=[pl.BlockSpec((None, 16, n), lambda l: (l, 0, 0)), pl.BlockSpec((None, D, n), lambda l: (l, 0, 0))],
        out_specs=_full((16, D)), out_shape=jax.ShapeDtypeStruct((16, D), F32),
        compiler_params=_cp("arbitrary"))(dmc, w_mod)


def _cctx_final(parts, cc):
    def body(p_ref, c_ref, o_ref):
        s = p_ref[0, 0:8, :]
        for j in range(1, N_CHIPS):
            s = s + p_ref[2 * j, 0:8, :]
        xv = c_ref[...]
        sg = _sigmoid(xv)
        o_ref[...] = s * (sg * (1.0 + xv * (1.0 - sg)))

    return pl.pallas_call(
        body, name="cctx_final", in_specs=[_full(parts.shape), _full(cc.shape)], out_specs=_full((8, LANES)),
        out_shape=jax.ShapeDtypeStruct((8, LANES), F32), compiler_params=_cp())(parts, cc)


def _me():
    return lax.axis_index("x"), lax.axis_index("y"), lax.axis_index("c")


def _flip(v, bit):
    return 1 - v if bit else v


def _remote(src, dst, ssem, rsem, peer):
    return pltpu.make_async_remote_copy(src_ref=src, dst_ref=dst, send_sem=ssem, recv_sem=rsem,
                                        device_id=peer, device_id_type=MESH_ID)


def _ag8(xb, name):
    R = xb.shape[0]

    def pallas(x):
        def body(x_ref, o_ref, ssem, rsem):
            mx, my, mc = _me()
            me = 4 * mx + 2 * my + mc
            sib = (mx, my, 1 - mc)
            peers = _plane_peers(mx, my, mc)
            sends = [_remote(x_ref, o_ref.at[me], ssem.at[0], rsem.at[0], sib)]
            sends += [_remote(x_ref, o_ref.at[me], ssem.at[1 + k], rsem.at[1 + k], peer)
                      for k, (peer, _) in enumerate(peers)]
            for cp in sends:
                cp.start()
            for k, (peer, pj) in enumerate(peers):
                got = o_ref.at[2 * pj + mc]
                _remote(got, got, ssem.at[1 + k], rsem.at[1 + k], peer).wait_recv()
                fw = _remote(got, got, ssem.at[4 + k], rsem.at[4 + k], sib)
                fw.start()
                sends.append(fw)
            _remote(x_ref, o_ref.at[4 * mx + 2 * my + 1 - mc], ssem.at[0], rsem.at[0], sib).wait_recv()
            for k, (_, pj) in enumerate(peers):
                theirs = o_ref.at[2 * pj + 1 - mc]
                _remote(theirs, theirs, ssem.at[4 + k], rsem.at[4 + k], sib).wait_recv()
            for cp in sends:
                cp.wait_send()

        return pl.pallas_call(
            body, name=name, in_specs=[ANY], out_specs=ANY, out_shape=jax.ShapeDtypeStruct((N_DEV, R, LANES), F32),
            scratch_shapes=[pltpu.SemaphoreType.DMA((N_DEV - 1,)), pltpu.SemaphoreType.DMA((N_DEV - 1,))])(x)

    mx, my, mc = _me()
    return lax.dynamic_update_slice(pallas(xb), xb[None], (4 * mx + 2 * my + mc, 0, 0))


def _plane_peers(mx, my, mc):
    out = []
    for k in range(1, N_CHIPS):
        px, py = _flip(mx, k & 2), _flip(my, k & 1)
        out.append(((px, py, mc), 2 * px + py))
    return out


def _plane_exchange(ins, outs, ssem, rsem, scatter):
    n = len(ins)

    def desc(k, a, arriving):
        mx, my, mc = _me()
        j = 2 * mx + my
        peer, pj = _plane_peers(mx, my, mc)[k]
        src = ins[a].at[pj if scatter else j]
        dst = outs[a].at[pj if arriving else j]
        return _remote(src, dst, ssem.at[k * n + a], rsem.at[k * n + a], peer)

    def start():
        for k in range(N_CHIPS - 1):
            for a in range(n):
                desc(k, a, False).start()

    def wait():
        for k in range(N_CHIPS - 1):
            for a in range(n):
                desc(k, a, True).wait_recv()
        for k in range(N_CHIPS - 1):
            for a in range(n):
                desc(k, a, False).wait_send()

    return start, wait


def _chip_gather(bufs, name):
    n = len(bufs)
    halves = [b.shape[1] // 2 for b in bufs]

    def body(*refs):
        outs = refs[n:2 * n]
        ssem, rsem, fsem, gsem = refs[2 * n:]
        mx, my, mc = _me()
        j = 2 * mx + my
        sib = (mx, my, 1 - mc)

        def half(a, blk, c):
            return outs[a].at[blk, pl.ds(c * halves[a], halves[a]), :]

        peers = _plane_peers(mx, my, mc)
        sends = []
        for k, (peer, _) in enumerate(peers):
            for a in range(n):
                mine = half(a, j, mc)
                cp = _remote(mine, mine, ssem.at[k * n + a], rsem.at[k * n + a], peer)
                cp.start()
                sends.append(cp)
        for k, (peer, pj) in enumerate(peers):
            for a in range(n):
                got = half(a, pj, mc)
                _remote(got, got, ssem.at[k * n + a], rsem.at[k * n + a], peer).wait_recv()
                fw = _remote(got, got, fsem.at[k * n + a], gsem.at[k * n + a], sib)
                fw.start()
                sends.append(fw)
        for k, (_, pj) in enumerate(peers):
            for a in range(n):
                theirs = half(a, pj, 1 - mc)
                _remote(theirs, theirs, fsem.at[k * n + a], gsem.at[k * n + a], sib).wait_recv()
        for cp in sends:
            cp.wait_send()

    sems = pltpu.SemaphoreType.DMA((3 * n,))
    return pl.pallas_call(
        body, name=name, in_specs=[ANY] * n, out_specs=[ANY] * n,
        out_shape=[jax.ShapeDtypeStruct(b.shape, b.dtype) for b in bufs],
        input_output_aliases={a: a for a in range(n)},
        scratch_shapes=[sems, sems, sems, sems])(*bufs)


def _chip_scatter(gs, name):
    n = len(gs)

    def body(*refs):
        ins, outs = refs[:n], refs[n:2 * n]
        ssem, rsem = refs[2 * n:]
        mx, my, mc = _me()
        j = 2 * mx + my
        peers = _plane_peers(mx, my, mc)
        sends = []
        for k, (peer, pj) in enumerate(peers):
            for a in range(n):
                cp = _remote(ins[a].at[pj], outs[a].at[j], ssem.at[k * n + a], rsem.at[k * n + a], peer)
                cp.start()
                sends.append(cp)
        for k, (peer, pj) in enumerate(peers):
            for a in range(n):
                _remote(ins[a].at[pj], outs[a].at[pj], ssem.at[k * n + a], rsem.at[k * n + a], peer).wait_recv()
        for cp in sends:
            cp.wait_send()

    return pl.pallas_call(
        body, name=name, in_specs=[ANY] * n, out_specs=[ANY] * n,
        out_shape=[jax.ShapeDtypeStruct(g.shape, g.dtype) for g in gs],
        scratch_shapes=[pltpu.SemaphoreType.DMA((3 * n,)), pltpu.SemaphoreType.DMA((3 * n,))])(*gs)


def _sibling_swap(xs, name):
    n = len(xs)

    def body(*refs):
        ins, outs = refs[:n], refs[n:2 * n]
        ssem, rsem = refs[2 * n:]
        mx, my, mc = _me()
        cps = [_remote(ins[a], outs[a], ssem.at[a], rsem.at[a], (mx, my, 1 - mc)) for a in range(n)]
        for cp in cps:
            cp.start()
        for cp in cps:
            cp.wait()

    return pl.pallas_call(
        body, name=name, in_specs=[ANY] * n, out_specs=[ANY] * n,
        out_shape=[jax.ShapeDtypeStruct(x.shape, x.dtype) for x in xs],
        scratch_shapes=[pltpu.SemaphoreType.DMA((n,)), pltpu.SemaphoreType.DMA((n,))])(*xs)


def _sibling_halves(gs, name):
    n = len(gs)

    def body(*refs):
        ins, outs = refs[:n], refs[n:2 * n]
        ssem, rsem = refs[2 * n:]
        mx, my, mc = _me()
        cps = []
        for a in range(n):
            h = gs[a].shape[1] // 2
            cps.append(_remote(ins[a].at[:, pl.ds((1 - mc) * h, h), :], outs[a], ssem.at[a], rsem.at[a],
                               (mx, my, 1 - mc)))
        for cp in cps:
            cp.start()
        for cp in cps:
            cp.wait()

    return pl.pallas_call(
        body, name=name, in_specs=[ANY] * n, out_specs=[ANY] * n,
        out_shape=[jax.ShapeDtypeStruct((g.shape[0], g.shape[1] // 2, g.shape[2]), g.dtype) for g in gs],
        scratch_shapes=[pltpu.SemaphoreType.DMA((n,)), pltpu.SemaphoreType.DMA((n,))])(*gs)


def _sibling_fill(hs, name):
    n = len(hs)

    def body(*refs):
        ins, outs = refs[:n], refs[n:2 * n]
        ssem, rsem = refs[2 * n:]
        mx, my, mc = _me()
        cps = []
        for a in range(n):
            h = hs[a].shape[0]
            cps.append(_remote(ins[a], outs[a].at[pl.ds(mc * h, h), :], ssem.at[a], rsem.at[a], (mx, my, 1 - mc)))
        for cp in cps:
            cp.start()
        for a, cp in enumerate(cps):
            h = hs[a].shape[0]
            theirs = outs[a].at[pl.ds((1 - mc) * h, h), :]
            _remote(ins[a], theirs, ssem.at[a], rsem.at[a], (mx, my, 1 - mc)).wait_recv()
            cp.wait_send()

    return pl.pallas_call(
        body, name=name, in_specs=[ANY] * n, out_specs=[ANY] * n,
        out_shape=[jax.ShapeDtypeStruct((2 * x.shape[0], x.shape[1]), x.dtype) for x in hs],
        scratch_shapes=[pltpu.SemaphoreType.DMA((n,)), pltpu.SemaphoreType.DMA((n,))])(*hs)


def _add_cast(g, sb):
    J, h, b = g.shape
    th = _row_tile(h, b * J)

    def body(g_ref, s_ref, o_ref):
        o_ref[...] = (g_ref[...].astype(F32) + s_ref[...].astype(F32)).astype(BF16)

    spec = pl.BlockSpec((J, th, b), lambda i: (0, i, 0))
    return pl.pallas_call(
        body, grid=(h // th,), name="add_planes", in_specs=[spec, spec], out_specs=spec,
        out_shape=jax.ShapeDtypeStruct((J, h, b), BF16), compiler_params=_cp("parallel"))(g, sb)


_WEIGHTS = ("c_ctx", "w_mod", "b_mod", "norm1", "w_in", "q_gain", "k_gain", "conv_w", "sg_norm", "w_s", "b_s",
            "w_a", "w_b", "w_c", "w_o", "norm2", "w_ff1", "w_ff3", "w_ff2")
_BIG = ("w_in", "w_a", "w_b", "w_c", "w_o", "w_ff1", "w_ff3", "w_ff2")


def _constants():
    idx = np.arange(LANES)
    e = (idx[:, None] // 64 == idx[None, :] // 64).astype(np.float32) / 64.0
    c512 = np.arange(512)
    fold = (c512[:, None] % 64 == idx[None, :]).astype(np.float32)
    c256 = np.arange(SG_W)
    gsum = (c256[:, None] // 64 == idx[None, :]).astype(np.float32)
    return jnp.asarray(e, BF16), jnp.asarray(fold, F32), jnp.asarray(gsum, F32)


def _rope_tables(n_lat, n_ctx):
    t = jnp.arange(n_lat)
    inv = ROPE_THETA ** (-jnp.arange(0, HEAD_DIM // 2, 2, dtype=F32) / (HEAD_DIM // 2))
    ar = (t // GRID_W).astype(F32)[:, None] * inv
    ac = (t % GRID_W).astype(F32)[:, None] * inv
    cos = jnp.concatenate([jnp.cos(ar), jnp.cos(ar), jnp.cos(ac), jnp.cos(ac)], axis=1)
    sin = jnp.concatenate([-jnp.sin(ar), jnp.sin(ar), -jnp.sin(ac), jnp.sin(ac)], axis=1)
    cos = jnp.concatenate([cos, jnp.ones((n_ctx, HEAD_DIM), F32)], axis=0)
    sin = jnp.concatenate([sin, jnp.zeros((n_ctx, HEAD_DIM), F32)], axis=0)
    return jnp.concatenate([cos, cos], axis=1), jnp.concatenate([sin, sin], axis=1)


def kernel(x, c, ctx, c_ctx, w_mod, b_mod, norm1, w_in, q_gain, k_gain, conv_w, sg_norm, w_s, b_s, w_a, w_b, w_c, w_o, norm2, w_ff1, w_ff3, w_ff2, loss_target, m_c_ctx, m_w_mod, m_b_mod, m_norm1, m_w_in, m_q_gain, m_k_gain, m_conv_w, m_sg_norm, m_w_s, m_b_s, m_w_a, m_w_b, m_w_c, m_w_o, m_norm2, m_w_ff1, m_w_ff3, m_w_ff2, v_c_ctx, v_w_mod, v_b_mod, v_norm1, v_w_in, v_q_gain, v_k_gain, v_conv_w, v_sg_norm, v_w_s, v_b_s, v_w_a, v_w_b, v_w_c, v_w_o, v_norm2, v_w_ff1, v_w_ff3, v_w_ff2):
    given = dict(locals())
    mx, my, mc = _me()
    chip = 2 * mx + my
    dev = 4 * mx + 2 * my + mc
    L = norm1.shape[0]
    S, Lc = x.shape[1], ctx.shape[1]
    T = S + Lc
    D = D_MODEL
    n_mod, n_in, n_ff = w_mod.shape[2], w_in.shape[2], w_ff1.shape[2]
    n_cw = conv_w.shape[2]
    e_avg, fold, gsum = _constants()
    cos_t, sin_t = _rope_tables(S, Lc)

    cw_rows = (L * 3 * n_cw) // LANES
    pad = (-(8 + cw_rows)) % 8
    buf = jnp.concatenate([c.reshape(8, LANES), conv_w.reshape(cw_rows, LANES), jnp.zeros((pad, LANES), F32)], axis=0)
    g1 = _ag8(buf, "gather_cond")
    conds = g1[:, :8].reshape(N_DEV, D)
    cw_full = jnp.stack([g1[2 * j, 8:8 + cw_rows].reshape(L, 3, n_cw) for j in range(N_CHIPS)], axis=2)
    cw_full = cw_full.reshape(L, 3, N_CHIPS * n_cw)
    cw8 = jnp.pad(cw_full, ((0, 0), (0, 5), (0, 0)))
    a_raw = jnp.concatenate([conds, c_ctx[None], jnp.zeros((7, D), F32)], axis=0)
    bsh = lax.dynamic_slice_in_dim(b_mod, chip * n_mod, n_mod, axis=1)[:, None, :]
    mod_sh = _mod_fwd(a_raw, w_mod, bsh)
    g2 = _ag8(mod_sh.reshape(-1, LANES), "gather_mod")
    mods = jnp.stack([g2[2 * j].reshape(L, 16, n_mod) for j in range(N_CHIPS)], axis=2).reshape(L, 16, N_CHIPS * n_mod)
    lat = lax.dynamic_index_in_dim(mods, dev, axis=1, keepdims=False)
    mod = jnp.stack([lat.reshape(L, 6, D), mods[:, 8].reshape(L, 6, D)], axis=1)
    mod = jnp.pad(mod, ((0, 0), (0, 0), (0, 2), (0, 0)))

    qg = jnp.tile(q_gain, (1, N_Q_HEADS))[:, None, :]
    kg = jnp.tile(k_gain, (1, N_KV_HEADS))[:, None, :]
    sgn = sg_norm[:, None, :]
    ws_b = w_s.astype(BF16)
    zero = jnp.zeros_like(ws_b)
    bd = jnp.concatenate([jnp.concatenate([ws_b, zero], axis=3), jnp.concatenate([zero, ws_b], axis=3)], axis=2)
    bdt = jnp.swapaxes(bd, 2, 3)
    bias = jnp.tile(jnp.repeat(jnp.swapaxes(b_s, 1, 2), SG_W // 4, axis=2), (1, 2, 1))

    def shard_bufs(l, names=_BIG):
        return [lax.dynamic_update_slice(lax.empty((N_CHIPS,) + given[nm].shape[1:], BF16),
                                         given[nm][l].astype(BF16)[None], (chip, 0, 0)) for nm in names]

    def unpack(bufs):
        win, wa, wb, wc, wo, w1, w3, w2 = bufs
        return win, wa, wb, wc, wo.reshape(1, D, D), w1, w3, w2

    def layer_fwd(X, l, W, nxt):
        win = W[0]
        h, ht = _norm_mod(X, norm1[l][None], mod[l], 0, 1, S)
        p = _mm_nn(h, win, F32, "in_proj")
        ya = _conv_fwd(p, cw8[l], S)
        q, k, v = _qkv_prep(p, cos_t, sin_t, qg[l], kg[l], e_avg)
        at, qa, *got = _flash_fwd(q.reshape(N_KV_HEADS, GROUP, T, LANES), k, v, S, nxt)
        if len(W) == 1:
            W, got = unpack([win] + got[:len(_BIG) - 1]), got[len(_BIG) - 1:]
        win, wa, wb, wc, wo, w1, w3, w2 = W
        yc = _gmlp_fwd(p, sgn[l], bd[l], bias[l])
        mg = _merge_fwd(ya, at, yc, p, wa, wb, wc)
        X1, f1 = _mm_res(mg[None], wo, X, mod[l], 2, S, "out_proj")
        h2, h2t = _norm_mod(X1, norm2[l][None], mod[l], 3, 4, S)
        a1, a3, act = _ffn_up(h2, w1, w3)
        X2, f2 = _mm_res(act, w2, X1, mod[l], 5, S, "ffn_down")
        return X2, W, got, dict(X=X, ht=ht, h2t=h2t, p=p, ya=ya, k=k, v=v, at=at, qa=qa, yc=yc, mg=mg, X1=X1, f1=f1,
                             a1=a1, a3=a3, act=act, f2=f2)

    def layer_bwd(dX2, dyf, l, W, sv, pending):
        win, wa, wb, wc, wo, w1, w3, w2 = W
        da1, da3 = _ffn_down_bwd(dyf, w2, sv["a1"], sv["a3"])
        dw2 = _mm_tn(sv["act"], dyf, _shard_rows(n_ff), _rows(D), N_CHIPS, n_ff, D, T, "dw_ff2")
        dh2 = _mm_nt_acc([da1, da3], [w1, w3], False, "ffn_up_bwd")
        dw1 = _mm_dw(sv["h2t"], da1, _shard_rows(n_ff), N_CHIPS, n_ff, "dw_ff1")
        dw3 = _mm_dw(sv["h2t"], da3, _shard_rows(n_ff), N_CHIPS, n_ff, "dw_ff3")
        dX1, dn2, dsh2, dsc2, dyo, dgt1 = _norm_mod_bwd(sv["X1"], dh2, dX2, norm2[l][None], mod[l], 4, S,
                                                        (sv["f1"], mod[l], 2))
        dwo = _mm_tn(sv["mg"], dyo, _rows(D), _rows(D), 1, D, D, T, "dw_o")
        dp, dya, doa, dyc, dwa, dwb, dwc = _merge_bwd(dyo, sv["ya"], sv["at"], sv["yc"], sv["p"], wa, wb, wc, wo[0])
        dp, dcw = _conv_bwd(dp, dya, sv["p"], cw8[l], S)
        dp, dsg, dws, dbs = _gmlp_bwd(dp, dyc, sv["p"], sgn[l], bd[l], bdt[l], bias[l], gsum)
        early = [dwa.astype(BF16), dwb.astype(BF16), dwc.astype(BF16), dwo.reshape(N_CHIPS, D // N_CHIPS, D),
                 dw1, dw3, dw2]
        dq, dk, dv, *recv = _flash_bwd(sv["qa"], doa.reshape(N_KV_HEADS, GROUP, T, LANES), sv["k"], sv["v"], S,
                                       list(pending) + (early if l == 0 else []))
        dp, dqg, dkg = _qkv_prep_bwd(dp, dq, dk, dv, sv["p"], cos_t, sin_t,
                                     qg[l], kg[l], e_avg, fold)
        dh = _mm_nt_acc([dp], [win], True, "in_proj_bwd")
        dwin = _mm_dw(sv["ht"], dp, _row_cols(n_in), N_CHIPS, n_in, "dw_in")
        below = (saved[l - 1]["f2"], mod[l - 1], 5) if l else None
        dX0, dn1, dsh1, dsc1, *nxt = _norm_mod_bwd(sv["X"], dh, dX1, norm1[l][None], mod[l], 1, S, below)
        dmod = [dsh1, dsc1, dgt1, dsh2, dsc2]
        big = [dwin] + early
        small = dict(norm1=dn1[0], norm2=dn2[0], q_gain=dqg[0, :HEAD_DIM], k_gain=dkg[0, :HEAD_DIM],
                     conv_w=dcw[:3], sg_norm=dsg[0], w_s=dws, b_s=jnp.swapaxes(dbs[:, :4], 0, 1), dmod=dmod)
        return dX0, nxt, big, small, recv

    X = jnp.concatenate([x[0], ctx[0]], axis=0)
    Ws, saved = [_chip_gather(shard_bufs(0, _BIG[:1]), "gather_weights")], []
    for l in range(L):
        nxt = (shard_bufs(0, _BIG[1:]) if l == 0 else []) + (shard_bufs(l + 1) if l + 1 < L else [])
        X, Ws[l], got, sv = layer_fwd(X, l, Ws[l], nxt)
        if got:
            Ws.append(unpack(got))
        saved.append(sv)
    dX, lpart = _loss_grad(X, loss_target[0], S)
    loss = lax.psum(lpart[0, 0], ("x", "y", "c"))

    out = {nm: () for nm in _BIG}
    smalls = [None] * L

    def own_block(r, g):
        return lax.dynamic_update_slice(r, lax.dynamic_slice_in_dim(g, chip, 1, axis=0), (chip, 0, 0))

    def update(l, names, grads):
        for nm, g in zip(names, grads):
            out[nm] = _adamw_layer(given[nm], given["m_" + nm], given["v_" + nm], g, l, out[nm])

    def plane_update(l, names, recv, sent):
        mine = [_sum_lead(own_block(r, g), "sum_chips") for r, g in zip(recv, sent)]
        update(l, names, zip(mine, _sibling_swap(mine, "swap_planes")))

    pending = []
    dyf, dgt2 = _gate_bwd(dX, saved[L - 1]["f2"], mod[L - 1], 5, S)
    for l in reversed(range(L)):
        dX, nxt, big, smalls[l], recv = layer_bwd(dX, dyf, l, Ws[l], saved[l], pending)
        smalls[l]["dmod"] = jnp.concatenate(smalls[l]["dmod"] + [dgt2], axis=1)
        if nxt:
            dyf, dgt2 = nxt
        if recv:
            plane_update(l + 1, _BIG, recv[:len(pending)], pending)
            if l == 0:
                plane_update(0, _BIG[1:], recv[len(pending):], big[1:])
        pending = big
    last = big[:1]
    sib = _sibling_halves(last, "swap_halves")
    own = [lax.dynamic_slice_in_dim(g, mc * (g.shape[1] // 2), g.shape[1] // 2, axis=1) for g in last]
    sent = [_add_cast(g, s_) for g, s_ in zip(own, sib)]
    recv = [own_block(r, g) for r, g in zip(_chip_scatter(sent, "scatter_grads"), sent)]
    halves = [_sum_lead(r, "sum_chips") for r in recv]
    full = _sibling_fill(halves, "fill_halves")
    update(0, _BIG[:1], [(lax.dynamic_update_slice(f, hv, (mc * hv.shape[0], 0)),) for f, hv in zip(full, halves)])
    grad_x = dX[:S][None]

    def flat(nm):
        return jnp.stack([smalls[l][nm] for l in range(L)]).reshape(-1)

    dmod_all = jnp.stack([smalls[l]["dmod"] for l in range(L)])
    dml = dmod_all[:, 0].reshape(-1)
    dmc = dmod_all[:, 1].reshape(-1)
    names = ("norm1", "q_gain", "k_gain", "conv_w", "sg_norm", "w_s", "b_s", "norm2")
    parts = [dml, dml + dmc, dmc] + [flat(nm) for nm in names]
    sizes = [int(a.shape[0]) for a in parts]
    total = sum(sizes)
    padn = (-total) % (8 * LANES)
    sbuf = jnp.concatenate(parts + [jnp.zeros((padn,), F32)]).reshape(-1, LANES)
    g3 = _ag8(sbuf, "gather_small")
    ssum = _sum_lead(g3, "sum_devices").reshape(-1)
    offs = np.cumsum([0] + sizes)
    seg = {nm: ssum[offs[i + 3]:offs[i + 4]] for i, nm in enumerate(names)}
    gb_mod = ssum[offs[1]:offs[2]].reshape(L, N_CHIPS * n_mod)
    dmc_sum = ssum[offs[2]:offs[3]].reshape(L, N_CHIPS * n_mod)
    dml_all = g3.reshape(N_DEV, -1)[:, :sizes[0]].reshape(N_DEV, L, N_CHIPS * n_mod)
    dml_sh = jnp.swapaxes(lax.dynamic_slice_in_dim(dml_all, chip * n_mod, n_mod, axis=2), 0, 1)
    dmc_sh = lax.dynamic_slice_in_dim(dmc_sum, chip * n_mod, n_mod, axis=1)[:, None, :]
    dms = jnp.concatenate([dml_sh, dmc_sh, jnp.zeros((L, 7, n_mod), F32)], axis=1)
    g_wmod = _wmod_grad(a_raw, dms)
    part = _cctx_partial(jnp.concatenate([dmc_sh, jnp.zeros((L, 15, n_mod), F32)], axis=1), w_mod)
    g4 = _ag8(part.reshape(-1, LANES), "gather_cctx")
    g_cctx = _cctx_final(g4, c_ctx.reshape(8, LANES)).reshape(D)

    g_conv = lax.dynamic_slice_in_dim(seg["conv_w"].reshape(L, 3, N_CHIPS * n_cw), chip * n_cw, n_cw, axis=2)
    small_g = dict(c_ctx=g_cctx, w_mod=g_wmod, b_mod=gb_mod, norm1=seg["norm1"].reshape(norm1.shape),
                   q_gain=seg["q_gain"].reshape(q_gain.shape), k_gain=seg["k_gain"].reshape(k_gain.shape),
                   conv_w=g_conv, sg_norm=seg["sg_norm"].reshape(sg_norm.shape), w_s=seg["w_s"].reshape(w_s.shape),
                   b_s=seg["b_s"].reshape(b_s.shape), norm2=seg["norm2"].reshape(norm2.shape))
    res = {}
    for nm in _WEIGHTS:
        if nm in _BIG:
            res[nm] = out[nm]
        else:
            res[nm] = _adamw(given[nm], given["m_" + nm], given["v_" + nm], small_g[nm])
    return (loss, grad_x, *[res[nm][0] for nm in _WEIGHTS], *[res[nm][1] for nm in _WEIGHTS],
            *[res[nm][2] for nm in _WEIGHTS], *[res[nm][3] for nm in _WEIGHTS])
```

```python
import functools

import jax
import jax.numpy as jnp
import numpy as np
from jax import lax
from jax.experimental import pallas as pl
from jax.experimental.pallas import tpu as pltpu

F32 = jnp.float32
BF16 = jnp.bfloat16
EPS = 1e-6
LOG2E = 1.4426950408889634
D_MODEL = 1024
HEAD_DIM = 64
N_Q_HEADS = 8
N_KV_HEADS = 2
GROUP = N_Q_HEADS // N_KV_HEADS
GRID_W = 64
ROPE_THETA = 10000.0
CHUNK = 128
CONV_W = 256
SG_W = 256
OFF_Q = 3 * CONV_W
QKV_W = 768
OFF_U = OFF_Q + QKV_W
OFF_G = OFF_U + 2 * SG_W
IN_W = OFF_G + 3 * D_MODEL
N_CHIPS = 4
N_DEV = 8
LANES = 128
FWD_KEYS = 256
UNROLL_FWD = 8
UNROLL_BWD = 4
AUG = 3
ADAM_LR, ADAM_B1, ADAM_B2, ADAM_EPS, ADAM_WD, ADAM_STEP = 0.001, 0.9, 0.999, 1e-8, 0.01, 10
VMEM_LIMIT_V7X = 52 * 1024 * 1024
MXU_DEPTH_V7X = 256
MESH_ID = pl.DeviceIdType.MESH
NT = (((1,), (1,)), ((), ()))
TN = (((0,), (0,)), ((), ()))
ANY = pl.BlockSpec(memory_space=pl.ANY)


def _cp(*sem):
    return pltpu.CompilerParams(dimension_semantics=sem or None, vmem_limit_bytes=VMEM_LIMIT_V7X)


def _tile(n, target, mult=16):
    best = None
    for t in range(mult, n + 1, mult):
        if n % t == 0 and t <= target:
            best = t
    assert best is not None, (n, target, mult)
    return best


def _full(shape):
    nd = len(shape)
    return pl.BlockSpec(tuple(shape), lambda *_: (0,) * nd)


def _segments(i, tm, n_lat, fn):
    k, off = divmod(n_lat, tm)

    @pl.when(i < k)
    def _():
        fn(0, tm, 0)

    @pl.when(i == k)
    def _():
        if off:
            fn(0, off, 0)
        fn(off, tm, 1)

    @pl.when(i > k)
    def _():
        fn(0, tm, 1)


def _dot(a, b):
    return jnp.dot(a, b, preferred_element_type=F32)


def _dg(a, b, dims):
    return lax.dot_general(a, b, dims, preferred_element_type=F32)


def _split3(x):
    hi = x.astype(BF16)
    r1 = x - hi.astype(F32)
    mid = r1.astype(BF16)
    lo = (r1 - mid.astype(F32)).astype(BF16)
    return hi.astype(F32), mid.astype(F32), lo.astype(F32)


def _lane(shape):
    return lax.broadcasted_iota(jnp.int32, shape, len(shape) - 1)


def _aug(val, stat):
    lane = _lane(val.shape)
    hi, mid, lo = _split3(stat)
    ext = jnp.where(lane == 64, hi, jnp.where(lane == 65, mid, jnp.where(lane == 66, lo, 0.0)))
    return jnp.where(lane < 64, val, ext)


def _seg_mean(x, e):
    outs = []
    for g in range(x.shape[1] // LANES):
        blk = x[:, g * LANES:(g + 1) * LANES]
        hi = blk.astype(BF16)
        lo = (blk - hi.astype(F32)).astype(BF16)
        outs.append(_dot(hi, e) + _dot(lo, e))
    return outs[0] if len(outs) == 1 else jnp.concatenate(outs, axis=1)


def _rope(x, cos, sin_signed, inverse):
    w = x.shape[1]
    reps = w // LANES
    c = cos if reps == 1 else jnp.tile(cos, (1, reps))
    s = sin_signed if reps == 1 else jnp.tile(sin_signed, (1, reps))
    first = (_lane(x.shape) % 32) < 16
    partner = jnp.where(first, pltpu.roll(x, w - 16, 1), pltpu.roll(x, 16, 1))
    return x * c - partner * s if inverse else x * c + partner * s


def _sigmoid(x):
    return 1.0 / (1.0 + jnp.exp(-x))


_GELU_K = 0.7978845608028654
_GELU_C = 0.044715


def _gelu(x):
    return 0.5 * x * (1.0 + jnp.tanh(_GELU_K * (x + _GELU_C * x * x * x)))


def _gelu_grad(x):
    t = jnp.tanh(_GELU_K * (x + _GELU_C * x * x * x))
    return 0.5 * (1.0 + t) + 0.5 * x * (1.0 - t * t) * _GELU_K * (1.0 + 3.0 * _GELU_C * x * x)


def _loop_unrolled(n, step, init, unroll):
    def trip(t, carry):
        for u in range(unroll):
            carry = step(t * unroll + u, carry)
        return carry

    carry = lax.fori_loop(0, n // unroll, trip, init) if n >= unroll else init
    for r in range(n - n % unroll, n):
        carry = step(r, carry)
    return carry


def _heads_to_rows(x, n_heads):
    out = []
    for h in range(n_heads):
        grp = x[:, (h // 2) * LANES:(h // 2 + 1) * LANES]
        out.append(grp if h % 2 == 0 else pltpu.roll(grp, 64, 1))
    return out


def _rows_to_heads(blocks):
    outs = []
    lane = _lane(blocks[0].shape)
    for a in range(len(blocks) // 2):
        outs.append(jnp.where(lane < 64, blocks[2 * a], pltpu.roll(blocks[2 * a + 1], 64, 1)))
    return outs[0] if len(outs) == 1 else jnp.concatenate(outs, axis=1)


def _norm_mod(x, g, mod, i_shift, i_scale, n_lat, transposed):
    T, D = x.shape
    tm = _tile(T, 768, LANES)

    def body(x_ref, g_ref, mod_ref, h_ref, *ht_ref):
        def fn(r0, r1, seg):
            xv = x_ref[r0:r1, :]
            r = lax.rsqrt(jnp.mean(xv * xv, axis=-1, keepdims=True) + EPS)
            n = xv * r * g_ref[...]
            h = n * (1.0 + mod_ref[seg, i_scale:i_scale + 1, :]) + mod_ref[seg, i_shift:i_shift + 1, :]
            h_ref[r0:r1, :] = h.astype(BF16)

        _segments(pl.program_id(0), tm, n_lat, fn)
        if transposed:
            ht_ref[0][...] = h_ref[...].astype(F32).T.astype(BF16)

    return pl.pallas_call(
        body, grid=(T // tm,), name="norm_mod",
        in_specs=[pl.BlockSpec((tm, D), lambda i: (i, 0)), _full(g.shape), _full(mod.shape)],
        out_specs=[pl.BlockSpec((tm, D), lambda i: (i, 0))] + [pl.BlockSpec((D, tm), lambda i: (0, i))] * transposed,
        out_shape=[jax.ShapeDtypeStruct((T, D), BF16)] + [jax.ShapeDtypeStruct((D, T), BF16)] * transposed,
        compiler_params=_cp("parallel"))(x, g, mod)


def _norm_mod_bwd(x, dh, dres, g, mod, i_scale, n_lat, gate=None):
    T, D = x.shape
    tm = _tile(T, 528)

    def body(x_ref, dh_ref, dres_ref, g_ref, mod_ref, *rest):
        if gate is None:
            dx_ref, dg_ref, dsh_ref, dsc_ref = rest
        else:
            f_ref, gmod_ref, dx_ref, dg_ref, dsh_ref, dsc_ref, dy_ref, dgt_ref = rest
        i = pl.program_id(0)

        @pl.when(i == 0)
        def _():
            dg_ref[...] = jnp.zeros_like(dg_ref)
            dsh_ref[...] = jnp.zeros_like(dsh_ref)
            dsc_ref[...] = jnp.zeros_like(dsc_ref)
            if gate is not None:
                dgt_ref[...] = jnp.zeros_like(dgt_ref)

        def fn(r0, r1, seg):
            xv = x_ref[r0:r1, :]
            dh = dh_ref[r0:r1, :]
            r = lax.rsqrt(jnp.mean(xv * xv, axis=-1, keepdims=True) + EPS)
            xh = xv * r
            gv = g_ref[...]
            dsh_ref[seg] += jnp.sum(dh, axis=0, keepdims=True)
            dsc_ref[seg] += jnp.sum(dh * (xh * gv), axis=0, keepdims=True)
            dn = dh * (1.0 + mod_ref[seg, i_scale:i_scale + 1, :])
            dg_ref[...] += jnp.sum(dn * xh, axis=0, keepdims=True)
            gd = gv * dn
            dxv = dres_ref[r0:r1, :] + r * (gd - xh * jnp.mean(xh * gd, axis=-1, keepdims=True))
            dx_ref[r0:r1, :] = dxv
            if gate is not None:
                dy_ref[r0:r1, :] = (dxv * gmod_ref[seg, gate[2]:gate[2] + 1, :]).astype(BF16)
                dgt_ref[seg] += jnp.sum(dxv * f_ref[r0:r1, :].astype(F32), axis=0, keepdims=True)

        _segments(i, tm, n_lat, fn)

    row = pl.BlockSpec((tm, D), lambda i: (i, 0))
    extra_in = [] if gate is None else [gate[0], gate[1]]
    return pl.pallas_call(
        body, grid=(T // tm,), name="norm_mod_bwd" if gate is None else "norm_gate_bwd",
        in_specs=[row, row, row, _full(g.shape), _full(mod.shape)] + ([] if gate is None else [row, _full(gate[1].shape)]),
        out_specs=[row, _full((1, D)), _full((2, 1, D)), _full((2, 1, D))] + ([] if gate is None else [row, _full((2, 1, D))]),
        out_shape=[jax.ShapeDtypeStruct((T, D), F32), jax.ShapeDtypeStruct((1, D), F32),
                   jax.ShapeDtypeStruct((2, 1, D), F32), jax.ShapeDtypeStruct((2, 1, D), F32)]
        + ([] if gate is None else [jax.ShapeDtypeStruct((T, D), BF16), jax.ShapeDtypeStruct((2, 1, D), F32)]),
        compiler_params=_cp("arbitrary"))(x, dh, dres, g, mod, *extra_in)


def _gate_bwd(dx, f, mod, i_gate, n_lat):
    T, D = dx.shape
    tm = _tile(T, 528)

    def body(dx_ref, f_ref, mod_ref, dy_ref, dg_ref):
        i = pl.program_id(0)

        @pl.when(i == 0)
        def _():
            dg_ref[...] = jnp.zeros_like(dg_ref)

        def fn(r0, r1, seg):
            dxv = dx_ref[r0:r1, :]
            dy_ref[r0:r1, :] = (dxv * mod_ref[seg, i_gate:i_gate + 1, :]).astype(BF16)
            dg_ref[seg] += jnp.sum(dxv * f_ref[r0:r1, :].astype(F32), axis=0, keepdims=True)

        _segments(i, tm, n_lat, fn)

    row = pl.BlockSpec((tm, D), lambda i: (i, 0))
    return pl.pallas_call(
        body, grid=(T // tm,), name="gate_bwd",
        in_specs=[row, row, _full(mod.shape)], out_specs=[row, _full((2, 1, D))],
        out_shape=[jax.ShapeDtypeStruct((T, D), BF16), jax.ShapeDtypeStruct((2, 1, D), F32)],
        compiler_params=_cp("arbitrary"))(dx, f, mod)


def _mm_nn(a, w, out_dtype, name):
    M, K = a.shape
    J, _, n = w.shape
    tm = _tile(M, 1056)

    def body(a_ref, w_ref, o_ref):
        o_ref[...] = _dot(a_ref[...], w_ref[...]).astype(o_ref.dtype)

    return pl.pallas_call(
        body, grid=(M // tm, J), name=name,
        in_specs=[pl.BlockSpec((tm, K), lambda i, j: (i, 0)), pl.BlockSpec((None, K, n), lambda i, j: (j, 0, 0))],
        out_specs=pl.BlockSpec((tm, n), lambda i, j: (i, j)),
        out_shape=jax.ShapeDtypeStruct((M, J * n), out_dtype), compiler_params=_cp("parallel", "arbitrary"))(a, w)


def _mm_res(a3, w, res, mod, i_gate, n_lat, name):
    J, M, k = a3.shape
    N = w.shape[2]
    tm = _tile(M, 528)

    def body(a_ref, w_ref, res_ref, mod_ref, x_ref, f_ref):
        acc = _dot(a_ref[0], w_ref[0])
        for j in range(1, J):
            acc += _dot(a_ref[j], w_ref[j])
        f_ref[...] = acc.astype(BF16)

        def fn(r0, r1, seg):
            x_ref[r0:r1, :] = res_ref[r0:r1, :] + mod_ref[seg, i_gate:i_gate + 1, :] * acc[r0:r1, :]

        _segments(pl.program_id(0), tm, n_lat, fn)

    row = pl.BlockSpec((tm, N), lambda i: (i, 0))
    return pl.pallas_call(
        body, grid=(M // tm,), name=name,
        in_specs=[pl.BlockSpec((J, tm, k), lambda i: (0, i, 0)), _full(w.shape), row, _full(mod.shape)],
        out_specs=[row, row],
        out_shape=[jax.ShapeDtypeStruct((M, N), F32), jax.ShapeDtypeStruct((M, N), BF16)],
        compiler_params=_cp("parallel"))(a3, w, res, mod)


def _mm_nt_acc(dys, ws, row_major, name):
    J, K, n = ws[0].shape
    M = dys[0].shape[0] if row_major else dys[0].shape[1]
    tm = _tile(M, 1056)
    P = len(dys)

    def body(*refs):
        o_ref = refs[2 * P]
        j = pl.program_id(1)
        part = _dg(refs[0][...], refs[P][...], NT)
        for p in range(1, P):
            part += _dg(refs[p][...], refs[P + p][...], NT)

        @pl.when(j == 0)
        def _():
            o_ref[...] = part

        @pl.when(j > 0)
        def _():
            o_ref[...] += part

    dy_spec = (pl.BlockSpec((tm, n), lambda i, j: (i, j)) if row_major
               else pl.BlockSpec((None, tm, n), lambda i, j: (j, i, 0)))
    w_spec = pl.BlockSpec((None, K, n), lambda i, j: (j, 0, 0))
    return pl.pallas_call(
        body, grid=(M // tm, J), name=name,
        in_specs=[dy_spec] * P + [w_spec] * P,
        out_specs=pl.BlockSpec((tm, K), lambda i, j: (i, 0)),
        out_shape=jax.ShapeDtypeStruct((M, K), F32), compiler_params=_cp("parallel", "arbitrary"))(*dys, *ws)


def _mm_tn(x, dy, x_spec, dy_spec, J, K, n, T, name):
    tk = _tile(T, 1056, MXU_DEPTH_V7X)
    nt = T // tk

    def body(x_ref, dy_ref, o_ref, acc):
        t = pl.program_id(1)
        part = _dg(x_ref[...], dy_ref[...], TN)

        @pl.when(t == 0)
        def _():
            acc[...] = part

        @pl.when(t > 0)
        def _():
            acc[...] += part

        @pl.when(t == nt - 1)
        def _():
            o_ref[...] = acc[...].astype(BF16)

    return pl.pallas_call(
        body, grid=(J, nt), name=name,
        in_specs=[x_spec(tk), dy_spec(tk)],
        out_specs=pl.BlockSpec((None, K, n), lambda j, t: (j, 0, 0)),
        out_shape=jax.ShapeDtypeStruct((J, K, n), BF16), scratch_shapes=[pltpu.VMEM((K, n), F32)],
        compiler_params=_cp("parallel", "arbitrary"))(x, dy)


def _mm_dw(xt, dy, dy_spec, J, n, name):
    K, T = xt.shape
    tk = _tile(T, 1056, MXU_DEPTH_V7X)
    nt = T // tk

    def body(xt_ref, dy_ref, o_ref, acc):
        t = pl.program_id(1)
        part = _dot(xt_ref[...], dy_ref[...])

        @pl.when(t == 0)
        def _():
            acc[...] = part

        @pl.when(t > 0)
        def _():
            acc[...] += part

        @pl.when(t == nt - 1)
        def _():
            o_ref[...] = acc[...].astype(BF16)

    return pl.pallas_call(
        body, grid=(J, nt), name=name,
        in_specs=[pl.BlockSpec((K, tk), lambda j, t: (0, t)), dy_spec(tk)],
        out_specs=pl.BlockSpec((None, K, n), lambda j, t: (j, 0, 0)),
        out_shape=jax.ShapeDtypeStruct((J, K, n), BF16), scratch_shapes=[pltpu.VMEM((K, n), F32)],
        compiler_params=_cp("parallel", "arbitrary"))(xt, dy)


def _rows(width):
    return lambda tk: pl.BlockSpec((tk, width), lambda j, t: (t, 0))


def _row_cols(width):
    return lambda tk: pl.BlockSpec((tk, width), lambda j, t: (t, j))


def _shard_rows(width):
    return lambda tk: pl.BlockSpec((None, tk, width), lambda j, t: (j, t, 0))


def _ffn_up(h, w1, w3):
    T, D = h.shape
    J, _, n = w1.shape
    tm = _tile(T, 1056)

    def body(h_ref, w1_ref, w3_ref, a1_ref, a3_ref, act_ref):
        hv = h_ref[...]
        a1 = _dot(hv, w1_ref[...])
        a3 = _dot(hv, w3_ref[...])
        a1_ref[...] = a1.astype(BF16)
        a3_ref[...] = a3.astype(BF16)
        act_ref[...] = (a1 * _sigmoid(a1) * a3).astype(BF16)

    w_spec = pl.BlockSpec((None, D, n), lambda i, j: (j, 0, 0))
    o_spec = pl.BlockSpec((None, tm, n), lambda i, j: (j, i, 0))
    return pl.pallas_call(
        body, grid=(T // tm, J), name="ffn_up",
        in_specs=[pl.BlockSpec((tm, D), lambda i, j: (i, 0)), w_spec, w_spec], out_specs=[o_spec] * 3,
        out_shape=[jax.ShapeDtypeStruct((J, T, n), BF16)] * 3,
        compiler_params=_cp("parallel", "arbitrary"))(h, w1, w3)


def _ffn_down_bwd(dy, w2, a1, a3):
    T, D = dy.shape
    J, n, _ = w2.shape
    tm = _tile(T, 1056)

    def body(dy_ref, w2_ref, a1_ref, a3_ref, da1_ref, da3_ref):
        dact = _dg(dy_ref[...], w2_ref[...], NT)
        a1v = a1_ref[...].astype(F32)
        sig = _sigmoid(a1v)
        da3_ref[...] = (dact * a1v * sig).astype(BF16)
        da1_ref[...] = (dact * a3_ref[...].astype(F32) * (sig * (1.0 + a1v * (1.0 - sig)))).astype(BF16)

    a_spec = pl.BlockSpec((None, tm, n), lambda i, j: (j, i, 0))
    return pl.pallas_call(
        body, grid=(T // tm, J), name="ffn_down_bwd",
        in_specs=[pl.BlockSpec((tm, D), lambda i, j: (i, 0)), pl.BlockSpec((None, n, D), lambda i, j: (j, 0, 0)),
                  a_spec, a_spec],
        out_specs=[a_spec, a_spec], out_shape=[jax.ShapeDtypeStruct((J, T, n), BF16)] * 2,
        compiler_params=_cp("parallel", "arbitrary"))(dy, w2, a1, a3)


def _qkv_prep(p, cos, sin, qg, kg, e):
    T = p.shape[0]
    tm = _tile(T, 528)

    def body(p_ref, cos_ref, sin_ref, qg_ref, kg_ref, e_ref, q_ref, k_ref, v_ref):
        ev = e_ref[...]
        cv, sv = cos_ref[...], sin_ref[...]
        xq = p_ref[:, 0:512]
        qn = xq * lax.rsqrt(_seg_mean(xq * xq, ev) + EPS) * qg_ref[...]
        qr = _rope(qn, cv, sv, False) * (HEAD_DIM ** -0.5 * LOG2E)
        xk = p_ref[:, 512:640]
        kn = xk * lax.rsqrt(_seg_mean(xk * xk, ev) + EPS) * kg_ref[...]
        kr = _rope(kn, cv, sv, False)
        lane = _lane((tm, LANES))
        ones = jnp.where(lane < 64 + AUG, -1.0, 0.0)
        for h, blk in enumerate(_heads_to_rows(qr, N_Q_HEADS)):
            q_ref[h] = jnp.where(lane < 64, blk, 0.0).astype(BF16)
        for h, blk in enumerate(_heads_to_rows(kr, N_KV_HEADS)):
            k_ref[h] = jnp.where(lane < 64, blk, ones).astype(BF16)
        for h, blk in enumerate(_heads_to_rows(p_ref[:, 640:768], N_KV_HEADS)):
            v_ref[h] = jnp.where(lane < 64, blk, ones).astype(BF16)

    tab = pl.BlockSpec((tm, LANES), lambda i: (i, 0))
    return pl.pallas_call(
        body, grid=(T // tm,), name="qkv_prep",
        in_specs=[pl.BlockSpec((tm, QKV_W), lambda i: (i, 1)), tab, tab, _full(qg.shape), _full(kg.shape),
                  _full(e.shape)],
        out_specs=[pl.BlockSpec((N_Q_HEADS, tm, LANES), lambda i: (0, i, 0)),
                   pl.BlockSpec((N_KV_HEADS, tm, LANES), lambda i: (0, i, 0)),
                   pl.BlockSpec((N_KV_HEADS, tm, LANES), lambda i: (0, i, 0))],
        out_shape=[jax.ShapeDtypeStruct((N_Q_HEADS, T, LANES), BF16),
                   jax.ShapeDtypeStruct((N_KV_HEADS, T, LANES), BF16),
                   jax.ShapeDtypeStruct((N_KV_HEADS, T, LANES), BF16)],
        compiler_params=_cp("parallel"))(p, cos, sin, qg, kg, e)


def _qkv_prep_bwd(dp, dq, dk, dv, p, cos, sin, qg, kg, e, fold):
    T = p.shape[0]
    tq = dq.shape[3] // GROUP
    tm = _tile(T, 768, tq)
    nt = T // tm

    def body(dp_in, dq_ref, dk_ref, dv_ref, p_ref, cos_ref, sin_ref, qg_ref, kg_ref, e_ref, fold_ref,
             dp_ref, dqg_ref, dkg_ref, accq, acck):
        del dp_in
        i = pl.program_id(0)

        @pl.when(i == 0)
        def _():
            accq[...] = jnp.zeros_like(accq)
            acck[...] = jnp.zeros_like(acck)

        ev = e_ref[...]
        cv, sv = cos_ref[...], sin_ref[...]

        def one(x, dr, gain, acc):
            r = lax.rsqrt(_seg_mean(x * x, ev) + EPS)
            xh = x * r
            dn = _rope(dr, cv, sv, True)
            acc[0:1, :] += jnp.sum(dn * xh, axis=0, keepdims=True)
            gd = gain * dn
            return r * (gd - xh * _seg_mean(xh * gd, ev))

        slabs = [[dq_ref[h, b].T for b in range(tm // tq)] for h in range(N_KV_HEADS)]
        heads = [jnp.concatenate([sl[g * tq:(g + 1) * tq] for sl in slabs[h]], axis=0)
                 for h in range(N_KV_HEADS) for g in range(GROUP)]
        dqr = _rows_to_heads(heads) * (HEAD_DIM ** -0.5)
        dkr = _rows_to_heads([dk_ref[h] for h in range(N_KV_HEADS)]) * (1.0 / LOG2E)
        dvv = _rows_to_heads([dv_ref[h] for h in range(N_KV_HEADS)])
        dp_ref[:, 0:512] = one(p_ref[:, 0:512], dqr, qg_ref[...], accq).astype(BF16)
        dp_ref[:, 512:640] = one(p_ref[:, 512:640], dkr, kg_ref[...], acck).astype(BF16)
        dp_ref[:, 640:768] = dvv.astype(BF16)

        @pl.when(i == nt - 1)
        def _():
            fv = fold_ref[...]
            dqg_ref[...] = jnp.dot(accq[...], fv, preferred_element_type=F32, precision=lax.Precision.HIGHEST)
            dkg_ref[...] = jnp.dot(acck[...], fv[0:LANES, :], preferred_element_type=F32,
                                   precision=lax.Precision.HIGHEST)

    tab = pl.BlockSpec((tm, LANES), lambda i: (i, 0))
    sec = pl.BlockSpec((tm, QKV_W), lambda i: (i, 1))
    return pl.pallas_call(
        body, grid=(nt,), name="qkv_prep_bwd",
        in_specs=[ANY, pl.BlockSpec((N_KV_HEADS, tm // tq, LANES, GROUP * tq), lambda i: (0, i, 0, 0)),
                  pl.BlockSpec((N_KV_HEADS, tm, LANES), lambda i: (0, i, 0)),
                  pl.BlockSpec((N_KV_HEADS, tm, LANES), lambda i: (0, i, 0)),
                  sec, tab, tab, _full(qg.shape), _full(kg.shape), _full(e.shape), _full(fold.shape)],
        out_specs=[sec, _full((8, LANES)), _full((8, LANES))],
        out_shape=[jax.ShapeDtypeStruct(dp.shape, BF16), jax.ShapeDtypeStruct((8, LANES), F32),
                   jax.ShapeDtypeStruct((8, LANES), F32)],
        scratch_shapes=[pltpu.VMEM((8, 512), F32), pltpu.VMEM((8, LANES), F32)],
        input_output_aliases={0: 0}, compiler_params=_cp("arbitrary"))(dp, dq, dk, dv, p, cos, sin, qg, kg, e, fold)


def _flash_fwd(q, k, v, n_lat, gather=()):
    _, _, T, _ = q.shape
    tq = tk = 256
    nq = T // tq
    M = GROUP * tq

    wide_k = FWD_KEYS if n_lat % FWD_KEYS == 0 else tk
    n_g = len(gather)

    def body(q_ref, k_ref, v_ref, *rest):
        o_ref, qa_ref = rest[n_g], rest[n_g + 1]
        i = pl.program_id(0)
        if n_g:
            bufs = rest[n_g + 2:2 * n_g + 2]
            start, wait = _plane_exchange(bufs, bufs, rest[-2], rest[-1], False)
            pl.when((i == 0) & (pl.program_id(1) == 0))(start)
        qv = q_ref[...].reshape(M, LANES)

        def step(r0, width, carry):
            m, acc = carry
            sc = _dg(qv, k_ref[pl.ds(r0, width), :], NT)
            m_new = jnp.maximum(m, jnp.max(sc, axis=1, keepdims=True))
            pr = jnp.exp2(sc - m_new)
            acc = jnp.exp2(m - m_new) * acc + _dot(pr.astype(BF16), v_ref[pl.ds(r0, width), :])
            return m_new, acc

        def wide(s, carry):
            return step(s * wide_k if isinstance(s, int) else pl.multiple_of(s * wide_k, wide_k), wide_k, carry)

        def finish(m, acc):
            den = -acc[:, 64:65]
            out = acc / den
            o_ref[...] = _rows_to_heads([out[g * tq:(g + 1) * tq] for g in range(GROUP)]).astype(BF16)
            qa_ref[...] = _aug(qv.astype(F32), m + jnp.log2(den)).astype(BF16).reshape(GROUP, tq, LANES)

        init = (jnp.full((M, 1), -1e30, F32), jnp.zeros((M, LANES), F32))

        @pl.when(i < n_lat // tq)
        def _():
            carry = _loop_unrolled(n_lat // wide_k, wide, init, UNROLL_FWD)
            for r0 in range(n_lat, T, tk):
                carry = step(r0, tk, carry)
            finish(*carry)

        @pl.when(i >= n_lat // tq)
        def _():
            carry = init
            for r0 in range(n_lat, T, tk):
                carry = step(r0, tk, carry)
            finish(*carry)

        if n_g:
            pl.when((i == nq - 1) & (pl.program_id(1) == N_KV_HEADS - 1))(wait)

    q_spec = pl.BlockSpec((None, GROUP, tq, LANES), lambda i, h: (h, 0, i, 0))
    kv_spec = pl.BlockSpec((None, T, LANES), lambda i, h: (h, 0, 0))
    sems = [pltpu.SemaphoreType.DMA((3 * n_g,))] * 2 if n_g else []
    return pl.pallas_call(
        body, grid=(nq, N_KV_HEADS), name="flash_fwd_gather" if n_g else "flash_fwd",
        in_specs=[q_spec, kv_spec, kv_spec] + [ANY] * n_g,
        out_specs=[pl.BlockSpec((tq, GROUP * HEAD_DIM), lambda i, h: (i, h)), q_spec] + [ANY] * n_g,
        out_shape=[jax.ShapeDtypeStruct((T, N_Q_HEADS * HEAD_DIM), BF16), jax.ShapeDtypeStruct(q.shape, BF16)]
        + [jax.ShapeDtypeStruct(b.shape, b.dtype) for b in gather],
        input_output_aliases={3 + a: 2 + a for a in range(n_g)}, scratch_shapes=sems,
        compiler_params=_cp("arbitrary", "arbitrary"))(q, k, v, *gather)


def _flash_bwd(qa, doa, k, v, n_lat, scatter=()):
    _, _, T, _ = qa.shape
    tq = tk = 256
    nkv = T // tk
    M = GROUP * tq

    n_s = len(scatter)

    def body(qa_hbm, doa_hbm, k_ref, v_ref, *rest):
        dq_hbm, dk_ref, dv_ref = rest[n_s:n_s + 3]
        q_sc, do_sc, dq_sc, sems = rest[2 * n_s + 3:2 * n_s + 7]
        h = pl.program_id(0)
        j = pl.program_id(1)
        if n_s:
            start, wait = _plane_exchange(rest[:n_s], rest[n_s + 3:2 * n_s + 3], rest[-2], rest[-1], True)
            pl.when((h == 0) & (j == 0))(start)

        @pl.when(j == 0)
        def _():
            c1 = pltpu.make_async_copy(qa_hbm.at[h], q_sc, sems.at[0])
            c2 = pltpu.make_async_copy(doa_hbm.at[h], do_sc, sems.at[1])
            c1.start()
            c2.start()
            dq_sc[...] = jnp.zeros_like(dq_sc)
            c1.wait()
            c2.wait()

        kb = k_ref[...]
        vb = v_ref[...]
        kbt = kb.astype(F32).T.astype(BF16)

        def step(i, carry):
            dk, dv = carry
            r0 = i * tq if isinstance(i, int) else pl.multiple_of(i * tq, tq)
            qv = q_sc[:, pl.ds(r0, tq), :].reshape(M, LANES)
            dov = do_sc[:, pl.ds(r0, tq), :].reshape(M, LANES)
            pr = jnp.exp2(_dg(kb, qv, NT))
            ds = (pr * _dg(vb, dov, NT)).astype(BF16)
            dv = dv + _dot(pr.astype(BF16), dov)
            dk = dk + _dot(ds, qv)
            dq_sc[i] += _dot(kbt, ds)
            return dk, dv

        z = jnp.zeros((tk, LANES), F32)
        carry = _loop_unrolled(n_lat // tq, step, (z, z), UNROLL_BWD)
        dk_ref[...] = carry[0]
        dv_ref[...] = carry[1]

        @pl.when(j >= n_lat // tk)
        def _():
            c = (dk_ref[...], dv_ref[...])
            for i in range(n_lat // tq, T // tq):
                c = step(i, c)
            dk_ref[...] = c[0]
            dv_ref[...] = c[1]

        @pl.when(j == nkv - 1)
        def _():
            c3 = pltpu.make_async_copy(dq_sc, dq_hbm.at[h], sems.at[2])
            c3.start()
            c3.wait()

        if n_s:
            pl.when((h == N_KV_HEADS - 1) & (j == nkv - 1))(wait)

    kv_spec = pl.BlockSpec((None, tk, LANES), lambda h, j: (h, j, 0))
    return pl.pallas_call(
        body, grid=(N_KV_HEADS, nkv), name="flash_bwd_scatter" if n_s else "flash_bwd",
        in_specs=[ANY, ANY, kv_spec, kv_spec] + [ANY] * n_s, out_specs=[ANY, kv_spec, kv_spec] + [ANY] * n_s,
        out_shape=[jax.ShapeDtypeStruct((N_KV_HEADS, T // tq, LANES, M), F32), jax.ShapeDtypeStruct(k.shape, F32),
                   jax.ShapeDtypeStruct(k.shape, F32)] + [jax.ShapeDtypeStruct(g.shape, g.dtype) for g in scatter],
        scratch_shapes=[pltpu.VMEM((GROUP, T, LANES), BF16), pltpu.VMEM((GROUP, T, LANES), BF16),
                        pltpu.VMEM((T // tq, LANES, M), F32), pltpu.SemaphoreType.DMA((3,))]
        + ([pltpu.SemaphoreType.DMA((3 * n_s,))] * 2 if n_s else []),
        compiler_params=_cp("arbitrary", "arbitrary"))(qa, doa, k, v, *scatter)


def _conv_masks(i, tm, n_lat, T):
    row = lax.broadcasted_iota(jnp.int32, (tm, 1), 0)
    g = row + i * tm
    return row, (g == 0) | (g == n_lat), (g == n_lat - 1) | (g == T - 1)


def _shift_rows(v, prev_row, next_row, row, first, last):
    tm = v.shape[0]
    down = jnp.where(row == 0, prev_row, pltpu.roll(v, 1, 0))
    up = jnp.where(row == tm - 1, next_row, pltpu.roll(v, tm - 1, 0))
    return jnp.where(first, 0.0, down), jnp.where(last, 0.0, up)


def _halo_specs(tm, T, width, col):
    nb = T // 8
    return (pl.BlockSpec((8, width), lambda i: (jnp.maximum(i * (tm // 8) - 1, 0), col)),
            pl.BlockSpec((8, width), lambda i: (jnp.minimum((i + 1) * (tm // 8), nb - 1), col)))


def _conv_fwd(p, cw, n_lat):
    T = p.shape[0]
    tm = _tile(T, 1056)

    def body(p_ref, pp_ref, pn_ref, cw_ref, o_ref):
        row, first, last = _conv_masks(pl.program_id(0), tm, n_lat, T)
        z = p_ref[:, 256:512] * p_ref[:, 512:768]
        zp = pp_ref[7:8, 256:512] * pp_ref[7:8, 512:768]
        zn = pn_ref[0:1, 256:512] * pn_ref[0:1, 512:768]
        zd, zu = _shift_rows(z, zp, zn, row, first, last)
        conv = cw_ref[0:1, :] * zd + cw_ref[1:2, :] * z + cw_ref[2:3, :] * zu
        o_ref[...] = (p_ref[:, 0:256] * conv).astype(BF16)

    prev, nxt = _halo_specs(tm, T, 768, 0)
    return pl.pallas_call(
        body, grid=(T // tm,), name="conv_fwd",
        in_specs=[pl.BlockSpec((tm, 768), lambda i: (i, 0)), prev, nxt, _full(cw.shape)],
        out_specs=pl.BlockSpec((tm, CONV_W), lambda i: (i, 0)),
        out_shape=jax.ShapeDtypeStruct((T, CONV_W), BF16), compiler_params=_cp("parallel"))(p, p, p, cw)


def _conv_bwd(dp, dy, p, cw, n_lat):
    T = p.shape[0]
    tm = _tile(T, 1056)

    def body(dp_in, dy_ref, dyp_ref, dyn_ref, p_ref, pp_ref, pn_ref, cw_ref, dp_ref, dcw_ref):
        del dp_in
        i = pl.program_id(0)

        @pl.when(i == 0)
        def _():
            dcw_ref[...] = jnp.zeros_like(dcw_ref)

        row, first, last = _conv_masks(i, tm, n_lat, T)
        ab, ac, ax = p_ref[:, 0:256], p_ref[:, 256:512], p_ref[:, 512:768]
        z = ac * ax
        zp = pp_ref[7:8, 256:512] * pp_ref[7:8, 512:768]
        zn = pn_ref[0:1, 256:512] * pn_ref[0:1, 512:768]
        zd, zu = _shift_rows(z, zp, zn, row, first, last)
        w0, w1, w2 = cw_ref[0:1, :], cw_ref[1:2, :], cw_ref[2:3, :]
        dy = dy_ref[...]
        dc = dy * ab
        dcd, dcu = _shift_rows(dc, dyp_ref[7:8, :] * pp_ref[7:8, 0:256], dyn_ref[0:1, :] * pn_ref[0:1, 0:256],
                               row, first, last)
        dz = w0 * dcu + w1 * dc + w2 * dcd
        dp_ref[:, 0:256] = (dy * (w0 * zd + w1 * z + w2 * zu)).astype(BF16)
        dp_ref[:, 256:512] = (dz * ax).astype(BF16)
        dp_ref[:, 512:768] = (dz * ac).astype(BF16)
        dcw_ref[0:1, :] += jnp.sum(dc * zd, axis=0, keepdims=True)
        dcw_ref[1:2, :] += jnp.sum(dc * z, axis=0, keepdims=True)
        dcw_ref[2:3, :] += jnp.sum(dc * zu, axis=0, keepdims=True)

    prev, nxt = _halo_specs(tm, T, 768, 0)
    dprev, dnxt = _halo_specs(tm, T, CONV_W, 0)
    sec = pl.BlockSpec((tm, 768), lambda i: (i, 0))
    return pl.pallas_call(
        body, grid=(T // tm,), name="conv_bwd",
        in_specs=[ANY, pl.BlockSpec((tm, CONV_W), lambda i: (i, 0)), dprev, dnxt, sec, prev, nxt, _full(cw.shape)],
        out_specs=[sec, _full((8, CONV_W))],
        out_shape=[jax.ShapeDtypeStruct(dp.shape, BF16), jax.ShapeDtypeStruct((8, CONV_W), F32)],
        input_output_aliases={0: 0}, compiler_params=_cp("arbitrary"))(dp, dy, dy, dy, p, p, p, cw)


def _gmlp_mix(bd_ref, vs, grp):
    out = jnp.zeros((2 * CHUNK, SG_W), F32)
    for g in range(4):
        out = jnp.where(grp == g, _dot(bd_ref[g], vs), out)
    return out


def _gmlp_fwd(p, sgn, bd, bias):
    T = p.shape[0]
    tm = _tile(T, 768, 2 * CHUNK)

    def body(p_ref, sgn_ref, bd_ref, bias_ref, o_ref):
        x = _gelu(p_ref[:, 256:512])
        vn = (x * lax.rsqrt(jnp.mean(x * x, axis=-1, keepdims=True) + EPS) * sgn_ref[...]).astype(BF16)
        grp = _lane((2 * CHUNK, SG_W)) // 64
        for s in range(tm // (2 * CHUNK)):
            rs = slice(s * 2 * CHUNK, (s + 1) * 2 * CHUNK)
            mixed = _gmlp_mix(bd_ref, vn[rs], grp) + bias_ref[...]
            o_ref[rs, :] = (_gelu(p_ref[rs, 0:256]) * mixed).astype(BF16)

    return pl.pallas_call(
        body, grid=(T // tm,), name="gmlp_fwd",
        in_specs=[pl.BlockSpec((tm, 2 * SG_W), lambda i: (i, 3)), _full(sgn.shape), _full(bd.shape),
                  _full(bias.shape)],
        out_specs=pl.BlockSpec((tm, SG_W), lambda i: (i, 0)),
        out_shape=jax.ShapeDtypeStruct((T, SG_W), BF16), compiler_params=_cp("parallel"))(p, sgn, bd, bias)


def _gmlp_bwd(dp, dy, p, sgn, bd, bdt, bias, gsum):
    T = p.shape[0]
    tm = _tile(T, 768, 2 * CHUNK)
    nt = T // tm
    C2 = 2 * CHUNK

    def body(dp_in, dy_ref, p_ref, sgn_ref, bd_ref, bdt_ref, bias_ref, gsum_ref,
             dp_ref, dsg_ref, dws_ref, dbs_ref, acc_w, acc_b):
        del dp_in
        i = pl.program_id(0)

        @pl.when(i == 0)
        def _():
            dsg_ref[...] = jnp.zeros_like(dsg_ref)
            acc_w[...] = jnp.zeros_like(acc_w)
            acc_b[...] = jnp.zeros_like(acc_b)

        u = p_ref[:, 0:256]
        sv = p_ref[:, 256:512]
        ug = _gelu(u)
        x = _gelu(sv)
        r = lax.rsqrt(jnp.mean(x * x, axis=-1, keepdims=True) + EPS)
        xh = x * r
        sg = sgn_ref[...]
        vn = (xh * sg).astype(BF16)
        grp = _lane((C2, SG_W)) // 64
        dug, dvn = [], []
        for s in range(tm // C2):
            rs = slice(s * C2, (s + 1) * C2)
            vs = vn[rs]
            dys = dy_ref[rs, :]
            dug.append(dys * (_gmlp_mix(bd_ref, vs, grp) + bias_ref[...]))
            dmix = dys * ug[rs]
            acc_b[...] += dmix
            dmb = dmix.astype(BF16)
            dvn.append(_gmlp_mix(bdt_ref, dmb, grp))
            for g in range(4):
                acc_w[g] += _dg(jnp.where(grp == g, dmb, jnp.zeros_like(dmb)), vs, NT)
        dug = jnp.concatenate(dug, axis=0)
        dvn = jnp.concatenate(dvn, axis=0)
        dsg_ref[...] += jnp.sum(dvn * xh, axis=0, keepdims=True)
        gd = sg * dvn
        dx = r * (gd - xh * jnp.mean(xh * gd, axis=-1, keepdims=True))
        dp_ref[:, 0:256] = (dug * _gelu_grad(u)).astype(BF16)
        dp_ref[:, 256:512] = (dx * _gelu_grad(sv)).astype(BF16)

        @pl.when(i == nt - 1)
        def _():
            for g in range(4):
                dws_ref[g] = acc_w[g, 0:CHUNK, 0:CHUNK] + acc_w[g, CHUNK:C2, CHUNK:C2]
            dbs_ref[...] = jnp.dot(acc_b[0:CHUNK, :] + acc_b[CHUNK:C2, :], gsum_ref[...],
                                   preferred_element_type=F32, precision=lax.Precision.HIGHEST)

    sec = pl.BlockSpec((tm, 2 * SG_W), lambda i: (i, 3))
    return pl.pallas_call(
        body, grid=(nt,), name="gmlp_bwd",
        in_specs=[ANY, pl.BlockSpec((tm, SG_W), lambda i: (i, 0)), sec, _full(sgn.shape), _full(bd.shape),
                  _full(bdt.shape), _full(bias.shape), _full(gsum.shape)],
        out_specs=[sec, _full((1, SG_W)), _full((4, CHUNK, CHUNK)), _full((CHUNK, LANES))],
        out_shape=[jax.ShapeDtypeStruct(dp.shape, BF16), jax.ShapeDtypeStruct((1, SG_W), F32),
                   jax.ShapeDtypeStruct((4, CHUNK, CHUNK), F32), jax.ShapeDtypeStruct((CHUNK, LANES), F32)],
        scratch_shapes=[pltpu.VMEM((4, C2, C2), F32), pltpu.VMEM((C2, SG_W), F32)],
        input_output_aliases={0: 0}, compiler_params=_cp("arbitrary"))(dp, dy, p, sgn, bd, bdt, bias, gsum)


def _merge_fwd(ya, at, yc, p, wa, wb, wc):
    T = p.shape[0]
    tm = _tile(T, 528)
    n = wa.shape[2]

    def body(ya_ref, at_ref, yc_ref, ga_ref, gb_ref, gc_ref, wa_ref, wb_ref, wc_ref, o_ref):
        yav, atv, ycv = ya_ref[...], at_ref[...], yc_ref[...]
        for j in range(N_CHIPS):
            cs = slice(j * n, (j + 1) * n)
            m = (_sigmoid(ga_ref[:, cs]) * _dot(yav, wa_ref[j]) + _sigmoid(gb_ref[:, cs]) * _dot(atv, wb_ref[j])
                 + _sigmoid(gc_ref[:, cs]) * _dot(ycv, wc_ref[j]))
            o_ref[:, cs] = m.astype(BF16)

    def rows(w, col=0):
        return pl.BlockSpec((tm, w), lambda i: (i, col))

    return pl.pallas_call(
        body, grid=(T // tm,), name="merge_fwd",
        in_specs=[rows(CONV_W), rows(512), rows(SG_W), rows(D_MODEL, 2), rows(D_MODEL, 3), rows(D_MODEL, 4),
                  _full(wa.shape), _full(wb.shape), _full(wc.shape)],
        out_specs=rows(D_MODEL), out_shape=jax.ShapeDtypeStruct((T, D_MODEL), BF16),
        compiler_params=_cp("parallel"))(ya, at, yc, p, p, p, wa, wb, wc)


def _merge_bwd(dyo, ya, at, yc, p, wa, wb, wc, wo):
    T = p.shape[0]
    tm = _tile(T, 528)
    n = wa.shape[2]

    def body(dyo_ref, ya_ref, at_ref, yc_ref, ga_ref, gb_ref, gc_ref, wa_ref, wb_ref, wc_ref, wo_ref,
             dp_ref, dya_ref, doa_ref, dyc_ref, dwa_ref, dwb_ref, dwc_ref):
        i = pl.program_id(0)

        @pl.when(i == 0)
        def _():
            dwa_ref[...] = jnp.zeros_like(dwa_ref)
            dwb_ref[...] = jnp.zeros_like(dwb_ref)
            dwc_ref[...] = jnp.zeros_like(dwc_ref)

        dp_ref[:, 0:OFF_G] = jnp.zeros((tm, OFF_G), BF16)
        dm = _dg(dyo_ref[...], wo_ref[...], NT)
        yav, atv, ycv = ya_ref[...], at_ref[...], yc_ref[...]
        dya = jnp.zeros((tm, CONV_W), F32)
        dat = jnp.zeros((tm, 512), F32)
        dyc = jnp.zeros((tm, SG_W), F32)
        for j in range(N_CHIPS):
            cs = slice(j * n, (j + 1) * n)
            dmj = dm[:, cs]
            for y_in, w_ref, g_ref, dw_ref, which in (
                    (yav, wa_ref, ga_ref, dwa_ref, 0), (atv, wb_ref, gb_ref, dwb_ref, 1),
                    (ycv, wc_ref, gc_ref, dwc_ref, 2)):
                sg = _sigmoid(g_ref[:, cs])
                y = _dot(y_in, w_ref[j])
                c0 = OFF_G + which * D_MODEL + j * n
                dp_ref[:, c0:c0 + n] = (dmj * y * sg * (1.0 - sg)).astype(BF16)
                dyb = (dmj * sg).astype(BF16)
                dw_ref[j] += _dg(y_in, dyb, TN)
                back = _dg(dyb, w_ref[j], NT)
                if which == 0:
                    dya = dya + back
                elif which == 1:
                    dat = dat + back
                else:
                    dyc = dyc + back
        dya_ref[...] = dya
        dyc_ref[...] = dyc
        prod = dat * atv.astype(F32)
        lane = _lane((tm, LANES))
        dat_rows = _heads_to_rows(dat, N_Q_HEADS)
        for h in range(N_Q_HEADS):
            grp = prod[:, (h // 2) * LANES:(h // 2 + 1) * LANES]
            keep = (lane < 64) if h % 2 == 0 else (lane >= 64)
            delta = jnp.sum(jnp.where(keep, grp, 0.0), axis=1, keepdims=True)
            doa_ref[h] = _aug(dat_rows[h], delta).astype(BF16)

    def rows(w, col=0):
        return pl.BlockSpec((tm, w), lambda i: (i, col))

    return pl.pallas_call(
        body, grid=(T // tm,), name="merge_bwd",
        in_specs=[rows(D_MODEL), rows(CONV_W), rows(512), rows(SG_W), rows(D_MODEL, 2), rows(D_MODEL, 3),
                  rows(D_MODEL, 4), _full(wa.shape), _full(wb.shape), _full(wc.shape), _full(wo.shape)],
        out_specs=[rows(IN_W), rows(CONV_W),
                   pl.BlockSpec((N_Q_HEADS, tm, LANES), lambda i: (0, i, 0)), rows(SG_W),
                   _full(wa.shape), _full(wb.shape), _full(wc.shape)],
        out_shape=[jax.ShapeDtypeStruct((T, IN_W), BF16)] + [
            jax.ShapeDtypeStruct((T, CONV_W), F32), jax.ShapeDtypeStruct((N_Q_HEADS, T, LANES), BF16),
            jax.ShapeDtypeStruct((T, SG_W), F32), jax.ShapeDtypeStruct(wa.shape, F32),
            jax.ShapeDtypeStruct(wb.shape, F32), jax.ShapeDtypeStruct(wc.shape, F32)],
        compiler_params=_cp("arbitrary"))(dyo, ya, at, yc, p, p, p, wa, wb, wc, wo)


def _loss_grad(xf, tgt, n_lat):
    T, D = xf.shape
    tm = _tile(np.gcd(n_lat, T), 512)
    nl = n_lat // tm

    def body(x_ref, t_ref, dy_ref, l_ref):
        i = pl.program_id(0)

        @pl.when(i == 0)
        def _():
            l_ref[...] = jnp.zeros_like(l_ref)

        @pl.when(i < nl)
        def _():
            err = x_ref[...] - t_ref[...]
            dy_ref[...] = err * (1.0 / D)
            sq = jnp.sum(jnp.sum(err * err, axis=1, keepdims=True), axis=0, keepdims=True)
            l_ref[...] += (0.5 / D) * sq

        @pl.when(i >= nl)
        def _():
            dy_ref[...] = jnp.zeros_like(dy_ref)

    return pl.pallas_call(
        body, grid=(T // tm,), name="loss_grad",
        in_specs=[pl.BlockSpec((tm, D), lambda i: (i, 0)), pl.BlockSpec((tm, D), lambda i: (jnp.minimum(i, nl - 1), 0))],
        out_specs=[pl.BlockSpec((tm, D), lambda i: (i, 0)), _full((8, LANES))],
        out_shape=[jax.ShapeDtypeStruct((T, D), F32), jax.ShapeDtypeStruct((8, LANES), F32)],
        compiler_params=_cp("arbitrary"))(xf, tgt)


def _row_tile(R, C):
    if R * C <= (1 << 19) or R % 8:
        return R
    return _tile(R, max(8, (1 << 19) // C), 8)


def _adamw(w, m, v, g1, g2=None):
    shape = w.shape
    C = shape[-1]
    R = int(np.prod(shape[:-1])) if len(shape) > 1 else 1
    tr = _row_tile(R, C)
    ins = [a.reshape(R, C) for a in ((w, m, v, g1) if g2 is None else (w, m, v, g1, g2))]

    def body(*refs):
        w_ref, m_ref, v_ref = refs[0], refs[1], refs[2]
        g_ref, d_ref, m2_ref, v2_ref = refs[-4:]
        g = refs[3][...] if g2 is None else refs[3][...] + refs[4][...]
        m2 = ADAM_B1 * m_ref[...] + (1.0 - ADAM_B1) * g
        v2 = ADAM_B2 * v_ref[...] + (1.0 - ADAM_B2) * (g * g)
        m_hat = m2 / (1.0 - ADAM_B1 ** ADAM_STEP)
        v_hat = v2 / (1.0 - ADAM_B2 ** ADAM_STEP)
        g_ref[...] = g
        d_ref[...] = -ADAM_LR * (m_hat / (jnp.sqrt(v_hat) + ADAM_EPS) + ADAM_WD * w_ref[...])
        m2_ref[...] = m2
        v2_ref[...] = v2

    spec = pl.BlockSpec((tr, C), lambda i: (i, 0))
    outs = pl.pallas_call(
        body, grid=(R // tr,), name="adamw", in_specs=[spec] * len(ins), out_specs=[spec] * 4,
        out_shape=[jax.ShapeDtypeStruct((R, C), F32)] * 4, compiler_params=_cp("parallel"))(*ins)
    return [o.reshape(shape) for o in outs]


def _adamw_layer(w, m, v, gs, l, prev):
    L, a, b = w.shape
    tr = _row_tile(a, b)
    n_in = 3 + len(gs)

    def body(*refs):
        g_ref, d_ref, m2_ref, v2_ref = refs[-4:]
        g = refs[3][...]
        for r in refs[4:3 + len(gs)]:
            g = g + r[...]
        m2 = ADAM_B1 * refs[1][...] + (1.0 - ADAM_B1) * g
        v2 = ADAM_B2 * refs[2][...] + (1.0 - ADAM_B2) * (g * g)
        m_hat = m2 / (1.0 - ADAM_B1 ** ADAM_STEP)
        v_hat = v2 / (1.0 - ADAM_B2 ** ADAM_STEP)
        g_ref[...] = g
        d_ref[...] = -ADAM_LR * (m_hat / (jnp.sqrt(v_hat) + ADAM_EPS) + ADAM_WD * refs[0][...])
        m2_ref[...] = m2
        v2_ref[...] = v2

    layer = pl.BlockSpec((None, tr, b), lambda i: (l, i, 0))
    outs = pl.pallas_call(
        body, grid=(a // tr,), name="adamw_layer",
        in_specs=[layer] * 3 + [pl.BlockSpec((tr, b), lambda i: (i, 0))] * len(gs) + [ANY] * len(prev),
        out_specs=[layer] * 4, out_shape=[jax.ShapeDtypeStruct((L, a, b), F32)] * 4,
        input_output_aliases={n_in + k: k for k in range(len(prev))},
        compiler_params=_cp("parallel"))(w, m, v, *gs, *prev)
    return list(outs)


def _sum_lead(x, name):
    n, R, C = x.shape
    tr = _row_tile(R, C * n)

    def body(x_ref, o_ref):
        acc = x_ref[0].astype(F32)
        for s in range(1, n):
            acc = acc + x_ref[s].astype(F32)
        o_ref[...] = acc

    return pl.pallas_call(
        body, grid=(R // tr,), name=name, in_specs=[pl.BlockSpec((n, tr, C), lambda i: (0, i, 0))],
        out_specs=pl.BlockSpec((tr, C), lambda i: (i, 0)), out_shape=jax.ShapeDtypeStruct((R, C), F32),
        compiler_params=_cp("parallel"))(x)


def _silu(x):
    return x * _sigmoid(x)


def _mod_fwd(a_raw, w_mod, bsh):
    L, D, n = w_mod.shape

    def body(a_ref, w_ref, b_ref, o_ref):
        o_ref[...] = _dot(_silu(a_ref[...]).astype(BF16), w_ref[...].astype(BF16)) + b_ref[...]

    return pl.pallas_call(
        body, grid=(L,), name="mod_fwd",
        in_specs=[_full(a_raw.shape), pl.BlockSpec((None, D, n), lambda l: (l, 0, 0)),
                  pl.BlockSpec((None, 1, n), lambda l: (l, 0, 0))],
        out_specs=pl.BlockSpec((None, 16, n), lambda l: (l, 0, 0)),
        out_shape=jax.ShapeDtypeStruct((L, 16, n), F32), compiler_params=_cp("parallel"))(a_raw, w_mod, bsh)


def _wmod_grad(a_raw, dms):
    L, _, n = dms.shape
    D = a_raw.shape[1]

    def body(a_ref, dm_ref, o_ref):
        o_ref[...] = _dg(_silu(a_ref[...]).astype(BF16), dm_ref[...].astype(BF16), TN)

    return pl.pallas_call(
        body, grid=(L,), name="wmod_grad",
        in_specs=[_full(a_raw.shape), pl.BlockSpec((None, 16, n), lambda l: (l, 0, 0))],
        out_specs=pl.BlockSpec((None, D, n), lambda l: (l, 0, 0)),
        out_shape=jax.ShapeDtypeStruct((L, D, n), F32), compiler_params=_cp("parallel"))(a_raw, dms)


def _cctx_partial(dmc, w_mod):
    L, D, n = w_mod.shape

    def body(dm_ref, w_ref, o_ref):
        part = _dg(dm_ref[...].astype(BF16), w_ref[...].astype(BF16), NT)

        @pl.when(pl.program_id(0) == 0)
        def _():
            o_ref[...] = part

        @pl.when(pl.program_id(0) > 0)
        def _():
            o_ref[...] += part

    return pl.pallas_call(
        body, grid=(L,), name="cctx_partial",
        in_specs=[pl.BlockSpec((None, 16, n), lambda l: (l, 0, 0)), pl.BlockSpec((None, D, n), lambda l: (l, 0, 0))],
        out_specs=_full((16, D)), out_shape=jax.ShapeDtypeStruct((16, D), F32),
        compiler_params=_cp("arbitrary"))(dmc, w_mod)


def _cctx_final(parts, cc):
    def body(p_ref, c_ref, o_ref):
        s = p_ref[0, 0:8, :]
        for j in range(1, N_CHIPS):
            s = s + p_ref[2 * j, 0:8, :]
        xv = c_ref[...]
        sg = _sigmoid(xv)
        o_ref[...] = s * (sg * (1.0 + xv * (1.0 - sg)))

    return pl.pallas_call(
        body, name="cctx_final", in_specs=[_full(parts.shape), _full(cc.shape)], out_specs=_full((8, LANES)),
        out_shape=jax.ShapeDtypeStruct((8, LANES), F32), compiler_params=_cp())(parts, cc)


def _me():
    return lax.axis_index("x"), lax.axis_index("y"), lax.axis_index("c")


def _flip(v, bit):
    return 1 - v if bit else v


def _remote(src, dst, ssem, rsem, peer):
    return pltpu.make_async_remote_copy(src_ref=src, dst_ref=dst, send_sem=ssem, recv_sem=rsem,
                                        device_id=peer, device_id_type=MESH_ID)


def _ag8(xb, name):
    R = xb.shape[0]

    def pallas(x):
        def body(x_ref, o_ref, ssem, rsem):
            mx, my, mc = _me()
            me = 4 * mx + 2 * my + mc
            sib = (mx, my, 1 - mc)
            peers = _plane_peers(mx, my, mc)
            sends = [_remote(x_ref, o_ref.at[me], ssem.at[0], rsem.at[0], sib)]
            sends += [_remote(x_ref, o_ref.at[me], ssem.at[1 + k], rsem.at[1 + k], peer)
                      for k, (peer, _) in enumerate(peers)]
            for cp in sends:
                cp.start()
            for k, (peer, pj) in enumerate(peers):
                got = o_ref.at[2 * pj + mc]
                _remote(got, got, ssem.at[1 + k], rsem.at[1 + k], peer).wait_recv()
                fw = _remote(got, got, ssem.at[4 + k], rsem.at[4 + k], sib)
                fw.start()
                sends.append(fw)
            _remote(x_ref, o_ref.at[4 * mx + 2 * my + 1 - mc], ssem.at[0], rsem.at[0], sib).wait_recv()
            for k, (_, pj) in enumerate(peers):
                theirs = o_ref.at[2 * pj + 1 - mc]
                _remote(theirs, theirs, ssem.at[4 + k], rsem.at[4 + k], sib).wait_recv()
            for cp in sends:
                cp.wait_send()

        return pl.pallas_call(
            body, name=name, in_specs=[ANY], out_specs=ANY, out_shape=jax.ShapeDtypeStruct((N_DEV, R, LANES), F32),
            scratch_shapes=[pltpu.SemaphoreType.DMA((N_DEV - 1,)), pltpu.SemaphoreType.DMA((N_DEV - 1,))])(x)

    mx, my, mc = _me()
    return lax.dynamic_update_slice(pallas(xb), xb[None], (4 * mx + 2 * my + mc, 0, 0))


def _plane_peers(mx, my, mc):
    out = []
    for k in range(1, N_CHIPS):
        px, py = _flip(mx, k & 2), _flip(my, k & 1)
        out.append(((px, py, mc), 2 * px + py))
    return out


def _plane_exchange(ins, outs, ssem, rsem, scatter):
    n = len(ins)

    def desc(k, a, arriving):
        mx, my, mc = _me()
        j = 2 * mx + my
        peer, pj = _plane_peers(mx, my, mc)[k]
        src = ins[a].at[pj if scatter else j]
        dst = outs[a].at[pj if arriving else j]
        return _remote(src, dst, ssem.at[k * n + a], rsem.at[k * n + a], peer)

    def start():
        for k in range(N_CHIPS - 1):
            for a in range(n):
                desc(k, a, False).start()

    def wait():
        for k in range(N_CHIPS - 1):
            for a in range(n):
                desc(k, a, True).wait_recv()
        for k in range(N_CHIPS - 1):
            for a in range(n):
                desc(k, a, False).wait_send()

    return start, wait


def _chip_gather(bufs, name):
    n = len(bufs)
    halves = [b.shape[1] // 2 for b in bufs]

    def body(*refs):
        outs = refs[n:2 * n]
        ssem, rsem, fsem, gsem = refs[2 * n:]
        mx, my, mc = _me()
        j = 2 * mx + my
        sib = (mx, my, 1 - mc)

        def half(a, blk, c):
            return outs[a].at[blk, pl.ds(c * halves[a], halves[a]), :]

        peers = _plane_peers(mx, my, mc)
        sends = []
        for k, (peer, _) in enumerate(peers):
            for a in range(n):
                mine = half(a, j, mc)
                cp = _remote(mine, mine, ssem.at[k * n + a], rsem.at[k * n + a], peer)
                cp.start()
                sends.append(cp)
        for k, (peer, pj) in enumerate(peers):
            for a in range(n):
                got = half(a, pj, mc)
                _remote(got, got, ssem.at[k * n + a], rsem.at[k * n + a], peer).wait_recv()
                fw = _remote(got, got, fsem.at[k * n + a], gsem.at[k * n + a], sib)
                fw.start()
                sends.append(fw)
        for k, (_, pj) in enumerate(peers):
            for a in range(n):
                theirs = half(a, pj, 1 - mc)
                _remote(theirs, theirs, fsem.at[k * n + a], gsem.at[k * n + a], sib).wait_recv()
        for cp in sends:
            cp.wait_send()

    sems = pltpu.SemaphoreType.DMA((3 * n,))
    return pl.pallas_call(
        body, name=name, in_specs=[ANY] * n, out_specs=[ANY] * n,
        out_shape=[jax.ShapeDtypeStruct(b.shape, b.dtype) for b in bufs],
        input_output_aliases={a: a for a in range(n)},
        scratch_shapes=[sems, sems, sems, sems])(*bufs)


def _chip_scatter(gs, name):
    n = len(gs)

    def body(*refs):
        ins, outs = refs[:n], refs[n:2 * n]
        ssem, rsem = refs[2 * n:]
        mx, my, mc = _me()
        j = 2 * mx + my
        peers = _plane_peers(mx, my, mc)
        sends = []
        for k, (peer, pj) in enumerate(peers):
            for a in range(n):
                cp = _remote(ins[a].at[pj], outs[a].at[j], ssem.at[k * n + a], rsem.at[k * n + a], peer)
                cp.start()
                sends.append(cp)
        for k, (peer, pj) in enumerate(peers):
            for a in range(n):
                _remote(ins[a].at[pj], outs[a].at[pj], ssem.at[k * n + a], rsem.at[k * n + a], peer).wait_recv()
        for cp in sends:
            cp.wait_send()

    return pl.pallas_call(
        body, name=name, in_specs=[ANY] * n, out_specs=[ANY] * n,
        out_shape=[jax.ShapeDtypeStruct(g.shape, g.dtype) for g in gs],
        scratch_shapes=[pltpu.SemaphoreType.DMA((3 * n,)), pltpu.SemaphoreType.DMA((3 * n,))])(*gs)


def _sibling_swap(xs, name):
    n = len(xs)

    def body(*refs):
        ins, outs = refs[:n], refs[n:2 * n]
        ssem, rsem = refs[2 * n:]
        mx, my, mc = _me()
        cps = [_remote(ins[a], outs[a], ssem.at[a], rsem.at[a], (mx, my, 1 - mc)) for a in range(n)]
        for cp in cps:
            cp.start()
        for cp in cps:
            cp.wait()

    return pl.pallas_call(
        body, name=name, in_specs=[ANY] * n, out_specs=[ANY] * n,
        out_shape=[jax.ShapeDtypeStruct(x.shape, x.dtype) for x in xs],
        scratch_shapes=[pltpu.SemaphoreType.DMA((n,)), pltpu.SemaphoreType.DMA((n,))])(*xs)


def _sibling_halves(gs, name):
    n = len(gs)

    def body(*refs):
        ins, outs = refs[:n], refs[n:2 * n]
        ssem, rsem = refs[2 * n:]
        mx, my, mc = _me()
        cps = []
        for a in range(n):
            h = gs[a].shape[1] // 2
            cps.append(_remote(ins[a].at[:, pl.ds((1 - mc) * h, h), :], outs[a], ssem.at[a], rsem.at[a],
                               (mx, my, 1 - mc)))
        for cp in cps:
            cp.start()
        for cp in cps:
            cp.wait()

    return pl.pallas_call(
        body, name=name, in_specs=[ANY] * n, out_specs=[ANY] * n,
        out_shape=[jax.ShapeDtypeStruct((g.shape[0], g.shape[1] // 2, g.shape[2]), g.dtype) for g in gs],
        scratch_shapes=[pltpu.SemaphoreType.DMA((n,)), pltpu.SemaphoreType.DMA((n,))])(*gs)


def _sibling_fill(hs, name):
    n = len(hs)

    def body(*refs):
        ins, outs = refs[:n], refs[n:2 * n]
        ssem, rsem = refs[2 * n:]
        mx, my, mc = _me()
        cps = []
        for a in range(n):
            h = hs[a].shape[0]
            cps.append(_remote(ins[a], outs[a].at[pl.ds(mc * h, h), :], ssem.at[a], rsem.at[a], (mx, my, 1 - mc)))
        for cp in cps:
            cp.start()
        for a, cp in enumerate(cps):
            h = hs[a].shape[0]
            theirs = outs[a].at[pl.ds((1 - mc) * h, h), :]
            _remote(ins[a], theirs, ssem.at[a], rsem.at[a], (mx, my, 1 - mc)).wait_recv()
            cp.wait_send()

    return pl.pallas_call(
        body, name=name, in_specs=[ANY] * n, out_specs=[ANY] * n,
        out_shape=[jax.ShapeDtypeStruct((2 * x.shape[0], x.shape[1]), x.dtype) for x in hs],
        scratch_shapes=[pltpu.SemaphoreType.DMA((n,)), pltpu.SemaphoreType.DMA((n,))])(*hs)


def _add_cast(g, sb):
    J, h, b = g.shape
    th = _row_tile(h, b * J)

    def body(g_ref, s_ref, o_ref):
        o_ref[...] = (g_ref[...].astype(F32) + s_ref[...].astype(F32)).astype(BF16)

    spec = pl.BlockSpec((J, th, b), lambda i: (0, i, 0))
    return pl.pallas_call(
        body, grid=(h // th,), name="add_planes", in_specs=[spec, spec], out_specs=spec,
        out_shape=jax.ShapeDtypeStruct((J, h, b), BF16), compiler_params=_cp("parallel"))(g, sb)


_WEIGHTS = ("c_ctx", "w_mod", "b_mod", "norm1", "w_in", "q_gain", "k_gain", "conv_w", "sg_norm", "w_s", "b_s",
            "w_a", "w_b", "w_c", "w_o", "norm2", "w_ff1", "w_ff3", "w_ff2")
_BIG = ("w_in", "w_a", "w_b", "w_c", "w_o", "w_ff1", "w_ff3", "w_ff2")
_TRANSPOSED = ("w_ff1", "w_ff3")


def _constants():
    idx = np.arange(LANES)
    e = (idx[:, None] // 64 == idx[None, :] // 64).astype(np.float32) / 64.0
    c512 = np.arange(512)
    fold = (c512[:, None] % 64 == idx[None, :]).astype(np.float32)
    c256 = np.arange(SG_W)
    gsum = (c256[:, None] // 64 == idx[None, :]).astype(np.float32)
    return jnp.asarray(e, BF16), jnp.asarray(fold, F32), jnp.asarray(gsum, F32)


def _rope_tables(n_lat, n_ctx):
    t = jnp.arange(n_lat)
    inv = ROPE_THETA ** (-jnp.arange(0, HEAD_DIM // 2, 2, dtype=F32) / (HEAD_DIM // 2))
    ar = (t // GRID_W).astype(F32)[:, None] * inv
    ac = (t % GRID_W).astype(F32)[:, None] * inv
    cos = jnp.concatenate([jnp.cos(ar), jnp.cos(ar), jnp.cos(ac), jnp.cos(ac)], axis=1)
    sin = jnp.concatenate([-jnp.sin(ar), jnp.sin(ar), -jnp.sin(ac), jnp.sin(ac)], axis=1)
    cos = jnp.concatenate([cos, jnp.ones((n_ctx, HEAD_DIM), F32)], axis=0)
    sin = jnp.concatenate([sin, jnp.zeros((n_ctx, HEAD_DIM), F32)], axis=0)
    return jnp.concatenate([cos, cos], axis=1), jnp.concatenate([sin, sin], axis=1)


def kernel(x, c, ctx, c_ctx, w_mod, b_mod, norm1, w_in, q_gain, k_gain, conv_w, sg_norm, w_s, b_s, w_a, w_b, w_c, w_o, norm2, w_ff1, w_ff3, w_ff2, loss_target, m_c_ctx, m_w_mod, m_b_mod, m_norm1, m_w_in, m_q_gain, m_k_gain, m_conv_w, m_sg_norm, m_w_s, m_b_s, m_w_a, m_w_b, m_w_c, m_w_o, m_norm2, m_w_ff1, m_w_ff3, m_w_ff2, v_c_ctx, v_w_mod, v_b_mod, v_norm1, v_w_in, v_q_gain, v_k_gain, v_conv_w, v_sg_norm, v_w_s, v_b_s, v_w_a, v_w_b, v_w_c, v_w_o, v_norm2, v_w_ff1, v_w_ff3, v_w_ff2):
    given = dict(locals())
    mx, my, mc = _me()
    chip = 2 * mx + my
    dev = 4 * mx + 2 * my + mc
    L = norm1.shape[0]
    S, Lc = x.shape[1], ctx.shape[1]
    T = S + Lc
    D = D_MODEL
    n_mod, n_in, n_ff = w_mod.shape[2], w_in.shape[2], w_ff1.shape[2]
    n_cw = conv_w.shape[2]
    e_avg, fold, gsum = _constants()
    cos_t, sin_t = _rope_tables(S, Lc)

    cw_rows = (L * 3 * n_cw) // LANES
    pad = (-(8 + cw_rows)) % 8
    buf = jnp.concatenate([c.reshape(8, LANES), conv_w.reshape(cw_rows, LANES), jnp.zeros((pad, LANES), F32)], axis=0)
    g1 = _ag8(buf, "gather_cond")
    conds = g1[:, :8].reshape(N_DEV, D)
    cw_full = jnp.stack([g1[2 * j, 8:8 + cw_rows].reshape(L, 3, n_cw) for j in range(N_CHIPS)], axis=2)
    cw_full = cw_full.reshape(L, 3, N_CHIPS * n_cw)
    cw8 = jnp.pad(cw_full, ((0, 0), (0, 5), (0, 0)))
    a_raw = jnp.concatenate([conds, c_ctx[None], jnp.zeros((7, D), F32)], axis=0)
    bsh = lax.dynamic_slice_in_dim(b_mod, chip * n_mod, n_mod, axis=1)[:, None, :]
    mod_sh = _mod_fwd(a_raw, w_mod, bsh)
    g2 = _ag8(mod_sh.reshape(-1, LANES), "gather_mod")
    mods = jnp.stack([g2[2 * j].reshape(L, 16, n_mod) for j in range(N_CHIPS)], axis=2).reshape(L, 16, N_CHIPS * n_mod)
    lat = lax.dynamic_index_in_dim(mods, dev, axis=1, keepdims=False)
    mod = jnp.stack([lat.reshape(L, 6, D), mods[:, 8].reshape(L, 6, D)], axis=1)
    mod = jnp.pad(mod, ((0, 0), (0, 0), (0, 2), (0, 0)))

    qg = jnp.tile(q_gain, (1, N_Q_HEADS))[:, None, :]
    kg = jnp.tile(k_gain, (1, N_KV_HEADS))[:, None, :]
    sgn = sg_norm[:, None, :]
    ws_b = w_s.astype(BF16)
    zero = jnp.zeros_like(ws_b)
    bd = jnp.concatenate([jnp.concatenate([ws_b, zero], axis=3), jnp.concatenate([zero, ws_b], axis=3)], axis=2)
    bdt = jnp.swapaxes(bd, 2, 3)
    bias = jnp.tile(jnp.repeat(jnp.swapaxes(b_s, 1, 2), SG_W // 4, axis=2), (1, 2, 1))

    def shard_bufs(l, names=_BIG):
        return [lax.dynamic_update_slice(lax.empty((N_CHIPS,) + given[nm].shape[1:], BF16),
                                         given[nm][l].astype(BF16)[None], (chip, 0, 0)) for nm in names]

    def unpack(bufs):
        win, wa, wb, wc, wo, w1, w3, w2 = bufs
        return win, wa, wb, wc, wo.reshape(1, D, D), w1, w3, w2

    def layer_fwd(X, l, W, nxt):
        win = W[0]
        h, ht = _norm_mod(X, norm1[l][None], mod[l], 0, 1, S, True)
        p = _mm_nn(h, win, F32, "in_proj")
        ya = _conv_fwd(p, cw8[l], S)
        q, k, v = _qkv_prep(p, cos_t, sin_t, qg[l], kg[l], e_avg)
        at, qa, *got = _flash_fwd(q.reshape(N_KV_HEADS, GROUP, T, LANES), k, v, S, nxt)
        if len(W) == 1:
            W, got = unpack([win] + got[:len(_BIG) - 1]), got[len(_BIG) - 1:]
        win, wa, wb, wc, wo, w1, w3, w2 = W
        yc = _gmlp_fwd(p, sgn[l], bd[l], bias[l])
        mg = _merge_fwd(ya, at, yc, p, wa, wb, wc)
        X1, f1 = _mm_res(mg[None], wo, X, mod[l], 2, S, "out_proj")
        h2, = _norm_mod(X1, norm2[l][None], mod[l], 3, 4, S, False)
        a1, a3, act = _ffn_up(h2, w1, w3)
        X2, f2 = _mm_res(act, w2, X1, mod[l], 5, S, "ffn_down")
        return X2, W, got, dict(X=X, ht=ht, h2=h2, p=p, ya=ya, k=k, v=v, at=at, qa=qa, yc=yc, mg=mg, X1=X1, f1=f1,
                             a1=a1, a3=a3, act=act, f2=f2)

    def layer_bwd(dX2, dyf, l, W, sv, pending):
        win, wa, wb, wc, wo, w1, w3, w2 = W
        da1, da3 = _ffn_down_bwd(dyf, w2, sv["a1"], sv["a3"])
        dw2 = _mm_tn(sv["act"], dyf, _shard_rows(n_ff), _rows(D), N_CHIPS, n_ff, D, T, "dw_ff2")
        dh2 = _mm_nt_acc([da1, da3], [w1, w3], False, "ffn_up_bwd")
        dw1 = _mm_tn(da1, sv["h2"], _shard_rows(n_ff), _rows(D), N_CHIPS, n_ff, D, T, "dw_ff1")
        dw3 = _mm_tn(da3, sv["h2"], _shard_rows(n_ff), _rows(D), N_CHIPS, n_ff, D, T, "dw_ff3")
        dX1, dn2, dsh2, dsc2, dyo, dgt1 = _norm_mod_bwd(sv["X1"], dh2, dX2, norm2[l][None], mod[l], 4, S,
                                                        (sv["f1"], mod[l], 2))
        dwo = _mm_tn(sv["mg"], dyo, _rows(D), _rows(D), 1, D, D, T, "dw_o")
        dp, dya, doa, dyc, dwa, dwb, dwc = _merge_bwd(dyo, sv["ya"], sv["at"], sv["yc"], sv["p"], wa, wb, wc, wo[0])
        dp, dcw = _conv_bwd(dp, dya, sv["p"], cw8[l], S)
        dp, dsg, dws, dbs = _gmlp_bwd(dp, dyc, sv["p"], sgn[l], bd[l], bdt[l], bias[l], gsum)
        early = [dwa.astype(BF16), dwb.astype(BF16), dwc.astype(BF16), dwo.reshape(N_CHIPS, D // N_CHIPS, D),
                 dw1, dw3, dw2]
        dq, dk, dv, *recv = _flash_bwd(sv["qa"], doa.reshape(N_KV_HEADS, GROUP, T, LANES), sv["k"], sv["v"], S,
                                       list(pending) + (early if l == 0 else []))
        dp, dqg, dkg = _qkv_prep_bwd(dp, dq, dk, dv, sv["p"], cos_t, sin_t,
                                     qg[l], kg[l], e_avg, fold)
        dh = _mm_nt_acc([dp], [win], True, "in_proj_bwd")
        dwin = _mm_dw(sv["ht"], dp, _row_cols(n_in), N_CHIPS, n_in, "dw_in")
        below = (saved[l - 1]["f2"], mod[l - 1], 5) if l else None
        dX0, dn1, dsh1, dsc1, *nxt = _norm_mod_bwd(sv["X"], dh, dX1, norm1[l][None], mod[l], 1, S, below)
        dmod = [dsh1, dsc1, dgt1, dsh2, dsc2]
        big = [dwin] + early
        small = dict(norm1=dn1[0], norm2=dn2[0], q_gain=dqg[0, :HEAD_DIM], k_gain=dkg[0, :HEAD_DIM],
                     conv_w=dcw[:3], sg_norm=dsg[0], w_s=dws, b_s=jnp.swapaxes(dbs[:, :4], 0, 1), dmod=dmod)
        return dX0, nxt, big, small, recv

    X = jnp.concatenate([x[0], ctx[0]], axis=0)
    Ws, saved = [_chip_gather(shard_bufs(0, _BIG[:1]), "gather_weights")], []
    for l in range(L):
        nxt = (shard_bufs(0, _BIG[1:]) if l == 0 else []) + (shard_bufs(l + 1) if l + 1 < L else [])
        X, Ws[l], got, sv = layer_fwd(X, l, Ws[l], nxt)
        if got:
            Ws.append(unpack(got))
        saved.append(sv)
    dX, lpart = _loss_grad(X, loss_target[0], S)
    loss = lax.psum(lpart[0, 0], ("x", "y", "c"))

    out = {nm: () for nm in _BIG}
    smalls = [None] * L

    def own_block(r, g):
        return lax.dynamic_update_slice(r, lax.dynamic_slice_in_dim(g, chip, 1, axis=0), (chip, 0, 0))

    def stored(nm, a):
        return jnp.swapaxes(a, 1, 2) if nm in _TRANSPOSED else a

    def update(l, names, grads):
        for nm, g in zip(names, grads):
            out[nm] = _adamw_layer(stored(nm, given[nm]), stored(nm, given["m_" + nm]), stored(nm, given["v_" + nm]),
                                   g, l, out[nm])

    def plane_update(l, names, recv, sent):
        mine = [_sum_lead(own_block(r, g), "sum_chips") for r, g in zip(recv, sent)]
        update(l, names, zip(mine, _sibling_swap(mine, "swap_planes")))

    pending = []
    dyf, dgt2 = _gate_bwd(dX, saved[L - 1]["f2"], mod[L - 1], 5, S)
    for l in reversed(range(L)):
        dX, nxt, big, smalls[l], recv = layer_bwd(dX, dyf, l, Ws[l], saved[l], pending)
        smalls[l]["dmod"] = jnp.concatenate(smalls[l]["dmod"] + [dgt2], axis=1)
        if nxt:
            dyf, dgt2 = nxt
        if recv:
            plane_update(l + 1, _BIG, recv[:len(pending)], pending)
            if l == 0:
                plane_update(0, _BIG[1:], recv[len(pending):], big[1:])
        pending = big
    last = big[:1]
    sib = _sibling_halves(last, "swap_halves")
    own = [lax.dynamic_slice_in_dim(g, mc * (g.shape[1] // 2), g.shape[1] // 2, axis=1) for g in last]
    sent = [_add_cast(g, s_) for g, s_ in zip(own, sib)]
    recv = [own_block(r, g) for r, g in zip(_chip_scatter(sent, "scatter_grads"), sent)]
    halves = [_sum_lead(r, "sum_chips") for r in recv]
    full = _sibling_fill(halves, "fill_halves")
    update(0, _BIG[:1], [(lax.dynamic_update_slice(f, hv, (mc * hv.shape[0], 0)),) for f, hv in zip(full, halves)])
    grad_x = dX[:S][None]

    def flat(nm):
        return jnp.stack([smalls[l][nm] for l in range(L)]).reshape(-1)

    dmod_all = jnp.stack([smalls[l]["dmod"] for l in range(L)])
    dml = dmod_all[:, 0].reshape(-1)
    dmc = dmod_all[:, 1].reshape(-1)
    names = ("norm1", "q_gain", "k_gain", "conv_w", "sg_norm", "w_s", "b_s", "norm2")
    parts = [dml, dml + dmc, dmc] + [flat(nm) for nm in names]
    sizes = [int(a.shape[0]) for a in parts]
    total = sum(sizes)
    padn = (-total) % (8 * LANES)
    sbuf = jnp.concatenate(parts + [jnp.zeros((padn,), F32)]).reshape(-1, LANES)
    g3 = _ag8(sbuf, "gather_small")
    ssum = _sum_lead(g3, "sum_devices").reshape(-1)
    offs = np.cumsum([0] + sizes)
    seg = {nm: ssum[offs[i + 3]:offs[i + 4]] for i, nm in enumerate(names)}
    gb_mod = ssum[offs[1]:offs[2]].reshape(L, N_CHIPS * n_mod)
    dmc_sum = ssum[offs[2]:offs[3]].reshape(L, N_CHIPS * n_mod)
    dml_all = g3.reshape(N_DEV, -1)[:, :sizes[0]].reshape(N_DEV, L, N_CHIPS * n_mod)
    dml_sh = jnp.swapaxes(lax.dynamic_slice_in_dim(dml_all, chip * n_mod, n_mod, axis=2), 0, 1)
    dmc_sh = lax.dynamic_slice_in_dim(dmc_sum, chip * n_mod, n_mod, axis=1)[:, None, :]
    dms = jnp.concatenate([dml_sh, dmc_sh, jnp.zeros((L, 7, n_mod), F32)], axis=1)
    g_wmod = _wmod_grad(a_raw, dms)
    part = _cctx_partial(jnp.concatenate([dmc_sh, jnp.zeros((L, 15, n_mod), F32)], axis=1), w_mod)
    g4 = _ag8(part.reshape(-1, LANES), "gather_cctx")
    g_cctx = _cctx_final(g4, c_ctx.reshape(8, LANES)).reshape(D)

    g_conv = lax.dynamic_slice_in_dim(seg["conv_w"].reshape(L, 3, N_CHIPS * n_cw), chip * n_cw, n_cw, axis=2)
    small_g = dict(c_ctx=g_cctx, w_mod=g_wmod, b_mod=gb_mod, norm1=seg["norm1"].reshape(norm1.shape),
                   q_gain=seg["q_gain"].reshape(q_gain.shape), k_gain=seg["k_gain"].reshape(k_gain.shape),
                   conv_w=g_conv, sg_norm=seg["sg_norm"].reshape(sg_norm.shape), w_s=seg["w_s"].reshape(w_s.shape),
                   b_s=seg["b_s"].reshape(b_s.shape), norm2=seg["norm2"].reshape(norm2.shape))
    res = {}
    for nm in _WEIGHTS:
        if nm in _BIG:
            res[nm] = [stored(nm, o) for o in out[nm]]
        else:
            res[nm] = _adamw(given[nm], given["m_" + nm], given["v_" + nm], small_g[nm])
    return (loss, grad_x, *[res[nm][0] for nm in _WEIGHTS], *[res[nm][1] for nm in _WEIGHTS],
            *[res[nm][2] for nm in _WEIGHTS], *[res[nm][3] for nm in _WEIGHTS])
```

```python
import functools

import jax
import jax.numpy as jnp
import numpy as np
from jax import lax
from jax.experimental import pallas as pl
from jax.experimental.pallas import tpu as pltpu

F32 = jnp.float32
BF16 = jnp.bfloat16
EPS = 1e-6
LOG2E = 1.4426950408889634
D_MODEL = 1024
HEAD_DIM = 64
N_Q_HEADS = 8
N_KV_HEADS = 2
GROUP = N_Q_HEADS // N_KV_HEADS
GRID_W = 64
ROPE_THETA = 10000.0
CHUNK = 128
CONV_W = 256
SG_W = 256
OFF_Q = 3 * CONV_W
QKV_W = 768
OFF_U = OFF_Q + QKV_W
OFF_G = OFF_U + 2 * SG_W
IN_W = OFF_G + 3 * D_MODEL
N_CHIPS = 4
N_DEV = 8
LANES = 128
FWD_KEYS = 256
UNROLL_FWD = 8
UNROLL_BWD = 8
AUG = 3
ADAM_LR, ADAM_B1, ADAM_B2, ADAM_EPS, ADAM_WD, ADAM_STEP = 0.001, 0.9, 0.999, 1e-8, 0.01, 10
VMEM_LIMIT_V7X = 52 * 1024 * 1024
MXU_DEPTH_V7X = 256
MESH_ID = pl.DeviceIdType.MESH
NT = (((1,), (1,)), ((), ()))
TN = (((0,), (0,)), ((), ()))
ANY = pl.BlockSpec(memory_space=pl.ANY)


def _cp(*sem):
    return pltpu.CompilerParams(dimension_semantics=sem or None, vmem_limit_bytes=VMEM_LIMIT_V7X)


def _tile(n, target, mult=16):
    best = None
    for t in range(mult, n + 1, mult):
        if n % t == 0 and t <= target:
            best = t
    assert best is not None, (n, target, mult)
    return best


def _full(shape):
    nd = len(shape)
    return pl.BlockSpec(tuple(shape), lambda *_: (0,) * nd)


def _segments(i, tm, n_lat, fn):
    k, off = divmod(n_lat, tm)

    @pl.when(i < k)
    def _():
        fn(0, tm, 0)

    @pl.when(i == k)
    def _():
        if off:
            fn(0, off, 0)
        fn(off, tm, 1)

    @pl.when(i > k)
    def _():
        fn(0, tm, 1)


def _dot(a, b):
    return jnp.dot(a, b, preferred_element_type=F32)


def _dg(a, b, dims):
    return lax.dot_general(a, b, dims, preferred_element_type=F32)


def _split3(x):
    hi = x.astype(BF16)
    r1 = x - hi.astype(F32)
    mid = r1.astype(BF16)
    lo = (r1 - mid.astype(F32)).astype(BF16)
    return hi.astype(F32), mid.astype(F32), lo.astype(F32)


def _lane(shape):
    return lax.broadcasted_iota(jnp.int32, shape, len(shape) - 1)


def _aug(val, stat):
    lane = _lane(val.shape)
    hi, mid, lo = _split3(stat)
    ext = jnp.where(lane == 64, hi, jnp.where(lane == 65, mid, jnp.where(lane == 66, lo, 0.0)))
    return jnp.where(lane < 64, val, ext)


def _seg_mean(x, e):
    outs = []
    for g in range(x.shape[1] // LANES):
        blk = x[:, g * LANES:(g + 1) * LANES]
        hi = blk.astype(BF16)
        lo = (blk - hi.astype(F32)).astype(BF16)
        outs.append(_dot(hi, e) + _dot(lo, e))
    return outs[0] if len(outs) == 1 else jnp.concatenate(outs, axis=1)


def _rope(x, cos, sin_signed, inverse):
    w = x.shape[1]
    reps = w // LANES
    c = cos if reps == 1 else jnp.tile(cos, (1, reps))
    s = sin_signed if reps == 1 else jnp.tile(sin_signed, (1, reps))
    first = (_lane(x.shape) % 32) < 16
    partner = jnp.where(first, pltpu.roll(x, w - 16, 1), pltpu.roll(x, 16, 1))
    return x * c - partner * s if inverse else x * c + partner * s


def _sigmoid(x):
    return 1.0 / (1.0 + jnp.exp(-x))


_GELU_K = 0.7978845608028654
_GELU_C = 0.044715


def _gelu(x):
    return 0.5 * x * (1.0 + jnp.tanh(_GELU_K * (x + _GELU_C * x * x * x)))


def _gelu_grad(x):
    t = jnp.tanh(_GELU_K * (x + _GELU_C * x * x * x))
    return 0.5 * (1.0 + t) + 0.5 * x * (1.0 - t * t) * _GELU_K * (1.0 + 3.0 * _GELU_C * x * x)


def _loop_unrolled(n, step, init, unroll):
    def trip(t, carry):
        for u in range(unroll):
            carry = step(t * unroll + u, carry)
        return carry

    carry = lax.fori_loop(0, n // unroll, trip, init) if n >= unroll else init
    for r in range(n - n % unroll, n):
        carry = step(r, carry)
    return carry


def _heads_to_rows(x, n_heads):
    out = []
    for h in range(n_heads):
        grp = x[:, (h // 2) * LANES:(h // 2 + 1) * LANES]
        out.append(grp if h % 2 == 0 else pltpu.roll(grp, 64, 1))
    return out


def _rows_to_heads(blocks):
    outs = []
    lane = _lane(blocks[0].shape)
    for a in range(len(blocks) // 2):
        outs.append(jnp.where(lane < 64, blocks[2 * a], pltpu.roll(blocks[2 * a + 1], 64, 1)))
    return outs[0] if len(outs) == 1 else jnp.concatenate(outs, axis=1)


def _norm_mod(x, g, mod, i_shift, i_scale, n_lat, transposed):
    T, D = x.shape
    tm = _tile(T, 768, LANES)

    def body(x_ref, g_ref, mod_ref, h_ref, *ht_ref):
        def fn(r0, r1, seg):
            xv = x_ref[r0:r1, :]
            r = lax.rsqrt(jnp.mean(xv * xv, axis=-1, keepdims=True) + EPS)
            n = xv * r * g_ref[...]
            h = n * (1.0 + mod_ref[seg, i_scale:i_scale + 1, :]) + mod_ref[seg, i_shift:i_shift + 1, :]
            h_ref[r0:r1, :] = h.astype(BF16)

        _segments(pl.program_id(0), tm, n_lat, fn)
        if transposed:
            ht_ref[0][...] = h_ref[...].astype(F32).T.astype(BF16)

    return pl.pallas_call(
        body, grid=(T // tm,), name="norm_mod",
        in_specs=[pl.BlockSpec((tm, D), lambda i: (i, 0)), _full(g.shape), _full(mod.shape)],
        out_specs=[pl.BlockSpec((tm, D), lambda i: (i, 0))] + [pl.BlockSpec((D, tm), lambda i: (0, i))] * transposed,
        out_shape=[jax.ShapeDtypeStruct((T, D), BF16)] + [jax.ShapeDtypeStruct((D, T), BF16)] * transposed,
        compiler_params=_cp("parallel"))(x, g, mod)


def _norm_mod_bwd(x, dh, dres, g, mod, i_scale, n_lat, gate=None):
    T, D = x.shape
    tm = _tile(T, 528)

    def body(x_ref, dh_ref, dres_ref, g_ref, mod_ref, *rest):
        if gate is None:
            dx_ref, dg_ref, dsh_ref, dsc_ref = rest
        else:
            f_ref, gmod_ref, dx_ref, dg_ref, dsh_ref, dsc_ref, dy_ref, dgt_ref = rest
        i = pl.program_id(0)

        @pl.when(i == 0)
        def _():
            dg_ref[...] = jnp.zeros_like(dg_ref)
            dsh_ref[...] = jnp.zeros_like(dsh_ref)
            dsc_ref[...] = jnp.zeros_like(dsc_ref)
            if gate is not None:
                dgt_ref[...] = jnp.zeros_like(dgt_ref)

        def fn(r0, r1, seg):
            xv = x_ref[r0:r1, :]
            dh = dh_ref[r0:r1, :]
            r = lax.rsqrt(jnp.mean(xv * xv, axis=-1, keepdims=True) + EPS)
            xh = xv * r
            gv = g_ref[...]
            dsh_ref[seg] += jnp.sum(dh, axis=0, keepdims=True)
            dsc_ref[seg] += jnp.sum(dh * (xh * gv), axis=0, keepdims=True)
            dn = dh * (1.0 + mod_ref[seg, i_scale:i_scale + 1, :])
            dg_ref[...] += jnp.sum(dn * xh, axis=0, keepdims=True)
            gd = gv * dn
            dxv = dres_ref[r0:r1, :] + r * (gd - xh * jnp.mean(xh * gd, axis=-1, keepdims=True))
            dx_ref[r0:r1, :] = dxv
            if gate is not None:
                dy_ref[r0:r1, :] = (dxv * gmod_ref[seg, gate[2]:gate[2] + 1, :]).astype(BF16)
                dgt_ref[seg] += jnp.sum(dxv * f_ref[r0:r1, :].astype(F32), axis=0, keepdims=True)

        _segments(i, tm, n_lat, fn)

    row = pl.BlockSpec((tm, D), lambda i: (i, 0))
    extra_in = [] if gate is None else [gate[0], gate[1]]
    return pl.pallas_call(
        body, grid=(T // tm,), name="norm_mod_bwd" if gate is None else "norm_gate_bwd",
        in_specs=[row, row, row, _full(g.shape), _full(mod.shape)] + ([] if gate is None else [row, _full(gate[1].shape)]),
        out_specs=[row, _full((1, D)), _full((2, 1, D)), _full((2, 1, D))] + ([] if gate is None else [row, _full((2, 1, D))]),
        out_shape=[jax.ShapeDtypeStruct((T, D), F32), jax.ShapeDtypeStruct((1, D), F32),
                   jax.ShapeDtypeStruct((2, 1, D), F32), jax.ShapeDtypeStruct((2, 1, D), F32)]
        + ([] if gate is None else [jax.ShapeDtypeStruct((T, D), BF16), jax.ShapeDtypeStruct((2, 1, D), F32)]),
        compiler_params=_cp("arbitrary"))(x, dh, dres, g, mod, *extra_in)


def _gate_bwd(dx, f, mod, i_gate, n_lat):
    T, D = dx.shape
    tm = _tile(T, 528)

    def body(dx_ref, f_ref, mod_ref, dy_ref, dg_ref):
        i = pl.program_id(0)

        @pl.when(i == 0)
        def _():
            dg_ref[...] = jnp.zeros_like(dg_ref)

        def fn(r0, r1, seg):
            dxv = dx_ref[r0:r1, :]
            dy_ref[r0:r1, :] = (dxv * mod_ref[seg, i_gate:i_gate + 1, :]).astype(BF16)
            dg_ref[seg] += jnp.sum(dxv * f_ref[r0:r1, :].astype(F32), axis=0, keepdims=True)

        _segments(i, tm, n_lat, fn)

    row = pl.BlockSpec((tm, D), lambda i: (i, 0))
    return pl.pallas_call(
        body, grid=(T // tm,), name="gate_bwd",
        in_specs=[row, row, _full(mod.shape)], out_specs=[row, _full((2, 1, D))],
        out_shape=[jax.ShapeDtypeStruct((T, D), BF16), jax.ShapeDtypeStruct((2, 1, D), F32)],
        compiler_params=_cp("arbitrary"))(dx, f, mod)


def _mm_nn(a, w, out_dtype, name):
    M, K = a.shape
    J, _, n = w.shape
    tm = _tile(M, 1056)

    def body(a_ref, w_ref, o_ref):
        o_ref[...] = _dot(a_ref[...], w_ref[...]).astype(o_ref.dtype)

    return pl.pallas_call(
        body, grid=(M // tm, J), name=name,
        in_specs=[pl.BlockSpec((tm, K), lambda i, j: (i, 0)), pl.BlockSpec((None, K, n), lambda i, j: (j, 0, 0))],
        out_specs=pl.BlockSpec((tm, n), lambda i, j: (i, j)),
        out_shape=jax.ShapeDtypeStruct((M, J * n), out_dtype), compiler_params=_cp("parallel", "arbitrary"))(a, w)


def _mm_res(a3, w, res, mod, i_gate, n_lat, name):
    J, M, k = a3.shape
    N = w.shape[2]
    tm = _tile(M, 528)

    def body(a_ref, w_ref, res_ref, mod_ref, x_ref, f_ref):
        acc = _dot(a_ref[0], w_ref[0])
        for j in range(1, J):
            acc += _dot(a_ref[j], w_ref[j])
        f_ref[...] = acc.astype(BF16)

        def fn(r0, r1, seg):
            x_ref[r0:r1, :] = res_ref[r0:r1, :] + mod_ref[seg, i_gate:i_gate + 1, :] * acc[r0:r1, :]

        _segments(pl.program_id(0), tm, n_lat, fn)

    row = pl.BlockSpec((tm, N), lambda i: (i, 0))
    return pl.pallas_call(
        body, grid=(M // tm,), name=name,
        in_specs=[pl.BlockSpec((J, tm, k), lambda i: (0, i, 0)), _full(w.shape), row, _full(mod.shape)],
        out_specs=[row, row],
        out_shape=[jax.ShapeDtypeStruct((M, N), F32), jax.ShapeDtypeStruct((M, N), BF16)],
        compiler_params=_cp("parallel"))(a3, w, res, mod)


def _mm_nt_acc(dys, ws, row_major, name):
    J, K, n = ws[0].shape
    M = dys[0].shape[0] if row_major else dys[0].shape[1]
    tm = _tile(M, 1056)
    P = len(dys)

    def body(*refs):
        o_ref = refs[2 * P]
        j = pl.program_id(1)
        part = _dg(refs[0][...], refs[P][...], NT)
        for p in range(1, P):
            part += _dg(refs[p][...], refs[P + p][...], NT)

        @pl.when(j == 0)
        def _():
            o_ref[...] = part

        @pl.when(j > 0)
        def _():
            o_ref[...] += part

    dy_spec = (pl.BlockSpec((tm, n), lambda i, j: (i, j)) if row_major
               else pl.BlockSpec((None, tm, n), lambda i, j: (j, i, 0)))
    w_spec = pl.BlockSpec((None, K, n), lambda i, j: (j, 0, 0))
    return pl.pallas_call(
        body, grid=(M // tm, J), name=name,
        in_specs=[dy_spec] * P + [w_spec] * P,
        out_specs=pl.BlockSpec((tm, K), lambda i, j: (i, 0)),
        out_shape=jax.ShapeDtypeStruct((M, K), F32), compiler_params=_cp("parallel", "arbitrary"))(*dys, *ws)


def _mm_tn(x, dy, x_spec, dy_spec, J, K, n, T, name):
    tk = _tile(T, 1056, MXU_DEPTH_V7X)
    nt = T // tk

    def body(x_ref, dy_ref, o_ref, acc):
        t = pl.program_id(1)
        part = _dg(x_ref[...], dy_ref[...], TN)

        @pl.when(t == 0)
        def _():
            acc[...] = part

        @pl.when(t > 0)
        def _():
            acc[...] += part

        @pl.when(t == nt - 1)
        def _():
            o_ref[...] = acc[...].astype(BF16)

    return pl.pallas_call(
        body, grid=(J, nt), name=name,
        in_specs=[x_spec(tk), dy_spec(tk)],
        out_specs=pl.BlockSpec((None, K, n), lambda j, t: (j, 0, 0)),
        out_shape=jax.ShapeDtypeStruct((J, K, n), BF16), scratch_shapes=[pltpu.VMEM((K, n), F32)],
        compiler_params=_cp("parallel", "arbitrary"))(x, dy)


def _mm_dw(xt, dy, dy_spec, J, n, name):
    K, T = xt.shape
    tk = _tile(T, 1056, MXU_DEPTH_V7X)
    nt = T // tk

    def body(xt_ref, dy_ref, o_ref, acc):
        t = pl.program_id(1)
        part = _dot(xt_ref[...], dy_ref[...])

        @pl.when(t == 0)
        def _():
            acc[...] = part

        @pl.when(t > 0)
        def _():
            acc[...] += part

        @pl.when(t == nt - 1)
        def _():
            o_ref[...] = acc[...].astype(BF16)

    return pl.pallas_call(
        body, grid=(J, nt), name=name,
        in_specs=[pl.BlockSpec((K, tk), lambda j, t: (0, t)), dy_spec(tk)],
        out_specs=pl.BlockSpec((None, K, n), lambda j, t: (j, 0, 0)),
        out_shape=jax.ShapeDtypeStruct((J, K, n), BF16), scratch_shapes=[pltpu.VMEM((K, n), F32)],
        compiler_params=_cp("parallel", "arbitrary"))(xt, dy)


def _rows(width):
    return lambda tk: pl.BlockSpec((tk, width), lambda j, t: (t, 0))


def _row_cols(width):
    return lambda tk: pl.BlockSpec((tk, width), lambda j, t: (t, j))


def _shard_rows(width):
    return lambda tk: pl.BlockSpec((None, tk, width), lambda j, t: (j, t, 0))


def _ffn_up(h, w1, w3):
    T, D = h.shape
    J, _, n = w1.shape
    tm = _tile(T, 1056)

    def body(h_ref, w1_ref, w3_ref, a1_ref, a3_ref, act_ref):
        hv = h_ref[...]
        a1 = _dot(hv, w1_ref[...])
        a3 = _dot(hv, w3_ref[...])
        a1_ref[...] = a1.astype(BF16)
        a3_ref[...] = a3.astype(BF16)
        act_ref[...] = (a1 * _sigmoid(a1) * a3).astype(BF16)

    w_spec = pl.BlockSpec((None, D, n), lambda i, j: (j, 0, 0))
    o_spec = pl.BlockSpec((None, tm, n), lambda i, j: (j, i, 0))
    return pl.pallas_call(
        body, grid=(T // tm, J), name="ffn_up",
        in_specs=[pl.BlockSpec((tm, D), lambda i, j: (i, 0)), w_spec, w_spec], out_specs=[o_spec] * 3,
        out_shape=[jax.ShapeDtypeStruct((J, T, n), BF16)] * 3,
        compiler_params=_cp("parallel", "arbitrary"))(h, w1, w3)


def _ffn_down_bwd(dy, w2, a1, a3):
    T, D = dy.shape
    J, n, _ = w2.shape
    tm = _tile(T, 1056)

    def body(dy_ref, w2_ref, a1_ref, a3_ref, da1_ref, da3_ref):
        dact = _dg(dy_ref[...], w2_ref[...], NT)
        a1v = a1_ref[...].astype(F32)
        sig = _sigmoid(a1v)
        da3_ref[...] = (dact * a1v * sig).astype(BF16)
        da1_ref[...] = (dact * a3_ref[...].astype(F32) * (sig * (1.0 + a1v * (1.0 - sig)))).astype(BF16)

    a_spec = pl.BlockSpec((None, tm, n), lambda i, j: (j, i, 0))
    return pl.pallas_call(
        body, grid=(T // tm, J), name="ffn_down_bwd",
        in_specs=[pl.BlockSpec((tm, D), lambda i, j: (i, 0)), pl.BlockSpec((None, n, D), lambda i, j: (j, 0, 0)),
                  a_spec, a_spec],
        out_specs=[a_spec, a_spec], out_shape=[jax.ShapeDtypeStruct((J, T, n), BF16)] * 2,
        compiler_params=_cp("parallel", "arbitrary"))(dy, w2, a1, a3)


def _qkv_prep(p, cos, sin, qg, kg, e):
    T = p.shape[0]
    tm = _tile(T, 528)

    def body(p_ref, cos_ref, sin_ref, qg_ref, kg_ref, e_ref, q_ref, k_ref, v_ref):
        ev = e_ref[...]
        cv, sv = cos_ref[...], sin_ref[...]
        xq = p_ref[:, 0:512]
        qn = xq * lax.rsqrt(_seg_mean(xq * xq, ev) + EPS) * qg_ref[...]
        qr = _rope(qn, cv, sv, False) * (HEAD_DIM ** -0.5 * LOG2E)
        xk = p_ref[:, 512:640]
        kn = xk * lax.rsqrt(_seg_mean(xk * xk, ev) + EPS) * kg_ref[...]
        kr = _rope(kn, cv, sv, False)
        lane = _lane((tm, LANES))
        ones = jnp.where(lane < 64 + AUG, -1.0, 0.0)
        for h, blk in enumerate(_heads_to_rows(qr, N_Q_HEADS)):
            q_ref[h] = jnp.where(lane < 64, blk, 0.0).astype(BF16)
        for h, blk in enumerate(_heads_to_rows(kr, N_KV_HEADS)):
            k_ref[h] = jnp.where(lane < 64, blk, ones).astype(BF16)
        for h, blk in enumerate(_heads_to_rows(p_ref[:, 640:768], N_KV_HEADS)):
            v_ref[h] = jnp.where(lane < 64, blk, ones).astype(BF16)

    tab = pl.BlockSpec((tm, LANES), lambda i: (i, 0))
    return pl.pallas_call(
        body, grid=(T // tm,), name="qkv_prep",
        in_specs=[pl.BlockSpec((tm, QKV_W), lambda i: (i, 1)), tab, tab, _full(qg.shape), _full(kg.shape),
                  _full(e.shape)],
        out_specs=[pl.BlockSpec((N_Q_HEADS, tm, LANES), lambda i: (0, i, 0)),
                   pl.BlockSpec((N_KV_HEADS, tm, LANES), lambda i: (0, i, 0)),
                   pl.BlockSpec((N_KV_HEADS, tm, LANES), lambda i: (0, i, 0))],
        out_shape=[jax.ShapeDtypeStruct((N_Q_HEADS, T, LANES), BF16),
                   jax.ShapeDtypeStruct((N_KV_HEADS, T, LANES), BF16),
                   jax.ShapeDtypeStruct((N_KV_HEADS, T, LANES), BF16)],
        compiler_params=_cp("parallel"))(p, cos, sin, qg, kg, e)


def _qkv_prep_bwd(dp, dq, dk, dv, p, cos, sin, qg, kg, e, fold):
    T = p.shape[0]
    tq = dq.shape[3] // GROUP
    tm = _tile(T, 768, tq)
    nt = T // tm

    def body(dp_in, dq_ref, dk_ref, dv_ref, p_ref, cos_ref, sin_ref, qg_ref, kg_ref, e_ref, fold_ref,
             dp_ref, dqg_ref, dkg_ref, accq, acck):
        del dp_in
        i = pl.program_id(0)

        @pl.when(i == 0)
        def _():
            accq[...] = jnp.zeros_like(accq)
            acck[...] = jnp.zeros_like(acck)

        ev = e_ref[...]
        cv, sv = cos_ref[...], sin_ref[...]

        def one(x, dr, gain, acc):
            r = lax.rsqrt(_seg_mean(x * x, ev) + EPS)
            xh = x * r
            dn = _rope(dr, cv, sv, True)
            acc[0:1, :] += jnp.sum(dn * xh, axis=0, keepdims=True)
            gd = gain * dn
            return r * (gd - xh * _seg_mean(xh * gd, ev))

        slabs = [[dq_ref[h, b].T for b in range(tm // tq)] for h in range(N_KV_HEADS)]
        heads = [jnp.concatenate([sl[g * tq:(g + 1) * tq] for sl in slabs[h]], axis=0)
                 for h in range(N_KV_HEADS) for g in range(GROUP)]
        dqr = _rows_to_heads(heads) * (HEAD_DIM ** -0.5)
        dkr = _rows_to_heads([dk_ref[h] for h in range(N_KV_HEADS)]) * (1.0 / LOG2E)
        dvv = _rows_to_heads([dv_ref[h] for h in range(N_KV_HEADS)])
        dp_ref[:, 0:512] = one(p_ref[:, 0:512], dqr, qg_ref[...], accq).astype(BF16)
        dp_ref[:, 512:640] = one(p_ref[:, 512:640], dkr, kg_ref[...], acck).astype(BF16)
        dp_ref[:, 640:768] = dvv.astype(BF16)

        @pl.when(i == nt - 1)
        def _():
            fv = fold_ref[...]
            dqg_ref[...] = jnp.dot(accq[...], fv, preferred_element_type=F32, precision=lax.Precision.HIGHEST)
            dkg_ref[...] = jnp.dot(acck[...], fv[0:LANES, :], preferred_element_type=F32,
                                   precision=lax.Precision.HIGHEST)

    tab = pl.BlockSpec((tm, LANES), lambda i: (i, 0))
    sec = pl.BlockSpec((tm, QKV_W), lambda i: (i, 1))
    return pl.pallas_call(
        body, grid=(nt,), name="qkv_prep_bwd",
        in_specs=[ANY, pl.BlockSpec((N_KV_HEADS, tm // tq, LANES, GROUP * tq), lambda i: (0, i, 0, 0)),
                  pl.BlockSpec((N_KV_HEADS, tm, LANES), lambda i: (0, i, 0)),
                  pl.BlockSpec((N_KV_HEADS, tm, LANES), lambda i: (0, i, 0)),
                  sec, tab, tab, _full(qg.shape), _full(kg.shape), _full(e.shape), _full(fold.shape)],
        out_specs=[sec, _full((8, LANES)), _full((8, LANES))],
        out_shape=[jax.ShapeDtypeStruct(dp.shape, BF16), jax.ShapeDtypeStruct((8, LANES), F32),
                   jax.ShapeDtypeStruct((8, LANES), F32)],
        scratch_shapes=[pltpu.VMEM((8, 512), F32), pltpu.VMEM((8, LANES), F32)],
        input_output_aliases={0: 0}, compiler_params=_cp("arbitrary"))(dp, dq, dk, dv, p, cos, sin, qg, kg, e, fold)


def _flash_fwd(q, k, v, n_lat, gather=()):
    _, _, T, _ = q.shape
    tq = tk = 256
    nq = T // tq
    M = GROUP * tq

    wide_k = FWD_KEYS if n_lat % FWD_KEYS == 0 else tk
    n_g = len(gather)

    def body(q_ref, k_ref, v_ref, *rest):
        o_ref, qa_ref = rest[n_g], rest[n_g + 1]
        i = pl.program_id(0)
        if n_g:
            bufs = rest[n_g + 2:2 * n_g + 2]
            start, wait = _plane_exchange(bufs, bufs, rest[-2], rest[-1], False)
            pl.when((i == 0) & (pl.program_id(1) == 0))(start)
        qv = q_ref[...].reshape(M, LANES)

        def step(r0, width, carry):
            m, acc = carry
            sc = _dg(qv, k_ref[pl.ds(r0, width), :], NT)
            m_new = jnp.maximum(m, jnp.max(sc, axis=1, keepdims=True))
            pr = jnp.exp2(sc - m_new)
            acc = jnp.exp2(m - m_new) * acc + _dot(pr.astype(BF16), v_ref[pl.ds(r0, width), :])
            return m_new, acc

        def wide(s, carry):
            return step(s * wide_k if isinstance(s, int) else pl.multiple_of(s * wide_k, wide_k), wide_k, carry)

        def finish(m, acc):
            den = -acc[:, 64:65]
            out = acc / den
            o_ref[...] = _rows_to_heads([out[g * tq:(g + 1) * tq] for g in range(GROUP)]).astype(BF16)
            qa_ref[...] = _aug(qv.astype(F32), m + jnp.log2(den)).astype(BF16).reshape(GROUP, tq, LANES)

        init = (jnp.full((M, 1), -1e30, F32), jnp.zeros((M, LANES), F32))

        @pl.when(i < n_lat // tq)
        def _():
            carry = _loop_unrolled(n_lat // wide_k, wide, init, UNROLL_FWD)
            for r0 in range(n_lat, T, tk):
                carry = step(r0, tk, carry)
            finish(*carry)

        @pl.when(i >= n_lat // tq)
        def _():
            carry = init
            for r0 in range(n_lat, T, tk):
                carry = step(r0, tk, carry)
            finish(*carry)

        if n_g:
            pl.when((i == nq - 1) & (pl.program_id(1) == N_KV_HEADS - 1))(wait)

    q_spec = pl.BlockSpec((None, GROUP, tq, LANES), lambda i, h: (h, 0, i, 0))
    kv_spec = pl.BlockSpec((None, T, LANES), lambda i, h: (h, 0, 0))
    sems = [pltpu.SemaphoreType.DMA((3 * n_g,))] * 2 if n_g else []
    return pl.pallas_call(
        body, grid=(nq, N_KV_HEADS), name="flash_fwd_gather" if n_g else "flash_fwd",
        in_specs=[q_spec, kv_spec, kv_spec] + [ANY] * n_g,
        out_specs=[pl.BlockSpec((tq, GROUP * HEAD_DIM), lambda i, h: (i, h)), q_spec] + [ANY] * n_g,
        out_shape=[jax.ShapeDtypeStruct((T, N_Q_HEADS * HEAD_DIM), BF16), jax.ShapeDtypeStruct(q.shape, BF16)]
        + [jax.ShapeDtypeStruct(b.shape, b.dtype) for b in gather],
        input_output_aliases={3 + a: 2 + a for a in range(n_g)}, scratch_shapes=sems,
        compiler_params=_cp("arbitrary", "arbitrary"))(q, k, v, *gather)


def _flash_bwd(qa, doa, k, v, n_lat, scatter=()):
    _, _, T, _ = qa.shape
    tq = tk = 256
    nkv = T // tk
    M = GROUP * tq

    n_s = len(scatter)

    def body(qa_hbm, doa_hbm, k_ref, v_ref, *rest):
        dq_hbm, dk_ref, dv_ref = rest[n_s:n_s + 3]
        q_sc, do_sc, dq_sc, sems = rest[2 * n_s + 3:2 * n_s + 7]
        h = pl.program_id(0)
        j = pl.program_id(1)
        if n_s:
            start, wait = _plane_exchange(rest[:n_s], rest[n_s + 3:2 * n_s + 3], rest[-2], rest[-1], True)
            pl.when((h == 0) & (j == 0))(start)

        @pl.when(j == 0)
        def _():
            c1 = pltpu.make_async_copy(qa_hbm.at[h], q_sc, sems.at[0])
            c2 = pltpu.make_async_copy(doa_hbm.at[h], do_sc, sems.at[1])
            c1.start()
            c2.start()
            dq_sc[...] = jnp.zeros_like(dq_sc)
            c1.wait()
            c2.wait()

        kb = k_ref[...]
        vb = v_ref[...]
        kbt = kb.astype(F32).T.astype(BF16)

        def step(i, carry):
            dk, dv = carry
            r0 = i * tq if isinstance(i, int) else pl.multiple_of(i * tq, tq)
            qv = q_sc[:, pl.ds(r0, tq), :].reshape(M, LANES)
            dov = do_sc[:, pl.ds(r0, tq), :].reshape(M, LANES)
            pr = jnp.exp2(_dg(kb, qv, NT))
            ds = (pr * _dg(vb, dov, NT)).astype(BF16)
            dv = dv + _dot(pr.astype(BF16), dov)
            dk = dk + _dot(ds, qv)
            dq_sc[i] += _dot(kbt, ds)
            return dk, dv

        z = jnp.zeros((tk, LANES), F32)
        carry = _loop_unrolled(n_lat // tq, step, (z, z), UNROLL_BWD)
        dk_ref[...] = carry[0]
        dv_ref[...] = carry[1]

        @pl.when(j >= n_lat // tk)
        def _():
            c = (dk_ref[...], dv_ref[...])
            for i in range(n_lat // tq, T // tq):
                c = step(i, c)
            dk_ref[...] = c[0]
            dv_ref[...] = c[1]

        @pl.when(j == nkv - 1)
        def _():
            c3 = pltpu.make_async_copy(dq_sc, dq_hbm.at[h], sems.at[2])
            c3.start()
            c3.wait()

        if n_s:
            pl.when((h == N_KV_HEADS - 1) & (j == nkv - 1))(wait)

    kv_spec = pl.BlockSpec((None, tk, LANES), lambda h, j: (h, j, 0))
    return pl.pallas_call(
        body, grid=(N_KV_HEADS, nkv), name="flash_bwd_scatter" if n_s else "flash_bwd",
        in_specs=[ANY, ANY, kv_spec, kv_spec] + [ANY] * n_s, out_specs=[ANY, kv_spec, kv_spec] + [ANY] * n_s,
        out_shape=[jax.ShapeDtypeStruct((N_KV_HEADS, T // tq, LANES, M), F32), jax.ShapeDtypeStruct(k.shape, F32),
                   jax.ShapeDtypeStruct(k.shape, F32)] + [jax.ShapeDtypeStruct(g.shape, g.dtype) for g in scatter],
        scratch_shapes=[pltpu.VMEM((GROUP, T, LANES), BF16), pltpu.VMEM((GROUP, T, LANES), BF16),
                        pltpu.VMEM((T // tq, LANES, M), F32), pltpu.SemaphoreType.DMA((3,))]
        + ([pltpu.SemaphoreType.DMA((3 * n_s,))] * 2 if n_s else []),
        compiler_params=_cp("arbitrary", "arbitrary"))(qa, doa, k, v, *scatter)


def _conv_masks(i, tm, n_lat, T):
    row = lax.broadcasted_iota(jnp.int32, (tm, 1), 0)
    g = row + i * tm
    return row, (g == 0) | (g == n_lat), (g == n_lat - 1) | (g == T - 1)


def _shift_rows(v, prev_row, next_row, row, first, last):
    tm = v.shape[0]
    down = jnp.where(row == 0, prev_row, pltpu.roll(v, 1, 0))
    up = jnp.where(row == tm - 1, next_row, pltpu.roll(v, tm - 1, 0))
    return jnp.where(first, 0.0, down), jnp.where(last, 0.0, up)


def _halo_specs(tm, T, width, col):
    nb = T // 8
    return (pl.BlockSpec((8, width), lambda i: (jnp.maximum(i * (tm // 8) - 1, 0), col)),
            pl.BlockSpec((8, width), lambda i: (jnp.minimum((i + 1) * (tm // 8), nb - 1), col)))


def _conv_fwd(p, cw, n_lat):
    T = p.shape[0]
    tm = _tile(T, 1056)

    def body(p_ref, pp_ref, pn_ref, cw_ref, o_ref):
        row, first, last = _conv_masks(pl.program_id(0), tm, n_lat, T)
        z = p_ref[:, 256:512] * p_ref[:, 512:768]
        zp = pp_ref[7:8, 256:512] * pp_ref[7:8, 512:768]
        zn = pn_ref[0:1, 256:512] * pn_ref[0:1, 512:768]
        zd, zu = _shift_rows(z, zp, zn, row, first, last)
        conv = cw_ref[0:1, :] * zd + cw_ref[1:2, :] * z + cw_ref[2:3, :] * zu
        o_ref[...] = (p_ref[:, 0:256] * conv).astype(BF16)

    prev, nxt = _halo_specs(tm, T, 768, 0)
    return pl.pallas_call(
        body, grid=(T // tm,), name="conv_fwd",
        in_specs=[pl.BlockSpec((tm, 768), lambda i: (i, 0)), prev, nxt, _full(cw.shape)],
        out_specs=pl.BlockSpec((tm, CONV_W), lambda i: (i, 0)),
        out_shape=jax.ShapeDtypeStruct((T, CONV_W), BF16), compiler_params=_cp("parallel"))(p, p, p, cw)


def _conv_bwd(dp, dy, p, cw, n_lat):
    T = p.shape[0]
    tm = _tile(T, 1056)

    def body(dp_in, dy_ref, dyp_ref, dyn_ref, p_ref, pp_ref, pn_ref, cw_ref, dp_ref, dcw_ref):
        del dp_in
        i = pl.program_id(0)

        @pl.when(i == 0)
        def _():
            dcw_ref[...] = jnp.zeros_like(dcw_ref)

        row, first, last = _conv_masks(i, tm, n_lat, T)
        ab, ac, ax = p_ref[:, 0:256], p_ref[:, 256:512], p_ref[:, 512:768]
        z = ac * ax
        zp = pp_ref[7:8, 256:512] * pp_ref[7:8, 512:768]
        zn = pn_ref[0:1, 256:512] * pn_ref[0:1, 512:768]
        zd, zu = _shift_rows(z, zp, zn, row, first, last)
        w0, w1, w2 = cw_ref[0:1, :], cw_ref[1:2, :], cw_ref[2:3, :]
        dy = dy_ref[...]
        dc = dy * ab
        dcd, dcu = _shift_rows(dc, dyp_ref[7:8, :] * pp_ref[7:8, 0:256], dyn_ref[0:1, :] * pn_ref[0:1, 0:256],
                               row, first, last)
        dz = w0 * dcu + w1 * dc + w2 * dcd
        dp_ref[:, 0:256] = (dy * (w0 * zd + w1 * z + w2 * zu)).astype(BF16)
        dp_ref[:, 256:512] = (dz * ax).astype(BF16)
        dp_ref[:, 512:768] = (dz * ac).astype(BF16)
        dcw_ref[0:1, :] += jnp.sum(dc * zd, axis=0, keepdims=True)
        dcw_ref[1:2, :] += jnp.sum(dc * z, axis=0, keepdims=True)
        dcw_ref[2:3, :] += jnp.sum(dc * zu, axis=0, keepdims=True)

    prev, nxt = _halo_specs(tm, T, 768, 0)
    dprev, dnxt = _halo_specs(tm, T, CONV_W, 0)
    sec = pl.BlockSpec((tm, 768), lambda i: (i, 0))
    return pl.pallas_call(
        body, grid=(T // tm,), name="conv_bwd",
        in_specs=[ANY, pl.BlockSpec((tm, CONV_W), lambda i: (i, 0)), dprev, dnxt, sec, prev, nxt, _full(cw.shape)],
        out_specs=[sec, _full((8, CONV_W))],
        out_shape=[jax.ShapeDtypeStruct(dp.shape, BF16), jax.ShapeDtypeStruct((8, CONV_W), F32)],
        input_output_aliases={0: 0}, compiler_params=_cp("arbitrary"))(dp, dy, dy, dy, p, p, p, cw)


def _gmlp_mix(bd_ref, vs, grp):
    out = jnp.zeros((2 * CHUNK, SG_W), F32)
    for g in range(4):
        out = jnp.where(grp == g, _dot(bd_ref[g], vs), out)
    return out


def _gmlp_fwd(p, sgn, bd, bias):
    T = p.shape[0]
    tm = _tile(T, 768, 2 * CHUNK)

    def body(p_ref, sgn_ref, bd_ref, bias_ref, o_ref):
        x = _gelu(p_ref[:, 256:512])
        vn = (x * lax.rsqrt(jnp.mean(x * x, axis=-1, keepdims=True) + EPS) * sgn_ref[...]).astype(BF16)
        grp = _lane((2 * CHUNK, SG_W)) // 64
        for s in range(tm // (2 * CHUNK)):
            rs = slice(s * 2 * CHUNK, (s + 1) * 2 * CHUNK)
            mixed = _gmlp_mix(bd_ref, vn[rs], grp) + bias_ref[...]
            o_ref[rs, :] = (_gelu(p_ref[rs, 0:256]) * mixed).astype(BF16)

    return pl.pallas_call(
        body, grid=(T // tm,), name="gmlp_fwd",
        in_specs=[pl.BlockSpec((tm, 2 * SG_W), lambda i: (i, 3)), _full(sgn.shape), _full(bd.shape),
                  _full(bias.shape)],
        out_specs=pl.BlockSpec((tm, SG_W), lambda i: (i, 0)),
        out_shape=jax.ShapeDtypeStruct((T, SG_W), BF16), compiler_params=_cp("parallel"))(p, sgn, bd, bias)


def _gmlp_bwd(dp, dy, p, sgn, bd, bdt, bias, gsum):
    T = p.shape[0]
    tm = _tile(T, 768, 2 * CHUNK)
    nt = T // tm
    C2 = 2 * CHUNK

    def body(dp_in, dy_ref, p_ref, sgn_ref, bd_ref, bdt_ref, bias_ref, gsum_ref,
             dp_ref, dsg_ref, dws_ref, dbs_ref, acc_w, acc_b):
        del dp_in
        i = pl.program_id(0)

        @pl.when(i == 0)
        def _():
            dsg_ref[...] = jnp.zeros_like(dsg_ref)
            acc_w[...] = jnp.zeros_like(acc_w)
            acc_b[...] = jnp.zeros_like(acc_b)

        u = p_ref[:, 0:256]
        sv = p_ref[:, 256:512]
        ug = _gelu(u)
        x = _gelu(sv)
        r = lax.rsqrt(jnp.mean(x * x, axis=-1, keepdims=True) + EPS)
        xh = x * r
        sg = sgn_ref[...]
        vn = (xh * sg).astype(BF16)
        grp = _lane((C2, SG_W)) // 64
        dug, dvn = [], []
        for s in range(tm // C2):
            rs = slice(s * C2, (s + 1) * C2)
            vs = vn[rs]
            dys = dy_ref[rs, :]
            dug.append(dys * (_gmlp_mix(bd_ref, vs, grp) + bias_ref[...]))
            dmix = dys * ug[rs]
            acc_b[...] += dmix
            dmb = dmix.astype(BF16)
            dvn.append(_gmlp_mix(bdt_ref, dmb, grp))
            for g in range(4):
                acc_w[g] += _dg(jnp.where(grp == g, dmb, jnp.zeros_like(dmb)), vs, NT)
        dug = jnp.concatenate(dug, axis=0)
        dvn = jnp.concatenate(dvn, axis=0)
        dsg_ref[...] += jnp.sum(dvn * xh, axis=0, keepdims=True)
        gd = sg * dvn
        dx = r * (gd - xh * jnp.mean(xh * gd, axis=-1, keepdims=True))
        dp_ref[:, 0:256] = (dug * _gelu_grad(u)).astype(BF16)
        dp_ref[:, 256:512] = (dx * _gelu_grad(sv)).astype(BF16)

        @pl.when(i == nt - 1)
        def _():
            for g in range(4):
                dws_ref[g] = acc_w[g, 0:CHUNK, 0:CHUNK] + acc_w[g, CHUNK:C2, CHUNK:C2]
            dbs_ref[...] = jnp.dot(acc_b[0:CHUNK, :] + acc_b[CHUNK:C2, :], gsum_ref[...],
                                   preferred_element_type=F32, precision=lax.Precision.HIGHEST)

    sec = pl.BlockSpec((tm, 2 * SG_W), lambda i: (i, 3))
    return pl.pallas_call(
        body, grid=(nt,), name="gmlp_bwd",
        in_specs=[ANY, pl.BlockSpec((tm, SG_W), lambda i: (i, 0)), sec, _full(sgn.shape), _full(bd.shape),
                  _full(bdt.shape), _full(bias.shape), _full(gsum.shape)],
        out_specs=[sec, _full((1, SG_W)), _full((4, CHUNK, CHUNK)), _full((CHUNK, LANES))],
        out_shape=[jax.ShapeDtypeStruct(dp.shape, BF16), jax.ShapeDtypeStruct((1, SG_W), F32),
                   jax.ShapeDtypeStruct((4, CHUNK, CHUNK), F32), jax.ShapeDtypeStruct((CHUNK, LANES), F32)],
        scratch_shapes=[pltpu.VMEM((4, C2, C2), F32), pltpu.VMEM((C2, SG_W), F32)],
        input_output_aliases={0: 0}, compiler_params=_cp("arbitrary"))(dp, dy, p, sgn, bd, bdt, bias, gsum)


def _merge_fwd(ya, at, yc, p, wa, wb, wc):
    T = p.shape[0]
    tm = _tile(T, 528)
    n = wa.shape[2]

    def body(ya_ref, at_ref, yc_ref, ga_ref, gb_ref, gc_ref, wa_ref, wb_ref, wc_ref, o_ref):
        yav, atv, ycv = ya_ref[...], at_ref[...], yc_ref[...]
        for j in range(N_CHIPS):
            cs = slice(j * n, (j + 1) * n)
            m = (_sigmoid(ga_ref[:, cs]) * _dot(yav, wa_ref[j]) + _sigmoid(gb_ref[:, cs]) * _dot(atv, wb_ref[j])
                 + _sigmoid(gc_ref[:, cs]) * _dot(ycv, wc_ref[j]))
            o_ref[:, cs] = m.astype(BF16)

    def rows(w, col=0):
        return pl.BlockSpec((tm, w), lambda i: (i, col))

    return pl.pallas_call(
        body, grid=(T // tm,), name="merge_fwd",
        in_specs=[rows(CONV_W), rows(512), rows(SG_W), rows(D_MODEL, 2), rows(D_MODEL, 3), rows(D_MODEL, 4),
                  _full(wa.shape), _full(wb.shape), _full(wc.shape)],
        out_specs=rows(D_MODEL), out_shape=jax.ShapeDtypeStruct((T, D_MODEL), BF16),
        compiler_params=_cp("parallel"))(ya, at, yc, p, p, p, wa, wb, wc)


def _merge_bwd(dyo, ya, at, yc, p, wa, wb, wc, wo):
    T = p.shape[0]
    tm = _tile(T, 528)
    n = wa.shape[2]

    def body(dyo_ref, ya_ref, at_ref, yc_ref, ga_ref, gb_ref, gc_ref, wa_ref, wb_ref, wc_ref, wo_ref,
             dp_ref, dya_ref, doa_ref, dyc_ref, dwa_ref, dwb_ref, dwc_ref):
        i = pl.program_id(0)

        @pl.when(i == 0)
        def _():
            dwa_ref[...] = jnp.zeros_like(dwa_ref)
            dwb_ref[...] = jnp.zeros_like(dwb_ref)
            dwc_ref[...] = jnp.zeros_like(dwc_ref)

        dp_ref[:, 0:OFF_G] = jnp.zeros((tm, OFF_G), BF16)
        dm = _dg(dyo_ref[...], wo_ref[...], NT)
        yav, atv, ycv = ya_ref[...], at_ref[...], yc_ref[...]
        dya = jnp.zeros((tm, CONV_W), F32)
        dat = jnp.zeros((tm, 512), F32)
        dyc = jnp.zeros((tm, SG_W), F32)
        for j in range(N_CHIPS):
            cs = slice(j * n, (j + 1) * n)
            dmj = dm[:, cs]
            for y_in, w_ref, g_ref, dw_ref, which in (
                    (yav, wa_ref, ga_ref, dwa_ref, 0), (atv, wb_ref, gb_ref, dwb_ref, 1),
                    (ycv, wc_ref, gc_ref, dwc_ref, 2)):
                sg = _sigmoid(g_ref[:, cs])
                y = _dot(y_in, w_ref[j])
                c0 = OFF_G + which * D_MODEL + j * n
                dp_ref[:, c0:c0 + n] = (dmj * y * sg * (1.0 - sg)).astype(BF16)
                dyb = (dmj * sg).astype(BF16)
                dw_ref[j] += _dg(y_in, dyb, TN)
                back = _dg(dyb, w_ref[j], NT)
                if which == 0:
                    dya = dya + back
                elif which == 1:
                    dat = dat + back
                else:
                    dyc = dyc + back
        dya_ref[...] = dya
        dyc_ref[...] = dyc
        prod = dat * atv.astype(F32)
        lane = _lane((tm, LANES))
        dat_rows = _heads_to_rows(dat, N_Q_HEADS)
        for h in range(N_Q_HEADS):
            grp = prod[:, (h // 2) * LANES:(h // 2 + 1) * LANES]
            keep = (lane < 64) if h % 2 == 0 else (lane >= 64)
            delta = jnp.sum(jnp.where(keep, grp, 0.0), axis=1, keepdims=True)
            doa_ref[h] = _aug(dat_rows[h], delta).astype(BF16)

    def rows(w, col=0):
        return pl.BlockSpec((tm, w), lambda i: (i, col))

    return pl.pallas_call(
        body, grid=(T // tm,), name="merge_bwd",
        in_specs=[rows(D_MODEL), rows(CONV_W), rows(512), rows(SG_W), rows(D_MODEL, 2), rows(D_MODEL, 3),
                  rows(D_MODEL, 4), _full(wa.shape), _full(wb.shape), _full(wc.shape), _full(wo.shape)],
        out_specs=[rows(IN_W), rows(CONV_W),
                   pl.BlockSpec((N_Q_HEADS, tm, LANES), lambda i: (0, i, 0)), rows(SG_W),
                   _full(wa.shape), _full(wb.shape), _full(wc.shape)],
        out_shape=[jax.ShapeDtypeStruct((T, IN_W), BF16)] + [
            jax.ShapeDtypeStruct((T, CONV_W), F32), jax.ShapeDtypeStruct((N_Q_HEADS, T, LANES), BF16),
            jax.ShapeDtypeStruct((T, SG_W), F32), jax.ShapeDtypeStruct(wa.shape, F32),
            jax.ShapeDtypeStruct(wb.shape, F32), jax.ShapeDtypeStruct(wc.shape, F32)],
        compiler_params=_cp("arbitrary"))(dyo, ya, at, yc, p, p, p, wa, wb, wc, wo)


def _loss_grad(xf, tgt, n_lat):
    T, D = xf.shape
    tm = _tile(np.gcd(n_lat, T), 512)
    nl = n_lat // tm

    def body(x_ref, t_ref, dy_ref, l_ref):
        i = pl.program_id(0)

        @pl.when(i == 0)
        def _():
            l_ref[...] = jnp.zeros_like(l_ref)

        @pl.when(i < nl)
        def _():
            err = x_ref[...] - t_ref[...]
            dy_ref[...] = err * (1.0 / D)
            sq = jnp.sum(jnp.sum(err * err, axis=1, keepdims=True), axis=0, keepdims=True)
            l_ref[...] += (0.5 / D) * sq

        @pl.when(i >= nl)
        def _():
            dy_ref[...] = jnp.zeros_like(dy_ref)

    return pl.pallas_call(
        body, grid=(T // tm,), name="loss_grad",
        in_specs=[pl.BlockSpec((tm, D), lambda i: (i, 0)), pl.BlockSpec((tm, D), lambda i: (jnp.minimum(i, nl - 1), 0))],
        out_specs=[pl.BlockSpec((tm, D), lambda i: (i, 0)), _full((8, LANES))],
        out_shape=[jax.ShapeDtypeStruct((T, D), F32), jax.ShapeDtypeStruct((8, LANES), F32)],
        compiler_params=_cp("arbitrary"))(xf, tgt)


def _row_tile(R, C):
    if R * C <= (1 << 19) or R % 8:
        return R
    return _tile(R, max(8, (1 << 19) // C), 8)


def _adamw(w, m, v, g1, g2=None):
    shape = w.shape
    C = shape[-1]
    R = int(np.prod(shape[:-1])) if len(shape) > 1 else 1
    tr = _row_tile(R, C)
    ins = [a.reshape(R, C) for a in ((w, m, v, g1) if g2 is None else (w, m, v, g1, g2))]

    def body(*refs):
        w_ref, m_ref, v_ref = refs[0], refs[1], refs[2]
        g_ref, d_ref, m2_ref, v2_ref = refs[-4:]
        g = refs[3][...] if g2 is None else refs[3][...] + refs[4][...]
        m2 = ADAM_B1 * m_ref[...] + (1.0 - ADAM_B1) * g
        v2 = ADAM_B2 * v_ref[...] + (1.0 - ADAM_B2) * (g * g)
        m_hat = m2 / (1.0 - ADAM_B1 ** ADAM_STEP)
        v_hat = v2 / (1.0 - ADAM_B2 ** ADAM_STEP)
        g_ref[...] = g
        d_ref[...] = -ADAM_LR * (m_hat / (jnp.sqrt(v_hat) + ADAM_EPS) + ADAM_WD * w_ref[...])
        m2_ref[...] = m2
        v2_ref[...] = v2

    spec = pl.BlockSpec((tr, C), lambda i: (i, 0))
    outs = pl.pallas_call(
        body, grid=(R // tr,), name="adamw", in_specs=[spec] * len(ins), out_specs=[spec] * 4,
        out_shape=[jax.ShapeDtypeStruct((R, C), F32)] * 4, compiler_params=_cp("parallel"))(*ins)
    return [o.reshape(shape) for o in outs]


def _adamw_layer(w, m, v, gs, l, prev):
    L, a, b = w.shape
    tr = _row_tile(a, b)
    n_in = 3 + len(gs)

    def body(*refs):
        g_ref, d_ref, m2_ref, v2_ref = refs[-4:]
        g = refs[3][...]
        for r in refs[4:3 + len(gs)]:
            g = g + r[...]
        m2 = ADAM_B1 * refs[1][...] + (1.0 - ADAM_B1) * g
        v2 = ADAM_B2 * refs[2][...] + (1.0 - ADAM_B2) * (g * g)
        m_hat = m2 / (1.0 - ADAM_B1 ** ADAM_STEP)
        v_hat = v2 / (1.0 - ADAM_B2 ** ADAM_STEP)
        g_ref[...] = g
        d_ref[...] = -ADAM_LR * (m_hat / (jnp.sqrt(v_hat) + ADAM_EPS) + ADAM_WD * refs[0][...])
        m2_ref[...] = m2
        v2_ref[...] = v2

    layer = pl.BlockSpec((None, tr, b), lambda i: (l, i, 0))
    outs = pl.pallas_call(
        body, grid=(a // tr,), name="adamw_layer",
        in_specs=[layer] * 3 + [pl.BlockSpec((tr, b), lambda i: (i, 0))] * len(gs) + [ANY] * len(prev),
        out_specs=[layer] * 4, out_shape=[jax.ShapeDtypeStruct((L, a, b), F32)] * 4,
        input_output_aliases={n_in + k: k for k in range(len(prev))},
        compiler_params=_cp("parallel"))(w, m, v, *gs, *prev)
    return list(outs)


def _sum_lead(x, name):
    n, R, C = x.shape
    tr = _row_tile(R, C * n)

    def body(x_ref, o_ref):
        acc = x_ref[0].astype(F32)
        for s in range(1, n):
            acc = acc + x_ref[s].astype(F32)
        o_ref[...] = acc

    return pl.pallas_call(
        body, grid=(R // tr,), name=name, in_specs=[pl.BlockSpec((n, tr, C), lambda i: (0, i, 0))],
        out_specs=pl.BlockSpec((tr, C), lambda i: (i, 0)), out_shape=jax.ShapeDtypeStruct((R, C), F32),
        compiler_params=_cp("parallel"))(x)


def _silu(x):
    return x * _sigmoid(x)


def _mod_fwd(a_raw, w_mod, bsh):
    L, D, n = w_mod.shape

    def body(a_ref, w_ref, b_ref, o_ref):
        o_ref[...] = _dot(_silu(a_ref[...]).astype(BF16), w_ref[...].astype(BF16)) + b_ref[...]

    return pl.pallas_call(
        body, grid=(L,), name="mod_fwd",
        in_specs=[_full(a_raw.shape), pl.BlockSpec((None, D, n), lambda l: (l, 0, 0)),
                  pl.BlockSpec((None, 1, n), lambda l: (l, 0, 0))],
        out_specs=pl.BlockSpec((None, 16, n), lambda l: (l, 0, 0)),
        out_shape=jax.ShapeDtypeStruct((L, 16, n), F32), compiler_params=_cp("parallel"))(a_raw, w_mod, bsh)


def _wmod_grad(a_raw, dms):
    L, _, n = dms.shape
    D = a_raw.shape[1]

    def body(a_ref, dm_ref, o_ref):
        o_ref[...] = _dg(_silu(a_ref[...]).astype(BF16), dm_ref[...].astype(BF16), TN)

    return pl.pallas_call(
        body, grid=(L,), name="wmod_grad",
        in_specs=[_full(a_raw.shape), pl.BlockSpec((None, 16, n), lambda l: (l, 0, 0))],
        out_specs=pl.BlockSpec((None, D, n), lambda l: (l, 0, 0)),
        out_shape=jax.ShapeDtypeStruct((L, D, n), F32), compiler_params=_cp("parallel"))(a_raw, dms)


def _cctx_partial(dmc, w_mod):
    L, D, n = w_mod.shape

    def body(dm_ref, w_ref, o_ref):
        part = _dg(dm_ref[...].astype(BF16), w_ref[...].astype(BF16), NT)

        @pl.when(pl.program_id(0) == 0)
        def _():
            o_ref[...] = part

        @pl.when(pl.program_id(0) > 0)
        def _():
            o_ref[...] += part

    return pl.pallas_call(
        body, grid=(L,), name="cctx_partial",
        in_specs=[pl.BlockSpec((None, 16, n), lambda l: (l, 0, 0)), pl.BlockSpec((None, D, n), lambda l: (l, 0, 0))],
        out_specs=_full((16, D)), out_shape=jax.ShapeDtypeStruct((16, D), F32),
        compiler_params=_cp("arbitrary"))(dmc, w_mod)


def _cctx_final(parts, cc):
    def body(p_ref, c_ref, o_ref):
        s = p_ref[0, 0:8, :]
        for j in range(1, N_CHIPS):
            s = s + p_ref[2 * j, 0:8, :]
        xv = c_ref[...]
        sg = _sigmoid(xv)
        o_ref[...] = s * (sg * (1.0 + xv * (1.0 - sg)))

    return pl.pallas_call(
        body, name="cctx_final", in_specs=[_full(parts.shape), _full(cc.shape)], out_specs=_full((8, LANES)),
        out_shape=jax.ShapeDtypeStruct((8, LANES), F32), compiler_params=_cp())(parts, cc)


def _me():
    return lax.axis_index("x"), lax.axis_index("y"), lax.axis_index("c")


def _flip(v, bit):
    return 1 - v if bit else v


def _remote(src, dst, ssem, rsem, peer):
    return pltpu.make_async_remote_copy(src_ref=src, dst_ref=dst, send_sem=ssem, recv_sem=rsem,
                                        device_id=peer, device_id_type=MESH_ID)


def _ag8(xb, name):
    R = xb.shape[0]

    def pallas(x):
        def body(x_ref, o_ref, ssem, rsem):
            mx, my, mc = _me()
            me = 4 * mx + 2 * my + mc
            sib = (mx, my, 1 - mc)
            peers = _plane_peers(mx, my, mc)
            sends = [_remote(x_ref, o_ref.at[me], ssem.at[0], rsem.at[0], sib)]
            sends += [_remote(x_ref, o_ref.at[me], ssem.at[1 + k], rsem.at[1 + k], peer)
                      for k, (peer, _) in enumerate(peers)]
            for cp in sends:
                cp.start()
            for k, (peer, pj) in enumerate(peers):
                got = o_ref.at[2 * pj + mc]
                _remote(got, got, ssem.at[1 + k], rsem.at[1 + k], peer).wait_recv()
                fw = _remote(got, got, ssem.at[4 + k], rsem.at[4 + k], sib)
                fw.start()
                sends.append(fw)
            _remote(x_ref, o_ref.at[4 * mx + 2 * my + 1 - mc], ssem.at[0], rsem.at[0], sib).wait_recv()
            for k, (_, pj) in enumerate(peers):
                theirs = o_ref.at[2 * pj + 1 - mc]
                _remote(theirs, theirs, ssem.at[4 + k], rsem.at[4 + k], sib).wait_recv()
            for cp in sends:
                cp.wait_send()

        return pl.pallas_call(
            body, name=name, in_specs=[ANY], out_specs=ANY, out_shape=jax.ShapeDtypeStruct((N_DEV, R, LANES), F32),
            scratch_shapes=[pltpu.SemaphoreType.DMA((N_DEV - 1,)), pltpu.SemaphoreType.DMA((N_DEV - 1,))])(x)

    mx, my, mc = _me()
    return lax.dynamic_update_slice(pallas(xb), xb[None], (4 * mx + 2 * my + mc, 0, 0))


def _plane_peers(mx, my, mc):
    out = []
    for k in range(1, N_CHIPS):
        px, py = _flip(mx, k & 2), _flip(my, k & 1)
        out.append(((px, py, mc), 2 * px + py))
    return out


def _plane_exchange(ins, outs, ssem, rsem, scatter):
    n = len(ins)

    def desc(k, a, arriving):
        mx, my, mc = _me()
        j = 2 * mx + my
        peer, pj = _plane_peers(mx, my, mc)[k]
        src = ins[a].at[pj if scatter else j]
        dst = outs[a].at[pj if arriving else j]
        return _remote(src, dst, ssem.at[k * n + a], rsem.at[k * n + a], peer)

    def start():
        for k in range(N_CHIPS - 1):
            for a in range(n):
                desc(k, a, False).start()

    def wait():
        for k in range(N_CHIPS - 1):
            for a in range(n):
                desc(k, a, True).wait_recv()
        for k in range(N_CHIPS - 1):
            for a in range(n):
                desc(k, a, False).wait_send()

    return start, wait


def _chip_gather(bufs, name):
    n = len(bufs)
    halves = [b.shape[1] // 2 for b in bufs]

    def body(*refs):
        outs = refs[n:2 * n]
        ssem, rsem, fsem, gsem = refs[2 * n:]
        mx, my, mc = _me()
        j = 2 * mx + my
        sib = (mx, my, 1 - mc)

        def half(a, blk, c):
            return outs[a].at[blk, pl.ds(c * halves[a], halves[a]), :]

        peers = _plane_peers(mx, my, mc)
        sends = []
        for k, (peer, _) in enumerate(peers):
            for a in range(n):
                mine = half(a, j, mc)
                cp = _remote(mine, mine, ssem.at[k * n + a], rsem.at[k * n + a], peer)
                cp.start()
                sends.append(cp)
        for k, (peer, pj) in enumerate(peers):
            for a in range(n):
                got = half(a, pj, mc)
                _remote(got, got, ssem.at[k * n + a], rsem.at[k * n + a], peer).wait_recv()
                fw = _remote(got, got, fsem.at[k * n + a], gsem.at[k * n + a], sib)
                fw.start()
                sends.append(fw)
        for k, (_, pj) in enumerate(peers):
            for a in range(n):
                theirs = half(a, pj, 1 - mc)
                _remote(theirs, theirs, fsem.at[k * n + a], gsem.at[k * n + a], sib).wait_recv()
        for cp in sends:
            cp.wait_send()

    sems = pltpu.SemaphoreType.DMA((3 * n,))
    return pl.pallas_call(
        body, name=name, in_specs=[ANY] * n, out_specs=[ANY] * n,
        out_shape=[jax.ShapeDtypeStruct(b.shape, b.dtype) for b in bufs],
        input_output_aliases={a: a for a in range(n)},
        scratch_shapes=[sems, sems, sems, sems])(*bufs)


def _chip_scatter(gs, name):
    n = len(gs)

    def body(*refs):
        ins, outs = refs[:n], refs[n:2 * n]
        ssem, rsem = refs[2 * n:]
        mx, my, mc = _me()
        j = 2 * mx + my
        peers = _plane_peers(mx, my, mc)
        sends = []
        for k, (peer, pj) in enumerate(peers):
            for a in range(n):
                cp = _remote(ins[a].at[pj], outs[a].at[j], ssem.at[k * n + a], rsem.at[k * n + a], peer)
                cp.start()
                sends.append(cp)
        for k, (peer, pj) in enumerate(peers):
            for a in range(n):
                _remote(ins[a].at[pj], outs[a].at[pj], ssem.at[k * n + a], rsem.at[k * n + a], peer).wait_recv()
        for cp in sends:
            cp.wait_send()

    return pl.pallas_call(
        body, name=name, in_specs=[ANY] * n, out_specs=[ANY] * n,
        out_shape=[jax.ShapeDtypeStruct(g.shape, g.dtype) for g in gs],
        scratch_shapes=[pltpu.SemaphoreType.DMA((3 * n,)), pltpu.SemaphoreType.DMA((3 * n,))])(*gs)


def _sibling_swap(xs, name):
    n = len(xs)

    def body(*refs):
        ins, outs = refs[:n], refs[n:2 * n]
        ssem, rsem = refs[2 * n:]
        mx, my, mc = _me()
        cps = [_remote(ins[a], outs[a], ssem.at[a], rsem.at[a], (mx, my, 1 - mc)) for a in range(n)]
        for cp in cps:
            cp.start()
        for cp in cps:
            cp.wait()

    return pl.pallas_call(
        body, name=name, in_specs=[ANY] * n, out_specs=[ANY] * n,
        out_shape=[jax.ShapeDtypeStruct(x.shape, x.dtype) for x in xs],
        scratch_shapes=[pltpu.SemaphoreType.DMA((n,)), pltpu.SemaphoreType.DMA((n,))])(*xs)


def _sibling_halves(gs, name):
    n = len(gs)

    def body(*refs):
        ins, outs = refs[:n], refs[n:2 * n]
        ssem, rsem = refs[2 * n:]
        mx, my, mc = _me()
        cps = []
        for a in range(n):
            h = gs[a].shape[1] // 2
            cps.append(_remote(ins[a].at[:, pl.ds((1 - mc) * h, h), :], outs[a], ssem.at[a], rsem.at[a],
                               (mx, my, 1 - mc)))
        for cp in cps:
            cp.start()
        for cp in cps:
            cp.wait()

    return pl.pallas_call(
        body, name=name, in_specs=[ANY] * n, out_specs=[ANY] * n,
        out_shape=[jax.ShapeDtypeStruct((g.shape[0], g.shape[1] // 2, g.shape[2]), g.dtype) for g in gs],
        scratch_shapes=[pltpu.SemaphoreType.DMA((n,)), pltpu.SemaphoreType.DMA((n,))])(*gs)


def _sibling_fill(hs, name):
    n = len(hs)

    def body(*refs):
        ins, outs = refs[:n], refs[n:2 * n]
        ssem, rsem = refs[2 * n:]
        mx, my, mc = _me()
        cps = []
        for a in range(n):
            h = hs[a].shape[0]
            cps.append(_remote(ins[a], outs[a].at[pl.ds(mc * h, h), :], ssem.at[a], rsem.at[a], (mx, my, 1 - mc)))
        for cp in cps:
            cp.start()
        for a, cp in enumerate(cps):
            h = hs[a].shape[0]
            theirs = outs[a].at[pl.ds((1 - mc) * h, h), :]
            _remote(ins[a], theirs, ssem.at[a], rsem.at[a], (mx, my, 1 - mc)).wait_recv()
            cp.wait_send()

    return pl.pallas_call(
        body, name=name, in_specs=[ANY] * n, out_specs=[ANY] * n,
        out_shape=[jax.ShapeDtypeStruct((2 * x.shape[0], x.shape[1]), x.dtype) for x in hs],
        scratch_shapes=[pltpu.SemaphoreType.DMA((n,)), pltpu.SemaphoreType.DMA((n,))])(*hs)


def _add_cast(g, sb):
    J, h, b = g.shape
    th = _row_tile(h, b * J)

    def body(g_ref, s_ref, o_ref):
        o_ref[...] = (g_ref[...].astype(F32) + s_ref[...].astype(F32)).astype(BF16)

    spec = pl.BlockSpec((J, th, b), lambda i: (0, i, 0))
    return pl.pallas_call(
        body, grid=(h // th,), name="add_planes", in_specs=[spec, spec], out_specs=spec,
        out_shape=jax.ShapeDtypeStruct((J, h, b), BF16), compiler_params=_cp("parallel"))(g, sb)


_WEIGHTS = ("c_ctx", "w_mod", "b_mod", "norm1", "w_in", "q_gain", "k_gain", "conv_w", "sg_norm", "w_s", "b_s",
            "w_a", "w_b", "w_c", "w_o", "norm2", "w_ff1", "w_ff3", "w_ff2")
_BIG = ("w_in", "w_a", "w_b", "w_c", "w_o", "w_ff1", "w_ff3", "w_ff2")
_TRANSPOSED = ("w_ff1", "w_ff3")


def _constants():
    idx = np.arange(LANES)
    e = (idx[:, None] // 64 == idx[None, :] // 64).astype(np.float32) / 64.0
    c512 = np.arange(512)
    fold = (c512[:, None] % 64 == idx[None, :]).astype(np.float32)
    c256 = np.arange(SG_W)
    gsum = (c256[:, None] // 64 == idx[None, :]).astype(np.float32)
    return jnp.asarray(e, BF16), jnp.asarray(fold, F32), jnp.asarray(gsum, F32)


def _rope_tables(n_lat, n_ctx):
    t = jnp.arange(n_lat)
    inv = ROPE_THETA ** (-jnp.arange(0, HEAD_DIM // 2, 2, dtype=F32) / (HEAD_DIM // 2))
    ar = (t // GRID_W).astype(F32)[:, None] * inv
    ac = (t % GRID_W).astype(F32)[:, None] * inv
    cos = jnp.concatenate([jnp.cos(ar), jnp.cos(ar), jnp.cos(ac), jnp.cos(ac)], axis=1)
    sin = jnp.concatenate([-jnp.sin(ar), jnp.sin(ar), -jnp.sin(ac), jnp.sin(ac)], axis=1)
    cos = jnp.concatenate([cos, jnp.ones((n_ctx, HEAD_DIM), F32)], axis=0)
    sin = jnp.concatenate([sin, jnp.zeros((n_ctx, HEAD_DIM), F32)], axis=0)
    return jnp.concatenate([cos, cos], axis=1), jnp.concatenate([sin, sin], axis=1)


def kernel(x, c, ctx, c_ctx, w_mod, b_mod, norm1, w_in, q_gain, k_gain, conv_w, sg_norm, w_s, b_s, w_a, w_b, w_c, w_o, norm2, w_ff1, w_ff3, w_ff2, loss_target, m_c_ctx, m_w_mod, m_b_mod, m_norm1, m_w_in, m_q_gain, m_k_gain, m_conv_w, m_sg_norm, m_w_s, m_b_s, m_w_a, m_w_b, m_w_c, m_w_o, m_norm2, m_w_ff1, m_w_ff3, m_w_ff2, v_c_ctx, v_w_mod, v_b_mod, v_norm1, v_w_in, v_q_gain, v_k_gain, v_conv_w, v_sg_norm, v_w_s, v_b_s, v_w_a, v_w_b, v_w_c, v_w_o, v_norm2, v_w_ff1, v_w_ff3, v_w_ff2):
    given = dict(locals())
    mx, my, mc = _me()
    chip = 2 * mx + my
    dev = 4 * mx + 2 * my + mc
    L = norm1.shape[0]
    S, Lc = x.shape[1], ctx.shape[1]
    T = S + Lc
    D = D_MODEL
    n_mod, n_in, n_ff = w_mod.shape[2], w_in.shape[2], w_ff1.shape[2]
    n_cw = conv_w.shape[2]
    e_avg, fold, gsum = _constants()
    cos_t, sin_t = _rope_tables(S, Lc)

    cw_rows = (L * 3 * n_cw) // LANES
    pad = (-(8 + cw_rows)) % 8
    buf = jnp.concatenate([c.reshape(8, LANES), conv_w.reshape(cw_rows, LANES), jnp.zeros((pad, LANES), F32)], axis=0)
    g1 = _ag8(buf, "gather_cond")
    conds = g1[:, :8].reshape(N_DEV, D)
    cw_full = jnp.stack([g1[2 * j, 8:8 + cw_rows].reshape(L, 3, n_cw) for j in range(N_CHIPS)], axis=2)
    cw_full = cw_full.reshape(L, 3, N_CHIPS * n_cw)
    cw8 = jnp.pad(cw_full, ((0, 0), (0, 5), (0, 0)))
    a_raw = jnp.concatenate([conds, c_ctx[None], jnp.zeros((7, D), F32)], axis=0)
    bsh = lax.dynamic_slice_in_dim(b_mod, chip * n_mod, n_mod, axis=1)[:, None, :]
    mod_sh = _mod_fwd(a_raw, w_mod, bsh)
    g2 = _ag8(mod_sh.reshape(-1, LANES), "gather_mod")
    mods = jnp.stack([g2[2 * j].reshape(L, 16, n_mod) for j in range(N_CHIPS)], axis=2).reshape(L, 16, N_CHIPS * n_mod)
    lat = lax.dynamic_index_in_dim(mods, dev, axis=1, keepdims=False)
    mod = jnp.stack([lat.reshape(L, 6, D), mods[:, 8].reshape(L, 6, D)], axis=1)
    mod = jnp.pad(mod, ((0, 0), (0, 0), (0, 2), (0, 0)))

    qg = jnp.tile(q_gain, (1, N_Q_HEADS))[:, None, :]
    kg = jnp.tile(k_gain, (1, N_KV_HEADS))[:, None, :]
    sgn = sg_norm[:, None, :]
    ws_b = w_s.astype(BF16)
    zero = jnp.zeros_like(ws_b)
    bd = jnp.concatenate([jnp.concatenate([ws_b, zero], axis=3), jnp.concatenate([zero, ws_b], axis=3)], axis=2)
    bdt = jnp.swapaxes(bd, 2, 3)
    bias = jnp.tile(jnp.repeat(jnp.swapaxes(b_s, 1, 2), SG_W // 4, axis=2), (1, 2, 1))

    def shard_bufs(l, names=_BIG):
        return [lax.dynamic_update_slice(lax.empty((N_CHIPS,) + given[nm].shape[1:], BF16),
                                         given[nm][l].astype(BF16)[None], (chip, 0, 0)) for nm in names]

    def unpack(bufs):
        win, wa, wb, wc, wo, w1, w3, w2 = bufs
        return win, wa, wb, wc, wo.reshape(1, D, D), w1, w3, w2

    def layer_fwd(X, l, W, nxt):
        win = W[0]
        h, ht = _norm_mod(X, norm1[l][None], mod[l], 0, 1, S, True)
        p = _mm_nn(h, win, F32, "in_proj")
        ya = _conv_fwd(p, cw8[l], S)
        q, k, v = _qkv_prep(p, cos_t, sin_t, qg[l], kg[l], e_avg)
        at, qa, *got = _flash_fwd(q.reshape(N_KV_HEADS, GROUP, T, LANES), k, v, S, nxt)
        if len(W) == 1:
            W, got = unpack([win] + got[:len(_BIG) - 1]), got[len(_BIG) - 1:]
        win, wa, wb, wc, wo, w1, w3, w2 = W
        yc = _gmlp_fwd(p, sgn[l], bd[l], bias[l])
        mg = _merge_fwd(ya, at, yc, p, wa, wb, wc)
        X1, f1 = _mm_res(mg[None], wo, X, mod[l], 2, S, "out_proj")
        h2, = _norm_mod(X1, norm2[l][None], mod[l], 3, 4, S, False)
        a1, a3, act = _ffn_up(h2, w1, w3)
        X2, f2 = _mm_res(act, w2, X1, mod[l], 5, S, "ffn_down")
        return X2, W, got, dict(X=X, ht=ht, h2=h2, p=p, ya=ya, k=k, v=v, at=at, qa=qa, yc=yc, mg=mg, X1=X1, f1=f1,
                             a1=a1, a3=a3, act=act, f2=f2)

    def layer_bwd(dX2, dyf, l, W, sv, pending):
        win, wa, wb, wc, wo, w1, w3, w2 = W
        da1, da3 = _ffn_down_bwd(dyf, w2, sv["a1"], sv["a3"])
        dw2 = _mm_tn(sv["act"], dyf, _shard_rows(n_ff), _rows(D), N_CHIPS, n_ff, D, T, "dw_ff2")
        dh2 = _mm_nt_acc([da1, da3], [w1, w3], False, "ffn_up_bwd")
        dw1 = _mm_tn(da1, sv["h2"], _shard_rows(n_ff), _rows(D), N_CHIPS, n_ff, D, T, "dw_ff1")
        dw3 = _mm_tn(da3, sv["h2"], _shard_rows(n_ff), _rows(D), N_CHIPS, n_ff, D, T, "dw_ff3")
        dX1, dn2, dsh2, dsc2, dyo, dgt1 = _norm_mod_bwd(sv["X1"], dh2, dX2, norm2[l][None], mod[l], 4, S,
                                                        (sv["f1"], mod[l], 2))
        dwo = _mm_tn(sv["mg"], dyo, _rows(D), _rows(D), 1, D, D, T, "dw_o")
        dp, dya, doa, dyc, dwa, dwb, dwc = _merge_bwd(dyo, sv["ya"], sv["at"], sv["yc"], sv["p"], wa, wb, wc, wo[0])
        dp, dcw = _conv_bwd(dp, dya, sv["p"], cw8[l], S)
        dp, dsg, dws, dbs = _gmlp_bwd(dp, dyc, sv["p"], sgn[l], bd[l], bdt[l], bias[l], gsum)
        early = [dwa.astype(BF16), dwb.astype(BF16), dwc.astype(BF16), dwo.reshape(N_CHIPS, D // N_CHIPS, D),
                 dw1, dw3, dw2]
        dq, dk, dv, *recv = _flash_bwd(sv["qa"], doa.reshape(N_KV_HEADS, GROUP, T, LANES), sv["k"], sv["v"], S,
                                       list(pending) + (early if l == 0 else []))
        dp, dqg, dkg = _qkv_prep_bwd(dp, dq, dk, dv, sv["p"], cos_t, sin_t,
                                     qg[l], kg[l], e_avg, fold)
        dh = _mm_nt_acc([dp], [win], True, "in_proj_bwd")
        dwin = _mm_dw(sv["ht"], dp, _row_cols(n_in), N_CHIPS, n_in, "dw_in")
        below = (saved[l - 1]["f2"], mod[l - 1], 5) if l else None
        dX0, dn1, dsh1, dsc1, *nxt = _norm_mod_bwd(sv["X"], dh, dX1, norm1[l][None], mod[l], 1, S, below)
        dmod = [dsh1, dsc1, dgt1, dsh2, dsc2]
        big = [dwin] + early
        small = dict(norm1=dn1[0], norm2=dn2[0], q_gain=dqg[0, :HEAD_DIM], k_gain=dkg[0, :HEAD_DIM],
                     conv_w=dcw[:3], sg_norm=dsg[0], w_s=dws, b_s=jnp.swapaxes(dbs[:, :4], 0, 1), dmod=dmod)
        return dX0, nxt, big, small, recv

    X = jnp.concatenate([x[0], ctx[0]], axis=0)
    Ws, saved = [_chip_gather(shard_bufs(0, _BIG[:1]), "gather_weights")], []
    for l in range(L):
        nxt = (shard_bufs(0, _BIG[1:]) if l == 0 else []) + (shard_bufs(l + 1) if l + 1 < L else [])
        X, Ws[l], got, sv = layer_fwd(X, l, Ws[l], nxt)
        if got:
            Ws.append(unpack(got))
        saved.append(sv)
    dX, lpart = _loss_grad(X, loss_target[0], S)
    loss = lax.psum(lpart[0, 0], ("x", "y", "c"))

    out = {nm: () for nm in _BIG}
    smalls = [None] * L

    def own_block(r, g):
        return lax.dynamic_update_slice(r, lax.dynamic_slice_in_dim(g, chip, 1, axis=0), (chip, 0, 0))

    def stored(nm, a):
        return jnp.swapaxes(a, 1, 2) if nm in _TRANSPOSED else a

    def update(l, names, grads):
        for nm, g in zip(names, grads):
            out[nm] = _adamw_layer(stored(nm, given[nm]), stored(nm, given["m_" + nm]), stored(nm, given["v_" + nm]),
                                   g, l, out[nm])

    def plane_update(l, names, recv, sent):
        mine = [_sum_lead(own_block(r, g), "sum_chips") for r, g in zip(recv, sent)]
        update(l, names, zip(mine, _sibling_swap(mine, "swap_planes")))

    pending = []
    dyf, dgt2 = _gate_bwd(dX, saved[L - 1]["f2"], mod[L - 1], 5, S)
    for l in reversed(range(L)):
        dX, nxt, big, smalls[l], recv = layer_bwd(dX, dyf, l, Ws[l], saved[l], pending)
        smalls[l]["dmod"] = jnp.concatenate(smalls[l]["dmod"] + [dgt2], axis=1)
        if nxt:
            dyf, dgt2 = nxt
        if recv:
            plane_update(l + 1, _BIG, recv[:len(pending)], pending)
            if l == 0:
                plane_update(0, _BIG[1:], recv[len(pending):], big[1:])
        pending = big
    last = big[:1]
    sib = _sibling_halves(last, "swap_halves")
    own = [lax.dynamic_slice_in_dim(g, mc * (g.shape[1] // 2), g.shape[1] // 2, axis=1) for g in last]
    sent = [_add_cast(g, s_) for g, s_ in zip(own, sib)]
    recv = [own_block(r, g) for r, g in zip(_chip_scatter(sent, "scatter_grads"), sent)]
    halves = [_sum_lead(r, "sum_chips") for r in recv]
    full = _sibling_fill(halves, "fill_halves")
    update(0, _BIG[:1], [(lax.dynamic_update_slice(f, hv, (mc * hv.shape[0], 0)),) for f, hv in zip(full, halves)])
    grad_x = dX[:S][None]

    def flat(nm):
        return jnp.stack([smalls[l][nm] for l in range(L)]).reshape(-1)

    dmod_all = jnp.stack([smalls[l]["dmod"] for l in range(L)])
    dml = dmod_all[:, 0].reshape(-1)
    dmc = dmod_all[:, 1].reshape(-1)
    names = ("norm1", "q_gain", "k_gain", "conv_w", "sg_norm", "w_s", "b_s", "norm2")
    parts = [dml, dml + dmc, dmc] + [flat(nm) for nm in names]
    sizes = [int(a.shape[0]) for a in parts]
    total = sum(sizes)
    padn = (-total) % (8 * LANES)
    sbuf = jnp.concatenate(parts + [jnp.zeros((padn,), F32)]).reshape(-1, LANES)
    g3 = _ag8(sbuf, "gather_small")
    ssum = _sum_lead(g3, "sum_devices").reshape(-1)
    offs = np.cumsum([0] + sizes)
    seg = {nm: ssum[offs[i + 3]:offs[i + 4]] for i, nm in enumerate(names)}
    gb_mod = ssum[offs[1]:offs[2]].reshape(L, N_CHIPS * n_mod)
    dmc_sum = ssum[offs[2]:offs[3]].reshape(L, N_CHIPS * n_mod)
    dml_all = g3.reshape(N_DEV, -1)[:, :sizes[0]].reshape(N_DEV, L, N_CHIPS * n_mod)
    dml_sh = jnp.swapaxes(lax.dynamic_slice_in_dim(dml_all, chip * n_mod, n_mod, axis=2), 0, 1)
    dmc_sh = lax.dynamic_slice_in_dim(dmc_sum, chip * n_mod, n_mod, axis=1)[:, None, :]
    dms = jnp.concatenate([dml_sh, dmc_sh, jnp.zeros((L, 7, n_mod), F32)], axis=1)
    g_wmod = _wmod_grad(a_raw, dms)
    part = _cctx_partial(jnp.concatenate([dmc_sh, jnp.zeros((L, 15, n_mod), F32)], axis=1), w_mod)
    g4 = _ag8(part.reshape(-1, LANES), "gather_cctx")
    g_cctx = _cctx_final(g4, c_ctx.reshape(8, LANES)).reshape(D)

    g_conv = lax.dynamic_slice_in_dim(seg["conv_w"].reshape(L, 3, N_CHIPS * n_cw), chip * n_cw, n_cw, axis=2)
    small_g = dict(c_ctx=g_cctx, w_mod=g_wmod, b_mod=gb_mod, norm1=seg["norm1"].reshape(norm1.shape),
                   q_gain=seg["q_gain"].reshape(q_gain.shape), k_gain=seg["k_gain"].reshape(k_gain.shape),
                   conv_w=g_conv, sg_norm=seg["sg_norm"].reshape(sg_norm.shape), w_s=seg["w_s"].reshape(w_s.shape),
                   b_s=seg["b_s"].reshape(b_s.shape), norm2=seg["norm2"].reshape(norm2.shape))
    res = {}
    for nm in _WEIGHTS:
        if nm in _BIG:
            res[nm] = [stored(nm, o) for o in out[nm]]
        else:
            res[nm] = _adamw(given[nm], given["m_" + nm], given["v_" + nm], small_g[nm])
    return (loss, grad_x, *[res[nm][0] for nm in _WEIGHTS], *[res[nm][1] for nm in _WEIGHTS],
            *[res[nm][2] for nm in _WEIGHTS], *[res[nm][3] for nm in _WEIGHTS])
```

```python
import functools

import jax
import jax.numpy as jnp
import numpy as np
from jax import lax
from jax.experimental import pallas as pl
from jax.experimental.pallas import tpu as pltpu

F32 = jnp.float32
BF16 = jnp.bfloat16
EPS = 1e-6
LOG2E = 1.4426950408889634
D_MODEL = 1024
HEAD_DIM = 64
N_Q_HEADS = 8
N_KV_HEADS = 2
GROUP = N_Q_HEADS // N_KV_HEADS
GRID_W = 64
ROPE_THETA = 10000.0
CHUNK = 128
CONV_W = 256
SG_W = 256
OFF_Q = 3 * CONV_W
QKV_W = 768
OFF_U = OFF_Q + QKV_W
OFF_G = OFF_U + 2 * SG_W
IN_W = OFF_G + 3 * D_MODEL
N_CHIPS = 4
N_DEV = 8
LANES = 128
FWD_KEYS = 256
UNROLL_FWD = 16
UNROLL_BWD = 8
AUG = 3
ADAM_LR, ADAM_B1, ADAM_B2, ADAM_EPS, ADAM_WD, ADAM_STEP = 0.001, 0.9, 0.999, 1e-8, 0.01, 10
VMEM_LIMIT_V7X = 52 * 1024 * 1024
MXU_DEPTH_V7X = 256
MESH_ID = pl.DeviceIdType.MESH
NT = (((1,), (1,)), ((), ()))
TN = (((0,), (0,)), ((), ()))
ANY = pl.BlockSpec(memory_space=pl.ANY)


def _cp(*sem):
    return pltpu.CompilerParams(dimension_semantics=sem or None, vmem_limit_bytes=VMEM_LIMIT_V7X)


def _tile(n, target, mult=16):
    best = None
    for t in range(mult, n + 1, mult):
        if n % t == 0 and t <= target:
            best = t
    assert best is not None, (n, target, mult)
    return best


def _full(shape):
    nd = len(shape)
    return pl.BlockSpec(tuple(shape), lambda *_: (0,) * nd)


def _segments(i, tm, n_lat, fn):
    k, off = divmod(n_lat, tm)

    @pl.when(i < k)
    def _():
        fn(0, tm, 0)

    @pl.when(i == k)
    def _():
        if off:
            fn(0, off, 0)
        fn(off, tm, 1)

    @pl.when(i > k)
    def _():
        fn(0, tm, 1)


def _dot(a, b):
    return jnp.dot(a, b, preferred_element_type=F32)


def _dg(a, b, dims):
    return lax.dot_general(a, b, dims, preferred_element_type=F32)


def _split3(x):
    hi = x.astype(BF16)
    r1 = x - hi.astype(F32)
    mid = r1.astype(BF16)
    lo = (r1 - mid.astype(F32)).astype(BF16)
    return hi.astype(F32), mid.astype(F32), lo.astype(F32)


def _lane(shape):
    return lax.broadcasted_iota(jnp.int32, shape, len(shape) - 1)


def _aug(val, stat):
    lane = _lane(val.shape)
    hi, mid, lo = _split3(stat)
    ext = jnp.where(lane == 64, hi, jnp.where(lane == 65, mid, jnp.where(lane == 66, lo, 0.0)))
    return jnp.where(lane < 64, val, ext)


def _seg_mean(x, e):
    outs = []
    for g in range(x.shape[1] // LANES):
        blk = x[:, g * LANES:(g + 1) * LANES]
        hi = blk.astype(BF16)
        lo = (blk - hi.astype(F32)).astype(BF16)
        outs.append(_dot(hi, e) + _dot(lo, e))
    return outs[0] if len(outs) == 1 else jnp.concatenate(outs, axis=1)


def _rope(x, cos, sin_signed, inverse):
    w = x.shape[1]
    reps = w // LANES
    c = cos if reps == 1 else jnp.tile(cos, (1, reps))
    s = sin_signed if reps == 1 else jnp.tile(sin_signed, (1, reps))
    first = (_lane(x.shape) % 32) < 16
    partner = jnp.where(first, pltpu.roll(x, w - 16, 1), pltpu.roll(x, 16, 1))
    return x * c - partner * s if inverse else x * c + partner * s


def _sigmoid(x):
    return 1.0 / (1.0 + jnp.exp(-x))


_GELU_K = 0.7978845608028654
_GELU_C = 0.044715


def _gelu(x):
    return 0.5 * x * (1.0 + jnp.tanh(_GELU_K * (x + _GELU_C * x * x * x)))


def _gelu_grad(x):
    t = jnp.tanh(_GELU_K * (x + _GELU_C * x * x * x))
    return 0.5 * (1.0 + t) + 0.5 * x * (1.0 - t * t) * _GELU_K * (1.0 + 3.0 * _GELU_C * x * x)


def _loop_unrolled(n, step, init, unroll):
    def trip(t, carry):
        for u in range(unroll):
            carry = step(t * unroll + u, carry)
        return carry

    carry = lax.fori_loop(0, n // unroll, trip, init) if n >= unroll else init
    for r in range(n - n % unroll, n):
        carry = step(r, carry)
    return carry


def _heads_to_rows(x, n_heads):
    out = []
    for h in range(n_heads):
        grp = x[:, (h // 2) * LANES:(h // 2 + 1) * LANES]
        out.append(grp if h % 2 == 0 else pltpu.roll(grp, 64, 1))
    return out


def _rows_to_heads(blocks):
    outs = []
    lane = _lane(blocks[0].shape)
    for a in range(len(blocks) // 2):
        outs.append(jnp.where(lane < 64, blocks[2 * a], pltpu.roll(blocks[2 * a + 1], 64, 1)))
    return outs[0] if len(outs) == 1 else jnp.concatenate(outs, axis=1)


def _norm_mod(x, g, mod, i_shift, i_scale, n_lat, transposed):
    T, D = x.shape
    tm = _tile(T, 768, LANES)

    def body(x_ref, g_ref, mod_ref, h_ref, *ht_ref):
        def fn(r0, r1, seg):
            xv = x_ref[r0:r1, :]
            r = lax.rsqrt(jnp.mean(xv * xv, axis=-1, keepdims=True) + EPS)
            n = xv * r * g_ref[...]
            h = n * (1.0 + mod_ref[seg, i_scale:i_scale + 1, :]) + mod_ref[seg, i_shift:i_shift + 1, :]
            h_ref[r0:r1, :] = h.astype(BF16)

        _segments(pl.program_id(0), tm, n_lat, fn)
        if transposed:
            ht_ref[0][...] = h_ref[...].astype(F32).T.astype(BF16)

    return pl.pallas_call(
        body, grid=(T // tm,), name="norm_mod",
        in_specs=[pl.BlockSpec((tm, D), lambda i: (i, 0)), _full(g.shape), _full(mod.shape)],
        out_specs=[pl.BlockSpec((tm, D), lambda i: (i, 0))] + [pl.BlockSpec((D, tm), lambda i: (0, i))] * transposed,
        out_shape=[jax.ShapeDtypeStruct((T, D), BF16)] + [jax.ShapeDtypeStruct((D, T), BF16)] * transposed,
        compiler_params=_cp("parallel"))(x, g, mod)


def _norm_mod_bwd(x, dh, dres, g, mod, i_scale, n_lat, gate=None):
    T, D = x.shape
    tm = _tile(T, 528)

    def body(x_ref, dh_ref, dres_ref, g_ref, mod_ref, *rest):
        if gate is None:
            dx_ref, dg_ref, dsh_ref, dsc_ref = rest
        else:
            f_ref, gmod_ref, dx_ref, dg_ref, dsh_ref, dsc_ref, dy_ref, dgt_ref = rest
        i = pl.program_id(0)

        @pl.when(i == 0)
        def _():
            dg_ref[...] = jnp.zeros_like(dg_ref)
            dsh_ref[...] = jnp.zeros_like(dsh_ref)
            dsc_ref[...] = jnp.zeros_like(dsc_ref)
            if gate is not None:
                dgt_ref[...] = jnp.zeros_like(dgt_ref)

        def fn(r0, r1, seg):
            xv = x_ref[r0:r1, :]
            dh = dh_ref[r0:r1, :]
            r = lax.rsqrt(jnp.mean(xv * xv, axis=-1, keepdims=True) + EPS)
            xh = xv * r
            gv = g_ref[...]
            dsh_ref[seg] += jnp.sum(dh, axis=0, keepdims=True)
            dsc_ref[seg] += jnp.sum(dh * (xh * gv), axis=0, keepdims=True)
            dn = dh * (1.0 + mod_ref[seg, i_scale:i_scale + 1, :])
            dg_ref[...] += jnp.sum(dn * xh, axis=0, keepdims=True)
            gd = gv * dn
            dxv = dres_ref[r0:r1, :] + r * (gd - xh * jnp.mean(xh * gd, axis=-1, keepdims=True))
            dx_ref[r0:r1, :] = dxv
            if gate is not None:
                dy_ref[r0:r1, :] = (dxv * gmod_ref[seg, gate[2]:gate[2] + 1, :]).astype(BF16)
                dgt_ref[seg] += jnp.sum(dxv * f_ref[r0:r1, :].astype(F32), axis=0, keepdims=True)

        _segments(i, tm, n_lat, fn)

    row = pl.BlockSpec((tm, D), lambda i: (i, 0))
    extra_in = [] if gate is None else [gate[0], gate[1]]
    return pl.pallas_call(
        body, grid=(T // tm,), name="norm_mod_bwd" if gate is None else "norm_gate_bwd",
        in_specs=[row, row, row, _full(g.shape), _full(mod.shape)] + ([] if gate is None else [row, _full(gate[1].shape)]),
        out_specs=[row, _full((1, D)), _full((2, 1, D)), _full((2, 1, D))] + ([] if gate is None else [row, _full((2, 1, D))]),
        out_shape=[jax.ShapeDtypeStruct((T, D), F32), jax.ShapeDtypeStruct((1, D), F32),
                   jax.ShapeDtypeStruct((2, 1, D), F32), jax.ShapeDtypeStruct((2, 1, D), F32)]
        + ([] if gate is None else [jax.ShapeDtypeStruct((T, D), BF16), jax.ShapeDtypeStruct((2, 1, D), F32)]),
        compiler_params=_cp("arbitrary"))(x, dh, dres, g, mod, *extra_in)


def _gate_bwd(dx, f, mod, i_gate, n_lat):
    T, D = dx.shape
    tm = _tile(T, 528)

    def body(dx_ref, f_ref, mod_ref, dy_ref, dg_ref):
        i = pl.program_id(0)

        @pl.when(i == 0)
        def _():
            dg_ref[...] = jnp.zeros_like(dg_ref)

        def fn(r0, r1, seg):
            dxv = dx_ref[r0:r1, :]
            dy_ref[r0:r1, :] = (dxv * mod_ref[seg, i_gate:i_gate + 1, :]).astype(BF16)
            dg_ref[seg] += jnp.sum(dxv * f_ref[r0:r1, :].astype(F32), axis=0, keepdims=True)

        _segments(i, tm, n_lat, fn)

    row = pl.BlockSpec((tm, D), lambda i: (i, 0))
    return pl.pallas_call(
        body, grid=(T // tm,), name="gate_bwd",
        in_specs=[row, row, _full(mod.shape)], out_specs=[row, _full((2, 1, D))],
        out_shape=[jax.ShapeDtypeStruct((T, D), BF16), jax.ShapeDtypeStruct((2, 1, D), F32)],
        compiler_params=_cp("arbitrary"))(dx, f, mod)


def _mm_nn(a, w, out_dtype, name):
    M, K = a.shape
    J, _, n = w.shape
    tm = _tile(M, 1056)

    def body(a_ref, w_ref, o_ref):
        o_ref[...] = _dot(a_ref[...], w_ref[...]).astype(o_ref.dtype)

    return pl.pallas_call(
        body, grid=(M // tm, J), name=name,
        in_specs=[pl.BlockSpec((tm, K), lambda i, j: (i, 0)), pl.BlockSpec((None, K, n), lambda i, j: (j, 0, 0))],
        out_specs=pl.BlockSpec((tm, n), lambda i, j: (i, j)),
        out_shape=jax.ShapeDtypeStruct((M, J * n), out_dtype), compiler_params=_cp("parallel", "arbitrary"))(a, w)


def _mm_res(a3, w, res, mod, i_gate, n_lat, name):
    J, M, k = a3.shape
    N = w.shape[2]
    tm = _tile(M, 528)

    def body(a_ref, w_ref, res_ref, mod_ref, x_ref, f_ref):
        acc = _dot(a_ref[0], w_ref[0])
        for j in range(1, J):
            acc += _dot(a_ref[j], w_ref[j])
        f_ref[...] = acc.astype(BF16)

        def fn(r0, r1, seg):
            x_ref[r0:r1, :] = res_ref[r0:r1, :] + mod_ref[seg, i_gate:i_gate + 1, :] * acc[r0:r1, :]

        _segments(pl.program_id(0), tm, n_lat, fn)

    row = pl.BlockSpec((tm, N), lambda i: (i, 0))
    return pl.pallas_call(
        body, grid=(M // tm,), name=name,
        in_specs=[pl.BlockSpec((J, tm, k), lambda i: (0, i, 0)), _full(w.shape), row, _full(mod.shape)],
        out_specs=[row, row],
        out_shape=[jax.ShapeDtypeStruct((M, N), F32), jax.ShapeDtypeStruct((M, N), BF16)],
        compiler_params=_cp("parallel"))(a3, w, res, mod)


def _mm_nt_acc(dys, ws, row_major, name):
    J, K, n = ws[0].shape
    M = dys[0].shape[0] if row_major else dys[0].shape[1]
    tm = _tile(M, 1056)
    P = len(dys)

    def body(*refs):
        o_ref = refs[2 * P]
        j = pl.program_id(1)
        part = _dg(refs[0][...], refs[P][...], NT)
        for p in range(1, P):
            part += _dg(refs[p][...], refs[P + p][...], NT)

        @pl.when(j == 0)
        def _():
            o_ref[...] = part

        @pl.when(j > 0)
        def _():
            o_ref[...] += part

    dy_spec = (pl.BlockSpec((tm, n), lambda i, j: (i, j)) if row_major
               else pl.BlockSpec((None, tm, n), lambda i, j: (j, i, 0)))
    w_spec = pl.BlockSpec((None, K, n), lambda i, j: (j, 0, 0))
    return pl.pallas_call(
        body, grid=(M // tm, J), name=name,
        in_specs=[dy_spec] * P + [w_spec] * P,
        out_specs=pl.BlockSpec((tm, K), lambda i, j: (i, 0)),
        out_shape=jax.ShapeDtypeStruct((M, K), F32), compiler_params=_cp("parallel", "arbitrary"))(*dys, *ws)


def _mm_tn(x, dy, x_spec, dy_spec, J, K, n, T, name):
    tk = _tile(T, 1056, MXU_DEPTH_V7X)
    nt = T // tk

    def body(x_ref, dy_ref, o_ref, acc):
        t = pl.program_id(1)
        part = _dg(x_ref[...], dy_ref[...], TN)

        @pl.when(t == 0)
        def _():
            acc[...] = part

        @pl.when(t > 0)
        def _():
            acc[...] += part

        @pl.when(t == nt - 1)
        def _():
            o_ref[...] = acc[...].astype(BF16)

    return pl.pallas_call(
        body, grid=(J, nt), name=name,
        in_specs=[x_spec(tk), dy_spec(tk)],
        out_specs=pl.BlockSpec((None, K, n), lambda j, t: (j, 0, 0)),
        out_shape=jax.ShapeDtypeStruct((J, K, n), BF16), scratch_shapes=[pltpu.VMEM((K, n), F32)],
        compiler_params=_cp("parallel", "arbitrary"))(x, dy)


def _mm_dw(xt, dy, dy_spec, J, n, name):
    K, T = xt.shape
    tk = _tile(T, 1056, MXU_DEPTH_V7X)
    nt = T // tk

    def body(xt_ref, dy_ref, o_ref, acc):
        t = pl.program_id(1)
        part = _dot(xt_ref[...], dy_ref[...])

        @pl.when(t == 0)
        def _():
            acc[...] = part

        @pl.when(t > 0)
        def _():
            acc[...] += part

        @pl.when(t == nt - 1)
        def _():
            o_ref[...] = acc[...].astype(BF16)

    return pl.pallas_call(
        body, grid=(J, nt), name=name,
        in_specs=[pl.BlockSpec((K, tk), lambda j, t: (0, t)), dy_spec(tk)],
        out_specs=pl.BlockSpec((None, K, n), lambda j, t: (j, 0, 0)),
        out_shape=jax.ShapeDtypeStruct((J, K, n), BF16), scratch_shapes=[pltpu.VMEM((K, n), F32)],
        compiler_params=_cp("parallel", "arbitrary"))(xt, dy)


def _rows(width):
    return lambda tk: pl.BlockSpec((tk, width), lambda j, t: (t, 0))


def _row_cols(width):
    return lambda tk: pl.BlockSpec((tk, width), lambda j, t: (t, j))


def _shard_rows(width):
    return lambda tk: pl.BlockSpec((None, tk, width), lambda j, t: (j, t, 0))


def _ffn_up(h, w1, w3):
    T, D = h.shape
    J, _, n = w1.shape
    tm = _tile(T, 1056)

    def body(h_ref, w1_ref, w3_ref, a1_ref, a3_ref, act_ref):
        hv = h_ref[...]
        a1 = _dot(hv, w1_ref[...])
        a3 = _dot(hv, w3_ref[...])
        a1_ref[...] = a1.astype(BF16)
        a3_ref[...] = a3.astype(BF16)
        act_ref[...] = (a1 * _sigmoid(a1) * a3).astype(BF16)

    w_spec = pl.BlockSpec((None, D, n), lambda i, j: (j, 0, 0))
    o_spec = pl.BlockSpec((None, tm, n), lambda i, j: (j, i, 0))
    return pl.pallas_call(
        body, grid=(T // tm, J), name="ffn_up",
        in_specs=[pl.BlockSpec((tm, D), lambda i, j: (i, 0)), w_spec, w_spec], out_specs=[o_spec] * 3,
        out_shape=[jax.ShapeDtypeStruct((J, T, n), BF16)] * 3,
        compiler_params=_cp("parallel", "arbitrary"))(h, w1, w3)


def _ffn_down_bwd(dy, w2, a1, a3):
    T, D = dy.shape
    J, n, _ = w2.shape
    tm = _tile(T, 1056)

    def body(dy_ref, w2_ref, a1_ref, a3_ref, da1_ref, da3_ref):
        dact = _dg(dy_ref[...], w2_ref[...], NT)
        a1v = a1_ref[...].astype(F32)
        sig = _sigmoid(a1v)
        da3_ref[...] = (dact * a1v * sig).astype(BF16)
        da1_ref[...] = (dact * a3_ref[...].astype(F32) * (sig * (1.0 + a1v * (1.0 - sig)))).astype(BF16)

    a_spec = pl.BlockSpec((None, tm, n), lambda i, j: (j, i, 0))
    return pl.pallas_call(
        body, grid=(T // tm, J), name="ffn_down_bwd",
        in_specs=[pl.BlockSpec((tm, D), lambda i, j: (i, 0)), pl.BlockSpec((None, n, D), lambda i, j: (j, 0, 0)),
                  a_spec, a_spec],
        out_specs=[a_spec, a_spec], out_shape=[jax.ShapeDtypeStruct((J, T, n), BF16)] * 2,
        compiler_params=_cp("parallel", "arbitrary"))(dy, w2, a1, a3)


def _qkv_prep(p, cos, sin, qg, kg, e):
    T = p.shape[0]
    tm = _tile(T, 528)

    def body(p_ref, cos_ref, sin_ref, qg_ref, kg_ref, e_ref, q_ref, k_ref, v_ref):
        ev = e_ref[...]
        cv, sv = cos_ref[...], sin_ref[...]
        xq = p_ref[:, 0:512]
        qn = xq * lax.rsqrt(_seg_mean(xq * xq, ev) + EPS) * qg_ref[...]
        qr = _rope(qn, cv, sv, False) * (HEAD_DIM ** -0.5 * LOG2E)
        xk = p_ref[:, 512:640]
        kn = xk * lax.rsqrt(_seg_mean(xk * xk, ev) + EPS) * kg_ref[...]
        kr = _rope(kn, cv, sv, False)
        lane = _lane((tm, LANES))
        ones = jnp.where(lane < 64 + AUG, -1.0, 0.0)
        for h, blk in enumerate(_heads_to_rows(qr, N_Q_HEADS)):
            q_ref[h] = jnp.where(lane < 64, blk, 0.0).astype(BF16)
        for h, blk in enumerate(_heads_to_rows(kr, N_KV_HEADS)):
            k_ref[h] = jnp.where(lane < 64, blk, ones).astype(BF16)
        for h, blk in enumerate(_heads_to_rows(p_ref[:, 640:768], N_KV_HEADS)):
            v_ref[h] = jnp.where(lane < 64, blk, ones).astype(BF16)

    tab = pl.BlockSpec((tm, LANES), lambda i: (i, 0))
    return pl.pallas_call(
        body, grid=(T // tm,), name="qkv_prep",
        in_specs=[pl.BlockSpec((tm, QKV_W), lambda i: (i, 1)), tab, tab, _full(qg.shape), _full(kg.shape),
                  _full(e.shape)],
        out_specs=[pl.BlockSpec((N_Q_HEADS, tm, LANES), lambda i: (0, i, 0)),
                   pl.BlockSpec((N_KV_HEADS, tm, LANES), lambda i: (0, i, 0)),
                   pl.BlockSpec((N_KV_HEADS, tm, LANES), lambda i: (0, i, 0))],
        out_shape=[jax.ShapeDtypeStruct((N_Q_HEADS, T, LANES), BF16),
                   jax.ShapeDtypeStruct((N_KV_HEADS, T, LANES), BF16),
                   jax.ShapeDtypeStruct((N_KV_HEADS, T, LANES), BF16)],
        compiler_params=_cp("parallel"))(p, cos, sin, qg, kg, e)


def _qkv_prep_bwd(dp, dq, dk, dv, p, cos, sin, qg, kg, e, fold):
    T = p.shape[0]
    tq = dq.shape[3] // GROUP
    tm = _tile(T, 768, tq)
    nt = T // tm

    def body(dp_in, dq_ref, dk_ref, dv_ref, p_ref, cos_ref, sin_ref, qg_ref, kg_ref, e_ref, fold_ref,
             dp_ref, dqg_ref, dkg_ref, accq, acck):
        del dp_in
        i = pl.program_id(0)

        @pl.when(i == 0)
        def _():
            accq[...] = jnp.zeros_like(accq)
            acck[...] = jnp.zeros_like(acck)

        ev = e_ref[...]
        cv, sv = cos_ref[...], sin_ref[...]

        def one(x, dr, gain, acc):
            r = lax.rsqrt(_seg_mean(x * x, ev) + EPS)
            xh = x * r
            dn = _rope(dr, cv, sv, True)
            acc[0:1, :] += jnp.sum(dn * xh, axis=0, keepdims=True)
            gd = gain * dn
            return r * (gd - xh * _seg_mean(xh * gd, ev))

        slabs = [[dq_ref[h, b].T for b in range(tm // tq)] for h in range(N_KV_HEADS)]
        heads = [jnp.concatenate([sl[g * tq:(g + 1) * tq] for sl in slabs[h]], axis=0)
                 for h in range(N_KV_HEADS) for g in range(GROUP)]
        dqr = _rows_to_heads(heads) * (HEAD_DIM ** -0.5)
        dkr = _rows_to_heads([dk_ref[h] for h in range(N_KV_HEADS)]) * (1.0 / LOG2E)
        dvv = _rows_to_heads([dv_ref[h] for h in range(N_KV_HEADS)])
        dp_ref[:, 0:512] = one(p_ref[:, 0:512], dqr, qg_ref[...], accq).astype(BF16)
        dp_ref[:, 512:640] = one(p_ref[:, 512:640], dkr, kg_ref[...], acck).astype(BF16)
        dp_ref[:, 640:768] = dvv.astype(BF16)

        @pl.when(i == nt - 1)
        def _():
            fv = fold_ref[...]
            dqg_ref[...] = jnp.dot(accq[...], fv, preferred_element_type=F32, precision=lax.Precision.HIGHEST)
            dkg_ref[...] = jnp.dot(acck[...], fv[0:LANES, :], preferred_element_type=F32,
                                   precision=lax.Precision.HIGHEST)

    tab = pl.BlockSpec((tm, LANES), lambda i: (i, 0))
    sec = pl.BlockSpec((tm, QKV_W), lambda i: (i, 1))
    return pl.pallas_call(
        body, grid=(nt,), name="qkv_prep_bwd",
        in_specs=[ANY, pl.BlockSpec((N_KV_HEADS, tm // tq, LANES, GROUP * tq), lambda i: (0, i, 0, 0)),
                  pl.BlockSpec((N_KV_HEADS, tm, LANES), lambda i: (0, i, 0)),
                  pl.BlockSpec((N_KV_HEADS, tm, LANES), lambda i: (0, i, 0)),
                  sec, tab, tab, _full(qg.shape), _full(kg.shape), _full(e.shape), _full(fold.shape)],
        out_specs=[sec, _full((8, LANES)), _full((8, LANES))],
        out_shape=[jax.ShapeDtypeStruct(dp.shape, BF16), jax.ShapeDtypeStruct((8, LANES), F32),
                   jax.ShapeDtypeStruct((8, LANES), F32)],
        scratch_shapes=[pltpu.VMEM((8, 512), F32), pltpu.VMEM((8, LANES), F32)],
        input_output_aliases={0: 0}, compiler_params=_cp("arbitrary"))(dp, dq, dk, dv, p, cos, sin, qg, kg, e, fold)


def _flash_fwd(q, k, v, n_lat, gather=()):
    _, _, T, _ = q.shape
    tq = tk = 256
    nq = T // tq
    M = GROUP * tq

    wide_k = FWD_KEYS if n_lat % FWD_KEYS == 0 else tk
    n_g = len(gather)

    def body(q_ref, k_ref, v_ref, *rest):
        o_ref, qa_ref = rest[n_g], rest[n_g + 1]
        i = pl.program_id(0)
        if n_g:
            bufs = rest[n_g + 2:2 * n_g + 2]
            start, wait = _plane_exchange(bufs, bufs, rest[-2], rest[-1], False)
            pl.when((i == 0) & (pl.program_id(1) == 0))(start)
        qv = q_ref[...].reshape(M, LANES)

        def step(r0, width, carry):
            m, acc = carry
            sc = _dg(qv, k_ref[pl.ds(r0, width), :], NT)
            m_new = jnp.maximum(m, jnp.max(sc, axis=1, keepdims=True))
            pr = jnp.exp2(sc - m_new)
            acc = jnp.exp2(m - m_new) * acc + _dot(pr.astype(BF16), v_ref[pl.ds(r0, width), :])
            return m_new, acc

        def wide(s, carry):
            return step(s * wide_k if isinstance(s, int) else pl.multiple_of(s * wide_k, wide_k), wide_k, carry)

        def finish(m, acc):
            den = -acc[:, 64:65]
            out = acc / den
            o_ref[...] = _rows_to_heads([out[g * tq:(g + 1) * tq] for g in range(GROUP)]).astype(BF16)
            qa_ref[...] = _aug(qv.astype(F32), m + jnp.log2(den)).astype(BF16).reshape(GROUP, tq, LANES)

        init = (jnp.full((M, 1), -1e30, F32), jnp.zeros((M, LANES), F32))

        @pl.when(i < n_lat // tq)
        def _():
            carry = _loop_unrolled(n_lat // wide_k, wide, init, UNROLL_FWD)
            for r0 in range(n_lat, T, tk):
                carry = step(r0, tk, carry)
            finish(*carry)

        @pl.when(i >= n_lat // tq)
        def _():
            carry = init
            for r0 in range(n_lat, T, tk):
                carry = step(r0, tk, carry)
            finish(*carry)

        if n_g:
            pl.when((i == nq - 1) & (pl.program_id(1) == N_KV_HEADS - 1))(wait)

    q_spec = pl.BlockSpec((None, GROUP, tq, LANES), lambda i, h: (h, 0, i, 0))
    kv_spec = pl.BlockSpec((None, T, LANES), lambda i, h: (h, 0, 0))
    sems = [pltpu.SemaphoreType.DMA((3 * n_g,))] * 2 if n_g else []
    return pl.pallas_call(
        body, grid=(nq, N_KV_HEADS), name="flash_fwd_gather" if n_g else "flash_fwd",
        in_specs=[q_spec, kv_spec, kv_spec] + [ANY] * n_g,
        out_specs=[pl.BlockSpec((tq, GROUP * HEAD_DIM), lambda i, h: (i, h)), q_spec] + [ANY] * n_g,
        out_shape=[jax.ShapeDtypeStruct((T, N_Q_HEADS * HEAD_DIM), BF16), jax.ShapeDtypeStruct(q.shape, BF16)]
        + [jax.ShapeDtypeStruct(b.shape, b.dtype) for b in gather],
        input_output_aliases={3 + a: 2 + a for a in range(n_g)}, scratch_shapes=sems,
        compiler_params=_cp("arbitrary", "arbitrary"))(q, k, v, *gather)


def _flash_bwd(qa, doa, k, v, n_lat, scatter=()):
    _, _, T, _ = qa.shape
    tq = tk = 256
    nkv = T // tk
    M = GROUP * tq

    n_s = len(scatter)

    def body(qa_hbm, doa_hbm, k_ref, v_ref, *rest):
        dq_hbm, dk_ref, dv_ref = rest[n_s:n_s + 3]
        q_sc, do_sc, dq_sc, sems = rest[2 * n_s + 3:2 * n_s + 7]
        h = pl.program_id(0)
        j = pl.program_id(1)
        if n_s:
            start, wait = _plane_exchange(rest[:n_s], rest[n_s + 3:2 * n_s + 3], rest[-2], rest[-1], True)
            pl.when((h == 0) & (j == 0))(start)

        @pl.when(j == 0)
        def _():
            c1 = pltpu.make_async_copy(qa_hbm.at[h], q_sc, sems.at[0])
            c2 = pltpu.make_async_copy(doa_hbm.at[h], do_sc, sems.at[1])
            c1.start()
            c2.start()
            dq_sc[...] = jnp.zeros_like(dq_sc)
            c1.wait()
            c2.wait()

        kb = k_ref[...]
        vb = v_ref[...]
        kbt = kb.astype(F32).T.astype(BF16)

        def step(i, carry):
            dk, dv = carry
            r0 = i * tq if isinstance(i, int) else pl.multiple_of(i * tq, tq)
            qv = q_sc[:, pl.ds(r0, tq), :].reshape(M, LANES)
            dov = do_sc[:, pl.ds(r0, tq), :].reshape(M, LANES)
            pr = jnp.exp2(_dg(kb, qv, NT))
            ds = (pr * _dg(vb, dov, NT)).astype(BF16)
            dv = dv + _dot(pr.astype(BF16), dov)
            dk = dk + _dot(ds, qv)
            dq_sc[i] += _dot(kbt, ds)
            return dk, dv

        z = jnp.zeros((tk, LANES), F32)
        carry = _loop_unrolled(n_lat // tq, step, (z, z), UNROLL_BWD)
        dk_ref[...] = carry[0]
        dv_ref[...] = carry[1]

        @pl.when(j >= n_lat // tk)
        def _():
            c = (dk_ref[...], dv_ref[...])
            for i in range(n_lat // tq, T // tq):
                c = step(i, c)
            dk_ref[...] = c[0]
            dv_ref[...] = c[1]

        @pl.when(j == nkv - 1)
        def _():
            c3 = pltpu.make_async_copy(dq_sc, dq_hbm.at[h], sems.at[2])
            c3.start()
            c3.wait()

        if n_s:
            pl.when((h == N_KV_HEADS - 1) & (j == nkv - 1))(wait)

    kv_spec = pl.BlockSpec((None, tk, LANES), lambda h, j: (h, j, 0))
    return pl.pallas_call(
        body, grid=(N_KV_HEADS, nkv), name="flash_bwd_scatter" if n_s else "flash_bwd",
        in_specs=[ANY, ANY, kv_spec, kv_spec] + [ANY] * n_s, out_specs=[ANY, kv_spec, kv_spec] + [ANY] * n_s,
        out_shape=[jax.ShapeDtypeStruct((N_KV_HEADS, T // tq, LANES, M), F32), jax.ShapeDtypeStruct(k.shape, F32),
                   jax.ShapeDtypeStruct(k.shape, F32)] + [jax.ShapeDtypeStruct(g.shape, g.dtype) for g in scatter],
        scratch_shapes=[pltpu.VMEM((GROUP, T, LANES), BF16), pltpu.VMEM((GROUP, T, LANES), BF16),
                        pltpu.VMEM((T // tq, LANES, M), F32), pltpu.SemaphoreType.DMA((3,))]
        + ([pltpu.SemaphoreType.DMA((3 * n_s,))] * 2 if n_s else []),
        compiler_params=_cp("arbitrary", "arbitrary"))(qa, doa, k, v, *scatter)


def _conv_masks(i, tm, n_lat, T):
    row = lax.broadcasted_iota(jnp.int32, (tm, 1), 0)
    g = row + i * tm
    return row, (g == 0) | (g == n_lat), (g == n_lat - 1) | (g == T - 1)


def _shift_rows(v, prev_row, next_row, row, first, last):
    tm = v.shape[0]
    down = jnp.where(row == 0, prev_row, pltpu.roll(v, 1, 0))
    up = jnp.where(row == tm - 1, next_row, pltpu.roll(v, tm - 1, 0))
    return jnp.where(first, 0.0, down), jnp.where(last, 0.0, up)


def _halo_specs(tm, T, width, col):
    nb = T // 8
    return (pl.BlockSpec((8, width), lambda i: (jnp.maximum(i * (tm // 8) - 1, 0), col)),
            pl.BlockSpec((8, width), lambda i: (jnp.minimum((i + 1) * (tm // 8), nb - 1), col)))


def _conv_fwd(p, cw, n_lat):
    T = p.shape[0]
    tm = _tile(T, 1056)

    def body(p_ref, pp_ref, pn_ref, cw_ref, o_ref):
        row, first, last = _conv_masks(pl.program_id(0), tm, n_lat, T)
        z = p_ref[:, 256:512] * p_ref[:, 512:768]
        zp = pp_ref[7:8, 256:512] * pp_ref[7:8, 512:768]
        zn = pn_ref[0:1, 256:512] * pn_ref[0:1, 512:768]
        zd, zu = _shift_rows(z, zp, zn, row, first, last)
        conv = cw_ref[0:1, :] * zd + cw_ref[1:2, :] * z + cw_ref[2:3, :] * zu
        o_ref[...] = (p_ref[:, 0:256] * conv).astype(BF16)

    prev, nxt = _halo_specs(tm, T, 768, 0)
    return pl.pallas_call(
        body, grid=(T // tm,), name="conv_fwd",
        in_specs=[pl.BlockSpec((tm, 768), lambda i: (i, 0)), prev, nxt, _full(cw.shape)],
        out_specs=pl.BlockSpec((tm, CONV_W), lambda i: (i, 0)),
        out_shape=jax.ShapeDtypeStruct((T, CONV_W), BF16), compiler_params=_cp("parallel"))(p, p, p, cw)


def _conv_bwd(dp, dy, p, cw, n_lat):
    T = p.shape[0]
    tm = _tile(T, 1056)

    def body(dp_in, dy_ref, dyp_ref, dyn_ref, p_ref, pp_ref, pn_ref, cw_ref, dp_ref, dcw_ref):
        del dp_in
        i = pl.program_id(0)

        @pl.when(i == 0)
        def _():
            dcw_ref[...] = jnp.zeros_like(dcw_ref)

        row, first, last = _conv_masks(i, tm, n_lat, T)
        ab, ac, ax = p_ref[:, 0:256], p_ref[:, 256:512], p_ref[:, 512:768]
        z = ac * ax
        zp = pp_ref[7:8, 256:512] * pp_ref[7:8, 512:768]
        zn = pn_ref[0:1, 256:512] * pn_ref[0:1, 512:768]
        zd, zu = _shift_rows(z, zp, zn, row, first, last)
        w0, w1, w2 = cw_ref[0:1, :], cw_ref[1:2, :], cw_ref[2:3, :]
        dy = dy_ref[...]
        dc = dy * ab
        dcd, dcu = _shift_rows(dc, dyp_ref[7:8, :] * pp_ref[7:8, 0:256], dyn_ref[0:1, :] * pn_ref[0:1, 0:256],
                               row, first, last)
        dz = w0 * dcu + w1 * dc + w2 * dcd
        dp_ref[:, 0:256] = (dy * (w0 * zd + w1 * z + w2 * zu)).astype(BF16)
        dp_ref[:, 256:512] = (dz * ax).astype(BF16)
        dp_ref[:, 512:768] = (dz * ac).astype(BF16)
        dcw_ref[0:1, :] += jnp.sum(dc * zd, axis=0, keepdims=True)
        dcw_ref[1:2, :] += jnp.sum(dc * z, axis=0, keepdims=True)
        dcw_ref[2:3, :] += jnp.sum(dc * zu, axis=0, keepdims=True)

    prev, nxt = _halo_specs(tm, T, 768, 0)
    dprev, dnxt = _halo_specs(tm, T, CONV_W, 0)
    sec = pl.BlockSpec((tm, 768), lambda i: (i, 0))
    return pl.pallas_call(
        body, grid=(T // tm,), name="conv_bwd",
        in_specs=[ANY, pl.BlockSpec((tm, CONV_W), lambda i: (i, 0)), dprev, dnxt, sec, prev, nxt, _full(cw.shape)],
        out_specs=[sec, _full((8, CONV_W))],
        out_shape=[jax.ShapeDtypeStruct(dp.shape, BF16), jax.ShapeDtypeStruct((8, CONV_W), F32)],
        input_output_aliases={0: 0}, compiler_params=_cp("arbitrary"))(dp, dy, dy, dy, p, p, p, cw)


def _gmlp_mix(bd_ref, vs, grp):
    out = jnp.zeros((2 * CHUNK, SG_W), F32)
    for g in range(4):
        out = jnp.where(grp == g, _dot(bd_ref[g], vs), out)
    return out


def _gmlp_fwd(p, sgn, bd, bias):
    T = p.shape[0]
    tm = _tile(T, 768, 2 * CHUNK)

    def body(p_ref, sgn_ref, bd_ref, bias_ref, o_ref):
        x = _gelu(p_ref[:, 256:512])
        vn = (x * lax.rsqrt(jnp.mean(x * x, axis=-1, keepdims=True) + EPS) * sgn_ref[...]).astype(BF16)
        grp = _lane((2 * CHUNK, SG_W)) // 64
        for s in range(tm // (2 * CHUNK)):
            rs = slice(s * 2 * CHUNK, (s + 1) * 2 * CHUNK)
            mixed = _gmlp_mix(bd_ref, vn[rs], grp) + bias_ref[...]
            o_ref[rs, :] = (_gelu(p_ref[rs, 0:256]) * mixed).astype(BF16)

    return pl.pallas_call(
        body, grid=(T // tm,), name="gmlp_fwd",
        in_specs=[pl.BlockSpec((tm, 2 * SG_W), lambda i: (i, 3)), _full(sgn.shape), _full(bd.shape),
                  _full(bias.shape)],
        out_specs=pl.BlockSpec((tm, SG_W), lambda i: (i, 0)),
        out_shape=jax.ShapeDtypeStruct((T, SG_W), BF16), compiler_params=_cp("parallel"))(p, sgn, bd, bias)


def _gmlp_bwd(dp, dy, p, sgn, bd, bdt, bias, gsum):
    T = p.shape[0]
    tm = _tile(T, 768, 2 * CHUNK)
    nt = T // tm
    C2 = 2 * CHUNK

    def body(dp_in, dy_ref, p_ref, sgn_ref, bd_ref, bdt_ref, bias_ref, gsum_ref,
             dp_ref, dsg_ref, dws_ref, dbs_ref, acc_w, acc_b):
        del dp_in
        i = pl.program_id(0)

        @pl.when(i == 0)
        def _():
            dsg_ref[...] = jnp.zeros_like(dsg_ref)
            acc_w[...] = jnp.zeros_like(acc_w)
            acc_b[...] = jnp.zeros_like(acc_b)

        u = p_ref[:, 0:256]
        sv = p_ref[:, 256:512]
        ug = _gelu(u)
        x = _gelu(sv)
        r = lax.rsqrt(jnp.mean(x * x, axis=-1, keepdims=True) + EPS)
        xh = x * r
        sg = sgn_ref[...]
        vn = (xh * sg).astype(BF16)
        grp = _lane((C2, SG_W)) // 64
        dug, dvn = [], []
        for s in range(tm // C2):
            rs = slice(s * C2, (s + 1) * C2)
            vs = vn[rs]
            dys = dy_ref[rs, :]
            dug.append(dys * (_gmlp_mix(bd_ref, vs, grp) + bias_ref[...]))
            dmix = dys * ug[rs]
            acc_b[...] += dmix
            dmb = dmix.astype(BF16)
            dvn.append(_gmlp_mix(bdt_ref, dmb, grp))
            for g in range(4):
                acc_w[g] += _dg(jnp.where(grp == g, dmb, jnp.zeros_like(dmb)), vs, NT)
        dug = jnp.concatenate(dug, axis=0)
        dvn = jnp.concatenate(dvn, axis=0)
        dsg_ref[...] += jnp.sum(dvn * xh, axis=0, keepdims=True)
        gd = sg * dvn
        dx = r * (gd - xh * jnp.mean(xh * gd, axis=-1, keepdims=True))
        dp_ref[:, 0:256] = (dug * _gelu_grad(u)).astype(BF16)
        dp_ref[:, 256:512] = (dx * _gelu_grad(sv)).astype(BF16)

        @pl.when(i == nt - 1)
        def _():
            for g in range(4):
                dws_ref[g] = acc_w[g, 0:CHUNK, 0:CHUNK] + acc_w[g, CHUNK:C2, CHUNK:C2]
            dbs_ref[...] = jnp.dot(acc_b[0:CHUNK, :] + acc_b[CHUNK:C2, :], gsum_ref[...],
                                   preferred_element_type=F32, precision=lax.Precision.HIGHEST)

    sec = pl.BlockSpec((tm, 2 * SG_W), lambda i: (i, 3))
    return pl.pallas_call(
        body, grid=(nt,), name="gmlp_bwd",
        in_specs=[ANY, pl.BlockSpec((tm, SG_W), lambda i: (i, 0)), sec, _full(sgn.shape), _full(bd.shape),
                  _full(bdt.shape), _full(bias.shape), _full(gsum.shape)],
        out_specs=[sec, _full((1, SG_W)), _full((4, CHUNK, CHUNK)), _full((CHUNK, LANES))],
        out_shape=[jax.ShapeDtypeStruct(dp.shape, BF16), jax.ShapeDtypeStruct((1, SG_W), F32),
                   jax.ShapeDtypeStruct((4, CHUNK, CHUNK), F32), jax.ShapeDtypeStruct((CHUNK, LANES), F32)],
        scratch_shapes=[pltpu.VMEM((4, C2, C2), F32), pltpu.VMEM((C2, SG_W), F32)],
        input_output_aliases={0: 0}, compiler_params=_cp("arbitrary"))(dp, dy, p, sgn, bd, bdt, bias, gsum)


def _merge_fwd(ya, at, yc, p, wa, wb, wc):
    T = p.shape[0]
    tm = _tile(T, 528)
    n = wa.shape[2]

    def body(ya_ref, at_ref, yc_ref, ga_ref, gb_ref, gc_ref, wa_ref, wb_ref, wc_ref, o_ref):
        yav, atv, ycv = ya_ref[...], at_ref[...], yc_ref[...]
        for j in range(N_CHIPS):
            cs = slice(j * n, (j + 1) * n)
            m = (_sigmoid(ga_ref[:, cs]) * _dot(yav, wa_ref[j]) + _sigmoid(gb_ref[:, cs]) * _dot(atv, wb_ref[j])
                 + _sigmoid(gc_ref[:, cs]) * _dot(ycv, wc_ref[j]))
            o_ref[:, cs] = m.astype(BF16)

    def rows(w, col=0):
        return pl.BlockSpec((tm, w), lambda i: (i, col))

    return pl.pallas_call(
        body, grid=(T // tm,), name="merge_fwd",
        in_specs=[rows(CONV_W), rows(512), rows(SG_W), rows(D_MODEL, 2), rows(D_MODEL, 3), rows(D_MODEL, 4),
                  _full(wa.shape), _full(wb.shape), _full(wc.shape)],
        out_specs=rows(D_MODEL), out_shape=jax.ShapeDtypeStruct((T, D_MODEL), BF16),
        compiler_params=_cp("parallel"))(ya, at, yc, p, p, p, wa, wb, wc)


def _merge_bwd(dyo, ya, at, yc, p, wa, wb, wc, wo):
    T = p.shape[0]
    tm = _tile(T, 528)
    n = wa.shape[2]

    def body(dyo_ref, ya_ref, at_ref, yc_ref, ga_ref, gb_ref, gc_ref, wa_ref, wb_ref, wc_ref, wo_ref,
             dp_ref, dya_ref, doa_ref, dyc_ref, dwa_ref, dwb_ref, dwc_ref):
        i = pl.program_id(0)

        @pl.when(i == 0)
        def _():
            dwa_ref[...] = jnp.zeros_like(dwa_ref)
            dwb_ref[...] = jnp.zeros_like(dwb_ref)
            dwc_ref[...] = jnp.zeros_like(dwc_ref)

        dp_ref[:, 0:OFF_G] = jnp.zeros((tm, OFF_G), BF16)
        dm = _dg(dyo_ref[...], wo_ref[...], NT)
        yav, atv, ycv = ya_ref[...], at_ref[...], yc_ref[...]
        dya = jnp.zeros((tm, CONV_W), F32)
        dat = jnp.zeros((tm, 512), F32)
        dyc = jnp.zeros((tm, SG_W), F32)
        for j in range(N_CHIPS):
            cs = slice(j * n, (j + 1) * n)
            dmj = dm[:, cs]
            for y_in, w_ref, g_ref, dw_ref, which in (
                    (yav, wa_ref, ga_ref, dwa_ref, 0), (atv, wb_ref, gb_ref, dwb_ref, 1),
                    (ycv, wc_ref, gc_ref, dwc_ref, 2)):
                sg = _sigmoid(g_ref[:, cs])
                y = _dot(y_in, w_ref[j])
                c0 = OFF_G + which * D_MODEL + j * n
                dp_ref[:, c0:c0 + n] = (dmj * y * sg * (1.0 - sg)).astype(BF16)
                dyb = (dmj * sg).astype(BF16)
                dw_ref[j] += _dg(y_in, dyb, TN)
                back = _dg(dyb, w_ref[j], NT)
                if which == 0:
                    dya = dya + back
                elif which == 1:
                    dat = dat + back
                else:
                    dyc = dyc + back
        dya_ref[...] = dya
        dyc_ref[...] = dyc
        prod = dat * atv.astype(F32)
        lane = _lane((tm, LANES))
        dat_rows = _heads_to_rows(dat, N_Q_HEADS)
        for h in range(N_Q_HEADS):
            grp = prod[:, (h // 2) * LANES:(h // 2 + 1) * LANES]
            keep = (lane < 64) if h % 2 == 0 else (lane >= 64)
            delta = jnp.sum(jnp.where(keep, grp, 0.0), axis=1, keepdims=True)
            doa_ref[h] = _aug(dat_rows[h], delta).astype(BF16)

    def rows(w, col=0):
        return pl.BlockSpec((tm, w), lambda i: (i, col))

    return pl.pallas_call(
        body, grid=(T // tm,), name="merge_bwd",
        in_specs=[rows(D_MODEL), rows(CONV_W), rows(512), rows(SG_W), rows(D_MODEL, 2), rows(D_MODEL, 3),
                  rows(D_MODEL, 4), _full(wa.shape), _full(wb.shape), _full(wc.shape), _full(wo.shape)],
        out_specs=[rows(IN_W), rows(CONV_W),
                   pl.BlockSpec((N_Q_HEADS, tm, LANES), lambda i: (0, i, 0)), rows(SG_W),
                   _full(wa.shape), _full(wb.shape), _full(wc.shape)],
        out_shape=[jax.ShapeDtypeStruct((T, IN_W), BF16)] + [
            jax.ShapeDtypeStruct((T, CONV_W), F32), jax.ShapeDtypeStruct((N_Q_HEADS, T, LANES), BF16),
            jax.ShapeDtypeStruct((T, SG_W), F32), jax.ShapeDtypeStruct(wa.shape, F32),
            jax.ShapeDtypeStruct(wb.shape, F32), jax.ShapeDtypeStruct(wc.shape, F32)],
        compiler_params=_cp("arbitrary"))(dyo, ya, at, yc, p, p, p, wa, wb, wc, wo)


def _loss_grad(xf, tgt, n_lat):
    T, D = xf.shape
    tm = _tile(np.gcd(n_lat, T), 512)
    nl = n_lat // tm

    def body(x_ref, t_ref, dy_ref, l_ref):
        i = pl.program_id(0)

        @pl.when(i == 0)
        def _():
            l_ref[...] = jnp.zeros_like(l_ref)

        @pl.when(i < nl)
        def _():
            err = x_ref[...] - t_ref[...]
            dy_ref[...] = err * (1.0 / D)
            sq = jnp.sum(jnp.sum(err * err, axis=1, keepdims=True), axis=0, keepdims=True)
            l_ref[...] += (0.5 / D) * sq

        @pl.when(i >= nl)
        def _():
            dy_ref[...] = jnp.zeros_like(dy_ref)

    return pl.pallas_call(
        body, grid=(T // tm,), name="loss_grad",
        in_specs=[pl.BlockSpec((tm, D), lambda i: (i, 0)), pl.BlockSpec((tm, D), lambda i: (jnp.minimum(i, nl - 1), 0))],
        out_specs=[pl.BlockSpec((tm, D), lambda i: (i, 0)), _full((8, LANES))],
        out_shape=[jax.ShapeDtypeStruct((T, D), F32), jax.ShapeDtypeStruct((8, LANES), F32)],
        compiler_params=_cp("arbitrary"))(xf, tgt)


def _row_tile(R, C):
    if R * C <= (1 << 19) or R % 8:
        return R
    return _tile(R, max(8, (1 << 19) // C), 8)


def _adamw(w, m, v, g1, g2=None):
    shape = w.shape
    C = shape[-1]
    R = int(np.prod(shape[:-1])) if len(shape) > 1 else 1
    tr = _row_tile(R, C)
    ins = [a.reshape(R, C) for a in ((w, m, v, g1) if g2 is None else (w, m, v, g1, g2))]

    def body(*refs):
        w_ref, m_ref, v_ref = refs[0], refs[1], refs[2]
        g_ref, d_ref, m2_ref, v2_ref = refs[-4:]
        g = refs[3][...] if g2 is None else refs[3][...] + refs[4][...]
        m2 = ADAM_B1 * m_ref[...] + (1.0 - ADAM_B1) * g
        v2 = ADAM_B2 * v_ref[...] + (1.0 - ADAM_B2) * (g * g)
        m_hat = m2 / (1.0 - ADAM_B1 ** ADAM_STEP)
        v_hat = v2 / (1.0 - ADAM_B2 ** ADAM_STEP)
        g_ref[...] = g
        d_ref[...] = -ADAM_LR * (m_hat / (jnp.sqrt(v_hat) + ADAM_EPS) + ADAM_WD * w_ref[...])
        m2_ref[...] = m2
        v2_ref[...] = v2

    spec = pl.BlockSpec((tr, C), lambda i: (i, 0))
    outs = pl.pallas_call(
        body, grid=(R // tr,), name="adamw", in_specs=[spec] * len(ins), out_specs=[spec] * 4,
        out_shape=[jax.ShapeDtypeStruct((R, C), F32)] * 4, compiler_params=_cp("parallel"))(*ins)
    return [o.reshape(shape) for o in outs]


def _adamw_layer(w, m, v, gs, l, prev):
    L, a, b = w.shape
    tr = _row_tile(a, b)
    n_in = 3 + len(gs)

    def body(*refs):
        g_ref, d_ref, m2_ref, v2_ref = refs[-4:]
        g = refs[3][...]
        for r in refs[4:3 + len(gs)]:
            g = g + r[...]
        m2 = ADAM_B1 * refs[1][...] + (1.0 - ADAM_B1) * g
        v2 = ADAM_B2 * refs[2][...] + (1.0 - ADAM_B2) * (g * g)
        m_hat = m2 / (1.0 - ADAM_B1 ** ADAM_STEP)
        v_hat = v2 / (1.0 - ADAM_B2 ** ADAM_STEP)
        g_ref[...] = g
        d_ref[...] = -ADAM_LR * (m_hat / (jnp.sqrt(v_hat) + ADAM_EPS) + ADAM_WD * refs[0][...])
        m2_ref[...] = m2
        v2_ref[...] = v2

    layer = pl.BlockSpec((None, tr, b), lambda i: (l, i, 0))
    outs = pl.pallas_call(
        body, grid=(a // tr,), name="adamw_layer",
        in_specs=[layer] * 3 + [pl.BlockSpec((tr, b), lambda i: (i, 0))] * len(gs) + [ANY] * len(prev),
        out_specs=[layer] * 4, out_shape=[jax.ShapeDtypeStruct((L, a, b), F32)] * 4,
        input_output_aliases={n_in + k: k for k in range(len(prev))},
        compiler_params=_cp("parallel"))(w, m, v, *gs, *prev)
    return list(outs)


def _sum_lead(x, name):
    n, R, C = x.shape
    tr = _row_tile(R, C * n)

    def body(x_ref, o_ref):
        acc = x_ref[0].astype(F32)
        for s in range(1, n):
            acc = acc + x_ref[s].astype(F32)
        o_ref[...] = acc

    return pl.pallas_call(
        body, grid=(R // tr,), name=name, in_specs=[pl.BlockSpec((n, tr, C), lambda i: (0, i, 0))],
        out_specs=pl.BlockSpec((tr, C), lambda i: (i, 0)), out_shape=jax.ShapeDtypeStruct((R, C), F32),
        compiler_params=_cp("parallel"))(x)


def _silu(x):
    return x * _sigmoid(x)


def _mod_fwd(a_raw, w_mod, bsh):
    L, D, n = w_mod.shape

    def body(a_ref, w_ref, b_ref, o_ref):
        o_ref[...] = _dot(_silu(a_ref[...]).astype(BF16), w_ref[...].astype(BF16)) + b_ref[...]

    return pl.pallas_call(
        body, grid=(L,), name="mod_fwd",
        in_specs=[_full(a_raw.shape), pl.BlockSpec((None, D, n), lambda l: (l, 0, 0)),
                  pl.BlockSpec((None, 1, n), lambda l: (l, 0, 0))],
        out_specs=pl.BlockSpec((None, 16, n), lambda l: (l, 0, 0)),
        out_shape=jax.ShapeDtypeStruct((L, 16, n), F32), compiler_params=_cp("parallel"))(a_raw, w_mod, bsh)


def _wmod_grad(a_raw, dms):
    L, _, n = dms.shape
    D = a_raw.shape[1]

    def body(a_ref, dm_ref, o_ref):
        o_ref[...] = _dg(_silu(a_ref[...]).astype(BF16), dm_ref[...].astype(BF16), TN)

    return pl.pallas_call(
        body, grid=(L,), name="wmod_grad",
        in_specs=[_full(a_raw.shape), pl.BlockSpec((None, 16, n), lambda l: (l, 0, 0))],
        out_specs=pl.BlockSpec((None, D, n), lambda l: (l, 0, 0)),
        out_shape=jax.ShapeDtypeStruct((L, D, n), F32), compiler_params=_cp("parallel"))(a_raw, dms)


def _cctx_partial(dmc, w_mod):
    L, D, n = w_mod.shape

    def body(dm_ref, w_ref, o_ref):
        part = _dg(dm_ref[...].astype(BF16), w_ref[...].astype(BF16), NT)

        @pl.when(pl.program_id(0) == 0)
        def _():
            o_ref[...] = part

        @pl.when(pl.program_id(0) > 0)
        def _():
            o_ref[...] += part

    return pl.pallas_call(
        body, grid=(L,), name="cctx_partial",
        in_specs=[pl.BlockSpec((None, 16, n), lambda l: (l, 0, 0)), pl.BlockSpec((None, D, n), lambda l: (l, 0, 0))],
        out_specs=_full((16, D)), out_shape=jax.ShapeDtypeStruct((16, D), F32),
        compiler_params=_cp("arbitrary"))(dmc, w_mod)


def _cctx_final(parts, cc):
    def body(p_ref, c_ref, o_ref):
        s = p_ref[0, 0:8, :]
        for j in range(1, N_CHIPS):
            s = s + p_ref[2 * j, 0:8, :]
        xv = c_ref[...]
        sg = _sigmoid(xv)
        o_ref[...] = s * (sg * (1.0 + xv * (1.0 - sg)))

    return pl.pallas_call(
        body, name="cctx_final", in_specs=[_full(parts.shape), _full(cc.shape)], out_specs=_full((8, LANES)),
        out_shape=jax.ShapeDtypeStruct((8, LANES), F32), compiler_params=_cp())(parts, cc)


def _me():
    return lax.axis_index("x"), lax.axis_index("y"), lax.axis_index("c")


def _flip(v, bit):
    return 1 - v if bit else v


def _remote(src, dst, ssem, rsem, peer):
    return pltpu.make_async_remote_copy(src_ref=src, dst_ref=dst, send_sem=ssem, recv_sem=rsem,
                                        device_id=peer, device_id_type=MESH_ID)


def _ag8(xb, name):
    R = xb.shape[0]

    def pallas(x):
        def body(x_ref, o_ref, ssem, rsem):
            mx, my, mc = _me()
            me = 4 * mx + 2 * my + mc
            sib = (mx, my, 1 - mc)
            peers = _plane_peers(mx, my, mc)
            sends = [_remote(x_ref, o_ref.at[me], ssem.at[0], rsem.at[0], sib)]
            sends += [_remote(x_ref, o_ref.at[me], ssem.at[1 + k], rsem.at[1 + k], peer)
                      for k, (peer, _) in enumerate(peers)]
            for cp in sends:
                cp.start()
            for k, (peer, pj) in enumerate(peers):
                got = o_ref.at[2 * pj + mc]
                _remote(got, got, ssem.at[1 + k], rsem.at[1 + k], peer).wait_recv()
                fw = _remote(got, got, ssem.at[4 + k], rsem.at[4 + k], sib)
                fw.start()
                sends.append(fw)
            _remote(x_ref, o_ref.at[4 * mx + 2 * my + 1 - mc], ssem.at[0], rsem.at[0], sib).wait_recv()
            for k, (_, pj) in enumerate(peers):
                theirs = o_ref.at[2 * pj + 1 - mc]
                _remote(theirs, theirs, ssem.at[4 + k], rsem.at[4 + k], sib).wait_recv()
            for cp in sends:
                cp.wait_send()

        return pl.pallas_call(
            body, name=name, in_specs=[ANY], out_specs=ANY, out_shape=jax.ShapeDtypeStruct((N_DEV, R, LANES), F32),
            scratch_shapes=[pltpu.SemaphoreType.DMA((N_DEV - 1,)), pltpu.SemaphoreType.DMA((N_DEV - 1,))])(x)

    mx, my, mc = _me()
    return lax.dynamic_update_slice(pallas(xb), xb[None], (4 * mx + 2 * my + mc, 0, 0))


def _plane_peers(mx, my, mc):
    out = []
    for k in range(1, N_CHIPS):
        px, py = _flip(mx, k & 2), _flip(my, k & 1)
        out.append(((px, py, mc), 2 * px + py))
    return out


def _plane_exchange(ins, outs, ssem, rsem, scatter):
    n = len(ins)

    def desc(k, a, arriving):
        mx, my, mc = _me()
        j = 2 * mx + my
        peer, pj = _plane_peers(mx, my, mc)[k]
        src = ins[a].at[pj if scatter else j]
        dst = outs[a].at[pj if arriving else j]
        return _remote(src, dst, ssem.at[k * n + a], rsem.at[k * n + a], peer)

    def start():
        for k in range(N_CHIPS - 1):
            for a in range(n):
                desc(k, a, False).start()

    def wait():
        for k in range(N_CHIPS - 1):
            for a in range(n):
                desc(k, a, True).wait_recv()
        for k in range(N_CHIPS - 1):
            for a in range(n):
                desc(k, a, False).wait_send()

    return start, wait


def _chip_gather(bufs, name):
    n = len(bufs)
    halves = [b.shape[1] // 2 for b in bufs]

    def body(*refs):
        outs = refs[n:2 * n]
        ssem, rsem, fsem, gsem = refs[2 * n:]
        mx, my, mc = _me()
        j = 2 * mx + my
        sib = (mx, my, 1 - mc)

        def half(a, blk, c):
            return outs[a].at[blk, pl.ds(c * halves[a], halves[a]), :]

        peers = _plane_peers(mx, my, mc)
        sends = []
        for k, (peer, _) in enumerate(peers):
            for a in range(n):
                mine = half(a, j, mc)
                cp = _remote(mine, mine, ssem.at[k * n + a], rsem.at[k * n + a], peer)
                cp.start()
                sends.append(cp)
        for k, (peer, pj) in enumerate(peers):
            for a in range(n):
                got = half(a, pj, mc)
                _remote(got, got, ssem.at[k * n + a], rsem.at[k * n + a], peer).wait_recv()
                fw = _remote(got, got, fsem.at[k * n + a], gsem.at[k * n + a], sib)
                fw.start()
                sends.append(fw)
        for k, (_, pj) in enumerate(peers):
            for a in range(n):
                theirs = half(a, pj, 1 - mc)
                _remote(theirs, theirs, fsem.at[k * n + a], gsem.at[k * n + a], sib).wait_recv()
        for cp in sends:
            cp.wait_send()

    sems = pltpu.SemaphoreType.DMA((3 * n,))
    return pl.pallas_call(
        body, name=name, in_specs=[ANY] * n, out_specs=[ANY] * n,
        out_shape=[jax.ShapeDtypeStruct(b.shape, b.dtype) for b in bufs],
        input_output_aliases={a: a for a in range(n)},
        scratch_shapes=[sems, sems, sems, sems])(*bufs)


def _chip_scatter(gs, name):
    n = len(gs)

    def body(*refs):
        ins, outs = refs[:n], refs[n:2 * n]
        ssem, rsem = refs[2 * n:]
        mx, my, mc = _me()
        j = 2 * mx + my
        peers = _plane_peers(mx, my, mc)
        sends = []
        for k, (peer, pj) in enumerate(peers):
            for a in range(n):
                cp = _remote(ins[a].at[pj], outs[a].at[j], ssem.at[k * n + a], rsem.at[k * n + a], peer)
                cp.start()
                sends.append(cp)
        for k, (peer, pj) in enumerate(peers):
            for a in range(n):
                _remote(ins[a].at[pj], outs[a].at[pj], ssem.at[k * n + a], rsem.at[k * n + a], peer).wait_recv()
        for cp in sends:
            cp.wait_send()

    return pl.pallas_call(
        body, name=name, in_specs=[ANY] * n, out_specs=[ANY] * n,
        out_shape=[jax.ShapeDtypeStruct(g.shape, g.dtype) for g in gs],
        scratch_shapes=[pltpu.SemaphoreType.DMA((3 * n,)), pltpu.SemaphoreType.DMA((3 * n,))])(*gs)


def _sibling_swap(xs, name):
    n = len(xs)

    def body(*refs):
        ins, outs = refs[:n], refs[n:2 * n]
        ssem, rsem = refs[2 * n:]
        mx, my, mc = _me()
        cps = [_remote(ins[a], outs[a], ssem.at[a], rsem.at[a], (mx, my, 1 - mc)) for a in range(n)]
        for cp in cps:
            cp.start()
        for cp in cps:
            cp.wait()

    return pl.pallas_call(
        body, name=name, in_specs=[ANY] * n, out_specs=[ANY] * n,
        out_shape=[jax.ShapeDtypeStruct(x.shape, x.dtype) for x in xs],
        scratch_shapes=[pltpu.SemaphoreType.DMA((n,)), pltpu.SemaphoreType.DMA((n,))])(*xs)


def _sibling_halves(gs, name):
    n = len(gs)

    def body(*refs):
        ins, outs = refs[:n], refs[n:2 * n]
        ssem, rsem = refs[2 * n:]
        mx, my, mc = _me()
        cps = []
        for a in range(n):
            h = gs[a].shape[1] // 2
            cps.append(_remote(ins[a].at[:, pl.ds((1 - mc) * h, h), :], outs[a], ssem.at[a], rsem.at[a],
                               (mx, my, 1 - mc)))
        for cp in cps:
            cp.start()
        for cp in cps:
            cp.wait()

    return pl.pallas_call(
        body, name=name, in_specs=[ANY] * n, out_specs=[ANY] * n,
        out_shape=[jax.ShapeDtypeStruct((g.shape[0], g.shape[1] // 2, g.shape[2]), g.dtype) for g in gs],
        scratch_shapes=[pltpu.SemaphoreType.DMA((n,)), pltpu.SemaphoreType.DMA((n,))])(*gs)


def _sibling_fill(hs, name):
    n = len(hs)

    def body(*refs):
        ins, outs = refs[:n], refs[n:2 * n]
        ssem, rsem = refs[2 * n:]
        mx, my, mc = _me()
        cps = []
        for a in range(n):
            h = hs[a].shape[0]
            cps.append(_remote(ins[a], outs[a].at[pl.ds(mc * h, h), :], ssem.at[a], rsem.at[a], (mx, my, 1 - mc)))
        for cp in cps:
            cp.start()
        for a, cp in enumerate(cps):
            h = hs[a].shape[0]
            theirs = outs[a].at[pl.ds((1 - mc) * h, h), :]
            _remote(ins[a], theirs, ssem.at[a], rsem.at[a], (mx, my, 1 - mc)).wait_recv()
            cp.wait_send()

    return pl.pallas_call(
        body, name=name, in_specs=[ANY] * n, out_specs=[ANY] * n,
        out_shape=[jax.ShapeDtypeStruct((2 * x.shape[0], x.shape[1]), x.dtype) for x in hs],
        scratch_shapes=[pltpu.SemaphoreType.DMA((n,)), pltpu.SemaphoreType.DMA((n,))])(*hs)


def _add_cast(g, sb):
    J, h, b = g.shape
    th = _row_tile(h, b * J)

    def body(g_ref, s_ref, o_ref):
        o_ref[...] = (g_ref[...].astype(F32) + s_ref[...].astype(F32)).astype(BF16)

    spec = pl.BlockSpec((J, th, b), lambda i: (0, i, 0))
    return pl.pallas_call(
        body, grid=(h // th,), name="add_planes", in_specs=[spec, spec], out_specs=spec,
        out_shape=jax.ShapeDtypeStruct((J, h, b), BF16), compiler_params=_cp("parallel"))(g, sb)


_WEIGHTS = ("c_ctx", "w_mod", "b_mod", "norm1", "w_in", "q_gain", "k_gain", "conv_w", "sg_norm", "w_s", "b_s",
            "w_a", "w_b", "w_c", "w_o", "norm2", "w_ff1", "w_ff3", "w_ff2")
_BIG = ("w_in", "w_a", "w_b", "w_c", "w_o", "w_ff1", "w_ff3", "w_ff2")
_TRANSPOSED = ("w_ff1", "w_ff3")


def _constants():
    idx = np.arange(LANES)
    e = (idx[:, None] // 64 == idx[None, :] // 64).astype(np.float32) / 64.0
    c512 = np.arange(512)
    fold = (c512[:, None] % 64 == idx[None, :]).astype(np.float32)
    c256 = np.arange(SG_W)
    gsum = (c256[:, None] // 64 == idx[None, :]).astype(np.float32)
    return jnp.asarray(e, BF16), jnp.asarray(fold, F32), jnp.asarray(gsum, F32)


def _rope_tables(n_lat, n_ctx):
    t = jnp.arange(n_lat)
    inv = ROPE_THETA ** (-jnp.arange(0, HEAD_DIM // 2, 2, dtype=F32) / (HEAD_DIM // 2))
    ar = (t // GRID_W).astype(F32)[:, None] * inv
    ac = (t % GRID_W).astype(F32)[:, None] * inv
    cos = jnp.concatenate([jnp.cos(ar), jnp.cos(ar), jnp.cos(ac), jnp.cos(ac)], axis=1)
    sin = jnp.concatenate([-jnp.sin(ar), jnp.sin(ar), -jnp.sin(ac), jnp.sin(ac)], axis=1)
    cos = jnp.concatenate([cos, jnp.ones((n_ctx, HEAD_DIM), F32)], axis=0)
    sin = jnp.concatenate([sin, jnp.zeros((n_ctx, HEAD_DIM), F32)], axis=0)
    return jnp.concatenate([cos, cos], axis=1), jnp.concatenate([sin, sin], axis=1)


def kernel(x, c, ctx, c_ctx, w_mod, b_mod, norm1, w_in, q_gain, k_gain, conv_w, sg_norm, w_s, b_s, w_a, w_b, w_c, w_o, norm2, w_ff1, w_ff3, w_ff2, loss_target, m_c_ctx, m_w_mod, m_b_mod, m_norm1, m_w_in, m_q_gain, m_k_gain, m_conv_w, m_sg_norm, m_w_s, m_b_s, m_w_a, m_w_b, m_w_c, m_w_o, m_norm2, m_w_ff1, m_w_ff3, m_w_ff2, v_c_ctx, v_w_mod, v_b_mod, v_norm1, v_w_in, v_q_gain, v_k_gain, v_conv_w, v_sg_norm, v_w_s, v_b_s, v_w_a, v_w_b, v_w_c, v_w_o, v_norm2, v_w_ff1, v_w_ff3, v_w_ff2):
    given = dict(locals())
    mx, my, mc = _me()
    chip = 2 * mx + my
    dev = 4 * mx + 2 * my + mc
    L = norm1.shape[0]
    S, Lc = x.shape[1], ctx.shape[1]
    T = S + Lc
    D = D_MODEL
    n_mod, n_in, n_ff = w_mod.shape[2], w_in.shape[2], w_ff1.shape[2]
    n_cw = conv_w.shape[2]
    e_avg, fold, gsum = _constants()
    cos_t, sin_t = _rope_tables(S, Lc)

    cw_rows = (L * 3 * n_cw) // LANES
    pad = (-(8 + cw_rows)) % 8
    buf = jnp.concatenate([c.reshape(8, LANES), conv_w.reshape(cw_rows, LANES), jnp.zeros((pad, LANES), F32)], axis=0)
    g1 = _ag8(buf, "gather_cond")
    conds = g1[:, :8].reshape(N_DEV, D)
    cw_full = jnp.stack([g1[2 * j, 8:8 + cw_rows].reshape(L, 3, n_cw) for j in range(N_CHIPS)], axis=2)
    cw_full = cw_full.reshape(L, 3, N_CHIPS * n_cw)
    cw8 = jnp.pad(cw_full, ((0, 0), (0, 5), (0, 0)))
    a_raw = jnp.concatenate([conds, c_ctx[None], jnp.zeros((7, D), F32)], axis=0)
    bsh = lax.dynamic_slice_in_dim(b_mod, chip * n_mod, n_mod, axis=1)[:, None, :]
    mod_sh = _mod_fwd(a_raw, w_mod, bsh)
    g2 = _ag8(mod_sh.reshape(-1, LANES), "gather_mod")
    mods = jnp.stack([g2[2 * j].reshape(L, 16, n_mod) for j in range(N_CHIPS)], axis=2).reshape(L, 16, N_CHIPS * n_mod)
    lat = lax.dynamic_index_in_dim(mods, dev, axis=1, keepdims=False)
    mod = jnp.stack([lat.reshape(L, 6, D), mods[:, 8].reshape(L, 6, D)], axis=1)
    mod = jnp.pad(mod, ((0, 0), (0, 0), (0, 2), (0, 0)))

    qg = jnp.tile(q_gain, (1, N_Q_HEADS))[:, None, :]
    kg = jnp.tile(k_gain, (1, N_KV_HEADS))[:, None, :]
    sgn = sg_norm[:, None, :]
    ws_b = w_s.astype(BF16)
    zero = jnp.zeros_like(ws_b)
    bd = jnp.concatenate([jnp.concatenate([ws_b, zero], axis=3), jnp.concatenate([zero, ws_b], axis=3)], axis=2)
    bdt = jnp.swapaxes(bd, 2, 3)
    bias = jnp.tile(jnp.repeat(jnp.swapaxes(b_s, 1, 2), SG_W // 4, axis=2), (1, 2, 1))

    def shard_bufs(l, names=_BIG):
        return [lax.dynamic_update_slice(lax.empty((N_CHIPS,) + given[nm].shape[1:], BF16),
                                         given[nm][l].astype(BF16)[None], (chip, 0, 0)) for nm in names]

    def unpack(bufs):
        win, wa, wb, wc, wo, w1, w3, w2 = bufs
        return win, wa, wb, wc, wo.reshape(1, D, D), w1, w3, w2

    def layer_fwd(X, l, W, nxt):
        win = W[0]
        h, ht = _norm_mod(X, norm1[l][None], mod[l], 0, 1, S, True)
        p = _mm_nn(h, win, F32, "in_proj")
        ya = _conv_fwd(p, cw8[l], S)
        q, k, v = _qkv_prep(p, cos_t, sin_t, qg[l], kg[l], e_avg)
        at, qa, *got = _flash_fwd(q.reshape(N_KV_HEADS, GROUP, T, LANES), k, v, S, nxt)
        if len(W) == 1:
            W, got = unpack([win] + got[:len(_BIG) - 1]), got[len(_BIG) - 1:]
        win, wa, wb, wc, wo, w1, w3, w2 = W
        yc = _gmlp_fwd(p, sgn[l], bd[l], bias[l])
        mg = _merge_fwd(ya, at, yc, p, wa, wb, wc)
        X1, f1 = _mm_res(mg[None], wo, X, mod[l], 2, S, "out_proj")
        h2, = _norm_mod(X1, norm2[l][None], mod[l], 3, 4, S, False)
        a1, a3, act = _ffn_up(h2, w1, w3)
        X2, f2 = _mm_res(act, w2, X1, mod[l], 5, S, "ffn_down")
        return X2, W, got, dict(X=X, ht=ht, h2=h2, p=p, ya=ya, k=k, v=v, at=at, qa=qa, yc=yc, mg=mg, X1=X1, f1=f1,
                             a1=a1, a3=a3, act=act, f2=f2)

    def layer_bwd(dX2, dyf, l, W, sv, pending):
        win, wa, wb, wc, wo, w1, w3, w2 = W
        da1, da3 = _ffn_down_bwd(dyf, w2, sv["a1"], sv["a3"])
        dw2 = _mm_tn(sv["act"], dyf, _shard_rows(n_ff), _rows(D), N_CHIPS, n_ff, D, T, "dw_ff2")
        dh2 = _mm_nt_acc([da1, da3], [w1, w3], False, "ffn_up_bwd")
        dw1 = _mm_tn(da1, sv["h2"], _shard_rows(n_ff), _rows(D), N_CHIPS, n_ff, D, T, "dw_ff1")
        dw3 = _mm_tn(da3, sv["h2"], _shard_rows(n_ff), _rows(D), N_CHIPS, n_ff, D, T, "dw_ff3")
        dX1, dn2, dsh2, dsc2, dyo, dgt1 = _norm_mod_bwd(sv["X1"], dh2, dX2, norm2[l][None], mod[l], 4, S,
                                                        (sv["f1"], mod[l], 2))
        dwo = _mm_tn(sv["mg"], dyo, _rows(D), _rows(D), 1, D, D, T, "dw_o")
        dp, dya, doa, dyc, dwa, dwb, dwc = _merge_bwd(dyo, sv["ya"], sv["at"], sv["yc"], sv["p"], wa, wb, wc, wo[0])
        dp, dcw = _conv_bwd(dp, dya, sv["p"], cw8[l], S)
        dp, dsg, dws, dbs = _gmlp_bwd(dp, dyc, sv["p"], sgn[l], bd[l], bdt[l], bias[l], gsum)
        early = [dwa.astype(BF16), dwb.astype(BF16), dwc.astype(BF16), dwo.reshape(N_CHIPS, D // N_CHIPS, D),
                 dw1, dw3, dw2]
        dq, dk, dv, *recv = _flash_bwd(sv["qa"], doa.reshape(N_KV_HEADS, GROUP, T, LANES), sv["k"], sv["v"], S,
                                       list(pending) + (early if l == 0 else []))
        dp, dqg, dkg = _qkv_prep_bwd(dp, dq, dk, dv, sv["p"], cos_t, sin_t,
                                     qg[l], kg[l], e_avg, fold)
        dh = _mm_nt_acc([dp], [win], True, "in_proj_bwd")
        dwin = _mm_dw(sv["ht"], dp, _row_cols(n_in), N_CHIPS, n_in, "dw_in")
        below = (saved[l - 1]["f2"], mod[l - 1], 5) if l else None
        dX0, dn1, dsh1, dsc1, *nxt = _norm_mod_bwd(sv["X"], dh, dX1, norm1[l][None], mod[l], 1, S, below)
        dmod = [dsh1, dsc1, dgt1, dsh2, dsc2]
        big = [dwin] + early
        small = dict(norm1=dn1[0], norm2=dn2[0], q_gain=dqg[0, :HEAD_DIM], k_gain=dkg[0, :HEAD_DIM],
                     conv_w=dcw[:3], sg_norm=dsg[0], w_s=dws, b_s=jnp.swapaxes(dbs[:, :4], 0, 1), dmod=dmod)
        return dX0, nxt, big, small, recv

    X = jnp.concatenate([x[0], ctx[0]], axis=0)
    Ws, saved = [_chip_gather(shard_bufs(0, _BIG[:1]), "gather_weights")], []
    for l in range(L):
        nxt = (shard_bufs(0, _BIG[1:]) if l == 0 else []) + (shard_bufs(l + 1) if l + 1 < L else [])
        X, Ws[l], got, sv = layer_fwd(X, l, Ws[l], nxt)
        if got:
            Ws.append(unpack(got))
        saved.append(sv)
    dX, lpart = _loss_grad(X, loss_target[0], S)
    loss = lax.psum(lpart[0, 0], ("x", "y", "c"))

    out = {nm: () for nm in _BIG}
    smalls = [None] * L

    def own_block(r, g):
        return lax.dynamic_update_slice(r, lax.dynamic_slice_in_dim(g, chip, 1, axis=0), (chip, 0, 0))

    def stored(nm, a):
        return jnp.swapaxes(a, 1, 2) if nm in _TRANSPOSED else a

    def update(l, names, grads):
        for nm, g in zip(names, grads):
            out[nm] = _adamw_layer(stored(nm, given[nm]), stored(nm, given["m_" + nm]), stored(nm, given["v_" + nm]),
                                   g, l, out[nm])

    def plane_update(l, names, recv, sent):
        mine = [_sum_lead(own_block(r, g), "sum_chips") for r, g in zip(recv, sent)]
        update(l, names, zip(mine, _sibling_swap(mine, "swap_planes")))

    pending = []
    dyf, dgt2 = _gate_bwd(dX, saved[L - 1]["f2"], mod[L - 1], 5, S)
    for l in reversed(range(L)):
        dX, nxt, big, smalls[l], recv = layer_bwd(dX, dyf, l, Ws[l], saved[l], pending)
        smalls[l]["dmod"] = jnp.concatenate(smalls[l]["dmod"] + [dgt2], axis=1)
        if nxt:
            dyf, dgt2 = nxt
        if recv:
            plane_update(l + 1, _BIG, recv[:len(pending)], pending)
            if l == 0:
                plane_update(0, _BIG[1:], recv[len(pending):], big[1:])
        pending = big
    last = big[:1]
    sib = _sibling_halves(last, "swap_halves")
    own = [lax.dynamic_slice_in_dim(g, mc * (g.shape[1] // 2), g.shape[1] // 2, axis=1) for g in last]
    sent = [_add_cast(g, s_) for g, s_ in zip(own, sib)]
    recv = [own_block(r, g) for r, g in zip(_chip_scatter(sent, "scatter_grads"), sent)]
    halves = [_sum_lead(r, "sum_chips") for r in recv]
    full = _sibling_fill(halves, "fill_halves")
    update(0, _BIG[:1], [(lax.dynamic_update_slice(f, hv, (mc * hv.shape[0], 0)),) for f, hv in zip(full, halves)])
    grad_x = dX[:S][None]

    def flat(nm):
        return jnp.stack([smalls[l][nm] for l in range(L)]).reshape(-1)

    dmod_all = jnp.stack([smalls[l]["dmod"] for l in range(L)])
    dml = dmod_all[:, 0].reshape(-1)
    dmc = dmod_all[:, 1].reshape(-1)
    names = ("norm1", "q_gain", "k_gain", "conv_w", "sg_norm", "w_s", "b_s", "norm2")
    parts = [dml, dml + dmc, dmc] + [flat(nm) for nm in names]
    sizes = [int(a.shape[0]) for a in parts]
    total = sum(sizes)
    padn = (-total) % (8 * LANES)
    sbuf = jnp.concatenate(parts + [jnp.zeros((padn,), F32)]).reshape(-1, LANES)
    g3 = _ag8(sbuf, "gather_small")
    ssum = _sum_lead(g3, "sum_devices").reshape(-1)
    offs = np.cumsum([0] + sizes)
    seg = {nm: ssum[offs[i + 3]:offs[i + 4]] for i, nm in enumerate(names)}
    gb_mod = ssum[offs[1]:offs[2]].reshape(L, N_CHIPS * n_mod)
    dmc_sum = ssum[offs[2]:offs[3]].reshape(L, N_CHIPS * n_mod)
    dml_all = g3.reshape(N_DEV, -1)[:, :sizes[0]].reshape(N_DEV, L, N_CHIPS * n_mod)
    dml_sh = jnp.swapaxes(lax.dynamic_slice_in_dim(dml_all, chip * n_mod, n_mod, axis=2), 0, 1)
    dmc_sh = lax.dynamic_slice_in_dim(dmc_sum, chip * n_mod, n_mod, axis=1)[:, None, :]
    dms = jnp.concatenate([dml_sh, dmc_sh, jnp.zeros((L, 7, n_mod), F32)], axis=1)
    g_wmod = _wmod_grad(a_raw, dms)
    part = _cctx_partial(jnp.concatenate([dmc_sh, jnp.zeros((L, 15, n_mod), F32)], axis=1), w_mod)
    g4 = _ag8(part.reshape(-1, LANES), "gather_cctx")
    g_cctx = _cctx_final(g4, c_ctx.reshape(8, LANES)).reshape(D)

    g_conv = lax.dynamic_slice_in_dim(seg["conv_w"].reshape(L, 3, N_CHIPS * n_cw), chip * n_cw, n_cw, axis=2)
    small_g = dict(c_ctx=g_cctx, w_mod=g_wmod, b_mod=gb_mod, norm1=seg["norm1"].reshape(norm1.shape),
                   q_gain=seg["q_gain"].reshape(q_gain.shape), k_gain=seg["k_gain"].reshape(k_gain.shape),
                   conv_w=g_conv, sg_norm=seg["sg_norm"].reshape(sg_norm.shape), w_s=seg["w_s"].reshape(w_s.shape),
                   b_s=seg["b_s"].reshape(b_s.shape), norm2=seg["norm2"].reshape(norm2.shape))
    res = {}
    for nm in _WEIGHTS:
        if nm in _BIG:
            res[nm] = [stored(nm, o) for o in out[nm]]
        else:
            res[nm] = _adamw(given[nm], given["m_" + nm], given["v_" + nm], small_g[nm])
    return (loss, grad_x, *[res[nm][0] for nm in _WEIGHTS], *[res[nm][1] for nm in _WEIGHTS],
            *[res[nm][2] for nm in _WEIGHTS], *[res[nm][3] for nm in _WEIGHTS])
```

```python
import functools

import jax
import jax.numpy as jnp
import numpy as np
from jax import lax
from jax.experimental import pallas as pl
from jax.experimental.pallas import tpu as pltpu

F32 = jnp.float32
BF16 = jnp.bfloat16
EPS = 1e-6
LOG2E = 1.4426950408889634
D_MODEL = 1024
HEAD_DIM = 64
N_Q_HEADS = 8
N_KV_HEADS = 2
GROUP = N_Q_HEADS // N_KV_HEADS
GRID_W = 64
ROPE_THETA = 10000.0
CHUNK = 128
CONV_W = 256
SG_W = 256
OFF_Q = 3 * CONV_W
QKV_W = 768
OFF_U = OFF_Q + QKV_W
OFF_G = OFF_U + 2 * SG_W
IN_W = OFF_G + 3 * D_MODEL
N_CHIPS = 4
N_DEV = 8
LANES = 128
FWD_KEYS = 256
UNROLL_FWD = 16
UNROLL_BWD = 16
AUG = 3
ADAM_LR, ADAM_B1, ADAM_B2, ADAM_EPS, ADAM_WD, ADAM_STEP = 0.001, 0.9, 0.999, 1e-8, 0.01, 10
VMEM_LIMIT_V7X = 52 * 1024 * 1024
MXU_DEPTH_V7X = 256
MESH_ID = pl.DeviceIdType.MESH
NT = (((1,), (1,)), ((), ()))
TN = (((0,), (0,)), ((), ()))
ANY = pl.BlockSpec(memory_space=pl.ANY)


def _cp(*sem):
    return pltpu.CompilerParams(dimension_semantics=sem or None, vmem_limit_bytes=VMEM_LIMIT_V7X)


def _tile(n, target, mult=16):
    best = None
    for t in range(mult, n + 1, mult):
        if n % t == 0 and t <= target:
            best = t
    assert best is not None, (n, target, mult)
    return best


def _full(shape):
    nd = len(shape)
    return pl.BlockSpec(tuple(shape), lambda *_: (0,) * nd)


def _segments(i, tm, n_lat, fn):
    k, off = divmod(n_lat, tm)

    @pl.when(i < k)
    def _():
        fn(0, tm, 0)

    @pl.when(i == k)
    def _():
        if off:
            fn(0, off, 0)
        fn(off, tm, 1)

    @pl.when(i > k)
    def _():
        fn(0, tm, 1)


def _dot(a, b):
    return jnp.dot(a, b, preferred_element_type=F32)


def _dg(a, b, dims):
    return lax.dot_general(a, b, dims, preferred_element_type=F32)


def _split3(x):
    hi = x.astype(BF16)
    r1 = x - hi.astype(F32)
    mid = r1.astype(BF16)
    lo = (r1 - mid.astype(F32)).astype(BF16)
    return hi.astype(F32), mid.astype(F32), lo.astype(F32)


def _lane(shape):
    return lax.broadcasted_iota(jnp.int32, shape, len(shape) - 1)


def _aug(val, stat):
    lane = _lane(val.shape)
    hi, mid, lo = _split3(stat)
    ext = jnp.where(lane == 64, hi, jnp.where(lane == 65, mid, jnp.where(lane == 66, lo, 0.0)))
    return jnp.where(lane < 64, val, ext)


def _seg_mean(x, e):
    outs = []
    for g in range(x.shape[1] // LANES):
        blk = x[:, g * LANES:(g + 1) * LANES]
        hi = blk.astype(BF16)
        lo = (blk - hi.astype(F32)).astype(BF16)
        outs.append(_dot(hi, e) + _dot(lo, e))
    return outs[0] if len(outs) == 1 else jnp.concatenate(outs, axis=1)


def _rope(x, cos, sin_signed, inverse):
    w = x.shape[1]
    reps = w // LANES
    c = cos if reps == 1 else jnp.tile(cos, (1, reps))
    s = sin_signed if reps == 1 else jnp.tile(sin_signed, (1, reps))
    first = (_lane(x.shape) % 32) < 16
    partner = jnp.where(first, pltpu.roll(x, w - 16, 1), pltpu.roll(x, 16, 1))
    return x * c - partner * s if inverse else x * c + partner * s


def _sigmoid(x):
    return 1.0 / (1.0 + jnp.exp(-x))


_GELU_K = 0.7978845608028654
_GELU_C = 0.044715


def _gelu(x):
    return 0.5 * x * (1.0 + jnp.tanh(_GELU_K * (x + _GELU_C * x * x * x)))


def _gelu_grad(x):
    t = jnp.tanh(_GELU_K * (x + _GELU_C * x * x * x))
    return 0.5 * (1.0 + t) + 0.5 * x * (1.0 - t * t) * _GELU_K * (1.0 + 3.0 * _GELU_C * x * x)


def _loop_unrolled(n, step, init, unroll):
    def trip(t, carry):
        for u in range(unroll):
            carry = step(t * unroll + u, carry)
        return carry

    carry = lax.fori_loop(0, n // unroll, trip, init) if n >= unroll else init
    for r in range(n - n % unroll, n):
        carry = step(r, carry)
    return carry


def _heads_to_rows(x, n_heads):
    out = []
    for h in range(n_heads):
        grp = x[:, (h // 2) * LANES:(h // 2 + 1) * LANES]
        out.append(grp if h % 2 == 0 else pltpu.roll(grp, 64, 1))
    return out


def _rows_to_heads(blocks):
    outs = []
    lane = _lane(blocks[0].shape)
    for a in range(len(blocks) // 2):
        outs.append(jnp.where(lane < 64, blocks[2 * a], pltpu.roll(blocks[2 * a + 1], 64, 1)))
    return outs[0] if len(outs) == 1 else jnp.concatenate(outs, axis=1)


def _norm_mod(x, g, mod, i_shift, i_scale, n_lat, transposed):
    T, D = x.shape
    tm = _tile(T, 768, LANES)

    def body(x_ref, g_ref, mod_ref, h_ref, *ht_ref):
        def fn(r0, r1, seg):
            xv = x_ref[r0:r1, :]
            r = lax.rsqrt(jnp.mean(xv * xv, axis=-1, keepdims=True) + EPS)
            n = xv * r * g_ref[...]
            h = n * (1.0 + mod_ref[seg, i_scale:i_scale + 1, :]) + mod_ref[seg, i_shift:i_shift + 1, :]
            h_ref[r0:r1, :] = h.astype(BF16)

        _segments(pl.program_id(0), tm, n_lat, fn)
        if transposed:
            ht_ref[0][...] = h_ref[...].astype(F32).T.astype(BF16)

    return pl.pallas_call(
        body, grid=(T // tm,), name="norm_mod",
        in_specs=[pl.BlockSpec((tm, D), lambda i: (i, 0)), _full(g.shape), _full(mod.shape)],
        out_specs=[pl.BlockSpec((tm, D), lambda i: (i, 0))] + [pl.BlockSpec((D, tm), lambda i: (0, i))] * transposed,
        out_shape=[jax.ShapeDtypeStruct((T, D), BF16)] + [jax.ShapeDtypeStruct((D, T), BF16)] * transposed,
        compiler_params=_cp("parallel"))(x, g, mod)


def _norm_mod_bwd(x, dh, dres, g, mod, i_scale, n_lat, gate=None):
    T, D = x.shape
    tm = _tile(T, 528)

    def body(x_ref, dh_ref, dres_ref, g_ref, mod_ref, *rest):
        if gate is None:
            dx_ref, dg_ref, dsh_ref, dsc_ref = rest
        else:
            f_ref, gmod_ref, dx_ref, dg_ref, dsh_ref, dsc_ref, dy_ref, dgt_ref = rest
        i = pl.program_id(0)

        @pl.when(i == 0)
        def _():
            dg_ref[...] = jnp.zeros_like(dg_ref)
            dsh_ref[...] = jnp.zeros_like(dsh_ref)
            dsc_ref[...] = jnp.zeros_like(dsc_ref)
            if gate is not None:
                dgt_ref[...] = jnp.zeros_like(dgt_ref)

        def fn(r0, r1, seg):
            xv = x_ref[r0:r1, :]
            dh = dh_ref[r0:r1, :]
            r = lax.rsqrt(jnp.mean(xv * xv, axis=-1, keepdims=True) + EPS)
            xh = xv * r
            gv = g_ref[...]
            dsh_ref[seg] += jnp.sum(dh, axis=0, keepdims=True)
            dsc_ref[seg] += jnp.sum(dh * (xh * gv), axis=0, keepdims=True)
            dn = dh * (1.0 + mod_ref[seg, i_scale:i_scale + 1, :])
            dg_ref[...] += jnp.sum(dn * xh, axis=0, keepdims=True)
            gd = gv * dn
            dxv = dres_ref[r0:r1, :] + r * (gd - xh * jnp.mean(xh * gd, axis=-1, keepdims=True))
            dx_ref[r0:r1, :] = dxv
            if gate is not None:
                dy_ref[r0:r1, :] = (dxv * gmod_ref[seg, gate[2]:gate[2] + 1, :]).astype(BF16)
                dgt_ref[seg] += jnp.sum(dxv * f_ref[r0:r1, :].astype(F32), axis=0, keepdims=True)

        _segments(i, tm, n_lat, fn)

    row = pl.BlockSpec((tm, D), lambda i: (i, 0))
    extra_in = [] if gate is None else [gate[0], gate[1]]
    return pl.pallas_call(
        body, grid=(T // tm,), name="norm_mod_bwd" if gate is None else "norm_gate_bwd",
        in_specs=[row, row, row, _full(g.shape), _full(mod.shape)] + ([] if gate is None else [row, _full(gate[1].shape)]),
        out_specs=[row, _full((1, D)), _full((2, 1, D)), _full((2, 1, D))] + ([] if gate is None else [row, _full((2, 1, D))]),
        out_shape=[jax.ShapeDtypeStruct((T, D), F32), jax.ShapeDtypeStruct((1, D), F32),
                   jax.ShapeDtypeStruct((2, 1, D), F32), jax.ShapeDtypeStruct((2, 1, D), F32)]
        + ([] if gate is None else [jax.ShapeDtypeStruct((T, D), BF16), jax.ShapeDtypeStruct((2, 1, D), F32)]),
        compiler_params=_cp("arbitrary"))(x, dh, dres, g, mod, *extra_in)


def _gate_bwd(dx, f, mod, i_gate, n_lat):
    T, D = dx.shape
    tm = _tile(T, 528)

    def body(dx_ref, f_ref, mod_ref, dy_ref, dg_ref):
        i = pl.program_id(0)

        @pl.when(i == 0)
        def _():
            dg_ref[...] = jnp.zeros_like(dg_ref)

        def fn(r0, r1, seg):
            dxv = dx_ref[r0:r1, :]
            dy_ref[r0:r1, :] = (dxv * mod_ref[seg, i_gate:i_gate + 1, :]).astype(BF16)
            dg_ref[seg] += jnp.sum(dxv * f_ref[r0:r1, :].astype(F32), axis=0, keepdims=True)

        _segments(i, tm, n_lat, fn)

    row = pl.BlockSpec((tm, D), lambda i: (i, 0))
    return pl.pallas_call(
        body, grid=(T // tm,), name="gate_bwd",
        in_specs=[row, row, _full(mod.shape)], out_specs=[row, _full((2, 1, D))],
        out_shape=[jax.ShapeDtypeStruct((T, D), BF16), jax.ShapeDtypeStruct((2, 1, D), F32)],
        compiler_params=_cp("arbitrary"))(dx, f, mod)


def _mm_nn(a, w, out_dtype, name):
    M, K = a.shape
    J, _, n = w.shape
    tm = _tile(M, 1056)

    def body(a_ref, w_ref, o_ref):
        o_ref[...] = _dot(a_ref[...], w_ref[...]).astype(o_ref.dtype)

    return pl.pallas_call(
        body, grid=(M // tm, J), name=name,
        in_specs=[pl.BlockSpec((tm, K), lambda i, j: (i, 0)), pl.BlockSpec((None, K, n), lambda i, j: (j, 0, 0))],
        out_specs=pl.BlockSpec((tm, n), lambda i, j: (i, j)),
        out_shape=jax.ShapeDtypeStruct((M, J * n), out_dtype), compiler_params=_cp("parallel", "arbitrary"))(a, w)


def _mm_res(a3, w, res, mod, i_gate, n_lat, name):
    J, M, k = a3.shape
    N = w.shape[2]
    tm = _tile(M, 528)

    def body(a_ref, w_ref, res_ref, mod_ref, x_ref, f_ref):
        acc = _dot(a_ref[0], w_ref[0])
        for j in range(1, J):
            acc += _dot(a_ref[j], w_ref[j])
        f_ref[...] = acc.astype(BF16)

        def fn(r0, r1, seg):
            x_ref[r0:r1, :] = res_ref[r0:r1, :] + mod_ref[seg, i_gate:i_gate + 1, :] * acc[r0:r1, :]

        _segments(pl.program_id(0), tm, n_lat, fn)

    row = pl.BlockSpec((tm, N), lambda i: (i, 0))
    return pl.pallas_call(
        body, grid=(M // tm,), name=name,
        in_specs=[pl.BlockSpec((J, tm, k), lambda i: (0, i, 0)), _full(w.shape), row, _full(mod.shape)],
        out_specs=[row, row],
        out_shape=[jax.ShapeDtypeStruct((M, N), F32), jax.ShapeDtypeStruct((M, N), BF16)],
        compiler_params=_cp("parallel"))(a3, w, res, mod)


def _mm_nt_acc(dys, ws, row_major, name):
    J, K, n = ws[0].shape
    M = dys[0].shape[0] if row_major else dys[0].shape[1]
    tm = _tile(M, 1056)
    P = len(dys)

    def body(*refs):
        o_ref = refs[2 * P]
        j = pl.program_id(1)
        part = _dg(refs[0][...], refs[P][...], NT)
        for p in range(1, P):
            part += _dg(refs[p][...], refs[P + p][...], NT)

        @pl.when(j == 0)
        def _():
            o_ref[...] = part

        @pl.when(j > 0)
        def _():
            o_ref[...] += part

    dy_spec = (pl.BlockSpec((tm, n), lambda i, j: (i, j)) if row_major
               else pl.BlockSpec((None, tm, n), lambda i, j: (j, i, 0)))
    w_spec = pl.BlockSpec((None, K, n), lambda i, j: (j, 0, 0))
    return pl.pallas_call(
        body, grid=(M // tm, J), name=name,
        in_specs=[dy_spec] * P + [w_spec] * P,
        out_specs=pl.BlockSpec((tm, K), lambda i, j: (i, 0)),
        out_shape=jax.ShapeDtypeStruct((M, K), F32), compiler_params=_cp("parallel", "arbitrary"))(*dys, *ws)


def _mm_tn(x, dy, x_spec, dy_spec, J, K, n, T, name):
    tk = _tile(T, 1056, MXU_DEPTH_V7X)
    nt = T // tk

    def body(x_ref, dy_ref, o_ref, acc):
        t = pl.program_id(1)
        part = _dg(x_ref[...], dy_ref[...], TN)

        @pl.when(t == 0)
        def _():
            acc[...] = part

        @pl.when(t > 0)
        def _():
            acc[...] += part

        @pl.when(t == nt - 1)
        def _():
            o_ref[...] = acc[...].astype(BF16)

    return pl.pallas_call(
        body, grid=(J, nt), name=name,
        in_specs=[x_spec(tk), dy_spec(tk)],
        out_specs=pl.BlockSpec((None, K, n), lambda j, t: (j, 0, 0)),
        out_shape=jax.ShapeDtypeStruct((J, K, n), BF16), scratch_shapes=[pltpu.VMEM((K, n), F32)],
        compiler_params=_cp("parallel", "arbitrary"))(x, dy)


def _mm_dw(xt, dy, dy_spec, J, n, name):
    K, T = xt.shape
    tk = _tile(T, 1056, MXU_DEPTH_V7X)
    nt = T // tk

    def body(xt_ref, dy_ref, o_ref, acc):
        t = pl.program_id(1)
        part = _dot(xt_ref[...], dy_ref[...])

        @pl.when(t == 0)
        def _():
            acc[...] = part

        @pl.when(t > 0)
        def _():
            acc[...] += part

        @pl.when(t == nt - 1)
        def _():
            o_ref[...] = acc[...].astype(BF16)

    return pl.pallas_call(
        body, grid=(J, nt), name=name,
        in_specs=[pl.BlockSpec((K, tk), lambda j, t: (0, t)), dy_spec(tk)],
        out_specs=pl.BlockSpec((None, K, n), lambda j, t: (j, 0, 0)),
        out_shape=jax.ShapeDtypeStruct((J, K, n), BF16), scratch_shapes=[pltpu.VMEM((K, n), F32)],
        compiler_params=_cp("parallel", "arbitrary"))(xt, dy)


def _rows(width):
    return lambda tk: pl.BlockSpec((tk, width), lambda j, t: (t, 0))


def _row_cols(width):
    return lambda tk: pl.BlockSpec((tk, width), lambda j, t: (t, j))


def _shard_rows(width):
    return lambda tk: pl.BlockSpec((None, tk, width), lambda j, t: (j, t, 0))


def _ffn_up(h, w1, w3):
    T, D = h.shape
    J, _, n = w1.shape
    tm = _tile(T, 1056)

    def body(h_ref, w1_ref, w3_ref, a1_ref, a3_ref, act_ref):
        hv = h_ref[...]
        a1 = _dot(hv, w1_ref[...])
        a3 = _dot(hv, w3_ref[...])
        a1_ref[...] = a1.astype(BF16)
        a3_ref[...] = a3.astype(BF16)
        act_ref[...] = (a1 * _sigmoid(a1) * a3).astype(BF16)

    w_spec = pl.BlockSpec((None, D, n), lambda i, j: (j, 0, 0))
    o_spec = pl.BlockSpec((None, tm, n), lambda i, j: (j, i, 0))
    return pl.pallas_call(
        body, grid=(T // tm, J), name="ffn_up",
        in_specs=[pl.BlockSpec((tm, D), lambda i, j: (i, 0)), w_spec, w_spec], out_specs=[o_spec] * 3,
        out_shape=[jax.ShapeDtypeStruct((J, T, n), BF16)] * 3,
        compiler_params=_cp("parallel", "arbitrary"))(h, w1, w3)


def _ffn_down_bwd(dy, w2, a1, a3):
    T, D = dy.shape
    J, n, _ = w2.shape
    tm = _tile(T, 1056)

    def body(dy_ref, w2_ref, a1_ref, a3_ref, da1_ref, da3_ref):
        dact = _dg(dy_ref[...], w2_ref[...], NT)
        a1v = a1_ref[...].astype(F32)
        sig = _sigmoid(a1v)
        da3_ref[...] = (dact * a1v * sig).astype(BF16)
        da1_ref[...] = (dact * a3_ref[...].astype(F32) * (sig * (1.0 + a1v * (1.0 - sig)))).astype(BF16)

    a_spec = pl.BlockSpec((None, tm, n), lambda i, j: (j, i, 0))
    return pl.pallas_call(
        body, grid=(T // tm, J), name="ffn_down_bwd",
        in_specs=[pl.BlockSpec((tm, D), lambda i, j: (i, 0)), pl.BlockSpec((None, n, D), lambda i, j: (j, 0, 0)),
                  a_spec, a_spec],
        out_specs=[a_spec, a_spec], out_shape=[jax.ShapeDtypeStruct((J, T, n), BF16)] * 2,
        compiler_params=_cp("parallel", "arbitrary"))(dy, w2, a1, a3)


def _qkv_prep(p, cos, sin, qg, kg, e):
    T = p.shape[0]
    tm = _tile(T, 528)

    def body(p_ref, cos_ref, sin_ref, qg_ref, kg_ref, e_ref, q_ref, k_ref, v_ref):
        ev = e_ref[...]
        cv, sv = cos_ref[...], sin_ref[...]
        xq = p_ref[:, 0:512]
        qn = xq * lax.rsqrt(_seg_mean(xq * xq, ev) + EPS) * qg_ref[...]
        qr = _rope(qn, cv, sv, False) * (HEAD_DIM ** -0.5 * LOG2E)
        xk = p_ref[:, 512:640]
        kn = xk * lax.rsqrt(_seg_mean(xk * xk, ev) + EPS) * kg_ref[...]
        kr = _rope(kn, cv, sv, False)
        lane = _lane((tm, LANES))
        ones = jnp.where(lane < 64 + AUG, -1.0, 0.0)
        for h, blk in enumerate(_heads_to_rows(qr, N_Q_HEADS)):
            q_ref[h] = jnp.where(lane < 64, blk, 0.0).astype(BF16)
        for h, blk in enumerate(_heads_to_rows(kr, N_KV_HEADS)):
            k_ref[h] = jnp.where(lane < 64, blk, ones).astype(BF16)
        for h, blk in enumerate(_heads_to_rows(p_ref[:, 640:768], N_KV_HEADS)):
            v_ref[h] = jnp.where(lane < 64, blk, ones).astype(BF16)

    tab = pl.BlockSpec((tm, LANES), lambda i: (i, 0))
    return pl.pallas_call(
        body, grid=(T // tm,), name="qkv_prep",
        in_specs=[pl.BlockSpec((tm, QKV_W), lambda i: (i, 1)), tab, tab, _full(qg.shape), _full(kg.shape),
                  _full(e.shape)],
        out_specs=[pl.BlockSpec((N_Q_HEADS, tm, LANES), lambda i: (0, i, 0)),
                   pl.BlockSpec((N_KV_HEADS, tm, LANES), lambda i: (0, i, 0)),
                   pl.BlockSpec((N_KV_HEADS, tm, LANES), lambda i: (0, i, 0))],
        out_shape=[jax.ShapeDtypeStruct((N_Q_HEADS, T, LANES), BF16),
                   jax.ShapeDtypeStruct((N_KV_HEADS, T, LANES), BF16),
                   jax.ShapeDtypeStruct((N_KV_HEADS, T, LANES), BF16)],
        compiler_params=_cp("parallel"))(p, cos, sin, qg, kg, e)


def _qkv_prep_bwd(dp, dq, dk, dv, p, cos, sin, qg, kg, e, fold):
    T = p.shape[0]
    tq = dq.shape[3] // GROUP
    tm = _tile(T, 768, tq)
    nt = T // tm

    def body(dp_in, dq_ref, dk_ref, dv_ref, p_ref, cos_ref, sin_ref, qg_ref, kg_ref, e_ref, fold_ref,
             dp_ref, dqg_ref, dkg_ref, accq, acck):
        del dp_in
        i = pl.program_id(0)

        @pl.when(i == 0)
        def _():
            accq[...] = jnp.zeros_like(accq)
            acck[...] = jnp.zeros_like(acck)

        ev = e_ref[...]
        cv, sv = cos_ref[...], sin_ref[...]

        def one(x, dr, gain, acc):
            r = lax.rsqrt(_seg_mean(x * x, ev) + EPS)
            xh = x * r
            dn = _rope(dr, cv, sv, True)
            acc[0:1, :] += jnp.sum(dn * xh, axis=0, keepdims=True)
            gd = gain * dn
            return r * (gd - xh * _seg_mean(xh * gd, ev))

        slabs = [[dq_ref[h, b].T for b in range(tm // tq)] for h in range(N_KV_HEADS)]
        heads = [jnp.concatenate([sl[g * tq:(g + 1) * tq] for sl in slabs[h]], axis=0)
                 for h in range(N_KV_HEADS) for g in range(GROUP)]
        dqr = _rows_to_heads(heads) * (HEAD_DIM ** -0.5)
        dkr = _rows_to_heads([dk_ref[h] for h in range(N_KV_HEADS)]) * (1.0 / LOG2E)
        dvv = _rows_to_heads([dv_ref[h] for h in range(N_KV_HEADS)])
        dp_ref[:, 0:512] = one(p_ref[:, 0:512], dqr, qg_ref[...], accq).astype(BF16)
        dp_ref[:, 512:640] = one(p_ref[:, 512:640], dkr, kg_ref[...], acck).astype(BF16)
        dp_ref[:, 640:768] = dvv.astype(BF16)

        @pl.when(i == nt - 1)
        def _():
            fv = fold_ref[...]
            dqg_ref[...] = jnp.dot(accq[...], fv, preferred_element_type=F32, precision=lax.Precision.HIGHEST)
            dkg_ref[...] = jnp.dot(acck[...], fv[0:LANES, :], preferred_element_type=F32,
                                   precision=lax.Precision.HIGHEST)

    tab = pl.BlockSpec((tm, LANES), lambda i: (i, 0))
    sec = pl.BlockSpec((tm, QKV_W), lambda i: (i, 1))
    return pl.pallas_call(
        body, grid=(nt,), name="qkv_prep_bwd",
        in_specs=[ANY, pl.BlockSpec((N_KV_HEADS, tm // tq, LANES, GROUP * tq), lambda i: (0, i, 0, 0)),
                  pl.BlockSpec((N_KV_HEADS, tm, LANES), lambda i: (0, i, 0)),
                  pl.BlockSpec((N_KV_HEADS, tm, LANES), lambda i: (0, i, 0)),
                  sec, tab, tab, _full(qg.shape), _full(kg.shape), _full(e.shape), _full(fold.shape)],
        out_specs=[sec, _full((8, LANES)), _full((8, LANES))],
        out_shape=[jax.ShapeDtypeStruct(dp.shape, BF16), jax.ShapeDtypeStruct((8, LANES), F32),
                   jax.ShapeDtypeStruct((8, LANES), F32)],
        scratch_shapes=[pltpu.VMEM((8, 512), F32), pltpu.VMEM((8, LANES), F32)],
        input_output_aliases={0: 0}, compiler_params=_cp("arbitrary"))(dp, dq, dk, dv, p, cos, sin, qg, kg, e, fold)


def _flash_fwd(q, k, v, n_lat, gather=()):
    _, _, T, _ = q.shape
    tq = tk = 256
    nq = T // tq
    M = GROUP * tq

    wide_k = FWD_KEYS if n_lat % FWD_KEYS == 0 else tk
    n_g = len(gather)

    def body(q_ref, k_ref, v_ref, *rest):
        o_ref, qa_ref = rest[n_g], rest[n_g + 1]
        i = pl.program_id(0)
        if n_g:
            bufs = rest[n_g + 2:2 * n_g + 2]
            start, wait = _plane_exchange(bufs, bufs, rest[-2], rest[-1], False)
            pl.when((i == 0) & (pl.program_id(1) == 0))(start)
        qv = q_ref[...].reshape(M, LANES)

        def step(r0, width, carry):
            m, acc = carry
            sc = _dg(qv, k_ref[pl.ds(r0, width), :], NT)
            m_new = jnp.maximum(m, jnp.max(sc, axis=1, keepdims=True))
            pr = jnp.exp2(sc - m_new)
            acc = jnp.exp2(m - m_new) * acc + _dot(pr.astype(BF16), v_ref[pl.ds(r0, width), :])
            return m_new, acc

        def wide(s, carry):
            return step(s * wide_k if isinstance(s, int) else pl.multiple_of(s * wide_k, wide_k), wide_k, carry)

        def finish(m, acc):
            den = -acc[:, 64:65]
            out = acc / den
            o_ref[...] = _rows_to_heads([out[g * tq:(g + 1) * tq] for g in range(GROUP)]).astype(BF16)
            qa_ref[...] = _aug(qv.astype(F32), m + jnp.log2(den)).astype(BF16).reshape(GROUP, tq, LANES)

        init = (jnp.full((M, 1), -1e30, F32), jnp.zeros((M, LANES), F32))

        @pl.when(i < n_lat // tq)
        def _():
            carry = _loop_unrolled(n_lat // wide_k, wide, init, UNROLL_FWD)
            for r0 in range(n_lat, T, tk):
                carry = step(r0, tk, carry)
            finish(*carry)

        @pl.when(i >= n_lat // tq)
        def _():
            carry = init
            for r0 in range(n_lat, T, tk):
                carry = step(r0, tk, carry)
            finish(*carry)

        if n_g:
            pl.when((i == nq - 1) & (pl.program_id(1) == N_KV_HEADS - 1))(wait)

    q_spec = pl.BlockSpec((None, GROUP, tq, LANES), lambda i, h: (h, 0, i, 0))
    kv_spec = pl.BlockSpec((None, T, LANES), lambda i, h: (h, 0, 0))
    sems = [pltpu.SemaphoreType.DMA((3 * n_g,))] * 2 if n_g else []
    return pl.pallas_call(
        body, grid=(nq, N_KV_HEADS), name="flash_fwd_gather" if n_g else "flash_fwd",
        in_specs=[q_spec, kv_spec, kv_spec] + [ANY] * n_g,
        out_specs=[pl.BlockSpec((tq, GROUP * HEAD_DIM), lambda i, h: (i, h)), q_spec] + [ANY] * n_g,
        out_shape=[jax.ShapeDtypeStruct((T, N_Q_HEADS * HEAD_DIM), BF16), jax.ShapeDtypeStruct(q.shape, BF16)]
        + [jax.ShapeDtypeStruct(b.shape, b.dtype) for b in gather],
        input_output_aliases={3 + a: 2 + a for a in range(n_g)}, scratch_shapes=sems,
        compiler_params=_cp("arbitrary", "arbitrary"))(q, k, v, *gather)


def _flash_bwd(qa, doa, k, v, n_lat, scatter=()):
    _, _, T, _ = qa.shape
    tq = tk = 256
    nkv = T // tk
    M = GROUP * tq

    n_s = len(scatter)

    def body(qa_hbm, doa_hbm, k_ref, v_ref, *rest):
        dq_hbm, dk_ref, dv_ref = rest[n_s:n_s + 3]
        q_sc, do_sc, dq_sc, sems = rest[2 * n_s + 3:2 * n_s + 7]
        h = pl.program_id(0)
        j = pl.program_id(1)
        if n_s:
            start, wait = _plane_exchange(rest[:n_s], rest[n_s + 3:2 * n_s + 3], rest[-2], rest[-1], True)
            pl.when((h == 0) & (j == 0))(start)

        @pl.when(j == 0)
        def _():
            c1 = pltpu.make_async_copy(qa_hbm.at[h], q_sc, sems.at[0])
            c2 = pltpu.make_async_copy(doa_hbm.at[h], do_sc, sems.at[1])
            c1.start()
            c2.start()
            dq_sc[...] = jnp.zeros_like(dq_sc)
            c1.wait()
            c2.wait()

        kb = k_ref[...]
        vb = v_ref[...]
        kbt = kb.astype(F32).T.astype(BF16)

        def step(i, carry):
            dk, dv = carry
            r0 = i * tq if isinstance(i, int) else pl.multiple_of(i * tq, tq)
            qv = q_sc[:, pl.ds(r0, tq), :].reshape(M, LANES)
            dov = do_sc[:, pl.ds(r0, tq), :].reshape(M, LANES)
            pr = jnp.exp2(_dg(kb, qv, NT))
            ds = (pr * _dg(vb, dov, NT)).astype(BF16)
            dv = dv + _dot(pr.astype(BF16), dov)
            dk = dk + _dot(ds, qv)
            dq_sc[i] += _dot(kbt, ds)
            return dk, dv

        z = jnp.zeros((tk, LANES), F32)
        carry = _loop_unrolled(n_lat // tq, step, (z, z), UNROLL_BWD)
        dk_ref[...] = carry[0]
        dv_ref[...] = carry[1]

        @pl.when(j >= n_lat // tk)
        def _():
            c = (dk_ref[...], dv_ref[...])
            for i in range(n_lat // tq, T // tq):
                c = step(i, c)
            dk_ref[...] = c[0]
            dv_ref[...] = c[1]

        @pl.when(j == nkv - 1)
        def _():
            c3 = pltpu.make_async_copy(dq_sc, dq_hbm.at[h], sems.at[2])
            c3.start()
            c3.wait()

        if n_s:
            pl.when((h == N_KV_HEADS - 1) & (j == nkv - 1))(wait)

    kv_spec = pl.BlockSpec((None, tk, LANES), lambda h, j: (h, j, 0))
    return pl.pallas_call(
        body, grid=(N_KV_HEADS, nkv), name="flash_bwd_scatter" if n_s else "flash_bwd",
        in_specs=[ANY, ANY, kv_spec, kv_spec] + [ANY] * n_s, out_specs=[ANY, kv_spec, kv_spec] + [ANY] * n_s,
        out_shape=[jax.ShapeDtypeStruct((N_KV_HEADS, T // tq, LANES, M), F32), jax.ShapeDtypeStruct(k.shape, F32),
                   jax.ShapeDtypeStruct(k.shape, F32)] + [jax.ShapeDtypeStruct(g.shape, g.dtype) for g in scatter],
        scratch_shapes=[pltpu.VMEM((GROUP, T, LANES), BF16), pltpu.VMEM((GROUP, T, LANES), BF16),
                        pltpu.VMEM((T // tq, LANES, M), F32), pltpu.SemaphoreType.DMA((3,))]
        + ([pltpu.SemaphoreType.DMA((3 * n_s,))] * 2 if n_s else []),
        compiler_params=_cp("arbitrary", "arbitrary"))(qa, doa, k, v, *scatter)


def _conv_masks(i, tm, n_lat, T):
    row = lax.broadcasted_iota(jnp.int32, (tm, 1), 0)
    g = row + i * tm
    return row, (g == 0) | (g == n_lat), (g == n_lat - 1) | (g == T - 1)


def _shift_rows(v, prev_row, next_row, row, first, last):
    tm = v.shape[0]
    down = jnp.where(row == 0, prev_row, pltpu.roll(v, 1, 0))
    up = jnp.where(row == tm - 1, next_row, pltpu.roll(v, tm - 1, 0))
    return jnp.where(first, 0.0, down), jnp.where(last, 0.0, up)


def _halo_specs(tm, T, width, col):
    nb = T // 8
    return (pl.BlockSpec((8, width), lambda i: (jnp.maximum(i * (tm // 8) - 1, 0), col)),
            pl.BlockSpec((8, width), lambda i: (jnp.minimum((i + 1) * (tm // 8), nb - 1), col)))


def _conv_fwd(p, cw, n_lat):
    T = p.shape[0]
    tm = _tile(T, 1056)

    def body(p_ref, pp_ref, pn_ref, cw_ref, o_ref):
        row, first, last = _conv_masks(pl.program_id(0), tm, n_lat, T)
        z = p_ref[:, 256:512] * p_ref[:, 512:768]
        zp = pp_ref[7:8, 256:512] * pp_ref[7:8, 512:768]
        zn = pn_ref[0:1, 256:512] * pn_ref[0:1, 512:768]
        zd, zu = _shift_rows(z, zp, zn, row, first, last)
        conv = cw_ref[0:1, :] * zd + cw_ref[1:2, :] * z + cw_ref[2:3, :] * zu
        o_ref[...] = (p_ref[:, 0:256] * conv).astype(BF16)

    prev, nxt = _halo_specs(tm, T, 768, 0)
    return pl.pallas_call(
        body, grid=(T // tm,), name="conv_fwd",
        in_specs=[pl.BlockSpec((tm, 768), lambda i: (i, 0)), prev, nxt, _full(cw.shape)],
        out_specs=pl.BlockSpec((tm, CONV_W), lambda i: (i, 0)),
        out_shape=jax.ShapeDtypeStruct((T, CONV_W), BF16), compiler_params=_cp("parallel"))(p, p, p, cw)


def _conv_bwd(dp, dy, p, cw, n_lat):
    T = p.shape[0]
    tm = _tile(T, 1056)

    def body(dp_in, dy_ref, dyp_ref, dyn_ref, p_ref, pp_ref, pn_ref, cw_ref, dp_ref, dcw_ref):
        del dp_in
        i = pl.program_id(0)

        @pl.when(i == 0)
        def _():
            dcw_ref[...] = jnp.zeros_like(dcw_ref)

        row, first, last = _conv_masks(i, tm, n_lat, T)
        ab, ac, ax = p_ref[:, 0:256], p_ref[:, 256:512], p_ref[:, 512:768]
        z = ac * ax
        zp = pp_ref[7:8, 256:512] * pp_ref[7:8, 512:768]
        zn = pn_ref[0:1, 256:512] * pn_ref[0:1, 512:768]
        zd, zu = _shift_rows(z, zp, zn, row, first, last)
        w0, w1, w2 = cw_ref[0:1, :], cw_ref[1:2, :], cw_ref[2:3, :]
        dy = dy_ref[...]
        dc = dy * ab
        dcd, dcu = _shift_rows(dc, dyp_ref[7:8, :] * pp_ref[7:8, 0:256], dyn_ref[0:1, :] * pn_ref[0:1, 0:256],
                               row, first, last)
        dz = w0 * dcu + w1 * dc + w2 * dcd
        dp_ref[:, 0:256] = (dy * (w0 * zd + w1 * z + w2 * zu)).astype(BF16)
        dp_ref[:, 256:512] = (dz * ax).astype(BF16)
        dp_ref[:, 512:768] = (dz * ac).astype(BF16)
        dcw_ref[0:1, :] += jnp.sum(dc * zd, axis=0, keepdims=True)
        dcw_ref[1:2, :] += jnp.sum(dc * z, axis=0, keepdims=True)
        dcw_ref[2:3, :] += jnp.sum(dc * zu, axis=0, keepdims=True)

    prev, nxt = _halo_specs(tm, T, 768, 0)
    dprev, dnxt = _halo_specs(tm, T, CONV_W, 0)
    sec = pl.BlockSpec((tm, 768), lambda i: (i, 0))
    return pl.pallas_call(
        body, grid=(T // tm,), name="conv_bwd",
        in_specs=[ANY, pl.BlockSpec((tm, CONV_W), lambda i: (i, 0)), dprev, dnxt, sec, prev, nxt, _full(cw.shape)],
        out_specs=[sec, _full((8, CONV_W))],
        out_shape=[jax.ShapeDtypeStruct(dp.shape, BF16), jax.ShapeDtypeStruct((8, CONV_W), F32)],
        input_output_aliases={0: 0}, compiler_params=_cp("arbitrary"))(dp, dy, dy, dy, p, p, p, cw)


def _gmlp_mix(bd_ref, vs, grp):
    out = jnp.zeros((2 * CHUNK, SG_W), F32)
    for g in range(4):
        out = jnp.where(grp == g, _dot(bd_ref[g], vs), out)
    return out


def _gmlp_fwd(p, sgn, bd, bias):
    T = p.shape[0]
    tm = _tile(T, 768, 2 * CHUNK)

    def body(p_ref, sgn_ref, bd_ref, bias_ref, o_ref):
        x = _gelu(p_ref[:, 256:512])
        vn = (x * lax.rsqrt(jnp.mean(x * x, axis=-1, keepdims=True) + EPS) * sgn_ref[...]).astype(BF16)
        grp = _lane((2 * CHUNK, SG_W)) // 64
        for s in range(tm // (2 * CHUNK)):
            rs = slice(s * 2 * CHUNK, (s + 1) * 2 * CHUNK)
            mixed = _gmlp_mix(bd_ref, vn[rs], grp) + bias_ref[...]
            o_ref[rs, :] = (_gelu(p_ref[rs, 0:256]) * mixed).astype(BF16)

    return pl.pallas_call(
        body, grid=(T // tm,), name="gmlp_fwd",
        in_specs=[pl.BlockSpec((tm, 2 * SG_W), lambda i: (i, 3)), _full(sgn.shape), _full(bd.shape),
                  _full(bias.shape)],
        out_specs=pl.BlockSpec((tm, SG_W), lambda i: (i, 0)),
        out_shape=jax.ShapeDtypeStruct((T, SG_W), BF16), compiler_params=_cp("parallel"))(p, sgn, bd, bias)


def _gmlp_bwd(dp, dy, p, sgn, bd, bdt, bias, gsum):
    T = p.shape[0]
    tm = _tile(T, 768, 2 * CHUNK)
    nt = T // tm
    C2 = 2 * CHUNK

    def body(dp_in, dy_ref, p_ref, sgn_ref, bd_ref, bdt_ref, bias_ref, gsum_ref,
             dp_ref, dsg_ref, dws_ref, dbs_ref, acc_w, acc_b):
        del dp_in
        i = pl.program_id(0)

        @pl.when(i == 0)
        def _():
            dsg_ref[...] = jnp.zeros_like(dsg_ref)
            acc_w[...] = jnp.zeros_like(acc_w)
            acc_b[...] = jnp.zeros_like(acc_b)

        u = p_ref[:, 0:256]
        sv = p_ref[:, 256:512]
        ug = _gelu(u)
        x = _gelu(sv)
        r = lax.rsqrt(jnp.mean(x * x, axis=-1, keepdims=True) + EPS)
        xh = x * r
        sg = sgn_ref[...]
        vn = (xh * sg).astype(BF16)
        grp = _lane((C2, SG_W)) // 64
        dug, dvn = [], []
        for s in range(tm // C2):
            rs = slice(s * C2, (s + 1) * C2)
            vs = vn[rs]
            dys = dy_ref[rs, :]
            dug.append(dys * (_gmlp_mix(bd_ref, vs, grp) + bias_ref[...]))
            dmix = dys * ug[rs]
            acc_b[...] += dmix
            dmb = dmix.astype(BF16)
            dvn.append(_gmlp_mix(bdt_ref, dmb, grp))
            for g in range(4):
                acc_w[g] += _dg(jnp.where(grp == g, dmb, jnp.zeros_like(dmb)), vs, NT)
        dug = jnp.concatenate(dug, axis=0)
        dvn = jnp.concatenate(dvn, axis=0)
        dsg_ref[...] += jnp.sum(dvn * xh, axis=0, keepdims=True)
        gd = sg * dvn
        dx = r * (gd - xh * jnp.mean(xh * gd, axis=-1, keepdims=True))
        dp_ref[:, 0:256] = (dug * _gelu_grad(u)).astype(BF16)
        dp_ref[:, 256:512] = (dx * _gelu_grad(sv)).astype(BF16)

        @pl.when(i == nt - 1)
        def _():
            for g in range(4):
                dws_ref[g] = acc_w[g, 0:CHUNK, 0:CHUNK] + acc_w[g, CHUNK:C2, CHUNK:C2]
            dbs_ref[...] = jnp.dot(acc_b[0:CHUNK, :] + acc_b[CHUNK:C2, :], gsum_ref[...],
                                   preferred_element_type=F32, precision=lax.Precision.HIGHEST)

    sec = pl.BlockSpec((tm, 2 * SG_W), lambda i: (i, 3))
    return pl.pallas_call(
        body, grid=(nt,), name="gmlp_bwd",
        in_specs=[ANY, pl.BlockSpec((tm, SG_W), lambda i: (i, 0)), sec, _full(sgn.shape), _full(bd.shape),
                  _full(bdt.shape), _full(bias.shape), _full(gsum.shape)],
        out_specs=[sec, _full((1, SG_W)), _full((4, CHUNK, CHUNK)), _full((CHUNK, LANES))],
        out_shape=[jax.ShapeDtypeStruct(dp.shape, BF16), jax.ShapeDtypeStruct((1, SG_W), F32),
                   jax.ShapeDtypeStruct((4, CHUNK, CHUNK), F32), jax.ShapeDtypeStruct((CHUNK, LANES), F32)],
        scratch_shapes=[pltpu.VMEM((4, C2, C2), F32), pltpu.VMEM((C2, SG_W), F32)],
        input_output_aliases={0: 0}, compiler_params=_cp("arbitrary"))(dp, dy, p, sgn, bd, bdt, bias, gsum)


def _merge_fwd(ya, at, yc, p, wa, wb, wc):
    T = p.shape[0]
    tm = _tile(T, 528)
    n = wa.shape[2]

    def body(ya_ref, at_ref, yc_ref, ga_ref, gb_ref, gc_ref, wa_ref, wb_ref, wc_ref, o_ref):
        yav, atv, ycv = ya_ref[...], at_ref[...], yc_ref[...]
        for j in range(N_CHIPS):
            cs = slice(j * n, (j + 1) * n)
            m = (_sigmoid(ga_ref[:, cs]) * _dot(yav, wa_ref[j]) + _sigmoid(gb_ref[:, cs]) * _dot(atv, wb_ref[j])
                 + _sigmoid(gc_ref[:, cs]) * _dot(ycv, wc_ref[j]))
            o_ref[:, cs] = m.astype(BF16)

    def rows(w, col=0):
        return pl.BlockSpec((tm, w), lambda i: (i, col))

    return pl.pallas_call(
        body, grid=(T // tm,), name="merge_fwd",
        in_specs=[rows(CONV_W), rows(512), rows(SG_W), rows(D_MODEL, 2), rows(D_MODEL, 3), rows(D_MODEL, 4),
                  _full(wa.shape), _full(wb.shape), _full(wc.shape)],
        out_specs=rows(D_MODEL), out_shape=jax.ShapeDtypeStruct((T, D_MODEL), BF16),
        compiler_params=_cp("parallel"))(ya, at, yc, p, p, p, wa, wb, wc)


def _merge_bwd(dyo, ya, at, yc, p, wa, wb, wc, wo):
    T = p.shape[0]
    tm = _tile(T, 528)
    n = wa.shape[2]

    def body(dyo_ref, ya_ref, at_ref, yc_ref, ga_ref, gb_ref, gc_ref, wa_ref, wb_ref, wc_ref, wo_ref,
             dp_ref, dya_ref, doa_ref, dyc_ref, dwa_ref, dwb_ref, dwc_ref):
        i = pl.program_id(0)

        @pl.when(i == 0)
        def _():
            dwa_ref[...] = jnp.zeros_like(dwa_ref)
            dwb_ref[...] = jnp.zeros_like(dwb_ref)
            dwc_ref[...] = jnp.zeros_like(dwc_ref)

        dp_ref[:, 0:OFF_G] = jnp.zeros((tm, OFF_G), BF16)
        dm = _dg(dyo_ref[...], wo_ref[...], NT)
        yav, atv, ycv = ya_ref[...], at_ref[...], yc_ref[...]
        dya = jnp.zeros((tm, CONV_W), F32)
        dat = jnp.zeros((tm, 512), F32)
        dyc = jnp.zeros((tm, SG_W), F32)
        for j in range(N_CHIPS):
            cs = slice(j * n, (j + 1) * n)
            dmj = dm[:, cs]
            for y_in, w_ref, g_ref, dw_ref, which in (
                    (yav, wa_ref, ga_ref, dwa_ref, 0), (atv, wb_ref, gb_ref, dwb_ref, 1),
                    (ycv, wc_ref, gc_ref, dwc_ref, 2)):
                sg = _sigmoid(g_ref[:, cs])
                y = _dot(y_in, w_ref[j])
                c0 = OFF_G + which * D_MODEL + j * n
                dp_ref[:, c0:c0 + n] = (dmj * y * sg * (1.0 - sg)).astype(BF16)
                dyb = (dmj * sg).astype(BF16)
                dw_ref[j] += _dg(y_in, dyb, TN)
                back = _dg(dyb, w_ref[j], NT)
                if which == 0:
                    dya = dya + back
                elif which == 1:
                    dat = dat + back
                else:
                    dyc = dyc + back
        dya_ref[...] = dya
        dyc_ref[...] = dyc
        prod = dat * atv.astype(F32)
        lane = _lane((tm, LANES))
        dat_rows = _heads_to_rows(dat, N_Q_HEADS)
        for h in range(N_Q_HEADS):
            grp = prod[:, (h // 2) * LANES:(h // 2 + 1) * LANES]
            keep = (lane < 64) if h % 2 == 0 else (lane >= 64)
            delta = jnp.sum(jnp.where(keep, grp, 0.0), axis=1, keepdims=True)
            doa_ref[h] = _aug(dat_rows[h], delta).astype(BF16)

    def rows(w, col=0):
        return pl.BlockSpec((tm, w), lambda i: (i, col))

    return pl.pallas_call(
        body, grid=(T // tm,), name="merge_bwd",
        in_specs=[rows(D_MODEL), rows(CONV_W), rows(512), rows(SG_W), rows(D_MODEL, 2), rows(D_MODEL, 3),
                  rows(D_MODEL, 4), _full(wa.shape), _full(wb.shape), _full(wc.shape), _full(wo.shape)],
        out_specs=[rows(IN_W), rows(CONV_W),
                   pl.BlockSpec((N_Q_HEADS, tm, LANES), lambda i: (0, i, 0)), rows(SG_W),
                   _full(wa.shape), _full(wb.shape), _full(wc.shape)],
        out_shape=[jax.ShapeDtypeStruct((T, IN_W), BF16)] + [
            jax.ShapeDtypeStruct((T, CONV_W), F32), jax.ShapeDtypeStruct((N_Q_HEADS, T, LANES), BF16),
            jax.ShapeDtypeStruct((T, SG_W), F32), jax.ShapeDtypeStruct(wa.shape, F32),
            jax.ShapeDtypeStruct(wb.shape, F32), jax.ShapeDtypeStruct(wc.shape, F32)],
        compiler_params=_cp("arbitrary"))(dyo, ya, at, yc, p, p, p, wa, wb, wc, wo)


def _loss_grad(xf, tgt, n_lat):
    T, D = xf.shape
    tm = _tile(np.gcd(n_lat, T), 512)
    nl = n_lat // tm

    def body(x_ref, t_ref, dy_ref, l_ref):
        i = pl.program_id(0)

        @pl.when(i == 0)
        def _():
            l_ref[...] = jnp.zeros_like(l_ref)

        @pl.when(i < nl)
        def _():
            err = x_ref[...] - t_ref[...]
            dy_ref[...] = err * (1.0 / D)
            sq = jnp.sum(jnp.sum(err * err, axis=1, keepdims=True), axis=0, keepdims=True)
            l_ref[...] += (0.5 / D) * sq

        @pl.when(i >= nl)
        def _():
            dy_ref[...] = jnp.zeros_like(dy_ref)

    return pl.pallas_call(
        body, grid=(T // tm,), name="loss_grad",
        in_specs=[pl.BlockSpec((tm, D), lambda i: (i, 0)), pl.BlockSpec((tm, D), lambda i: (jnp.minimum(i, nl - 1), 0))],
        out_specs=[pl.BlockSpec((tm, D), lambda i: (i, 0)), _full((8, LANES))],
        out_shape=[jax.ShapeDtypeStruct((T, D), F32), jax.ShapeDtypeStruct((8, LANES), F32)],
        compiler_params=_cp("arbitrary"))(xf, tgt)


def _row_tile(R, C):
    if R * C <= (1 << 19) or R % 8:
        return R
    return _tile(R, max(8, (1 << 19) // C), 8)


def _adamw(w, m, v, g1, g2=None):
    shape = w.shape
    C = shape[-1]
    R = int(np.prod(shape[:-1])) if len(shape) > 1 else 1
    tr = _row_tile(R, C)
    ins = [a.reshape(R, C) for a in ((w, m, v, g1) if g2 is None else (w, m, v, g1, g2))]

    def body(*refs):
        w_ref, m_ref, v_ref = refs[0], refs[1], refs[2]
        g_ref, d_ref, m2_ref, v2_ref = refs[-4:]
        g = refs[3][...] if g2 is None else refs[3][...] + refs[4][...]
        m2 = ADAM_B1 * m_ref[...] + (1.0 - ADAM_B1) * g
        v2 = ADAM_B2 * v_ref[...] + (1.0 - ADAM_B2) * (g * g)
        m_hat = m2 / (1.0 - ADAM_B1 ** ADAM_STEP)
        v_hat = v2 / (1.0 - ADAM_B2 ** ADAM_STEP)
        g_ref[...] = g
        d_ref[...] = -ADAM_LR * (m_hat / (jnp.sqrt(v_hat) + ADAM_EPS) + ADAM_WD * w_ref[...])
        m2_ref[...] = m2
        v2_ref[...] = v2

    spec = pl.BlockSpec((tr, C), lambda i: (i, 0))
    outs = pl.pallas_call(
        body, grid=(R // tr,), name="adamw", in_specs=[spec] * len(ins), out_specs=[spec] * 4,
        out_shape=[jax.ShapeDtypeStruct((R, C), F32)] * 4, compiler_params=_cp("parallel"))(*ins)
    return [o.reshape(shape) for o in outs]


def _adamw_layer(w, m, v, gs, l, prev):
    L, a, b = w.shape
    tr = _row_tile(a, b)
    n_in = 3 + len(gs)

    def body(*refs):
        g_ref, d_ref, m2_ref, v2_ref = refs[-4:]
        g = refs[3][...]
        for r in refs[4:3 + len(gs)]:
            g = g + r[...]
        m2 = ADAM_B1 * refs[1][...] + (1.0 - ADAM_B1) * g
        v2 = ADAM_B2 * refs[2][...] + (1.0 - ADAM_B2) * (g * g)
        m_hat = m2 / (1.0 - ADAM_B1 ** ADAM_STEP)
        v_hat = v2 / (1.0 - ADAM_B2 ** ADAM_STEP)
        g_ref[...] = g
        d_ref[...] = -ADAM_LR * (m_hat / (jnp.sqrt(v_hat) + ADAM_EPS) + ADAM_WD * refs[0][...])
        m2_ref[...] = m2
        v2_ref[...] = v2

    layer = pl.BlockSpec((None, tr, b), lambda i: (l, i, 0))
    outs = pl.pallas_call(
        body, grid=(a // tr,), name="adamw_layer",
        in_specs=[layer] * 3 + [pl.BlockSpec((tr, b), lambda i: (i, 0))] * len(gs) + [ANY] * len(prev),
        out_specs=[layer] * 4, out_shape=[jax.ShapeDtypeStruct((L, a, b), F32)] * 4,
        input_output_aliases={n_in + k: k for k in range(len(prev))},
        compiler_params=_cp("parallel"))(w, m, v, *gs, *prev)
    return list(outs)


def _sum_lead(x, name):
    n, R, C = x.shape
    tr = _row_tile(R, C * n)

    def body(x_ref, o_ref):
        acc = x_ref[0].astype(F32)
        for s in range(1, n):
            acc = acc + x_ref[s].astype(F32)
        o_ref[...] = acc

    return pl.pallas_call(
        body, grid=(R // tr,), name=name, in_specs=[pl.BlockSpec((n, tr, C), lambda i: (0, i, 0))],
        out_specs=pl.BlockSpec((tr, C), lambda i: (i, 0)), out_shape=jax.ShapeDtypeStruct((R, C), F32),
        compiler_params=_cp("parallel"))(x)


def _silu(x):
    return x * _sigmoid(x)


def _mod_fwd(a_raw, w_mod, bsh):
    L, D, n = w_mod.shape

    def body(a_ref, w_ref, b_ref, o_ref):
        o_ref[...] = _dot(_silu(a_ref[...]).astype(BF16), w_ref[...].astype(BF16)) + b_ref[...]

    return pl.pallas_call(
        body, grid=(L,), name="mod_fwd",
        in_specs=[_full(a_raw.shape), pl.BlockSpec((None, D, n), lambda l: (l, 0, 0)),
                  pl.BlockSpec((None, 1, n), lambda l: (l, 0, 0))],
        out_specs=pl.BlockSpec((None, 16, n), lambda l: (l, 0, 0)),
        out_shape=jax.ShapeDtypeStruct((L, 16, n), F32), compiler_params=_cp("parallel"))(a_raw, w_mod, bsh)


def _wmod_grad(a_raw, dms):
    L, _, n = dms.shape
    D = a_raw.shape[1]

    def body(a_ref, dm_ref, o_ref):
        o_ref[...] = _dg(_silu(a_ref[...]).astype(BF16), dm_ref[...].astype(BF16), TN)

    return pl.pallas_call(
        body, grid=(L,), name="wmod_grad",
        in_specs=[_full(a_raw.shape), pl.BlockSpec((None, 16, n), lambda l: (l, 0, 0))],
        out_specs=pl.BlockSpec((None, D, n), lambda l: (l, 0, 0)),
        out_shape=jax.ShapeDtypeStruct((L, D, n), F32), compiler_params=_cp("parallel"))(a_raw, dms)


def _cctx_partial(dmc, w_mod):
    L, D, n = w_mod.shape

    def body(dm_ref, w_ref, o_ref):
        part = _dg(dm_ref[...].astype(BF16), w_ref[...].astype(BF16), NT)

        @pl.when(pl.program_id(0) == 0)
        def _():
            o_ref[...] = part

        @pl.when(pl.program_id(0) > 0)
        def _():
            o_ref[...] += part

    return pl.pallas_call(
        body, grid=(L,), name="cctx_partial",
        in_specs=[pl.BlockSpec((None, 16, n), lambda l: (l, 0, 0)), pl.BlockSpec((None, D, n), lambda l: (l, 0, 0))],
        out_specs=_full((16, D)), out_shape=jax.ShapeDtypeStruct((16, D), F32),
        compiler_params=_cp("arbitrary"))(dmc, w_mod)


def _cctx_final(parts, cc):
    def body(p_ref, c_ref, o_ref):
        s = p_ref[0, 0:8, :]
        for j in range(1, N_CHIPS):
            s = s + p_ref[2 * j, 0:8, :]
        xv = c_ref[...]
        sg = _sigmoid(xv)
        o_ref[...] = s * (sg * (1.0 + xv * (1.0 - sg)))

    return pl.pallas_call(
        body, name="cctx_final", in_specs=[_full(parts.shape), _full(cc.shape)], out_specs=_full((8, LANES)),
        out_shape=jax.ShapeDtypeStruct((8, LANES), F32), compiler_params=_cp())(parts, cc)


def _me():
    return lax.axis_index("x"), lax.axis_index("y"), lax.axis_index("c")


def _flip(v, bit):
    return 1 - v if bit else v


def _remote(src, dst, ssem, rsem, peer):
    return pltpu.make_async_remote_copy(src_ref=src, dst_ref=dst, send_sem=ssem, recv_sem=rsem,
                                        device_id=peer, device_id_type=MESH_ID)


def _ag8(xb, name):
    R = xb.shape[0]

    def pallas(x):
        def body(x_ref, o_ref, ssem, rsem):
            mx, my, mc = _me()
            me = 4 * mx + 2 * my + mc
            sib = (mx, my, 1 - mc)
            peers = _plane_peers(mx, my, mc)
            sends = [_remote(x_ref, o_ref.at[me], ssem.at[0], rsem.at[0], sib)]
            sends += [_remote(x_ref, o_ref.at[me], ssem.at[1 + k], rsem.at[1 + k], peer)
                      for k, (peer, _) in enumerate(peers)]
            for cp in sends:
                cp.start()
            for k, (peer, pj) in enumerate(peers):
                got = o_ref.at[2 * pj + mc]
                _remote(got, got, ssem.at[1 + k], rsem.at[1 + k], peer).wait_recv()
                fw = _remote(got, got, ssem.at[4 + k], rsem.at[4 + k], sib)
                fw.start()
                sends.append(fw)
            _remote(x_ref, o_ref.at[4 * mx + 2 * my + 1 - mc], ssem.at[0], rsem.at[0], sib).wait_recv()
            for k, (_, pj) in enumerate(peers):
                theirs = o_ref.at[2 * pj + 1 - mc]
                _remote(theirs, theirs, ssem.at[4 + k], rsem.at[4 + k], sib).wait_recv()
            for cp in sends:
                cp.wait_send()

        return pl.pallas_call(
            body, name=name, in_specs=[ANY], out_specs=ANY, out_shape=jax.ShapeDtypeStruct((N_DEV, R, LANES), F32),
            scratch_shapes=[pltpu.SemaphoreType.DMA((N_DEV - 1,)), pltpu.SemaphoreType.DMA((N_DEV - 1,))])(x)

    mx, my, mc = _me()
    return lax.dynamic_update_slice(pallas(xb), xb[None], (4 * mx + 2 * my + mc, 0, 0))


def _plane_peers(mx, my, mc):
    out = []
    for k in range(1, N_CHIPS):
        px, py = _flip(mx, k & 2), _flip(my, k & 1)
        out.append(((px, py, mc), 2 * px + py))
    return out


def _plane_exchange(ins, outs, ssem, rsem, scatter):
    n = len(ins)

    def desc(k, a, arriving):
        mx, my, mc = _me()
        j = 2 * mx + my
        peer, pj = _plane_peers(mx, my, mc)[k]
        src = ins[a].at[pj if scatter else j]
        dst = outs[a].at[pj if arriving else j]
        return _remote(src, dst, ssem.at[k * n + a], rsem.at[k * n + a], peer)

    def start():
        for k in range(N_CHIPS - 1):
            for a in range(n):
                desc(k, a, False).start()

    def wait():
        for k in range(N_CHIPS - 1):
            for a in range(n):
                desc(k, a, True).wait_recv()
        for k in range(N_CHIPS - 1):
            for a in range(n):
                desc(k, a, False).wait_send()

    return start, wait


def _chip_gather(bufs, name):
    n = len(bufs)
    halves = [b.shape[1] // 2 for b in bufs]

    def body(*refs):
        outs = refs[n:2 * n]
        ssem, rsem, fsem, gsem = refs[2 * n:]
        mx, my, mc = _me()
        j = 2 * mx + my
        sib = (mx, my, 1 - mc)

        def half(a, blk, c):
            return outs[a].at[blk, pl.ds(c * halves[a], halves[a]), :]

        peers = _plane_peers(mx, my, mc)
        sends = []
        for k, (peer, _) in enumerate(peers):
            for a in range(n):
                mine = half(a, j, mc)
                cp = _remote(mine, mine, ssem.at[k * n + a], rsem.at[k * n + a], peer)
                cp.start()
                sends.append(cp)
        for k, (peer, pj) in enumerate(peers):
            for a in range(n):
                got = half(a, pj, mc)
                _remote(got, got, ssem.at[k * n + a], rsem.at[k * n + a], peer).wait_recv()
                fw = _remote(got, got, fsem.at[k * n + a], gsem.at[k * n + a], sib)
                fw.start()
                sends.append(fw)
        for k, (_, pj) in enumerate(peers):
            for a in range(n):
                theirs = half(a, pj, 1 - mc)
                _remote(theirs, theirs, fsem.at[k * n + a], gsem.at[k * n + a], sib).wait_recv()
        for cp in sends:
            cp.wait_send()

    sems = pltpu.SemaphoreType.DMA((3 * n,))
    return pl.pallas_call(
        body, name=name, in_specs=[ANY] * n, out_specs=[ANY] * n,
        out_shape=[jax.ShapeDtypeStruct(b.shape, b.dtype) for b in bufs],
        input_output_aliases={a: a for a in range(n)},
        scratch_shapes=[sems, sems, sems, sems])(*bufs)


def _chip_scatter(gs, name):
    n = len(gs)

    def body(*refs):
        ins, outs = refs[:n], refs[n:2 * n]
        ssem, rsem = refs[2 * n:]
        mx, my, mc = _me()
        j = 2 * mx + my
        peers = _plane_peers(mx, my, mc)
        sends = []
        for k, (peer, pj) in enumerate(peers):
            for a in range(n):
                cp = _remote(ins[a].at[pj], outs[a].at[j], ssem.at[k * n + a], rsem.at[k * n + a], peer)
                cp.start()
                sends.append(cp)
        for k, (peer, pj) in enumerate(peers):
            for a in range(n):
                _remote(ins[a].at[pj], outs[a].at[pj], ssem.at[k * n + a], rsem.at[k * n + a], peer).wait_recv()
        for cp in sends:
            cp.wait_send()

    return pl.pallas_call(
        body, name=name, in_specs=[ANY] * n, out_specs=[ANY] * n,
        out_shape=[jax.ShapeDtypeStruct(g.shape, g.dtype) for g in gs],
        scratch_shapes=[pltpu.SemaphoreType.DMA((3 * n,)), pltpu.SemaphoreType.DMA((3 * n,))])(*gs)


def _sibling_swap(xs, name):
    n = len(xs)

    def body(*refs):
        ins, outs = refs[:n], refs[n:2 * n]
        ssem, rsem = refs[2 * n:]
        mx, my, mc = _me()
        cps = [_remote(ins[a], outs[a], ssem.at[a], rsem.at[a], (mx, my, 1 - mc)) for a in range(n)]
        for cp in cps:
            cp.start()
        for cp in cps:
            cp.wait()

    return pl.pallas_call(
        body, name=name, in_specs=[ANY] * n, out_specs=[ANY] * n,
        out_shape=[jax.ShapeDtypeStruct(x.shape, x.dtype) for x in xs],
        scratch_shapes=[pltpu.SemaphoreType.DMA((n,)), pltpu.SemaphoreType.DMA((n,))])(*xs)


def _sibling_halves(gs, name):
    n = len(gs)

    def body(*refs):
        ins, outs = refs[:n], refs[n:2 * n]
        ssem, rsem = refs[2 * n:]
        mx, my, mc = _me()
        cps = []
        for a in range(n):
            h = gs[a].shape[1] // 2
            cps.append(_remote(ins[a].at[:, pl.ds((1 - mc) * h, h), :], outs[a], ssem.at[a], rsem.at[a],
                               (mx, my, 1 - mc)))
        for cp in cps:
            cp.start()
        for cp in cps:
            cp.wait()

    return pl.pallas_call(
        body, name=name, in_specs=[ANY] * n, out_specs=[ANY] * n,
        out_shape=[jax.ShapeDtypeStruct((g.shape[0], g.shape[1] // 2, g.shape[2]), g.dtype) for g in gs],
        scratch_shapes=[pltpu.SemaphoreType.DMA((n,)), pltpu.SemaphoreType.DMA((n,))])(*gs)


def _sibling_fill(hs, name):
    n = len(hs)

    def body(*refs):
        ins, outs = refs[:n], refs[n:2 * n]
        ssem, rsem = refs[2 * n:]
        mx, my, mc = _me()
        cps = []
        for a in range(n):
            h = hs[a].shape[0]
            cps.append(_remote(ins[a], outs[a].at[pl.ds(mc * h, h), :], ssem.at[a], rsem.at[a], (mx, my, 1 - mc)))
        for cp in cps:
            cp.start()
        for a, cp in enumerate(cps):
            h = hs[a].shape[0]
            theirs = outs[a].at[pl.ds((1 - mc) * h, h), :]
            _remote(ins[a], theirs, ssem.at[a], rsem.at[a], (mx, my, 1 - mc)).wait_recv()
            cp.wait_send()

    return pl.pallas_call(
        body, name=name, in_specs=[ANY] * n, out_specs=[ANY] * n,
        out_shape=[jax.ShapeDtypeStruct((2 * x.shape[0], x.shape[1]), x.dtype) for x in hs],
        scratch_shapes=[pltpu.SemaphoreType.DMA((n,)), pltpu.SemaphoreType.DMA((n,))])(*hs)


def _add_cast(g, sb):
    J, h, b = g.shape
    th = _row_tile(h, b * J)

    def body(g_ref, s_ref, o_ref):
        o_ref[...] = (g_ref[...].astype(F32) + s_ref[...].astype(F32)).astype(BF16)

    spec = pl.BlockSpec((J, th, b), lambda i: (0, i, 0))
    return pl.pallas_call(
        body, grid=(h // th,), name="add_planes", in_specs=[spec, spec], out_specs=spec,
        out_shape=jax.ShapeDtypeStruct((J, h, b), BF16), compiler_params=_cp("parallel"))(g, sb)


_WEIGHTS = ("c_ctx", "w_mod", "b_mod", "norm1", "w_in", "q_gain", "k_gain", "conv_w", "sg_norm", "w_s", "b_s",
            "w_a", "w_b", "w_c", "w_o", "norm2", "w_ff1", "w_ff3", "w_ff2")
_BIG = ("w_in", "w_a", "w_b", "w_c", "w_o", "w_ff1", "w_ff3", "w_ff2")
_TRANSPOSED = ("w_ff1", "w_ff3")


def _constants():
    idx = np.arange(LANES)
    e = (idx[:, None] // 64 == idx[None, :] // 64).astype(np.float32) / 64.0
    c512 = np.arange(512)
    fold = (c512[:, None] % 64 == idx[None, :]).astype(np.float32)
    c256 = np.arange(SG_W)
    gsum = (c256[:, None] // 64 == idx[None, :]).astype(np.float32)
    return jnp.asarray(e, BF16), jnp.asarray(fold, F32), jnp.asarray(gsum, F32)


def _rope_tables(n_lat, n_ctx):
    t = jnp.arange(n_lat)
    inv = ROPE_THETA ** (-jnp.arange(0, HEAD_DIM // 2, 2, dtype=F32) / (HEAD_DIM // 2))
    ar = (t // GRID_W).astype(F32)[:, None] * inv
    ac = (t % GRID_W).astype(F32)[:, None] * inv
    cos = jnp.concatenate([jnp.cos(ar), jnp.cos(ar), jnp.cos(ac), jnp.cos(ac)], axis=1)
    sin = jnp.concatenate([-jnp.sin(ar), jnp.sin(ar), -jnp.sin(ac), jnp.sin(ac)], axis=1)
    cos = jnp.concatenate([cos, jnp.ones((n_ctx, HEAD_DIM), F32)], axis=0)
    sin = jnp.concatenate([sin, jnp.zeros((n_ctx, HEAD_DIM), F32)], axis=0)
    return jnp.concatenate([cos, cos], axis=1), jnp.concatenate([sin, sin], axis=1)


def kernel(x, c, ctx, c_ctx, w_mod, b_mod, norm1, w_in, q_gain, k_gain, conv_w, sg_norm, w_s, b_s, w_a, w_b, w_c, w_o, norm2, w_ff1, w_ff3, w_ff2, loss_target, m_c_ctx, m_w_mod, m_b_mod, m_norm1, m_w_in, m_q_gain, m_k_gain, m_conv_w, m_sg_norm, m_w_s, m_b_s, m_w_a, m_w_b, m_w_c, m_w_o, m_norm2, m_w_ff1, m_w_ff3, m_w_ff2, v_c_ctx, v_w_mod, v_b_mod, v_norm1, v_w_in, v_q_gain, v_k_gain, v_conv_w, v_sg_norm, v_w_s, v_b_s, v_w_a, v_w_b, v_w_c, v_w_o, v_norm2, v_w_ff1, v_w_ff3, v_w_ff2):
    given = dict(locals())
    mx, my, mc = _me()
    chip = 2 * mx + my
    dev = 4 * mx + 2 * my + mc
    L = norm1.shape[0]
    S, Lc = x.shape[1], ctx.shape[1]
    T = S + Lc
    D = D_MODEL
    n_mod, n_in, n_ff = w_mod.shape[2], w_in.shape[2], w_ff1.shape[2]
    n_cw = conv_w.shape[2]
    e_avg, fold, gsum = _constants()
    cos_t, sin_t = _rope_tables(S, Lc)

    cw_rows = (L * 3 * n_cw) // LANES
    pad = (-(8 + cw_rows)) % 8
    buf = jnp.concatenate([c.reshape(8, LANES), conv_w.reshape(cw_rows, LANES), jnp.zeros((pad, LANES), F32)], axis=0)
    g1 = _ag8(buf, "gather_cond")
    conds = g1[:, :8].reshape(N_DEV, D)
    cw_full = jnp.stack([g1[2 * j, 8:8 + cw_rows].reshape(L, 3, n_cw) for j in range(N_CHIPS)], axis=2)
    cw_full = cw_full.reshape(L, 3, N_CHIPS * n_cw)
    cw8 = jnp.pad(cw_full, ((0, 0), (0, 5), (0, 0)))
    a_raw = jnp.concatenate([conds, c_ctx[None], jnp.zeros((7, D), F32)], axis=0)
    bsh = lax.dynamic_slice_in_dim(b_mod, chip * n_mod, n_mod, axis=1)[:, None, :]
    mod_sh = _mod_fwd(a_raw, w_mod, bsh)
    g2 = _ag8(mod_sh.reshape(-1, LANES), "gather_mod")
    mods = jnp.stack([g2[2 * j].reshape(L, 16, n_mod) for j in range(N_CHIPS)], axis=2).reshape(L, 16, N_CHIPS * n_mod)
    lat = lax.dynamic_index_in_dim(mods, dev, axis=1, keepdims=False)
    mod = jnp.stack([lat.reshape(L, 6, D), mods[:, 8].reshape(L, 6, D)], axis=1)
    mod = jnp.pad(mod, ((0, 0), (0, 0), (0, 2), (0, 0)))

    qg = jnp.tile(q_gain, (1, N_Q_HEADS))[:, None, :]
    kg = jnp.tile(k_gain, (1, N_KV_HEADS))[:, None, :]
    sgn = sg_norm[:, None, :]
    ws_b = w_s.astype(BF16)
    zero = jnp.zeros_like(ws_b)
    bd = jnp.concatenate([jnp.concatenate([ws_b, zero], axis=3), jnp.concatenate([zero, ws_b], axis=3)], axis=2)
    bdt = jnp.swapaxes(bd, 2, 3)
    bias = jnp.tile(jnp.repeat(jnp.swapaxes(b_s, 1, 2), SG_W // 4, axis=2), (1, 2, 1))

    def shard_bufs(l, names=_BIG):
        return [lax.dynamic_update_slice(lax.empty((N_CHIPS,) + given[nm].shape[1:], BF16),
                                         given[nm][l].astype(BF16)[None], (chip, 0, 0)) for nm in names]

    def unpack(bufs):
        win, wa, wb, wc, wo, w1, w3, w2 = bufs
        return win, wa, wb, wc, wo.reshape(1, D, D), w1, w3, w2

    def layer_fwd(X, l, W, nxt):
        win = W[0]
        h, ht = _norm_mod(X, norm1[l][None], mod[l], 0, 1, S, True)
        p = _mm_nn(h, win, F32, "in_proj")
        ya = _conv_fwd(p, cw8[l], S)
        q, k, v = _qkv_prep(p, cos_t, sin_t, qg[l], kg[l], e_avg)
        at, qa, *got = _flash_fwd(q.reshape(N_KV_HEADS, GROUP, T, LANES), k, v, S, nxt)
        if len(W) == 1:
            W, got = unpack([win] + got[:len(_BIG) - 1]), got[len(_BIG) - 1:]
        win, wa, wb, wc, wo, w1, w3, w2 = W
        yc = _gmlp_fwd(p, sgn[l], bd[l], bias[l])
        mg = _merge_fwd(ya, at, yc, p, wa, wb, wc)
        X1, f1 = _mm_res(mg[None], wo, X, mod[l], 2, S, "out_proj")
        h2, = _norm_mod(X1, norm2[l][None], mod[l], 3, 4, S, False)
        a1, a3, act = _ffn_up(h2, w1, w3)
        X2, f2 = _mm_res(act, w2, X1, mod[l], 5, S, "ffn_down")
        return X2, W, got, dict(X=X, ht=ht, h2=h2, p=p, ya=ya, k=k, v=v, at=at, qa=qa, yc=yc, mg=mg, X1=X1, f1=f1,
                             a1=a1, a3=a3, act=act, f2=f2)

    def layer_bwd(dX2, dyf, l, W, sv, pending):
        win, wa, wb, wc, wo, w1, w3, w2 = W
        da1, da3 = _ffn_down_bwd(dyf, w2, sv["a1"], sv["a3"])
        dw2 = _mm_tn(sv["act"], dyf, _shard_rows(n_ff), _rows(D), N_CHIPS, n_ff, D, T, "dw_ff2")
        dh2 = _mm_nt_acc([da1, da3], [w1, w3], False, "ffn_up_bwd")
        dw1 = _mm_tn(da1, sv["h2"], _shard_rows(n_ff), _rows(D), N_CHIPS, n_ff, D, T, "dw_ff1")
        dw3 = _mm_tn(da3, sv["h2"], _shard_rows(n_ff), _rows(D), N_CHIPS, n_ff, D, T, "dw_ff3")
        dX1, dn2, dsh2, dsc2, dyo, dgt1 = _norm_mod_bwd(sv["X1"], dh2, dX2, norm2[l][None], mod[l], 4, S,
                                                        (sv["f1"], mod[l], 2))
        dwo = _mm_tn(sv["mg"], dyo, _rows(D), _rows(D), 1, D, D, T, "dw_o")
        dp, dya, doa, dyc, dwa, dwb, dwc = _merge_bwd(dyo, sv["ya"], sv["at"], sv["yc"], sv["p"], wa, wb, wc, wo[0])
        dp, dcw = _conv_bwd(dp, dya, sv["p"], cw8[l], S)
        dp, dsg, dws, dbs = _gmlp_bwd(dp, dyc, sv["p"], sgn[l], bd[l], bdt[l], bias[l], gsum)
        early = [dwa.astype(BF16), dwb.astype(BF16), dwc.astype(BF16), dwo.reshape(N_CHIPS, D // N_CHIPS, D),
                 dw1, dw3, dw2]
        dq, dk, dv, *recv = _flash_bwd(sv["qa"], doa.reshape(N_KV_HEADS, GROUP, T, LANES), sv["k"], sv["v"], S,
                                       list(pending) + (early if l == 0 else []))
        dp, dqg, dkg = _qkv_prep_bwd(dp, dq, dk, dv, sv["p"], cos_t, sin_t,
                                     qg[l], kg[l], e_avg, fold)
        dh = _mm_nt_acc([dp], [win], True, "in_proj_bwd")
        dwin = _mm_dw(sv["ht"], dp, _row_cols(n_in), N_CHIPS, n_in, "dw_in")
        below = (saved[l - 1]["f2"], mod[l - 1], 5) if l else None
        dX0, dn1, dsh1, dsc1, *nxt = _norm_mod_bwd(sv["X"], dh, dX1, norm1[l][None], mod[l], 1, S, below)
        dmod = [dsh1, dsc1, dgt1, dsh2, dsc2]
        big = [dwin] + early
        small = dict(norm1=dn1[0], norm2=dn2[0], q_gain=dqg[0, :HEAD_DIM], k_gain=dkg[0, :HEAD_DIM],
                     conv_w=dcw[:3], sg_norm=dsg[0], w_s=dws, b_s=jnp.swapaxes(dbs[:, :4], 0, 1), dmod=dmod)
        return dX0, nxt, big, small, recv

    X = jnp.concatenate([x[0], ctx[0]], axis=0)
    Ws, saved = [_chip_gather(shard_bufs(0, _BIG[:1]), "gather_weights")], []
    for l in range(L):
        nxt = (shard_bufs(0, _BIG[1:]) if l == 0 else []) + (shard_bufs(l + 1) if l + 1 < L else [])
        X, Ws[l], got, sv = layer_fwd(X, l, Ws[l], nxt)
        if got:
            Ws.append(unpack(got))
        saved.append(sv)
    dX, lpart = _loss_grad(X, loss_target[0], S)
    loss = lax.psum(lpart[0, 0], ("x", "y", "c"))

    out = {nm: () for nm in _BIG}
    smalls = [None] * L

    def own_block(r, g):
        return lax.dynamic_update_slice(r, lax.dynamic_slice_in_dim(g, chip, 1, axis=0), (chip, 0, 0))

    def stored(nm, a):
        return jnp.swapaxes(a, 1, 2) if nm in _TRANSPOSED else a

    def update(l, names, grads):
        for nm, g in zip(names, grads):
            out[nm] = _adamw_layer(stored(nm, given[nm]), stored(nm, given["m_" + nm]), stored(nm, given["v_" + nm]),
                                   g, l, out[nm])

    def plane_update(l, names, recv, sent):
        mine = [_sum_lead(own_block(r, g), "sum_chips") for r, g in zip(recv, sent)]
        update(l, names, zip(mine, _sibling_swap(mine, "swap_planes")))

    pending = []
    dyf, dgt2 = _gate_bwd(dX, saved[L - 1]["f2"], mod[L - 1], 5, S)
    for l in reversed(range(L)):
        dX, nxt, big, smalls[l], recv = layer_bwd(dX, dyf, l, Ws[l], saved[l], pending)
        smalls[l]["dmod"] = jnp.concatenate(smalls[l]["dmod"] + [dgt2], axis=1)
        if nxt:
            dyf, dgt2 = nxt
        if recv:
            plane_update(l + 1, _BIG, recv[:len(pending)], pending)
            if l == 0:
                plane_update(0, _BIG[1:], recv[len(pending):], big[1:])
        pending = big
    last = big[:1]
    sib = _sibling_halves(last, "swap_halves")
    own = [lax.dynamic_slice_in_dim(g, mc * (g.shape[1] // 2), g.shape[1] // 2, axis=1) for g in last]
    sent = [_add_cast(g, s_) for g, s_ in zip(own, sib)]
    recv = [own_block(r, g) for r, g in zip(_chip_scatter(sent, "scatter_grads"), sent)]
    halves = [_sum_lead(r, "sum_chips") for r in recv]
    full = _sibling_fill(halves, "fill_halves")
    update(0, _BIG[:1], [(lax.dynamic_update_slice(f, hv, (mc * hv.shape[0], 0)),) for f, hv in zip(full, halves)])
    grad_x = dX[:S][None]

    def flat(nm):
        return jnp.stack([smalls[l][nm] for l in range(L)]).reshape(-1)

    dmod_all = jnp.stack([smalls[l]["dmod"] for l in range(L)])
    dml = dmod_all[:, 0].reshape(-1)
    dmc = dmod_all[:, 1].reshape(-1)
    names = ("norm1", "q_gain", "k_gain", "conv_w", "sg_norm", "w_s", "b_s", "norm2")
    parts = [dml, dml + dmc, dmc] + [flat(nm) for nm in names]
    sizes = [int(a.shape[0]) for a in parts]
    total = sum(sizes)
    padn = (-total) % (8 * LANES)
    sbuf = jnp.concatenate(parts + [jnp.zeros((padn,), F32)]).reshape(-1, LANES)
    g3 = _ag8(sbuf, "gather_small")
    ssum = _sum_lead(g3, "sum_devices").reshape(-1)
    offs = np.cumsum([0] + sizes)
    seg = {nm: ssum[offs[i + 3]:offs[i + 4]] for i, nm in enumerate(names)}
    gb_mod = ssum[offs[1]:offs[2]].reshape(L, N_CHIPS * n_mod)
    dmc_sum = ssum[offs[2]:offs[3]].reshape(L, N_CHIPS * n_mod)
    dml_all = g3.reshape(N_DEV, -1)[:, :sizes[0]].reshape(N_DEV, L, N_CHIPS * n_mod)
    dml_sh = jnp.swapaxes(lax.dynamic_slice_in_dim(dml_all, chip * n_mod, n_mod, axis=2), 0, 1)
    dmc_sh = lax.dynamic_slice_in_dim(dmc_sum, chip * n_mod, n_mod, axis=1)[:, None, :]
    dms = jnp.concatenate([dml_sh, dmc_sh, jnp.zeros((L, 7, n_mod), F32)], axis=1)
    g_wmod = _wmod_grad(a_raw, dms)
    part = _cctx_partial(jnp.concatenate([dmc_sh, jnp.zeros((L, 15, n_mod), F32)], axis=1), w_mod)
    g4 = _ag8(part.reshape(-1, LANES), "gather_cctx")
    g_cctx = _cctx_final(g4, c_ctx.reshape(8, LANES)).reshape(D)

    g_conv = lax.dynamic_slice_in_dim(seg["conv_w"].reshape(L, 3, N_CHIPS * n_cw), chip * n_cw, n_cw, axis=2)
    small_g = dict(c_ctx=g_cctx, w_mod=g_wmod, b_mod=gb_mod, norm1=seg["norm1"].reshape(norm1.shape),
                   q_gain=seg["q_gain"].reshape(q_gain.shape), k_gain=seg["k_gain"].reshape(k_gain.shape),
                   conv_w=g_conv, sg_norm=seg["sg_norm"].reshape(sg_norm.shape), w_s=seg["w_s"].reshape(w_s.shape),
                   b_s=seg["b_s"].reshape(b_s.shape), norm2=seg["norm2"].reshape(norm2.shape))
    res = {}
    for nm in _WEIGHTS:
        if nm in _BIG:
            res[nm] = [stored(nm, o) for o in out[nm]]
        else:
            res[nm] = _adamw(given[nm], given["m_" + nm], given["v_" + nm], small_g[nm])
    return (loss, grad_x, *[res[nm][0] for nm in _WEIGHTS], *[res[nm][1] for nm in _WEIGHTS],
            *[res[nm][2] for nm in _WEIGHTS], *[res[nm][3] for nm in _WEIGHTS])
```
